```python
import math
import numpy as np
import jax
import jax.numpy as jnp
from jax import lax

D_MODEL = 1024
BATCH = 8
SEQ = 16384
DEPTH = 1

ATTN_PATTERNS = ((128, 1), (512, 4), (2048, 16))
N_GROUPS_A = 3
HEADS_PER_GROUP = 4
N_HEADS_A = N_GROUPS_A * HEADS_PER_GROUP
HEAD_DIM_A = 128
D_ATTN = N_HEADS_A * HEAD_DIM_A
D_ATTN_OUT = HEADS_PER_GROUP * HEAD_DIM_A
BLOCK_A = 128
N_HEADS_B = 8
HEAD_DIM_K = 128
HEAD_DIM_V = 128
D_KEY_B = N_HEADS_B * HEAD_DIM_K
D_VAL_B = N_HEADS_B * HEAD_DIM_V
CONV_WIDTH = 4
CHUNK = 64
N_BRANCHES = 2
D_FF = ((8 * D_MODEL // 3 + 255) // 256) * 256
EPS = 1e-6
IN_SPLITS = (D_ATTN, D_ATTN, D_ATTN, D_KEY_B, D_KEY_B, D_VAL_B, D_VAL_B, N_HEADS_B, N_HEADS_B, N_BRANCHES * D_MODEL)
D_IN = 3 * D_ATTN + 2 * D_KEY_B + 2 * D_VAL_B + 2 * N_HEADS_B + N_BRANCHES * D_MODEL

kernel_name = 'hybrid_dilated_swa_gated_deltanet'


def rmsnorm(x, w):
    xf = x.astype(jnp.float32)
    xf = xf * lax.rsqrt(jnp.mean(xf * xf, axis=-1, keepdims=True) + EPS)
    return xf.astype(x.dtype) * w


def l2norm(t):
    return t * lax.rsqrt(jnp.sum(t * t, axis=-1, keepdims=True) + EPS)


def alibi_slopes():
    return jnp.exp2(-8.0 * jnp.arange(1, N_HEADS_A + 1, dtype=jnp.float32) / N_HEADS_A)


def dilated_window_attention(q, k, v, slopes, window, dilation):
    b, s, h, dh = q.shape
    f32 = jnp.float32
    w_sub = window // dilation
    span = BLOCK_A * dilation
    s_pad = -(-s // span) * span
    nb = s_pad // span

    def to_blocks(t):
        t = jnp.pad(t, ((0, 0), (0, s_pad - s), (0, 0), (0, 0)))
        return t.reshape(b, nb, BLOCK_A, dilation, h, dh)

    qb, kb, vb = to_blocks(q), to_blocks(k), to_blocks(v)

    def with_prev(t):
        prev = jnp.pad(t[:, :-1], ((0, 0), (1, 0), (0, 0), (0, 0), (0, 0), (0, 0)))
        return jnp.concatenate([prev, t], axis=2)

    kc, vc = with_prev(kb), with_prev(vb)
    scores = jnp.einsum('bnirhd,bnjrhd->bnrhij', qb, kc).astype(f32) * (dh ** -0.5)
    i = jnp.arange(BLOCK_A)[:, None]
    j = jnp.arange(2 * BLOCK_A)[None, :]
    delta = BLOCK_A + i - j
    band = (delta >= 0) & (delta <= w_sub)
    has_prev = (jnp.arange(nb)[:, None, None] > 0) | (j >= BLOCK_A)[None]
    valid = band[None] & has_prev
    bias = -(slopes.astype(f32) * dilation)[:, None, None] * delta.astype(f32)
    scores = jnp.where(valid[None, :, None, None], scores + bias, -jnp.inf)
    m = jnp.max(scores, axis=-1, keepdims=True)
    p = jnp.exp(scores - m)
    den = jnp.sum(p, axis=-1)
    o = jnp.einsum('bnrhij,bnjrhd->bnirhd', p, vc.astype(f32))
    o = o / jnp.transpose(den, (0, 1, 4, 2, 3))[..., None]
    lse = jnp.transpose(m[..., 0] + jnp.log(den), (0, 1, 4, 2, 3))
    o = o.reshape(b, s_pad, h, dh)[:, :s]
    lse = lse.reshape(b, s_pad, h)[:, :s]
    return o, lse


def causal_depthwise_conv(x, w):
    c = x.shape[-1]
    return lax.conv_general_dilated(x, w[:, None, :].astype(x.dtype), window_strides=(1,),
                                    padding=((CONV_WIDTH - 1, 0),),
                                    dimension_numbers=('NWC', 'WIO', 'NWC'),
                                    feature_group_count=c)


def gated_delta_rule(q, k, v, g, beta):
    b, s, h, dk = q.shape
    dv = v.shape[-1]
    n = s // CHUNK

    def chunks(t):
        return jnp.moveaxis(t.reshape(b, n, CHUNK, h, -1), 3, 1)

    q, k, v = chunks(q), chunks(k), chunks(v)
    g = jnp.moveaxis(g.reshape(b, n, CHUNK, h), 3, 1)
    beta = jnp.moveaxis(beta.reshape(b, n, CHUNK, h), 3, 1)
    gc = jnp.cumsum(g, axis=-1)
    incl = jnp.tril(jnp.ones((CHUNK, CHUNK), dtype=bool))
    strict = jnp.tril(jnp.ones((CHUNK, CHUNK), dtype=bool), -1)
    decay = jnp.exp(jnp.where(incl, gc[..., :, None] - gc[..., None, :], -jnp.inf))
    kk = jnp.einsum('bhnid,bhnjd->bhnij', k, k)
    a = jnp.where(strict, beta[..., :, None] * kk * decay, 0.0)
    eye = jnp.eye(CHUNK, dtype=a.dtype)
    rhs = jnp.concatenate([beta[..., None] * v, (beta * jnp.exp(gc))[..., None] * k], axis=-1)
    sol = lax.linalg.triangular_solve(eye + a, rhs, left_side=True, lower=True, unit_diagonal=True)
    u_bar, w = sol[..., :dv], sol[..., dv:]
    qk = jnp.einsum('bhnid,bhnjd->bhnij', q, k) * decay
    q_dec = q * jnp.exp(gc)[..., None]
    k_dec = k * jnp.exp(gc[..., -1:] - gc)[..., None]
    g_last = jnp.exp(gc[..., -1])

    def step(state, xs):
        u_bar_c, w_c, qk_c, q_c, k_c, gl_c = xs
        u = u_bar_c - jnp.einsum('bhck,bhkv->bhcv', w_c, state)
        o = jnp.einsum('bhck,bhkv->bhcv', q_c, state) + jnp.einsum('bhcj,bhjv->bhcv', qk_c, u)
        state = gl_c[..., None, None] * state + jnp.einsum('bhck,bhcv->bhkv', k_c, u)
        return state, o

    xs = tuple(jnp.moveaxis(t, 2, 0) for t in (u_bar, w, qk, q_dec, k_dec, g_last))
    state0 = jnp.zeros((b, h, dk, dv), jnp.float32)
    _, o = lax.scan(step, state0, xs)
    return jnp.transpose(o, (1, 0, 3, 2, 4)).reshape(b, s, h, dv)


def _fwd_setup_inputs(seed: int = 0) -> dict:
    key = jax.random.key(seed)
    ks = jax.random.split(key, 16)
    f32 = jnp.float32

    def nrm(k, shape, fan_in):
        return jax.random.normal(k, shape, f32) * (fan_in ** -0.5)

    def gain(k, shape):
        return 1.0 + 0.02 * jax.random.normal(k, shape, f32)

    x = jax.random.normal(ks[0], (BATCH, SEQ, D_MODEL), f32)
    a_log = jnp.log(jax.random.uniform(ks[4], (DEPTH, N_HEADS_B), f32, minval=1.0, maxval=16.0))
    dt = jnp.exp(jax.random.uniform(ks[5], (DEPTH, N_HEADS_B), f32, minval=math.log(1e-3), maxval=math.log(1e-1)))
    dt_bias = dt + jnp.log(-jnp.expm1(-dt))
    return {
        'x': x,
        'norm_mix': gain(ks[1], (DEPTH, D_MODEL)),
        'w_in': nrm(ks[2], (DEPTH, D_MODEL, D_IN), D_MODEL),
        'conv_w': nrm(ks[3], (DEPTH, CONV_WIDTH, 2 * D_KEY_B + D_VAL_B), CONV_WIDTH),
        'a_log': a_log,
        'dt_bias': dt_bias,
        'dn_norm': gain(ks[6], (DEPTH, HEAD_DIM_V)),
        'w_proj_attn': nrm(ks[7], (DEPTH, D_ATTN_OUT, D_MODEL), D_ATTN_OUT),
        'w_proj_delta': nrm(ks[8], (DEPTH, D_VAL_B, D_MODEL), D_VAL_B),
        'w_out': nrm(ks[9], (DEPTH, D_MODEL, D_MODEL), D_MODEL),
        'norm_ffn': gain(ks[10], (DEPTH, D_MODEL)),
        'w_gate': nrm(ks[11], (DEPTH, D_MODEL, D_FF), D_MODEL),
        'w_up': nrm(ks[12], (DEPTH, D_MODEL, D_FF), D_MODEL),
        'w_down': nrm(ks[13], (DEPTH, D_FF, D_MODEL), D_FF),
        'norm_final': gain(ks[14], (D_MODEL,)),
    }


def _fwd_reference(x, norm_mix, w_in, conv_w, a_log, dt_bias, dn_norm, w_proj_attn, w_proj_delta,
              w_out, norm_ffn, w_gate, w_up, w_down, norm_final):
    b, s, _ = x.shape
    f32 = jnp.float32
    slopes = alibi_slopes()
    split_at = np.cumsum(IN_SPLITS)[:-1].tolist()
    h = x
    for layer in range(DEPTH):
        u = rmsnorm(h, norm_mix[layer])
        proj = u @ w_in[layer]
        qa, ka, va, qd, kd, vd, z, beta_raw, a_raw, gate_raw = jnp.split(proj, split_at, axis=-1)

        qa = qa.reshape(b, s, N_HEADS_A, HEAD_DIM_A)
        ka = ka.reshape(b, s, N_HEADS_A, HEAD_DIM_A)
        va = va.reshape(b, s, N_HEADS_A, HEAD_DIM_A)
        outs, lses = [], []
        for gi, (window, dilation) in enumerate(ATTN_PATTERNS):
            hs = slice(gi * HEADS_PER_GROUP, (gi + 1) * HEADS_PER_GROUP)
            o, lse = dilated_window_attention(qa[:, :, hs], ka[:, :, hs], va[:, :, hs], slopes[hs], window, dilation)
            outs.append(o)
            lses.append(lse)
        mix_w = jax.nn.softmax(jnp.stack(lses), axis=0)
        y_a = jnp.einsum('gbsh,gbshd->bshd', mix_w, jnp.stack(outs)).reshape(b, s, D_ATTN_OUT).astype(x.dtype)
        y_a = y_a @ w_proj_attn[layer]

        qkv = jax.nn.silu(causal_depthwise_conv(jnp.concatenate([qd, kd, vd], axis=-1), conv_w[layer]))
        qd, kd, vd = jnp.split(qkv, [D_KEY_B, 2 * D_KEY_B], axis=-1)
        qd = l2norm(qd.reshape(b, s, N_HEADS_B, HEAD_DIM_K).astype(f32)) * (HEAD_DIM_K ** -0.5)
        kd = l2norm(kd.reshape(b, s, N_HEADS_B, HEAD_DIM_K).astype(f32))
        vd = vd.reshape(b, s, N_HEADS_B, HEAD_DIM_V).astype(f32)
        beta = jax.nn.sigmoid(beta_raw.astype(f32))
        g = -jnp.exp(a_log[layer].astype(f32)) * jax.nn.softplus(a_raw.astype(f32) + dt_bias[layer].astype(f32))
        o_d = gated_delta_rule(qd, kd, vd, g, beta)
        o_d = rmsnorm(o_d, dn_norm[layer].astype(f32)) * jax.nn.silu(z.reshape(b, s, N_HEADS_B, HEAD_DIM_V).astype(f32))
        y_b = o_d.reshape(b, s, D_VAL_B).astype(x.dtype) @ w_proj_delta[layer]

        gate_a, gate_b = jnp.split(jax.nn.sigmoid(gate_raw), N_BRANCHES, axis=-1)
        h = h + (gate_a * y_a + gate_b * y_b) @ w_out[layer]

        hn = rmsnorm(h, norm_ffn[layer])
        h = h + (jax.nn.silu(hn @ w_gate[layer]) * (hn @ w_up[layer])) @ w_down[layer]
    return rmsnorm(h, norm_final)


import jax as _jax
import jax.numpy as _jnp

TWIN_FORMAT = 'train_step'
FWD_PARAMS = ['x', 'norm_mix', 'w_in', 'conv_w', 'a_log', 'dt_bias', 'dn_norm', 'w_proj_attn', 'w_proj_delta', 'w_out', 'norm_ffn', 'w_gate', 'w_up', 'w_down', 'norm_final']
TWIN_WEIGHTS = ['norm_mix', 'w_in', 'conv_w', 'a_log', 'dt_bias', 'dn_norm', 'w_proj_attn', 'w_proj_delta', 'w_out', 'norm_ffn', 'w_gate', 'w_up', 'w_down', 'norm_final']
TWIN_DIFF_INPUT = 'x'
TWIN_INPUTS = ['x', 'norm_mix', 'w_in', 'conv_w', 'a_log', 'dt_bias', 'dn_norm', 'w_proj_attn', 'w_proj_delta', 'w_out', 'norm_ffn', 'w_gate', 'w_up', 'w_down', 'norm_final', 'loss_target', 'm_norm_mix', 'm_w_in', 'm_conv_w', 'm_a_log', 'm_dt_bias', 'm_dn_norm', 'm_w_proj_attn', 'm_w_proj_delta', 'm_w_out', 'm_norm_ffn', 'm_w_gate', 'm_w_up', 'm_w_down', 'm_norm_final', 'v_norm_mix', 'v_w_in', 'v_conv_w', 'v_a_log', 'v_dt_bias', 'v_dn_norm', 'v_w_proj_attn', 'v_w_proj_delta', 'v_w_out', 'v_norm_ffn', 'v_w_gate', 'v_w_up', 'v_w_down', 'v_norm_final']
TWIN_OUTPUTS = ['loss', 'grad_x', 'grad_norm_mix', 'grad_w_in', 'grad_conv_w', 'grad_a_log', 'grad_dt_bias', 'grad_dn_norm', 'grad_w_proj_attn', 'grad_w_proj_delta', 'grad_w_out', 'grad_norm_ffn', 'grad_w_gate', 'grad_w_up', 'grad_w_down', 'grad_norm_final', 'delta_norm_mix', 'delta_w_in', 'delta_conv_w', 'delta_a_log', 'delta_dt_bias', 'delta_dn_norm', 'delta_w_proj_attn', 'delta_w_proj_delta', 'delta_w_out', 'delta_norm_ffn', 'delta_w_gate', 'delta_w_up', 'delta_w_down', 'delta_norm_final', 'new_m_norm_mix', 'new_m_w_in', 'new_m_conv_w', 'new_m_a_log', 'new_m_dt_bias', 'new_m_dn_norm', 'new_m_w_proj_attn', 'new_m_w_proj_delta', 'new_m_w_out', 'new_m_norm_ffn', 'new_m_w_gate', 'new_m_w_up', 'new_m_w_down', 'new_m_norm_final', 'new_v_norm_mix', 'new_v_w_in', 'new_v_conv_w', 'new_v_a_log', 'new_v_dt_bias', 'new_v_dn_norm', 'new_v_w_proj_attn', 'new_v_w_proj_delta', 'new_v_w_out', 'new_v_norm_ffn', 'new_v_w_gate', 'new_v_w_up', 'new_v_w_down', 'new_v_norm_final']
TWIN_LEAF_KINDS = {'loss': 'loss', 'grad_x': 'grad_x', 'grad_norm_mix': 'grad_w', 'grad_w_in': 'grad_w', 'grad_conv_w': 'grad_w', 'grad_a_log': 'grad_w', 'grad_dt_bias': 'grad_w', 'grad_dn_norm': 'grad_w', 'grad_w_proj_attn': 'grad_w', 'grad_w_proj_delta': 'grad_w', 'grad_w_out': 'grad_w', 'grad_norm_ffn': 'grad_w', 'grad_w_gate': 'grad_w', 'grad_w_up': 'grad_w', 'grad_w_down': 'grad_w', 'grad_norm_final': 'grad_w', 'delta_norm_mix': 'delta_w', 'delta_w_in': 'delta_w', 'delta_conv_w': 'delta_w', 'delta_a_log': 'delta_w', 'delta_dt_bias': 'delta_w', 'delta_dn_norm': 'delta_w', 'delta_w_proj_attn': 'delta_w', 'delta_w_proj_delta': 'delta_w', 'delta_w_out': 'delta_w', 'delta_norm_ffn': 'delta_w', 'delta_w_gate': 'delta_w', 'delta_w_up': 'delta_w', 'delta_w_down': 'delta_w', 'delta_norm_final': 'delta_w', 'new_m_norm_mix': 'new_m', 'new_m_w_in': 'new_m', 'new_m_conv_w': 'new_m', 'new_m_a_log': 'new_m', 'new_m_dt_bias': 'new_m', 'new_m_dn_norm': 'new_m', 'new_m_w_proj_attn': 'new_m', 'new_m_w_proj_delta': 'new_m', 'new_m_w_out': 'new_m', 'new_m_norm_ffn': 'new_m', 'new_m_w_gate': 'new_m', 'new_m_w_up': 'new_m', 'new_m_w_down': 'new_m', 'new_m_norm_final': 'new_m', 'new_v_norm_mix': 'new_v', 'new_v_w_in': 'new_v', 'new_v_conv_w': 'new_v', 'new_v_a_log': 'new_v', 'new_v_dt_bias': 'new_v', 'new_v_dn_norm': 'new_v', 'new_v_w_proj_attn': 'new_v', 'new_v_w_proj_delta': 'new_v', 'new_v_w_out': 'new_v', 'new_v_norm_ffn': 'new_v', 'new_v_w_gate': 'new_v', 'new_v_w_up': 'new_v', 'new_v_w_down': 'new_v', 'new_v_norm_final': 'new_v'}


def _forward(args):
    return _fwd_reference(*[args[k] for k in FWD_PARAMS])


def _output_shape():
    def fwd():
        inp = _fwd_setup_inputs(0)
        return _fwd_reference(*[inp[k] for k in FWD_PARAMS])
    out = _jax.eval_shape(fwd)
    return out.shape, out.dtype

N_MICROBATCH = 1
ADAM_LR = 0.001
ADAM_B1 = 0.9
ADAM_B2 = 0.999
ADAM_EPS = 1e-08
ADAM_WD = 0.01
ADAM_STEP = 10
PER_EXAMPLE_BATCH_AXIS = {'x': 0, 'loss_target': 0}
SHARED_INPUTS = []
_WEIGHT_DTYPES = {'norm_mix': _jnp.float32, 'w_in': _jnp.float32, 'conv_w': _jnp.float32, 'a_log': _jnp.float32, 'dt_bias': _jnp.float32, 'dn_norm': _jnp.float32, 'w_proj_attn': _jnp.float32, 'w_proj_delta': _jnp.float32, 'w_out': _jnp.float32, 'norm_ffn': _jnp.float32, 'w_gate': _jnp.float32, 'w_up': _jnp.float32, 'w_down': _jnp.float32, 'norm_final': _jnp.float32}
MOMENT_SCALE = {'norm_mix': 2.310900e-01, 'w_in': 7.060504e-02, 'conv_w': 9.100566e-02, 'a_log': 5.290149e-01, 'dt_bias': 4.295242e-01, 'dn_norm': 4.036913e-01, 'w_proj_attn': 6.626692e-02, 'w_proj_delta': 1.196795e-01, 'w_out': 1.339639e-01, 'norm_ffn': 2.527557e-01, 'w_gate': 1.087334e-01, 'w_up': 1.053633e-01, 'w_down': 1.753620e-01, 'norm_final': 1.280070e+02}


def _to_microbatches(a, axis):
    t = _jnp.moveaxis(a, axis, 0)
    t = t.reshape((N_MICROBATCH, t.shape[0] // N_MICROBATCH) + t.shape[1:])
    return _jnp.moveaxis(t, 1, axis + 1)


def setup_inputs(seed: int = 0) -> dict:
    inp = _fwd_setup_inputs(seed)
    key = _jax.random.fold_in(_jax.random.key(seed), 7919)
    shape, _ = _output_shape()
    out = dict(inp)
    out["loss_target"] = _jax.random.normal(_jax.random.fold_in(key, 0), shape, _jnp.float32)
    for i, name in enumerate(TWIN_WEIGHTS):
        w = inp[name].astype(_jnp.float32)
        if MOMENT_SCALE is None:
            s = _jnp.sqrt(_jnp.mean(_jnp.square(w)) + 1e-30)
        else:
            s = MOMENT_SCALE[name]
        km, kv = _jax.random.split(_jax.random.fold_in(key, i + 1))
        out[name] = w
        out["m_" + name] = s * _jax.random.normal(km, w.shape, _jnp.float32)
        out["v_" + name] = (s * s) * _jax.random.uniform(kv, w.shape, _jnp.float32, 0.5, 1.5)
    if N_MICROBATCH > 1:
        for name, axis in PER_EXAMPLE_BATCH_AXIS.items():
            out[name] = _to_microbatches(out[name], axis)
    return {'x': out['x'], 'norm_mix': out['norm_mix'], 'w_in': out['w_in'], 'conv_w': out['conv_w'], 'a_log': out['a_log'], 'dt_bias': out['dt_bias'], 'dn_norm': out['dn_norm'], 'w_proj_attn': out['w_proj_attn'], 'w_proj_delta': out['w_proj_delta'], 'w_out': out['w_out'], 'norm_ffn': out['norm_ffn'], 'w_gate': out['w_gate'], 'w_up': out['w_up'], 'w_down': out['w_down'], 'norm_final': out['norm_final'], 'loss_target': out['loss_target'], 'm_norm_mix': out['m_norm_mix'], 'm_w_in': out['m_w_in'], 'm_conv_w': out['m_conv_w'], 'm_a_log': out['m_a_log'], 'm_dt_bias': out['m_dt_bias'], 'm_dn_norm': out['m_dn_norm'], 'm_w_proj_attn': out['m_w_proj_attn'], 'm_w_proj_delta': out['m_w_proj_delta'], 'm_w_out': out['m_w_out'], 'm_norm_ffn': out['m_norm_ffn'], 'm_w_gate': out['m_w_gate'], 'm_w_up': out['m_w_up'], 'm_w_down': out['m_w_down'], 'm_norm_final': out['m_norm_final'], 'v_norm_mix': out['v_norm_mix'], 'v_w_in': out['v_w_in'], 'v_conv_w': out['v_conv_w'], 'v_a_log': out['v_a_log'], 'v_dt_bias': out['v_dt_bias'], 'v_dn_norm': out['v_dn_norm'], 'v_w_proj_attn': out['v_w_proj_attn'], 'v_w_proj_delta': out['v_w_proj_delta'], 'v_w_out': out['v_w_out'], 'v_norm_ffn': out['v_norm_ffn'], 'v_w_gate': out['v_w_gate'], 'v_w_up': out['v_w_up'], 'v_w_down': out['v_w_down'], 'v_norm_final': out['v_norm_final']}


def _loss(weights, diff, rest, loss_target):
    with _jax.named_scope("forward"):
        args = {**rest, TWIN_DIFF_INPUT: diff, **{k: w.astype(_WEIGHT_DTYPES[k]) for k, w in weights.items()}}
        y = _forward(args)
    with _jax.named_scope("loss_head"):
        err = _jnp.square(y.astype(_jnp.float32) - loss_target)
        return 0.5 * _jnp.sum(_jnp.mean(err, axis=-1)) if err.ndim else 0.5 * err


def _adamw(w, g, m, v):
    m = ADAM_B1 * m + (1.0 - ADAM_B1) * g
    v = ADAM_B2 * v + (1.0 - ADAM_B2) * _jnp.square(g)
    m_hat = m / (1.0 - ADAM_B1 ** ADAM_STEP)
    v_hat = v / (1.0 - ADAM_B2 ** ADAM_STEP)
    delta = -ADAM_LR * (m_hat / (_jnp.sqrt(v_hat) + ADAM_EPS) + ADAM_WD * w)
    return delta, m, v


def reference(x, norm_mix, w_in, conv_w, a_log, dt_bias, dn_norm, w_proj_attn, w_proj_delta, w_out, norm_ffn, w_gate, w_up, w_down, norm_final, loss_target, m_norm_mix, m_w_in, m_conv_w, m_a_log, m_dt_bias, m_dn_norm, m_w_proj_attn, m_w_proj_delta, m_w_out, m_norm_ffn, m_w_gate, m_w_up, m_w_down, m_norm_final, v_norm_mix, v_w_in, v_conv_w, v_a_log, v_dt_bias, v_dn_norm, v_w_proj_attn, v_w_proj_delta, v_w_out, v_norm_ffn, v_w_gate, v_w_up, v_w_down, v_norm_final):
    given = dict(x=x, norm_mix=norm_mix, w_in=w_in, conv_w=conv_w, a_log=a_log, dt_bias=dt_bias, dn_norm=dn_norm, w_proj_attn=w_proj_attn, w_proj_delta=w_proj_delta, w_out=w_out, norm_ffn=norm_ffn, w_gate=w_gate, w_up=w_up, w_down=w_down, norm_final=norm_final, loss_target=loss_target, m_norm_mix=m_norm_mix, m_w_in=m_w_in, m_conv_w=m_conv_w, m_a_log=m_a_log, m_dt_bias=m_dt_bias, m_dn_norm=m_dn_norm, m_w_proj_attn=m_w_proj_attn, m_w_proj_delta=m_w_proj_delta, m_w_out=m_w_out, m_norm_ffn=m_norm_ffn, m_w_gate=m_w_gate, m_w_up=m_w_up, m_w_down=m_w_down, m_norm_final=m_norm_final, v_norm_mix=v_norm_mix, v_w_in=v_w_in, v_conv_w=v_conv_w, v_a_log=v_a_log, v_dt_bias=v_dt_bias, v_dn_norm=v_dn_norm, v_w_proj_attn=v_w_proj_attn, v_w_proj_delta=v_w_proj_delta, v_w_out=v_w_out, v_norm_ffn=v_norm_ffn, v_w_gate=v_w_gate, v_w_up=v_w_up, v_w_down=v_w_down, v_norm_final=v_norm_final)
    weights = {n: given[n] for n in TWIN_WEIGHTS}
    shared = {n: given[n] for n in SHARED_INPUTS}
    per_example = {n: given[n] for n in ['x']}
    grad_fn = _jax.value_and_grad(_loss, argnums=(0, 1))

    def one_microbatch(ex, loss_target):
        ex = dict(ex)
        diff = ex.pop(TWIN_DIFF_INPUT)
        return grad_fn(weights, diff, {**shared, **ex}, loss_target)

    if N_MICROBATCH == 1:
        loss, (grad_w, grad_x) = one_microbatch(per_example, given["loss_target"])
    else:
        def body(carry, xs):
            loss_sum, grad_sum = carry
            l_k, (gw_k, gx_k) = one_microbatch(xs[0], xs[1])
            with _jax.named_scope("update"):
                return (loss_sum + l_k, _jax.tree.map(_jnp.add, grad_sum, gw_k)), gx_k

        init = (_jnp.zeros((), _jnp.float32), _jax.tree.map(_jnp.zeros_like, weights))
        (loss, grad_w), grad_x = _jax.lax.scan(body, init, (per_example, given["loss_target"]))
    with _jax.named_scope("update"):
        delta_w, new_m, new_v = {}, {}, {}
        for n in TWIN_WEIGHTS:
            delta_w[n], new_m[n], new_v[n] = _adamw(weights[n], grad_w[n], given["m_" + n], given["v_" + n])
    return (loss, grad_x, *[grad_w[n] for n in TWIN_WEIGHTS], *[delta_w[n] for n in TWIN_WEIGHTS],
            *[new_m[n] for n in TWIN_WEIGHTS], *[new_v[n] for n in TWIN_WEIGHTS])
```

```python
import math

import jax
import jax.numpy as jnp
from jax import lax
from jax.experimental import pallas as pl
from jax.experimental.pallas import tpu as pltpu

F32 = jnp.float32
BF16 = jnp.bfloat16
HI = lax.Precision.HIGHEST

D_MODEL = 1024
N_DEV = 8
HEAD = 128
N_HEADS_A = 12
HEADS_PER_GROUP = 4
DILATIONS = (1, 4, 16)
BLOCK_A = 128
D_ATTN = N_HEADS_A * HEAD
D_ATTN_OUT = HEADS_PER_GROUP * HEAD
N_HEADS_B = 8
D_B = N_HEADS_B * HEAD
CONV_WIDTH = 4
CH = 64
D_FF = 2816
EPS = 1e-6
D_IN = 3 * D_ATTN + 4 * D_B + 2 * N_HEADS_B + 2 * D_MODEL
SHARD_IN = D_IN // N_DEV
PB_Z, PB_BETA, PB_A, PB_GATE = 3072, 4096, 5120, 6144
D_PA = 3 * D_ATTN
ADAM_LR, ADAM_B1, ADAM_B2, ADAM_EPS, ADAM_WD, ADAM_STEP = 0.001, 0.9, 0.999, 1e-08, 0.01, 10
VMEM_LIMIT = 56 * 1024 * 1024

NN = ((1,), (0,))
NT = ((1,), (1,))
TN = ((0,), (0,))


def _dot(a, b, dims=NN, prec=None):
    return lax.dot_general(a, b, (dims, ((), ())), precision=prec, preferred_element_type=F32)


def _bdot(a, b, dims=NN):
    return _dot(a.astype(BF16), b.astype(BF16), dims)


def _hdot(a, b, dims=NN):
    return _dot(a.astype(F32), b.astype(F32), dims, HI)


def _cp(*sem):
    return pltpu.CompilerParams(dimension_semantics=sem, vmem_limit_bytes=VMEM_LIMIT)


def _sigmoid(x):
    return 1.0 / (1.0 + jnp.exp(-x))


def _softplus(x):
    return jnp.maximum(x, 0.0) + jnp.log(1.0 + jnp.exp(-jnp.abs(x)))


def _rowsum(x):
    return jnp.sum(x, axis=-1, keepdims=True)


def _matmul(a, b, mode, out_dtype, tm, tn, tk, add=None, name="mm"):
    if mode == "nn":
        (m, k), (k2, n) = a.shape, b.shape
    elif mode == "nt":
        (m, k), (n, k2) = a.shape, b.shape
    else:
        (k, m), (k2, n) = a.shape, b.shape
    assert k == k2, (a.shape, b.shape, mode)
    tm, tn, tk = min(tm, m), min(tn, n), min(tk, k)
    assert m % tm == 0 and n % tn == 0 and k % tk == 0, (a.shape, b.shape, tm, tn, tk)
    nk = k // tk
    dims = {"nn": NN, "nt": NT, "tn": TN}[mode]

    def body(*refs):
        if add is None:
            a_ref, b_ref, o_ref, acc = refs
            add_ref = None
        else:
            a_ref, b_ref, add_ref, o_ref, acc = refs
        kk = pl.program_id(2)

        @pl.when(kk == 0)
        def _():
            acc[...] = jnp.zeros_like(acc)

        acc[...] += _bdot(a_ref[...], b_ref[...], dims)

        @pl.when(kk == nk - 1)
        def _():
            r = acc[...]
            if add_ref is not None:
                r = r + add_ref[...].astype(F32)
            o_ref[...] = r.astype(out_dtype)

    a_spec = (pl.BlockSpec((tk, tm), lambda i, j, kk: (kk, i)) if mode == "tn"
              else pl.BlockSpec((tm, tk), lambda i, j, kk: (i, kk)))
    b_spec = (pl.BlockSpec((tn, tk), lambda i, j, kk: (j, kk)) if mode == "nt"
              else pl.BlockSpec((tk, tn), lambda i, j, kk: (kk, j)))
    in_specs = [a_spec, b_spec]
    args = [a, b]
    if add is not None:
        in_specs.append(pl.BlockSpec((tm, tn), lambda i, j, kk: (i, j)))
        args.append(add)
    return pl.pallas_call(
        body, name=name, grid=(m // tm, n // tn, nk),
        in_specs=in_specs, out_specs=pl.BlockSpec((tm, tn), lambda i, j, kk: (i, j)),
        out_shape=jax.ShapeDtypeStruct((m, n), out_dtype),
        scratch_shapes=[pltpu.VMEM((tm, tn), F32)],
        compiler_params=_cp("parallel", "parallel", "arbitrary"),
    )(*args)


def _row_spec(tm, cols, cb=0):
    return pl.BlockSpec((tm, cols), lambda i, cb=cb: (i, cb))


def _bcast_spec(rows, cols):
    return pl.BlockSpec((rows, cols), lambda i: (0, 0))


def _rms_fwd(x, w, tm=512):
    t, d = x.shape

    def body(x_ref, w_ref, o_ref):
        xv = x_ref[...]
        r = lax.rsqrt(jnp.mean(xv * xv, axis=-1, keepdims=True) + EPS)
        o_ref[...] = (xv * r * w_ref[...]).astype(BF16)

    return pl.pallas_call(
        body, name="rms_fwd", grid=(t // tm,),
        in_specs=[_row_spec(tm, d), _bcast_spec(1, d)], out_specs=_row_spec(tm, d),
        out_shape=jax.ShapeDtypeStruct((t, d), BF16), compiler_params=_cp("parallel"),
    )(x, w)


def _rms_bwd(x, w, dy, resid, tm=512):
    t, d = x.shape

    def body(x_ref, w_ref, dy_ref, res_ref, dx_ref, dw_ref):
        xv = x_ref[...]
        r = lax.rsqrt(jnp.mean(xv * xv, axis=-1, keepdims=True) + EPS)
        xh = xv * r
        dyv = dy_ref[...].astype(F32)
        dxh = dyv * w_ref[...]
        dx_ref[...] = res_ref[...] + r * (dxh - xh * jnp.mean(dxh * xh, axis=-1, keepdims=True))

        @pl.when(pl.program_id(0) == 0)
        def _():
            dw_ref[...] = jnp.zeros_like(dw_ref)

        dw_ref[...] += jnp.sum(dyv * xh, axis=0, keepdims=True)

    return pl.pallas_call(
        body, name="rms_bwd", grid=(t // tm,),
        in_specs=[_row_spec(tm, d), _bcast_spec(1, d), _row_spec(tm, d), _row_spec(tm, d)],
        out_specs=[_row_spec(tm, d), _bcast_spec(1, d)],
        out_shape=[jax.ShapeDtypeStruct((t, d), F32), jax.ShapeDtypeStruct((1, d), F32)],
        compiler_params=_cp("arbitrary"),
    )(x, w, dy, resid)


def _final_loss(h, w, target, tm=512):
    t, d = h.shape

    def body(h_ref, w_ref, t_ref, loss_ref, dh_ref, dw_ref):
        hv = h_ref[...]
        r = lax.rsqrt(jnp.mean(hv * hv, axis=-1, keepdims=True) + EPS)
        xh = hv * r
        wv = w_ref[...]
        err = xh * wv - t_ref[...]
        dy = err * (1.0 / d)
        dxh = dy * wv
        dh_ref[...] = r * (dxh - xh * jnp.mean(dxh * xh, axis=-1, keepdims=True))

        @pl.when(pl.program_id(0) == 0)
        def _():
            dw_ref[...] = jnp.zeros_like(dw_ref)
            loss_ref[...] = jnp.zeros_like(loss_ref)

        dw_ref[...] += jnp.sum(dy * xh, axis=0, keepdims=True)
        part = 0.5 * jnp.sum(jnp.mean(err * err, axis=-1, keepdims=True), axis=0, keepdims=True)
        loss_ref[...] += part + jnp.zeros((1, HEAD), F32)

    return pl.pallas_call(
        body, name="final_loss", grid=(t // tm,),
        in_specs=[_row_spec(tm, d), _bcast_spec(1, d), _row_spec(tm, d)],
        out_specs=[_bcast_spec(1, HEAD), _row_spec(tm, d), _bcast_spec(1, d)],
        out_shape=[jax.ShapeDtypeStruct((1, HEAD), F32), jax.ShapeDtypeStruct((t, d), F32),
                   jax.ShapeDtypeStruct((1, d), F32)],
        compiler_params=_cp("arbitrary"),
    )(h, w, target)


def _swiglu_fwd(gu, tm=256):
    t = gu.shape[0]
    ff = gu.shape[1] // 2

    def body(gu_ref, o_ref):
        g = gu_ref[:, :ff].astype(F32)
        o_ref[...] = (g * _sigmoid(g) * gu_ref[:, ff:].astype(F32)).astype(BF16)

    return pl.pallas_call(
        body, name="swiglu_fwd", grid=(t // tm,),
        in_specs=[_row_spec(tm, 2 * ff)], out_specs=_row_spec(tm, ff),
        out_shape=jax.ShapeDtypeStruct((t, ff), BF16), compiler_params=_cp("parallel"),
    )(gu)


def _swiglu_bwd(gu, dact, tm=256):
    t = gu.shape[0]
    ff = gu.shape[1] // 2

    def body(gu_ref, d_ref, o_ref):
        g = gu_ref[:, :ff].astype(F32)
        u = gu_ref[:, ff:].astype(F32)
        dv = d_ref[...].astype(F32)
        sg = _sigmoid(g)
        o_ref[:, :ff] = (dv * u * (sg + g * sg * (1.0 - sg))).astype(BF16)
        o_ref[:, ff:] = (dv * g * sg).astype(BF16)

    return pl.pallas_call(
        body, name="swiglu_bwd", grid=(t // tm,),
        in_specs=[_row_spec(tm, 2 * ff), _row_spec(tm, ff)], out_specs=_row_spec(tm, 2 * ff),
        out_shape=jax.ShapeDtypeStruct((t, 2 * ff), BF16), compiler_params=_cp("parallel"),
    )(gu, dact)


def _gate_merge_fwd(pb, ya, yb, tm=512):
    t, d = ya.shape
    cb = PB_GATE // d

    def body(ga_ref, gb_ref, ya_ref, yb_ref, o_ref):
        o_ref[...] = (_sigmoid(ga_ref[...]) * ya_ref[...] + _sigmoid(gb_ref[...]) * yb_ref[...]).astype(BF16)

    return pl.pallas_call(
        body, name="gate_merge_fwd", grid=(t // tm,),
        in_specs=[_row_spec(tm, d, cb), _row_spec(tm, d, cb + 1), _row_spec(tm, d), _row_spec(tm, d)],
        out_specs=_row_spec(tm, d),
        out_shape=jax.ShapeDtypeStruct((t, d), BF16), compiler_params=_cp("parallel"),
    )(pb, pb, ya, yb)


def _gate_merge_bwd(pb, ya, yb, dm, tm=512):
    t, d = ya.shape
    cb = PB_GATE // d

    def body(ga_ref, gb_ref, ya_ref, yb_ref, dm_ref, dya_ref, dyb_ref, dga_ref, dgb_ref):
        dmv = dm_ref[...].astype(F32)
        sa = _sigmoid(ga_ref[...])
        sb = _sigmoid(gb_ref[...])
        dya_ref[...] = (dmv * sa).astype(BF16)
        dyb_ref[...] = (dmv * sb).astype(BF16)
        dga_ref[...] = (dmv * ya_ref[...] * sa * (1.0 - sa)).astype(BF16)
        dgb_ref[...] = (dmv * yb_ref[...] * sb * (1.0 - sb)).astype(BF16)

    return pl.pallas_call(
        body, name="gate_merge_bwd", grid=(t // tm,),
        in_specs=[_row_spec(tm, d, cb), _row_spec(tm, d, cb + 1), _row_spec(tm, d), _row_spec(tm, d),
                  _row_spec(tm, d)],
        out_specs=[_row_spec(tm, d)] * 4,
        out_shape=[jax.ShapeDtypeStruct((t, d), BF16)] * 4, compiler_params=_cp("parallel"),
    )(pb, pb, ya, yb, dm)


def _head_norm_fwd(o, pb, wn, tm=512):
    t, d = o.shape
    nh = d // HEAD

    def body(o_ref, z_ref, w_ref, out_ref):
        wv = w_ref[...]
        for h in range(nh):
            sl = slice(h * HEAD, (h + 1) * HEAD)
            ov = o_ref[:, sl]
            zv = z_ref[:, sl]
            r = lax.rsqrt(jnp.mean(ov * ov, axis=-1, keepdims=True) + EPS)
            out_ref[:, sl] = (ov * r * wv * (zv * _sigmoid(zv))).astype(BF16)

    return pl.pallas_call(
        body, name="head_norm_fwd", grid=(t // tm,),
        in_specs=[_row_spec(tm, d), _row_spec(tm, d, PB_Z // d), _bcast_spec(1, HEAD)],
        out_specs=_row_spec(tm, d),
        out_shape=jax.ShapeDtypeStruct((t, d), BF16), compiler_params=_cp("parallel"),
    )(o, pb, wn)


def _head_norm_bwd(o, pb, wn, dout, tm=512):
    t, d = o.shape
    nh = d // HEAD

    def body(o_ref, z_ref, w_ref, d_ref, do_ref, dz_ref, dw_ref):
        wv = w_ref[...]
        dw_acc = jnp.zeros((1, HEAD), F32)
        for h in range(nh):
            sl = slice(h * HEAD, (h + 1) * HEAD)
            ov = o_ref[:, sl]
            zv = z_ref[:, sl]
            dv = d_ref[:, sl].astype(F32)
            r = lax.rsqrt(jnp.mean(ov * ov, axis=-1, keepdims=True) + EPS)
            xh = ov * r
            sz = _sigmoid(zv)
            dn = dv * (zv * sz)
            dz_ref[:, sl] = (dv * xh * wv * (sz + zv * sz * (1.0 - sz))).astype(BF16)
            dxh = dn * wv
            do_ref[:, sl] = r * (dxh - xh * jnp.mean(dxh * xh, axis=-1, keepdims=True))
            dw_acc = dw_acc + jnp.sum(dn * xh, axis=0, keepdims=True)

        @pl.when(pl.program_id(0) == 0)
        def _():
            dw_ref[...] = jnp.zeros_like(dw_ref)

        dw_ref[...] += dw_acc

    return pl.pallas_call(
        body, name="head_norm_bwd", grid=(t // tm,),
        in_specs=[_row_spec(tm, d), _row_spec(tm, d, PB_Z // d), _bcast_spec(1, HEAD), _row_spec(tm, d)],
        out_specs=[_row_spec(tm, d), _row_spec(tm, d), _bcast_spec(1, HEAD)],
        out_shape=[jax.ShapeDtypeStruct((t, d), F32), jax.ShapeDtypeStruct((t, d), BF16),
                   jax.ShapeDtypeStruct((1, HEAD), F32)],
        compiler_params=_cp("arbitrary"),
    )(o, pb, wn, dout)


def _attn_bias(gi, hh, dil):
    i = lax.broadcasted_iota(jnp.int32, (BLOCK_A, BLOCK_A), 0)
    j = lax.broadcasted_iota(jnp.int32, (BLOCK_A, BLOCK_A), 1)
    hf = (gi * HEADS_PER_GROUP + hh + 1).astype(F32)
    slope = jnp.exp(jnp.full((1, BLOCK_A), -8.0 * math.log(2.0) / N_HEADS_A, F32) * hf) * float(dil)
    d_prev = (BLOCK_A + i - j).astype(F32)
    d_cur = (i - j).astype(F32)
    return -slope * d_prev, -slope * d_cur, j >= i, j <= i


def _attn_views(t, dil, tq):
    rows = t // dil
    tq = min(tq, rows)
    nq = tq // BLOCK_A
    return rows, tq, nq, rows // tq


def _attn_fwd_group(pa, gi, dil, tq=512):
    t = pa.shape[0]
    rows, tq, nq, nsteps = _attn_views(t, dil, tq)
    pav = pa.reshape(rows, dil * D_PA)
    cpt = D_PA // HEAD
    scale = HEAD ** -0.5

    def body(q_ref, k_ref, v_ref, kp_ref, vp_ref, o_ref, l_ref):
        hh = pl.program_id(0) % HEADS_PER_GROUP
        step = pl.program_id(1)
        b_prev, b_cur, m_prev, m_cur = _attn_bias(gi, hh, dil)
        for a in range(nq):
            sl = slice(a * BLOCK_A, (a + 1) * BLOCK_A)
            q = q_ref[sl, :]
            if a == 0:
                kp, vp = kp_ref[...], vp_ref[...]
                mp = jnp.logical_and(m_prev, step > 0)
            else:
                psl = slice((a - 1) * BLOCK_A, a * BLOCK_A)
                kp, vp = k_ref[psl, :], v_ref[psl, :]
                mp = m_prev
            s_p = jnp.where(mp, _bdot(q, kp, NT) * scale + b_prev, -1e30)
            s_c = jnp.where(m_cur, _bdot(q, k_ref[sl, :], NT) * scale + b_cur, -1e30)
            m = jnp.maximum(jnp.max(s_p, axis=-1, keepdims=True), jnp.max(s_c, axis=-1, keepdims=True))
            p_p = jnp.exp(s_p - m)
            p_c = jnp.exp(s_c - m)
            den = _rowsum(p_p) + _rowsum(p_c)
            o_ref[sl, :] = (_bdot(p_p, vp) + _bdot(p_c, v_ref[sl, :])) / den
            l_ref[sl, :] = (m + jnp.log(den)) + jnp.zeros((BLOCK_A, HEAD), F32)

    def col(base):
        return lambda cb, s: (s, (cb // HEADS_PER_GROUP) * cpt + base + cb % HEADS_PER_GROUP)

    def col_prev(base):
        return lambda cb, s: (jnp.maximum(s * nq - 1, 0), (cb // HEADS_PER_GROUP) * cpt + base + cb % HEADS_PER_GROUP)

    qb, kb, vb = gi * HEADS_PER_GROUP, N_HEADS_A + gi * HEADS_PER_GROUP, 2 * N_HEADS_A + gi * HEADS_PER_GROUP
    ospec = pl.BlockSpec((tq, HEAD), lambda cb, s: (s, cb))
    o, l = pl.pallas_call(
        body, name=f"attn_fwd_g{gi}", grid=(dil * HEADS_PER_GROUP, nsteps),
        in_specs=[pl.BlockSpec((tq, HEAD), col(qb)), pl.BlockSpec((tq, HEAD), col(kb)),
                  pl.BlockSpec((tq, HEAD), col(vb)),
                  pl.BlockSpec((BLOCK_A, HEAD), col_prev(kb)), pl.BlockSpec((BLOCK_A, HEAD), col_prev(vb))],
        out_specs=[ospec, ospec],
        out_shape=[jax.ShapeDtypeStruct((rows, dil * D_ATTN_OUT), F32)] * 2,
        compiler_params=_cp("parallel", "parallel"),
    )(pav, pav, pav, pav, pav)
    return o.reshape(t, D_ATTN_OUT), l.reshape(t, D_ATTN_OUT)


def _attn_merge(os, ls, tm=512):
    t, d = os[0].shape

    def body(o0, o1, o2, l0, l1, l2, y_ref, lse_ref):
        a0, a1, a2 = l0[...], l1[...], l2[...]
        m = jnp.maximum(jnp.maximum(a0, a1), a2)
        e0, e1, e2 = jnp.exp(a0 - m), jnp.exp(a1 - m), jnp.exp(a2 - m)
        den = e0 + e1 + e2
        y_ref[...] = ((e0 * o0[...] + e1 * o1[...] + e2 * o2[...]) / den).astype(BF16)
        lse_ref[...] = m + jnp.log(den)

    return pl.pallas_call(
        body, name="attn_merge", grid=(t // tm,),
        in_specs=[_row_spec(tm, d)] * 6, out_specs=[_row_spec(tm, d)] * 2,
        out_shape=[jax.ShapeDtypeStruct((t, d), BF16), jax.ShapeDtypeStruct((t, d), F32)],
        compiler_params=_cp("parallel"),
    )(*os, *ls)


def _attn_bwd_group(pa, dy, y, lse, gi, dil, tq=512):
    t = pa.shape[0]
    rows, tq, nq, nsteps = _attn_views(t, dil, tq)
    pav = pa.reshape(rows, dil * D_PA)
    dyv = dy.reshape(rows, dil * D_ATTN_OUT)
    yv = y.reshape(rows, dil * D_ATTN_OUT)
    lv = lse.reshape(rows, dil * D_ATTN_OUT)
    cpt = D_PA // HEAD
    opt = D_ATTN_OUT // HEAD
    scale = HEAD ** -0.5
    nblk = rows // BLOCK_A

    def body(q_ref, k_ref, v_ref, dy_ref, y_ref, l_ref, kp_ref, vp_ref, qn_ref, dyn_ref, yn_ref, ln_ref,
             dq_ref, dk_ref, dv_ref):
        hh = pl.program_id(0) % HEADS_PER_GROUP
        step = pl.program_id(1)
        b_prev, b_cur, m_prev, m_cur = _attn_bias(gi, hh, dil)
        dk_acc = [jnp.zeros((BLOCK_A, HEAD), F32) for _ in range(nq)]
        dv_acc = [jnp.zeros((BLOCK_A, HEAD), F32) for _ in range(nq)]
        for a in range(nq + 1):
            if a < nq:
                sl = slice(a * BLOCK_A, (a + 1) * BLOCK_A)
                q, dyb, yb, lb = q_ref[sl, :], dy_ref[sl, :], y_ref[sl, :], l_ref[sl, :]
            else:
                q, dyb, yb, lb = qn_ref[...], dyn_ref[...], yn_ref[...], ln_ref[...]
            delta = _rowsum(dyb.astype(F32) * yb.astype(F32))
            dq = jnp.zeros((BLOCK_A, HEAD), F32)
            if a == 0:
                kp, vp = kp_ref[...], vp_ref[...]
                mp = jnp.logical_and(m_prev, step > 0)
            else:
                psl = slice((a - 1) * BLOCK_A, a * BLOCK_A)
                kp, vp = k_ref[psl, :], v_ref[psl, :]
                mp = jnp.logical_and(m_prev, step < nsteps - 1) if a == nq else m_prev
            s = _bdot(q, kp, NT) * scale + b_prev
            p = jnp.where(mp, jnp.exp(jnp.where(mp, s - lb, 0.0)), 0.0)
            ds = p * (_bdot(dyb, vp, NT) - delta)
            if a < nq:
                dq = dq + _bdot(ds, kp)
            if a > 0:
                dk_acc[a - 1] = dk_acc[a - 1] + _bdot(ds, q, TN)
                dv_acc[a - 1] = dv_acc[a - 1] + _bdot(p, dyb, TN)
            if a < nq:
                kc, vc = k_ref[sl, :], v_ref[sl, :]
                s = _bdot(q, kc, NT) * scale + b_cur
                p = jnp.where(m_cur, jnp.exp(jnp.where(m_cur, s - lb, 0.0)), 0.0)
                ds = p * (_bdot(dyb, vc, NT) - delta)
                dq = dq + _bdot(ds, kc)
                dk_acc[a] = dk_acc[a] + _bdot(ds, q, TN)
                dv_acc[a] = dv_acc[a] + _bdot(p, dyb, TN)
                dq_ref[sl, :] = (dq * scale).astype(BF16)
        for a in range(nq):
            sl = slice(a * BLOCK_A, (a + 1) * BLOCK_A)
            dk_ref[sl, :] = (dk_acc[a] * scale).astype(BF16)
            dv_ref[sl, :] = dv_acc[a].astype(BF16)

    def col(base, per):
        return lambda cb, s: (s, (cb // HEADS_PER_GROUP) * per + base + cb % HEADS_PER_GROUP)

    def col_prev(base, per):
        return lambda cb, s: (jnp.maximum(s * nq - 1, 0), (cb // HEADS_PER_GROUP) * per + base + cb % HEADS_PER_GROUP)

    def col_next(base, per):
        return lambda cb, s: (jnp.minimum((s + 1) * nq, nblk - 1),
                              (cb // HEADS_PER_GROUP) * per + base + cb % HEADS_PER_GROUP)

    qb, kb, vb = gi * HEADS_PER_GROUP, N_HEADS_A + gi * HEADS_PER_GROUP, 2 * N_HEADS_A + gi * HEADS_PER_GROUP
    big, small = (tq, HEAD), (BLOCK_A, HEAD)
    ospec = pl.BlockSpec(big, lambda cb, s: (s, cb))
    dq, dk, dv = pl.pallas_call(
        body, name=f"attn_bwd_g{gi}", grid=(dil * HEADS_PER_GROUP, nsteps),
        in_specs=[pl.BlockSpec(big, col(qb, cpt)), pl.BlockSpec(big, col(kb, cpt)), pl.BlockSpec(big, col(vb, cpt)),
                  pl.BlockSpec(big, col(0, opt)), pl.BlockSpec(big, col(0, opt)), pl.BlockSpec(big, col(0, opt)),
                  pl.BlockSpec(small, col_prev(kb, cpt)), pl.BlockSpec(small, col_prev(vb, cpt)),
                  pl.BlockSpec(small, col_next(qb, cpt)), pl.BlockSpec(small, col_next(0, opt)),
                  pl.BlockSpec(small, col_next(0, opt)), pl.BlockSpec(small, col_next(0, opt))],
        out_specs=[ospec] * 3,
        out_shape=[jax.ShapeDtypeStruct((rows, dil * D_ATTN_OUT), BF16)] * 3,
        compiler_params=_cp("parallel", "parallel"),
    )(pav, pav, pav, dyv, yv, lv, pav, pav, pav, dyv, yv, lv)
    return tuple(z.reshape(t, D_ATTN_OUT) for z in (dq, dk, dv))


def _shift_down(cur, prev8, s):
    if s == 0:
        return cur
    rolled = pltpu.roll(cur, s, 0)
    prolled = pltpu.roll(prev8, s, 0)
    rid = lax.broadcasted_iota(jnp.int32, prev8.shape, 0)
    top = jnp.where(rid < s, prolled, rolled[:8])
    return jnp.concatenate([top, rolled[8:]], axis=0)


def _shift_up(cur, next8, s):
    if s == 0:
        return cur
    n = cur.shape[0]
    rolled = pltpu.roll(cur, n - s, 0)
    nrolled = pltpu.roll(next8, 8 - s, 0)
    rid = lax.broadcasted_iota(jnp.int32, next8.shape, 0)
    bottom = jnp.where(rid >= 8 - s, nrolled, rolled[n - 8:])
    return jnp.concatenate([rolled[:n - 8], bottom], axis=0)


def _conv(xv, prev8, wv):
    c = jnp.zeros_like(xv)
    shifted = []
    for s in range(CONV_WIDTH):
        xs = _shift_down(xv, prev8, s)
        shifted.append(xs)
        c = c + wv[CONV_WIDTH - 1 - s:CONV_WIDTH - s, :] * xs
    return c, shifted


def _dn_prep_fwd(pb, conv_w, a_log_bc, dt_bias_bc, tm=256):
    t = pb.shape[0]
    c3 = 3 * D_B
    r8 = tm // 8

    def body(x_ref, xp_ref, b_ref, a_ref, w_ref, al_ref, dt_ref, q_ref, k_ref, v_ref, g_ref, beta_ref):
        prev8 = jnp.where(pl.program_id(0) > 0, xp_ref[...], 0.0)
        c, _ = _conv(x_ref[...], prev8, w_ref[...])
        s = c * _sigmoid(c)
        for h in range(N_HEADS_B):
            sl = slice(h * HEAD, (h + 1) * HEAD)
            sq = s[:, h * HEAD:(h + 1) * HEAD]
            q_ref[:, sl] = sq * lax.rsqrt(_rowsum(sq * sq) + EPS) * (HEAD ** -0.5)
            sk = s[:, D_B + h * HEAD:D_B + (h + 1) * HEAD]
            k_ref[:, sl] = sk * lax.rsqrt(_rowsum(sk * sk) + EPS)
        v_ref[...] = s[:, 2 * D_B:]
        beta_ref[...] = _sigmoid(b_ref[...])
        g_ref[...] = -jnp.exp(al_ref[...]) * _softplus(a_ref[...] + dt_ref[...])

    return pl.pallas_call(
        body, name="dn_prep_fwd", grid=(t // tm,),
        in_specs=[_row_spec(tm, c3, 0),
                  pl.BlockSpec((8, c3), lambda i: (jnp.maximum(i * r8 - 1, 0), 0)),
                  _row_spec(tm, D_B, PB_BETA // D_B), _row_spec(tm, D_B, PB_A // D_B),
                  _bcast_spec(CONV_WIDTH, c3), _bcast_spec(1, D_B), _bcast_spec(1, D_B)],
        out_specs=[_row_spec(tm, D_B)] * 5,
        out_shape=[jax.ShapeDtypeStruct((t, D_B), F32)] * 5,
        compiler_params=_cp("parallel"),
    )(pb, pb, pb, pb, conv_w, a_log_bc, dt_bias_bc)


def _dn_prep_bwd(pb, conv_w, a_log_bc, dt_bias_bc, g, dq, dk, dv, dg, dbeta, tm=128):
    t = pb.shape[0]
    c3 = 3 * D_B
    r8 = tm // 8

    def body(x_ref, xp_ref, b_ref, a_ref, w_ref, al_ref, dt_ref, g_ref, dq_ref, dk_ref, dv_ref, dg_ref, db_ref,
             dc_ref, dbr_ref, dar_ref, dw_ref, dal_ref, ddt_ref):
        prev8 = jnp.where(pl.program_id(0) > 0, xp_ref[...], 0.0)
        c, shifted = _conv(x_ref[...], prev8, w_ref[...])
        sg = _sigmoid(c)
        s = c * sg
        dsilu = sg + c * sg * (1.0 - sg)
        for h in range(N_HEADS_B):
            sl = slice(h * HEAD, (h + 1) * HEAD)
            for base, d_ref, mult in ((0, dq_ref, HEAD ** -0.5), (D_B, dk_ref, 1.0)):
                ssl = slice(base + h * HEAD, base + (h + 1) * HEAD)
                sv = s[:, ssl]
                r = lax.rsqrt(_rowsum(sv * sv) + EPS)
                yh = sv * r
                dyv = d_ref[:, sl] * mult
                dc_ref[:, ssl] = r * (dyv - yh * _rowsum(dyv * yh)) * dsilu[:, ssl]
        dc_ref[:, 2 * D_B:] = dv_ref[...] * dsilu[:, 2 * D_B:]
        dcv = dc_ref[...]

        @pl.when(pl.program_id(0) == 0)
        def _():
            dw_ref[...] = jnp.zeros_like(dw_ref)
            dal_ref[...] = jnp.zeros_like(dal_ref)
            ddt_ref[...] = jnp.zeros_like(ddt_ref)

        for sft in range(CONV_WIDTH):
            j = CONV_WIDTH - 1 - sft
            dw_ref[j:j + 1, :] += jnp.sum(dcv * shifted[sft], axis=0, keepdims=True)
        beta = _sigmoid(b_ref[...])
        dbr_ref[...] = (db_ref[...] * beta * (1.0 - beta) * (1.0 / HEAD)).astype(BF16)
        dgv = dg_ref[...]
        da = dgv * (-jnp.exp(al_ref[...])) * _sigmoid(a_ref[...] + dt_ref[...])
        dar_ref[...] = (da * (1.0 / HEAD)).astype(BF16)
        dal_ref[...] += jnp.sum(dgv * g_ref[...], axis=0, keepdims=True)
        ddt_ref[...] += jnp.sum(da, axis=0, keepdims=True)

    row = _row_spec(tm, D_B)
    return pl.pallas_call(
        body, name="dn_prep_bwd", grid=(t // tm,),
        in_specs=[_row_spec(tm, c3, 0),
                  pl.BlockSpec((8, c3), lambda i: (jnp.maximum(i * r8 - 1, 0), 0)),
                  _row_spec(tm, D_B, PB_BETA // D_B), _row_spec(tm, D_B, PB_A // D_B),
                  _bcast_spec(CONV_WIDTH, c3), _bcast_spec(1, D_B), _bcast_spec(1, D_B),
                  row, row, row, row, row, row],
        out_specs=[_row_spec(tm, c3), row, row, _bcast_spec(CONV_WIDTH, c3), _bcast_spec(1, D_B),
                   _bcast_spec(1, D_B)],
        out_shape=[jax.ShapeDtypeStruct((t, c3), F32), jax.ShapeDtypeStruct((t, D_B), BF16),
                   jax.ShapeDtypeStruct((t, D_B), BF16), jax.ShapeDtypeStruct((CONV_WIDTH, c3), F32),
                   jax.ShapeDtypeStruct((1, D_B), F32), jax.ShapeDtypeStruct((1, D_B), F32)],
        compiler_params=_cp("arbitrary"),
    )(pb, pb, pb, pb, conv_w, a_log_bc, dt_bias_bc, g, dq, dk, dv, dg, dbeta)


def _conv_bwd_input(dc, conv_w, tm=256):
    t, c3 = dc.shape
    r8 = tm // 8
    nlast = t // 8 - 1
    nsteps = t // tm

    def body(d_ref, dn_ref, w_ref, o_ref):
        next8 = jnp.where(pl.program_id(0) < nsteps - 1, dn_ref[...], 0.0)
        dv = d_ref[...]
        wv = w_ref[...]
        acc = jnp.zeros_like(dv)
        for s in range(CONV_WIDTH):
            acc = acc + wv[CONV_WIDTH - 1 - s:CONV_WIDTH - s, :] * _shift_up(dv, next8, s)
        o_ref[...] = acc.astype(BF16)

    return pl.pallas_call(
        body, name="conv_bwd_input", grid=(nsteps,),
        in_specs=[_row_spec(tm, c3), pl.BlockSpec((8, c3), lambda i: (jnp.minimum((i + 1) * r8, nlast), 0)),
                  _bcast_spec(CONV_WIDTH, c3)],
        out_specs=_row_spec(tm, c3),
        out_shape=jax.ShapeDtypeStruct((t, c3), BF16), compiler_params=_cp("parallel"),
    )(dc, dc, conv_w)


def _lanes(x):
    return x[:, :CH]


def _tri_inv(a, r, c):
    eye = (r == c).astype(F32)
    b16 = (r >> 4) == (c >> 4)
    b32 = (r >> 5) == (c >> 5)
    n = jnp.where(b16, -a, 0.0)
    x = eye + n
    p = _hdot(n, n)
    x = x + _hdot(x, p)
    p = _hdot(p, p)
    x = x + _hdot(x, p)
    p = _hdot(p, p)
    x = x + _hdot(x, p)
    l32 = jnp.where(jnp.logical_and(b32, jnp.logical_not(b16)), a, 0.0)
    x = x - _hdot(_hdot(x, l32), x)
    l64 = jnp.where(b32, 0.0, a)
    x = x - _hdot(_hdot(x, l64), x)
    return x


def _chunk_local(q, k, v, g, beta):
    r = lax.broadcasted_iota(jnp.int32, (CH, CH), 0)
    c = lax.broadcasted_iota(jnp.int32, (CH, CH), 1)
    incl, strict = r >= c, r > c
    lm = incl.astype(F32)
    gcb = _hdot(lm, g)
    dlog = _hdot(lm, jnp.where(strict, _lanes(g), 0.0))
    decay = jnp.where(incl, jnp.exp(jnp.where(incl, dlog, 0.0)), 0.0)
    bcol = _lanes(beta)
    kk = _bdot(k, k, NT)
    a = jnp.where(strict, bcol * kk * decay, 0.0)
    tm = _tri_inv(a, r, c)
    eg = jnp.exp(gcb)
    u_bar = _hdot(tm, beta * v)
    w = _hdot(tm, beta * eg * k)
    qkraw = _bdot(q, k, NT)
    gl = gcb[CH - 1:CH, :]
    ek = jnp.exp(gl - gcb)
    return dict(incl=incl, strict=strict, r=r, c=c, decay=decay, bcol=bcol, kk=kk, tm=tm, eg=eg,
                u_bar=u_bar, w=w, qkraw=qkraw, gl=gl, ek=ek)


def _dn_chunk_fwd(q, k, v, g, beta, cps=8):
    t = q.shape[0]
    tm = cps * CH

    def body(q_ref, k_ref, v_ref, g_ref, b_ref, ub_ref, w_ref, qd_ref, kd_ref, qk_ref, gl_ref):
        def chunk(ci, carry):
            sl = pl.ds(pl.multiple_of(ci * CH, CH), CH)
            qv, kv = q_ref[sl, :], k_ref[sl, :]
            loc = _chunk_local(qv, kv, v_ref[sl, :], g_ref[sl, :], b_ref[sl, :])
            ub_ref[sl, :] = loc["u_bar"]
            w_ref[sl, :] = loc["w"]
            qd_ref[sl, :] = qv * loc["eg"]
            kd_ref[sl, :] = kv * loc["ek"]
            qk_ref[sl, :] = loc["qkraw"] * loc["decay"]
            gl_ref[pl.ds(ci, 1), :] = jnp.exp(loc["gl"])
            return carry

        lax.fori_loop(0, cps, chunk, 0)

    hspec = pl.BlockSpec((tm, HEAD), lambda h, i: (i, h))
    return pl.pallas_call(
        body, name="dn_chunk_fwd", grid=(N_HEADS_B, t // tm),
        in_specs=[hspec] * 5,
        out_specs=[hspec] * 4 + [pl.BlockSpec((None, tm, CH), lambda h, i: (h, i, 0)),
                                 pl.BlockSpec((cps, HEAD), lambda h, i: (i, h))],
        out_shape=[jax.ShapeDtypeStruct((t, D_B), F32)] * 4
        + [jax.ShapeDtypeStruct((N_HEADS_B, t, CH), F32), jax.ShapeDtypeStruct((t // CH, D_B), F32)],
        compiler_params=_cp("parallel", "parallel"),
    )(q, k, v, g, beta)


def _dn_scan_fwd(ub, w, qd, kd, qk, gl, cps=8):
    t = ub.shape[0]
    tm = cps * CH

    def body(ub_ref, w_ref, qd_ref, kd_ref, qk_ref, gl_ref, o_ref, st_ref, s_acc):
        @pl.when(pl.program_id(1) == 0)
        def _():
            s_acc[...] = jnp.zeros_like(s_acc)

        for ci in range(cps):
            sl = slice(ci * CH, (ci + 1) * CH)
            sv = s_acc[...]
            st_ref[ci * HEAD:(ci + 1) * HEAD, :] = sv
            u = ub_ref[sl, :] - _bdot(w_ref[sl, :], sv)
            o_ref[sl, :] = _bdot(qd_ref[sl, :], sv) + _bdot(qk_ref[sl, :], u)
            s_acc[...] = gl_ref[ci:ci + 1, :] * sv + _bdot(kd_ref[sl, :], u, TN)

    hspec = pl.BlockSpec((tm, HEAD), lambda h, i: (i, h))
    return pl.pallas_call(
        body, name="dn_scan_fwd", grid=(N_HEADS_B, t // tm),
        in_specs=[hspec] * 4 + [pl.BlockSpec((None, tm, CH), lambda h, i: (h, i, 0)),
                                pl.BlockSpec((cps, HEAD), lambda h, i: (i, h))],
        out_specs=[hspec, pl.BlockSpec((None, cps * HEAD, HEAD), lambda h, i: (h, i, 0))],
        out_shape=[jax.ShapeDtypeStruct((t, D_B), F32),
                   jax.ShapeDtypeStruct((N_HEADS_B, (t // CH) * HEAD, HEAD), F32)],
        scratch_shapes=[pltpu.VMEM((HEAD, HEAD), F32)],
        compiler_params=_cp("parallel", "arbitrary"),
    )(ub, w, qd, kd, qk, gl)


def _dn_scan_bwd(ub, w, qd, kd, qk, gl, st, do, cps=8):
    t = ub.shape[0]
    tm = cps * CH
    ns = t // tm

    def body(ub_ref, w_ref, qd_ref, kd_ref, qk_ref, gl_ref, st_ref, do_ref,
             dub_ref, dw_ref, dqd_ref, dkd_ref, dqk_ref, dgl_ref, ds_acc):
        @pl.when(pl.program_id(1) == 0)
        def _():
            ds_acc[...] = jnp.zeros_like(ds_acc)

        for ci in reversed(range(cps)):
            sl = slice(ci * CH, (ci + 1) * CH)
            sv = st_ref[ci * HEAD:(ci + 1) * HEAD, :]
            wv, kdv, qdv, qkv = w_ref[sl, :], kd_ref[sl, :], qd_ref[sl, :], qk_ref[sl, :]
            u = ub_ref[sl, :] - _bdot(wv, sv)
            dsv = ds_acc[...]
            dgl_ref[ci:ci + 1, :] = jnp.sum(_rowsum(dsv * sv), axis=0, keepdims=True) + jnp.zeros((1, HEAD), F32)
            dkd_ref[sl, :] = _bdot(u, dsv, NT)
            dov = do_ref[sl, :]
            du = _bdot(kdv, dsv) + _bdot(qkv, dov, TN)
            dqd_ref[sl, :] = _bdot(dov, sv, NT)
            dqk_ref[sl, :] = _bdot(dov, u, NT)
            dub_ref[sl, :] = du
            dw_ref[sl, :] = -_bdot(du, sv, NT)
            ds_acc[...] = gl_ref[ci:ci + 1, :] * dsv + _bdot(qdv, dov, TN) - _bdot(wv, du, TN)

    hspec = pl.BlockSpec((tm, HEAD), lambda h, i: (ns - 1 - i, h))
    qkspec = pl.BlockSpec((None, tm, CH), lambda h, i: (h, ns - 1 - i, 0))
    glspec = pl.BlockSpec((cps, HEAD), lambda h, i: (ns - 1 - i, h))
    return pl.pallas_call(
        body, name="dn_scan_bwd", grid=(N_HEADS_B, ns),
        in_specs=[hspec] * 4 + [qkspec, glspec,
                                pl.BlockSpec((None, cps * HEAD, HEAD), lambda h, i: (h, ns - 1 - i, 0)), hspec],
        out_specs=[hspec] * 4 + [qkspec, glspec],
        out_shape=[jax.ShapeDtypeStruct((t, D_B), F32)] * 4
        + [jax.ShapeDtypeStruct((N_HEADS_B, t, CH), F32), jax.ShapeDtypeStruct((t // CH, D_B), F32)],
        scratch_shapes=[pltpu.VMEM((HEAD, HEAD), F32)],
        compiler_params=_cp("parallel", "arbitrary"),
    )(ub, w, qd, kd, qk, gl, st, do)


def _dn_chunk_bwd(q, k, v, g, beta, dub, dw, dqd, dkd, dqk, dgl, cps=8):
    t = q.shape[0]
    tm = cps * CH

    def body(q_ref, k_ref, v_ref, g_ref, b_ref, dub_ref, dw_ref, dqd_ref, dkd_ref, dqk_ref, dgl_ref,
             dq_ref, dk_ref, dv_ref, dg_ref, db_ref):
        def chunk(ci, carry):
            sl = pl.ds(pl.multiple_of(ci * CH, CH), CH)
            qv, kv, vv, beta_v = q_ref[sl, :], k_ref[sl, :], v_ref[sl, :], b_ref[sl, :]
            loc = _chunk_local(qv, kv, vv, g_ref[sl, :], beta_v)
            incl, strict = loc["incl"], loc["strict"]
            tmat, eg, ek, decay, bcol, kk = loc["tm"], loc["eg"], loc["ek"], loc["decay"], loc["bcol"], loc["kk"]
            drv = _hdot(tmat, dub_ref[sl, :], TN)
            drk = _hdot(tmat, dw_ref[sl, :], TN)
            da = jnp.where(strict, -(_hdot(drv, loc["u_bar"], NT) + _hdot(drk, loc["w"], NT)), 0.0)
            dv_ref[sl, :] = beta_v * drv
            beg = beta_v * eg
            t1 = drk * kv
            dbeta = _rowsum(drv * vv + t1 * eg) + _rowsum(da * kk * decay)
            dkk = da * bcol * decay
            dqk_m = jnp.where(incl, dqk_ref[sl, :], 0.0)
            ddecay = da * bcol * kk + dqk_m * loc["qkraw"]
            dqkraw = dqk_m * decay
            dqdv, dkdv = dqd_ref[sl, :], dkd_ref[sl, :]
            dq_ref[sl, :] = _bdot(dqkraw, kv) + dqdv * eg
            dk_ref[sl, :] = (beg * drk + _bdot(dqkraw, qv, TN) + _bdot(dkk, kv) + _bdot(dkk, kv, TN)
                             + dkdv * ek)
            e = ddecay * decay
            skd = _rowsum(dkdv * kv * ek)
            dgc = _rowsum(beg * t1) + _rowsum(e) + _rowsum(dqdv * qv * eg) - skd
            colsum = _hdot(e, jnp.ones((CH, HEAD), F32), TN)
            last = jnp.sum(skd, axis=0, keepdims=True) + dgl_ref[pl.ds(ci, 1), :] * jnp.exp(loc["gl"])
            rid = lax.broadcasted_iota(jnp.int32, (CH, HEAD), 0)
            dgc_bc = (dgc - colsum) + jnp.where(rid == CH - 1, last, 0.0)
            um = (loc["r"] <= loc["c"]).astype(F32)
            dg_ref[sl, :] = _hdot(um, dgc_bc)
            db_ref[sl, :] = dbeta + jnp.zeros((CH, HEAD), F32)
            return carry

        lax.fori_loop(0, cps, chunk, 0)

    hspec = pl.BlockSpec((tm, HEAD), lambda h, i: (i, h))
    return pl.pallas_call(
        body, name="dn_chunk_bwd", grid=(N_HEADS_B, t // tm),
        in_specs=[hspec] * 9 + [pl.BlockSpec((None, tm, CH), lambda h, i: (h, i, 0)),
                                pl.BlockSpec((cps, HEAD), lambda h, i: (i, h))],
        out_specs=[hspec] * 5,
        out_shape=[jax.ShapeDtypeStruct((t, D_B), F32)] * 5,
        compiler_params=_cp("parallel", "parallel"),
    )(q, k, v, g, beta, dub, dw, dqd, dkd, dqk, dgl)


FLIPS = [(fx, fy, fc) for fx in (0, 1) for fy in (0, 1) for fc in (0, 1)][1:]


def _mesh_pos():
    return lax.axis_index("x"), lax.axis_index("y"), lax.axis_index("c")


def _peer(pos, flip):
    return tuple((1 - p) if f else p for p, f in zip(pos, flip))


def _dev_index(pos):
    return 4 * pos[0] + 2 * pos[1] + pos[2]


def _exchange(tensors, scatter, name):
    nt = len(tensors)
    hbm = pl.BlockSpec(memory_space=pltpu.HBM)

    def body(*refs):
        ins, outs = refs[:nt], refs[nt:2 * nt]
        send_sems, recv_sems, local_sems = refs[2 * nt:]
        pos = _mesh_pos()
        me = _dev_index(pos)

        def remote(ti, fi, landing):
            peer = _peer(pos, FLIPS[fi])
            src = ins[ti].at[_dev_index(peer)] if scatter[ti] else ins[ti]
            return pltpu.make_async_remote_copy(
                src_ref=src, dst_ref=outs[ti].at[landing(peer)],
                send_sem=send_sems.at[ti * 7 + fi], recv_sem=recv_sems.at[ti * 7 + fi],
                device_id=peer, device_id_type=pl.DeviceIdType.MESH)

        local = [pltpu.make_async_copy(ins[ti].at[me] if scatter[ti] else ins[ti], outs[ti].at[me],
                                       local_sems.at[ti]) for ti in range(nt)]
        for cp in local:
            cp.start()
        sends = [remote(ti, fi, lambda peer: me) for ti in range(nt) for fi in range(7)]
        for cp in sends:
            cp.start()
        for ti in range(nt):
            for fi in range(7):
                remote(ti, fi, _dev_index).wait_recv()
        for cp in sends:
            cp.wait_send()
        for cp in local:
            cp.wait()

    out_shape = [jax.ShapeDtypeStruct(x.shape if sc else (N_DEV,) + x.shape, x.dtype)
                 for x, sc in zip(tensors, scatter)]
    return pl.pallas_call(
        body, name=name, in_specs=[hbm] * nt, out_specs=[hbm] * nt, out_shape=out_shape,
        scratch_shapes=[pltpu.SemaphoreType.DMA((nt * 7,)), pltpu.SemaphoreType.DMA((nt * 7,)),
                        pltpu.SemaphoreType.DMA((nt,))],
        compiler_params=pltpu.CompilerParams(has_side_effects=True),
    )(*tensors)


def _adamw(land, w, m, v, name, tm=256):
    n, r, c = land.shape
    tm = r if r <= tm else max(s for s in range(8, tm + 1, 8) if r % s == 0)
    bc1 = 1.0 / (1.0 - ADAM_B1 ** ADAM_STEP)
    bc2 = 1.0 / (1.0 - ADAM_B2 ** ADAM_STEP)

    def body(l_ref, w_ref, m_ref, v_ref, g_ref, d_ref, nm_ref, nv_ref):
        g = l_ref[0].astype(F32)
        for i in range(1, n):
            g = g + l_ref[i].astype(F32)
        nm = ADAM_B1 * m_ref[...] + (1.0 - ADAM_B1) * g
        nv = ADAM_B2 * v_ref[...] + (1.0 - ADAM_B2) * (g * g)
        g_ref[...] = g
        nm_ref[...] = nm
        nv_ref[...] = nv
        d_ref[...] = -ADAM_LR * ((nm * bc1) / (jnp.sqrt(nv * bc2) + ADAM_EPS) + ADAM_WD * w_ref[...])

    spec = pl.BlockSpec((tm, c), lambda i: (i, 0))
    return pl.pallas_call(
        body, name=name, grid=(r // tm,),
        in_specs=[pl.BlockSpec((n, tm, c), lambda i: (0, i, 0)), spec, spec, spec],
        out_specs=[spec] * 4, out_shape=[jax.ShapeDtypeStruct((r, c), F32)] * 4,
        compiler_params=_cp("parallel"),
    )(land, w, m, v)


PACK_W = 2048


def _pack_rows(parts):
    flat = jnp.concatenate([p.reshape(-1).astype(F32) for p in parts])
    pad = (-flat.shape[0]) % (8 * PACK_W)
    return jnp.pad(flat, (0, pad)).reshape(-1, PACK_W)


def _unpack_rows(packed, shapes):
    flat = packed.reshape(-1)
    out, off = [], 0
    for s in shapes:
        n = math.prod(s)
        out.append(flat[off:off + n].reshape(s))
        off += n
    return out


def _col_slabs(gfull, width):
    r = gfull.shape[0]
    return jnp.transpose(gfull.reshape(r, N_DEV, width), (1, 0, 2)).astype(BF16)


def _row_slabs(gfull):
    return gfull.reshape(N_DEV, gfull.shape[0] // N_DEV, gfull.shape[1]).astype(BF16)


def _from_col_slabs(gathered):
    n, r, width = gathered.shape
    return jnp.transpose(gathered, (1, 0, 2)).reshape(r, n * width)


def _local_step(xs, target, norm_mix, wf_in, cw, a_log, dt_bias, dn_norm, wf_pa, wf_pd, wf_out, norm_ffn, wf_gu,
                wf_down, norm_final):
    d = D_MODEL
    n_main = D_PA + 4 * D_B
    w_pa_cols = wf_in[:, :D_PA]
    w_pb_cols = jnp.concatenate([
        wf_in[:, D_PA:n_main],
        jnp.repeat(wf_in[:, n_main:n_main + N_HEADS_B], HEAD, axis=1),
        jnp.repeat(wf_in[:, n_main + N_HEADS_B:n_main + 2 * N_HEADS_B], HEAD, axis=1),
        wf_in[:, n_main + 2 * N_HEADS_B:]], axis=1)
    w_all = jnp.concatenate([w_pa_cols, w_pb_cols], axis=1)
    a_log_bc = jnp.repeat(a_log, HEAD, axis=1)
    dt_bias_bc = jnp.repeat(dt_bias, HEAD, axis=1)

    u = _rms_fwd(xs, norm_mix)
    pa = _matmul(u, w_pa_cols, "nn", BF16, 1024, 1536, d, name="proj_a")
    pb = _matmul(u, w_pb_cols, "nn", F32, 1024, 1024, d, name="proj_b")
    os_, ls_ = [], []
    for gi, dil in enumerate(DILATIONS):
        o_g, l_g = _attn_fwd_group(pa, gi, dil)
        os_.append(o_g)
        ls_.append(l_g)
    y_att, lse = _attn_merge(os_, ls_)
    qn, kn, vn, gdec, beta = _dn_prep_fwd(pb, cw, a_log_bc, dt_bias_bc)
    ub, ww, qd, kd, qk, gl = _dn_chunk_fwd(qn, kn, vn, gdec, beta)
    o_dn, states = _dn_scan_fwd(ub, ww, qd, kd, qk, gl)
    o_gated = _head_norm_fwd(o_dn, pb, dn_norm)
    y_a = _matmul(y_att, wf_pa, "nn", F32, 1024, d, D_ATTN_OUT, name="proj_attn")
    y_b = _matmul(o_gated, wf_pd, "nn", F32, 1024, d, d, name="proj_delta")
    merged = _gate_merge_fwd(pb, y_a, y_b)
    h1 = _matmul(merged, wf_out, "nn", F32, 1024, d, d, add=xs, name="out_proj")
    hn = _rms_fwd(h1, norm_ffn)
    gu = _matmul(hn, wf_gu, "nn", BF16, 1024, 1408, d, name="ffn_in")
    act = _swiglu_fwd(gu)
    h2 = _matmul(act, wf_down, "nn", F32, 512, d, D_FF, add=h1, name="ffn_out")
    loss_part, dh2, d_norm_final = _final_loss(h2, norm_final.reshape(1, d), target)

    dact = _matmul(dh2, wf_down, "nt", BF16, 1024, 1408, d, name="d_act")
    gw_down = _matmul(act, dh2, "tn", F32, 1408, d, 512, name="gw_down")
    dgu = _swiglu_bwd(gu, dact)
    dhn = _matmul(dgu, wf_gu, "nt", BF16, 1024, d, 1408, name="d_hn")
    gw_gu = _matmul(hn, dgu, "tn", F32, d, 1408, 512, name="gw_gu")
    dh1, d_norm_ffn = _rms_bwd(h1, norm_ffn, dhn, dh2)
    dmerged = _matmul(dh1, wf_out, "nt", BF16, 1024, d, d, name="d_merged")
    gw_out = _matmul(merged, dh1, "tn", F32, d, d, 512, name="gw_out")
    dya, dyb, dga, dgb = _gate_merge_bwd(pb, y_a, y_b, dmerged)
    dy_att = _matmul(dya, wf_pa, "nt", BF16, 1024, D_ATTN_OUT, d, name="d_y_att")
    gw_pa = _matmul(y_att, dya, "tn", F32, D_ATTN_OUT, d, 512, name="gw_pa")
    do_gated = _matmul(dyb, wf_pd, "nt", BF16, 1024, d, d, name="d_o_gated")
    gw_pd = _matmul(o_gated, dyb, "tn", F32, d, d, 512, name="gw_pd")
    do_dn, dz, d_dn_norm = _head_norm_bwd(o_dn, pb, dn_norm, do_gated)
    dub, dww, dqd, dkd, dqk, dgl = _dn_scan_bwd(ub, ww, qd, kd, qk, gl, states, do_dn)
    dqn, dkn, dvn, dgdec, dbeta = _dn_chunk_bwd(qn, kn, vn, gdec, beta, dub, dww, dqd, dkd, dqk, dgl)
    dc, dbeta_raw, da_raw, d_conv_full, d_alog_bc, d_dt_bc = _dn_prep_bwd(
        pb, cw, a_log_bc, dt_bias_bc, gdec, dqn, dkn, dvn, dgdec, dbeta)
    dqkv_pre = _conv_bwd_input(dc, cw)
    dqs, dks, dvs = [], [], []
    for gi, dil in enumerate(DILATIONS):
        dq_g, dk_g, dv_g = _attn_bwd_group(pa, dy_att, y_att, lse, gi, dil)
        dqs.append(dq_g)
        dks.append(dk_g)
        dvs.append(dv_g)
    dproj = jnp.concatenate(dqs + dks + dvs + [dqkv_pre, dz, dbeta_raw, da_raw, dga, dgb], axis=1)
    du = _matmul(dproj, w_all, "nt", BF16, 1024, d, 1280, name="d_u")
    gw_all = _matmul(u, dproj, "tn", F32, d, 1280, 512, name="gw_in")
    dx, d_norm_mix = _rms_bwd(xs, norm_mix, du, dh1)

    gw_in = jnp.concatenate([
        gw_all[:, :n_main],
        gw_all[:, n_main:n_main + D_B].reshape(d, N_HEADS_B, HEAD).sum(-1),
        gw_all[:, n_main + D_B:n_main + 2 * D_B].reshape(d, N_HEADS_B, HEAD).sum(-1),
        gw_all[:, n_main + 2 * D_B:]], axis=1)
    d_a_log = d_alog_bc.reshape(1, N_HEADS_B, HEAD)[:, :, 0]
    d_dt_bias = d_dt_bc.reshape(1, N_HEADS_B, HEAD)[:, :, 0]
    return (loss_part, dx, gw_in, gw_pa, gw_pd, gw_out, gw_gu, gw_down, d_conv_full, d_norm_mix, d_norm_ffn,
            d_norm_final, d_dn_norm, d_a_log, d_dt_bias)


def kernel(x, norm_mix, w_in, conv_w, a_log, dt_bias, dn_norm, w_proj_attn, w_proj_delta, w_out, norm_ffn, w_gate, w_up, w_down, norm_final, loss_target, m_norm_mix, m_w_in, m_conv_w, m_a_log, m_dt_bias, m_dn_norm, m_w_proj_attn, m_w_proj_delta, m_w_out, m_norm_ffn, m_w_gate, m_w_up, m_w_down, m_norm_final, v_norm_mix, v_w_in, v_conv_w, v_a_log, v_dt_bias, v_dn_norm, v_w_proj_attn, v_w_proj_delta, v_w_out, v_norm_ffn, v_w_gate, v_w_up, v_w_down, v_norm_final):
    d = D_MODEL
    xs = x[0]
    target = loss_target[0]
    me = _dev_index(_mesh_pos())

    shards = [w_in[0], w_proj_attn[0], w_proj_delta[0], w_out[0], w_gate[0], w_up[0], w_down[0]]
    g_in, g_pa, g_pd, g_out, g_gate, g_up, g_down, g_conv = _exchange(
        [s.astype(BF16) for s in shards] + [conv_w[0]], [False] * 8, "gather_weights")
    wf_in = _from_col_slabs(g_in)
    wf_pa = _from_col_slabs(g_pa)
    wf_pd = g_pd.reshape(D_B, d)
    wf_out = g_out.reshape(d, d)
    wf_gu = jnp.concatenate([_from_col_slabs(g_gate), _from_col_slabs(g_up)], axis=1)
    wf_down = g_down.reshape(D_FF, d)
    cw = _from_col_slabs(g_conv)

    (loss_part, dx, gw_in, gw_pa, gw_pd, gw_out, gw_gu, gw_down, d_conv_full, d_norm_mix, d_norm_ffn, d_norm_final,
     d_dn_norm, d_a_log, d_dt_bias) = _local_step(xs, target, norm_mix, wf_in, cw, a_log, dt_bias, dn_norm, wf_pa,
                                                  wf_pd, wf_out, norm_ffn, wf_gu, wf_down, norm_final)

    slabs = [_col_slabs(gw_in, SHARD_IN), _col_slabs(gw_pa, d // N_DEV), _row_slabs(gw_pd), _row_slabs(gw_out),
             _col_slabs(gw_gu[:, :D_FF], D_FF // N_DEV), _col_slabs(gw_gu[:, D_FF:], D_FF // N_DEV),
             _row_slabs(gw_down)]
    small_shapes = [(1, d), (1, d), (d,), (1, HEAD), (1, N_HEADS_B), (1, N_HEADS_B), (1, 1), (CONV_WIDTH, 3 * D_B)]
    packed = _pack_rows([d_norm_mix, d_norm_ffn, d_norm_final, d_dn_norm, d_a_log, d_dt_bias,
                         loss_part[:, :1], d_conv_full])
    landed = _exchange(slabs + [packed], [True] * 7 + [False], "exchange_grads")
    zero1 = jnp.zeros((1, 1), F32)
    zconv = jnp.zeros((CONV_WIDTH, 3 * D_B), F32)
    small_w = _pack_rows([norm_mix, norm_ffn, norm_final, dn_norm, a_log, dt_bias, zero1, zconv])
    small_m = _pack_rows([m_norm_mix, m_norm_ffn, m_norm_final, m_dn_norm, m_a_log, m_dt_bias, zero1, zconv])
    small_v = _pack_rows([v_norm_mix, v_norm_ffn, v_norm_final, v_dn_norm, v_a_log, v_dt_bias, zero1, zconv])
    small = [_unpack_rows(z, small_shapes) for z in _adamw(landed[7], small_w, small_m, small_v, "adamw_small")]
    loss = small[0][6].reshape(())
    conv_shard = 3 * D_B // N_DEV
    g_conv_own = lax.dynamic_slice_in_dim(small[0][7], me * conv_shard, conv_shard, axis=1)
    r_conv = _adamw(g_conv_own[None], conv_w[0], m_conv_w[0], v_conv_w[0], "adamw_conv")
    big = [_adamw(landed[i], w[0], m[0], v[0], f"adamw_{i}") for i, (w, m, v) in enumerate([
        (w_in, m_w_in, v_w_in), (w_proj_attn, m_w_proj_attn, v_w_proj_attn),
        (w_proj_delta, m_w_proj_delta, v_w_proj_delta), (w_out, m_w_out, v_w_out),
        (w_gate, m_w_gate, v_w_gate), (w_up, m_w_up, v_w_up), (w_down, m_w_down, v_w_down)])]

    def leaves(k):
        sm = small[k]
        return [sm[0], big[0][k][None], r_conv[k][None], sm[4], sm[5], sm[3], big[1][k][None], big[2][k][None],
                big[3][k][None], sm[1], big[4][k][None], big[5][k][None], big[6][k][None], sm[2]]

    return (loss, dx[None], *leaves(0), *leaves(1), *leaves(2), *leaves(3))
```

```python
import math

import jax
import jax.numpy as jnp
from jax import lax
from jax.experimental import pallas as pl
from jax.experimental.pallas import tpu as pltpu

F32 = jnp.float32
BF16 = jnp.bfloat16
HI = lax.Precision.HIGH

D_MODEL = 1024
N_DEV = 8
HEAD = 128
N_HEADS_A = 12
HEADS_PER_GROUP = 4
DILATIONS = (1, 4, 16)
BLOCK_A = 128
D_ATTN = N_HEADS_A * HEAD
D_ATTN_OUT = HEADS_PER_GROUP * HEAD
N_HEADS_B = 8
D_B = N_HEADS_B * HEAD
CONV_WIDTH = 4
CH = 64
CHUNK_GROUP = 4
D_FF = 2816
EPS = 1e-6
D_IN = 3 * D_ATTN + 4 * D_B + 2 * N_HEADS_B + 2 * D_MODEL
SHARD_IN = D_IN // N_DEV
PB_Z, PB_BETA, PB_A, PB_GATE = 3072, 4096, 5120, 6144
D_PA = 3 * D_ATTN
ADAM_LR, ADAM_B1, ADAM_B2, ADAM_EPS, ADAM_WD, ADAM_STEP = 0.001, 0.9, 0.999, 1e-08, 0.01, 10
VMEM_LIMIT = 56 * 1024 * 1024

NN = ((1,), (0,))
NT = ((1,), (1,))
TN = ((0,), (0,))


def _dot(a, b, dims=NN, prec=None):
    return lax.dot_general(a, b, (dims, ((), ())), precision=prec, preferred_element_type=F32)


def _bdot(a, b, dims=NN):
    return _dot(a.astype(BF16), b.astype(BF16), dims)


def _hdot(a, b, dims=NN):
    return _dot(a.astype(F32), b.astype(F32), dims, HI)


def _cp(*sem):
    return pltpu.CompilerParams(dimension_semantics=sem, vmem_limit_bytes=VMEM_LIMIT)


def _sigmoid(x):
    return 1.0 / (1.0 + jnp.exp(-x))


def _softplus(x):
    return jnp.maximum(x, 0.0) + jnp.log(1.0 + jnp.exp(-jnp.abs(x)))


def _rowsum(x):
    return jnp.sum(x, axis=-1, keepdims=True)


def _matmul(a, b, mode, out_dtype, tm, tn, tk, add=None, name="mm"):
    if mode == "nn":
        (m, k), (k2, n) = a.shape, b.shape
    elif mode == "nt":
        (m, k), (n, k2) = a.shape, b.shape
    else:
        (k, m), (k2, n) = a.shape, b.shape
    assert k == k2, (a.shape, b.shape, mode)
    tm, tn, tk = min(tm, m), min(tn, n), min(tk, k)
    assert m % tm == 0 and n % tn == 0 and k % tk == 0, (a.shape, b.shape, tm, tn, tk)
    nk = k // tk
    dims = {"nn": NN, "nt": NT, "tn": TN}[mode]

    def body(*refs):
        if add is None:
            a_ref, b_ref, o_ref, acc = refs
            add_ref = None
        else:
            a_ref, b_ref, add_ref, o_ref, acc = refs
        kk = pl.program_id(2)

        @pl.when(kk == 0)
        def _():
            acc[...] = jnp.zeros_like(acc)

        acc[...] += _bdot(a_ref[...], b_ref[...], dims)

        @pl.when(kk == nk - 1)
        def _():
            r = acc[...]
            if add_ref is not None:
                r = r + add_ref[...].astype(F32)
            o_ref[...] = r.astype(out_dtype)

    a_spec = (pl.BlockSpec((tk, tm), lambda i, j, kk: (kk, i)) if mode == "tn"
              else pl.BlockSpec((tm, tk), lambda i, j, kk: (i, kk)))
    b_spec = (pl.BlockSpec((tn, tk), lambda i, j, kk: (j, kk)) if mode == "nt"
              else pl.BlockSpec((tk, tn), lambda i, j, kk: (kk, j)))
    in_specs = [a_spec, b_spec]
    args = [a, b]
    if add is not None:
        in_specs.append(pl.BlockSpec((tm, tn), lambda i, j, kk: (i, j)))
        args.append(add)
    return pl.pallas_call(
        body, name=name, grid=(m // tm, n // tn, nk),
        in_specs=in_specs, out_specs=pl.BlockSpec((tm, tn), lambda i, j, kk: (i, j)),
        out_shape=jax.ShapeDtypeStruct((m, n), out_dtype),
        scratch_shapes=[pltpu.VMEM((tm, tn), F32)],
        compiler_params=_cp("parallel", "parallel", "arbitrary"),
    )(*args)


def _row_spec(tm, cols, cb=0):
    return pl.BlockSpec((tm, cols), lambda i, cb=cb: (i, cb))


def _bcast_spec(rows, cols):
    return pl.BlockSpec((rows, cols), lambda i: (0, 0))


def _rms_fwd(x, w, tm=512):
    t, d = x.shape

    def body(x_ref, w_ref, o_ref):
        xv = x_ref[...]
        r = lax.rsqrt(jnp.mean(xv * xv, axis=-1, keepdims=True) + EPS)
        o_ref[...] = (xv * r * w_ref[...]).astype(BF16)

    return pl.pallas_call(
        body, name="rms_fwd", grid=(t // tm,),
        in_specs=[_row_spec(tm, d), _bcast_spec(1, d)], out_specs=_row_spec(tm, d),
        out_shape=jax.ShapeDtypeStruct((t, d), BF16), compiler_params=_cp("parallel"),
    )(x, w)


def _rms_bwd(x, w, dy, resid, tm=512):
    t, d = x.shape

    def body(x_ref, w_ref, dy_ref, res_ref, dx_ref, dw_ref):
        xv = x_ref[...]
        r = lax.rsqrt(jnp.mean(xv * xv, axis=-1, keepdims=True) + EPS)
        xh = xv * r
        dyv = dy_ref[...].astype(F32)
        dxh = dyv * w_ref[...]
        dx_ref[...] = res_ref[...] + r * (dxh - xh * jnp.mean(dxh * xh, axis=-1, keepdims=True))

        @pl.when(pl.program_id(0) == 0)
        def _():
            dw_ref[...] = jnp.zeros_like(dw_ref)

        dw_ref[...] += jnp.sum(dyv * xh, axis=0, keepdims=True)

    return pl.pallas_call(
        body, name="rms_bwd", grid=(t // tm,),
        in_specs=[_row_spec(tm, d), _bcast_spec(1, d), _row_spec(tm, d), _row_spec(tm, d)],
        out_specs=[_row_spec(tm, d), _bcast_spec(1, d)],
        out_shape=[jax.ShapeDtypeStruct((t, d), F32), jax.ShapeDtypeStruct((1, d), F32)],
        compiler_params=_cp("arbitrary"),
    )(x, w, dy, resid)


def _final_loss(h, w, target, tm=512):
    t, d = h.shape

    def body(h_ref, w_ref, t_ref, loss_ref, dh_ref, dw_ref):
        hv = h_ref[...]
        r = lax.rsqrt(jnp.mean(hv * hv, axis=-1, keepdims=True) + EPS)
        xh = hv * r
        wv = w_ref[...]
        err = xh * wv - t_ref[...]
        dy = err * (1.0 / d)
        dxh = dy * wv
        dh_ref[...] = r * (dxh - xh * jnp.mean(dxh * xh, axis=-1, keepdims=True))

        @pl.when(pl.program_id(0) == 0)
        def _():
            dw_ref[...] = jnp.zeros_like(dw_ref)
            loss_ref[...] = jnp.zeros_like(loss_ref)

        dw_ref[...] += jnp.sum(dy * xh, axis=0, keepdims=True)
        part = 0.5 * jnp.sum(jnp.mean(err * err, axis=-1, keepdims=True), axis=0, keepdims=True)
        loss_ref[...] += part + jnp.zeros((1, HEAD), F32)

    return pl.pallas_call(
        body, name="final_loss", grid=(t // tm,),
        in_specs=[_row_spec(tm, d), _bcast_spec(1, d), _row_spec(tm, d)],
        out_specs=[_bcast_spec(1, HEAD), _row_spec(tm, d), _bcast_spec(1, d)],
        out_shape=[jax.ShapeDtypeStruct((1, HEAD), F32), jax.ShapeDtypeStruct((t, d), F32),
                   jax.ShapeDtypeStruct((1, d), F32)],
        compiler_params=_cp("arbitrary"),
    )(h, w, target)


def _swiglu_fwd(gu, tm=256):
    t = gu.shape[0]
    ff = gu.shape[1] // 2

    def body(gu_ref, o_ref):
        g = gu_ref[:, :ff].astype(F32)
        o_ref[...] = (g * _sigmoid(g) * gu_ref[:, ff:].astype(F32)).astype(BF16)

    return pl.pallas_call(
        body, name="swiglu_fwd", grid=(t // tm,),
        in_specs=[_row_spec(tm, 2 * ff)], out_specs=_row_spec(tm, ff),
        out_shape=jax.ShapeDtypeStruct((t, ff), BF16), compiler_params=_cp("parallel"),
    )(gu)


def _swiglu_bwd(gu, dact, tm=256):
    t = gu.shape[0]
    ff = gu.shape[1] // 2

    def body(gu_ref, d_ref, o_ref):
        g = gu_ref[:, :ff].astype(F32)
        u = gu_ref[:, ff:].astype(F32)
        dv = d_ref[...].astype(F32)
        sg = _sigmoid(g)
        o_ref[:, :ff] = (dv * u * (sg + g * sg * (1.0 - sg))).astype(BF16)
        o_ref[:, ff:] = (dv * g * sg).astype(BF16)

    return pl.pallas_call(
        body, name="swiglu_bwd", grid=(t // tm,),
        in_specs=[_row_spec(tm, 2 * ff), _row_spec(tm, ff)], out_specs=_row_spec(tm, 2 * ff),
        out_shape=jax.ShapeDtypeStruct((t, 2 * ff), BF16), compiler_params=_cp("parallel"),
    )(gu, dact)


def _gate_merge_fwd(pb, ya, yb, tm=512):
    t, d = ya.shape
    cb = PB_GATE // d

    def body(ga_ref, gb_ref, ya_ref, yb_ref, o_ref):
        o_ref[...] = (_sigmoid(ga_ref[...]) * ya_ref[...] + _sigmoid(gb_ref[...]) * yb_ref[...]).astype(BF16)

    return pl.pallas_call(
        body, name="gate_merge_fwd", grid=(t // tm,),
        in_specs=[_row_spec(tm, d, cb), _row_spec(tm, d, cb + 1), _row_spec(tm, d), _row_spec(tm, d)],
        out_specs=_row_spec(tm, d),
        out_shape=jax.ShapeDtypeStruct((t, d), BF16), compiler_params=_cp("parallel"),
    )(pb, pb, ya, yb)


def _gate_merge_bwd(pb, ya, yb, dm, tm=512):
    t, d = ya.shape
    cb = PB_GATE // d

    def body(ga_ref, gb_ref, ya_ref, yb_ref, dm_ref, dya_ref, dyb_ref, dga_ref, dgb_ref):
        dmv = dm_ref[...].astype(F32)
        sa = _sigmoid(ga_ref[...])
        sb = _sigmoid(gb_ref[...])
        dya_ref[...] = (dmv * sa).astype(BF16)
        dyb_ref[...] = (dmv * sb).astype(BF16)
        dga_ref[...] = (dmv * ya_ref[...] * sa * (1.0 - sa)).astype(BF16)
        dgb_ref[...] = (dmv * yb_ref[...] * sb * (1.0 - sb)).astype(BF16)

    return pl.pallas_call(
        body, name="gate_merge_bwd", grid=(t // tm,),
        in_specs=[_row_spec(tm, d, cb), _row_spec(tm, d, cb + 1), _row_spec(tm, d), _row_spec(tm, d),
                  _row_spec(tm, d)],
        out_specs=[_row_spec(tm, d)] * 4,
        out_shape=[jax.ShapeDtypeStruct((t, d), BF16)] * 4, compiler_params=_cp("parallel"),
    )(pb, pb, ya, yb, dm)


def _head_norm_fwd(o, pb, wn, tm=512):
    t, d = o.shape
    nh = d // HEAD

    def body(o_ref, z_ref, w_ref, out_ref):
        wv = w_ref[...]
        for h in range(nh):
            sl = slice(h * HEAD, (h + 1) * HEAD)
            ov = o_ref[:, sl]
            zv = z_ref[:, sl]
            r = lax.rsqrt(jnp.mean(ov * ov, axis=-1, keepdims=True) + EPS)
            out_ref[:, sl] = (ov * r * wv * (zv * _sigmoid(zv))).astype(BF16)

    return pl.pallas_call(
        body, name="head_norm_fwd", grid=(t // tm,),
        in_specs=[_row_spec(tm, d), _row_spec(tm, d, PB_Z // d), _bcast_spec(1, HEAD)],
        out_specs=_row_spec(tm, d),
        out_shape=jax.ShapeDtypeStruct((t, d), BF16), compiler_params=_cp("parallel"),
    )(o, pb, wn)


def _head_norm_bwd(o, pb, wn, dout, tm=512):
    t, d = o.shape
    nh = d // HEAD

    def body(o_ref, z_ref, w_ref, d_ref, do_ref, dz_ref, dw_ref):
        wv = w_ref[...]
        dw_acc = jnp.zeros((1, HEAD), F32)
        for h in range(nh):
            sl = slice(h * HEAD, (h + 1) * HEAD)
            ov = o_ref[:, sl]
            zv = z_ref[:, sl]
            dv = d_ref[:, sl].astype(F32)
            r = lax.rsqrt(jnp.mean(ov * ov, axis=-1, keepdims=True) + EPS)
            xh = ov * r
            sz = _sigmoid(zv)
            dn = dv * (zv * sz)
            dz_ref[:, sl] = (dv * xh * wv * (sz + zv * sz * (1.0 - sz))).astype(BF16)
            dxh = dn * wv
            do_ref[:, sl] = r * (dxh - xh * jnp.mean(dxh * xh, axis=-1, keepdims=True))
            dw_acc = dw_acc + jnp.sum(dn * xh, axis=0, keepdims=True)

        @pl.when(pl.program_id(0) == 0)
        def _():
            dw_ref[...] = jnp.zeros_like(dw_ref)

        dw_ref[...] += dw_acc

    return pl.pallas_call(
        body, name="head_norm_bwd", grid=(t // tm,),
        in_specs=[_row_spec(tm, d), _row_spec(tm, d, PB_Z // d), _bcast_spec(1, HEAD), _row_spec(tm, d)],
        out_specs=[_row_spec(tm, d), _row_spec(tm, d), _bcast_spec(1, HEAD)],
        out_shape=[jax.ShapeDtypeStruct((t, d), F32), jax.ShapeDtypeStruct((t, d), BF16),
                   jax.ShapeDtypeStruct((1, HEAD), F32)],
        compiler_params=_cp("arbitrary"),
    )(o, pb, wn, dout)


def _attn_bias(gi, hh, dil):
    i = lax.broadcasted_iota(jnp.int32, (BLOCK_A, BLOCK_A), 0)
    j = lax.broadcasted_iota(jnp.int32, (BLOCK_A, BLOCK_A), 1)
    hf = (gi * HEADS_PER_GROUP + hh + 1).astype(F32)
    slope = jnp.exp(jnp.full((1, BLOCK_A), -8.0 * math.log(2.0) / N_HEADS_A, F32) * hf) * float(dil)
    d_prev = (BLOCK_A + i - j).astype(F32)
    d_cur = (i - j).astype(F32)
    return -slope * d_prev, -slope * d_cur, j >= i, j <= i


def _attn_views(t, dil, tq):
    rows = t // dil
    tq = min(tq, rows)
    nq = tq // BLOCK_A
    return rows, tq, nq, rows // tq


def _attn_fwd_group(pa, gi, dil, tq=512):
    t = pa.shape[0]
    rows, tq, nq, nsteps = _attn_views(t, dil, tq)
    pav = pa.reshape(rows, dil * D_PA)
    cpt = D_PA // HEAD
    scale = HEAD ** -0.5

    def body(q_ref, k_ref, v_ref, kp_ref, vp_ref, o_ref, l_ref):
        hh = pl.program_id(0) % HEADS_PER_GROUP
        step = pl.program_id(1)
        b_prev, b_cur, m_prev, m_cur = _attn_bias(gi, hh, dil)
        for a in range(nq):
            sl = slice(a * BLOCK_A, (a + 1) * BLOCK_A)
            q = q_ref[sl, :]
            if a == 0:
                kp, vp = kp_ref[...], vp_ref[...]
                mp = jnp.logical_and(m_prev, step > 0)
            else:
                psl = slice((a - 1) * BLOCK_A, a * BLOCK_A)
                kp, vp = k_ref[psl, :], v_ref[psl, :]
                mp = m_prev
            s_p = jnp.where(mp, _bdot(q, kp, NT) * scale + b_prev, -1e30)
            s_c = jnp.where(m_cur, _bdot(q, k_ref[sl, :], NT) * scale + b_cur, -1e30)
            m = jnp.maximum(jnp.max(s_p, axis=-1, keepdims=True), jnp.max(s_c, axis=-1, keepdims=True))
            p_p = jnp.exp(s_p - m)
            p_c = jnp.exp(s_c - m)
            den = _rowsum(p_p) + _rowsum(p_c)
            o_ref[sl, :] = (_bdot(p_p, vp) + _bdot(p_c, v_ref[sl, :])) / den
            l_ref[sl, :] = (m + jnp.log(den)) + jnp.zeros((BLOCK_A, HEAD), F32)

    def col(base):
        return lambda cb, s: (s, (cb // HEADS_PER_GROUP) * cpt + base + cb % HEADS_PER_GROUP)

    def col_prev(base):
        return lambda cb, s: (jnp.maximum(s * nq - 1, 0), (cb // HEADS_PER_GROUP) * cpt + base + cb % HEADS_PER_GROUP)

    qb, kb, vb = gi * HEADS_PER_GROUP, N_HEADS_A + gi * HEADS_PER_GROUP, 2 * N_HEADS_A + gi * HEADS_PER_GROUP
    ospec = pl.BlockSpec((tq, HEAD), lambda cb, s: (s, cb))
    o, l = pl.pallas_call(
        body, name=f"attn_fwd_g{gi}", grid=(dil * HEADS_PER_GROUP, nsteps),
        in_specs=[pl.BlockSpec((tq, HEAD), col(qb)), pl.BlockSpec((tq, HEAD), col(kb)),
                  pl.BlockSpec((tq, HEAD), col(vb)),
                  pl.BlockSpec((BLOCK_A, HEAD), col_prev(kb)), pl.BlockSpec((BLOCK_A, HEAD), col_prev(vb))],
        out_specs=[ospec, ospec],
        out_shape=[jax.ShapeDtypeStruct((rows, dil * D_ATTN_OUT), F32)] * 2,
        compiler_params=_cp("parallel", "parallel"),
    )(pav, pav, pav, pav, pav)
    return o.reshape(t, D_ATTN_OUT), l.reshape(t, D_ATTN_OUT)


def _attn_merge(os, ls, tm=512):
    t, d = os[0].shape

    def body(o0, o1, o2, l0, l1, l2, y_ref, lse_ref):
        a0, a1, a2 = l0[...], l1[...], l2[...]
        m = jnp.maximum(jnp.maximum(a0, a1), a2)
        e0, e1, e2 = jnp.exp(a0 - m), jnp.exp(a1 - m), jnp.exp(a2 - m)
        den = e0 + e1 + e2
        y_ref[...] = ((e0 * o0[...] + e1 * o1[...] + e2 * o2[...]) / den).astype(BF16)
        lse_ref[...] = m + jnp.log(den)

    return pl.pallas_call(
        body, name="attn_merge", grid=(t // tm,),
        in_specs=[_row_spec(tm, d)] * 6, out_specs=[_row_spec(tm, d)] * 2,
        out_shape=[jax.ShapeDtypeStruct((t, d), BF16), jax.ShapeDtypeStruct((t, d), F32)],
        compiler_params=_cp("parallel"),
    )(*os, *ls)


def _attn_bwd_group(pa, dy, y, lse, gi, dil, tq=512):
    t = pa.shape[0]
    rows, tq, nq, nsteps = _attn_views(t, dil, tq)
    pav = pa.reshape(rows, dil * D_PA)
    dyv = dy.reshape(rows, dil * D_ATTN_OUT)
    yv = y.reshape(rows, dil * D_ATTN_OUT)
    lv = lse.reshape(rows, dil * D_ATTN_OUT)
    cpt = D_PA // HEAD
    opt = D_ATTN_OUT // HEAD
    scale = HEAD ** -0.5
    nblk = rows // BLOCK_A

    def body(q_ref, k_ref, v_ref, dy_ref, y_ref, l_ref, kp_ref, vp_ref, qn_ref, dyn_ref, yn_ref, ln_ref,
             dq_ref, dk_ref, dv_ref):
        hh = pl.program_id(0) % HEADS_PER_GROUP
        step = pl.program_id(1)
        b_prev, b_cur, m_prev, m_cur = _attn_bias(gi, hh, dil)
        dk_acc = [jnp.zeros((BLOCK_A, HEAD), F32) for _ in range(nq)]
        dv_acc = [jnp.zeros((BLOCK_A, HEAD), F32) for _ in range(nq)]
        for a in range(nq + 1):
            if a < nq:
                sl = slice(a * BLOCK_A, (a + 1) * BLOCK_A)
                q, dyb, yb, lb = q_ref[sl, :], dy_ref[sl, :], y_ref[sl, :], l_ref[sl, :]
            else:
                q, dyb, yb, lb = qn_ref[...], dyn_ref[...], yn_ref[...], ln_ref[...]
            delta = _rowsum(dyb.astype(F32) * yb.astype(F32))
            dq = jnp.zeros((BLOCK_A, HEAD), F32)
            if a == 0:
                kp, vp = kp_ref[...], vp_ref[...]
                mp = jnp.logical_and(m_prev, step > 0)
            else:
                psl = slice((a - 1) * BLOCK_A, a * BLOCK_A)
                kp, vp = k_ref[psl, :], v_ref[psl, :]
                mp = jnp.logical_and(m_prev, step < nsteps - 1) if a == nq else m_prev
            s = _bdot(q, kp, NT) * scale + b_prev
            p = jnp.where(mp, jnp.exp(jnp.where(mp, s - lb, 0.0)), 0.0)
            ds = p * (_bdot(dyb, vp, NT) - delta)
            if a < nq:
                dq = dq + _bdot(ds, kp)
            if a > 0:
                dk_acc[a - 1] = dk_acc[a - 1] + _bdot(ds, q, TN)
                dv_acc[a - 1] = dv_acc[a - 1] + _bdot(p, dyb, TN)
            if a < nq:
                kc, vc = k_ref[sl, :], v_ref[sl, :]
                s = _bdot(q, kc, NT) * scale + b_cur
                p = jnp.where(m_cur, jnp.exp(jnp.where(m_cur, s - lb, 0.0)), 0.0)
                ds = p * (_bdot(dyb, vc, NT) - delta)
                dq = dq + _bdot(ds, kc)
                dk_acc[a] = dk_acc[a] + _bdot(ds, q, TN)
                dv_acc[a] = dv_acc[a] + _bdot(p, dyb, TN)
                dq_ref[sl, :] = (dq * scale).astype(BF16)
        for a in range(nq):
            sl = slice(a * BLOCK_A, (a + 1) * BLOCK_A)
            dk_ref[sl, :] = (dk_acc[a] * scale).astype(BF16)
            dv_ref[sl, :] = dv_acc[a].astype(BF16)

    def col(base, per):
        return lambda cb, s: (s, (cb // HEADS_PER_GROUP) * per + base + cb % HEADS_PER_GROUP)

    def col_prev(base, per):
        return lambda cb, s: (jnp.maximum(s * nq - 1, 0), (cb // HEADS_PER_GROUP) * per + base + cb % HEADS_PER_GROUP)

    def col_next(base, per):
        return lambda cb, s: (jnp.minimum((s + 1) * nq, nblk - 1),
                              (cb // HEADS_PER_GROUP) * per + base + cb % HEADS_PER_GROUP)

    qb, kb, vb = gi * HEADS_PER_GROUP, N_HEADS_A + gi * HEADS_PER_GROUP, 2 * N_HEADS_A + gi * HEADS_PER_GROUP
    big, small = (tq, HEAD), (BLOCK_A, HEAD)
    ospec = pl.BlockSpec(big, lambda cb, s: (s, cb))
    dq, dk, dv = pl.pallas_call(
        body, name=f"attn_bwd_g{gi}", grid=(dil * HEADS_PER_GROUP, nsteps),
        in_specs=[pl.BlockSpec(big, col(qb, cpt)), pl.BlockSpec(big, col(kb, cpt)), pl.BlockSpec(big, col(vb, cpt)),
                  pl.BlockSpec(big, col(0, opt)), pl.BlockSpec(big, col(0, opt)), pl.BlockSpec(big, col(0, opt)),
                  pl.BlockSpec(small, col_prev(kb, cpt)), pl.BlockSpec(small, col_prev(vb, cpt)),
                  pl.BlockSpec(small, col_next(qb, cpt)), pl.BlockSpec(small, col_next(0, opt)),
                  pl.BlockSpec(small, col_next(0, opt)), pl.BlockSpec(small, col_next(0, opt))],
        out_specs=[ospec] * 3,
        out_shape=[jax.ShapeDtypeStruct((rows, dil * D_ATTN_OUT), BF16)] * 3,
        compiler_params=_cp("parallel", "parallel"),
    )(pav, pav, pav, dyv, yv, lv, pav, pav, pav, dyv, yv, lv)
    return tuple(z.reshape(t, D_ATTN_OUT) for z in (dq, dk, dv))


def _shift_down(cur, prev8, s):
    if s == 0:
        return cur
    rolled = pltpu.roll(cur, s, 0)
    prolled = pltpu.roll(prev8, s, 0)
    rid = lax.broadcasted_iota(jnp.int32, prev8.shape, 0)
    top = jnp.where(rid < s, prolled, rolled[:8])
    return jnp.concatenate([top, rolled[8:]], axis=0)


def _shift_up(cur, next8, s):
    if s == 0:
        return cur
    n = cur.shape[0]
    rolled = pltpu.roll(cur, n - s, 0)
    nrolled = pltpu.roll(next8, 8 - s, 0)
    rid = lax.broadcasted_iota(jnp.int32, next8.shape, 0)
    bottom = jnp.where(rid >= 8 - s, nrolled, rolled[n - 8:])
    return jnp.concatenate([rolled[:n - 8], bottom], axis=0)


def _conv(xv, prev8, wv):
    c = jnp.zeros_like(xv)
    shifted = []
    for s in range(CONV_WIDTH):
        xs = _shift_down(xv, prev8, s)
        shifted.append(xs)
        c = c + wv[CONV_WIDTH - 1 - s:CONV_WIDTH - s, :] * xs
    return c, shifted


def _dn_prep_fwd(pb, conv_w, a_log_bc, dt_bias_bc, tm=256):
    t = pb.shape[0]
    c3 = 3 * D_B
    r8 = tm // 8

    def body(x_ref, xp_ref, b_ref, a_ref, w_ref, al_ref, dt_ref, q_ref, k_ref, v_ref, g_ref, beta_ref):
        prev8 = jnp.where(pl.program_id(0) > 0, xp_ref[...], 0.0)
        c, _ = _conv(x_ref[...], prev8, w_ref[...])
        s = c * _sigmoid(c)
        for h in range(N_HEADS_B):
            sl = slice(h * HEAD, (h + 1) * HEAD)
            sq = s[:, h * HEAD:(h + 1) * HEAD]
            q_ref[:, sl] = sq * lax.rsqrt(_rowsum(sq * sq) + EPS) * (HEAD ** -0.5)
            sk = s[:, D_B + h * HEAD:D_B + (h + 1) * HEAD]
            k_ref[:, sl] = sk * lax.rsqrt(_rowsum(sk * sk) + EPS)
        v_ref[...] = s[:, 2 * D_B:]
        beta_ref[...] = _sigmoid(b_ref[...])
        g_ref[...] = -jnp.exp(al_ref[...]) * _softplus(a_ref[...] + dt_ref[...])

    return pl.pallas_call(
        body, name="dn_prep_fwd", grid=(t // tm,),
        in_specs=[_row_spec(tm, c3, 0),
                  pl.BlockSpec((8, c3), lambda i: (jnp.maximum(i * r8 - 1, 0), 0)),
                  _row_spec(tm, D_B, PB_BETA // D_B), _row_spec(tm, D_B, PB_A // D_B),
                  _bcast_spec(CONV_WIDTH, c3), _bcast_spec(1, D_B), _bcast_spec(1, D_B)],
        out_specs=[_row_spec(tm, D_B)] * 5,
        out_shape=[jax.ShapeDtypeStruct((t, D_B), F32)] * 5,
        compiler_params=_cp("parallel"),
    )(pb, pb, pb, pb, conv_w, a_log_bc, dt_bias_bc)


def _dn_prep_bwd(pb, conv_w, a_log_bc, dt_bias_bc, g, dq, dk, dv, dg, dbeta, tm=128):
    t = pb.shape[0]
    c3 = 3 * D_B
    r8 = tm // 8

    def body(x_ref, xp_ref, b_ref, a_ref, w_ref, al_ref, dt_ref, g_ref, dq_ref, dk_ref, dv_ref, dg_ref, db_ref,
             dc_ref, dbr_ref, dar_ref, dw_ref, dal_ref, ddt_ref):
        prev8 = jnp.where(pl.program_id(0) > 0, xp_ref[...], 0.0)
        c, shifted = _conv(x_ref[...], prev8, w_ref[...])
        sg = _sigmoid(c)
        s = c * sg
        dsilu = sg + c * sg * (1.0 - sg)
        for h in range(N_HEADS_B):
            sl = slice(h * HEAD, (h + 1) * HEAD)
            for base, d_ref, mult in ((0, dq_ref, HEAD ** -0.5), (D_B, dk_ref, 1.0)):
                ssl = slice(base + h * HEAD, base + (h + 1) * HEAD)
                sv = s[:, ssl]
                r = lax.rsqrt(_rowsum(sv * sv) + EPS)
                yh = sv * r
                dyv = d_ref[:, sl] * mult
                dc_ref[:, ssl] = r * (dyv - yh * _rowsum(dyv * yh)) * dsilu[:, ssl]
        dc_ref[:, 2 * D_B:] = dv_ref[...] * dsilu[:, 2 * D_B:]
        dcv = dc_ref[...]

        @pl.when(pl.program_id(0) == 0)
        def _():
            dw_ref[...] = jnp.zeros_like(dw_ref)
            dal_ref[...] = jnp.zeros_like(dal_ref)
            ddt_ref[...] = jnp.zeros_like(ddt_ref)

        for sft in range(CONV_WIDTH):
            j = CONV_WIDTH - 1 - sft
            dw_ref[j:j + 1, :] += jnp.sum(dcv * shifted[sft], axis=0, keepdims=True)
        beta = _sigmoid(b_ref[...])
        dbr_ref[...] = (db_ref[...] * beta * (1.0 - beta) * (1.0 / HEAD)).astype(BF16)
        dgv = dg_ref[...]
        da = dgv * (-jnp.exp(al_ref[...])) * _sigmoid(a_ref[...] + dt_ref[...])
        dar_ref[...] = (da * (1.0 / HEAD)).astype(BF16)
        dal_ref[...] += jnp.sum(dgv * g_ref[...], axis=0, keepdims=True)
        ddt_ref[...] += jnp.sum(da, axis=0, keepdims=True)

    row = _row_spec(tm, D_B)
    return pl.pallas_call(
        body, name="dn_prep_bwd", grid=(t // tm,),
        in_specs=[_row_spec(tm, c3, 0),
                  pl.BlockSpec((8, c3), lambda i: (jnp.maximum(i * r8 - 1, 0), 0)),
                  _row_spec(tm, D_B, PB_BETA // D_B), _row_spec(tm, D_B, PB_A // D_B),
                  _bcast_spec(CONV_WIDTH, c3), _bcast_spec(1, D_B), _bcast_spec(1, D_B),
                  row, row, row, row, row, row],
        out_specs=[_row_spec(tm, c3), row, row, _bcast_spec(CONV_WIDTH, c3), _bcast_spec(1, D_B),
                   _bcast_spec(1, D_B)],
        out_shape=[jax.ShapeDtypeStruct((t, c3), F32), jax.ShapeDtypeStruct((t, D_B), BF16),
                   jax.ShapeDtypeStruct((t, D_B), BF16), jax.ShapeDtypeStruct((CONV_WIDTH, c3), F32),
                   jax.ShapeDtypeStruct((1, D_B), F32), jax.ShapeDtypeStruct((1, D_B), F32)],
        compiler_params=_cp("arbitrary"),
    )(pb, pb, pb, pb, conv_w, a_log_bc, dt_bias_bc, g, dq, dk, dv, dg, dbeta)


def _conv_bwd_input(dc, conv_w, tm=256):
    t, c3 = dc.shape
    r8 = tm // 8
    nlast = t // 8 - 1
    nsteps = t // tm

    def body(d_ref, dn_ref, w_ref, o_ref):
        next8 = jnp.where(pl.program_id(0) < nsteps - 1, dn_ref[...], 0.0)
        dv = d_ref[...]
        wv = w_ref[...]
        acc = jnp.zeros_like(dv)
        for s in range(CONV_WIDTH):
            acc = acc + wv[CONV_WIDTH - 1 - s:CONV_WIDTH - s, :] * _shift_up(dv, next8, s)
        o_ref[...] = acc.astype(BF16)

    return pl.pallas_call(
        body, name="conv_bwd_input", grid=(nsteps,),
        in_specs=[_row_spec(tm, c3), pl.BlockSpec((8, c3), lambda i: (jnp.minimum((i + 1) * r8, nlast), 0)),
                  _bcast_spec(CONV_WIDTH, c3)],
        out_specs=_row_spec(tm, c3),
        out_shape=jax.ShapeDtypeStruct((t, c3), BF16), compiler_params=_cp("parallel"),
    )(dc, dc, conv_w)


def _lanes(x):
    return x[:, :CH]


def _tri_inv(a_list, r, c):
    eye = (r == c).astype(F32)
    b16 = (r >> 4) == (c >> 4)
    b32 = (r >> 5) == (c >> 5)
    ns = [jnp.where(b16, -a, 0.0) for a in a_list]
    xs = [eye + n for n in ns]
    ps = [_hdot(n, n) for n in ns]
    for last in (False, False, True):
        xs = [x + _hdot(x, p) for x, p in zip(xs, ps)]
        if not last:
            ps = [_hdot(p, p) for p in ps]
    for mask in (jnp.logical_and(b32, jnp.logical_not(b16)), jnp.logical_not(b32)):
        ts = [_hdot(x, jnp.where(mask, a, 0.0)) for x, a in zip(xs, a_list)]
        xs = [x - _hdot(t, x) for x, t in zip(xs, ts)]
    return xs


def _chunk_local(qs, ks, vs, gs, betas):
    r = lax.broadcasted_iota(jnp.int32, (CH, CH), 0)
    c = lax.broadcasted_iota(jnp.int32, (CH, CH), 1)
    incl, strict = r >= c, r > c
    lm = incl.astype(F32)
    cums = [_hdot(lm, jnp.concatenate([g, jnp.where(strict, _lanes(g), 0.0)], axis=1)) for g in gs]
    gcbs = [cm[:, :HEAD] for cm in cums]
    decays = [jnp.where(incl, jnp.exp(jnp.where(incl, cm[:, HEAD:], 0.0)), 0.0) for cm in cums]
    bcols = [_lanes(b) for b in betas]
    kks = [_bdot(k, k, NT) for k in ks]
    qkraws = [_bdot(q, k, NT) for q, k in zip(qs, ks)]
    tms = _tri_inv([jnp.where(strict, bc * kk * dc, 0.0) for bc, kk, dc in zip(bcols, kks, decays)], r, c)
    egs = [jnp.exp(gcb) for gcb in gcbs]
    sols = [_hdot(tm, jnp.concatenate([b * v, b * eg * k], axis=1))
            for tm, b, v, eg, k in zip(tms, betas, vs, egs, ks)]
    gls = [gcb[CH - 1:CH, :] for gcb in gcbs]
    eks = [jnp.exp(gl - gcb) for gl, gcb in zip(gls, gcbs)]
    return [dict(incl=incl, strict=strict, r=r, c=c, decay=dc, bcol=bc, kk=kk, tm=tm, eg=eg,
                 u_bar=sol[:, :HEAD], w=sol[:, HEAD:], qkraw=qkraw, gl=gl, ek=ek)
            for dc, bc, kk, tm, eg, sol, qkraw, gl, ek in zip(decays, bcols, kks, tms, egs, sols, qkraws, gls, eks)]


def _dn_chunk_fwd(q, k, v, g, beta, cps=8):
    t = q.shape[0]
    tm = cps * CH

    def body(q_ref, k_ref, v_ref, g_ref, b_ref, ub_ref, w_ref, qd_ref, kd_ref, qk_ref, gl_ref):
        for base in range(0, cps, CHUNK_GROUP):
            sls = [slice((base + j) * CH, (base + j + 1) * CH) for j in range(CHUNK_GROUP)]
            qs, ks = [q_ref[sl, :] for sl in sls], [k_ref[sl, :] for sl in sls]
            locs = _chunk_local(qs, ks, [v_ref[sl, :] for sl in sls], [g_ref[sl, :] for sl in sls],
                                [b_ref[sl, :] for sl in sls])
            for j, (sl, qv, kv, loc) in enumerate(zip(sls, qs, ks, locs)):
                ub_ref[sl, :] = loc["u_bar"]
                w_ref[sl, :] = loc["w"]
                qd_ref[sl, :] = qv * loc["eg"]
                kd_ref[sl, :] = kv * loc["ek"]
                qk_ref[sl, :] = loc["qkraw"] * loc["decay"]
                gl_ref[base + j:base + j + 1, :] = jnp.exp(loc["gl"])

    hspec = pl.BlockSpec((tm, HEAD), lambda h, i: (i, h))
    return pl.pallas_call(
        body, name="dn_chunk_fwd", grid=(N_HEADS_B, t // tm),
        in_specs=[hspec] * 5,
        out_specs=[hspec] * 4 + [pl.BlockSpec((None, tm, CH), lambda h, i: (h, i, 0)),
                                 pl.BlockSpec((cps, HEAD), lambda h, i: (i, h))],
        out_shape=[jax.ShapeDtypeStruct((t, D_B), F32)] * 4
        + [jax.ShapeDtypeStruct((N_HEADS_B, t, CH), F32), jax.ShapeDtypeStruct((t // CH, D_B), F32)],
        compiler_params=_cp("parallel", "parallel"),
    )(q, k, v, g, beta)


def _dn_scan_fwd(ub, w, qd, kd, qk, gl, cps=8):
    t = ub.shape[0]
    tm = cps * CH

    def body(ub_ref, w_ref, qd_ref, kd_ref, qk_ref, gl_ref, o_ref, st_ref, s_acc):
        @pl.when(pl.program_id(1) == 0)
        def _():
            s_acc[...] = jnp.zeros_like(s_acc)

        for ci in range(cps):
            sl = slice(ci * CH, (ci + 1) * CH)
            sv = s_acc[...]
            st_ref[ci * HEAD:(ci + 1) * HEAD, :] = sv
            u = ub_ref[sl, :] - _bdot(w_ref[sl, :], sv)
            o_ref[sl, :] = _bdot(qd_ref[sl, :], sv) + _bdot(qk_ref[sl, :], u)
            s_acc[...] = gl_ref[ci:ci + 1, :] * sv + _bdot(kd_ref[sl, :], u, TN)

    hspec = pl.BlockSpec((tm, HEAD), lambda h, i: (i, h))
    return pl.pallas_call(
        body, name="dn_scan_fwd", grid=(N_HEADS_B, t // tm),
        in_specs=[hspec] * 4 + [pl.BlockSpec((None, tm, CH), lambda h, i: (h, i, 0)),
                                pl.BlockSpec((cps, HEAD), lambda h, i: (i, h))],
        out_specs=[hspec, pl.BlockSpec((None, cps * HEAD, HEAD), lambda h, i: (h, i, 0))],
        out_shape=[jax.ShapeDtypeStruct((t, D_B), F32),
                   jax.ShapeDtypeStruct((N_HEADS_B, (t // CH) * HEAD, HEAD), F32)],
        scratch_shapes=[pltpu.VMEM((HEAD, HEAD), F32)],
        compiler_params=_cp("parallel", "arbitrary"),
    )(ub, w, qd, kd, qk, gl)


def _dn_scan_bwd(ub, w, qd, kd, qk, gl, st, do, cps=8):
    t = ub.shape[0]
    tm = cps * CH
    ns = t // tm

    def body(ub_ref, w_ref, qd_ref, kd_ref, qk_ref, gl_ref, st_ref, do_ref,
             dub_ref, dw_ref, dqd_ref, dkd_ref, dqk_ref, dgl_ref, ds_acc):
        @pl.when(pl.program_id(1) == 0)
        def _():
            ds_acc[...] = jnp.zeros_like(ds_acc)

        for ci in reversed(range(cps)):
            sl = slice(ci * CH, (ci + 1) * CH)
            sv = st_ref[ci * HEAD:(ci + 1) * HEAD, :]
            wv, kdv, qdv, qkv = w_ref[sl, :], kd_ref[sl, :], qd_ref[sl, :], qk_ref[sl, :]
            u = ub_ref[sl, :] - _bdot(wv, sv)
            dsv = ds_acc[...]
            dgl_ref[ci:ci + 1, :] = jnp.sum(_rowsum(dsv * sv), axis=0, keepdims=True) + jnp.zeros((1, HEAD), F32)
            dkd_ref[sl, :] = _bdot(u, dsv, NT)
            dov = do_ref[sl, :]
            du = _bdot(kdv, dsv) + _bdot(qkv, dov, TN)
            dqd_ref[sl, :] = _bdot(dov, sv, NT)
            dqk_ref[sl, :] = _bdot(dov, u, NT)
            dub_ref[sl, :] = du
            dw_ref[sl, :] = -_bdot(du, sv, NT)
            ds_acc[...] = gl_ref[ci:ci + 1, :] * dsv + _bdot(qdv, dov, TN) - _bdot(wv, du, TN)

    hspec = pl.BlockSpec((tm, HEAD), lambda h, i: (ns - 1 - i, h))
    qkspec = pl.BlockSpec((None, tm, CH), lambda h, i: (h, ns - 1 - i, 0))
    glspec = pl.BlockSpec((cps, HEAD), lambda h, i: (ns - 1 - i, h))
    return pl.pallas_call(
        body, name="dn_scan_bwd", grid=(N_HEADS_B, ns),
        in_specs=[hspec] * 4 + [qkspec, glspec,
                                pl.BlockSpec((None, cps * HEAD, HEAD), lambda h, i: (h, ns - 1 - i, 0)), hspec],
        out_specs=[hspec] * 4 + [qkspec, glspec],
        out_shape=[jax.ShapeDtypeStruct((t, D_B), F32)] * 4
        + [jax.ShapeDtypeStruct((N_HEADS_B, t, CH), F32), jax.ShapeDtypeStruct((t // CH, D_B), F32)],
        scratch_shapes=[pltpu.VMEM((HEAD, HEAD), F32)],
        compiler_params=_cp("parallel", "arbitrary"),
    )(ub, w, qd, kd, qk, gl, st, do)


def _dn_chunk_bwd(q, k, v, g, beta, dub, dw, dqd, dkd, dqk, dgl, cps=8):
    t = q.shape[0]
    tm = cps * CH

    def body(q_ref, k_ref, v_ref, g_ref, b_ref, dub_ref, dw_ref, dqd_ref, dkd_ref, dqk_ref, dgl_ref,
             dq_ref, dk_ref, dv_ref, dg_ref, db_ref):
        ones = jnp.ones((CH, HEAD), F32)
        rid = lax.broadcasted_iota(jnp.int32, (CH, HEAD), 0)

        def rest(ci, sl, qv, kv, vv, beta_v, loc, dr, da):
            incl = loc["incl"]
            eg, ek, decay, bcol, kk = loc["eg"], loc["ek"], loc["decay"], loc["bcol"], loc["kk"]
            drv, drk = dr[:, :HEAD], dr[:, HEAD:]
            dv_ref[sl, :] = beta_v * drv
            beg = beta_v * eg
            t1 = drk * kv
            dbeta = _rowsum(drv * vv + t1 * eg) + _rowsum(da * kk * decay)
            dkk = da * bcol * decay
            dqk_m = jnp.where(incl, dqk_ref[sl, :], 0.0)
            ddecay = da * bcol * kk + dqk_m * loc["qkraw"]
            dqkraw = dqk_m * decay
            dqdv, dkdv = dqd_ref[sl, :], dkd_ref[sl, :]
            dq_ref[sl, :] = _bdot(dqkraw, kv) + dqdv * eg
            dk_ref[sl, :] = (beg * drk + _bdot(dqkraw, qv, TN) + _bdot(dkk, kv) + _bdot(dkk, kv, TN)
                             + dkdv * ek)
            e = ddecay * decay
            skd = _rowsum(dkdv * kv * ek)
            dgc = _rowsum(beg * t1) + _rowsum(e) + _rowsum(dqdv * qv * eg) - skd
            colsum = _hdot(e, ones, TN)
            last = jnp.sum(skd, axis=0, keepdims=True) + dgl_ref[ci:ci + 1, :] * jnp.exp(loc["gl"])
            db_ref[sl, :] = dbeta + jnp.zeros((CH, HEAD), F32)
            return (dgc - colsum) + jnp.where(rid == CH - 1, last, 0.0)

        for base in range(0, cps, CHUNK_GROUP):
            cis = list(range(base, base + CHUNK_GROUP))
            sls = [slice(ci * CH, (ci + 1) * CH) for ci in cis]
            qs, ks, vs = [q_ref[sl, :] for sl in sls], [k_ref[sl, :] for sl in sls], [v_ref[sl, :] for sl in sls]
            betas = [b_ref[sl, :] for sl in sls]
            locs = _chunk_local(qs, ks, vs, [g_ref[sl, :] for sl in sls], betas)
            drs = [_hdot(loc["tm"], jnp.concatenate([dub_ref[sl, :], dw_ref[sl, :]], axis=1), TN)
                   for loc, sl in zip(locs, sls)]
            das = [jnp.where(loc["strict"],
                             -_hdot(dr, jnp.concatenate([loc["u_bar"], loc["w"]], axis=1), NT), 0.0)
                   for loc, dr in zip(locs, drs)]
            dgcs = [rest(*args) for args in zip(cis, sls, qs, ks, vs, betas, locs, drs, das)]
            um = (locs[0]["r"] <= locs[0]["c"]).astype(F32)
            for sl, dgc_bc in zip(sls, dgcs):
                dg_ref[sl, :] = _hdot(um, dgc_bc)

    hspec = pl.BlockSpec((tm, HEAD), lambda h, i: (i, h))
    return pl.pallas_call(
        body, name="dn_chunk_bwd", grid=(N_HEADS_B, t // tm),
        in_specs=[hspec] * 9 + [pl.BlockSpec((None, tm, CH), lambda h, i: (h, i, 0)),
                                pl.BlockSpec((cps, HEAD), lambda h, i: (i, h))],
        out_specs=[hspec] * 5,
        out_shape=[jax.ShapeDtypeStruct((t, D_B), F32)] * 5,
        compiler_params=_cp("parallel", "parallel"),
    )(q, k, v, g, beta, dub, dw, dqd, dkd, dqk, dgl)


FLIPS = [(fx, fy, fc) for fx in (0, 1) for fy in (0, 1) for fc in (0, 1)][1:]


def _mesh_pos():
    return lax.axis_index("x"), lax.axis_index("y"), lax.axis_index("c")


def _peer(pos, flip):
    return tuple((1 - p) if f else p for p, f in zip(pos, flip))


def _dev_index(pos):
    return 4 * pos[0] + 2 * pos[1] + pos[2]


def _exchange(tensors, scatter, name):
    nt = len(tensors)
    hbm = pl.BlockSpec(memory_space=pltpu.HBM)

    def body(*refs):
        ins, outs = refs[:nt], refs[nt:2 * nt]
        send_sems, recv_sems, local_sems = refs[2 * nt:]
        pos = _mesh_pos()
        me = _dev_index(pos)

        def remote(ti, fi, landing):
            peer = _peer(pos, FLIPS[fi])
            src = ins[ti].at[_dev_index(peer)] if scatter[ti] else ins[ti]
            return pltpu.make_async_remote_copy(
                src_ref=src, dst_ref=outs[ti].at[landing(peer)],
                send_sem=send_sems.at[ti * 7 + fi], recv_sem=recv_sems.at[ti * 7 + fi],
                device_id=peer, device_id_type=pl.DeviceIdType.MESH)

        local = [pltpu.make_async_copy(ins[ti].at[me] if scatter[ti] else ins[ti], outs[ti].at[me],
                                       local_sems.at[ti]) for ti in range(nt)]
        for cp in local:
            cp.start()
        sends = [remote(ti, fi, lambda peer: me) for ti in range(nt) for fi in range(7)]
        for cp in sends:
            cp.start()
        for ti in range(nt):
            for fi in range(7):
                remote(ti, fi, _dev_index).wait_recv()
        for cp in sends:
            cp.wait_send()
        for cp in local:
            cp.wait()

    out_shape = [jax.ShapeDtypeStruct(x.shape if sc else (N_DEV,) + x.shape, x.dtype)
                 for x, sc in zip(tensors, scatter)]
    return pl.pallas_call(
        body, name=name, in_specs=[hbm] * nt, out_specs=[hbm] * nt, out_shape=out_shape,
        scratch_shapes=[pltpu.SemaphoreType.DMA((nt * 7,)), pltpu.SemaphoreType.DMA((nt * 7,)),
                        pltpu.SemaphoreType.DMA((nt,))],
        compiler_params=pltpu.CompilerParams(has_side_effects=True),
    )(*tensors)


def _adamw(land, w, m, v, name, tm=256):
    n, r, c = land.shape
    tm = r if r <= tm else max(s for s in range(8, tm + 1, 8) if r % s == 0)
    bc1 = 1.0 / (1.0 - ADAM_B1 ** ADAM_STEP)
    bc2 = 1.0 / (1.0 - ADAM_B2 ** ADAM_STEP)

    def body(l_ref, w_ref, m_ref, v_ref, g_ref, d_ref, nm_ref, nv_ref):
        g = l_ref[0].astype(F32)
        for i in range(1, n):
            g = g + l_ref[i].astype(F32)
        nm = ADAM_B1 * m_ref[...] + (1.0 - ADAM_B1) * g
        nv = ADAM_B2 * v_ref[...] + (1.0 - ADAM_B2) * (g * g)
        g_ref[...] = g
        nm_ref[...] = nm
        nv_ref[...] = nv
        d_ref[...] = -ADAM_LR * ((nm * bc1) / (jnp.sqrt(nv * bc2) + ADAM_EPS) + ADAM_WD * w_ref[...])

    spec = pl.BlockSpec((tm, c), lambda i: (i, 0))
    return pl.pallas_call(
        body, name=name, grid=(r // tm,),
        in_specs=[pl.BlockSpec((n, tm, c), lambda i: (0, i, 0)), spec, spec, spec],
        out_specs=[spec] * 4, out_shape=[jax.ShapeDtypeStruct((r, c), F32)] * 4,
        compiler_params=_cp("parallel"),
    )(land, w, m, v)


PACK_W = 2048


def _pack_rows(parts):
    flat = jnp.concatenate([p.reshape(-1).astype(F32) for p in parts])
    pad = (-flat.shape[0]) % (8 * PACK_W)
    return jnp.pad(flat, (0, pad)).reshape(-1, PACK_W)


def _unpack_rows(packed, shapes):
    flat = packed.reshape(-1)
    out, off = [], 0
    for s in shapes:
        n = math.prod(s)
        out.append(flat[off:off + n].reshape(s))
        off += n
    return out


def _col_slabs(gfull, width):
    r = gfull.shape[0]
    return jnp.transpose(gfull.reshape(r, N_DEV, width), (1, 0, 2)).astype(BF16)


def _row_slabs(gfull):
    return gfull.reshape(N_DEV, gfull.shape[0] // N_DEV, gfull.shape[1]).astype(BF16)


def _from_col_slabs(gathered):
    n, r, width = gathered.shape
    return jnp.transpose(gathered, (1, 0, 2)).reshape(r, n * width)


def _local_step(xs, target, norm_mix, wf_in, cw, a_log, dt_bias, dn_norm, wf_pa, wf_pd, wf_out, norm_ffn, wf_gu,
                wf_down, norm_final):
    d = D_MODEL
    n_main = D_PA + 4 * D_B
    w_pa_cols = wf_in[:, :D_PA]
    w_pb_cols = jnp.concatenate([
        wf_in[:, D_PA:n_main],
        jnp.repeat(wf_in[:, n_main:n_main + N_HEADS_B], HEAD, axis=1),
        jnp.repeat(wf_in[:, n_main + N_HEADS_B:n_main + 2 * N_HEADS_B], HEAD, axis=1),
        wf_in[:, n_main + 2 * N_HEADS_B:]], axis=1)
    w_all = jnp.concatenate([w_pa_cols, w_pb_cols], axis=1)
    a_log_bc = jnp.repeat(a_log, HEAD, axis=1)
    dt_bias_bc = jnp.repeat(dt_bias, HEAD, axis=1)

    u = _rms_fwd(xs, norm_mix)
    pa = _matmul(u, w_pa_cols, "nn", BF16, 1024, 1536, d, name="proj_a")
    pb = _matmul(u, w_pb_cols, "nn", F32, 1024, 1024, d, name="proj_b")
    os_, ls_ = [], []
    for gi, dil in enumerate(DILATIONS):
        o_g, l_g = _attn_fwd_group(pa, gi, dil)
        os_.append(o_g)
        ls_.append(l_g)
    y_att, lse = _attn_merge(os_, ls_)
    qn, kn, vn, gdec, beta = _dn_prep_fwd(pb, cw, a_log_bc, dt_bias_bc)
    ub, ww, qd, kd, qk, gl = _dn_chunk_fwd(qn, kn, vn, gdec, beta)
    o_dn, states = _dn_scan_fwd(ub, ww, qd, kd, qk, gl)
    o_gated = _head_norm_fwd(o_dn, pb, dn_norm)
    y_a = _matmul(y_att, wf_pa, "nn", F32, 1024, d, D_ATTN_OUT, name="proj_attn")
    y_b = _matmul(o_gated, wf_pd, "nn", F32, 1024, d, d, name="proj_delta")
    merged = _gate_merge_fwd(pb, y_a, y_b)
    h1 = _matmul(merged, wf_out, "nn", F32, 1024, d, d, add=xs, name="out_proj")
    hn = _rms_fwd(h1, norm_ffn)
    gu = _matmul(hn, wf_gu, "nn", BF16, 1024, 1408, d, name="ffn_in")
    act = _swiglu_fwd(gu)
    h2 = _matmul(act, wf_down, "nn", F32, 512, d, D_FF, add=h1, name="ffn_out")
    loss_part, dh2, d_norm_final = _final_loss(h2, norm_final.reshape(1, d), target)

    dact = _matmul(dh2, wf_down, "nt", BF16, 1024, 1408, d, name="d_act")
    gw_down = _matmul(act, dh2, "tn", F32, 1408, d, 512, name="gw_down")
    dgu = _swiglu_bwd(gu, dact)
    dhn = _matmul(dgu, wf_gu, "nt", BF16, 1024, d, 1408, name="d_hn")
    gw_gu = _matmul(hn, dgu, "tn", F32, d, 1408, 512, name="gw_gu")
    dh1, d_norm_ffn = _rms_bwd(h1, norm_ffn, dhn, dh2)
    dmerged = _matmul(dh1, wf_out, "nt", BF16, 1024, d, d, name="d_merged")
    gw_out = _matmul(merged, dh1, "tn", F32, d, d, 512, name="gw_out")
    dya, dyb, dga, dgb = _gate_merge_bwd(pb, y_a, y_b, dmerged)
    dy_att = _matmul(dya, wf_pa, "nt", BF16, 1024, D_ATTN_OUT, d, name="d_y_att")
    gw_pa = _matmul(y_att, dya, "tn", F32, D_ATTN_OUT, d, 512, name="gw_pa")
    do_gated = _matmul(dyb, wf_pd, "nt", BF16, 1024, d, d, name="d_o_gated")
    gw_pd = _matmul(o_gated, dyb, "tn", F32, d, d, 512, name="gw_pd")
    do_dn, dz, d_dn_norm = _head_norm_bwd(o_dn, pb, dn_norm, do_gated)
    dub, dww, dqd, dkd, dqk, dgl = _dn_scan_bwd(ub, ww, qd, kd, qk, gl, states, do_dn)
    dqn, dkn, dvn, dgdec, dbeta = _dn_chunk_bwd(qn, kn, vn, gdec, beta, dub, dww, dqd, dkd, dqk, dgl)
    dc, dbeta_raw, da_raw, d_conv_full, d_alog_bc, d_dt_bc = _dn_prep_bwd(
        pb, cw, a_log_bc, dt_bias_bc, gdec, dqn, dkn, dvn, dgdec, dbeta)
    dqkv_pre = _conv_bwd_input(dc, cw)
    dqs, dks, dvs = [], [], []
    for gi, dil in enumerate(DILATIONS):
        dq_g, dk_g, dv_g = _attn_bwd_group(pa, dy_att, y_att, lse, gi, dil)
        dqs.append(dq_g)
        dks.append(dk_g)
        dvs.append(dv_g)
    dproj = jnp.concatenate(dqs + dks + dvs + [dqkv_pre, dz, dbeta_raw, da_raw, dga, dgb], axis=1)
    du = _matmul(dproj, w_all, "nt", BF16, 1024, d, 1280, name="d_u")
    gw_all = _matmul(u, dproj, "tn", F32, d, 1280, 512, name="gw_in")
    dx, d_norm_mix = _rms_bwd(xs, norm_mix, du, dh1)

    gw_in = jnp.concatenate([
        gw_all[:, :n_main],
        gw_all[:, n_main:n_main + D_B].reshape(d, N_HEADS_B, HEAD).sum(-1),
        gw_all[:, n_main + D_B:n_main + 2 * D_B].reshape(d, N_HEADS_B, HEAD).sum(-1),
        gw_all[:, n_main + 2 * D_B:]], axis=1)
    d_a_log = d_alog_bc.reshape(1, N_HEADS_B, HEAD)[:, :, 0]
    d_dt_bias = d_dt_bc.reshape(1, N_HEADS_B, HEAD)[:, :, 0]
    return (loss_part, dx, gw_in, gw_pa, gw_pd, gw_out, gw_gu, gw_down, d_conv_full, d_norm_mix, d_norm_ffn,
            d_norm_final, d_dn_norm, d_a_log, d_dt_bias)


def kernel(x, norm_mix, w_in, conv_w, a_log, dt_bias, dn_norm, w_proj_attn, w_proj_delta, w_out, norm_ffn, w_gate, w_up, w_down, norm_final, loss_target, m_norm_mix, m_w_in, m_conv_w, m_a_log, m_dt_bias, m_dn_norm, m_w_proj_attn, m_w_proj_delta, m_w_out, m_norm_ffn, m_w_gate, m_w_up, m_w_down, m_norm_final, v_norm_mix, v_w_in, v_conv_w, v_a_log, v_dt_bias, v_dn_norm, v_w_proj_attn, v_w_proj_delta, v_w_out, v_norm_ffn, v_w_gate, v_w_up, v_w_down, v_norm_final):
    d = D_MODEL
    xs = x[0]
    target = loss_target[0]
    me = _dev_index(_mesh_pos())

    shards = [w_in[0], w_proj_attn[0], w_proj_delta[0], w_out[0], w_gate[0], w_up[0], w_down[0]]
    g_in, g_pa, g_pd, g_out, g_gate, g_up, g_down, g_conv = _exchange(
        [s.astype(BF16) for s in shards] + [conv_w[0]], [False] * 8, "gather_weights")
    wf_in = _from_col_slabs(g_in)
    wf_pa = _from_col_slabs(g_pa)
    wf_pd = g_pd.reshape(D_B, d)
    wf_out = g_out.reshape(d, d)
    wf_gu = jnp.concatenate([_from_col_slabs(g_gate), _from_col_slabs(g_up)], axis=1)
    wf_down = g_down.reshape(D_FF, d)
    cw = _from_col_slabs(g_conv)

    (loss_part, dx, gw_in, gw_pa, gw_pd, gw_out, gw_gu, gw_down, d_conv_full, d_norm_mix, d_norm_ffn, d_norm_final,
     d_dn_norm, d_a_log, d_dt_bias) = _local_step(xs, target, norm_mix, wf_in, cw, a_log, dt_bias, dn_norm, wf_pa,
                                                  wf_pd, wf_out, norm_ffn, wf_gu, wf_down, norm_final)

    slabs = [_col_slabs(gw_in, SHARD_IN), _col_slabs(gw_pa, d // N_DEV), _row_slabs(gw_pd), _row_slabs(gw_out),
             _col_slabs(gw_gu[:, :D_FF], D_FF // N_DEV), _col_slabs(gw_gu[:, D_FF:], D_FF // N_DEV),
             _row_slabs(gw_down)]
    small_shapes = [(1, d), (1, d), (d,), (1, HEAD), (1, N_HEADS_B), (1, N_HEADS_B), (1, 1), (CONV_WIDTH, 3 * D_B)]
    packed = _pack_rows([d_norm_mix, d_norm_ffn, d_norm_final, d_dn_norm, d_a_log, d_dt_bias,
                         loss_part[:, :1], d_conv_full])
    landed = _exchange(slabs + [packed], [True] * 7 + [False], "exchange_grads")
    zero1 = jnp.zeros((1, 1), F32)
    zconv = jnp.zeros((CONV_WIDTH, 3 * D_B), F32)
    small_w = _pack_rows([norm_mix, norm_ffn, norm_final, dn_norm, a_log, dt_bias, zero1, zconv])
    small_m = _pack_rows([m_norm_mix, m_norm_ffn, m_norm_final, m_dn_norm, m_a_log, m_dt_bias, zero1, zconv])
    small_v = _pack_rows([v_norm_mix, v_norm_ffn, v_norm_final, v_dn_norm, v_a_log, v_dt_bias, zero1, zconv])
    small = [_unpack_rows(z, small_shapes) for z in _adamw(landed[7], small_w, small_m, small_v, "adamw_small")]
    loss = small[0][6].reshape(())
    conv_shard = 3 * D_B // N_DEV
    g_conv_own = lax.dynamic_slice_in_dim(small[0][7], me * conv_shard, conv_shard, axis=1)
    r_conv = _adamw(g_conv_own[None], conv_w[0], m_conv_w[0], v_conv_w[0], "adamw_conv")
    big = [_adamw(landed[i], w[0], m[0], v[0], f"adamw_{i}") for i, (w, m, v) in enumerate([
        (w_in, m_w_in, v_w_in), (w_proj_attn, m_w_proj_attn, v_w_proj_attn),
        (w_proj_delta, m_w_proj_delta, v_w_proj_delta), (w_out, m_w_out, v_w_out),
        (w_gate, m_w_gate, v_w_gate), (w_up, m_w_up, v_w_up), (w_down, m_w_down, v_w_down)])]

    def leaves(k):
        sm = small[k]
        return [sm[0], big[0][k][None], r_conv[k][None], sm[4], sm[5], sm[3], big[1][k][None], big[2][k][None],
                big[3][k][None], sm[1], big[4][k][None], big[5][k][None], big[6][k][None], sm[2]]

    return (loss, dx[None], *leaves(0), *leaves(1), *leaves(2), *leaves(3))
```

```python
import math

import jax
import jax.numpy as jnp
from jax import lax
from jax.experimental import pallas as pl
from jax.experimental.pallas import tpu as pltpu

F32 = jnp.float32
BF16 = jnp.bfloat16
HI = lax.Precision.HIGH

D_MODEL = 1024
N_DEV = 8
HEAD = 128
N_HEADS_A = 12
HEADS_PER_GROUP = 4
DILATIONS = (1, 4, 16)
BLOCK_A = 128
D_ATTN = N_HEADS_A * HEAD
D_ATTN_OUT = HEADS_PER_GROUP * HEAD
N_HEADS_B = 8
D_B = N_HEADS_B * HEAD
CONV_WIDTH = 4
CH = 64
CHUNK_GROUP = 8
SCAN_HEADS = 4
D_FF = 2816
EPS = 1e-6
D_IN = 3 * D_ATTN + 4 * D_B + 2 * N_HEADS_B + 2 * D_MODEL
SHARD_IN = D_IN // N_DEV
PB_Z, PB_BETA, PB_A, PB_GATE = 3072, 4096, 5120, 6144
D_PA = 3 * D_ATTN
ADAM_LR, ADAM_B1, ADAM_B2, ADAM_EPS, ADAM_WD, ADAM_STEP = 0.001, 0.9, 0.999, 1e-08, 0.01, 10
VMEM_LIMIT = 56 * 1024 * 1024

NN = ((1,), (0,))
NT = ((1,), (1,))
TN = ((0,), (0,))


def _dot(a, b, dims=NN, prec=None):
    return lax.dot_general(a, b, (dims, ((), ())), precision=prec, preferred_element_type=F32)


def _bdot(a, b, dims=NN):
    return _dot(a.astype(BF16), b.astype(BF16), dims)


def _hdot(a, b, dims=NN):
    return _dot(a.astype(F32), b.astype(F32), dims, HI)


def _cp(*sem):
    return pltpu.CompilerParams(dimension_semantics=sem, vmem_limit_bytes=VMEM_LIMIT)


def _sigmoid(x):
    return 1.0 / (1.0 + jnp.exp(-x))


def _softplus(x):
    return jnp.maximum(x, 0.0) + jnp.log(1.0 + jnp.exp(-jnp.abs(x)))


def _rowsum(x):
    return jnp.sum(x, axis=-1, keepdims=True)


def _matmul(a, b, mode, out_dtype, tm, tn, tk, add=None, name="mm"):
    if mode == "nn":
        (m, k), (k2, n) = a.shape, b.shape
    elif mode == "nt":
        (m, k), (n, k2) = a.shape, b.shape
    else:
        (k, m), (k2, n) = a.shape, b.shape
    assert k == k2, (a.shape, b.shape, mode)
    tm, tn, tk = min(tm, m), min(tn, n), min(tk, k)
    assert m % tm == 0 and n % tn == 0 and k % tk == 0, (a.shape, b.shape, tm, tn, tk)
    nk = k // tk
    dims = {"nn": NN, "nt": NT, "tn": TN}[mode]

    def body(*refs):
        if add is None:
            a_ref, b_ref, o_ref, acc = refs
            add_ref = None
        else:
            a_ref, b_ref, add_ref, o_ref, acc = refs
        kk = pl.program_id(2)

        @pl.when(kk == 0)
        def _():
            acc[...] = jnp.zeros_like(acc)

        acc[...] += _bdot(a_ref[...], b_ref[...], dims)

        @pl.when(kk == nk - 1)
        def _():
            r = acc[...]
            if add_ref is not None:
                r = r + add_ref[...].astype(F32)
            o_ref[...] = r.astype(out_dtype)

    a_spec = (pl.BlockSpec((tk, tm), lambda i, j, kk: (kk, i)) if mode == "tn"
              else pl.BlockSpec((tm, tk), lambda i, j, kk: (i, kk)))
    b_spec = (pl.BlockSpec((tn, tk), lambda i, j, kk: (j, kk)) if mode == "nt"
              else pl.BlockSpec((tk, tn), lambda i, j, kk: (kk, j)))
    in_specs = [a_spec, b_spec]
    args = [a, b]
    if add is not None:
        in_specs.append(pl.BlockSpec((tm, tn), lambda i, j, kk: (i, j)))
        args.append(add)
    return pl.pallas_call(
        body, name=name, grid=(m // tm, n // tn, nk),
        in_specs=in_specs, out_specs=pl.BlockSpec((tm, tn), lambda i, j, kk: (i, j)),
        out_shape=jax.ShapeDtypeStruct((m, n), out_dtype),
        scratch_shapes=[pltpu.VMEM((tm, tn), F32)],
        compiler_params=_cp("parallel", "parallel", "arbitrary"),
    )(*args)


def _matmul_nt_segments(segs, b, out_dtype, tm, tk, name):
    m = segs[0].shape[0]
    n, ktot = b.shape
    nks = [s.shape[1] // tk for s in segs]
    assert all(s.shape[1] % tk == 0 for s in segs) and sum(s.shape[1] for s in segs) == ktot and m % tm == 0
    starts = [sum(nks[:i]) for i in range(len(segs))]
    nk = sum(nks)

    def body(*refs):
        seg_refs, b_ref, o_ref, acc = refs[:len(segs)], refs[len(segs)], refs[len(segs) + 1], refs[len(segs) + 2]
        kk = pl.program_id(1)

        @pl.when(kk == 0)
        def _():
            acc[...] = jnp.zeros_like(acc)

        for a_ref, k0, nk_s in zip(seg_refs, starts, nks):
            @pl.when(jnp.logical_and(kk >= k0, kk < k0 + nk_s))
            def _(a_ref=a_ref):
                acc[...] += _bdot(a_ref[...], b_ref[...], NT)

        @pl.when(kk == nk - 1)
        def _():
            o_ref[...] = acc[...].astype(out_dtype)

    def seg_spec(k0, nk_s):
        return pl.BlockSpec((tm, tk), lambda i, kk: (i, jnp.clip(kk - k0, 0, nk_s - 1)))

    return pl.pallas_call(
        body, name=name, grid=(m // tm, nk),
        in_specs=[seg_spec(k0, nk_s) for k0, nk_s in zip(starts, nks)]
        + [pl.BlockSpec((n, tk), lambda i, kk: (0, kk))],
        out_specs=pl.BlockSpec((tm, n), lambda i, kk: (i, 0)),
        out_shape=jax.ShapeDtypeStruct((m, n), out_dtype),
        scratch_shapes=[pltpu.VMEM((tm, n), F32)],
        compiler_params=_cp("parallel", "arbitrary"),
    )(*segs, b)


def _row_spec(tm, cols, cb=0):
    return pl.BlockSpec((tm, cols), lambda i, cb=cb: (i, cb))


def _bcast_spec(rows, cols):
    return pl.BlockSpec((rows, cols), lambda i: (0, 0))


def _rms_fwd(x, w, tm=512):
    t, d = x.shape

    def body(x_ref, w_ref, o_ref):
        xv = x_ref[...]
        r = lax.rsqrt(jnp.mean(xv * xv, axis=-1, keepdims=True) + EPS)
        o_ref[...] = (xv * r * w_ref[...]).astype(BF16)

    return pl.pallas_call(
        body, name="rms_fwd", grid=(t // tm,),
        in_specs=[_row_spec(tm, d), _bcast_spec(1, d)], out_specs=_row_spec(tm, d),
        out_shape=jax.ShapeDtypeStruct((t, d), BF16), compiler_params=_cp("parallel"),
    )(x, w)


def _rms_bwd(x, w, dy, resid, tm=512):
    t, d = x.shape

    def body(x_ref, w_ref, dy_ref, res_ref, dx_ref, dw_ref):
        xv = x_ref[...]
        r = lax.rsqrt(jnp.mean(xv * xv, axis=-1, keepdims=True) + EPS)
        xh = xv * r
        dyv = dy_ref[...].astype(F32)
        dxh = dyv * w_ref[...]
        dx_ref[...] = res_ref[...] + r * (dxh - xh * jnp.mean(dxh * xh, axis=-1, keepdims=True))

        @pl.when(pl.program_id(0) == 0)
        def _():
            dw_ref[...] = jnp.zeros_like(dw_ref)

        dw_ref[...] += jnp.sum(dyv * xh, axis=0, keepdims=True)

    return pl.pallas_call(
        body, name="rms_bwd", grid=(t // tm,),
        in_specs=[_row_spec(tm, d), _bcast_spec(1, d), _row_spec(tm, d), _row_spec(tm, d)],
        out_specs=[_row_spec(tm, d), _bcast_spec(1, d)],
        out_shape=[jax.ShapeDtypeStruct((t, d), F32), jax.ShapeDtypeStruct((1, d), F32)],
        compiler_params=_cp("arbitrary"),
    )(x, w, dy, resid)


def _final_loss(h, w, target, tm=512):
    t, d = h.shape

    def body(h_ref, w_ref, t_ref, loss_ref, dh_ref, dw_ref):
        hv = h_ref[...]
        r = lax.rsqrt(jnp.mean(hv * hv, axis=-1, keepdims=True) + EPS)
        xh = hv * r
        wv = w_ref[...]
        err = xh * wv - t_ref[...]
        dy = err * (1.0 / d)
        dxh = dy * wv
        dh_ref[...] = r * (dxh - xh * jnp.mean(dxh * xh, axis=-1, keepdims=True))

        @pl.when(pl.program_id(0) == 0)
        def _():
            dw_ref[...] = jnp.zeros_like(dw_ref)
            loss_ref[...] = jnp.zeros_like(loss_ref)

        dw_ref[...] += jnp.sum(dy * xh, axis=0, keepdims=True)
        part = 0.5 * jnp.sum(jnp.mean(err * err, axis=-1, keepdims=True), axis=0, keepdims=True)
        loss_ref[...] += part + jnp.zeros((1, HEAD), F32)

    return pl.pallas_call(
        body, name="final_loss", grid=(t // tm,),
        in_specs=[_row_spec(tm, d), _bcast_spec(1, d), _row_spec(tm, d)],
        out_specs=[_bcast_spec(1, HEAD), _row_spec(tm, d), _bcast_spec(1, d)],
        out_shape=[jax.ShapeDtypeStruct((1, HEAD), F32), jax.ShapeDtypeStruct((t, d), F32),
                   jax.ShapeDtypeStruct((1, d), F32)],
        compiler_params=_cp("arbitrary"),
    )(h, w, target)


def _swiglu_fwd(gu, tm=256):
    t = gu.shape[0]
    ff = gu.shape[1] // 2

    def body(gu_ref, o_ref):
        g = gu_ref[:, :ff].astype(F32)
        o_ref[...] = (g * _sigmoid(g) * gu_ref[:, ff:].astype(F32)).astype(BF16)

    return pl.pallas_call(
        body, name="swiglu_fwd", grid=(t // tm,),
        in_specs=[_row_spec(tm, 2 * ff)], out_specs=_row_spec(tm, ff),
        out_shape=jax.ShapeDtypeStruct((t, ff), BF16), compiler_params=_cp("parallel"),
    )(gu)


def _swiglu_bwd(gu, dact, tm=256):
    t = gu.shape[0]
    ff = gu.shape[1] // 2

    def body(gu_ref, d_ref, o_ref):
        g = gu_ref[:, :ff].astype(F32)
        u = gu_ref[:, ff:].astype(F32)
        dv = d_ref[...].astype(F32)
        sg = _sigmoid(g)
        o_ref[:, :ff] = (dv * u * (sg + g * sg * (1.0 - sg))).astype(BF16)
        o_ref[:, ff:] = (dv * g * sg).astype(BF16)

    return pl.pallas_call(
        body, name="swiglu_bwd", grid=(t // tm,),
        in_specs=[_row_spec(tm, 2 * ff), _row_spec(tm, ff)], out_specs=_row_spec(tm, 2 * ff),
        out_shape=jax.ShapeDtypeStruct((t, 2 * ff), BF16), compiler_params=_cp("parallel"),
    )(gu, dact)


def _gate_merge_fwd(pb, ya, yb, tm=512):
    t, d = ya.shape
    cb = PB_GATE // d

    def body(ga_ref, gb_ref, ya_ref, yb_ref, o_ref):
        o_ref[...] = (_sigmoid(ga_ref[...]) * ya_ref[...] + _sigmoid(gb_ref[...]) * yb_ref[...]).astype(BF16)

    return pl.pallas_call(
        body, name="gate_merge_fwd", grid=(t // tm,),
        in_specs=[_row_spec(tm, d, cb), _row_spec(tm, d, cb + 1), _row_spec(tm, d), _row_spec(tm, d)],
        out_specs=_row_spec(tm, d),
        out_shape=jax.ShapeDtypeStruct((t, d), BF16), compiler_params=_cp("parallel"),
    )(pb, pb, ya, yb)


def _gate_merge_bwd(pb, ya, yb, dm, tm=512):
    t, d = ya.shape
    cb = PB_GATE // d

    def body(ga_ref, gb_ref, ya_ref, yb_ref, dm_ref, dya_ref, dyb_ref, dga_ref, dgb_ref):
        dmv = dm_ref[...].astype(F32)
        sa = _sigmoid(ga_ref[...])
        sb = _sigmoid(gb_ref[...])
        dya_ref[...] = (dmv * sa).astype(BF16)
        dyb_ref[...] = (dmv * sb).astype(BF16)
        dga_ref[...] = (dmv * ya_ref[...] * sa * (1.0 - sa)).astype(BF16)
        dgb_ref[...] = (dmv * yb_ref[...] * sb * (1.0 - sb)).astype(BF16)

    return pl.pallas_call(
        body, name="gate_merge_bwd", grid=(t // tm,),
        in_specs=[_row_spec(tm, d, cb), _row_spec(tm, d, cb + 1), _row_spec(tm, d), _row_spec(tm, d),
                  _row_spec(tm, d)],
        out_specs=[_row_spec(tm, d)] * 4,
        out_shape=[jax.ShapeDtypeStruct((t, d), BF16)] * 4, compiler_params=_cp("parallel"),
    )(pb, pb, ya, yb, dm)


def _head_norm_fwd(o, pb, wn, tm=512):
    t, d = o.shape
    nh = d // HEAD

    def body(o_ref, z_ref, w_ref, out_ref):
        wv = w_ref[...]
        for h in range(nh):
            sl = slice(h * HEAD, (h + 1) * HEAD)
            ov = o_ref[:, sl]
            zv = z_ref[:, sl]
            r = lax.rsqrt(jnp.mean(ov * ov, axis=-1, keepdims=True) + EPS)
            out_ref[:, sl] = (ov * r * wv * (zv * _sigmoid(zv))).astype(BF16)

    return pl.pallas_call(
        body, name="head_norm_fwd", grid=(t // tm,),
        in_specs=[_row_spec(tm, d), _row_spec(tm, d, PB_Z // d), _bcast_spec(1, HEAD)],
        out_specs=_row_spec(tm, d),
        out_shape=jax.ShapeDtypeStruct((t, d), BF16), compiler_params=_cp("parallel"),
    )(o, pb, wn)


def _head_norm_bwd(o, pb, wn, dout, tm=512):
    t, d = o.shape
    nh = d // HEAD

    def body(o_ref, z_ref, w_ref, d_ref, do_ref, dz_ref, dw_ref):
        wv = w_ref[...]
        dw_acc = jnp.zeros((1, HEAD), F32)
        for h in range(nh):
            sl = slice(h * HEAD, (h + 1) * HEAD)
            ov = o_ref[:, sl]
            zv = z_ref[:, sl]
            dv = d_ref[:, sl].astype(F32)
            r = lax.rsqrt(jnp.mean(ov * ov, axis=-1, keepdims=True) + EPS)
            xh = ov * r
            sz = _sigmoid(zv)
            dn = dv * (zv * sz)
            dz_ref[:, sl] = (dv * xh * wv * (sz + zv * sz * (1.0 - sz))).astype(BF16)
            dxh = dn * wv
            do_ref[:, sl] = r * (dxh - xh * jnp.mean(dxh * xh, axis=-1, keepdims=True))
            dw_acc = dw_acc + jnp.sum(dn * xh, axis=0, keepdims=True)

        @pl.when(pl.program_id(0) == 0)
        def _():
            dw_ref[...] = jnp.zeros_like(dw_ref)

        dw_ref[...] += dw_acc

    return pl.pallas_call(
        body, name="head_norm_bwd", grid=(t // tm,),
        in_specs=[_row_spec(tm, d), _row_spec(tm, d, PB_Z // d), _bcast_spec(1, HEAD), _row_spec(tm, d)],
        out_specs=[_row_spec(tm, d), _row_spec(tm, d), _bcast_spec(1, HEAD)],
        out_shape=[jax.ShapeDtypeStruct((t, d), F32), jax.ShapeDtypeStruct((t, d), BF16),
                   jax.ShapeDtypeStruct((1, HEAD), F32)],
        compiler_params=_cp("arbitrary"),
    )(o, pb, wn, dout)


def _attn_bias(gi, hh, dil):
    i = lax.broadcasted_iota(jnp.int32, (BLOCK_A, BLOCK_A), 0)
    j = lax.broadcasted_iota(jnp.int32, (BLOCK_A, BLOCK_A), 1)
    hf = (gi * HEADS_PER_GROUP + hh + 1).astype(F32)
    slope = jnp.exp(jnp.full((1, BLOCK_A), -8.0 * math.log(2.0) / N_HEADS_A, F32) * hf) * float(dil)
    d_prev = (BLOCK_A + i - j).astype(F32)
    d_cur = (i - j).astype(F32)
    return -slope * d_prev, -slope * d_cur, j >= i, j <= i


ATTN_TOKENS = 2048


def _sub_rows(a, r, dil):
    start = a * BLOCK_A * dil + r
    return pl.ds(start, BLOCK_A) if dil == 1 else pl.ds(start, BLOCK_A, stride=dil)


def _attn_fwd_group(pa, gi, dil, tb=ATTN_TOKENS):
    t = pa.shape[0]
    tb = min(tb, t)
    hb = BLOCK_A * dil
    nb = tb // hb
    scale = HEAD ** -0.5

    def body(q_ref, k_ref, v_ref, kp_ref, vp_ref, o_ref, l_ref):
        hh = pl.program_id(0)
        step = pl.program_id(1)
        b_prev, b_cur, m_prev, m_cur = _attn_bias(gi, hh, dil)
        m_first = jnp.logical_and(m_prev, step > 0)
        for r in range(dil):
            kp, vp = kp_ref[_sub_rows(0, r, dil), :], vp_ref[_sub_rows(0, r, dil), :]
            for a in range(nb):
                rows = _sub_rows(a, r, dil)
                q, kc, vc = q_ref[rows, :], k_ref[rows, :], v_ref[rows, :]
                s_p = jnp.where(m_first if a == 0 else m_prev, _bdot(q, kp, NT) * scale + b_prev, -1e30)
                s_c = jnp.where(m_cur, _bdot(q, kc, NT) * scale + b_cur, -1e30)
                m = jnp.maximum(jnp.max(s_p, axis=-1, keepdims=True), jnp.max(s_c, axis=-1, keepdims=True))
                p_p = jnp.exp(s_p - m)
                p_c = jnp.exp(s_c - m)
                den = _rowsum(p_p) + _rowsum(p_c)
                o_ref[rows, :] = (_bdot(p_p, vp) + _bdot(p_c, vc)) / den
                l_ref[rows, :] = (m + jnp.log(den)) + jnp.zeros((BLOCK_A, HEAD), F32)
                kp, vp = kc, vc

    def col(base):
        return lambda hh, s: (s, base + hh)

    def col_prev(base):
        return lambda hh, s: (jnp.maximum(s * nb - 1, 0), base + hh)

    qb, kb, vb = gi * HEADS_PER_GROUP, N_HEADS_A + gi * HEADS_PER_GROUP, 2 * N_HEADS_A + gi * HEADS_PER_GROUP
    ospec = pl.BlockSpec((tb, HEAD), lambda hh, s: (s, hh))
    return pl.pallas_call(
        body, name=f"attn_fwd_g{gi}", grid=(HEADS_PER_GROUP, t // tb),
        in_specs=[pl.BlockSpec((tb, HEAD), col(qb)), pl.BlockSpec((tb, HEAD), col(kb)),
                  pl.BlockSpec((tb, HEAD), col(vb)),
                  pl.BlockSpec((hb, HEAD), col_prev(kb)), pl.BlockSpec((hb, HEAD), col_prev(vb))],
        out_specs=[ospec, ospec],
        out_shape=[jax.ShapeDtypeStruct((t, D_ATTN_OUT), F32)] * 2,
        compiler_params=_cp("parallel", "parallel"),
    )(pa, pa, pa, pa, pa)


def _attn_merge(os, ls, tm=512):
    t, d = os[0].shape

    def body(o0, o1, o2, l0, l1, l2, y_ref, lse_ref):
        a0, a1, a2 = l0[...], l1[...], l2[...]
        m = jnp.maximum(jnp.maximum(a0, a1), a2)
        e0, e1, e2 = jnp.exp(a0 - m), jnp.exp(a1 - m), jnp.exp(a2 - m)
        den = e0 + e1 + e2
        y_ref[...] = (e0 * o0[...] + e1 * o1[...] + e2 * o2[...]) / den
        lse_ref[...] = m + jnp.log(den)

    return pl.pallas_call(
        body, name="attn_merge", grid=(t // tm,),
        in_specs=[_row_spec(tm, d)] * 6, out_specs=[_row_spec(tm, d)] * 2,
        out_shape=[jax.ShapeDtypeStruct((t, d), F32)] * 2,
        compiler_params=_cp("parallel"),
    )(*os, *ls)


def _attn_bwd_group(pa, dy, y, lse, gi, dil, tb=ATTN_TOKENS):
    t = pa.shape[0]
    tb = min(tb, t)
    hb = BLOCK_A * dil
    nb = tb // hb
    nsteps = t // tb
    scale = HEAD ** -0.5

    def body(q_ref, k_ref, v_ref, dy_ref, y_ref, l_ref, kp_ref, vp_ref, d_ref,
             dq_s, dk_s, dv_s, carry_k, carry_v):
        hh = pl.program_id(0)
        step = pl.program_id(1)

        @pl.when(step == 0)
        def _():
            carry_k[...] = jnp.zeros_like(carry_k)
            carry_v[...] = jnp.zeros_like(carry_v)

        b_prev, b_cur, m_prev, m_cur = _attn_bias(gi, hh, dil)
        m_first = jnp.logical_and(m_prev, step < nsteps - 1)
        for r in range(dil):
            halo = _sub_rows(0, r, dil)
            dk_in, dv_in = carry_k[halo, :], carry_v[halo, :]
            kp, vp = kp_ref[halo, :], vp_ref[halo, :]
            prev_rows = None
            dk_pend = dv_pend = None
            for a in range(nb):
                rows = _sub_rows(a, r, dil)
                q, kc, vc = q_ref[rows, :], k_ref[rows, :], v_ref[rows, :]
                dyb, lb = dy_ref[rows, :], l_ref[rows, :]
                delta = _rowsum(dyb * y_ref[rows, :])
                mp = m_first if a == 0 else m_prev
                s = _bdot(q, kp, NT) * scale + b_prev
                p = jnp.where(mp, jnp.exp(jnp.where(mp, s - lb, 0.0)), 0.0)
                ds = p * (_bdot(dyb, vp, NT) - delta)
                dq = _bdot(ds, kp)
                dk_prev, dv_prev = _bdot(ds, q, TN), _bdot(p, dyb, TN)
                if a == 0:
                    carry_k[halo, :] = dk_prev
                    carry_v[halo, :] = dv_prev
                else:
                    dk_s[prev_rows, :] = dk_pend + dk_prev
                    dv_s[prev_rows, :] = dv_pend + dv_prev
                s = _bdot(q, kc, NT) * scale + b_cur
                p = jnp.where(m_cur, jnp.exp(jnp.where(m_cur, s - lb, 0.0)), 0.0)
                ds = p * (_bdot(dyb, vc, NT) - delta)
                dq_s[rows, :] = dq + _bdot(ds, kc)
                dk_pend, dv_pend = _bdot(ds, q, TN), _bdot(p, dyb, TN)
                prev_rows, kp, vp = rows, kc, vc
            dk_s[prev_rows, :] = dk_pend + dk_in
            dv_s[prev_rows, :] = dv_pend + dv_in
        d_ref[:, :HEAD] = (dq_s[...] * scale).astype(BF16)
        d_ref[:, HEAD:2 * HEAD] = (dk_s[...] * scale).astype(BF16)
        d_ref[:, 2 * HEAD:] = dv_s[...].astype(BF16)

    def col(base):
        return lambda hh, s: (nsteps - 1 - s, base + hh)

    def col_prev(base):
        return lambda hh, s: (jnp.maximum((nsteps - 1 - s) * nb - 1, 0), base + hh)

    qb, kb, vb = gi * HEADS_PER_GROUP, N_HEADS_A + gi * HEADS_PER_GROUP, 2 * N_HEADS_A + gi * HEADS_PER_GROUP
    big, small = (tb, HEAD), (hb, HEAD)
    return pl.pallas_call(
        body, name=f"attn_bwd_g{gi}", grid=(HEADS_PER_GROUP, nsteps),
        in_specs=[pl.BlockSpec(big, col(qb)), pl.BlockSpec(big, col(kb)), pl.BlockSpec(big, col(vb)),
                  pl.BlockSpec(big, col(0)), pl.BlockSpec(big, col(0)), pl.BlockSpec(big, col(0)),
                  pl.BlockSpec(small, col_prev(kb)), pl.BlockSpec(small, col_prev(vb))],
        out_specs=pl.BlockSpec((tb, 3 * HEAD), col(0)),
        out_shape=jax.ShapeDtypeStruct((t, 3 * D_ATTN_OUT), BF16),
        scratch_shapes=[pltpu.VMEM(big, F32)] * 3 + [pltpu.VMEM(small, F32)] * 2,
        compiler_params=_cp("parallel", "arbitrary"),
    )(pa, pa, pa, dy, y, lse, pa, pa)


def _shift_down(cur, prev8, s):
    if s == 0:
        return cur
    rolled = pltpu.roll(cur, s, 0)
    prolled = pltpu.roll(prev8, s, 0)
    rid = lax.broadcasted_iota(jnp.int32, prev8.shape, 0)
    top = jnp.where(rid < s, prolled, rolled[:8])
    return jnp.concatenate([top, rolled[8:]], axis=0)


def _shift_up(cur, next8, s):
    if s == 0:
        return cur
    n = cur.shape[0]
    rolled = pltpu.roll(cur, n - s, 0)
    nrolled = pltpu.roll(next8, 8 - s, 0)
    rid = lax.broadcasted_iota(jnp.int32, next8.shape, 0)
    bottom = jnp.where(rid >= 8 - s, nrolled, rolled[n - 8:])
    return jnp.concatenate([rolled[:n - 8], bottom], axis=0)


def _conv(xv, prev8, wv):
    c = jnp.zeros_like(xv)
    shifted = []
    for s in range(CONV_WIDTH):
        xs = _shift_down(xv, prev8, s)
        shifted.append(xs)
        c = c + wv[CONV_WIDTH - 1 - s:CONV_WIDTH - s, :] * xs
    return c, shifted


def _dn_prep_fwd(pb, conv_w, a_log_bc, dt_bias_bc, tm=256):
    t = pb.shape[0]
    c3 = 3 * D_B
    r8 = tm // 8

    def body(x_ref, xp_ref, b_ref, a_ref, w_ref, al_ref, dt_ref, q_ref, k_ref, v_ref, g_ref, beta_ref):
        prev8 = jnp.where(pl.program_id(0) > 0, xp_ref[...], 0.0)
        c, _ = _conv(x_ref[...], prev8, w_ref[...])
        s = c * _sigmoid(c)
        for h in range(N_HEADS_B):
            sl = slice(h * HEAD, (h + 1) * HEAD)
            sq = s[:, h * HEAD:(h + 1) * HEAD]
            q_ref[:, sl] = sq * lax.rsqrt(_rowsum(sq * sq) + EPS) * (HEAD ** -0.5)
            sk = s[:, D_B + h * HEAD:D_B + (h + 1) * HEAD]
            k_ref[:, sl] = sk * lax.rsqrt(_rowsum(sk * sk) + EPS)
        v_ref[...] = s[:, 2 * D_B:]
        beta_ref[...] = _sigmoid(b_ref[...])
        g_ref[...] = -jnp.exp(al_ref[...]) * _softplus(a_ref[...] + dt_ref[...])

    return pl.pallas_call(
        body, name="dn_prep_fwd", grid=(t // tm,),
        in_specs=[_row_spec(tm, c3, 0),
                  pl.BlockSpec((8, c3), lambda i: (jnp.maximum(i * r8 - 1, 0), 0)),
                  _row_spec(tm, D_B, PB_BETA // D_B), _row_spec(tm, D_B, PB_A // D_B),
                  _bcast_spec(CONV_WIDTH, c3), _bcast_spec(1, D_B), _bcast_spec(1, D_B)],
        out_specs=[_row_spec(tm, D_B)] * 5,
        out_shape=[jax.ShapeDtypeStruct((t, D_B), F32)] * 5,
        compiler_params=_cp("parallel"),
    )(pb, pb, pb, pb, conv_w, a_log_bc, dt_bias_bc)


def _dn_prep_bwd(pb, conv_w, a_log_bc, dt_bias_bc, g, dq, dk, dv, dg, dbeta, tm=128):
    t = pb.shape[0]
    c3 = 3 * D_B
    r8 = tm // 8

    def body(x_ref, xp_ref, b_ref, a_ref, w_ref, al_ref, dt_ref, g_ref, dq_ref, dk_ref, dv_ref, dg_ref, db_ref,
             dc_ref, dbr_ref, dar_ref, dw_ref, dal_ref, ddt_ref):
        prev8 = jnp.where(pl.program_id(0) > 0, xp_ref[...], 0.0)
        c, shifted = _conv(x_ref[...], prev8, w_ref[...])
        sg = _sigmoid(c)
        s = c * sg
        dsilu = sg + c * sg * (1.0 - sg)
        for h in range(N_HEADS_B):
            sl = slice(h * HEAD, (h + 1) * HEAD)
            for base, d_ref, mult in ((0, dq_ref, HEAD ** -0.5), (D_B, dk_ref, 1.0)):
                ssl = slice(base + h * HEAD, base + (h + 1) * HEAD)
                sv = s[:, ssl]
                r = lax.rsqrt(_rowsum(sv * sv) + EPS)
                yh = sv * r
                dyv = d_ref[:, sl] * mult
                dc_ref[:, ssl] = r * (dyv - yh * _rowsum(dyv * yh)) * dsilu[:, ssl]
        dc_ref[:, 2 * D_B:] = dv_ref[...] * dsilu[:, 2 * D_B:]
        dcv = dc_ref[...]

        @pl.when(pl.program_id(0) == 0)
        def _():
            dw_ref[...] = jnp.zeros_like(dw_ref)
            dal_ref[...] = jnp.zeros_like(dal_ref)
            ddt_ref[...] = jnp.zeros_like(ddt_ref)

        for sft in range(CONV_WIDTH):
            j = CONV_WIDTH - 1 - sft
            dw_ref[j:j + 1, :] += jnp.sum(dcv * shifted[sft], axis=0, keepdims=True)
        beta = _sigmoid(b_ref[...])
        dbr_ref[...] = (db_ref[...] * beta * (1.0 - beta) * (1.0 / HEAD)).astype(BF16)
        dgv = dg_ref[...]
        da = dgv * (-jnp.exp(al_ref[...])) * _sigmoid(a_ref[...] + dt_ref[...])
        dar_ref[...] = (da * (1.0 / HEAD)).astype(BF16)
        dal_ref[...] += jnp.sum(dgv * g_ref[...], axis=0, keepdims=True)
        ddt_ref[...] += jnp.sum(da, axis=0, keepdims=True)

    row = _row_spec(tm, D_B)
    return pl.pallas_call(
        body, name="dn_prep_bwd", grid=(t // tm,),
        in_specs=[_row_spec(tm, c3, 0),
                  pl.BlockSpec((8, c3), lambda i: (jnp.maximum(i * r8 - 1, 0), 0)),
                  _row_spec(tm, D_B, PB_BETA // D_B), _row_spec(tm, D_B, PB_A // D_B),
                  _bcast_spec(CONV_WIDTH, c3), _bcast_spec(1, D_B), _bcast_spec(1, D_B),
                  row, row, row, row, row, row],
        out_specs=[_row_spec(tm, c3), row, row, _bcast_spec(CONV_WIDTH, c3), _bcast_spec(1, D_B),
                   _bcast_spec(1, D_B)],
        out_shape=[jax.ShapeDtypeStruct((t, c3), F32), jax.ShapeDtypeStruct((t, D_B), BF16),
                   jax.ShapeDtypeStruct((t, D_B), BF16), jax.ShapeDtypeStruct((CONV_WIDTH, c3), F32),
                   jax.ShapeDtypeStruct((1, D_B), F32), jax.ShapeDtypeStruct((1, D_B), F32)],
        compiler_params=_cp("arbitrary"),
    )(pb, pb, pb, pb, conv_w, a_log_bc, dt_bias_bc, g, dq, dk, dv, dg, dbeta)


def _conv_bwd_input(dc, conv_w, tm=256):
    t, c3 = dc.shape
    r8 = tm // 8
    nlast = t // 8 - 1
    nsteps = t // tm

    def body(d_ref, dn_ref, w_ref, o_ref):
        next8 = jnp.where(pl.program_id(0) < nsteps - 1, dn_ref[...], 0.0)
        dv = d_ref[...]
        wv = w_ref[...]
        acc = jnp.zeros_like(dv)
        for s in range(CONV_WIDTH):
            acc = acc + wv[CONV_WIDTH - 1 - s:CONV_WIDTH - s, :] * _shift_up(dv, next8, s)
        o_ref[...] = acc.astype(BF16)

    return pl.pallas_call(
        body, name="conv_bwd_input", grid=(nsteps,),
        in_specs=[_row_spec(tm, c3), pl.BlockSpec((8, c3), lambda i: (jnp.minimum((i + 1) * r8, nlast), 0)),
                  _bcast_spec(CONV_WIDTH, c3)],
        out_specs=_row_spec(tm, c3),
        out_shape=jax.ShapeDtypeStruct((t, c3), BF16), compiler_params=_cp("parallel"),
    )(dc, dc, conv_w)


def _lanes(x):
    return x[:, :CH]


def _tri_inv(a_list, r, c):
    eye = (r == c).astype(F32)
    b16 = (r >> 4) == (c >> 4)
    b32 = (r >> 5) == (c >> 5)
    ns = [jnp.where(b16, -a, 0.0) for a in a_list]
    xs = [eye + n for n in ns]
    ps = [_hdot(n, n) for n in ns]
    for last in (False, False, True):
        xs = [x + _hdot(x, p) for x, p in zip(xs, ps)]
        if not last:
            ps = [_hdot(p, p) for p in ps]
    for mask in (jnp.logical_and(b32, jnp.logical_not(b16)), jnp.logical_not(b32)):
        ts = [_hdot(x, jnp.where(mask, a, 0.0)) for x, a in zip(xs, a_list)]
        xs = [x - _hdot(t, x) for x, t in zip(xs, ts)]
    return xs


def _chunk_local(qs, ks, vs, gs, betas):
    r = lax.broadcasted_iota(jnp.int32, (CH, CH), 0)
    c = lax.broadcasted_iota(jnp.int32, (CH, CH), 1)
    incl, strict = r >= c, r > c
    lm = incl.astype(F32)
    cums = [_hdot(lm, jnp.concatenate([g, jnp.where(strict, _lanes(g), 0.0)], axis=1)) for g in gs]
    gcbs = [cm[:, :HEAD] for cm in cums]
    decays = [jnp.where(incl, jnp.exp(jnp.where(incl, cm[:, HEAD:], 0.0)), 0.0) for cm in cums]
    bcols = [_lanes(b) for b in betas]
    kks = [_bdot(k, k, NT) for k in ks]
    qkraws = [_bdot(q, k, NT) for q, k in zip(qs, ks)]
    tms = _tri_inv([jnp.where(strict, bc * kk * dc, 0.0) for bc, kk, dc in zip(bcols, kks, decays)], r, c)
    egs = [jnp.exp(gcb) for gcb in gcbs]
    sols = [_hdot(tm, jnp.concatenate([b * v, b * eg * k], axis=1))
            for tm, b, v, eg, k in zip(tms, betas, vs, egs, ks)]
    gls = [gcb[CH - 1:CH, :] for gcb in gcbs]
    eks = [jnp.exp(gl - gcb) for gl, gcb in zip(gls, gcbs)]
    return [dict(incl=incl, strict=strict, r=r, c=c, decay=dc, bcol=bc, kk=kk, tm=tm, eg=eg,
                 u_bar=sol[:, :HEAD], w=sol[:, HEAD:], qkraw=qkraw, gl=gl, ek=ek)
            for dc, bc, kk, tm, eg, sol, qkraw, gl, ek in zip(decays, bcols, kks, tms, egs, sols, qkraws, gls, eks)]


def _dn_chunk_fwd(q, k, v, g, beta, cps=8):
    t = q.shape[0]
    tm = cps * CH

    def body(q_ref, k_ref, v_ref, g_ref, b_ref, ub_ref, w_ref, qd_ref, kd_ref, qk_ref, gl_ref):
        for base in range(0, cps, CHUNK_GROUP):
            sls = [slice((base + j) * CH, (base + j + 1) * CH) for j in range(CHUNK_GROUP)]
            qs, ks = [q_ref[sl, :] for sl in sls], [k_ref[sl, :] for sl in sls]
            locs = _chunk_local(qs, ks, [v_ref[sl, :] for sl in sls], [g_ref[sl, :] for sl in sls],
                                [b_ref[sl, :] for sl in sls])
            for j, (sl, qv, kv, loc) in enumerate(zip(sls, qs, ks, locs)):
                ub_ref[sl, :] = loc["u_bar"]
                w_ref[sl, :] = loc["w"]
                qd_ref[sl, :] = qv * loc["eg"]
                kd_ref[sl, :] = kv * loc["ek"]
                qk_ref[sl, :] = loc["qkraw"] * loc["decay"]
                gl_ref[base + j:base + j + 1, :] = jnp.exp(loc["gl"])

    hspec = pl.BlockSpec((tm, HEAD), lambda h, i: (i, h))
    return pl.pallas_call(
        body, name="dn_chunk_fwd", grid=(N_HEADS_B, t // tm),
        in_specs=[hspec] * 5,
        out_specs=[hspec] * 4 + [pl.BlockSpec((None, tm, CH), lambda h, i: (h, i, 0)),
                                 pl.BlockSpec((cps, HEAD), lambda h, i: (i, h))],
        out_shape=[jax.ShapeDtypeStruct((t, D_B), F32)] * 4
        + [jax.ShapeDtypeStruct((N_HEADS_B, t, CH), F32), jax.ShapeDtypeStruct((t // CH, D_B), F32)],
        compiler_params=_cp("parallel", "parallel"),
    )(q, k, v, g, beta)


def _dn_scan_fwd(ub, w, qd, kd, qk, gl, cps=8):
    t = ub.shape[0]
    tm = cps * CH

    hg = SCAN_HEADS
    hs = list(range(hg))

    def body(ub_ref, w_ref, qd_ref, kd_ref, qk_ref, gl_ref, o_ref, st_ref, s_acc):
        @pl.when(pl.program_id(1) == 0)
        def _():
            s_acc[...] = jnp.zeros_like(s_acc)

        for ci in range(cps):
            sl = slice(ci * CH, (ci + 1) * CH)
            cols = [slice(h * HEAD, (h + 1) * HEAD) for h in hs]
            svs = [s_acc[h] for h in hs]
            for h in hs:
                st_ref[h, ci * HEAD:(ci + 1) * HEAD, :] = svs[h]
            us = [ub_ref[sl, cols[h]] - _bdot(w_ref[sl, cols[h]], svs[h]) for h in hs]
            for h in hs:
                s_acc[h] = gl_ref[ci:ci + 1, cols[h]] * svs[h] + _bdot(kd_ref[sl, cols[h]], us[h], TN)
            for h in hs:
                o_ref[sl, cols[h]] = _bdot(qd_ref[sl, cols[h]], svs[h]) + _bdot(qk_ref[h, sl, :], us[h])

    hspec = pl.BlockSpec((tm, hg * HEAD), lambda h, i: (i, h))
    return pl.pallas_call(
        body, name="dn_scan_fwd", grid=(N_HEADS_B // hg, t // tm),
        in_specs=[hspec] * 4 + [pl.BlockSpec((hg, tm, CH), lambda h, i: (h, i, 0)),
                                pl.BlockSpec((cps, hg * HEAD), lambda h, i: (i, h))],
        out_specs=[hspec, pl.BlockSpec((hg, cps * HEAD, HEAD), lambda h, i: (h, i, 0))],
        out_shape=[jax.ShapeDtypeStruct((t, D_B), F32),
                   jax.ShapeDtypeStruct((N_HEADS_B, (t // CH) * HEAD, HEAD), F32)],
        scratch_shapes=[pltpu.VMEM((hg, HEAD, HEAD), F32)],
        compiler_params=_cp("parallel", "arbitrary"),
    )(ub, w, qd, kd, qk, gl)


def _dn_scan_bwd(ub, w, qd, kd, qk, gl, st, do, cps=8):
    t = ub.shape[0]
    tm = cps * CH
    ns = t // tm

    hg = SCAN_HEADS
    hs = list(range(hg))

    def body(ub_ref, w_ref, qd_ref, kd_ref, qk_ref, gl_ref, st_ref, do_ref,
             dub_ref, dw_ref, dqd_ref, dkd_ref, dqk_ref, dgl_ref, ds_acc):
        @pl.when(pl.program_id(1) == 0)
        def _():
            ds_acc[...] = jnp.zeros_like(ds_acc)

        for ci in reversed(range(cps)):
            sl = slice(ci * CH, (ci + 1) * CH)
            cols = [slice(h * HEAD, (h + 1) * HEAD) for h in hs]
            svs = [st_ref[h, ci * HEAD:(ci + 1) * HEAD, :] for h in hs]
            wvs = [w_ref[sl, cols[h]] for h in hs]
            dovs = [do_ref[sl, cols[h]] for h in hs]
            dsvs = [ds_acc[h] for h in hs]
            us = [ub_ref[sl, cols[h]] - _bdot(wvs[h], svs[h]) for h in hs]
            dus = [_bdot(kd_ref[sl, cols[h]], dsvs[h]) + _bdot(qk_ref[h, sl, :], dovs[h], TN) for h in hs]
            for h in hs:
                ds_acc[h] = (gl_ref[ci:ci + 1, cols[h]] * dsvs[h] + _bdot(qd_ref[sl, cols[h]], dovs[h], TN)
                             - _bdot(wvs[h], dus[h], TN))
            for h in hs:
                dgl_ref[ci:ci + 1, cols[h]] = (jnp.sum(_rowsum(dsvs[h] * svs[h]), axis=0, keepdims=True)
                                              + jnp.zeros((1, HEAD), F32))
                dkd_ref[sl, cols[h]] = _bdot(us[h], dsvs[h], NT)
                dqd_ref[sl, cols[h]] = _bdot(dovs[h], svs[h], NT)
                dqk_ref[h, sl, :] = _bdot(dovs[h], us[h], NT)
                dub_ref[sl, cols[h]] = dus[h]
                dw_ref[sl, cols[h]] = -_bdot(dus[h], svs[h], NT)

    hspec = pl.BlockSpec((tm, hg * HEAD), lambda h, i: (ns - 1 - i, h))
    qkspec = pl.BlockSpec((hg, tm, CH), lambda h, i: (h, ns - 1 - i, 0))
    glspec = pl.BlockSpec((cps, hg * HEAD), lambda h, i: (ns - 1 - i, h))
    return pl.pallas_call(
        body, name="dn_scan_bwd", grid=(N_HEADS_B // hg, ns),
        in_specs=[hspec] * 4 + [qkspec, glspec,
                                pl.BlockSpec((hg, cps * HEAD, HEAD), lambda h, i: (h, ns - 1 - i, 0)), hspec],
        out_specs=[hspec] * 4 + [qkspec, glspec],
        out_shape=[jax.ShapeDtypeStruct((t, D_B), F32)] * 4
        + [jax.ShapeDtypeStruct((N_HEADS_B, t, CH), F32), jax.ShapeDtypeStruct((t // CH, D_B), F32)],
        scratch_shapes=[pltpu.VMEM((hg, HEAD, HEAD), F32)],
        compiler_params=_cp("parallel", "arbitrary"),
    )(ub, w, qd, kd, qk, gl, st, do)


def _dn_chunk_bwd(q, k, v, g, beta, dub, dw, dqd, dkd, dqk, dgl, cps=8):
    t = q.shape[0]
    tm = cps * CH

    def body(q_ref, k_ref, v_ref, g_ref, b_ref, dub_ref, dw_ref, dqd_ref, dkd_ref, dqk_ref, dgl_ref,
             dq_ref, dk_ref, dv_ref, dg_ref, db_ref):
        ones = jnp.ones((CH, HEAD), F32)
        rid = lax.broadcasted_iota(jnp.int32, (CH, HEAD), 0)

        def rest(ci, sl, qv, kv, vv, beta_v, loc, dr, da):
            incl = loc["incl"]
            eg, ek, decay, bcol, kk = loc["eg"], loc["ek"], loc["decay"], loc["bcol"], loc["kk"]
            drv, drk = dr[:, :HEAD], dr[:, HEAD:]
            dv_ref[sl, :] = beta_v * drv
            beg = beta_v * eg
            t1 = drk * kv
            dbeta = _rowsum(drv * vv + t1 * eg) + _rowsum(da * kk * decay)
            dkk = da * bcol * decay
            dqk_m = jnp.where(incl, dqk_ref[sl, :], 0.0)
            ddecay = da * bcol * kk + dqk_m * loc["qkraw"]
            dqkraw = dqk_m * decay
            dqdv, dkdv = dqd_ref[sl, :], dkd_ref[sl, :]
            dq_ref[sl, :] = _bdot(dqkraw, kv) + dqdv * eg
            dk_ref[sl, :] = (beg * drk + _bdot(dqkraw, qv, TN) + _bdot(dkk, kv) + _bdot(dkk, kv, TN)
                             + dkdv * ek)
            e = ddecay * decay
            skd = _rowsum(dkdv * kv * ek)
            dgc = _rowsum(beg * t1) + _rowsum(e) + _rowsum(dqdv * qv * eg) - skd
            colsum = _hdot(e, ones, TN)
            last = jnp.sum(skd, axis=0, keepdims=True) + dgl_ref[ci:ci + 1, :] * jnp.exp(loc["gl"])
            db_ref[sl, :] = dbeta + jnp.zeros((CH, HEAD), F32)
            return (dgc - colsum) + jnp.where(rid == CH - 1, last, 0.0)

        for base in range(0, cps, CHUNK_GROUP):
            cis = list(range(base, base + CHUNK_GROUP))
            sls = [slice(ci * CH, (ci + 1) * CH) for ci in cis]
            qs, ks, vs = [q_ref[sl, :] for sl in sls], [k_ref[sl, :] for sl in sls], [v_ref[sl, :] for sl in sls]
            betas = [b_ref[sl, :] for sl in sls]
            locs = _chunk_local(qs, ks, vs, [g_ref[sl, :] for sl in sls], betas)
            drs = [_hdot(loc["tm"], jnp.concatenate([dub_ref[sl, :], dw_ref[sl, :]], axis=1), TN)
                   for loc, sl in zip(locs, sls)]
            das = [jnp.where(loc["strict"],
                             -_hdot(dr, jnp.concatenate([loc["u_bar"], loc["w"]], axis=1), NT), 0.0)
                   for loc, dr in zip(locs, drs)]
            dgcs = [rest(*args) for args in zip(cis, sls, qs, ks, vs, betas, locs, drs, das)]
            um = (locs[0]["r"] <= locs[0]["c"]).astype(F32)
            for sl, dgc_bc in zip(sls, dgcs):
                dg_ref[sl, :] = _hdot(um, dgc_bc)

    hspec = pl.BlockSpec((tm, HEAD), lambda h, i: (i, h))
    return pl.pallas_call(
        body, name="dn_chunk_bwd", grid=(N_HEADS_B, t // tm),
        in_specs=[hspec] * 9 + [pl.BlockSpec((None, tm, CH), lambda h, i: (h, i, 0)),
                                pl.BlockSpec((cps, HEAD), lambda h, i: (i, h))],
        out_specs=[hspec] * 5,
        out_shape=[jax.ShapeDtypeStruct((t, D_B), F32)] * 5,
        compiler_params=_cp("parallel", "parallel"),
    )(q, k, v, g, beta, dub, dw, dqd, dkd, dqk, dgl)


FLIPS = [(fx, fy, fc) for fx in (0, 1) for fy in (0, 1) for fc in (0, 1)][1:]


def _mesh_pos():
    return lax.axis_index("x"), lax.axis_index("y"), lax.axis_index("c")


def _peer(pos, flip):
    return tuple((1 - p) if f else p for p, f in zip(pos, flip))


def _dev_index(pos):
    return 4 * pos[0] + 2 * pos[1] + pos[2]


def _exchange(tensors, scatter, name):
    nt = len(tensors)
    hbm = pl.BlockSpec(memory_space=pltpu.HBM)

    def body(*refs):
        ins, outs = refs[:nt], refs[nt:2 * nt]
        send_sems, recv_sems, local_sems = refs[2 * nt:]
        pos = _mesh_pos()
        me = _dev_index(pos)

        def remote(ti, fi, landing):
            peer = _peer(pos, FLIPS[fi])
            src = ins[ti].at[_dev_index(peer)] if scatter[ti] else ins[ti]
            return pltpu.make_async_remote_copy(
                src_ref=src, dst_ref=outs[ti].at[landing(peer)],
                send_sem=send_sems.at[ti * 7 + fi], recv_sem=recv_sems.at[ti * 7 + fi],
                device_id=peer, device_id_type=pl.DeviceIdType.MESH)

        local = [pltpu.make_async_copy(ins[ti].at[me] if scatter[ti] else ins[ti], outs[ti].at[me],
                                       local_sems.at[ti]) for ti in range(nt)]
        for cp in local:
            cp.start()
        sends = [remote(ti, fi, lambda peer: me) for ti in range(nt) for fi in range(7)]
        for cp in sends:
            cp.start()
        for ti in range(nt):
            for fi in range(7):
                remote(ti, fi, _dev_index).wait_recv()
        for cp in sends:
            cp.wait_send()
        for cp in local:
            cp.wait()

    out_shape = [jax.ShapeDtypeStruct(x.shape if sc else (N_DEV,) + x.shape, x.dtype)
                 for x, sc in zip(tensors, scatter)]
    return pl.pallas_call(
        body, name=name, in_specs=[hbm] * nt, out_specs=[hbm] * nt, out_shape=out_shape,
        scratch_shapes=[pltpu.SemaphoreType.DMA((nt * 7,)), pltpu.SemaphoreType.DMA((nt * 7,)),
                        pltpu.SemaphoreType.DMA((nt,))],
        compiler_params=pltpu.CompilerParams(has_side_effects=True),
    )(*tensors)


def _adamw(land, w, m, v, name, tm=256):
    n, r, c = land.shape
    tm = r if r <= tm else max(s for s in range(8, tm + 1, 8) if r % s == 0)
    bc1 = 1.0 / (1.0 - ADAM_B1 ** ADAM_STEP)
    bc2 = 1.0 / (1.0 - ADAM_B2 ** ADAM_STEP)

    def body(l_ref, w_ref, m_ref, v_ref, g_ref, d_ref, nm_ref, nv_ref):
        g = l_ref[0].astype(F32)
        for i in range(1, n):
            g = g + l_ref[i].astype(F32)
        nm = ADAM_B1 * m_ref[...] + (1.0 - ADAM_B1) * g
        nv = ADAM_B2 * v_ref[...] + (1.0 - ADAM_B2) * (g * g)
        g_ref[...] = g
        nm_ref[...] = nm
        nv_ref[...] = nv
        d_ref[...] = -ADAM_LR * ((nm * bc1) / (jnp.sqrt(nv * bc2) + ADAM_EPS) + ADAM_WD * w_ref[...])

    spec = pl.BlockSpec((tm, c), lambda i: (i, 0))
    return pl.pallas_call(
        body, name=name, grid=(r // tm,),
        in_specs=[pl.BlockSpec((n, tm, c), lambda i: (0, i, 0)), spec, spec, spec],
        out_specs=[spec] * 4, out_shape=[jax.ShapeDtypeStruct((r, c), F32)] * 4,
        compiler_params=_cp("parallel"),
    )(land, w, m, v)


PACK_W = 2048


def _pack_rows(parts):
    flat = jnp.concatenate([p.reshape(-1).astype(F32) for p in parts])
    pad = (-flat.shape[0]) % (8 * PACK_W)
    return jnp.pad(flat, (0, pad)).reshape(-1, PACK_W)


def _unpack_rows(packed, shapes):
    flat = packed.reshape(-1)
    out, off = [], 0
    for s in shapes:
        n = math.prod(s)
        out.append(flat[off:off + n].reshape(s))
        off += n
    return out


def _col_slabs(gfull, width):
    r = gfull.shape[0]
    return jnp.transpose(gfull.reshape(r, N_DEV, width), (1, 0, 2)).astype(BF16)


def _row_slabs(gfull):
    return gfull.reshape(N_DEV, gfull.shape[0] // N_DEV, gfull.shape[1]).astype(BF16)


def _from_col_slabs(gathered):
    n, r, width = gathered.shape
    return jnp.transpose(gathered, (1, 0, 2)).reshape(r, n * width)


def _local_step(xs, target, norm_mix, wf_in, cw, a_log, dt_bias, dn_norm, wf_pa, wf_pd, wf_out, norm_ffn, wf_gu,
                wf_down, norm_final):
    d = D_MODEL
    n_main = D_PA + 4 * D_B
    w_pa_cols = wf_in[:, :D_PA]
    w_pb_cols = jnp.concatenate([
        wf_in[:, D_PA:n_main],
        jnp.repeat(wf_in[:, n_main:n_main + N_HEADS_B], HEAD, axis=1),
        jnp.repeat(wf_in[:, n_main + N_HEADS_B:n_main + 2 * N_HEADS_B], HEAD, axis=1),
        wf_in[:, n_main + 2 * N_HEADS_B:]], axis=1)
    a_log_bc = jnp.repeat(a_log, HEAD, axis=1)
    dt_bias_bc = jnp.repeat(dt_bias, HEAD, axis=1)

    u = _rms_fwd(xs, norm_mix)
    pa = _matmul(u, w_pa_cols, "nn", F32, 1024, 1536, d, name="proj_a")
    pb = _matmul(u, w_pb_cols, "nn", F32, 1024, 1024, d, name="proj_b")
    os_, ls_ = [], []
    for gi, dil in enumerate(DILATIONS):
        o_g, l_g = _attn_fwd_group(pa, gi, dil)
        os_.append(o_g)
        ls_.append(l_g)
    y_att, lse = _attn_merge(os_, ls_)
    qn, kn, vn, gdec, beta = _dn_prep_fwd(pb, cw, a_log_bc, dt_bias_bc)
    ub, ww, qd, kd, qk, gl = _dn_chunk_fwd(qn, kn, vn, gdec, beta)
    o_dn, states = _dn_scan_fwd(ub, ww, qd, kd, qk, gl)
    o_gated = _head_norm_fwd(o_dn, pb, dn_norm)
    y_a = _matmul(y_att, wf_pa, "nn", F32, 1024, d, D_ATTN_OUT, name="proj_attn")
    y_b = _matmul(o_gated, wf_pd, "nn", F32, 1024, d, d, name="proj_delta")
    merged = _gate_merge_fwd(pb, y_a, y_b)
    h1 = _matmul(merged, wf_out, "nn", F32, 1024, d, d, add=xs, name="out_proj")
    hn = _rms_fwd(h1, norm_ffn)
    gu = _matmul(hn, wf_gu, "nn", BF16, 1024, 1408, d, name="ffn_in")
    act = _swiglu_fwd(gu)
    h2 = _matmul(act, wf_down, "nn", F32, 512, d, D_FF, add=h1, name="ffn_out")
    loss_part, dh2, d_norm_final = _final_loss(h2, norm_final.reshape(1, d), target)

    dact = _matmul(dh2, wf_down, "nt", BF16, 1024, 1408, d, name="d_act")
    gw_down = _matmul(act, dh2, "tn", F32, 1408, d, 512, name="gw_down")
    dgu = _swiglu_bwd(gu, dact)
    dhn = _matmul(dgu, wf_gu, "nt", BF16, 1024, d, 1408, name="d_hn")
    gw_gu = _matmul(hn, dgu, "tn", F32, d, 1408, 512, name="gw_gu")
    dh1, d_norm_ffn = _rms_bwd(h1, norm_ffn, dhn, dh2)
    dmerged = _matmul(dh1, wf_out, "nt", BF16, 1024, d, d, name="d_merged")
    gw_out = _matmul(merged, dh1, "tn", F32, d, d, 512, name="gw_out")
    dya, dyb, dga, dgb = _gate_merge_bwd(pb, y_a, y_b, dmerged)
    dy_att = _matmul(dya, wf_pa, "nt", F32, 1024, D_ATTN_OUT, d, name="d_y_att")
    gw_pa = _matmul(y_att, dya, "tn", F32, D_ATTN_OUT, d, 512, name="gw_pa")
    do_gated = _matmul(dyb, wf_pd, "nt", BF16, 1024, d, d, name="d_o_gated")
    gw_pd = _matmul(o_gated, dyb, "tn", F32, d, d, 512, name="gw_pd")
    do_dn, dz, d_dn_norm = _head_norm_bwd(o_dn, pb, dn_norm, do_gated)
    dub, dww, dqd, dkd, dqk, dgl = _dn_scan_bwd(ub, ww, qd, kd, qk, gl, states, do_dn)
    dqn, dkn, dvn, dgdec, dbeta = _dn_chunk_bwd(qn, kn, vn, gdec, beta, dub, dww, dqd, dkd, dqk, dgl)
    dc, dbeta_raw, da_raw, d_conv_full, d_alog_bc, d_dt_bc = _dn_prep_bwd(
        pb, cw, a_log_bc, dt_bias_bc, gdec, dqn, dkn, dvn, dgdec, dbeta)
    dqkv_pre = _conv_bwd_input(dc, cw)
    segs = [_attn_bwd_group(pa, dy_att, y_att, lse, gi, dil) for gi, dil in enumerate(DILATIONS)]
    segs += [dqkv_pre, dz, dbeta_raw, da_raw, dga, dgb]
    w_att = jnp.stack([w_pa_cols[:, i * D_ATTN:(i + 1) * D_ATTN].reshape(d, N_HEADS_A, HEAD) for i in range(3)],
                      axis=2).reshape(d, D_PA)
    du = _matmul_nt_segments(segs, jnp.concatenate([w_att, w_pb_cols], axis=1), BF16, 1024, 512, "d_u")
    gws = [_matmul(u, s, "tn", F32, d, 1536, 512, name=f"gw_in_{i}") for i, s in enumerate(segs)]
    dx, d_norm_mix = _rms_bwd(xs, norm_mix, du, dh1)

    g_att = jnp.concatenate(gws[:3], axis=1).reshape(d, N_HEADS_A, 3, HEAD)
    gw_in = jnp.concatenate(
        [g_att[:, :, i, :].reshape(d, D_ATTN) for i in range(3)]
        + [gws[3], gws[4], gws[5].reshape(d, N_HEADS_B, HEAD).sum(-1), gws[6].reshape(d, N_HEADS_B, HEAD).sum(-1),
           gws[7], gws[8]], axis=1)
    d_a_log = d_alog_bc.reshape(1, N_HEADS_B, HEAD)[:, :, 0]
    d_dt_bias = d_dt_bc.reshape(1, N_HEADS_B, HEAD)[:, :, 0]
    return (loss_part, dx, gw_in, gw_pa, gw_pd, gw_out, gw_gu, gw_down, d_conv_full, d_norm_mix, d_norm_ffn,
            d_norm_final, d_dn_norm, d_a_log, d_dt_bias)


def kernel(x, norm_mix, w_in, conv_w, a_log, dt_bias, dn_norm, w_proj_attn, w_proj_delta, w_out, norm_ffn, w_gate, w_up, w_down, norm_final, loss_target, m_norm_mix, m_w_in, m_conv_w, m_a_log, m_dt_bias, m_dn_norm, m_w_proj_attn, m_w_proj_delta, m_w_out, m_norm_ffn, m_w_gate, m_w_up, m_w_down, m_norm_final, v_norm_mix, v_w_in, v_conv_w, v_a_log, v_dt_bias, v_dn_norm, v_w_proj_attn, v_w_proj_delta, v_w_out, v_norm_ffn, v_w_gate, v_w_up, v_w_down, v_norm_final):
    d = D_MODEL
    xs = x[0]
    target = loss_target[0]
    me = _dev_index(_mesh_pos())

    shards = [w_in[0], w_proj_attn[0], w_proj_delta[0], w_out[0], w_gate[0], w_up[0], w_down[0]]
    g_in, g_pa, g_pd, g_out, g_gate, g_up, g_down, g_conv = _exchange(
        [s.astype(BF16) for s in shards] + [conv_w[0]], [False] * 8, "gather_weights")
    wf_in = _from_col_slabs(g_in)
    wf_pa = _from_col_slabs(g_pa)
    wf_pd = g_pd.reshape(D_B, d)
    wf_out = g_out.reshape(d, d)
    wf_gu = jnp.concatenate([_from_col_slabs(g_gate), _from_col_slabs(g_up)], axis=1)
    wf_down = g_down.reshape(D_FF, d)
    cw = _from_col_slabs(g_conv)

    (loss_part, dx, gw_in, gw_pa, gw_pd, gw_out, gw_gu, gw_down, d_conv_full, d_norm_mix, d_norm_ffn, d_norm_final,
     d_dn_norm, d_a_log, d_dt_bias) = _local_step(xs, target, norm_mix, wf_in, cw, a_log, dt_bias, dn_norm, wf_pa,
                                                  wf_pd, wf_out, norm_ffn, wf_gu, wf_down, norm_final)

    slabs = [_col_slabs(gw_in, SHARD_IN), _col_slabs(gw_pa, d // N_DEV), _row_slabs(gw_pd), _row_slabs(gw_out),
             _col_slabs(gw_gu[:, :D_FF], D_FF // N_DEV), _col_slabs(gw_gu[:, D_FF:], D_FF // N_DEV),
             _row_slabs(gw_down)]
    small_shapes = [(1, d), (1, d), (d,), (1, HEAD), (1, N_HEADS_B), (1, N_HEADS_B), (1, 1), (CONV_WIDTH, 3 * D_B)]
    packed = _pack_rows([d_norm_mix, d_norm_ffn, d_norm_final, d_dn_norm, d_a_log, d_dt_bias,
                         loss_part[:, :1], d_conv_full])
    landed = _exchange(slabs + [packed], [True] * 7 + [False], "exchange_grads")
    zero1 = jnp.zeros((1, 1), F32)
    zconv = jnp.zeros((CONV_WIDTH, 3 * D_B), F32)
    small_w = _pack_rows([norm_mix, norm_ffn, norm_final, dn_norm, a_log, dt_bias, zero1, zconv])
    small_m = _pack_rows([m_norm_mix, m_norm_ffn, m_norm_final, m_dn_norm, m_a_log, m_dt_bias, zero1, zconv])
    small_v = _pack_rows([v_norm_mix, v_norm_ffn, v_norm_final, v_dn_norm, v_a_log, v_dt_bias, zero1, zconv])
    small = [_unpack_rows(z, small_shapes) for z in _adamw(landed[7], small_w, small_m, small_v, "adamw_small")]
    loss = small[0][6].reshape(())
    conv_shard = 3 * D_B // N_DEV
    g_conv_own = lax.dynamic_slice_in_dim(small[0][7], me * conv_shard, conv_shard, axis=1)
    r_conv = _adamw(g_conv_own[None], conv_w[0], m_conv_w[0], v_conv_w[0], "adamw_conv")
    big = [_adamw(landed[i], w[0], m[0], v[0], f"adamw_{i}") for i, (w, m, v) in enumerate([
        (w_in, m_w_in, v_w_in), (w_proj_attn, m_w_proj_attn, v_w_proj_attn),
        (w_proj_delta, m_w_proj_delta, v_w_proj_delta), (w_out, m_w_out, v_w_out),
        (w_gate, m_w_gate, v_w_gate), (w_up, m_w_up, v_w_up), (w_down, m_w_down, v_w_down)])]

    def leaves(k):
        sm = small[k]
        return [sm[0], big[0][k][None], r_conv[k][None], sm[4], sm[5], sm[3], big[1][k][None], big[2][k][None],
                big[3][k][None], sm[1], big[4][k][None], big[5][k][None], big[6][k][None], sm[2]]

    return (loss, dx[None], *leaves(0), *leaves(1), *leaves(2), *leaves(3))
```

```python
import math

import jax
import jax.numpy as jnp
from jax import lax
from jax.experimental import pallas as pl
from jax.experimental.pallas import tpu as pltpu

F32 = jnp.float32
BF16 = jnp.bfloat16
HI = lax.Precision.HIGH

D_MODEL = 1024
N_DEV = 8
HEAD = 128
N_HEADS_A = 12
HEADS_PER_GROUP = 4
DILATIONS = (1, 4, 16)
BLOCK_A = 128
D_ATTN = N_HEADS_A * HEAD
D_ATTN_OUT = HEADS_PER_GROUP * HEAD
N_HEADS_B = 8
D_B = N_HEADS_B * HEAD
CONV_WIDTH = 4
CH = 64
CHUNK_GROUP = 8
SCAN_HEADS = 4
D_FF = 2816
EPS = 1e-6
D_IN = 3 * D_ATTN + 4 * D_B + 2 * N_HEADS_B + 2 * D_MODEL
SHARD_IN = D_IN // N_DEV
PB_Z, PB_BETA, PB_A, PB_GATE = 3072, 4096, 5120, 6144
D_PA = 3 * D_ATTN
ADAM_LR, ADAM_B1, ADAM_B2, ADAM_EPS, ADAM_WD, ADAM_STEP = 0.001, 0.9, 0.999, 1e-08, 0.01, 10
VMEM_LIMIT = 56 * 1024 * 1024

NN = ((1,), (0,))
NT = ((1,), (1,))
TN = ((0,), (0,))


def _dot(a, b, dims=NN, prec=None):
    return lax.dot_general(a, b, (dims, ((), ())), precision=prec, preferred_element_type=F32)


def _bdot(a, b, dims=NN):
    return _dot(a.astype(BF16), b.astype(BF16), dims)


def _hdot(a, b, dims=NN):
    return _dot(a.astype(F32), b.astype(F32), dims, HI)


def _cp(*sem):
    return pltpu.CompilerParams(dimension_semantics=sem, vmem_limit_bytes=VMEM_LIMIT)


def _sigmoid(x):
    return 1.0 / (1.0 + jnp.exp(-x))


def _softplus(x):
    return jnp.maximum(x, 0.0) + jnp.log(1.0 + jnp.exp(-jnp.abs(x)))


def _rowsum(x):
    return jnp.sum(x, axis=-1, keepdims=True)


def _matmul(a, b, mode, out_dtype, tm, tn, tk, add=None, name="mm"):
    if mode == "nn":
        (m, k), (k2, n) = a.shape, b.shape
    elif mode == "nt":
        (m, k), (n, k2) = a.shape, b.shape
    else:
        (k, m), (k2, n) = a.shape, b.shape
    assert k == k2, (a.shape, b.shape, mode)
    tm, tn, tk = min(tm, m), min(tn, n), min(tk, k)
    assert m % tm == 0 and n % tn == 0 and k % tk == 0, (a.shape, b.shape, tm, tn, tk)
    nk = k // tk
    dims = {"nn": NN, "nt": NT, "tn": TN}[mode]

    def body(*refs):
        if add is None:
            a_ref, b_ref, o_ref, acc = refs
            add_ref = None
        else:
            a_ref, b_ref, add_ref, o_ref, acc = refs
        kk = pl.program_id(2)

        @pl.when(kk == 0)
        def _():
            acc[...] = jnp.zeros_like(acc)

        acc[...] += _bdot(a_ref[...], b_ref[...], dims)

        @pl.when(kk == nk - 1)
        def _():
            r = acc[...]
            if add_ref is not None:
                r = r + add_ref[...].astype(F32)
            o_ref[...] = r.astype(out_dtype)

    a_spec = (pl.BlockSpec((tk, tm), lambda i, j, kk: (kk, i)) if mode == "tn"
              else pl.BlockSpec((tm, tk), lambda i, j, kk: (i, kk)))
    b_spec = (pl.BlockSpec((tn, tk), lambda i, j, kk: (j, kk)) if mode == "nt"
              else pl.BlockSpec((tk, tn), lambda i, j, kk: (kk, j)))
    in_specs = [a_spec, b_spec]
    args = [a, b]
    if add is not None:
        in_specs.append(pl.BlockSpec((tm, tn), lambda i, j, kk: (i, j)))
        args.append(add)
    return pl.pallas_call(
        body, name=name, grid=(m // tm, n // tn, nk),
        in_specs=in_specs, out_specs=pl.BlockSpec((tm, tn), lambda i, j, kk: (i, j)),
        out_shape=jax.ShapeDtypeStruct((m, n), out_dtype),
        scratch_shapes=[pltpu.VMEM((tm, tn), F32)],
        compiler_params=_cp("parallel", "parallel", "arbitrary"),
    )(*args)


def _matmul_nt_segments(segs, b, out_dtype, tm, tk, name, add=None, exch=None):
    m = segs[0].shape[0]
    n, ktot = b.shape
    ns = len(segs)
    nks = [s.shape[1] // tk for s in segs]
    assert all(s.shape[1] % tk == 0 for s in segs) and sum(s.shape[1] for s in segs) == ktot and m % tm == 0
    starts = [sum(nks[:i]) for i in range(ns)]
    nk = sum(nks)

    def body(*refs):
        seg_refs, b_ref = refs[:ns], refs[ns]
        add_ref = refs[ns + 1] if add is not None else None
        o_ref, acc = refs[-2], refs[-1]
        kk = pl.program_id(1)

        @pl.when(kk == 0)
        def _():
            acc[...] = jnp.zeros_like(acc)

        for a_ref, k0, nk_s in zip(seg_refs, starts, nks):
            @pl.when(jnp.logical_and(kk >= k0, kk < k0 + nk_s))
            def _(a_ref=a_ref):
                acc[...] += _bdot(a_ref[...], b_ref[...], NT)

        @pl.when(kk == nk - 1)
        def _():
            r = acc[...]
            if add_ref is not None:
                r = r + add_ref[...].astype(F32)
            o_ref[...] = r.astype(out_dtype)

    def seg_spec(k0, nk_s):
        return pl.BlockSpec((tm, tk), lambda i, kk: (i, jnp.clip(kk - k0, 0, nk_s - 1)))

    row = pl.BlockSpec((tm, n), lambda i, kk: (i, 0))
    in_specs = [seg_spec(k0, nk_s) for k0, nk_s in zip(starts, nks)] + [pl.BlockSpec((n, tk), lambda i, kk: (0, kk))]
    args = list(segs) + [b]
    if add is not None:
        in_specs.append(row)
        args.append(add)
    return _hosted_call(body, name=name, grid=(m // tm, nk), in_specs=in_specs, out_specs=[row],
                        out_shape=[jax.ShapeDtypeStruct((m, n), out_dtype)],
                        scratch_shapes=[pltpu.VMEM((tm, n), F32)], args=args,
                        dims=("parallel", "arbitrary"), exch=exch)


def _row_spec(tm, cols, cb=0):
    return pl.BlockSpec((tm, cols), lambda i, cb=cb: (i, cb))


def _bcast_spec(rows, cols):
    return pl.BlockSpec((rows, cols), lambda i: (0, 0))


def _rms_fwd(x, w, tm=512):
    t, d = x.shape

    def body(x_ref, w_ref, o_ref):
        xv = x_ref[...]
        r = lax.rsqrt(jnp.mean(xv * xv, axis=-1, keepdims=True) + EPS)
        o_ref[...] = (xv * r * w_ref[...]).astype(BF16)

    return pl.pallas_call(
        body, name="rms_fwd", grid=(t // tm,),
        in_specs=[_row_spec(tm, d), _bcast_spec(1, d)], out_specs=_row_spec(tm, d),
        out_shape=jax.ShapeDtypeStruct((t, d), BF16), compiler_params=_cp("parallel"),
    )(x, w)


def _rms_bwd(x, w, dy, resid, tm=512):
    t, d = x.shape

    def body(x_ref, w_ref, dy_ref, res_ref, dx_ref, dw_ref):
        xv = x_ref[...]
        r = lax.rsqrt(jnp.mean(xv * xv, axis=-1, keepdims=True) + EPS)
        xh = xv * r
        dyv = dy_ref[...].astype(F32)
        dxh = dyv * w_ref[...]
        dx_ref[...] = res_ref[...] + r * (dxh - xh * jnp.mean(dxh * xh, axis=-1, keepdims=True))

        @pl.when(pl.program_id(0) == 0)
        def _():
            dw_ref[...] = jnp.zeros_like(dw_ref)

        dw_ref[...] += jnp.sum(dyv * xh, axis=0, keepdims=True)

    return pl.pallas_call(
        body, name="rms_bwd", grid=(t // tm,),
        in_specs=[_row_spec(tm, d), _bcast_spec(1, d), _row_spec(tm, d), _row_spec(tm, d)],
        out_specs=[_row_spec(tm, d), _bcast_spec(1, d)],
        out_shape=[jax.ShapeDtypeStruct((t, d), F32), jax.ShapeDtypeStruct((1, d), F32)],
        compiler_params=_cp("arbitrary"),
    )(x, w, dy, resid)


def _final_loss(h, w, target, tm=512):
    t, d = h.shape

    def body(h_ref, w_ref, t_ref, loss_ref, dh_ref, dw_ref):
        hv = h_ref[...]
        r = lax.rsqrt(jnp.mean(hv * hv, axis=-1, keepdims=True) + EPS)
        xh = hv * r
        wv = w_ref[...]
        err = xh * wv - t_ref[...]
        dy = err * (1.0 / d)
        dxh = dy * wv
        dh_ref[...] = r * (dxh - xh * jnp.mean(dxh * xh, axis=-1, keepdims=True))

        @pl.when(pl.program_id(0) == 0)
        def _():
            dw_ref[...] = jnp.zeros_like(dw_ref)
            loss_ref[...] = jnp.zeros_like(loss_ref)

        dw_ref[...] += jnp.sum(dy * xh, axis=0, keepdims=True)
        part = 0.5 * jnp.sum(jnp.mean(err * err, axis=-1, keepdims=True), axis=0, keepdims=True)
        loss_ref[...] += part + jnp.zeros((1, HEAD), F32)

    return pl.pallas_call(
        body, name="final_loss", grid=(t // tm,),
        in_specs=[_row_spec(tm, d), _bcast_spec(1, d), _row_spec(tm, d)],
        out_specs=[_bcast_spec(1, HEAD), _row_spec(tm, d), _bcast_spec(1, d)],
        out_shape=[jax.ShapeDtypeStruct((1, HEAD), F32), jax.ShapeDtypeStruct((t, d), F32),
                   jax.ShapeDtypeStruct((1, d), F32)],
        compiler_params=_cp("arbitrary"),
    )(h, w, target)


def _swiglu_fwd(gu, tm=256):
    t = gu.shape[0]
    ff = gu.shape[1] // 2

    def body(gu_ref, o_ref):
        g = gu_ref[:, :ff].astype(F32)
        o_ref[...] = (g * _sigmoid(g) * gu_ref[:, ff:].astype(F32)).astype(BF16)

    return pl.pallas_call(
        body, name="swiglu_fwd", grid=(t // tm,),
        in_specs=[_row_spec(tm, 2 * ff)], out_specs=_row_spec(tm, ff),
        out_shape=jax.ShapeDtypeStruct((t, ff), BF16), compiler_params=_cp("parallel"),
    )(gu)


def _swiglu_bwd(gu, dact, tm=256):
    t = gu.shape[0]
    ff = gu.shape[1] // 2

    def body(gu_ref, d_ref, o_ref):
        g = gu_ref[:, :ff].astype(F32)
        u = gu_ref[:, ff:].astype(F32)
        dv = d_ref[...].astype(F32)
        sg = _sigmoid(g)
        o_ref[:, :ff] = (dv * u * (sg + g * sg * (1.0 - sg))).astype(BF16)
        o_ref[:, ff:] = (dv * g * sg).astype(BF16)

    return pl.pallas_call(
        body, name="swiglu_bwd", grid=(t // tm,),
        in_specs=[_row_spec(tm, 2 * ff), _row_spec(tm, ff)], out_specs=_row_spec(tm, 2 * ff),
        out_shape=jax.ShapeDtypeStruct((t, 2 * ff), BF16), compiler_params=_cp("parallel"),
    )(gu, dact)


def _gate_merge_fwd(pb, ya, yb, tm=512):
    t, d = ya.shape
    cb = PB_GATE // d

    def body(ga_ref, gb_ref, ya_ref, yb_ref, o_ref):
        o_ref[...] = (_sigmoid(ga_ref[...]) * ya_ref[...] + _sigmoid(gb_ref[...]) * yb_ref[...]).astype(BF16)

    return pl.pallas_call(
        body, name="gate_merge_fwd", grid=(t // tm,),
        in_specs=[_row_spec(tm, d, cb), _row_spec(tm, d, cb + 1), _row_spec(tm, d), _row_spec(tm, d)],
        out_specs=_row_spec(tm, d),
        out_shape=jax.ShapeDtypeStruct((t, d), BF16), compiler_params=_cp("parallel"),
    )(pb, pb, ya, yb)


def _gate_merge_bwd(pb, ya, yb, dm, tm=512):
    t, d = ya.shape
    cb = PB_GATE // d

    def body(ga_ref, gb_ref, ya_ref, yb_ref, dm_ref, dya_ref, dyb_ref, dga_ref, dgb_ref):
        dmv = dm_ref[...].astype(F32)
        sa = _sigmoid(ga_ref[...])
        sb = _sigmoid(gb_ref[...])
        dya_ref[...] = (dmv * sa).astype(BF16)
        dyb_ref[...] = (dmv * sb).astype(BF16)
        dga_ref[...] = (dmv * ya_ref[...] * sa * (1.0 - sa)).astype(BF16)
        dgb_ref[...] = (dmv * yb_ref[...] * sb * (1.0 - sb)).astype(BF16)

    return pl.pallas_call(
        body, name="gate_merge_bwd", grid=(t // tm,),
        in_specs=[_row_spec(tm, d, cb), _row_spec(tm, d, cb + 1), _row_spec(tm, d), _row_spec(tm, d),
                  _row_spec(tm, d)],
        out_specs=[_row_spec(tm, d)] * 4,
        out_shape=[jax.ShapeDtypeStruct((t, d), BF16)] * 4, compiler_params=_cp("parallel"),
    )(pb, pb, ya, yb, dm)


def _head_norm_fwd(o, pb, wn, tm=512):
    t, d = o.shape
    nh = d // HEAD

    def body(o_ref, z_ref, w_ref, out_ref):
        wv = w_ref[...]
        for h in range(nh):
            sl = slice(h * HEAD, (h + 1) * HEAD)
            ov = o_ref[:, sl]
            zv = z_ref[:, sl]
            r = lax.rsqrt(jnp.mean(ov * ov, axis=-1, keepdims=True) + EPS)
            out_ref[:, sl] = (ov * r * wv * (zv * _sigmoid(zv))).astype(BF16)

    return pl.pallas_call(
        body, name="head_norm_fwd", grid=(t // tm,),
        in_specs=[_row_spec(tm, d), _row_spec(tm, d, PB_Z // d), _bcast_spec(1, HEAD)],
        out_specs=_row_spec(tm, d),
        out_shape=jax.ShapeDtypeStruct((t, d), BF16), compiler_params=_cp("parallel"),
    )(o, pb, wn)


def _head_norm_bwd(o, pb, wn, dout, tm=512):
    t, d = o.shape
    nh = d // HEAD

    def body(o_ref, z_ref, w_ref, d_ref, do_ref, dz_ref, dw_ref):
        wv = w_ref[...]
        dw_acc = jnp.zeros((1, HEAD), F32)
        for h in range(nh):
            sl = slice(h * HEAD, (h + 1) * HEAD)
            ov = o_ref[:, sl]
            zv = z_ref[:, sl]
            dv = d_ref[:, sl].astype(F32)
            r = lax.rsqrt(jnp.mean(ov * ov, axis=-1, keepdims=True) + EPS)
            xh = ov * r
            sz = _sigmoid(zv)
            dn = dv * (zv * sz)
            dz_ref[:, sl] = (dv * xh * wv * (sz + zv * sz * (1.0 - sz))).astype(BF16)
            dxh = dn * wv
            do_ref[:, sl] = r * (dxh - xh * jnp.mean(dxh * xh, axis=-1, keepdims=True))
            dw_acc = dw_acc + jnp.sum(dn * xh, axis=0, keepdims=True)

        @pl.when(pl.program_id(0) == 0)
        def _():
            dw_ref[...] = jnp.zeros_like(dw_ref)

        dw_ref[...] += dw_acc

    return pl.pallas_call(
        body, name="head_norm_bwd", grid=(t // tm,),
        in_specs=[_row_spec(tm, d), _row_spec(tm, d, PB_Z // d), _bcast_spec(1, HEAD), _row_spec(tm, d)],
        out_specs=[_row_spec(tm, d), _row_spec(tm, d), _bcast_spec(1, HEAD)],
        out_shape=[jax.ShapeDtypeStruct((t, d), F32), jax.ShapeDtypeStruct((t, d), BF16),
                   jax.ShapeDtypeStruct((1, HEAD), F32)],
        compiler_params=_cp("arbitrary"),
    )(o, pb, wn, dout)


def _attn_bias(gi, hh, dil):
    i = lax.broadcasted_iota(jnp.int32, (BLOCK_A, BLOCK_A), 0)
    j = lax.broadcasted_iota(jnp.int32, (BLOCK_A, BLOCK_A), 1)
    hf = (gi * HEADS_PER_GROUP + hh + 1).astype(F32)
    slope = jnp.exp(jnp.full((1, BLOCK_A), -8.0 * math.log(2.0) / N_HEADS_A, F32) * hf) * float(dil)
    d_prev = (BLOCK_A + i - j).astype(F32)
    d_cur = (i - j).astype(F32)
    return -slope * d_prev, -slope * d_cur, j >= i, j <= i


ATTN_TOKENS = 2048


def _sub_rows(a, r, dil):
    start = a * BLOCK_A * dil + r
    return pl.ds(start, BLOCK_A) if dil == 1 else pl.ds(start, BLOCK_A, stride=dil)


def _attn_fwd_group(pa, gi, dil, tb=ATTN_TOKENS):
    t = pa.shape[0]
    tb = min(tb, t)
    hb = BLOCK_A * dil
    nb = tb // hb
    scale = HEAD ** -0.5

    def body(q_ref, k_ref, v_ref, kp_ref, vp_ref, o_ref, l_ref):
        hh = pl.program_id(0)
        step = pl.program_id(1)
        b_prev, b_cur, m_prev, m_cur = _attn_bias(gi, hh, dil)
        m_first = jnp.logical_and(m_prev, step > 0)
        for r in range(dil):
            kp, vp = kp_ref[_sub_rows(0, r, dil), :], vp_ref[_sub_rows(0, r, dil), :]
            for a in range(nb):
                rows = _sub_rows(a, r, dil)
                q, kc, vc = q_ref[rows, :], k_ref[rows, :], v_ref[rows, :]
                s_p = jnp.where(m_first if a == 0 else m_prev, _bdot(q, kp, NT) * scale + b_prev, -1e30)
                s_c = jnp.where(m_cur, _bdot(q, kc, NT) * scale + b_cur, -1e30)
                m = jnp.maximum(jnp.max(s_p, axis=-1, keepdims=True), jnp.max(s_c, axis=-1, keepdims=True))
                p_p = jnp.exp(s_p - m)
                p_c = jnp.exp(s_c - m)
                den = _rowsum(p_p) + _rowsum(p_c)
                o_ref[rows, :] = (_bdot(p_p, vp) + _bdot(p_c, vc)) / den
                l_ref[rows, :] = (m + jnp.log(den)) + jnp.zeros((BLOCK_A, HEAD), F32)
                kp, vp = kc, vc

    def col(base):
        return lambda hh, s: (s, base + hh)

    def col_prev(base):
        return lambda hh, s: (jnp.maximum(s * nb - 1, 0), base + hh)

    qb, kb, vb = gi * HEADS_PER_GROUP, N_HEADS_A + gi * HEADS_PER_GROUP, 2 * N_HEADS_A + gi * HEADS_PER_GROUP
    ospec = pl.BlockSpec((tb, HEAD), lambda hh, s: (s, hh))
    return pl.pallas_call(
        body, name=f"attn_fwd_g{gi}", grid=(HEADS_PER_GROUP, t // tb),
        in_specs=[pl.BlockSpec((tb, HEAD), col(qb)), pl.BlockSpec((tb, HEAD), col(kb)),
                  pl.BlockSpec((tb, HEAD), col(vb)),
                  pl.BlockSpec((hb, HEAD), col_prev(kb)), pl.BlockSpec((hb, HEAD), col_prev(vb))],
        out_specs=[ospec, ospec],
        out_shape=[jax.ShapeDtypeStruct((t, D_ATTN_OUT), F32)] * 2,
        compiler_params=_cp("parallel", "parallel"),
    )(pa, pa, pa, pa, pa)


def _attn_merge(os, ls, tm=512):
    t, d = os[0].shape

    def body(o0, o1, o2, l0, l1, l2, y_ref, lse_ref):
        a0, a1, a2 = l0[...], l1[...], l2[...]
        m = jnp.maximum(jnp.maximum(a0, a1), a2)
        e0, e1, e2 = jnp.exp(a0 - m), jnp.exp(a1 - m), jnp.exp(a2 - m)
        den = e0 + e1 + e2
        y_ref[...] = (e0 * o0[...] + e1 * o1[...] + e2 * o2[...]) / den
        lse_ref[...] = m + jnp.log(den)

    return pl.pallas_call(
        body, name="attn_merge", grid=(t // tm,),
        in_specs=[_row_spec(tm, d)] * 6, out_specs=[_row_spec(tm, d)] * 2,
        out_shape=[jax.ShapeDtypeStruct((t, d), F32)] * 2,
        compiler_params=_cp("parallel"),
    )(*os, *ls)


def _attn_bwd_group(pa, dy, y, lse, gi, dil, tb=ATTN_TOKENS):
    t = pa.shape[0]
    tb = min(tb, t)
    hb = BLOCK_A * dil
    nb = tb // hb
    nsteps = t // tb
    scale = HEAD ** -0.5

    def body(q_ref, k_ref, v_ref, dy_ref, y_ref, l_ref, kp_ref, vp_ref, d_ref,
             dq_s, dk_s, dv_s, carry_k, carry_v):
        hh = pl.program_id(0)
        step = pl.program_id(1)

        @pl.when(step == 0)
        def _():
            carry_k[...] = jnp.zeros_like(carry_k)
            carry_v[...] = jnp.zeros_like(carry_v)

        b_prev, b_cur, m_prev, m_cur = _attn_bias(gi, hh, dil)
        m_first = jnp.logical_and(m_prev, step < nsteps - 1)
        for r in range(dil):
            halo = _sub_rows(0, r, dil)
            dk_in, dv_in = carry_k[halo, :], carry_v[halo, :]
            kp, vp = kp_ref[halo, :], vp_ref[halo, :]
            prev_rows = None
            dk_pend = dv_pend = None
            for a in range(nb):
                rows = _sub_rows(a, r, dil)
                q, kc, vc = q_ref[rows, :], k_ref[rows, :], v_ref[rows, :]
                dyb, lb = dy_ref[rows, :], l_ref[rows, :]
                delta = _rowsum(dyb * y_ref[rows, :])
                mp = m_first if a == 0 else m_prev
                s = _bdot(q, kp, NT) * scale + b_prev
                p = jnp.where(mp, jnp.exp(jnp.where(mp, s - lb, 0.0)), 0.0)
                ds = p * (_bdot(dyb, vp, NT) - delta)
                dq = _bdot(ds, kp)
                dk_prev, dv_prev = _bdot(ds, q, TN), _bdot(p, dyb, TN)
                if a == 0:
                    carry_k[halo, :] = dk_prev
                    carry_v[halo, :] = dv_prev
                else:
                    dk_s[prev_rows, :] = dk_pend + dk_prev
                    dv_s[prev_rows, :] = dv_pend + dv_prev
                s = _bdot(q, kc, NT) * scale + b_cur
                p = jnp.where(m_cur, jnp.exp(jnp.where(m_cur, s - lb, 0.0)), 0.0)
                ds = p * (_bdot(dyb, vc, NT) - delta)
                dq_s[rows, :] = dq + _bdot(ds, kc)
                dk_pend, dv_pend = _bdot(ds, q, TN), _bdot(p, dyb, TN)
                prev_rows, kp, vp = rows, kc, vc
            dk_s[prev_rows, :] = dk_pend + dk_in
            dv_s[prev_rows, :] = dv_pend + dv_in
        d_ref[:, :HEAD] = (dq_s[...] * scale).astype(BF16)
        d_ref[:, HEAD:2 * HEAD] = (dk_s[...] * scale).astype(BF16)
        d_ref[:, 2 * HEAD:] = dv_s[...].astype(BF16)

    def col(base):
        return lambda hh, s: (nsteps - 1 - s, base + hh)

    def col_prev(base):
        return lambda hh, s: (jnp.maximum((nsteps - 1 - s) * nb - 1, 0), base + hh)

    qb, kb, vb = gi * HEADS_PER_GROUP, N_HEADS_A + gi * HEADS_PER_GROUP, 2 * N_HEADS_A + gi * HEADS_PER_GROUP
    big, small = (tb, HEAD), (hb, HEAD)
    return pl.pallas_call(
        body, name=f"attn_bwd_g{gi}", grid=(HEADS_PER_GROUP, nsteps),
        in_specs=[pl.BlockSpec(big, col(qb)), pl.BlockSpec(big, col(kb)), pl.BlockSpec(big, col(vb)),
                  pl.BlockSpec(big, col(0)), pl.BlockSpec(big, col(0)), pl.BlockSpec(big, col(0)),
                  pl.BlockSpec(small, col_prev(kb)), pl.BlockSpec(small, col_prev(vb))],
        out_specs=pl.BlockSpec((tb, 3 * HEAD), col(0)),
        out_shape=jax.ShapeDtypeStruct((t, 3 * D_ATTN_OUT), BF16),
        scratch_shapes=[pltpu.VMEM(big, F32)] * 3 + [pltpu.VMEM(small, F32)] * 2,
        compiler_params=_cp("parallel", "arbitrary"),
    )(pa, pa, pa, dy, y, lse, pa, pa)


def _shift_down(cur, prev8, s):
    if s == 0:
        return cur
    rolled = pltpu.roll(cur, s, 0)
    prolled = pltpu.roll(prev8, s, 0)
    rid = lax.broadcasted_iota(jnp.int32, prev8.shape, 0)
    top = jnp.where(rid < s, prolled, rolled[:8])
    return jnp.concatenate([top, rolled[8:]], axis=0)


def _shift_up(cur, next8, s):
    if s == 0:
        return cur
    n = cur.shape[0]
    rolled = pltpu.roll(cur, n - s, 0)
    nrolled = pltpu.roll(next8, 8 - s, 0)
    rid = lax.broadcasted_iota(jnp.int32, next8.shape, 0)
    bottom = jnp.where(rid >= 8 - s, nrolled, rolled[n - 8:])
    return jnp.concatenate([rolled[:n - 8], bottom], axis=0)


def _conv(xv, prev8, wv):
    c = jnp.zeros_like(xv)
    shifted = []
    for s in range(CONV_WIDTH):
        xs = _shift_down(xv, prev8, s)
        shifted.append(xs)
        c = c + wv[CONV_WIDTH - 1 - s:CONV_WIDTH - s, :] * xs
    return c, shifted


def _dn_prep_fwd(pb, conv_w, a_log_bc, dt_bias_bc, tm=256, exch=None):
    t = pb.shape[0]
    c3 = 3 * D_B
    r8 = tm // 8

    def body(x_ref, xp_ref, b_ref, a_ref, w_ref, al_ref, dt_ref, q_ref, k_ref, v_ref, g_ref, beta_ref):
        prev8 = jnp.where(pl.program_id(0) > 0, xp_ref[...], 0.0)
        c, _ = _conv(x_ref[...], prev8, w_ref[...])
        s = c * _sigmoid(c)
        for h in range(N_HEADS_B):
            sl = slice(h * HEAD, (h + 1) * HEAD)
            sq = s[:, h * HEAD:(h + 1) * HEAD]
            q_ref[:, sl] = sq * lax.rsqrt(_rowsum(sq * sq) + EPS) * (HEAD ** -0.5)
            sk = s[:, D_B + h * HEAD:D_B + (h + 1) * HEAD]
            k_ref[:, sl] = sk * lax.rsqrt(_rowsum(sk * sk) + EPS)
        v_ref[...] = s[:, 2 * D_B:]
        beta_ref[...] = _sigmoid(b_ref[...])
        g_ref[...] = -jnp.exp(al_ref[...]) * _softplus(a_ref[...] + dt_ref[...])

    return _hosted_call(
        body, name="dn_prep_fwd", grid=(t // tm,),
        in_specs=[_row_spec(tm, c3, 0),
                  pl.BlockSpec((8, c3), lambda i: (jnp.maximum(i * r8 - 1, 0), 0)),
                  _row_spec(tm, D_B, PB_BETA // D_B), _row_spec(tm, D_B, PB_A // D_B),
                  _bcast_spec(CONV_WIDTH, c3), _bcast_spec(1, D_B), _bcast_spec(1, D_B)],
        out_specs=[_row_spec(tm, D_B)] * 5,
        out_shape=[jax.ShapeDtypeStruct((t, D_B), F32)] * 5, scratch_shapes=[],
        args=(pb, pb, pb, pb, conv_w, a_log_bc, dt_bias_bc), dims=("parallel",), exch=exch)


def _dn_prep_bwd(pb, conv_w, a_log_bc, dt_bias_bc, g, dq, dk, dv, dg, dbeta, tm=128, exch=None):
    t = pb.shape[0]
    c3 = 3 * D_B
    r8 = tm // 8

    def body(x_ref, xp_ref, b_ref, a_ref, w_ref, al_ref, dt_ref, g_ref, dq_ref, dk_ref, dv_ref, dg_ref, db_ref,
             dc_ref, dbr_ref, dar_ref, dw_ref, dal_ref, ddt_ref):
        prev8 = jnp.where(pl.program_id(0) > 0, xp_ref[...], 0.0)
        c, shifted = _conv(x_ref[...], prev8, w_ref[...])
        sg = _sigmoid(c)
        s = c * sg
        dsilu = sg + c * sg * (1.0 - sg)
        for h in range(N_HEADS_B):
            sl = slice(h * HEAD, (h + 1) * HEAD)
            for base, d_ref, mult in ((0, dq_ref, HEAD ** -0.5), (D_B, dk_ref, 1.0)):
                ssl = slice(base + h * HEAD, base + (h + 1) * HEAD)
                sv = s[:, ssl]
                r = lax.rsqrt(_rowsum(sv * sv) + EPS)
                yh = sv * r
                dyv = d_ref[:, sl] * mult
                dc_ref[:, ssl] = r * (dyv - yh * _rowsum(dyv * yh)) * dsilu[:, ssl]
        dc_ref[:, 2 * D_B:] = dv_ref[...] * dsilu[:, 2 * D_B:]
        dcv = dc_ref[...]

        @pl.when(pl.program_id(0) == 0)
        def _():
            dw_ref[...] = jnp.zeros_like(dw_ref)
            dal_ref[...] = jnp.zeros_like(dal_ref)
            ddt_ref[...] = jnp.zeros_like(ddt_ref)

        for sft in range(CONV_WIDTH):
            j = CONV_WIDTH - 1 - sft
            dw_ref[j:j + 1, :] += jnp.sum(dcv * shifted[sft], axis=0, keepdims=True)
        beta = _sigmoid(b_ref[...])
        dbr_ref[...] = (db_ref[...] * beta * (1.0 - beta) * (1.0 / HEAD)).astype(BF16)
        dgv = dg_ref[...]
        da = dgv * (-jnp.exp(al_ref[...])) * _sigmoid(a_ref[...] + dt_ref[...])
        dar_ref[...] = (da * (1.0 / HEAD)).astype(BF16)
        dal_ref[...] += jnp.sum(dgv * g_ref[...], axis=0, keepdims=True)
        ddt_ref[...] += jnp.sum(da, axis=0, keepdims=True)

    row = _row_spec(tm, D_B)
    return _hosted_call(
        body, name="dn_prep_bwd", grid=(t // tm,),
        in_specs=[_row_spec(tm, c3, 0),
                  pl.BlockSpec((8, c3), lambda i: (jnp.maximum(i * r8 - 1, 0), 0)),
                  _row_spec(tm, D_B, PB_BETA // D_B), _row_spec(tm, D_B, PB_A // D_B),
                  _bcast_spec(CONV_WIDTH, c3), _bcast_spec(1, D_B), _bcast_spec(1, D_B),
                  row, row, row, row, row, row],
        out_specs=[_row_spec(tm, c3), row, row, _bcast_spec(CONV_WIDTH, c3), _bcast_spec(1, D_B),
                   _bcast_spec(1, D_B)],
        out_shape=[jax.ShapeDtypeStruct((t, c3), F32), jax.ShapeDtypeStruct((t, D_B), BF16),
                   jax.ShapeDtypeStruct((t, D_B), BF16), jax.ShapeDtypeStruct((CONV_WIDTH, c3), F32),
                   jax.ShapeDtypeStruct((1, D_B), F32), jax.ShapeDtypeStruct((1, D_B), F32)],
        scratch_shapes=[], args=(pb, pb, pb, pb, conv_w, a_log_bc, dt_bias_bc, g, dq, dk, dv, dg, dbeta),
        dims=("arbitrary",), exch=exch)


def _conv_bwd_input(dc, conv_w, tm=256):
    t, c3 = dc.shape
    r8 = tm // 8
    nlast = t // 8 - 1
    nsteps = t // tm

    def body(d_ref, dn_ref, w_ref, o_ref):
        next8 = jnp.where(pl.program_id(0) < nsteps - 1, dn_ref[...], 0.0)
        dv = d_ref[...]
        wv = w_ref[...]
        acc = jnp.zeros_like(dv)
        for s in range(CONV_WIDTH):
            acc = acc + wv[CONV_WIDTH - 1 - s:CONV_WIDTH - s, :] * _shift_up(dv, next8, s)
        o_ref[...] = acc.astype(BF16)

    return pl.pallas_call(
        body, name="conv_bwd_input", grid=(nsteps,),
        in_specs=[_row_spec(tm, c3), pl.BlockSpec((8, c3), lambda i: (jnp.minimum((i + 1) * r8, nlast), 0)),
                  _bcast_spec(CONV_WIDTH, c3)],
        out_specs=_row_spec(tm, c3),
        out_shape=jax.ShapeDtypeStruct((t, c3), BF16), compiler_params=_cp("parallel"),
    )(dc, dc, conv_w)


def _lanes(x):
    return x[:, :CH]


def _tri_inv(a_list, r, c):
    eye = (r == c).astype(F32)
    b16 = (r >> 4) == (c >> 4)
    b32 = (r >> 5) == (c >> 5)
    ns = [jnp.where(b16, -a, 0.0) for a in a_list]
    xs = [eye + n for n in ns]
    ps = [_hdot(n, n) for n in ns]
    for last in (False, False, True):
        xs = [x + _hdot(x, p) for x, p in zip(xs, ps)]
        if not last:
            ps = [_hdot(p, p) for p in ps]
    for mask in (jnp.logical_and(b32, jnp.logical_not(b16)), jnp.logical_not(b32)):
        ts = [_hdot(x, jnp.where(mask, a, 0.0)) for x, a in zip(xs, a_list)]
        xs = [x - _hdot(t, x) for x, t in zip(xs, ts)]
    return xs


def _chunk_local(qs, ks, vs, gs, betas):
    r = lax.broadcasted_iota(jnp.int32, (CH, CH), 0)
    c = lax.broadcasted_iota(jnp.int32, (CH, CH), 1)
    incl, strict = r >= c, r > c
    lm = incl.astype(F32)
    cums = [_hdot(lm, jnp.concatenate([g, jnp.where(strict, _lanes(g), 0.0)], axis=1)) for g in gs]
    gcbs = [cm[:, :HEAD] for cm in cums]
    decays = [jnp.where(incl, jnp.exp(jnp.where(incl, cm[:, HEAD:], 0.0)), 0.0) for cm in cums]
    bcols = [_lanes(b) for b in betas]
    kks = [_bdot(k, k, NT) for k in ks]
    qkraws = [_bdot(q, k, NT) for q, k in zip(qs, ks)]
    tms = _tri_inv([jnp.where(strict, bc * kk * dc, 0.0) for bc, kk, dc in zip(bcols, kks, decays)], r, c)
    egs = [jnp.exp(gcb) for gcb in gcbs]
    sols = [_hdot(tm, jnp.concatenate([b * v, b * eg * k], axis=1))
            for tm, b, v, eg, k in zip(tms, betas, vs, egs, ks)]
    gls = [gcb[CH - 1:CH, :] for gcb in gcbs]
    eks = [jnp.exp(gl - gcb) for gl, gcb in zip(gls, gcbs)]
    return [dict(incl=incl, strict=strict, r=r, c=c, decay=dc, bcol=bc, kk=kk, tm=tm, eg=eg,
                 u_bar=sol[:, :HEAD], w=sol[:, HEAD:], qkraw=qkraw, gl=gl, ek=ek)
            for dc, bc, kk, tm, eg, sol, qkraw, gl, ek in zip(decays, bcols, kks, tms, egs, sols, qkraws, gls, eks)]


def _dn_chunk_fwd(q, k, v, g, beta, cps=8):
    t = q.shape[0]
    tm = cps * CH

    def body(q_ref, k_ref, v_ref, g_ref, b_ref, ub_ref, w_ref, qd_ref, kd_ref, qk_ref, gl_ref):
        for base in range(0, cps, CHUNK_GROUP):
            sls = [slice((base + j) * CH, (base + j + 1) * CH) for j in range(CHUNK_GROUP)]
            qs, ks = [q_ref[sl, :] for sl in sls], [k_ref[sl, :] for sl in sls]
            locs = _chunk_local(qs, ks, [v_ref[sl, :] for sl in sls], [g_ref[sl, :] for sl in sls],
                                [b_ref[sl, :] for sl in sls])
            for j, (sl, qv, kv, loc) in enumerate(zip(sls, qs, ks, locs)):
                ub_ref[sl, :] = loc["u_bar"]
                w_ref[sl, :] = loc["w"]
                qd_ref[sl, :] = qv * loc["eg"]
                kd_ref[sl, :] = kv * loc["ek"]
                qk_ref[sl, :] = loc["qkraw"] * loc["decay"]
                gl_ref[base + j:base + j + 1, :] = jnp.exp(loc["gl"])

    hspec = pl.BlockSpec((tm, HEAD), lambda h, i: (i, h))
    return pl.pallas_call(
        body, name="dn_chunk_fwd", grid=(N_HEADS_B, t // tm),
        in_specs=[hspec] * 5,
        out_specs=[hspec] * 4 + [pl.BlockSpec((None, tm, CH), lambda h, i: (h, i, 0)),
                                 pl.BlockSpec((cps, HEAD), lambda h, i: (i, h))],
        out_shape=[jax.ShapeDtypeStruct((t, D_B), F32)] * 4
        + [jax.ShapeDtypeStruct((N_HEADS_B, t, CH), F32), jax.ShapeDtypeStruct((t // CH, D_B), F32)],
        compiler_params=_cp("parallel", "parallel"),
    )(q, k, v, g, beta)


def _dn_scan_fwd(ub, w, qd, kd, qk, gl, cps=8):
    t = ub.shape[0]
    tm = cps * CH

    hg = SCAN_HEADS
    hs = list(range(hg))

    def body(ub_ref, w_ref, qd_ref, kd_ref, qk_ref, gl_ref, o_ref, st_ref, s_acc):
        @pl.when(pl.program_id(1) == 0)
        def _():
            s_acc[...] = jnp.zeros_like(s_acc)

        for ci in range(cps):
            sl = slice(ci * CH, (ci + 1) * CH)
            cols = [slice(h * HEAD, (h + 1) * HEAD) for h in hs]
            svs = [s_acc[h] for h in hs]
            for h in hs:
                st_ref[h, ci * HEAD:(ci + 1) * HEAD, :] = svs[h]
            us = [ub_ref[sl, cols[h]] - _bdot(w_ref[sl, cols[h]], svs[h]) for h in hs]
            for h in hs:
                s_acc[h] = gl_ref[ci:ci + 1, cols[h]] * svs[h] + _bdot(kd_ref[sl, cols[h]], us[h], TN)
            for h in hs:
                o_ref[sl, cols[h]] = _bdot(qd_ref[sl, cols[h]], svs[h]) + _bdot(qk_ref[h, sl, :], us[h])

    hspec = pl.BlockSpec((tm, hg * HEAD), lambda h, i: (i, h))
    return pl.pallas_call(
        body, name="dn_scan_fwd", grid=(N_HEADS_B // hg, t // tm),
        in_specs=[hspec] * 4 + [pl.BlockSpec((hg, tm, CH), lambda h, i: (h, i, 0)),
                                pl.BlockSpec((cps, hg * HEAD), lambda h, i: (i, h))],
        out_specs=[hspec, pl.BlockSpec((hg, cps * HEAD, HEAD), lambda h, i: (h, i, 0))],
        out_shape=[jax.ShapeDtypeStruct((t, D_B), F32),
                   jax.ShapeDtypeStruct((N_HEADS_B, (t // CH) * HEAD, HEAD), F32)],
        scratch_shapes=[pltpu.VMEM((hg, HEAD, HEAD), F32)],
        compiler_params=_cp("parallel", "arbitrary"),
    )(ub, w, qd, kd, qk, gl)


def _dn_scan_bwd(ub, w, qd, kd, qk, gl, st, do, cps=8):
    t = ub.shape[0]
    tm = cps * CH
    ns = t // tm

    hg = SCAN_HEADS
    hs = list(range(hg))

    def body(ub_ref, w_ref, qd_ref, kd_ref, qk_ref, gl_ref, st_ref, do_ref,
             dub_ref, dw_ref, dqd_ref, dkd_ref, dqk_ref, dgl_ref, ds_acc):
        @pl.when(pl.program_id(1) == 0)
        def _():
            ds_acc[...] = jnp.zeros_like(ds_acc)

        for ci in reversed(range(cps)):
            sl = slice(ci * CH, (ci + 1) * CH)
            cols = [slice(h * HEAD, (h + 1) * HEAD) for h in hs]
            svs = [st_ref[h, ci * HEAD:(ci + 1) * HEAD, :] for h in hs]
            wvs = [w_ref[sl, cols[h]] for h in hs]
            dovs = [do_ref[sl, cols[h]] for h in hs]
            dsvs = [ds_acc[h] for h in hs]
            us = [ub_ref[sl, cols[h]] - _bdot(wvs[h], svs[h]) for h in hs]
            dus = [_bdot(kd_ref[sl, cols[h]], dsvs[h]) + _bdot(qk_ref[h, sl, :], dovs[h], TN) for h in hs]
            for h in hs:
                ds_acc[h] = (gl_ref[ci:ci + 1, cols[h]] * dsvs[h] + _bdot(qd_ref[sl, cols[h]], dovs[h], TN)
                             - _bdot(wvs[h], dus[h], TN))
            for h in hs:
                dgl_ref[ci:ci + 1, cols[h]] = (jnp.sum(_rowsum(dsvs[h] * svs[h]), axis=0, keepdims=True)
                                              + jnp.zeros((1, HEAD), F32))
                dkd_ref[sl, cols[h]] = _bdot(us[h], dsvs[h], NT)
                dqd_ref[sl, cols[h]] = _bdot(dovs[h], svs[h], NT)
                dqk_ref[h, sl, :] = _bdot(dovs[h], us[h], NT)
                dub_ref[sl, cols[h]] = dus[h]
                dw_ref[sl, cols[h]] = -_bdot(dus[h], svs[h], NT)

    hspec = pl.BlockSpec((tm, hg * HEAD), lambda h, i: (ns - 1 - i, h))
    qkspec = pl.BlockSpec((hg, tm, CH), lambda h, i: (h, ns - 1 - i, 0))
    glspec = pl.BlockSpec((cps, hg * HEAD), lambda h, i: (ns - 1 - i, h))
    return pl.pallas_call(
        body, name="dn_scan_bwd", grid=(N_HEADS_B // hg, ns),
        in_specs=[hspec] * 4 + [qkspec, glspec,
                                pl.BlockSpec((hg, cps * HEAD, HEAD), lambda h, i: (h, ns - 1 - i, 0)), hspec],
        out_specs=[hspec] * 4 + [qkspec, glspec],
        out_shape=[jax.ShapeDtypeStruct((t, D_B), F32)] * 4
        + [jax.ShapeDtypeStruct((N_HEADS_B, t, CH), F32), jax.ShapeDtypeStruct((t // CH, D_B), F32)],
        scratch_shapes=[pltpu.VMEM((hg, HEAD, HEAD), F32)],
        compiler_params=_cp("parallel", "arbitrary"),
    )(ub, w, qd, kd, qk, gl, st, do)


def _dn_chunk_bwd(q, k, v, g, beta, dub, dw, dqd, dkd, dqk, dgl, cps=8):
    t = q.shape[0]
    tm = cps * CH

    def body(q_ref, k_ref, v_ref, g_ref, b_ref, dub_ref, dw_ref, dqd_ref, dkd_ref, dqk_ref, dgl_ref,
             dq_ref, dk_ref, dv_ref, dg_ref, db_ref):
        ones = jnp.ones((CH, HEAD), F32)
        rid = lax.broadcasted_iota(jnp.int32, (CH, HEAD), 0)

        def rest(ci, sl, qv, kv, vv, beta_v, loc, dr, da):
            incl = loc["incl"]
            eg, ek, decay, bcol, kk = loc["eg"], loc["ek"], loc["decay"], loc["bcol"], loc["kk"]
            drv, drk = dr[:, :HEAD], dr[:, HEAD:]
            dv_ref[sl, :] = beta_v * drv
            beg = beta_v * eg
            t1 = drk * kv
            dbeta = _rowsum(drv * vv + t1 * eg) + _rowsum(da * kk * decay)
            dkk = da * bcol * decay
            dqk_m = jnp.where(incl, dqk_ref[sl, :], 0.0)
            ddecay = da * bcol * kk + dqk_m * loc["qkraw"]
            dqkraw = dqk_m * decay
            dqdv, dkdv = dqd_ref[sl, :], dkd_ref[sl, :]
            dq_ref[sl, :] = _bdot(dqkraw, kv) + dqdv * eg
            dk_ref[sl, :] = (beg * drk + _bdot(dqkraw, qv, TN) + _bdot(dkk, kv) + _bdot(dkk, kv, TN)
                             + dkdv * ek)
            e = ddecay * decay
            skd = _rowsum(dkdv * kv * ek)
            dgc = _rowsum(beg * t1) + _rowsum(e) + _rowsum(dqdv * qv * eg) - skd
            colsum = _hdot(e, ones, TN)
            last = jnp.sum(skd, axis=0, keepdims=True) + dgl_ref[ci:ci + 1, :] * jnp.exp(loc["gl"])
            db_ref[sl, :] = dbeta + jnp.zeros((CH, HEAD), F32)
            return (dgc - colsum) + jnp.where(rid == CH - 1, last, 0.0)

        for base in range(0, cps, CHUNK_GROUP):
            cis = list(range(base, base + CHUNK_GROUP))
            sls = [slice(ci * CH, (ci + 1) * CH) for ci in cis]
            qs, ks, vs = [q_ref[sl, :] for sl in sls], [k_ref[sl, :] for sl in sls], [v_ref[sl, :] for sl in sls]
            betas = [b_ref[sl, :] for sl in sls]
            locs = _chunk_local(qs, ks, vs, [g_ref[sl, :] for sl in sls], betas)
            drs = [_hdot(loc["tm"], jnp.concatenate([dub_ref[sl, :], dw_ref[sl, :]], axis=1), TN)
                   for loc, sl in zip(locs, sls)]
            das = [jnp.where(loc["strict"],
                             -_hdot(dr, jnp.concatenate([loc["u_bar"], loc["w"]], axis=1), NT), 0.0)
                   for loc, dr in zip(locs, drs)]
            dgcs = [rest(*args) for args in zip(cis, sls, qs, ks, vs, betas, locs, drs, das)]
            um = (locs[0]["r"] <= locs[0]["c"]).astype(F32)
            for sl, dgc_bc in zip(sls, dgcs):
                dg_ref[sl, :] = _hdot(um, dgc_bc)

    hspec = pl.BlockSpec((tm, HEAD), lambda h, i: (i, h))
    return pl.pallas_call(
        body, name="dn_chunk_bwd", grid=(N_HEADS_B, t // tm),
        in_specs=[hspec] * 9 + [pl.BlockSpec((None, tm, CH), lambda h, i: (h, i, 0)),
                                pl.BlockSpec((cps, HEAD), lambda h, i: (i, h))],
        out_specs=[hspec] * 5,
        out_shape=[jax.ShapeDtypeStruct((t, D_B), F32)] * 5,
        compiler_params=_cp("parallel", "parallel"),
    )(q, k, v, g, beta, dub, dw, dqd, dkd, dqk, dgl)


FLIPS = [(fx, fy, fc) for fx in (0, 1) for fy in (0, 1) for fc in (0, 1)][1:]


def _mesh_pos():
    return lax.axis_index("x"), lax.axis_index("y"), lax.axis_index("c")


def _peer(pos, flip):
    return tuple((1 - p) if f else p for p, f in zip(pos, flip))


def _dev_index(pos):
    return 4 * pos[0] + 2 * pos[1] + pos[2]


class _Exchange:
    def __init__(self, tensors, scatter):
        self.tensors, self.scatter, self.nt = list(tensors), list(scatter), len(tensors)
        hbm = pl.BlockSpec(memory_space=pltpu.HBM)
        self.in_specs = [hbm] * self.nt
        self.out_specs = [hbm] * self.nt
        self.out_shape = [jax.ShapeDtypeStruct(x.shape if sc else (N_DEV,) + x.shape, x.dtype)
                          for x, sc in zip(tensors, scatter)]
        self.scratch_shapes = [pltpu.SemaphoreType.DMA((self.nt * 7,)), pltpu.SemaphoreType.DMA((self.nt * 7,)),
                               pltpu.SemaphoreType.DMA((self.nt,))]

    def _copies(self, ins, outs, sems):
        send_sems, recv_sems, local_sems = sems
        pos = _mesh_pos()
        me = _dev_index(pos)

        def remote(ti, fi, landing):
            peer = _peer(pos, FLIPS[fi])
            src = ins[ti].at[_dev_index(peer)] if self.scatter[ti] else ins[ti]
            return pltpu.make_async_remote_copy(
                src_ref=src, dst_ref=outs[ti].at[landing(peer)],
                send_sem=send_sems.at[ti * 7 + fi], recv_sem=recv_sems.at[ti * 7 + fi],
                device_id=peer, device_id_type=pl.DeviceIdType.MESH)

        pairs = [(ti, fi) for ti in range(self.nt) for fi in range(7)]
        local = [pltpu.make_async_copy(ins[ti].at[me] if self.scatter[ti] else ins[ti], outs[ti].at[me],
                                       local_sems.at[ti]) for ti in range(self.nt)]
        sends = [remote(ti, fi, lambda peer: me) for ti, fi in pairs]
        recvs = [remote(ti, fi, _dev_index) for ti, fi in pairs]
        return local, sends, recvs

    def start(self, ins, outs, sems):
        local, sends, _ = self._copies(ins, outs, sems)
        for cp in local + sends:
            cp.start()

    def wait(self, ins, outs, sems):
        local, sends, recvs = self._copies(ins, outs, sems)
        for cp in recvs:
            cp.wait_recv()
        for cp in sends:
            cp.wait_send()
        for cp in local:
            cp.wait()


def _exchange(tensors, scatter, name):
    ex = _Exchange(tensors, scatter)

    def body(*refs):
        ins, outs, sems = refs[:ex.nt], refs[ex.nt:2 * ex.nt], refs[2 * ex.nt:]
        ex.start(ins, outs, sems)
        ex.wait(ins, outs, sems)

    return pl.pallas_call(
        body, name=name, in_specs=ex.in_specs, out_specs=ex.out_specs, out_shape=ex.out_shape,
        scratch_shapes=ex.scratch_shapes, compiler_params=pltpu.CompilerParams(has_side_effects=True),
    )(*tensors)


def _hosted_call(body, *, name, grid, in_specs, out_specs, out_shape, scratch_shapes, args, dims, exch=None):
    if exch is None:
        return pl.pallas_call(body, name=name, grid=grid, in_specs=in_specs, out_specs=out_specs,
                              out_shape=out_shape, scratch_shapes=scratch_shapes,
                              compiler_params=_cp(*dims))(*args)
    n_in, n_out, n_sc, ne = len(in_specs), len(out_specs), len(scratch_shapes), exch.nt
    nsteps = math.prod(grid)

    def wrapped(*refs):
        ins, ex_in = refs[:n_in], refs[n_in:n_in + ne]
        outs = refs[n_in + ne:n_in + ne + n_out]
        ex_out = refs[n_in + ne + n_out:n_in + 2 * ne + n_out]
        rest = refs[n_in + 2 * ne + n_out:]
        scratch, sems = rest[:n_sc], rest[n_sc:]
        step = pl.program_id(0)
        for ax in range(1, len(grid)):
            step = step * grid[ax] + pl.program_id(ax)

        @pl.when(step == 0)
        def _():
            exch.start(ex_in, ex_out, sems)

        body(*ins, *outs, *scratch)

        @pl.when(step == nsteps - 1)
        def _():
            exch.wait(ex_in, ex_out, sems)

    return pl.pallas_call(
        wrapped, name=name, grid=grid, in_specs=list(in_specs) + exch.in_specs,
        out_specs=list(out_specs) + exch.out_specs, out_shape=list(out_shape) + exch.out_shape,
        scratch_shapes=list(scratch_shapes) + exch.scratch_shapes,
        compiler_params=pltpu.CompilerParams(dimension_semantics=("arbitrary",) * len(grid),
                                             vmem_limit_bytes=VMEM_LIMIT, has_side_effects=True),
    )(*args, *exch.tensors)


def _adamw(land, w, m, v, name, tm=256):
    n, r, c = land.shape
    tm = r if r <= tm else max(s for s in range(8, tm + 1, 8) if r % s == 0)
    bc1 = 1.0 / (1.0 - ADAM_B1 ** ADAM_STEP)
    bc2 = 1.0 / (1.0 - ADAM_B2 ** ADAM_STEP)

    def body(l_ref, w_ref, m_ref, v_ref, g_ref, d_ref, nm_ref, nv_ref):
        g = l_ref[0].astype(F32)
        for i in range(1, n):
            g = g + l_ref[i].astype(F32)
        nm = ADAM_B1 * m_ref[...] + (1.0 - ADAM_B1) * g
        nv = ADAM_B2 * v_ref[...] + (1.0 - ADAM_B2) * (g * g)
        g_ref[...] = g
        nm_ref[...] = nm
        nv_ref[...] = nv
        d_ref[...] = -ADAM_LR * ((nm * bc1) / (jnp.sqrt(nv * bc2) + ADAM_EPS) + ADAM_WD * w_ref[...])

    spec = pl.BlockSpec((tm, c), lambda i: (i, 0))
    return pl.pallas_call(
        body, name=name, grid=(r // tm,),
        in_specs=[pl.BlockSpec((n, tm, c), lambda i: (0, i, 0)), spec, spec, spec],
        out_specs=[spec] * 4, out_shape=[jax.ShapeDtypeStruct((r, c), F32)] * 4,
        compiler_params=_cp("parallel"),
    )(land, w, m, v)


PACK_W = 2048


def _pack_rows(parts):
    flat = jnp.concatenate([p.reshape(-1).astype(F32) for p in parts])
    pad = (-flat.shape[0]) % (8 * PACK_W)
    return jnp.pad(flat, (0, pad)).reshape(-1, PACK_W)


def _unpack_rows(packed, shapes):
    flat = packed.reshape(-1)
    out, off = [], 0
    for s in shapes:
        n = math.prod(s)
        out.append(flat[off:off + n].reshape(s))
        off += n
    return out


def _col_slabs(gfull, width):
    r = gfull.shape[0]
    return jnp.transpose(gfull.reshape(r, N_DEV, width), (1, 0, 2)).astype(BF16)


def _row_slabs(gfull):
    return gfull.reshape(N_DEV, gfull.shape[0] // N_DEV, gfull.shape[1]).astype(BF16)


def _from_col_slabs(gathered):
    n, r, width = gathered.shape
    return jnp.transpose(gathered, (1, 0, 2)).reshape(r, n * width)


def _local_step(xs, target, norm_mix, wf_in, cw, a_log, dt_bias, dn_norm, rest, norm_ffn, norm_final,
                distributed=True):
    d = D_MODEL
    n_main = D_PA + 4 * D_B
    w_pa_cols = wf_in[:, :D_PA]
    w_pb_cols = jnp.concatenate([
        wf_in[:, D_PA:n_main],
        jnp.repeat(wf_in[:, n_main:n_main + N_HEADS_B], HEAD, axis=1),
        jnp.repeat(wf_in[:, n_main + N_HEADS_B:n_main + 2 * N_HEADS_B], HEAD, axis=1),
        wf_in[:, n_main + 2 * N_HEADS_B:]], axis=1)
    a_log_bc = jnp.repeat(a_log, HEAD, axis=1)
    dt_bias_bc = jnp.repeat(dt_bias, HEAD, axis=1)

    u = _rms_fwd(xs, norm_mix)
    pa = _matmul(u, w_pa_cols, "nn", F32, 1024, 1536, d, name="proj_a")
    pb = _matmul(u, w_pb_cols, "nn", F32, 1024, 1024, d, name="proj_b")
    os_, ls_ = [], []
    for gi, dil in enumerate(DILATIONS):
        o_g, l_g = _attn_fwd_group(pa, gi, dil)
        os_.append(o_g)
        ls_.append(l_g)
    y_att, lse = _attn_merge(os_, ls_)
    prep = _dn_prep_fwd(pb, cw, a_log_bc, dt_bias_bc, exch=_Exchange(rest, [False] * 6) if distributed else None)
    qn, kn, vn, gdec, beta = prep[:5]
    if distributed:
        g_pa, g_pd, g_out, g_gate, g_up, g_down = prep[5:]
        wf_pa, wf_pd, wf_out = _from_col_slabs(g_pa), g_pd.reshape(D_B, d), g_out.reshape(d, d)
        wf_gate, wf_up, wf_down = _from_col_slabs(g_gate), _from_col_slabs(g_up), g_down.reshape(D_FF, d)
    else:
        wf_pa, wf_pd, wf_out, wf_gate, wf_up, wf_down = rest
    wf_gu = jnp.concatenate([wf_gate, wf_up], axis=1)
    ub, ww, qd, kd, qk, gl = _dn_chunk_fwd(qn, kn, vn, gdec, beta)
    o_dn, states = _dn_scan_fwd(ub, ww, qd, kd, qk, gl)
    o_gated = _head_norm_fwd(o_dn, pb, dn_norm)
    y_a = _matmul(y_att, wf_pa, "nn", F32, 1024, d, D_ATTN_OUT, name="proj_attn")
    y_b = _matmul(o_gated, wf_pd, "nn", F32, 1024, d, d, name="proj_delta")
    merged = _gate_merge_fwd(pb, y_a, y_b)
    h1 = _matmul(merged, wf_out, "nn", F32, 1024, d, d, add=xs, name="out_proj")
    hn = _rms_fwd(h1, norm_ffn)
    gu = _matmul(hn, wf_gu, "nn", BF16, 1024, 1408, d, name="ffn_in")
    act = _swiglu_fwd(gu)
    h2 = _matmul(act, wf_down, "nn", F32, 512, d, D_FF, add=h1, name="ffn_out")
    loss_part, dh2, d_norm_final = _final_loss(h2, norm_final.reshape(1, d), target)

    dact = _matmul(dh2, wf_down, "nt", BF16, 1024, 1408, d, name="d_act")
    gw_down = _matmul(act, dh2, "tn", F32, 1408, d, 512, name="gw_down")
    dgu = _swiglu_bwd(gu, dact)
    dhn = _matmul(dgu, wf_gu, "nt", BF16, 1024, d, 1408, name="d_hn")
    gw_gu = _matmul(hn, dgu, "tn", F32, d, 1408, 512, name="gw_gu")
    dh1, d_norm_ffn = _rms_bwd(h1, norm_ffn, dhn, dh2)
    dmerged = _matmul(dh1, wf_out, "nt", BF16, 1024, d, d, name="d_merged")
    gw_out = _matmul(merged, dh1, "tn", F32, d, d, 512, name="gw_out")
    dya, dyb, dga, dgb = _gate_merge_bwd(pb, y_a, y_b, dmerged)
    dy_att = _matmul(dya, wf_pa, "nt", F32, 1024, D_ATTN_OUT, d, name="d_y_att")
    gw_pa = _matmul(y_att, dya, "tn", F32, D_ATTN_OUT, d, 512, name="gw_pa")
    do_gated = _matmul(dyb, wf_pd, "nt", BF16, 1024, d, d, name="d_o_gated")
    gw_pd = _matmul(o_gated, dyb, "tn", F32, d, d, 512, name="gw_pd")
    do_dn, dz, d_dn_norm = _head_norm_bwd(o_dn, pb, dn_norm, do_gated)
    dub, dww, dqd, dkd, dqk, dgl = _dn_scan_bwd(ub, ww, qd, kd, qk, gl, states, do_dn)
    dqn, dkn, dvn, dgdec, dbeta = _dn_chunk_bwd(qn, kn, vn, gdec, beta, dub, dww, dqd, dkd, dqk, dgl)
    slabs = [_col_slabs(gw_pa, d // N_DEV), _row_slabs(gw_pd), _row_slabs(gw_out),
             _col_slabs(gw_gu[:, :D_FF], D_FF // N_DEV), _col_slabs(gw_gu[:, D_FF:], D_FF // N_DEV),
             _row_slabs(gw_down)] if distributed else None
    prep = _dn_prep_bwd(pb, cw, a_log_bc, dt_bias_bc, gdec, dqn, dkn, dvn, dgdec, dbeta,
                        exch=_Exchange(slabs, [True] * 6) if distributed else None)
    dc, dbeta_raw, da_raw, d_conv_full, d_alog_bc, d_dt_bc = prep[:6]
    dqkv_pre = _conv_bwd_input(dc, cw)
    segs = [_attn_bwd_group(pa, dy_att, y_att, lse, gi, dil) for gi, dil in enumerate(DILATIONS)]
    segs += [dqkv_pre, dz, dbeta_raw, da_raw, dga, dgb]
    gws = [_matmul(u, s, "tn", F32, d, 1536, 512, name=f"gw_in_{i}") for i, s in enumerate(segs)]
    g_att = jnp.concatenate(gws[:3], axis=1).reshape(d, N_HEADS_A, 3, HEAD)
    gw_in = jnp.concatenate(
        [g_att[:, :, i, :].reshape(d, D_ATTN) for i in range(3)]
        + [gws[3], gws[4], gws[5].reshape(d, N_HEADS_B, HEAD).sum(-1), gws[6].reshape(d, N_HEADS_B, HEAD).sum(-1),
           gws[7], gws[8]], axis=1)
    w_att = jnp.stack([w_pa_cols[:, i * D_ATTN:(i + 1) * D_ATTN].reshape(d, N_HEADS_A, HEAD) for i in range(3)],
                      axis=2).reshape(d, D_PA)
    du_first =_matmul_nt_segments(
        segs[:4], jnp.concatenate([w_att, w_pb_cols[:, :3 * D_B]], axis=1), F32, 1024, 1536, "d_u_0",
        exch=_Exchange([_col_slabs(gw_in, SHARD_IN)], [True]) if distributed else None)
    du = _matmul_nt_segments(segs[4:], w_pb_cols[:, 3 * D_B:], BF16, 1024, 1024, "d_u_1", add=du_first[0])[0]
    dx, d_norm_mix = _rms_bwd(xs, norm_mix, du, dh1)
    d_a_log = d_alog_bc.reshape(1, N_HEADS_B, HEAD)[:, :, 0]
    d_dt_bias = d_dt_bc.reshape(1, N_HEADS_B, HEAD)[:, :, 0]
    small = (d_conv_full, d_norm_mix, d_norm_ffn, d_norm_final, d_dn_norm, d_a_log, d_dt_bias)
    if distributed:
        return (loss_part, dx, [du_first[1]] + list(prep[6:])) + small
    return (loss_part, dx, gw_in, gw_pa, gw_pd, gw_out, gw_gu, gw_down) + small


def kernel(x, norm_mix, w_in, conv_w, a_log, dt_bias, dn_norm, w_proj_attn, w_proj_delta, w_out, norm_ffn, w_gate, w_up, w_down, norm_final, loss_target, m_norm_mix, m_w_in, m_conv_w, m_a_log, m_dt_bias, m_dn_norm, m_w_proj_attn, m_w_proj_delta, m_w_out, m_norm_ffn, m_w_gate, m_w_up, m_w_down, m_norm_final, v_norm_mix, v_w_in, v_conv_w, v_a_log, v_dt_bias, v_dn_norm, v_w_proj_attn, v_w_proj_delta, v_w_out, v_norm_ffn, v_w_gate, v_w_up, v_w_down, v_norm_final):
    d = D_MODEL
    xs = x[0]
    target = loss_target[0]
    me = _dev_index(_mesh_pos())

    g_in, g_conv = _exchange([w_in[0].astype(BF16), conv_w[0]], [False] * 2, "gather_w_in")
    rest = [w[0].astype(BF16) for w in (w_proj_attn, w_proj_delta, w_out, w_gate, w_up, w_down)]
    (loss_part, dx, landed, d_conv_full, d_norm_mix, d_norm_ffn, d_norm_final, d_dn_norm, d_a_log,
     d_dt_bias) = _local_step(xs, target, norm_mix, _from_col_slabs(g_in), _from_col_slabs(g_conv), a_log, dt_bias,
                              dn_norm, rest, norm_ffn, norm_final)

    small_shapes = [(1, d), (1, d), (d,), (1, HEAD), (1, N_HEADS_B), (1, N_HEADS_B), (1, 1), (CONV_WIDTH, 3 * D_B)]
    packed = _pack_rows([d_norm_mix, d_norm_ffn, d_norm_final, d_dn_norm, d_a_log, d_dt_bias,
                         loss_part[:, :1], d_conv_full])
    landed = list(landed) + list(_exchange([packed], [False], "gather_small_grads"))
    zero1 = jnp.zeros((1, 1), F32)
    zconv = jnp.zeros((CONV_WIDTH, 3 * D_B), F32)
    small_w = _pack_rows([norm_mix, norm_ffn, norm_final, dn_norm, a_log, dt_bias, zero1, zconv])
    small_m = _pack_rows([m_norm_mix, m_norm_ffn, m_norm_final, m_dn_norm, m_a_log, m_dt_bias, zero1, zconv])
    small_v = _pack_rows([v_norm_mix, v_norm_ffn, v_norm_final, v_dn_norm, v_a_log, v_dt_bias, zero1, zconv])
    small = [_unpack_rows(z, small_shapes) for z in _adamw(landed[7], small_w, small_m, small_v, "adamw_small")]
    loss = small[0][6].reshape(())
    conv_shard = 3 * D_B // N_DEV
    g_conv_own = lax.dynamic_slice_in_dim(small[0][7], me * conv_shard, conv_shard, axis=1)
    r_conv = _adamw(g_conv_own[None], conv_w[0], m_conv_w[0], v_conv_w[0], "adamw_conv")
    big = [_adamw(landed[i], w[0], m[0], v[0], f"adamw_{i}") for i, (w, m, v) in enumerate([
        (w_in, m_w_in, v_w_in), (w_proj_attn, m_w_proj_attn, v_w_proj_attn),
        (w_proj_delta, m_w_proj_delta, v_w_proj_delta), (w_out, m_w_out, v_w_out),
        (w_gate, m_w_gate, v_w_gate), (w_up, m_w_up, v_w_up), (w_down, m_w_down, v_w_down)])]

    def leaves(k):
        sm = small[k]
        return [sm[0], big[0][k][None], r_conv[k][None], sm[4], sm[5], sm[3], big[1][k][None], big[2][k][None],
                big[3][k][None], sm[1], big[4][k][None], big[5][k][None], big[6][k][None], sm[2]]

    return (loss, dx[None], *leaves(0), *leaves(1), *leaves(2), *leaves(3))
```

```python
import math

import jax
import jax.numpy as jnp
from jax import lax
from jax.experimental import pallas as pl
from jax.experimental.pallas import tpu as pltpu

F32 = jnp.float32
BF16 = jnp.bfloat16
HI = lax.Precision.HIGH

D_MODEL = 1024
N_DEV = 8
HEAD = 128
N_HEADS_A = 12
HEADS_PER_GROUP = 4
DILATIONS = (1, 4, 16)
BLOCK_A = 128
D_ATTN = N_HEADS_A * HEAD
D_ATTN_OUT = HEADS_PER_GROUP * HEAD
N_HEADS_B = 8
D_B = N_HEADS_B * HEAD
CONV_WIDTH = 4
CH = 64
CHUNK_GROUP = 8
SCAN_HEADS = 4
D_FF = 2816
EPS = 1e-6
D_IN = 3 * D_ATTN + 4 * D_B + 2 * N_HEADS_B + 2 * D_MODEL
SHARD_IN = D_IN // N_DEV
PB_Z, PB_GATE = 3072, 4096
D_PA = 3 * D_ATTN
ADAM_LR, ADAM_B1, ADAM_B2, ADAM_EPS, ADAM_WD, ADAM_STEP = 0.001, 0.9, 0.999, 1e-08, 0.01, 10
VMEM_LIMIT = 56 * 1024 * 1024

NN = ((1,), (0,))
NT = ((1,), (1,))
TN = ((0,), (0,))


def _dot(a, b, dims=NN, prec=None):
    return lax.dot_general(a, b, (dims, ((), ())), precision=prec, preferred_element_type=F32)


def _bdot(a, b, dims=NN):
    return _dot(a.astype(BF16), b.astype(BF16), dims)


def _hdot(a, b, dims=NN):
    return _dot(a.astype(F32), b.astype(F32), dims, HI)


def _cp(*sem):
    return pltpu.CompilerParams(dimension_semantics=sem, vmem_limit_bytes=VMEM_LIMIT)


def _sigmoid(x):
    return 1.0 / (1.0 + jnp.exp(-x))


def _softplus(x):
    return jnp.maximum(x, 0.0) + jnp.log(1.0 + jnp.exp(-jnp.abs(x)))


def _rowsum(x):
    return jnp.sum(x, axis=-1, keepdims=True)


def _matmul(a, b, mode, out_dtype, tm, tn, tk, add=None, name="mm"):
    if mode == "nn":
        (m, k), (k2, n) = a.shape, b.shape
    elif mode == "nt":
        (m, k), (n, k2) = a.shape, b.shape
    else:
        (k, m), (k2, n) = a.shape, b.shape
    assert k == k2, (a.shape, b.shape, mode)
    tm, tn, tk = min(tm, m), min(tn, n), min(tk, k)
    assert m % tm == 0 and n % tn == 0 and k % tk == 0, (a.shape, b.shape, tm, tn, tk)
    nk = k // tk
    dims = {"nn": NN, "nt": NT, "tn": TN}[mode]

    def body(*refs):
        if add is None:
            a_ref, b_ref, o_ref, acc = refs
            add_ref = None
        else:
            a_ref, b_ref, add_ref, o_ref, acc = refs
        kk = pl.program_id(2)

        @pl.when(kk == 0)
        def _():
            acc[...] = jnp.zeros_like(acc)

        acc[...] += _bdot(a_ref[...], b_ref[...], dims)

        @pl.when(kk == nk - 1)
        def _():
            r = acc[...]
            if add_ref is not None:
                r = r + add_ref[...].astype(F32)
            o_ref[...] = r.astype(out_dtype)

    a_spec = (pl.BlockSpec((tk, tm), lambda i, j, kk: (kk, i)) if mode == "tn"
              else pl.BlockSpec((tm, tk), lambda i, j, kk: (i, kk)))
    b_spec = (pl.BlockSpec((tn, tk), lambda i, j, kk: (j, kk)) if mode == "nt"
              else pl.BlockSpec((tk, tn), lambda i, j, kk: (kk, j)))
    in_specs = [a_spec, b_spec]
    args = [a, b]
    if add is not None:
        in_specs.append(pl.BlockSpec((tm, tn), lambda i, j, kk: (i, j)))
        args.append(add)
    return pl.pallas_call(
        body, name=name, grid=(m // tm, n // tn, nk),
        in_specs=in_specs, out_specs=pl.BlockSpec((tm, tn), lambda i, j, kk: (i, j)),
        out_shape=jax.ShapeDtypeStruct((m, n), out_dtype),
        scratch_shapes=[pltpu.VMEM((tm, tn), F32)],
        compiler_params=_cp("parallel", "parallel", "arbitrary"),
    )(*args)


def _matmul_nt_segments(segs, b, out_dtype, tm, tk, name, add=None, exch=None):
    m = segs[0].shape[0]
    n, ktot = b.shape
    ns = len(segs)
    nks = [s.shape[1] // tk for s in segs]
    assert all(s.shape[1] % tk == 0 for s in segs) and sum(s.shape[1] for s in segs) == ktot and m % tm == 0
    starts = [sum(nks[:i]) for i in range(ns)]
    nk = sum(nks)

    def body(*refs):
        seg_refs, b_ref = refs[:ns], refs[ns]
        add_ref = refs[ns + 1] if add is not None else None
        o_ref, acc = refs[-2], refs[-1]
        kk = pl.program_id(1)

        @pl.when(kk == 0)
        def _():
            acc[...] = jnp.zeros_like(acc)

        for a_ref, k0, nk_s in zip(seg_refs, starts, nks):
            @pl.when(jnp.logical_and(kk >= k0, kk < k0 + nk_s))
            def _(a_ref=a_ref):
                acc[...] += _bdot(a_ref[...], b_ref[...], NT)

        @pl.when(kk == nk - 1)
        def _():
            r = acc[...]
            if add_ref is not None:
                r = r + add_ref[...].astype(F32)
            o_ref[...] = r.astype(out_dtype)

    def seg_spec(k0, nk_s):
        return pl.BlockSpec((tm, tk), lambda i, kk: (i, jnp.clip(kk - k0, 0, nk_s - 1)))

    row = pl.BlockSpec((tm, n), lambda i, kk: (i, 0))
    in_specs = [seg_spec(k0, nk_s) for k0, nk_s in zip(starts, nks)] + [pl.BlockSpec((n, tk), lambda i, kk: (0, kk))]
    args = list(segs) + [b]
    if add is not None:
        in_specs.append(row)
        args.append(add)
    return _hosted_call(body, name=name, grid=(m // tm, nk), in_specs=in_specs, out_specs=[row],
                        out_shape=[jax.ShapeDtypeStruct((m, n), out_dtype)],
                        scratch_shapes=[pltpu.VMEM((tm, n), F32)], args=args,
                        dims=("parallel", "arbitrary"), exch=exch)


def _row_spec(tm, cols, cb=0):
    return pl.BlockSpec((tm, cols), lambda i, cb=cb: (i, cb))


def _bcast_spec(rows, cols):
    return pl.BlockSpec((rows, cols), lambda i: (0, 0))


def _rms_fwd(x, w, tm=512):
    t, d = x.shape

    def body(x_ref, w_ref, o_ref):
        xv = x_ref[...]
        r = lax.rsqrt(jnp.mean(xv * xv, axis=-1, keepdims=True) + EPS)
        o_ref[...] = (xv * r * w_ref[...]).astype(BF16)

    return pl.pallas_call(
        body, name="rms_fwd", grid=(t // tm,),
        in_specs=[_row_spec(tm, d), _bcast_spec(1, d)], out_specs=_row_spec(tm, d),
        out_shape=jax.ShapeDtypeStruct((t, d), BF16), compiler_params=_cp("parallel"),
    )(x, w)


def _rms_bwd(x, w, dy, resid, tm=512):
    t, d = x.shape

    def body(x_ref, w_ref, dy_ref, res_ref, dx_ref, dw_ref):
        xv = x_ref[...]
        r = lax.rsqrt(jnp.mean(xv * xv, axis=-1, keepdims=True) + EPS)
        xh = xv * r
        dyv = dy_ref[...].astype(F32)
        dxh = dyv * w_ref[...]
        dx_ref[...] = res_ref[...] + r * (dxh - xh * jnp.mean(dxh * xh, axis=-1, keepdims=True))

        @pl.when(pl.program_id(0) == 0)
        def _():
            dw_ref[...] = jnp.zeros_like(dw_ref)

        dw_ref[...] += jnp.sum(dyv * xh, axis=0, keepdims=True)

    return pl.pallas_call(
        body, name="rms_bwd", grid=(t // tm,),
        in_specs=[_row_spec(tm, d), _bcast_spec(1, d), _row_spec(tm, d), _row_spec(tm, d)],
        out_specs=[_row_spec(tm, d), _bcast_spec(1, d)],
        out_shape=[jax.ShapeDtypeStruct((t, d), F32), jax.ShapeDtypeStruct((1, d), F32)],
        compiler_params=_cp("arbitrary"),
    )(x, w, dy, resid)


def _final_loss(h, w, target, tm=512):
    t, d = h.shape

    def body(h_ref, w_ref, t_ref, loss_ref, dh_ref, dw_ref):
        hv = h_ref[...]
        r = lax.rsqrt(jnp.mean(hv * hv, axis=-1, keepdims=True) + EPS)
        xh = hv * r
        wv = w_ref[...]
        err = xh * wv - t_ref[...]
        dy = err * (1.0 / d)
        dxh = dy * wv
        dh_ref[...] = r * (dxh - xh * jnp.mean(dxh * xh, axis=-1, keepdims=True))

        @pl.when(pl.program_id(0) == 0)
        def _():
            dw_ref[...] = jnp.zeros_like(dw_ref)
            loss_ref[...] = jnp.zeros_like(loss_ref)

        dw_ref[...] += jnp.sum(dy * xh, axis=0, keepdims=True)
        part = 0.5 * jnp.sum(jnp.mean(err * err, axis=-1, keepdims=True), axis=0, keepdims=True)
        loss_ref[...] += part + jnp.zeros((1, HEAD), F32)

    return pl.pallas_call(
        body, name="final_loss", grid=(t // tm,),
        in_specs=[_row_spec(tm, d), _bcast_spec(1, d), _row_spec(tm, d)],
        out_specs=[_bcast_spec(1, HEAD), _row_spec(tm, d), _bcast_spec(1, d)],
        out_shape=[jax.ShapeDtypeStruct((1, HEAD), F32), jax.ShapeDtypeStruct((t, d), F32),
                   jax.ShapeDtypeStruct((1, d), F32)],
        compiler_params=_cp("arbitrary"),
    )(h, w, target)


def _swiglu_fwd(gu, tm=256):
    t = gu.shape[0]
    ff = gu.shape[1] // 2

    def body(gu_ref, o_ref):
        g = gu_ref[:, :ff].astype(F32)
        o_ref[...] = (g * _sigmoid(g) * gu_ref[:, ff:].astype(F32)).astype(BF16)

    return pl.pallas_call(
        body, name="swiglu_fwd", grid=(t // tm,),
        in_specs=[_row_spec(tm, 2 * ff)], out_specs=_row_spec(tm, ff),
        out_shape=jax.ShapeDtypeStruct((t, ff), BF16), compiler_params=_cp("parallel"),
    )(gu)


def _swiglu_bwd(gu, dact, tm=256):
    t = gu.shape[0]
    ff = gu.shape[1] // 2

    def body(gu_ref, d_ref, o_ref):
        g = gu_ref[:, :ff].astype(F32)
        u = gu_ref[:, ff:].astype(F32)
        dv = d_ref[...].astype(F32)
        sg = _sigmoid(g)
        o_ref[:, :ff] = (dv * u * (sg + g * sg * (1.0 - sg))).astype(BF16)
        o_ref[:, ff:] = (dv * g * sg).astype(BF16)

    return pl.pallas_call(
        body, name="swiglu_bwd", grid=(t // tm,),
        in_specs=[_row_spec(tm, 2 * ff), _row_spec(tm, ff)], out_specs=_row_spec(tm, 2 * ff),
        out_shape=jax.ShapeDtypeStruct((t, 2 * ff), BF16), compiler_params=_cp("parallel"),
    )(gu, dact)


def _gate_merge_fwd(pb, ya, yb, tm=512):
    t, d = ya.shape
    cb = PB_GATE // d

    def body(ga_ref, gb_ref, ya_ref, yb_ref, o_ref):
        o_ref[...] = (_sigmoid(ga_ref[...]) * ya_ref[...] + _sigmoid(gb_ref[...]) * yb_ref[...]).astype(BF16)

    return pl.pallas_call(
        body, name="gate_merge_fwd", grid=(t // tm,),
        in_specs=[_row_spec(tm, d, cb), _row_spec(tm, d, cb + 1), _row_spec(tm, d), _row_spec(tm, d)],
        out_specs=_row_spec(tm, d),
        out_shape=jax.ShapeDtypeStruct((t, d), BF16), compiler_params=_cp("parallel"),
    )(pb, pb, ya, yb)


def _gate_merge_bwd(pb, ya, yb, dm, tm=512):
    t, d = ya.shape
    cb = PB_GATE // d

    def body(ga_ref, gb_ref, ya_ref, yb_ref, dm_ref, dya_ref, dyb_ref, dga_ref, dgb_ref):
        dmv = dm_ref[...].astype(F32)
        sa = _sigmoid(ga_ref[...])
        sb = _sigmoid(gb_ref[...])
        dya_ref[...] = (dmv * sa).astype(BF16)
        dyb_ref[...] = (dmv * sb).astype(BF16)
        dga_ref[...] = (dmv * ya_ref[...] * sa * (1.0 - sa)).astype(BF16)
        dgb_ref[...] = (dmv * yb_ref[...] * sb * (1.0 - sb)).astype(BF16)

    return pl.pallas_call(
        body, name="gate_merge_bwd", grid=(t // tm,),
        in_specs=[_row_spec(tm, d, cb), _row_spec(tm, d, cb + 1), _row_spec(tm, d), _row_spec(tm, d),
                  _row_spec(tm, d)],
        out_specs=[_row_spec(tm, d)] * 4,
        out_shape=[jax.ShapeDtypeStruct((t, d), BF16)] * 4, compiler_params=_cp("parallel"),
    )(pb, pb, ya, yb, dm)


def _head_norm_fwd(o, pb, wn, tm=512):
    t, d = o.shape
    nh = d // HEAD

    def body(o_ref, z_ref, w_ref, out_ref):
        wv = w_ref[...]
        for h in range(nh):
            sl = slice(h * HEAD, (h + 1) * HEAD)
            ov = o_ref[:, sl]
            zv = z_ref[:, sl]
            r = lax.rsqrt(jnp.mean(ov * ov, axis=-1, keepdims=True) + EPS)
            out_ref[:, sl] = (ov * r * wv * (zv * _sigmoid(zv))).astype(BF16)

    return pl.pallas_call(
        body, name="head_norm_fwd", grid=(t // tm,),
        in_specs=[_row_spec(tm, d), _row_spec(tm, d, PB_Z // d), _bcast_spec(1, HEAD)],
        out_specs=_row_spec(tm, d),
        out_shape=jax.ShapeDtypeStruct((t, d), BF16), compiler_params=_cp("parallel"),
    )(o, pb, wn)


def _head_norm_bwd(o, pb, wn, dout, tm=512):
    t, d = o.shape
    nh = d // HEAD

    def body(o_ref, z_ref, w_ref, d_ref, do_ref, dz_ref, dw_ref):
        wv = w_ref[...]
        dw_acc = jnp.zeros((1, HEAD), F32)
        for h in range(nh):
            sl = slice(h * HEAD, (h + 1) * HEAD)
            ov = o_ref[:, sl]
            zv = z_ref[:, sl]
            dv = d_ref[:, sl].astype(F32)
            r = lax.rsqrt(jnp.mean(ov * ov, axis=-1, keepdims=True) + EPS)
            xh = ov * r
            sz = _sigmoid(zv)
            dn = dv * (zv * sz)
            dz_ref[:, sl] = (dv * xh * wv * (sz + zv * sz * (1.0 - sz))).astype(BF16)
            dxh = dn * wv
            do_ref[:, sl] = r * (dxh - xh * jnp.mean(dxh * xh, axis=-1, keepdims=True))
            dw_acc = dw_acc + jnp.sum(dn * xh, axis=0, keepdims=True)

        @pl.when(pl.program_id(0) == 0)
        def _():
            dw_ref[...] = jnp.zeros_like(dw_ref)

        dw_ref[...] += dw_acc

    return pl.pallas_call(
        body, name="head_norm_bwd", grid=(t // tm,),
        in_specs=[_row_spec(tm, d), _row_spec(tm, d, PB_Z // d), _bcast_spec(1, HEAD), _row_spec(tm, d)],
        out_specs=[_row_spec(tm, d), _row_spec(tm, d), _bcast_spec(1, HEAD)],
        out_shape=[jax.ShapeDtypeStruct((t, d), F32), jax.ShapeDtypeStruct((t, d), BF16),
                   jax.ShapeDtypeStruct((1, HEAD), F32)],
        compiler_params=_cp("arbitrary"),
    )(o, pb, wn, dout)


def _attn_bias(gi, hh, dil):
    i = lax.broadcasted_iota(jnp.int32, (BLOCK_A, BLOCK_A), 0)
    j = lax.broadcasted_iota(jnp.int32, (BLOCK_A, BLOCK_A), 1)
    hf = (gi * HEADS_PER_GROUP + hh + 1).astype(F32)
    slope = jnp.exp(jnp.full((1, BLOCK_A), -8.0 * math.log(2.0) / N_HEADS_A, F32) * hf) * float(dil)
    d_prev = (BLOCK_A + i - j).astype(F32)
    d_cur = (i - j).astype(F32)
    return -slope * d_prev, -slope * d_cur, j >= i, j <= i


ATTN_TOKENS = 2048


def _sub_rows(a, r, dil):
    start = a * BLOCK_A * dil + r
    return pl.ds(start, BLOCK_A) if dil == 1 else pl.ds(start, BLOCK_A, stride=dil)


def _attn_fwd_group(pa, gi, dil, tb=ATTN_TOKENS):
    t = pa.shape[0]
    tb = min(tb, t)
    hb = BLOCK_A * dil
    nb = tb // hb
    scale = HEAD ** -0.5

    def body(q_ref, k_ref, v_ref, kp_ref, vp_ref, o_ref, l_ref):
        hh = pl.program_id(0)
        step = pl.program_id(1)
        b_prev, b_cur, m_prev, m_cur = _attn_bias(gi, hh, dil)
        m_first = jnp.logical_and(m_prev, step > 0)
        for r in range(dil):
            kp, vp = kp_ref[_sub_rows(0, r, dil), :], vp_ref[_sub_rows(0, r, dil), :]
            for a in range(nb):
                rows = _sub_rows(a, r, dil)
                q, kc, vc = q_ref[rows, :], k_ref[rows, :], v_ref[rows, :]
                s_p = jnp.where(m_first if a == 0 else m_prev, _bdot(q, kp, NT) * scale + b_prev, -1e30)
                s_c = jnp.where(m_cur, _bdot(q, kc, NT) * scale + b_cur, -1e30)
                m = jnp.maximum(jnp.max(s_p, axis=-1, keepdims=True), jnp.max(s_c, axis=-1, keepdims=True))
                p_p = jnp.exp(s_p - m)
                p_c = jnp.exp(s_c - m)
                den = _rowsum(p_p) + _rowsum(p_c)
                o_ref[rows, :] = (_bdot(p_p, vp) + _bdot(p_c, vc)) / den
                l_ref[rows, :] = (m + jnp.log(den)) + jnp.zeros((BLOCK_A, HEAD), F32)
                kp, vp = kc, vc

    def col(base):
        return lambda hh, s: (s, base + hh)

    def col_prev(base):
        return lambda hh, s: (jnp.maximum(s * nb - 1, 0), base + hh)

    qb, kb, vb = gi * HEADS_PER_GROUP, N_HEADS_A + gi * HEADS_PER_GROUP, 2 * N_HEADS_A + gi * HEADS_PER_GROUP
    ospec = pl.BlockSpec((tb, HEAD), lambda hh, s: (s, hh))
    return pl.pallas_call(
        body, name=f"attn_fwd_g{gi}", grid=(HEADS_PER_GROUP, t // tb),
        in_specs=[pl.BlockSpec((tb, HEAD), col(qb)), pl.BlockSpec((tb, HEAD), col(kb)),
                  pl.BlockSpec((tb, HEAD), col(vb)),
                  pl.BlockSpec((hb, HEAD), col_prev(kb)), pl.BlockSpec((hb, HEAD), col_prev(vb))],
        out_specs=[ospec, ospec],
        out_shape=[jax.ShapeDtypeStruct((t, D_ATTN_OUT), F32)] * 2,
        compiler_params=_cp("parallel", "parallel"),
    )(pa, pa, pa, pa, pa)


def _attn_merge(os, ls, tm=512):
    t, d = os[0].shape

    def body(o0, o1, o2, l0, l1, l2, y_ref, lse_ref):
        a0, a1, a2 = l0[...], l1[...], l2[...]
        m = jnp.maximum(jnp.maximum(a0, a1), a2)
        e0, e1, e2 = jnp.exp(a0 - m), jnp.exp(a1 - m), jnp.exp(a2 - m)
        den = e0 + e1 + e2
        y_ref[...] = (e0 * o0[...] + e1 * o1[...] + e2 * o2[...]) / den
        lse_ref[...] = m + jnp.log(den)

    return pl.pallas_call(
        body, name="attn_merge", grid=(t // tm,),
        in_specs=[_row_spec(tm, d)] * 6, out_specs=[_row_spec(tm, d)] * 2,
        out_shape=[jax.ShapeDtypeStruct((t, d), F32)] * 2,
        compiler_params=_cp("parallel"),
    )(*os, *ls)


def _attn_bwd_group(pa, dy, y, lse, gi, dil, tb=ATTN_TOKENS):
    t = pa.shape[0]
    tb = min(tb, t)
    hb = BLOCK_A * dil
    nb = tb // hb
    nsteps = t // tb
    scale = HEAD ** -0.5

    def body(q_ref, k_ref, v_ref, dy_ref, y_ref, l_ref, kp_ref, vp_ref, d_ref,
             dq_s, dk_s, dv_s, carry_k, carry_v):
        hh = pl.program_id(0)
        step = pl.program_id(1)

        @pl.when(step == 0)
        def _():
            carry_k[...] = jnp.zeros_like(carry_k)
            carry_v[...] = jnp.zeros_like(carry_v)

        b_prev, b_cur, m_prev, m_cur = _attn_bias(gi, hh, dil)
        m_first = jnp.logical_and(m_prev, step < nsteps - 1)
        for r in range(dil):
            halo = _sub_rows(0, r, dil)
            dk_in, dv_in = carry_k[halo, :], carry_v[halo, :]
            kp, vp = kp_ref[halo, :], vp_ref[halo, :]
            prev_rows = None
            dk_pend = dv_pend = None
            for a in range(nb):
                rows = _sub_rows(a, r, dil)
                q, kc, vc = q_ref[rows, :], k_ref[rows, :], v_ref[rows, :]
                dyb, lb = dy_ref[rows, :], l_ref[rows, :]
                delta = _rowsum(dyb * y_ref[rows, :])
                mp = m_first if a == 0 else m_prev
                s = _bdot(q, kp, NT) * scale + b_prev
                p = jnp.where(mp, jnp.exp(jnp.where(mp, s - lb, 0.0)), 0.0)
                ds = p * (_bdot(dyb, vp, NT) - delta)
                dq = _bdot(ds, kp)
                dk_prev, dv_prev = _bdot(ds, q, TN), _bdot(p, dyb, TN)
                if a == 0:
                    carry_k[halo, :] = dk_prev
                    carry_v[halo, :] = dv_prev
                else:
                    dk_s[prev_rows, :] = dk_pend + dk_prev
                    dv_s[prev_rows, :] = dv_pend + dv_prev
                s = _bdot(q, kc, NT) * scale + b_cur
                p = jnp.where(m_cur, jnp.exp(jnp.where(m_cur, s - lb, 0.0)), 0.0)
                ds = p * (_bdot(dyb, vc, NT) - delta)
                dq_s[rows, :] = dq + _bdot(ds, kc)
                dk_pend, dv_pend = _bdot(ds, q, TN), _bdot(p, dyb, TN)
                prev_rows, kp, vp = rows, kc, vc
            dk_s[prev_rows, :] = dk_pend + dk_in
            dv_s[prev_rows, :] = dv_pend + dv_in
        d_ref[:, :HEAD] = (dq_s[...] * scale).astype(BF16)
        d_ref[:, HEAD:2 * HEAD] = (dk_s[...] * scale).astype(BF16)
        d_ref[:, 2 * HEAD:] = dv_s[...].astype(BF16)

    def col(base):
        return lambda hh, s: (nsteps - 1 - s, base + hh)

    def col_prev(base):
        return lambda hh, s: (jnp.maximum((nsteps - 1 - s) * nb - 1, 0), base + hh)

    qb, kb, vb = gi * HEADS_PER_GROUP, N_HEADS_A + gi * HEADS_PER_GROUP, 2 * N_HEADS_A + gi * HEADS_PER_GROUP
    big, small = (tb, HEAD), (hb, HEAD)
    return pl.pallas_call(
        body, name=f"attn_bwd_g{gi}", grid=(HEADS_PER_GROUP, nsteps),
        in_specs=[pl.BlockSpec(big, col(qb)), pl.BlockSpec(big, col(kb)), pl.BlockSpec(big, col(vb)),
                  pl.BlockSpec(big, col(0)), pl.BlockSpec(big, col(0)), pl.BlockSpec(big, col(0)),
                  pl.BlockSpec(small, col_prev(kb)), pl.BlockSpec(small, col_prev(vb))],
        out_specs=pl.BlockSpec((tb, 3 * HEAD), col(0)),
        out_shape=jax.ShapeDtypeStruct((t, 3 * D_ATTN_OUT), BF16),
        scratch_shapes=[pltpu.VMEM(big, F32)] * 3 + [pltpu.VMEM(small, F32)] * 2,
        compiler_params=_cp("parallel", "arbitrary"),
    )(pa, pa, pa, dy, y, lse, pa, pa)


def _shift_down(cur, prev8, s):
    if s == 0:
        return cur
    rolled = pltpu.roll(cur, s, 0)
    prolled = pltpu.roll(prev8, s, 0)
    rid = lax.broadcasted_iota(jnp.int32, prev8.shape, 0)
    top = jnp.where(rid < s, prolled, rolled[:8])
    return jnp.concatenate([top, rolled[8:]], axis=0)


def _shift_up(cur, next8, s):
    if s == 0:
        return cur
    n = cur.shape[0]
    rolled = pltpu.roll(cur, n - s, 0)
    nrolled = pltpu.roll(next8, 8 - s, 0)
    rid = lax.broadcasted_iota(jnp.int32, next8.shape, 0)
    bottom = jnp.where(rid >= 8 - s, nrolled, rolled[n - 8:])
    return jnp.concatenate([rolled[:n - 8], bottom], axis=0)


def _conv(xv, prev8, wv):
    c = jnp.zeros_like(xv)
    shifted = []
    for s in range(CONV_WIDTH):
        xs = _shift_down(xv, prev8, s)
        shifted.append(xs)
        c = c + wv[CONV_WIDTH - 1 - s:CONV_WIDTH - s, :] * xs
    return c, shifted


def _head_expand():
    r = lax.broadcasted_iota(jnp.int32, (HEAD, D_B), 0)
    c = lax.broadcasted_iota(jnp.int32, (HEAD, D_B), 1) >> 7
    return (r == c).astype(F32), (r == c + N_HEADS_B).astype(F32)


def _dn_prep_fwd(pb, ps, conv_w, a_log_bc, dt_bias_bc, tm=256, exch=None):
    t = pb.shape[0]
    c3 = 3 * D_B
    r8 = tm // 8

    def body(x_ref, xp_ref, ps_ref, w_ref, al_ref, dt_ref, q_ref, k_ref, v_ref, g_ref, beta_ref):
        prev8 = jnp.where(pl.program_id(0) > 0, xp_ref[...], 0.0)
        c, _ = _conv(x_ref[...], prev8, w_ref[...])
        s = c * _sigmoid(c)
        for h in range(N_HEADS_B):
            sl = slice(h * HEAD, (h + 1) * HEAD)
            sq = s[:, h * HEAD:(h + 1) * HEAD]
            q_ref[:, sl] = sq * lax.rsqrt(_rowsum(sq * sq) + EPS) * (HEAD ** -0.5)
            sk = s[:, D_B + h * HEAD:D_B + (h + 1) * HEAD]
            k_ref[:, sl] = sk * lax.rsqrt(_rowsum(sk * sk) + EPS)
        v_ref[...] = s[:, 2 * D_B:]
        e_b, e_a = _head_expand()
        psv = ps_ref[...]
        beta_ref[...] = _sigmoid(_hdot(psv, e_b))
        g_ref[...] = -jnp.exp(al_ref[...]) * _softplus(_hdot(psv, e_a) + dt_ref[...])

    return _hosted_call(
        body, name="dn_prep_fwd", grid=(t // tm,),
        in_specs=[_row_spec(tm, c3, 0),
                  pl.BlockSpec((8, c3), lambda i: (jnp.maximum(i * r8 - 1, 0), 0)),
                  _row_spec(tm, HEAD),
                  _bcast_spec(CONV_WIDTH, c3), _bcast_spec(1, D_B), _bcast_spec(1, D_B)],
        out_specs=[_row_spec(tm, D_B)] * 5,
        out_shape=[jax.ShapeDtypeStruct((t, D_B), F32)] * 5, scratch_shapes=[],
        args=(pb, pb, ps, conv_w, a_log_bc, dt_bias_bc), dims=("parallel",), exch=exch)


def _dn_prep_bwd(pb, ps, conv_w, a_log_bc, dt_bias_bc, g, dq, dk, dv, dg, dbeta, tm=128, exch=None):
    t = pb.shape[0]
    c3 = 3 * D_B
    r8 = tm // 8

    def body(x_ref, xp_ref, ps_ref, w_ref, al_ref, dt_ref, g_ref, dq_ref, dk_ref, dv_ref, dg_ref, db_ref,
             dc_ref, dps_ref, dw_ref, dal_ref, ddt_ref):
        prev8 = jnp.where(pl.program_id(0) > 0, xp_ref[...], 0.0)
        c, shifted = _conv(x_ref[...], prev8, w_ref[...])
        sg = _sigmoid(c)
        s = c * sg
        dsilu = sg + c * sg * (1.0 - sg)
        for h in range(N_HEADS_B):
            sl = slice(h * HEAD, (h + 1) * HEAD)
            for base, d_ref, mult in ((0, dq_ref, HEAD ** -0.5), (D_B, dk_ref, 1.0)):
                ssl = slice(base + h * HEAD, base + (h + 1) * HEAD)
                sv = s[:, ssl]
                r = lax.rsqrt(_rowsum(sv * sv) + EPS)
                yh = sv * r
                dyv = d_ref[:, sl] * mult
                dc_ref[:, ssl] = r * (dyv - yh * _rowsum(dyv * yh)) * dsilu[:, ssl]
        dc_ref[:, 2 * D_B:] = dv_ref[...] * dsilu[:, 2 * D_B:]
        dcv = dc_ref[...]

        @pl.when(pl.program_id(0) == 0)
        def _():
            dw_ref[...] = jnp.zeros_like(dw_ref)
            dal_ref[...] = jnp.zeros_like(dal_ref)
            ddt_ref[...] = jnp.zeros_like(ddt_ref)

        for sft in range(CONV_WIDTH):
            j = CONV_WIDTH - 1 - sft
            dw_ref[j:j + 1, :] += jnp.sum(dcv * shifted[sft], axis=0, keepdims=True)
        e_b, e_a = _head_expand()
        psv = ps_ref[...]
        beta = _sigmoid(_hdot(psv, e_b))
        dgv = dg_ref[...]
        da = dgv * (-jnp.exp(al_ref[...])) * _sigmoid(_hdot(psv, e_a) + dt_ref[...])
        dps_ref[...] = (_hdot(db_ref[...] * beta * (1.0 - beta), e_b, NT) + _hdot(da, e_a, NT)) * (1.0 / HEAD)
        dal_ref[...] += jnp.sum(dgv * g_ref[...], axis=0, keepdims=True)
        ddt_ref[...] += jnp.sum(da, axis=0, keepdims=True)

    row = _row_spec(tm, D_B)
    return _hosted_call(
        body, name="dn_prep_bwd", grid=(t // tm,),
        in_specs=[_row_spec(tm, c3, 0),
                  pl.BlockSpec((8, c3), lambda i: (jnp.maximum(i * r8 - 1, 0), 0)),
                  _row_spec(tm, HEAD),
                  _bcast_spec(CONV_WIDTH, c3), _bcast_spec(1, D_B), _bcast_spec(1, D_B),
                  row, row, row, row, row, row],
        out_specs=[_row_spec(tm, c3), _row_spec(tm, HEAD), _bcast_spec(CONV_WIDTH, c3), _bcast_spec(1, D_B),
                   _bcast_spec(1, D_B)],
        out_shape=[jax.ShapeDtypeStruct((t, c3), F32), jax.ShapeDtypeStruct((t, HEAD), F32),
                   jax.ShapeDtypeStruct((CONV_WIDTH, c3), F32),
                   jax.ShapeDtypeStruct((1, D_B), F32), jax.ShapeDtypeStruct((1, D_B), F32)],
        scratch_shapes=[], args=(pb, pb, ps, conv_w, a_log_bc, dt_bias_bc, g, dq, dk, dv, dg, dbeta),
        dims=("arbitrary",), exch=exch)


def _conv_bwd_input(dc, conv_w, tm=256):
    t, c3 = dc.shape
    r8 = tm // 8
    nlast = t // 8 - 1
    nsteps = t // tm

    def body(d_ref, dn_ref, w_ref, o_ref):
        next8 = jnp.where(pl.program_id(0) < nsteps - 1, dn_ref[...], 0.0)
        dv = d_ref[...]
        wv = w_ref[...]
        acc = jnp.zeros_like(dv)
        for s in range(CONV_WIDTH):
            acc = acc + wv[CONV_WIDTH - 1 - s:CONV_WIDTH - s, :] * _shift_up(dv, next8, s)
        o_ref[...] = acc.astype(BF16)

    return pl.pallas_call(
        body, name="conv_bwd_input", grid=(nsteps,),
        in_specs=[_row_spec(tm, c3), pl.BlockSpec((8, c3), lambda i: (jnp.minimum((i + 1) * r8, nlast), 0)),
                  _bcast_spec(CONV_WIDTH, c3)],
        out_specs=_row_spec(tm, c3),
        out_shape=jax.ShapeDtypeStruct((t, c3), BF16), compiler_params=_cp("parallel"),
    )(dc, dc, conv_w)


def _lanes(x):
    return x[:, :CH]


def _tri_inv(a_list, r, c):
    eye = (r == c).astype(F32)
    b16 = (r >> 4) == (c >> 4)
    b32 = (r >> 5) == (c >> 5)
    ns = [jnp.where(b16, -a, 0.0) for a in a_list]
    xs = [eye + n for n in ns]
    ps = [_bdot(n, n) for n in ns]
    for last in (False, False, True):
        xs = [x + _bdot(x, p) for x, p in zip(xs, ps)]
        if not last:
            ps = [_bdot(p, p) for p in ps]
    for mask in (jnp.logical_and(b32, jnp.logical_not(b16)), jnp.logical_not(b32)):
        ts = [_bdot(x, jnp.where(mask, a, 0.0)) for x, a in zip(xs, a_list)]
        xs = [x - _bdot(t, x) for x, t in zip(xs, ts)]
    return xs


def _chunk_local(qs, ks, vs, gs, betas):
    r = lax.broadcasted_iota(jnp.int32, (CH, CH), 0)
    c = lax.broadcasted_iota(jnp.int32, (CH, CH), 1)
    incl, strict = r >= c, r > c
    lm = incl.astype(F32)
    cums = [_hdot(lm, jnp.concatenate([g, jnp.where(strict, _lanes(g), 0.0)], axis=1)) for g in gs]
    gcbs = [cm[:, :HEAD] for cm in cums]
    decays = [jnp.where(incl, jnp.exp(jnp.where(incl, cm[:, HEAD:], 0.0)), 0.0) for cm in cums]
    bcols = [_lanes(b) for b in betas]
    kks = [_bdot(k, k, NT) for k in ks]
    qkraws = [_bdot(q, k, NT) for q, k in zip(qs, ks)]
    tms = _tri_inv([jnp.where(strict, bc * kk * dc, 0.0) for bc, kk, dc in zip(bcols, kks, decays)], r, c)
    egs = [jnp.exp(gcb) for gcb in gcbs]
    sols = [_hdot(tm, jnp.concatenate([b * v, b * eg * k], axis=1))
            for tm, b, v, eg, k in zip(tms, betas, vs, egs, ks)]
    gls = [gcb[CH - 1:CH, :] for gcb in gcbs]
    eks = [jnp.exp(gl - gcb) for gl, gcb in zip(gls, gcbs)]
    return [dict(incl=incl, strict=strict, r=r, c=c, decay=dc, bcol=bc, kk=kk, tm=tm, eg=eg,
                 u_bar=sol[:, :HEAD], w=sol[:, HEAD:], qkraw=qkraw, gl=gl, ek=ek)
            for dc, bc, kk, tm, eg, sol, qkraw, gl, ek in zip(decays, bcols, kks, tms, egs, sols, qkraws, gls, eks)]


def _dn_chunk_fwd(q, k, v, g, beta, cps=8):
    t = q.shape[0]
    tm = cps * CH

    def body(q_ref, k_ref, v_ref, g_ref, b_ref, ub_ref, w_ref, qd_ref, kd_ref, qk_ref, gl_ref):
        for base in range(0, cps, CHUNK_GROUP):
            sls = [slice((base + j) * CH, (base + j + 1) * CH) for j in range(CHUNK_GROUP)]
            qs, ks = [q_ref[sl, :] for sl in sls], [k_ref[sl, :] for sl in sls]
            locs = _chunk_local(qs, ks, [v_ref[sl, :] for sl in sls], [g_ref[sl, :] for sl in sls],
                                [b_ref[sl, :] for sl in sls])
            for j, (sl, qv, kv, loc) in enumerate(zip(sls, qs, ks, locs)):
                ub_ref[sl, :] = loc["u_bar"]
                w_ref[sl, :] = loc["w"]
                qd_ref[sl, :] = qv * loc["eg"]
                kd_ref[sl, :] = kv * loc["ek"]
                qk_ref[sl, :] = loc["qkraw"] * loc["decay"]
                gl_ref[base + j:base + j + 1, :] = jnp.exp(loc["gl"])

    hspec = pl.BlockSpec((tm, HEAD), lambda h, i: (i, h))
    return pl.pallas_call(
        body, name="dn_chunk_fwd", grid=(N_HEADS_B, t // tm),
        in_specs=[hspec] * 5,
        out_specs=[hspec] * 4 + [pl.BlockSpec((None, tm, CH), lambda h, i: (h, i, 0)),
                                 pl.BlockSpec((cps, HEAD), lambda h, i: (i, h))],
        out_shape=[jax.ShapeDtypeStruct((t, D_B), F32)] * 4
        + [jax.ShapeDtypeStruct((N_HEADS_B, t, CH), F32), jax.ShapeDtypeStruct((t // CH, D_B), F32)],
        compiler_params=_cp("parallel", "parallel"),
    )(q, k, v, g, beta)


def _dn_scan_fwd(ub, w, qd, kd, qk, gl, cps=8):
    t = ub.shape[0]
    tm = cps * CH

    hg = SCAN_HEADS
    hs = list(range(hg))

    def body(ub_ref, w_ref, qd_ref, kd_ref, qk_ref, gl_ref, o_ref, st_ref, s_acc):
        @pl.when(pl.program_id(1) == 0)
        def _():
            s_acc[...] = jnp.zeros_like(s_acc)

        for ci in range(cps):
            sl = slice(ci * CH, (ci + 1) * CH)
            cols = [slice(h * HEAD, (h + 1) * HEAD) for h in hs]
            svs = [s_acc[h] for h in hs]
            for h in hs:
                st_ref[h, ci * HEAD:(ci + 1) * HEAD, :] = svs[h]
            us = [ub_ref[sl, cols[h]] - _bdot(w_ref[sl, cols[h]], svs[h]) for h in hs]
            for h in hs:
                s_acc[h] = gl_ref[ci:ci + 1, cols[h]] * svs[h] + _bdot(kd_ref[sl, cols[h]], us[h], TN)
            for h in hs:
                o_ref[sl, cols[h]] = _bdot(qd_ref[sl, cols[h]], svs[h]) + _bdot(qk_ref[h, sl, :], us[h])

    hspec = pl.BlockSpec((tm, hg * HEAD), lambda h, i: (i, h))
    return pl.pallas_call(
        body, name="dn_scan_fwd", grid=(N_HEADS_B // hg, t // tm),
        in_specs=[hspec] * 4 + [pl.BlockSpec((hg, tm, CH), lambda h, i: (h, i, 0)),
                                pl.BlockSpec((cps, hg * HEAD), lambda h, i: (i, h))],
        out_specs=[hspec, pl.BlockSpec((hg, cps * HEAD, HEAD), lambda h, i: (h, i, 0))],
        out_shape=[jax.ShapeDtypeStruct((t, D_B), F32),
                   jax.ShapeDtypeStruct((N_HEADS_B, (t // CH) * HEAD, HEAD), F32)],
        scratch_shapes=[pltpu.VMEM((hg, HEAD, HEAD), F32)],
        compiler_params=_cp("parallel", "arbitrary"),
    )(ub, w, qd, kd, qk, gl)


def _dn_scan_bwd(ub, w, qd, kd, qk, gl, st, do, cps=8):
    t = ub.shape[0]
    tm = cps * CH
    ns = t // tm

    hg = SCAN_HEADS
    hs = list(range(hg))

    def body(ub_ref, w_ref, qd_ref, kd_ref, qk_ref, gl_ref, st_ref, do_ref,
             dub_ref, dw_ref, dqd_ref, dkd_ref, dqk_ref, dgl_ref, ds_acc):
        @pl.when(pl.program_id(1) == 0)
        def _():
            ds_acc[...] = jnp.zeros_like(ds_acc)

        for ci in reversed(range(cps)):
            sl = slice(ci * CH, (ci + 1) * CH)
            cols = [slice(h * HEAD, (h + 1) * HEAD) for h in hs]
            svs = [st_ref[h, ci * HEAD:(ci + 1) * HEAD, :] for h in hs]
            wvs = [w_ref[sl, cols[h]] for h in hs]
            dovs = [do_ref[sl, cols[h]] for h in hs]
            dsvs = [ds_acc[h] for h in hs]
            us = [ub_ref[sl, cols[h]] - _bdot(wvs[h], svs[h]) for h in hs]
            dus = [_bdot(kd_ref[sl, cols[h]], dsvs[h]) + _bdot(qk_ref[h, sl, :], dovs[h], TN) for h in hs]
            for h in hs:
                ds_acc[h] = (gl_ref[ci:ci + 1, cols[h]] * dsvs[h] + _bdot(qd_ref[sl, cols[h]], dovs[h], TN)
                             - _bdot(wvs[h], dus[h], TN))
            for h in hs:
                dgl_ref[ci:ci + 1, cols[h]] = (jnp.sum(_rowsum(dsvs[h] * svs[h]), axis=0, keepdims=True)
                                              + jnp.zeros((1, HEAD), F32))
                dkd_ref[sl, cols[h]] = _bdot(us[h], dsvs[h], NT)
                dqd_ref[sl, cols[h]] = _bdot(dovs[h], svs[h], NT)
                dqk_ref[h, sl, :] = _bdot(dovs[h], us[h], NT)
                dub_ref[sl, cols[h]] = dus[h]
                dw_ref[sl, cols[h]] = -_bdot(dus[h], svs[h], NT)

    hspec = pl.BlockSpec((tm, hg * HEAD), lambda h, i: (ns - 1 - i, h))
    qkspec = pl.BlockSpec((hg, tm, CH), lambda h, i: (h, ns - 1 - i, 0))
    glspec = pl.BlockSpec((cps, hg * HEAD), lambda h, i: (ns - 1 - i, h))
    return pl.pallas_call(
        body, name="dn_scan_bwd", grid=(N_HEADS_B // hg, ns),
        in_specs=[hspec] * 4 + [qkspec, glspec,
                                pl.BlockSpec((hg, cps * HEAD, HEAD), lambda h, i: (h, ns - 1 - i, 0)), hspec],
        out_specs=[hspec] * 4 + [qkspec, glspec],
        out_shape=[jax.ShapeDtypeStruct((t, D_B), F32)] * 4
        + [jax.ShapeDtypeStruct((N_HEADS_B, t, CH), F32), jax.ShapeDtypeStruct((t // CH, D_B), F32)],
        scratch_shapes=[pltpu.VMEM((hg, HEAD, HEAD), F32)],
        compiler_params=_cp("parallel", "arbitrary"),
    )(ub, w, qd, kd, qk, gl, st, do)


def _dn_chunk_bwd(q, k, v, g, beta, dub, dw, dqd, dkd, dqk, dgl, cps=8):
    t = q.shape[0]
    tm = cps * CH

    def body(q_ref, k_ref, v_ref, g_ref, b_ref, dub_ref, dw_ref, dqd_ref, dkd_ref, dqk_ref, dgl_ref,
             dq_ref, dk_ref, dv_ref, dg_ref, db_ref):
        ones = jnp.ones((CH, HEAD), F32)
        rid = lax.broadcasted_iota(jnp.int32, (CH, HEAD), 0)

        def rest(ci, sl, qv, kv, vv, beta_v, loc, dr, da):
            incl = loc["incl"]
            eg, ek, decay, bcol, kk = loc["eg"], loc["ek"], loc["decay"], loc["bcol"], loc["kk"]
            drv, drk = dr[:, :HEAD], dr[:, HEAD:]
            dv_ref[sl, :] = beta_v * drv
            beg = beta_v * eg
            t1 = drk * kv
            dbeta = _rowsum(drv * vv + t1 * eg) + _rowsum(da * kk * decay)
            dkk = da * bcol * decay
            dqk_m = jnp.where(incl, dqk_ref[sl, :], 0.0)
            ddecay = da * bcol * kk + dqk_m * loc["qkraw"]
            dqkraw = dqk_m * decay
            dqdv, dkdv = dqd_ref[sl, :], dkd_ref[sl, :]
            dq_ref[sl, :] = _bdot(dqkraw, kv) + dqdv * eg
            dk_ref[sl, :] = (beg * drk + _bdot(dqkraw, qv, TN) + _bdot(dkk, kv) + _bdot(dkk, kv, TN)
                             + dkdv * ek)
            e = ddecay * decay
            skd = _rowsum(dkdv * kv * ek)
            dgc = _rowsum(beg * t1) + _rowsum(e) + _rowsum(dqdv * qv * eg) - skd
            colsum = _hdot(e, ones, TN)
            last = jnp.sum(skd, axis=0, keepdims=True) + dgl_ref[ci:ci + 1, :] * jnp.exp(loc["gl"])
            db_ref[sl, :] = dbeta + jnp.zeros((CH, HEAD), F32)
            return (dgc - colsum) + jnp.where(rid == CH - 1, last, 0.0)

        for base in range(0, cps, CHUNK_GROUP):
            cis = list(range(base, base + CHUNK_GROUP))
            sls = [slice(ci * CH, (ci + 1) * CH) for ci in cis]
            qs, ks, vs = [q_ref[sl, :] for sl in sls], [k_ref[sl, :] for sl in sls], [v_ref[sl, :] for sl in sls]
            betas = [b_ref[sl, :] for sl in sls]
            locs = _chunk_local(qs, ks, vs, [g_ref[sl, :] for sl in sls], betas)
            drs = [_hdot(loc["tm"], jnp.concatenate([dub_ref[sl, :], dw_ref[sl, :]], axis=1), TN)
                   for loc, sl in zip(locs, sls)]
            das = [jnp.where(loc["strict"],
                             -_hdot(dr, jnp.concatenate([loc["u_bar"], loc["w"]], axis=1), NT), 0.0)
                   for loc, dr in zip(locs, drs)]
            dgcs = [rest(*args) for args in zip(cis, sls, qs, ks, vs, betas, locs, drs, das)]
            um = (locs[0]["r"] <= locs[0]["c"]).astype(F32)
            for sl, dgc_bc in zip(sls, dgcs):
                dg_ref[sl, :] = _hdot(um, dgc_bc)

    hspec = pl.BlockSpec((tm, HEAD), lambda h, i: (i, h))
    return pl.pallas_call(
        body, name="dn_chunk_bwd", grid=(N_HEADS_B, t // tm),
        in_specs=[hspec] * 9 + [pl.BlockSpec((None, tm, CH), lambda h, i: (h, i, 0)),
                                pl.BlockSpec((cps, HEAD), lambda h, i: (i, h))],
        out_specs=[hspec] * 5,
        out_shape=[jax.ShapeDtypeStruct((t, D_B), F32)] * 5,
        compiler_params=_cp("parallel", "parallel"),
    )(q, k, v, g, beta, dub, dw, dqd, dkd, dqk, dgl)


FLIPS = [(fx, fy, fc) for fx in (0, 1) for fy in (0, 1) for fc in (0, 1)][1:]


def _mesh_pos():
    return lax.axis_index("x"), lax.axis_index("y"), lax.axis_index("c")


def _peer(pos, flip):
    return tuple((1 - p) if f else p for p, f in zip(pos, flip))


def _dev_index(pos):
    return 4 * pos[0] + 2 * pos[1] + pos[2]


class _Exchange:
    def __init__(self, tensors, scatter):
        self.tensors, self.scatter, self.nt = list(tensors), list(scatter), len(tensors)
        hbm = pl.BlockSpec(memory_space=pltpu.HBM)
        self.in_specs = [hbm] * self.nt
        self.out_specs = [hbm] * self.nt
        self.out_shape = [jax.ShapeDtypeStruct(x.shape if sc else (N_DEV,) + x.shape, x.dtype)
                          for x, sc in zip(tensors, scatter)]
        self.scratch_shapes = [pltpu.SemaphoreType.DMA((self.nt * 7,)), pltpu.SemaphoreType.DMA((self.nt * 7,)),
                               pltpu.SemaphoreType.DMA((self.nt,))]

    def _copies(self, ins, outs, sems):
        send_sems, recv_sems, local_sems = sems
        pos = _mesh_pos()
        me = _dev_index(pos)

        def remote(ti, fi, landing):
            peer = _peer(pos, FLIPS[fi])
            src = ins[ti].at[_dev_index(peer)] if self.scatter[ti] else ins[ti]
            return pltpu.make_async_remote_copy(
                src_ref=src, dst_ref=outs[ti].at[landing(peer)],
                send_sem=send_sems.at[ti * 7 + fi], recv_sem=recv_sems.at[ti * 7 + fi],
                device_id=peer, device_id_type=pl.DeviceIdType.MESH)

        pairs = [(ti, fi) for ti in range(self.nt) for fi in range(7)]
        local = [pltpu.make_async_copy(ins[ti].at[me] if self.scatter[ti] else ins[ti], outs[ti].at[me],
                                       local_sems.at[ti]) for ti in range(self.nt)]
        sends = [remote(ti, fi, lambda peer: me) for ti, fi in pairs]
        recvs = [remote(ti, fi, _dev_index) for ti, fi in pairs]
        return local, sends, recvs

    def start(self, ins, outs, sems):
        local, sends, _ = self._copies(ins, outs, sems)
        for cp in local + sends:
            cp.start()

    def wait(self, ins, outs, sems):
        local, sends, recvs = self._copies(ins, outs, sems)
        for cp in recvs:
            cp.wait_recv()
        for cp in sends:
            cp.wait_send()
        for cp in local:
            cp.wait()


def _exchange(tensors, scatter, name):
    ex = _Exchange(tensors, scatter)

    def body(*refs):
        ins, outs, sems = refs[:ex.nt], refs[ex.nt:2 * ex.nt], refs[2 * ex.nt:]
        ex.start(ins, outs, sems)
        ex.wait(ins, outs, sems)

    return pl.pallas_call(
        body, name=name, in_specs=ex.in_specs, out_specs=ex.out_specs, out_shape=ex.out_shape,
        scratch_shapes=ex.scratch_shapes, compiler_params=pltpu.CompilerParams(has_side_effects=True),
    )(*tensors)


def _hosted_call(body, *, name, grid, in_specs, out_specs, out_shape, scratch_shapes, args, dims, exch=None):
    if exch is None:
        return pl.pallas_call(body, name=name, grid=grid, in_specs=in_specs, out_specs=out_specs,
                              out_shape=out_shape, scratch_shapes=scratch_shapes,
                              compiler_params=_cp(*dims))(*args)
    n_in, n_out, n_sc, ne = len(in_specs), len(out_specs), len(scratch_shapes), exch.nt
    nsteps = math.prod(grid)

    def wrapped(*refs):
        ins, ex_in = refs[:n_in], refs[n_in:n_in + ne]
        outs = refs[n_in + ne:n_in + ne + n_out]
        ex_out = refs[n_in + ne + n_out:n_in + 2 * ne + n_out]
        rest = refs[n_in + 2 * ne + n_out:]
        scratch, sems = rest[:n_sc], rest[n_sc:]
        step = pl.program_id(0)
        for ax in range(1, len(grid)):
            step = step * grid[ax] + pl.program_id(ax)

        @pl.when(step == 0)
        def _():
            exch.start(ex_in, ex_out, sems)

        body(*ins, *outs, *scratch)

        @pl.when(step == nsteps - 1)
        def _():
            exch.wait(ex_in, ex_out, sems)

    return pl.pallas_call(
        wrapped, name=name, grid=grid, in_specs=list(in_specs) + exch.in_specs,
        out_specs=list(out_specs) + exch.out_specs, out_shape=list(out_shape) + exch.out_shape,
        scratch_shapes=list(scratch_shapes) + exch.scratch_shapes,
        compiler_params=pltpu.CompilerParams(dimension_semantics=("arbitrary",) * len(grid),
                                             vmem_limit_bytes=VMEM_LIMIT, has_side_effects=True),
    )(*args, *exch.tensors)


def _adamw(land, w, m, v, name, tm=256):
    n, r, c = land.shape
    tm = r if r <= tm else max(s for s in range(8, tm + 1, 8) if r % s == 0)
    bc1 = 1.0 / (1.0 - ADAM_B1 ** ADAM_STEP)
    bc2 = 1.0 / (1.0 - ADAM_B2 ** ADAM_STEP)

    def body(l_ref, w_ref, m_ref, v_ref, g_ref, d_ref, nm_ref, nv_ref):
        g = l_ref[0].astype(F32)
        for i in range(1, n):
            g = g + l_ref[i].astype(F32)
        nm = ADAM_B1 * m_ref[...] + (1.0 - ADAM_B1) * g
        nv = ADAM_B2 * v_ref[...] + (1.0 - ADAM_B2) * (g * g)
        g_ref[...] = g
        nm_ref[...] = nm
        nv_ref[...] = nv
        d_ref[...] = -ADAM_LR * ((nm * bc1) / (jnp.sqrt(nv * bc2) + ADAM_EPS) + ADAM_WD * w_ref[...])

    spec = pl.BlockSpec((tm, c), lambda i: (i, 0))
    return pl.pallas_call(
        body, name=name, grid=(r // tm,),
        in_specs=[pl.BlockSpec((n, tm, c), lambda i: (0, i, 0)), spec, spec, spec],
        out_specs=[spec] * 4, out_shape=[jax.ShapeDtypeStruct((r, c), F32)] * 4,
        compiler_params=_cp("parallel"),
    )(land, w, m, v)


PACK_W = 2048


def _pack_rows(parts):
    flat = jnp.concatenate([p.reshape(-1).astype(F32) for p in parts])
    pad = (-flat.shape[0]) % (8 * PACK_W)
    return jnp.pad(flat, (0, pad)).reshape(-1, PACK_W)


def _unpack_rows(packed, shapes):
    flat = packed.reshape(-1)
    out, off = [], 0
    for s in shapes:
        n = math.prod(s)
        out.append(flat[off:off + n].reshape(s))
        off += n
    return out


def _col_slabs(gfull, width):
    r = gfull.shape[0]
    return jnp.transpose(gfull.reshape(r, N_DEV, width), (1, 0, 2)).astype(BF16)


def _row_slabs(gfull):
    return gfull.reshape(N_DEV, gfull.shape[0] // N_DEV, gfull.shape[1]).astype(BF16)


def _from_col_slabs(gathered):
    n, r, width = gathered.shape
    return jnp.transpose(gathered, (1, 0, 2)).reshape(r, n * width)


def _local_step(xs, target, norm_mix, wf_in, cw, a_log, dt_bias, dn_norm, rest, norm_ffn, norm_final,
                distributed=True):
    d = D_MODEL
    n_main = D_PA + 4 * D_B
    w_pa_cols = wf_in[:, :D_PA]
    w_pb_cols = jnp.concatenate([wf_in[:, D_PA:n_main], wf_in[:, n_main + 2 * N_HEADS_B:]], axis=1)
    w_small = jnp.pad(wf_in[:, n_main:n_main + 2 * N_HEADS_B], ((0, 0), (0, HEAD - 2 * N_HEADS_B)))
    a_log_bc = jnp.repeat(a_log, HEAD, axis=1)
    dt_bias_bc = jnp.repeat(dt_bias, HEAD, axis=1)

    u = _rms_fwd(xs, norm_mix)
    pa = _matmul(u, w_pa_cols, "nn", F32, 1024, 1536, d, name="proj_a")
    pb = _matmul(u, w_pb_cols, "nn", F32, 1024, 1024, d, name="proj_b")
    ps = _matmul(u, w_small, "nn", F32, 2048, HEAD, d, name="proj_small")
    os_, ls_ = [], []
    for gi, dil in enumerate(DILATIONS):
        o_g, l_g = _attn_fwd_group(pa, gi, dil)
        os_.append(o_g)
        ls_.append(l_g)
    y_att, lse = _attn_merge(os_, ls_)
    prep = _dn_prep_fwd(pb, ps, cw, a_log_bc, dt_bias_bc,
                        exch=_Exchange(rest, [False] * 6) if distributed else None)
    qn, kn, vn, gdec, beta = prep[:5]
    if distributed:
        g_pa, g_pd, g_out, g_gate, g_up, g_down = prep[5:]
        wf_pa, wf_pd, wf_out = _from_col_slabs(g_pa), g_pd.reshape(D_B, d), g_out.reshape(d, d)
        wf_gate, wf_up, wf_down = _from_col_slabs(g_gate), _from_col_slabs(g_up), g_down.reshape(D_FF, d)
    else:
        wf_pa, wf_pd, wf_out, wf_gate, wf_up, wf_down = rest
    wf_gu = jnp.concatenate([wf_gate, wf_up], axis=1)
    ub, ww, qd, kd, qk, gl = _dn_chunk_fwd(qn, kn, vn, gdec, beta)
    o_dn, states = _dn_scan_fwd(ub, ww, qd, kd, qk, gl)
    o_gated = _head_norm_fwd(o_dn, pb, dn_norm)
    y_a = _matmul(y_att, wf_pa, "nn", F32, 1024, d, D_ATTN_OUT, name="proj_attn")
    y_b = _matmul(o_gated, wf_pd, "nn", F32, 1024, d, d, name="proj_delta")
    merged = _gate_merge_fwd(pb, y_a, y_b)
    h1 = _matmul(merged, wf_out, "nn", F32, 1024, d, d, add=xs, name="out_proj")
    hn = _rms_fwd(h1, norm_ffn)
    gu = _matmul(hn, wf_gu, "nn", BF16, 1024, 1408, d, name="ffn_in")
    act = _swiglu_fwd(gu)
    h2 = _matmul(act, wf_down, "nn", F32, 512, d, D_FF, add=h1, name="ffn_out")
    loss_part, dh2, d_norm_final = _final_loss(h2, norm_final.reshape(1, d), target)

    dact = _matmul(dh2, wf_down, "nt", BF16, 1024, 1408, d, name="d_act")
    gw_down = _matmul(act, dh2, "tn", F32, 1408, d, 512, name="gw_down")
    dgu = _swiglu_bwd(gu, dact)
    dhn = _matmul(dgu, wf_gu, "nt", BF16, 1024, d, 1408, name="d_hn")
    gw_gu = _matmul(hn, dgu, "tn", F32, d, 1408, 512, name="gw_gu")
    dh1, d_norm_ffn = _rms_bwd(h1, norm_ffn, dhn, dh2)
    dmerged = _matmul(dh1, wf_out, "nt", BF16, 1024, d, d, name="d_merged")
    gw_out = _matmul(merged, dh1, "tn", F32, d, d, 512, name="gw_out")
    dya, dyb, dga, dgb = _gate_merge_bwd(pb, y_a, y_b, dmerged)
    dy_att = _matmul(dya, wf_pa, "nt", F32, 1024, D_ATTN_OUT, d, name="d_y_att")
    gw_pa = _matmul(y_att, dya, "tn", F32, D_ATTN_OUT, d, 512, name="gw_pa")
    do_gated = _matmul(dyb, wf_pd, "nt", BF16, 1024, d, d, name="d_o_gated")
    gw_pd = _matmul(o_gated, dyb, "tn", F32, d, d, 512, name="gw_pd")
    do_dn, dz, d_dn_norm = _head_norm_bwd(o_dn, pb, dn_norm, do_gated)
    dub, dww, dqd, dkd, dqk, dgl = _dn_scan_bwd(ub, ww, qd, kd, qk, gl, states, do_dn)
    dqn, dkn, dvn, dgdec, dbeta = _dn_chunk_bwd(qn, kn, vn, gdec, beta, dub, dww, dqd, dkd, dqk, dgl)
    slabs = [_col_slabs(gw_pa, d // N_DEV), _row_slabs(gw_pd), _row_slabs(gw_out),
             _col_slabs(gw_gu[:, :D_FF], D_FF // N_DEV), _col_slabs(gw_gu[:, D_FF:], D_FF // N_DEV),
             _row_slabs(gw_down)] if distributed else None
    prep = _dn_prep_bwd(pb, ps, cw, a_log_bc, dt_bias_bc, gdec, dqn, dkn, dvn, dgdec, dbeta,
                        exch=_Exchange(slabs, [True] * 6) if distributed else None)
    dc, dps, d_conv_full, d_alog_bc, d_dt_bc = prep[:5]
    dqkv_pre = _conv_bwd_input(dc, cw)
    segs = [_attn_bwd_group(pa, dy_att, y_att, lse, gi, dil) for gi, dil in enumerate(DILATIONS)]
    segs += [dqkv_pre, dz, dga, dgb]
    gws = [_matmul(u, s, "tn", F32, d, 1536, 512, name=f"gw_in_{i}") for i, s in enumerate(segs)]
    gw_small = _matmul(u, dps, "tn", F32, d, HEAD, 512, name="gw_in_small")
    g_att = jnp.concatenate(gws[:3], axis=1).reshape(d, N_HEADS_A, 3, HEAD)
    gw_in = jnp.concatenate(
        [g_att[:, :, i, :].reshape(d, D_ATTN) for i in range(3)]
        + [gws[3], gws[4], gw_small[:, :2 * N_HEADS_B], gws[5], gws[6]], axis=1)
    w_att = jnp.stack([w_pa_cols[:, i * D_ATTN:(i + 1) * D_ATTN].reshape(d, N_HEADS_A, HEAD) for i in range(3)],
                      axis=2).reshape(d, D_PA)
    du_first = _matmul_nt_segments(
        segs[:4], jnp.concatenate([w_att, w_pb_cols[:, :3 * D_B]], axis=1), F32, 1024, 1536, "d_u_0",
        exch=_Exchange([_col_slabs(gw_in, SHARD_IN)], [True]) if distributed else None)
    du_small = _matmul(dps, w_small, "nt", F32, 1024, d, HEAD, add=du_first[0], name="d_u_small")
    du = _matmul_nt_segments(segs[4:], w_pb_cols[:, 3 * D_B:], BF16, 1024, 1024, "d_u_1", add=du_small)[0]
    dx, d_norm_mix = _rms_bwd(xs, norm_mix, du, dh1)
    d_a_log = d_alog_bc.reshape(1, N_HEADS_B, HEAD)[:, :, 0]
    d_dt_bias = d_dt_bc.reshape(1, N_HEADS_B, HEAD)[:, :, 0]
    small = (d_conv_full, d_norm_mix, d_norm_ffn, d_norm_final, d_dn_norm, d_a_log, d_dt_bias)
    if distributed:
        return (loss_part, dx, [du_first[1]] + list(prep[5:])) + small
    return (loss_part, dx, gw_in, gw_pa, gw_pd, gw_out, gw_gu, gw_down) + small


def kernel(x, norm_mix, w_in, conv_w, a_log, dt_bias, dn_norm, w_proj_attn, w_proj_delta, w_out, norm_ffn, w_gate, w_up, w_down, norm_final, loss_target, m_norm_mix, m_w_in, m_conv_w, m_a_log, m_dt_bias, m_dn_norm, m_w_proj_attn, m_w_proj_delta, m_w_out, m_norm_ffn, m_w_gate, m_w_up, m_w_down, m_norm_final, v_norm_mix, v_w_in, v_conv_w, v_a_log, v_dt_bias, v_dn_norm, v_w_proj_attn, v_w_proj_delta, v_w_out, v_norm_ffn, v_w_gate, v_w_up, v_w_down, v_norm_final):
    d = D_MODEL
    xs = x[0]
    target = loss_target[0]
    me = _dev_index(_mesh_pos())

    g_in, g_conv = _exchange([w_in[0].astype(BF16), conv_w[0]], [False] * 2, "gather_w_in")
    rest = [w[0].astype(BF16) for w in (w_proj_attn, w_proj_delta, w_out, w_gate, w_up, w_down)]
    (loss_part, dx, landed, d_conv_full, d_norm_mix, d_norm_ffn, d_norm_final, d_dn_norm, d_a_log,
     d_dt_bias) = _local_step(xs, target, norm_mix, _from_col_slabs(g_in), _from_col_slabs(g_conv), a_log, dt_bias,
                              dn_norm, rest, norm_ffn, norm_final)

    small_shapes = [(1, d), (1, d), (d,), (1, HEAD), (1, N_HEADS_B), (1, N_HEADS_B), (1, 1), (CONV_WIDTH, 3 * D_B)]
    packed = _pack_rows([d_norm_mix, d_norm_ffn, d_norm_final, d_dn_norm, d_a_log, d_dt_bias,
                         loss_part[:, :1], d_conv_full])
    landed = list(landed) + list(_exchange([packed], [False], "gather_small_grads"))
    zero1 = jnp.zeros((1, 1), F32)
    zconv = jnp.zeros((CONV_WIDTH, 3 * D_B), F32)
    small_w = _pack_rows([norm_mix, norm_ffn, norm_final, dn_norm, a_log, dt_bias, zero1, zconv])
    small_m = _pack_rows([m_norm_mix, m_norm_ffn, m_norm_final, m_dn_norm, m_a_log, m_dt_bias, zero1, zconv])
    small_v = _pack_rows([v_norm_mix, v_norm_ffn, v_norm_final, v_dn_norm, v_a_log, v_dt_bias, zero1, zconv])
    small = [_unpack_rows(z, small_shapes) for z in _adamw(landed[7], small_w, small_m, small_v, "adamw_small")]
    loss = small[0][6].reshape(())
    conv_shard = 3 * D_B // N_DEV
    g_conv_own = lax.dynamic_slice_in_dim(small[0][7], me * conv_shard, conv_shard, axis=1)
    r_conv = _adamw(g_conv_own[None], conv_w[0], m_conv_w[0], v_conv_w[0], "adamw_conv")
    big = [_adamw(landed[i], w[0], m[0], v[0], f"adamw_{i}") for i, (w, m, v) in enumerate([
        (w_in, m_w_in, v_w_in), (w_proj_attn, m_w_proj_attn, v_w_proj_attn),
        (w_proj_delta, m_w_proj_delta, v_w_proj_delta), (w_out, m_w_out, v_w_out),
        (w_gate, m_w_gate, v_w_gate), (w_up, m_w_up, v_w_up), (w_down, m_w_down, v_w_down)])]

    def leaves(k):
        sm = small[k]
        return [sm[0], big[0][k][None], r_conv[k][None], sm[4], sm[5], sm[3], big[1][k][None], big[2][k][None],
                big[3][k][None], sm[1], big[4][k][None], big[5][k][None], big[6][k][None], sm[2]]

    return (loss, dx[None], *leaves(0), *leaves(1), *leaves(2), *leaves(3))
```

```python
import math

import jax
import jax.numpy as jnp
from jax import lax
from jax.experimental import pallas as pl
from jax.experimental.pallas import tpu as pltpu

F32 = jnp.float32
BF16 = jnp.bfloat16
HI = lax.Precision.HIGH

D_MODEL = 1024
N_DEV = 8
HEAD = 128
N_HEADS_A = 12
HEADS_PER_GROUP = 4
DILATIONS = (1, 4, 16)
BLOCK_A = 128
D_ATTN = N_HEADS_A * HEAD
D_ATTN_OUT = HEADS_PER_GROUP * HEAD
N_HEADS_B = 8
D_B = N_HEADS_B * HEAD
CONV_WIDTH = 4
CH = 64
CHUNK_GROUP = 16
SCAN_HEADS = 4
D_FF = 2816
EPS = 1e-6
D_IN = 3 * D_ATTN + 4 * D_B + 2 * N_HEADS_B + 2 * D_MODEL
SHARD_IN = D_IN // N_DEV
PB_Z, PB_GATE = 3072, 4096
D_PA = 3 * D_ATTN
ADAM_LR, ADAM_B1, ADAM_B2, ADAM_EPS, ADAM_WD, ADAM_STEP = 0.001, 0.9, 0.999, 1e-08, 0.01, 10
VMEM_LIMIT = 56 * 1024 * 1024

NN = ((1,), (0,))
NT = ((1,), (1,))
TN = ((0,), (0,))


def _dot(a, b, dims=NN, prec=None):
    return lax.dot_general(a, b, (dims, ((), ())), precision=prec, preferred_element_type=F32)


def _bdot(a, b, dims=NN):
    return _dot(a.astype(BF16), b.astype(BF16), dims)


def _hdot(a, b, dims=NN):
    return _dot(a.astype(F32), b.astype(F32), dims, HI)


def _cp(*sem):
    return pltpu.CompilerParams(dimension_semantics=sem, vmem_limit_bytes=VMEM_LIMIT)


def _sigmoid(x):
    return 0.5 * jnp.tanh(0.5 * x) + 0.5


def _softplus(x):
    return jnp.maximum(x, 0.0) + jnp.log(1.0 + jnp.exp(-jnp.abs(x)))


def _rowsum(x):
    return jnp.sum(x, axis=-1, keepdims=True)


def _matmul(a, b, mode, out_dtype, tm, tn, tk, add=None, name="mm"):
    if mode == "nn":
        (m, k), (k2, n) = a.shape, b.shape
    elif mode == "nt":
        (m, k), (n, k2) = a.shape, b.shape
    else:
        (k, m), (k2, n) = a.shape, b.shape
    assert k == k2, (a.shape, b.shape, mode)
    tm, tn, tk = min(tm, m), min(tn, n), min(tk, k)
    assert m % tm == 0 and n % tn == 0 and k % tk == 0, (a.shape, b.shape, tm, tn, tk)
    nk = k // tk
    dims = {"nn": NN, "nt": NT, "tn": TN}[mode]

    def body(*refs):
        if add is None:
            a_ref, b_ref, o_ref, acc = refs
            add_ref = None
        else:
            a_ref, b_ref, add_ref, o_ref, acc = refs
        kk = pl.program_id(2)

        @pl.when(kk == 0)
        def _():
            acc[...] = jnp.zeros_like(acc)

        acc[...] += _bdot(a_ref[...], b_ref[...], dims)

        @pl.when(kk == nk - 1)
        def _():
            r = acc[...]
            if add_ref is not None:
                r = r + add_ref[...].astype(F32)
            o_ref[...] = r.astype(out_dtype)

    a_spec = (pl.BlockSpec((tk, tm), lambda i, j, kk: (kk, i)) if mode == "tn"
              else pl.BlockSpec((tm, tk), lambda i, j, kk: (i, kk)))
    b_spec = (pl.BlockSpec((tn, tk), lambda i, j, kk: (j, kk)) if mode == "nt"
              else pl.BlockSpec((tk, tn), lambda i, j, kk: (kk, j)))
    in_specs = [a_spec, b_spec]
    args = [a, b]
    if add is not None:
        in_specs.append(pl.BlockSpec((tm, tn), lambda i, j, kk: (i, j)))
        args.append(add)
    return pl.pallas_call(
        body, name=name, grid=(m // tm, n // tn, nk),
        in_specs=in_specs, out_specs=pl.BlockSpec((tm, tn), lambda i, j, kk: (i, j)),
        out_shape=jax.ShapeDtypeStruct((m, n), out_dtype),
        scratch_shapes=[pltpu.VMEM((tm, tn), F32)],
        compiler_params=_cp("parallel", "parallel", "arbitrary"),
    )(*args)


def _matmul_nt_segments(segs, b, out_dtype, tm, tk, name, add=None, exch=None):
    m = segs[0].shape[0]
    n, ktot = b.shape
    ns = len(segs)
    nks = [s.shape[1] // tk for s in segs]
    assert all(s.shape[1] % tk == 0 for s in segs) and sum(s.shape[1] for s in segs) == ktot and m % tm == 0
    starts = [sum(nks[:i]) for i in range(ns)]
    nk = sum(nks)

    def body(*refs):
        seg_refs, b_ref = refs[:ns], refs[ns]
        add_ref = refs[ns + 1] if add is not None else None
        o_ref, acc = refs[-2], refs[-1]
        kk = pl.program_id(1)

        @pl.when(kk == 0)
        def _():
            acc[...] = jnp.zeros_like(acc)

        for a_ref, k0, nk_s in zip(seg_refs, starts, nks):
            @pl.when(jnp.logical_and(kk >= k0, kk < k0 + nk_s))
            def _(a_ref=a_ref):
                acc[...] += _bdot(a_ref[...], b_ref[...], NT)

        @pl.when(kk == nk - 1)
        def _():
            r = acc[...]
            if add_ref is not None:
                r = r + add_ref[...].astype(F32)
            o_ref[...] = r.astype(out_dtype)

    def seg_spec(k0, nk_s):
        return pl.BlockSpec((tm, tk), lambda i, kk: (i, jnp.clip(kk - k0, 0, nk_s - 1)))

    row = pl.BlockSpec((tm, n), lambda i, kk: (i, 0))
    in_specs = [seg_spec(k0, nk_s) for k0, nk_s in zip(starts, nks)] + [pl.BlockSpec((n, tk), lambda i, kk: (0, kk))]
    args = list(segs) + [b]
    if add is not None:
        in_specs.append(row)
        args.append(add)
    return _hosted_call(body, name=name, grid=(m // tm, nk), in_specs=in_specs, out_specs=[row],
                        out_shape=[jax.ShapeDtypeStruct((m, n), out_dtype)],
                        scratch_shapes=[pltpu.VMEM((tm, n), F32)], args=args,
                        dims=("parallel", "arbitrary"), exch=exch)


def _row_spec(tm, cols, cb=0):
    return pl.BlockSpec((tm, cols), lambda i, cb=cb: (i, cb))


def _bcast_spec(rows, cols):
    return pl.BlockSpec((rows, cols), lambda i: (0, 0))


def _rms_fwd(x, w, tm=512):
    t, d = x.shape

    def body(x_ref, w_ref, o_ref):
        xv = x_ref[...]
        r = lax.rsqrt(jnp.mean(xv * xv, axis=-1, keepdims=True) + EPS)
        o_ref[...] = (xv * r * w_ref[...]).astype(BF16)

    return pl.pallas_call(
        body, name="rms_fwd", grid=(t // tm,),
        in_specs=[_row_spec(tm, d), _bcast_spec(1, d)], out_specs=_row_spec(tm, d),
        out_shape=jax.ShapeDtypeStruct((t, d), BF16), compiler_params=_cp("parallel"),
    )(x, w)


def _rms_bwd(x, w, dy, resid, tm=512):
    t, d = x.shape

    def body(x_ref, w_ref, dy_ref, res_ref, dx_ref, dw_ref):
        xv = x_ref[...]
        r = lax.rsqrt(jnp.mean(xv * xv, axis=-1, keepdims=True) + EPS)
        xh = xv * r
        dyv = dy_ref[...].astype(F32)
        dxh = dyv * w_ref[...]
        dx_ref[...] = res_ref[...] + r * (dxh - xh * jnp.mean(dxh * xh, axis=-1, keepdims=True))

        @pl.when(pl.program_id(0) == 0)
        def _():
            dw_ref[...] = jnp.zeros_like(dw_ref)

        dw_ref[...] += jnp.sum(dyv * xh, axis=0, keepdims=True)

    return pl.pallas_call(
        body, name="rms_bwd", grid=(t // tm,),
        in_specs=[_row_spec(tm, d), _bcast_spec(1, d), _row_spec(tm, d), _row_spec(tm, d)],
        out_specs=[_row_spec(tm, d), _bcast_spec(1, d)],
        out_shape=[jax.ShapeDtypeStruct((t, d), F32), jax.ShapeDtypeStruct((1, d), F32)],
        compiler_params=_cp("arbitrary"),
    )(x, w, dy, resid)


def _final_loss(h, w, target, tm=512):
    t, d = h.shape

    def body(h_ref, w_ref, t_ref, loss_ref, dh_ref, dw_ref):
        hv = h_ref[...]
        r = lax.rsqrt(jnp.mean(hv * hv, axis=-1, keepdims=True) + EPS)
        xh = hv * r
        wv = w_ref[...]
        err = xh * wv - t_ref[...]
        dy = err * (1.0 / d)
        dxh = dy * wv
        dh_ref[...] = r * (dxh - xh * jnp.mean(dxh * xh, axis=-1, keepdims=True))

        @pl.when(pl.program_id(0) == 0)
        def _():
            dw_ref[...] = jnp.zeros_like(dw_ref)
            loss_ref[...] = jnp.zeros_like(loss_ref)

        dw_ref[...] += jnp.sum(dy * xh, axis=0, keepdims=True)
        part = 0.5 * jnp.sum(jnp.mean(err * err, axis=-1, keepdims=True), axis=0, keepdims=True)
        loss_ref[...] += part + jnp.zeros((1, HEAD), F32)

    return pl.pallas_call(
        body, name="final_loss", grid=(t // tm,),
        in_specs=[_row_spec(tm, d), _bcast_spec(1, d), _row_spec(tm, d)],
        out_specs=[_bcast_spec(1, HEAD), _row_spec(tm, d), _bcast_spec(1, d)],
        out_shape=[jax.ShapeDtypeStruct((1, HEAD), F32), jax.ShapeDtypeStruct((t, d), F32),
                   jax.ShapeDtypeStruct((1, d), F32)],
        compiler_params=_cp("arbitrary"),
    )(h, w, target)


def _swiglu_fwd(gu, tm=256):
    t = gu.shape[0]
    ff = gu.shape[1] // 2

    def body(gu_ref, o_ref):
        g = gu_ref[:, :ff].astype(F32)
        o_ref[...] = (g * _sigmoid(g) * gu_ref[:, ff:].astype(F32)).astype(BF16)

    return pl.pallas_call(
        body, name="swiglu_fwd", grid=(t // tm,),
        in_specs=[_row_spec(tm, 2 * ff)], out_specs=_row_spec(tm, ff),
        out_shape=jax.ShapeDtypeStruct((t, ff), BF16), compiler_params=_cp("parallel"),
    )(gu)


def _swiglu_bwd(gu, dact, tm=256):
    t = gu.shape[0]
    ff = gu.shape[1] // 2

    def body(gu_ref, d_ref, o_ref):
        g = gu_ref[:, :ff].astype(F32)
        u = gu_ref[:, ff:].astype(F32)
        dv = d_ref[...].astype(F32)
        sg = _sigmoid(g)
        o_ref[:, :ff] = (dv * u * (sg + g * sg * (1.0 - sg))).astype(BF16)
        o_ref[:, ff:] = (dv * g * sg).astype(BF16)

    return pl.pallas_call(
        body, name="swiglu_bwd", grid=(t // tm,),
        in_specs=[_row_spec(tm, 2 * ff), _row_spec(tm, ff)], out_specs=_row_spec(tm, 2 * ff),
        out_shape=jax.ShapeDtypeStruct((t, 2 * ff), BF16), compiler_params=_cp("parallel"),
    )(gu, dact)


def _gate_merge_fwd(pb, ya, yb, tm=512):
    t, d = ya.shape
    cb = PB_GATE // d

    def body(ga_ref, gb_ref, ya_ref, yb_ref, o_ref):
        o_ref[...] = (_sigmoid(ga_ref[...]) * ya_ref[...] + _sigmoid(gb_ref[...]) * yb_ref[...]).astype(BF16)

    return pl.pallas_call(
        body, name="gate_merge_fwd", grid=(t // tm,),
        in_specs=[_row_spec(tm, d, cb), _row_spec(tm, d, cb + 1), _row_spec(tm, d), _row_spec(tm, d)],
        out_specs=_row_spec(tm, d),
        out_shape=jax.ShapeDtypeStruct((t, d), BF16), compiler_params=_cp("parallel"),
    )(pb, pb, ya, yb)


def _gate_merge_bwd(pb, ya, yb, dm, tm=512):
    t, d = ya.shape
    cb = PB_GATE // d

    def body(ga_ref, gb_ref, ya_ref, yb_ref, dm_ref, dya_ref, dyb_ref, dga_ref, dgb_ref):
        dmv = dm_ref[...].astype(F32)
        sa = _sigmoid(ga_ref[...])
        sb = _sigmoid(gb_ref[...])
        dya_ref[...] = (dmv * sa).astype(BF16)
        dyb_ref[...] = (dmv * sb).astype(BF16)
        dga_ref[...] = (dmv * ya_ref[...] * sa * (1.0 - sa)).astype(BF16)
        dgb_ref[...] = (dmv * yb_ref[...] * sb * (1.0 - sb)).astype(BF16)

    return pl.pallas_call(
        body, name="gate_merge_bwd", grid=(t // tm,),
        in_specs=[_row_spec(tm, d, cb), _row_spec(tm, d, cb + 1), _row_spec(tm, d), _row_spec(tm, d),
                  _row_spec(tm, d)],
        out_specs=[_row_spec(tm, d)] * 4,
        out_shape=[jax.ShapeDtypeStruct((t, d), BF16)] * 4, compiler_params=_cp("parallel"),
    )(pb, pb, ya, yb, dm)


def _head_norm_fwd(o, pb, wn, tm=512):
    t, d = o.shape
    nh = d // HEAD

    def body(o_ref, z_ref, w_ref, out_ref):
        wv = w_ref[...]
        for h in range(nh):
            sl = slice(h * HEAD, (h + 1) * HEAD)
            ov = o_ref[:, sl]
            zv = z_ref[:, sl]
            r = lax.rsqrt(jnp.mean(ov * ov, axis=-1, keepdims=True) + EPS)
            out_ref[:, sl] = (ov * r * wv * (zv * _sigmoid(zv))).astype(BF16)

    return pl.pallas_call(
        body, name="head_norm_fwd", grid=(t // tm,),
        in_specs=[_row_spec(tm, d), _row_spec(tm, d, PB_Z // d), _bcast_spec(1, HEAD)],
        out_specs=_row_spec(tm, d),
        out_shape=jax.ShapeDtypeStruct((t, d), BF16), compiler_params=_cp("parallel"),
    )(o, pb, wn)


def _head_norm_bwd(o, pb, wn, dout, tm=512):
    t, d = o.shape
    nh = d // HEAD

    def body(o_ref, z_ref, w_ref, d_ref, do_ref, dz_ref, dw_ref):
        wv = w_ref[...]
        dw_acc = jnp.zeros((1, HEAD), F32)
        for h in range(nh):
            sl = slice(h * HEAD, (h + 1) * HEAD)
            ov = o_ref[:, sl]
            zv = z_ref[:, sl]
            dv = d_ref[:, sl].astype(F32)
            r = lax.rsqrt(jnp.mean(ov * ov, axis=-1, keepdims=True) + EPS)
            xh = ov * r
            sz = _sigmoid(zv)
            dn = dv * (zv * sz)
            dz_ref[:, sl] = (dv * xh * wv * (sz + zv * sz * (1.0 - sz))).astype(BF16)
            dxh = dn * wv
            do_ref[:, sl] = r * (dxh - xh * jnp.mean(dxh * xh, axis=-1, keepdims=True))
            dw_acc = dw_acc + jnp.sum(dn * xh, axis=0, keepdims=True)

        @pl.when(pl.program_id(0) == 0)
        def _():
            dw_ref[...] = jnp.zeros_like(dw_ref)

        dw_ref[...] += dw_acc

    return pl.pallas_call(
        body, name="head_norm_bwd", grid=(t // tm,),
        in_specs=[_row_spec(tm, d), _row_spec(tm, d, PB_Z // d), _bcast_spec(1, HEAD), _row_spec(tm, d)],
        out_specs=[_row_spec(tm, d), _row_spec(tm, d), _bcast_spec(1, HEAD)],
        out_shape=[jax.ShapeDtypeStruct((t, d), F32), jax.ShapeDtypeStruct((t, d), BF16),
                   jax.ShapeDtypeStruct((1, HEAD), F32)],
        compiler_params=_cp("arbitrary"),
    )(o, pb, wn, dout)


def _attn_bias(gi, hh, dil):
    i = lax.broadcasted_iota(jnp.int32, (BLOCK_A, BLOCK_A), 0)
    j = lax.broadcasted_iota(jnp.int32, (BLOCK_A, BLOCK_A), 1)
    hf = (gi * HEADS_PER_GROUP + hh + 1).astype(F32)
    slope = jnp.exp(jnp.full((1, BLOCK_A), -8.0 * math.log(2.0) / N_HEADS_A, F32) * hf) * float(dil)
    d_prev = (BLOCK_A + i - j).astype(F32)
    d_cur = (i - j).astype(F32)
    return -slope * d_prev, -slope * d_cur, j >= i, j <= i


ATTN_TOKENS = 2048


def _sub_rows(a, r, dil):
    start = a * BLOCK_A * dil + r
    return pl.ds(start, BLOCK_A) if dil == 1 else pl.ds(start, BLOCK_A, stride=dil)


def _attn_fwd_group(pa, gi, dil, tb=ATTN_TOKENS):
    t = pa.shape[0]
    tb = min(tb, t)
    hb = BLOCK_A * dil
    nb = tb // hb
    scale = HEAD ** -0.5

    def body(q_ref, k_ref, v_ref, kp_ref, vp_ref, o_ref, l_ref):
        hh = pl.program_id(0)
        step = pl.program_id(1)
        b_prev, b_cur, m_prev, m_cur = _attn_bias(gi, hh, dil)
        m_first = jnp.logical_and(m_prev, step > 0)
        for r in range(dil):
            kp, vp = kp_ref[_sub_rows(0, r, dil), :], vp_ref[_sub_rows(0, r, dil), :]
            for a in range(nb):
                rows = _sub_rows(a, r, dil)
                q, kc, vc = q_ref[rows, :], k_ref[rows, :], v_ref[rows, :]
                s_p = jnp.where(m_first if a == 0 else m_prev, _bdot(q, kp, NT) * scale + b_prev, -1e30)
                s_c = jnp.where(m_cur, _bdot(q, kc, NT) * scale + b_cur, -1e30)
                m = jnp.maximum(jnp.max(s_p, axis=-1, keepdims=True), jnp.max(s_c, axis=-1, keepdims=True))
                p_p = jnp.exp(s_p - m)
                p_c = jnp.exp(s_c - m)
                den = _rowsum(p_p) + _rowsum(p_c)
                o_ref[rows, :] = (_bdot(p_p, vp) + _bdot(p_c, vc)) / den
                l_ref[rows, :] = (m + jnp.log(den)) + jnp.zeros((BLOCK_A, HEAD), F32)
                kp, vp = kc, vc

    def col(base):
        return lambda hh, s: (s, base + hh)

    def col_prev(base):
        return lambda hh, s: (jnp.maximum(s * nb - 1, 0), base + hh)

    qb, kb, vb = gi * HEADS_PER_GROUP, N_HEADS_A + gi * HEADS_PER_GROUP, 2 * N_HEADS_A + gi * HEADS_PER_GROUP
    ospec = pl.BlockSpec((tb, HEAD), lambda hh, s: (s, hh))
    return pl.pallas_call(
        body, name=f"attn_fwd_g{gi}", grid=(HEADS_PER_GROUP, t // tb),
        in_specs=[pl.BlockSpec((tb, HEAD), col(qb)), pl.BlockSpec((tb, HEAD), col(kb)),
                  pl.BlockSpec((tb, HEAD), col(vb)),
                  pl.BlockSpec((hb, HEAD), col_prev(kb)), pl.BlockSpec((hb, HEAD), col_prev(vb))],
        out_specs=[ospec, ospec],
        out_shape=[jax.ShapeDtypeStruct((t, D_ATTN_OUT), F32)] * 2,
        compiler_params=_cp("parallel", "parallel"),
    )(pa, pa, pa, pa, pa)


def _attn_merge(os, ls, tm=512):
    t, d = os[0].shape

    def body(o0, o1, o2, l0, l1, l2, y_ref, lse_ref):
        a0, a1, a2 = l0[...], l1[...], l2[...]
        m = jnp.maximum(jnp.maximum(a0, a1), a2)
        e0, e1, e2 = jnp.exp(a0 - m), jnp.exp(a1 - m), jnp.exp(a2 - m)
        den = e0 + e1 + e2
        y_ref[...] = (e0 * o0[...] + e1 * o1[...] + e2 * o2[...]) / den
        lse_ref[...] = m + jnp.log(den)

    return pl.pallas_call(
        body, name="attn_merge", grid=(t // tm,),
        in_specs=[_row_spec(tm, d)] * 6, out_specs=[_row_spec(tm, d)] * 2,
        out_shape=[jax.ShapeDtypeStruct((t, d), F32)] * 2,
        compiler_params=_cp("parallel"),
    )(*os, *ls)


def _attn_bwd_group(pa, dy, y, lse, gi, dil, tb=ATTN_TOKENS):
    t = pa.shape[0]
    tb = min(tb, t)
    hb = BLOCK_A * dil
    nb = tb // hb
    nsteps = t // tb
    scale = HEAD ** -0.5

    def body(q_ref, k_ref, v_ref, dy_ref, y_ref, l_ref, kp_ref, vp_ref, d_ref,
             dq_s, dk_s, dv_s, carry_k, carry_v):
        hh = pl.program_id(0)
        step = pl.program_id(1)

        @pl.when(step == 0)
        def _():
            carry_k[...] = jnp.zeros_like(carry_k)
            carry_v[...] = jnp.zeros_like(carry_v)

        b_prev, b_cur, m_prev, m_cur = _attn_bias(gi, hh, dil)
        m_first = jnp.logical_and(m_prev, step < nsteps - 1)
        for r in range(dil):
            halo = _sub_rows(0, r, dil)
            dk_in, dv_in = carry_k[halo, :], carry_v[halo, :]
            kp, vp = kp_ref[halo, :], vp_ref[halo, :]
            prev_rows = None
            dk_pend = dv_pend = None
            for a in range(nb):
                rows = _sub_rows(a, r, dil)
                q, kc, vc = q_ref[rows, :], k_ref[rows, :], v_ref[rows, :]
                dyb, lb = dy_ref[rows, :], l_ref[rows, :]
                delta = _rowsum(dyb * y_ref[rows, :])
                mp = m_first if a == 0 else m_prev
                s = _bdot(q, kp, NT) * scale + b_prev
                p = jnp.where(mp, jnp.exp(jnp.where(mp, s - lb, 0.0)), 0.0)
                ds = p * (_bdot(dyb, vp, NT) - delta)
                dq = _bdot(ds, kp)
                dk_prev, dv_prev = _bdot(ds, q, TN), _bdot(p, dyb, TN)
                if a == 0:
                    carry_k[halo, :] = dk_prev
                    carry_v[halo, :] = dv_prev
                else:
                    dk_s[prev_rows, :] = dk_pend + dk_prev
                    dv_s[prev_rows, :] = dv_pend + dv_prev
                s = _bdot(q, kc, NT) * scale + b_cur
                p = jnp.where(m_cur, jnp.exp(jnp.where(m_cur, s - lb, 0.0)), 0.0)
                ds = p * (_bdot(dyb, vc, NT) - delta)
                dq_s[rows, :] = dq + _bdot(ds, kc)
                dk_pend, dv_pend = _bdot(ds, q, TN), _bdot(p, dyb, TN)
                prev_rows, kp, vp = rows, kc, vc
            dk_s[prev_rows, :] = dk_pend + dk_in
            dv_s[prev_rows, :] = dv_pend + dv_in
        d_ref[:, :HEAD] = (dq_s[...] * scale).astype(BF16)
        d_ref[:, HEAD:2 * HEAD] = (dk_s[...] * scale).astype(BF16)
        d_ref[:, 2 * HEAD:] = dv_s[...].astype(BF16)

    def col(base):
        return lambda hh, s: (nsteps - 1 - s, base + hh)

    def col_prev(base):
        return lambda hh, s: (jnp.maximum((nsteps - 1 - s) * nb - 1, 0), base + hh)

    qb, kb, vb = gi * HEADS_PER_GROUP, N_HEADS_A + gi * HEADS_PER_GROUP, 2 * N_HEADS_A + gi * HEADS_PER_GROUP
    big, small = (tb, HEAD), (hb, HEAD)
    return pl.pallas_call(
        body, name=f"attn_bwd_g{gi}", grid=(HEADS_PER_GROUP, nsteps),
        in_specs=[pl.BlockSpec(big, col(qb)), pl.BlockSpec(big, col(kb)), pl.BlockSpec(big, col(vb)),
                  pl.BlockSpec(big, col(0)), pl.BlockSpec(big, col(0)), pl.BlockSpec(big, col(0)),
                  pl.BlockSpec(small, col_prev(kb)), pl.BlockSpec(small, col_prev(vb))],
        out_specs=pl.BlockSpec((tb, 3 * HEAD), col(0)),
        out_shape=jax.ShapeDtypeStruct((t, 3 * D_ATTN_OUT), BF16),
        scratch_shapes=[pltpu.VMEM(big, F32)] * 3 + [pltpu.VMEM(small, F32)] * 2,
        compiler_params=_cp("parallel", "arbitrary"),
    )(pa, pa, pa, dy, y, lse, pa, pa)


def _shift_down(cur, prev8, s):
    if s == 0:
        return cur
    rolled = pltpu.roll(cur, s, 0)
    prolled = pltpu.roll(prev8, s, 0)
    rid = lax.broadcasted_iota(jnp.int32, prev8.shape, 0)
    top = jnp.where(rid < s, prolled, rolled[:8])
    return jnp.concatenate([top, rolled[8:]], axis=0)


def _shift_up(cur, next8, s):
    if s == 0:
        return cur
    n = cur.shape[0]
    rolled = pltpu.roll(cur, n - s, 0)
    nrolled = pltpu.roll(next8, 8 - s, 0)
    rid = lax.broadcasted_iota(jnp.int32, next8.shape, 0)
    bottom = jnp.where(rid >= 8 - s, nrolled, rolled[n - 8:])
    return jnp.concatenate([rolled[:n - 8], bottom], axis=0)


def _conv(xv, prev8, wv):
    c = jnp.zeros_like(xv)
    shifted = []
    for s in range(CONV_WIDTH):
        xs = _shift_down(xv, prev8, s)
        shifted.append(xs)
        c = c + wv[CONV_WIDTH - 1 - s:CONV_WIDTH - s, :] * xs
    return c, shifted


def _head_expand(psv, first):
    tm = psv.shape[0]
    return jnp.concatenate([jnp.broadcast_to(psv[:, first + h:first + h + 1], (tm, HEAD))
                            for h in range(N_HEADS_B)], axis=1)


def _head_collect(x, first):
    lane = lax.broadcasted_iota(jnp.int32, (x.shape[0], HEAD), 1)
    out = jnp.zeros((x.shape[0], HEAD), F32)
    for h in range(N_HEADS_B):
        out = jnp.where(lane == first + h, x[:, h * HEAD:(h + 1) * HEAD], out)
    return out


def _dn_prep_fwd(pb, ps, conv_w, a_log_bc, dt_bias_bc, tm=256, exch=None):
    t = pb.shape[0]
    c3 = 3 * D_B
    r8 = tm // 8

    def body(x_ref, xp_ref, ps_ref, w_ref, al_ref, dt_ref, q_ref, k_ref, v_ref, g_ref, beta_ref):
        first = pl.program_id(0) > 0

        def silu_conv(cols):
            c, _ = _conv(x_ref[:, cols], jnp.where(first, xp_ref[:, cols], 0.0), w_ref[:, cols])
            return c * _sigmoid(c)

        for h in range(N_HEADS_B):
            sl = slice(h * HEAD, (h + 1) * HEAD)
            sq = silu_conv(sl)
            q_ref[:, sl] = sq * lax.rsqrt(_rowsum(sq * sq) + EPS) * (HEAD ** -0.5)
            sk = silu_conv(slice(D_B + h * HEAD, D_B + (h + 1) * HEAD))
            k_ref[:, sl] = sk * lax.rsqrt(_rowsum(sk * sk) + EPS)
            v_ref[:, sl] = silu_conv(slice(2 * D_B + h * HEAD, 2 * D_B + (h + 1) * HEAD))
        psv = ps_ref[...]
        beta_ref[...] = _sigmoid(_head_expand(psv, 0))
        g_ref[...] = -jnp.exp(al_ref[...]) * _softplus(_head_expand(psv, N_HEADS_B) + dt_ref[...])

    return _hosted_call(
        body, name="dn_prep_fwd", grid=(t // tm,),
        in_specs=[_row_spec(tm, c3, 0),
                  pl.BlockSpec((8, c3), lambda i: (jnp.maximum(i * r8 - 1, 0), 0)),
                  _row_spec(tm, HEAD),
                  _bcast_spec(CONV_WIDTH, c3), _bcast_spec(1, D_B), _bcast_spec(1, D_B)],
        out_specs=[_row_spec(tm, D_B)] * 5,
        out_shape=[jax.ShapeDtypeStruct((t, D_B), F32)] * 5, scratch_shapes=[],
        args=(pb, pb, ps, conv_w, a_log_bc, dt_bias_bc), dims=("parallel",), exch=exch)


def _dn_prep_bwd(pb, ps, conv_w, a_log_bc, dt_bias_bc, g, dq, dk, dv, dg, dbeta, tm=128, exch=None):
    t = pb.shape[0]
    c3 = 3 * D_B
    r8 = tm // 8

    def body(x_ref, xp_ref, ps_ref, w_ref, al_ref, dt_ref, g_ref, dq_ref, dk_ref, dv_ref, dg_ref, db_ref,
             dc_ref, dps_ref, dw_ref, dal_ref, ddt_ref):
        first = pl.program_id(0) > 0

        @pl.when(pl.program_id(0) == 0)
        def _():
            dw_ref[...] = jnp.zeros_like(dw_ref)
            dal_ref[...] = jnp.zeros_like(dal_ref)
            ddt_ref[...] = jnp.zeros_like(ddt_ref)

        def column_block(cols, d_ref, sl, mult, normed):
            c, shifted = _conv(x_ref[:, cols], jnp.where(first, xp_ref[:, cols], 0.0), w_ref[:, cols])
            sg = _sigmoid(c)
            dsilu = sg + c * sg * (1.0 - sg)
            dyv = d_ref[:, sl]
            if normed:
                sv = c * sg
                r = lax.rsqrt(_rowsum(sv * sv) + EPS)
                yh = sv * r
                dyv = dyv * mult
                dyv = r * (dyv - yh * _rowsum(dyv * yh))
            dcv = dyv * dsilu
            dc_ref[:, cols] = dcv
            for sft in range(CONV_WIDTH):
                j = CONV_WIDTH - 1 - sft
                dw_ref[j:j + 1, cols] += jnp.sum(dcv * shifted[sft], axis=0, keepdims=True)

        for h in range(N_HEADS_B):
            sl = slice(h * HEAD, (h + 1) * HEAD)
            column_block(sl, dq_ref, sl, HEAD ** -0.5, True)
            column_block(slice(D_B + h * HEAD, D_B + (h + 1) * HEAD), dk_ref, sl, 1.0, True)
            column_block(slice(2 * D_B + h * HEAD, 2 * D_B + (h + 1) * HEAD), dv_ref, sl, 1.0, False)
        psv = ps_ref[...]
        beta = _sigmoid(_head_expand(psv, 0))
        dgv = dg_ref[...]
        da = dgv * (-jnp.exp(al_ref[...])) * _sigmoid(_head_expand(psv, N_HEADS_B) + dt_ref[...])
        dps_ref[...] = _head_collect(db_ref[...] * beta * (1.0 - beta), 0) + _head_collect(da, N_HEADS_B)
        dal_ref[...] += jnp.sum(dgv * g_ref[...], axis=0, keepdims=True)
        ddt_ref[...] += jnp.sum(da, axis=0, keepdims=True)

    row = _row_spec(tm, D_B)
    return _hosted_call(
        body, name="dn_prep_bwd", grid=(t // tm,),
        in_specs=[_row_spec(tm, c3, 0),
                  pl.BlockSpec((8, c3), lambda i: (jnp.maximum(i * r8 - 1, 0), 0)),
                  _row_spec(tm, HEAD),
                  _bcast_spec(CONV_WIDTH, c3), _bcast_spec(1, D_B), _bcast_spec(1, D_B),
                  row, row, row, row, row, row],
        out_specs=[_row_spec(tm, c3), _row_spec(tm, HEAD), _bcast_spec(CONV_WIDTH, c3), _bcast_spec(1, D_B),
                   _bcast_spec(1, D_B)],
        out_shape=[jax.ShapeDtypeStruct((t, c3), F32), jax.ShapeDtypeStruct((t, HEAD), F32),
                   jax.ShapeDtypeStruct((CONV_WIDTH, c3), F32),
                   jax.ShapeDtypeStruct((1, D_B), F32), jax.ShapeDtypeStruct((1, D_B), F32)],
        scratch_shapes=[], args=(pb, pb, ps, conv_w, a_log_bc, dt_bias_bc, g, dq, dk, dv, dg, dbeta),
        dims=("arbitrary",), exch=exch)


def _conv_bwd_input(dc, conv_w, tm=256):
    t, c3 = dc.shape
    r8 = tm // 8
    nlast = t // 8 - 1
    nsteps = t // tm

    def body(d_ref, dn_ref, w_ref, o_ref):
        not_last = pl.program_id(0) < nsteps - 1
        for cb in range(c3 // HEAD):
            cols = slice(cb * HEAD, (cb + 1) * HEAD)
            next8 = jnp.where(not_last, dn_ref[:, cols], 0.0)
            dv = d_ref[:, cols]
            wv = w_ref[:, cols]
            acc = jnp.zeros_like(dv)
            for s in range(CONV_WIDTH):
                acc = acc + wv[CONV_WIDTH - 1 - s:CONV_WIDTH - s, :] * _shift_up(dv, next8, s)
            o_ref[:, cols] = acc.astype(BF16)

    return pl.pallas_call(
        body, name="conv_bwd_input", grid=(nsteps,),
        in_specs=[_row_spec(tm, c3), pl.BlockSpec((8, c3), lambda i: (jnp.minimum((i + 1) * r8, nlast), 0)),
                  _bcast_spec(CONV_WIDTH, c3)],
        out_specs=_row_spec(tm, c3),
        out_shape=jax.ShapeDtypeStruct((t, c3), BF16), compiler_params=_cp("parallel"),
    )(dc, dc, conv_w)


def _lanes(x):
    return x[:, :CH]


def _tri_inv(a_list, r, c):
    eye = (r == c).astype(F32)
    b16 = (r >> 4) == (c >> 4)
    b32 = (r >> 5) == (c >> 5)
    ns = [jnp.where(b16, -a, 0.0) for a in a_list]
    xs = [eye + n for n in ns]
    ps = [_bdot(n, n) for n in ns]
    for last in (False, False, True):
        xs = [x + _bdot(x, p) for x, p in zip(xs, ps)]
        if not last:
            ps = [_bdot(p, p) for p in ps]
    for mask in (jnp.logical_and(b32, jnp.logical_not(b16)), jnp.logical_not(b32)):
        ts = [_bdot(x, jnp.where(mask, a, 0.0)) for x, a in zip(xs, a_list)]
        xs = [x - _bdot(t, x) for x, t in zip(xs, ts)]
    return xs


def _chunk_local(qs, ks, vs, gs, betas):
    r = lax.broadcasted_iota(jnp.int32, (CH, CH), 0)
    c = lax.broadcasted_iota(jnp.int32, (CH, CH), 1)
    incl, strict = r >= c, r > c
    lm = incl.astype(F32)
    cums = [_hdot(lm, jnp.concatenate([g, jnp.where(strict, _lanes(g), 0.0)], axis=1)) for g in gs]
    gcbs = [cm[:, :HEAD] for cm in cums]
    decays = [jnp.where(incl, jnp.exp(jnp.where(incl, cm[:, HEAD:], 0.0)), 0.0) for cm in cums]
    bcols = [_lanes(b) for b in betas]
    kks = [_bdot(k, k, NT) for k in ks]
    qkraws = [_bdot(q, k, NT) for q, k in zip(qs, ks)]
    tms = _tri_inv([jnp.where(strict, bc * kk * dc, 0.0) for bc, kk, dc in zip(bcols, kks, decays)], r, c)
    egs = [jnp.exp(gcb) for gcb in gcbs]
    sols = [_hdot(tm, jnp.concatenate([b * v, b * eg * k], axis=1))
            for tm, b, v, eg, k in zip(tms, betas, vs, egs, ks)]
    gls = [gcb[CH - 1:CH, :] for gcb in gcbs]
    eks = [jnp.exp(gl - gcb) for gl, gcb in zip(gls, gcbs)]
    return [dict(incl=incl, strict=strict, r=r, c=c, decay=dc, bcol=bc, kk=kk, tm=tm, eg=eg,
                 u_bar=sol[:, :HEAD], w=sol[:, HEAD:], qkraw=qkraw, gl=gl, ek=ek)
            for dc, bc, kk, tm, eg, sol, qkraw, gl, ek in zip(decays, bcols, kks, tms, egs, sols, qkraws, gls, eks)]


def _dn_chunk_fwd(q, k, v, g, beta, cps=CHUNK_GROUP):
    t = q.shape[0]
    tm = cps * CH

    def body(q_ref, k_ref, v_ref, g_ref, b_ref, ub_ref, w_ref, qd_ref, kd_ref, qk_ref, gl_ref):
        for base in range(0, cps, CHUNK_GROUP):
            sls = [slice((base + j) * CH, (base + j + 1) * CH) for j in range(CHUNK_GROUP)]
            qs, ks = [q_ref[sl, :] for sl in sls], [k_ref[sl, :] for sl in sls]
            locs = _chunk_local(qs, ks, [v_ref[sl, :] for sl in sls], [g_ref[sl, :] for sl in sls],
                                [b_ref[sl, :] for sl in sls])
            for j, (sl, qv, kv, loc) in enumerate(zip(sls, qs, ks, locs)):
                ub_ref[sl, :] = loc["u_bar"]
                w_ref[sl, :] = loc["w"]
                qd_ref[sl, :] = qv * loc["eg"]
                kd_ref[sl, :] = kv * loc["ek"]
                qk_ref[sl, :] = loc["qkraw"] * loc["decay"]
                gl_ref[base + j:base + j + 1, :] = jnp.exp(loc["gl"])

    hspec = pl.BlockSpec((tm, HEAD), lambda h, i: (i, h))
    return pl.pallas_call(
        body, name="dn_chunk_fwd", grid=(N_HEADS_B, t // tm),
        in_specs=[hspec] * 5,
        out_specs=[hspec] * 4 + [pl.BlockSpec((None, tm, CH), lambda h, i: (h, i, 0)),
                                 pl.BlockSpec((cps, HEAD), lambda h, i: (i, h))],
        out_shape=[jax.ShapeDtypeStruct((t, D_B), F32)] * 4
        + [jax.ShapeDtypeStruct((N_HEADS_B, t, CH), F32), jax.ShapeDtypeStruct((t // CH, D_B), F32)],
        compiler_params=_cp("parallel", "parallel"),
    )(q, k, v, g, beta)


def _dn_scan_fwd(ub, w, qd, kd, qk, gl, cps=8):
    t = ub.shape[0]
    tm = cps * CH

    hg = SCAN_HEADS
    hs = list(range(hg))

    def body(ub_ref, w_ref, qd_ref, kd_ref, qk_ref, gl_ref, o_ref, st_ref, s_acc):
        @pl.when(pl.program_id(1) == 0)
        def _():
            s_acc[...] = jnp.zeros_like(s_acc)

        for ci in range(cps):
            sl = slice(ci * CH, (ci + 1) * CH)
            cols = [slice(h * HEAD, (h + 1) * HEAD) for h in hs]
            svs = [s_acc[h] for h in hs]
            for h in hs:
                st_ref[h, ci * HEAD:(ci + 1) * HEAD, :] = svs[h]
            us = [ub_ref[sl, cols[h]] - _bdot(w_ref[sl, cols[h]], svs[h]) for h in hs]
            for h in hs:
                s_acc[h] = gl_ref[ci:ci + 1, cols[h]] * svs[h] + _bdot(kd_ref[sl, cols[h]], us[h], TN)
            for h in hs:
                o_ref[sl, cols[h]] = _bdot(qd_ref[sl, cols[h]], svs[h]) + _bdot(qk_ref[h, sl, :], us[h])

    hspec = pl.BlockSpec((tm, hg * HEAD), lambda h, i: (i, h))
    return pl.pallas_call(
        body, name="dn_scan_fwd", grid=(N_HEADS_B // hg, t // tm),
        in_specs=[hspec] * 4 + [pl.BlockSpec((hg, tm, CH), lambda h, i: (h, i, 0)),
                                pl.BlockSpec((cps, hg * HEAD), lambda h, i: (i, h))],
        out_specs=[hspec, pl.BlockSpec((hg, cps * HEAD, HEAD), lambda h, i: (h, i, 0))],
        out_shape=[jax.ShapeDtypeStruct((t, D_B), F32),
                   jax.ShapeDtypeStruct((N_HEADS_B, (t // CH) * HEAD, HEAD), F32)],
        scratch_shapes=[pltpu.VMEM((hg, HEAD, HEAD), F32)],
        compiler_params=_cp("parallel", "arbitrary"),
    )(ub, w, qd, kd, qk, gl)


def _dn_scan_bwd(ub, w, qd, kd, qk, gl, st, do, cps=8):
    t = ub.shape[0]
    tm = cps * CH
    ns = t // tm

    hg = SCAN_HEADS
    hs = list(range(hg))

    def body(ub_ref, w_ref, qd_ref, kd_ref, qk_ref, gl_ref, st_ref, do_ref,
             dub_ref, dw_ref, dqd_ref, dkd_ref, dqk_ref, dgl_ref, ds_acc):
        @pl.when(pl.program_id(1) == 0)
        def _():
            ds_acc[...] = jnp.zeros_like(ds_acc)

        for ci in reversed(range(cps)):
            sl = slice(ci * CH, (ci + 1) * CH)
            cols = [slice(h * HEAD, (h + 1) * HEAD) for h in hs]
            svs = [st_ref[h, ci * HEAD:(ci + 1) * HEAD, :] for h in hs]
            wvs = [w_ref[sl, cols[h]] for h in hs]
            dovs = [do_ref[sl, cols[h]] for h in hs]
            dsvs = [ds_acc[h] for h in hs]
            us = [ub_ref[sl, cols[h]] - _bdot(wvs[h], svs[h]) for h in hs]
            dus = [_bdot(kd_ref[sl, cols[h]], dsvs[h]) + _bdot(qk_ref[h, sl, :], dovs[h], TN) for h in hs]
            for h in hs:
                ds_acc[h] = (gl_ref[ci:ci + 1, cols[h]] * dsvs[h] + _bdot(qd_ref[sl, cols[h]], dovs[h], TN)
                             - _bdot(wvs[h], dus[h], TN))
            for h in hs:
                dgl_ref[ci:ci + 1, cols[h]] = (jnp.sum(_rowsum(dsvs[h] * svs[h]), axis=0, keepdims=True)
                                              + jnp.zeros((1, HEAD), F32))
                dkd_ref[sl, cols[h]] = _bdot(us[h], dsvs[h], NT)
                dqd_ref[sl, cols[h]] = _bdot(dovs[h], svs[h], NT)
                dqk_ref[h, sl, :] = _bdot(dovs[h], us[h], NT)
                dub_ref[sl, cols[h]] = dus[h]
                dw_ref[sl, cols[h]] = -_bdot(dus[h], svs[h], NT)

    hspec = pl.BlockSpec((tm, hg * HEAD), lambda h, i: (ns - 1 - i, h))
    qkspec = pl.BlockSpec((hg, tm, CH), lambda h, i: (h, ns - 1 - i, 0))
    glspec = pl.BlockSpec((cps, hg * HEAD), lambda h, i: (ns - 1 - i, h))
    return pl.pallas_call(
        body, name="dn_scan_bwd", grid=(N_HEADS_B // hg, ns),
        in_specs=[hspec] * 4 + [qkspec, glspec,
                                pl.BlockSpec((hg, cps * HEAD, HEAD), lambda h, i: (h, ns - 1 - i, 0)), hspec],
        out_specs=[hspec] * 4 + [qkspec, glspec],
        out_shape=[jax.ShapeDtypeStruct((t, D_B), F32)] * 4
        + [jax.ShapeDtypeStruct((N_HEADS_B, t, CH), F32), jax.ShapeDtypeStruct((t // CH, D_B), F32)],
        scratch_shapes=[pltpu.VMEM((hg, HEAD, HEAD), F32)],
        compiler_params=_cp("parallel", "arbitrary"),
    )(ub, w, qd, kd, qk, gl, st, do)


def _dn_chunk_bwd(q, k, v, g, beta, dub, dw, dqd, dkd, dqk, dgl, cps=CHUNK_GROUP):
    t = q.shape[0]
    tm = cps * CH

    def body(q_ref, k_ref, v_ref, g_ref, b_ref, dub_ref, dw_ref, dqd_ref, dkd_ref, dqk_ref, dgl_ref,
             dq_ref, dk_ref, dv_ref, dg_ref, db_ref):
        ones = jnp.ones((CH, HEAD), F32)
        rid = lax.broadcasted_iota(jnp.int32, (CH, HEAD), 0)

        def rest(ci, sl, qv, kv, vv, beta_v, loc, dr, da):
            incl = loc["incl"]
            eg, ek, decay, bcol, kk = loc["eg"], loc["ek"], loc["decay"], loc["bcol"], loc["kk"]
            drv, drk = dr[:, :HEAD], dr[:, HEAD:]
            dv_ref[sl, :] = beta_v * drv
            beg = beta_v * eg
            t1 = drk * kv
            dbeta = _rowsum(drv * vv + t1 * eg) + _rowsum(da * kk * decay)
            dkk = da * bcol * decay
            dqk_m = jnp.where(incl, dqk_ref[sl, :], 0.0)
            ddecay = da * bcol * kk + dqk_m * loc["qkraw"]
            dqkraw = dqk_m * decay
            dqdv, dkdv = dqd_ref[sl, :], dkd_ref[sl, :]
            dq_ref[sl, :] = _bdot(dqkraw, kv) + dqdv * eg
            dk_ref[sl, :] = (beg * drk + _bdot(dqkraw, qv, TN) + _bdot(dkk, kv) + _bdot(dkk, kv, TN)
                             + dkdv * ek)
            e = ddecay * decay
            skd = _rowsum(dkdv * kv * ek)
            dgc = _rowsum(beg * t1) + _rowsum(e) + _rowsum(dqdv * qv * eg) - skd
            colsum = _hdot(e, ones, TN)
            last = jnp.sum(skd, axis=0, keepdims=True) + dgl_ref[ci:ci + 1, :] * jnp.exp(loc["gl"])
            db_ref[sl, :] = dbeta + jnp.zeros((CH, HEAD), F32)
            return (dgc - colsum) + jnp.where(rid == CH - 1, last, 0.0)

        for base in range(0, cps, CHUNK_GROUP):
            cis = list(range(base, base + CHUNK_GROUP))
            sls = [slice(ci * CH, (ci + 1) * CH) for ci in cis]
            qs, ks, vs = [q_ref[sl, :] for sl in sls], [k_ref[sl, :] for sl in sls], [v_ref[sl, :] for sl in sls]
            betas = [b_ref[sl, :] for sl in sls]
            locs = _chunk_local(qs, ks, vs, [g_ref[sl, :] for sl in sls], betas)
            drs = [_hdot(loc["tm"], jnp.concatenate([dub_ref[sl, :], dw_ref[sl, :]], axis=1), TN)
                   for loc, sl in zip(locs, sls)]
            das = [jnp.where(loc["strict"],
                             -_hdot(dr, jnp.concatenate([loc["u_bar"], loc["w"]], axis=1), NT), 0.0)
                   for loc, dr in zip(locs, drs)]
            dgcs = [rest(*args) for args in zip(cis, sls, qs, ks, vs, betas, locs, drs, das)]
            um = (locs[0]["r"] <= locs[0]["c"]).astype(F32)
            for sl, dgc_bc in zip(sls, dgcs):
                dg_ref[sl, :] = _hdot(um, dgc_bc)

    hspec = pl.BlockSpec((tm, HEAD), lambda h, i: (i, h))
    return pl.pallas_call(
        body, name="dn_chunk_bwd", grid=(N_HEADS_B, t // tm),
        in_specs=[hspec] * 9 + [pl.BlockSpec((None, tm, CH), lambda h, i: (h, i, 0)),
                                pl.BlockSpec((cps, HEAD), lambda h, i: (i, h))],
        out_specs=[hspec] * 5,
        out_shape=[jax.ShapeDtypeStruct((t, D_B), F32)] * 5,
        compiler_params=_cp("parallel", "parallel"),
    )(q, k, v, g, beta, dub, dw, dqd, dkd, dqk, dgl)


FLIPS = [(fx, fy, fc) for fx in (0, 1) for fy in (0, 1) for fc in (0, 1)][1:]


def _mesh_pos():
    return lax.axis_index("x"), lax.axis_index("y"), lax.axis_index("c")


def _peer(pos, flip):
    return tuple((1 - p) if f else p for p, f in zip(pos, flip))


def _dev_index(pos):
    return 4 * pos[0] + 2 * pos[1] + pos[2]


class _Exchange:
    def __init__(self, tensors, scatter):
        self.tensors, self.scatter, self.nt = list(tensors), list(scatter), len(tensors)
        hbm = pl.BlockSpec(memory_space=pltpu.HBM)
        self.in_specs = [hbm] * self.nt
        self.out_specs = [hbm] * self.nt
        self.out_shape = [jax.ShapeDtypeStruct(x.shape if sc else (N_DEV,) + x.shape, x.dtype)
                          for x, sc in zip(tensors, scatter)]
        self.scratch_shapes = [pltpu.SemaphoreType.DMA((self.nt * 7,)), pltpu.SemaphoreType.DMA((self.nt * 7,)),
                               pltpu.SemaphoreType.DMA((self.nt,))]

    def _copies(self, ins, outs, sems):
        send_sems, recv_sems, local_sems = sems
        pos = _mesh_pos()
        me = _dev_index(pos)

        def remote(ti, fi, landing):
            peer = _peer(pos, FLIPS[fi])
            src = ins[ti].at[_dev_index(peer)] if self.scatter[ti] else ins[ti]
            return pltpu.make_async_remote_copy(
                src_ref=src, dst_ref=outs[ti].at[landing(peer)],
                send_sem=send_sems.at[ti * 7 + fi], recv_sem=recv_sems.at[ti * 7 + fi],
                device_id=peer, device_id_type=pl.DeviceIdType.MESH)

        pairs = [(ti, fi) for ti in range(self.nt) for fi in range(7)]
        local = [pltpu.make_async_copy(ins[ti].at[me] if self.scatter[ti] else ins[ti], outs[ti].at[me],
                                       local_sems.at[ti]) for ti in range(self.nt)]
        sends = [remote(ti, fi, lambda peer: me) for ti, fi in pairs]
        recvs = [remote(ti, fi, _dev_index) for ti, fi in pairs]
        return local, sends, recvs

    def start(self, ins, outs, sems):
        local, sends, _ = self._copies(ins, outs, sems)
        for cp in local + sends:
            cp.start()

    def wait(self, ins, outs, sems):
        local, sends, recvs = self._copies(ins, outs, sems)
        for cp in recvs:
            cp.wait_recv()
        for cp in sends:
            cp.wait_send()
        for cp in local:
            cp.wait()


def _exchange(tensors, scatter, name):
    ex = _Exchange(tensors, scatter)

    def body(*refs):
        ins, outs, sems = refs[:ex.nt], refs[ex.nt:2 * ex.nt], refs[2 * ex.nt:]
        ex.start(ins, outs, sems)
        ex.wait(ins, outs, sems)

    return pl.pallas_call(
        body, name=name, in_specs=ex.in_specs, out_specs=ex.out_specs, out_shape=ex.out_shape,
        scratch_shapes=ex.scratch_shapes, compiler_params=pltpu.CompilerParams(has_side_effects=True),
    )(*tensors)


def _hosted_call(body, *, name, grid, in_specs, out_specs, out_shape, scratch_shapes, args, dims, exch=None):
    if exch is None:
        return pl.pallas_call(body, name=name, grid=grid, in_specs=in_specs, out_specs=out_specs,
                              out_shape=out_shape, scratch_shapes=scratch_shapes,
                              compiler_params=_cp(*dims))(*args)
    n_in, n_out, n_sc, ne = len(in_specs), len(out_specs), len(scratch_shapes), exch.nt
    nsteps = math.prod(grid)

    def wrapped(*refs):
        ins, ex_in = refs[:n_in], refs[n_in:n_in + ne]
        outs = refs[n_in + ne:n_in + ne + n_out]
        ex_out = refs[n_in + ne + n_out:n_in + 2 * ne + n_out]
        rest = refs[n_in + 2 * ne + n_out:]
        scratch, sems = rest[:n_sc], rest[n_sc:]
        step = pl.program_id(0)
        for ax in range(1, len(grid)):
            step = step * grid[ax] + pl.program_id(ax)

        @pl.when(step == 0)
        def _():
            exch.start(ex_in, ex_out, sems)

        body(*ins, *outs, *scratch)

        @pl.when(step == nsteps - 1)
        def _():
            exch.wait(ex_in, ex_out, sems)

    return pl.pallas_call(
        wrapped, name=name, grid=grid, in_specs=list(in_specs) + exch.in_specs,
        out_specs=list(out_specs) + exch.out_specs, out_shape=list(out_shape) + exch.out_shape,
        scratch_shapes=list(scratch_shapes) + exch.scratch_shapes,
        compiler_params=pltpu.CompilerParams(dimension_semantics=("arbitrary",) * len(grid),
                                             vmem_limit_bytes=VMEM_LIMIT, has_side_effects=True),
    )(*args, *exch.tensors)


def _adamw(land, w, m, v, name, tm=256):
    n, r, c = land.shape
    tm = r if r <= tm else max(s for s in range(8, tm + 1, 8) if r % s == 0)
    bc1 = 1.0 / (1.0 - ADAM_B1 ** ADAM_STEP)
    bc2 = 1.0 / (1.0 - ADAM_B2 ** ADAM_STEP)

    def body(l_ref, w_ref, m_ref, v_ref, g_ref, d_ref, nm_ref, nv_ref):
        g = l_ref[0].astype(F32)
        for i in range(1, n):
            g = g + l_ref[i].astype(F32)
        nm = ADAM_B1 * m_ref[...] + (1.0 - ADAM_B1) * g
        nv = ADAM_B2 * v_ref[...] + (1.0 - ADAM_B2) * (g * g)
        g_ref[...] = g
        nm_ref[...] = nm
        nv_ref[...] = nv
        d_ref[...] = -ADAM_LR * ((nm * bc1) / (jnp.sqrt(nv * bc2) + ADAM_EPS) + ADAM_WD * w_ref[...])

    spec = pl.BlockSpec((tm, c), lambda i: (i, 0))
    return pl.pallas_call(
        body, name=name, grid=(r // tm,),
        in_specs=[pl.BlockSpec((n, tm, c), lambda i: (0, i, 0)), spec, spec, spec],
        out_specs=[spec] * 4, out_shape=[jax.ShapeDtypeStruct((r, c), F32)] * 4,
        compiler_params=_cp("parallel"),
    )(land, w, m, v)


PACK_W = 2048


def _pack_rows(parts):
    flat = jnp.concatenate([p.reshape(-1).astype(F32) for p in parts])
    pad = (-flat.shape[0]) % (8 * PACK_W)
    return jnp.pad(flat, (0, pad)).reshape(-1, PACK_W)


def _unpack_rows(packed, shapes):
    flat = packed.reshape(-1)
    out, off = [], 0
    for s in shapes:
        n = math.prod(s)
        out.append(flat[off:off + n].reshape(s))
        off += n
    return out


def _col_slabs(gfull, width):
    r = gfull.shape[0]
    return jnp.transpose(gfull.reshape(r, N_DEV, width), (1, 0, 2)).astype(BF16)


def _row_slabs(gfull):
    return gfull.reshape(N_DEV, gfull.shape[0] // N_DEV, gfull.shape[1]).astype(BF16)


def _from_col_slabs(gathered):
    n, r, width = gathered.shape
    return jnp.transpose(gathered, (1, 0, 2)).reshape(r, n * width)


def _local_step(xs, target, norm_mix, wf_in, cw, a_log, dt_bias, dn_norm, rest, norm_ffn, norm_final,
                distributed=True):
    d = D_MODEL
    n_main = D_PA + 4 * D_B
    w_pa_cols = wf_in[:, :D_PA]
    w_pb_cols = jnp.concatenate([wf_in[:, D_PA:n_main], wf_in[:, n_main + 2 * N_HEADS_B:]], axis=1)
    w_small = jnp.pad(wf_in[:, n_main:n_main + 2 * N_HEADS_B], ((0, 0), (0, HEAD - 2 * N_HEADS_B)))
    a_log_bc = jnp.repeat(a_log, HEAD, axis=1)
    dt_bias_bc = jnp.repeat(dt_bias, HEAD, axis=1)

    u = _rms_fwd(xs, norm_mix)
    pa = _matmul(u, w_pa_cols, "nn", F32, 1024, 1536, d, name="proj_a")
    pb = _matmul(u, w_pb_cols, "nn", F32, 1024, 1024, d, name="proj_b")
    ps = _matmul(u, w_small, "nn", F32, 2048, HEAD, d, name="proj_small")
    os_, ls_ = [], []
    for gi, dil in enumerate(DILATIONS):
        o_g, l_g = _attn_fwd_group(pa, gi, dil)
        os_.append(o_g)
        ls_.append(l_g)
    y_att, lse = _attn_merge(os_, ls_)
    prep = _dn_prep_fwd(pb, ps, cw, a_log_bc, dt_bias_bc,
                        exch=_Exchange(rest, [False] * 6) if distributed else None)
    qn, kn, vn, gdec, beta = prep[:5]
    if distributed:
        g_pa, g_pd, g_out, g_gate, g_up, g_down = prep[5:]
        wf_pa, wf_pd, wf_out = _from_col_slabs(g_pa), g_pd.reshape(D_B, d), g_out.reshape(d, d)
        wf_gate, wf_up, wf_down = _from_col_slabs(g_gate), _from_col_slabs(g_up), g_down.reshape(D_FF, d)
    else:
        wf_pa, wf_pd, wf_out, wf_gate, wf_up, wf_down = rest
    wf_gu = jnp.concatenate([wf_gate, wf_up], axis=1)
    ub, ww, qd, kd, qk, gl = _dn_chunk_fwd(qn, kn, vn, gdec, beta)
    o_dn, states = _dn_scan_fwd(ub, ww, qd, kd, qk, gl)
    o_gated = _head_norm_fwd(o_dn, pb, dn_norm)
    y_a = _matmul(y_att, wf_pa, "nn", F32, 1024, d, D_ATTN_OUT, name="proj_attn")
    y_b = _matmul(o_gated, wf_pd, "nn", F32, 1024, d, d, name="proj_delta")
    merged = _gate_merge_fwd(pb, y_a, y_b)
    h1 = _matmul(merged, wf_out, "nn", F32, 1024, d, d, add=xs, name="out_proj")
    hn = _rms_fwd(h1, norm_ffn)
    gu = _matmul(hn, wf_gu, "nn", BF16, 1024, 1408, d, name="ffn_in")
    act = _swiglu_fwd(gu)
    h2 = _matmul(act, wf_down, "nn", F32, 512, d, D_FF, add=h1, name="ffn_out")
    loss_part, dh2, d_norm_final = _final_loss(h2, norm_final.reshape(1, d), target)

    dact = _matmul(dh2, wf_down, "nt", BF16, 1024, 1408, d, name="d_act")
    gw_down = _matmul(act, dh2, "tn", F32, 1408, d, 512, name="gw_down")
    dgu = _swiglu_bwd(gu, dact)
    dhn = _matmul(dgu, wf_gu, "nt", BF16, 1024, d, 1408, name="d_hn")
    gw_gu = _matmul(hn, dgu, "tn", F32, d, 1408, 512, name="gw_gu")
    dh1, d_norm_ffn = _rms_bwd(h1, norm_ffn, dhn, dh2)
    dmerged = _matmul(dh1, wf_out, "nt", BF16, 1024, d, d, name="d_merged")
    gw_out = _matmul(merged, dh1, "tn", F32, d, d, 512, name="gw_out")
    dya, dyb, dga, dgb = _gate_merge_bwd(pb, y_a, y_b, dmerged)
    dy_att = _matmul(dya, wf_pa, "nt", F32, 1024, D_ATTN_OUT, d, name="d_y_att")
    gw_pa = _matmul(y_att, dya, "tn", F32, D_ATTN_OUT, d, 512, name="gw_pa")
    do_gated = _matmul(dyb, wf_pd, "nt", BF16, 1024, d, d, name="d_o_gated")
    gw_pd = _matmul(o_gated, dyb, "tn", F32, d, d, 512, name="gw_pd")
    do_dn, dz, d_dn_norm = _head_norm_bwd(o_dn, pb, dn_norm, do_gated)
    dub, dww, dqd, dkd, dqk, dgl = _dn_scan_bwd(ub, ww, qd, kd, qk, gl, states, do_dn)
    dqn, dkn, dvn, dgdec, dbeta = _dn_chunk_bwd(qn, kn, vn, gdec, beta, dub, dww, dqd, dkd, dqk, dgl)
    slabs = [_col_slabs(gw_pa, d // N_DEV), _row_slabs(gw_pd), _row_slabs(gw_out),
             _col_slabs(gw_gu[:, :D_FF], D_FF // N_DEV), _col_slabs(gw_gu[:, D_FF:], D_FF // N_DEV),
             _row_slabs(gw_down)] if distributed else None
    prep = _dn_prep_bwd(pb, ps, cw, a_log_bc, dt_bias_bc, gdec, dqn, dkn, dvn, dgdec, dbeta,
                        exch=_Exchange(slabs, [True] * 6) if distributed else None)
    dc, dps, d_conv_full, d_alog_bc, d_dt_bc = prep[:5]
    dqkv_pre = _conv_bwd_input(dc, cw)
    segs = [_attn_bwd_group(pa, dy_att, y_att, lse, gi, dil) for gi, dil in enumerate(DILATIONS)]
    segs += [dqkv_pre, dz, dga, dgb]
    gws = [_matmul(u, s, "tn", F32, d, 1536, 512, name=f"gw_in_{i}") for i, s in enumerate(segs)]
    gw_small = _matmul(u, dps, "tn", F32, d, HEAD, 512, name="gw_in_small")
    g_att = jnp.concatenate(gws[:3], axis=1).reshape(d, N_HEADS_A, 3, HEAD)
    gw_in = jnp.concatenate(
        [g_att[:, :, i, :].reshape(d, D_ATTN) for i in range(3)]
        + [gws[3], gws[4], gw_small[:, :2 * N_HEADS_B], gws[5], gws[6]], axis=1)
    w_att = jnp.stack([w_pa_cols[:, i * D_ATTN:(i + 1) * D_ATTN].reshape(d, N_HEADS_A, HEAD) for i in range(3)],
                      axis=2).reshape(d, D_PA)
    du_first = _matmul_nt_segments(
        segs[:4], jnp.concatenate([w_att, w_pb_cols[:, :3 * D_B]], axis=1), F32, 1024, 1536, "d_u_0",
        exch=_Exchange([_col_slabs(gw_in, SHARD_IN)], [True]) if distributed else None)
    du_small = _matmul(dps, w_small, "nt", F32, 1024, d, HEAD, add=du_first[0], name="d_u_small")
    du = _matmul_nt_segments(segs[4:], w_pb_cols[:, 3 * D_B:], BF16, 1024, 1024, "d_u_1", add=du_small)[0]
    dx, d_norm_mix = _rms_bwd(xs, norm_mix, du, dh1)
    d_a_log = d_alog_bc.reshape(1, N_HEADS_B, HEAD)[:, :, 0]
    d_dt_bias = d_dt_bc.reshape(1, N_HEADS_B, HEAD)[:, :, 0]
    small = (d_conv_full, d_norm_mix, d_norm_ffn, d_norm_final, d_dn_norm, d_a_log, d_dt_bias)
    if distributed:
        return (loss_part, dx, [du_first[1]] + list(prep[5:])) + small
    return (loss_part, dx, gw_in, gw_pa, gw_pd, gw_out, gw_gu, gw_down) + small


def kernel(x, norm_mix, w_in, conv_w, a_log, dt_bias, dn_norm, w_proj_attn, w_proj_delta, w_out, norm_ffn, w_gate, w_up, w_down, norm_final, loss_target, m_norm_mix, m_w_in, m_conv_w, m_a_log, m_dt_bias, m_dn_norm, m_w_proj_attn, m_w_proj_delta, m_w_out, m_norm_ffn, m_w_gate, m_w_up, m_w_down, m_norm_final, v_norm_mix, v_w_in, v_conv_w, v_a_log, v_dt_bias, v_dn_norm, v_w_proj_attn, v_w_proj_delta, v_w_out, v_norm_ffn, v_w_gate, v_w_up, v_w_down, v_norm_final):
    d = D_MODEL
    xs = x[0]
    target = loss_target[0]
    me = _dev_index(_mesh_pos())

    g_in, g_conv = _exchange([w_in[0].astype(BF16), conv_w[0]], [False] * 2, "gather_w_in")
    rest = [w[0].astype(BF16) for w in (w_proj_attn, w_proj_delta, w_out, w_gate, w_up, w_down)]
    (loss_part, dx, landed, d_conv_full, d_norm_mix, d_norm_ffn, d_norm_final, d_dn_norm, d_a_log,
     d_dt_bias) = _local_step(xs, target, norm_mix, _from_col_slabs(g_in), _from_col_slabs(g_conv), a_log, dt_bias,
                              dn_norm, rest, norm_ffn, norm_final)

    small_shapes = [(1, d), (1, d), (d,), (1, HEAD), (1, N_HEADS_B), (1, N_HEADS_B), (1, 1), (CONV_WIDTH, 3 * D_B)]
    packed = _pack_rows([d_norm_mix, d_norm_ffn, d_norm_final, d_dn_norm, d_a_log, d_dt_bias,
                         loss_part[:, :1], d_conv_full])
    landed = list(landed) + list(_exchange([packed], [False], "gather_small_grads"))
    zero1 = jnp.zeros((1, 1), F32)
    zconv = jnp.zeros((CONV_WIDTH, 3 * D_B), F32)
    small_w = _pack_rows([norm_mix, norm_ffn, norm_final, dn_norm, a_log, dt_bias, zero1, zconv])
    small_m = _pack_rows([m_norm_mix, m_norm_ffn, m_norm_final, m_dn_norm, m_a_log, m_dt_bias, zero1, zconv])
    small_v = _pack_rows([v_norm_mix, v_norm_ffn, v_norm_final, v_dn_norm, v_a_log, v_dt_bias, zero1, zconv])
    small = [_unpack_rows(z, small_shapes) for z in _adamw(landed[7], small_w, small_m, small_v, "adamw_small")]
    loss = small[0][6].reshape(())
    conv_shard = 3 * D_B // N_DEV
    g_conv_own = lax.dynamic_slice_in_dim(small[0][7], me * conv_shard, conv_shard, axis=1)
    r_conv = _adamw(g_conv_own[None], conv_w[0], m_conv_w[0], v_conv_w[0], "adamw_conv")
    big = [_adamw(landed[i], w[0], m[0], v[0], f"adamw_{i}") for i, (w, m, v) in enumerate([
        (w_in, m_w_in, v_w_in), (w_proj_attn, m_w_proj_attn, v_w_proj_attn),
        (w_proj_delta, m_w_proj_delta, v_w_proj_delta), (w_out, m_w_out, v_w_out),
        (w_gate, m_w_gate, v_w_gate), (w_up, m_w_up, v_w_up), (w_down, m_w_down, v_w_down)])]

    def leaves(k):
        sm = small[k]
        return [sm[0], big[0][k][None], r_conv[k][None], sm[4], sm[5], sm[3], big[1][k][None], big[2][k][None],
                big[3][k][None], sm[1], big[4][k][None], big[5][k][None], big[6][k][None], sm[2]]

    return (loss, dx[None], *leaves(0), *leaves(1), *leaves(2), *leaves(3))
```

```python
import math

import jax
import jax.numpy as jnp
from jax import lax
from jax.experimental import pallas as pl
from jax.experimental.pallas import tpu as pltpu

F32 = jnp.float32
BF16 = jnp.bfloat16
HI = lax.Precision.HIGH

D_MODEL = 1024
N_DEV = 8
HEAD = 128
N_HEADS_A = 12
HEADS_PER_GROUP = 4
DILATIONS = (1, 4, 16)
BLOCK_A = 128
D_ATTN = N_HEADS_A * HEAD
D_ATTN_OUT = HEADS_PER_GROUP * HEAD
N_HEADS_B = 8
D_B = N_HEADS_B * HEAD
CONV_WIDTH = 4
CH = 64
CHUNK_GROUP = 16
SCAN_HEADS = 4
D_FF = 2816
EPS = 1e-6
D_IN = 3 * D_ATTN + 4 * D_B + 2 * N_HEADS_B + 2 * D_MODEL
SHARD_IN = D_IN // N_DEV
PB_Z, PB_GATE = 3072, 4096
D_PA = 3 * D_ATTN
ADAM_LR, ADAM_B1, ADAM_B2, ADAM_EPS, ADAM_WD, ADAM_STEP = 0.001, 0.9, 0.999, 1e-08, 0.01, 10
VMEM_LIMIT = 56 * 1024 * 1024

NN = ((1,), (0,))
NT = ((1,), (1,))
TN = ((0,), (0,))


def _dot(a, b, dims=NN, prec=None):
    return lax.dot_general(a, b, (dims, ((), ())), precision=prec, preferred_element_type=F32)


def _bdot(a, b, dims=NN):
    return _dot(a.astype(BF16), b.astype(BF16), dims)


def _hdot(a, b, dims=NN):
    return _dot(a.astype(F32), b.astype(F32), dims, HI)


def _cp(*sem):
    return pltpu.CompilerParams(dimension_semantics=sem, vmem_limit_bytes=VMEM_LIMIT)


def _sigmoid(x):
    return 0.5 * jnp.tanh(0.5 * x) + 0.5


def _softplus(x):
    return jnp.maximum(x, 0.0) + jnp.log(1.0 + jnp.exp(-jnp.abs(x)))


def _rowsum(x):
    return jnp.sum(x, axis=-1, keepdims=True)


def _matmul(a, b, mode, out_dtype, tm, tn, tk, add=None, name="mm"):
    if mode == "nn":
        (m, k), (k2, n) = a.shape, b.shape
    elif mode == "nt":
        (m, k), (n, k2) = a.shape, b.shape
    else:
        (k, m), (k2, n) = a.shape, b.shape
    assert k == k2, (a.shape, b.shape, mode)
    tm, tn, tk = min(tm, m), min(tn, n), min(tk, k)
    assert m % tm == 0 and n % tn == 0 and k % tk == 0, (a.shape, b.shape, tm, tn, tk)
    nk = k // tk
    dims = {"nn": NN, "nt": NT, "tn": TN}[mode]

    def body(*refs):
        if add is None:
            a_ref, b_ref, o_ref, acc = refs
            add_ref = None
        else:
            a_ref, b_ref, add_ref, o_ref, acc = refs
        kk = pl.program_id(2)

        @pl.when(kk == 0)
        def _():
            acc[...] = jnp.zeros_like(acc)

        acc[...] += _bdot(a_ref[...], b_ref[...], dims)

        @pl.when(kk == nk - 1)
        def _():
            r = acc[...]
            if add_ref is not None:
                r = r + add_ref[...].astype(F32)
            o_ref[...] = r.astype(out_dtype)

    a_spec = (pl.BlockSpec((tk, tm), lambda i, j, kk: (kk, i)) if mode == "tn"
              else pl.BlockSpec((tm, tk), lambda i, j, kk: (i, kk)))
    b_spec = (pl.BlockSpec((tn, tk), lambda i, j, kk: (j, kk)) if mode == "nt"
              else pl.BlockSpec((tk, tn), lambda i, j, kk: (kk, j)))
    in_specs = [a_spec, b_spec]
    args = [a, b]
    if add is not None:
        in_specs.append(pl.BlockSpec((tm, tn), lambda i, j, kk: (i, j)))
        args.append(add)
    return pl.pallas_call(
        body, name=name, grid=(m // tm, n // tn, nk),
        in_specs=in_specs, out_specs=pl.BlockSpec((tm, tn), lambda i, j, kk: (i, j)),
        out_shape=jax.ShapeDtypeStruct((m, n), out_dtype),
        scratch_shapes=[pltpu.VMEM((tm, tn), F32)],
        compiler_params=_cp("parallel", "parallel", "arbitrary"),
    )(*args)


def _matmul_nt_segments(segs, b, out_dtype, tm, tk, name, add=None, exch=None):
    m = segs[0].shape[0]
    n, ktot = b.shape
    ns = len(segs)
    nks = [s.shape[1] // tk for s in segs]
    assert all(s.shape[1] % tk == 0 for s in segs) and sum(s.shape[1] for s in segs) == ktot and m % tm == 0
    starts = [sum(nks[:i]) for i in range(ns)]
    nk = sum(nks)

    def body(*refs):
        seg_refs, b_ref = refs[:ns], refs[ns]
        add_ref = refs[ns + 1] if add is not None else None
        o_ref, acc = refs[-2], refs[-1]
        kk = pl.program_id(1)

        @pl.when(kk == 0)
        def _():
            acc[...] = jnp.zeros_like(acc)

        for a_ref, k0, nk_s in zip(seg_refs, starts, nks):
            @pl.when(jnp.logical_and(kk >= k0, kk < k0 + nk_s))
            def _(a_ref=a_ref):
                acc[...] += _bdot(a_ref[...], b_ref[...], NT)

        @pl.when(kk == nk - 1)
        def _():
            r = acc[...]
            if add_ref is not None:
                r = r + add_ref[...].astype(F32)
            o_ref[...] = r.astype(out_dtype)

    def seg_spec(k0, nk_s):
        return pl.BlockSpec((tm, tk), lambda i, kk: (i, jnp.clip(kk - k0, 0, nk_s - 1)))

    row = pl.BlockSpec((tm, n), lambda i, kk: (i, 0))
    in_specs = [seg_spec(k0, nk_s) for k0, nk_s in zip(starts, nks)] + [pl.BlockSpec((n, tk), lambda i, kk: (0, kk))]
    args = list(segs) + [b]
    if add is not None:
        in_specs.append(row)
        args.append(add)
    return _hosted_call(body, name=name, grid=(m // tm, nk), in_specs=in_specs, out_specs=[row],
                        out_shape=[jax.ShapeDtypeStruct((m, n), out_dtype)],
                        scratch_shapes=[pltpu.VMEM((tm, n), F32)], args=args,
                        dims=("parallel", "arbitrary"), exch=exch)


def _row_spec(tm, cols, cb=0):
    return pl.BlockSpec((tm, cols), lambda i, cb=cb: (i, cb))


def _bcast_spec(rows, cols):
    return pl.BlockSpec((rows, cols), lambda i: (0, 0))


def _rms_fwd(x, w, tm=512):
    t, d = x.shape

    def body(x_ref, w_ref, o_ref):
        xv = x_ref[...]
        r = lax.rsqrt(jnp.mean(xv * xv, axis=-1, keepdims=True) + EPS)
        o_ref[...] = (xv * r * w_ref[...]).astype(BF16)

    return pl.pallas_call(
        body, name="rms_fwd", grid=(t // tm,),
        in_specs=[_row_spec(tm, d), _bcast_spec(1, d)], out_specs=_row_spec(tm, d),
        out_shape=jax.ShapeDtypeStruct((t, d), BF16), compiler_params=_cp("parallel"),
    )(x, w)


def _rms_bwd(x, w, dy, resid, tm=512):
    t, d = x.shape

    def body(x_ref, w_ref, dy_ref, res_ref, dx_ref, dw_ref):
        xv = x_ref[...]
        r = lax.rsqrt(jnp.mean(xv * xv, axis=-1, keepdims=True) + EPS)
        xh = xv * r
        dyv = dy_ref[...].astype(F32)
        dxh = dyv * w_ref[...]
        dx_ref[...] = res_ref[...] + r * (dxh - xh * jnp.mean(dxh * xh, axis=-1, keepdims=True))

        @pl.when(pl.program_id(0) == 0)
        def _():
            dw_ref[...] = jnp.zeros_like(dw_ref)

        dw_ref[...] += jnp.sum(dyv * xh, axis=0, keepdims=True)

    return pl.pallas_call(
        body, name="rms_bwd", grid=(t // tm,),
        in_specs=[_row_spec(tm, d), _bcast_spec(1, d), _row_spec(tm, d), _row_spec(tm, d)],
        out_specs=[_row_spec(tm, d), _bcast_spec(1, d)],
        out_shape=[jax.ShapeDtypeStruct((t, d), F32), jax.ShapeDtypeStruct((1, d), F32)],
        compiler_params=_cp("arbitrary"),
    )(x, w, dy, resid)


def _final_loss(h, w, target, tm=512):
    t, d = h.shape

    def body(h_ref, w_ref, t_ref, loss_ref, dh_ref, dw_ref):
        hv = h_ref[...]
        r = lax.rsqrt(jnp.mean(hv * hv, axis=-1, keepdims=True) + EPS)
        xh = hv * r
        wv = w_ref[...]
        err = xh * wv - t_ref[...]
        dy = err * (1.0 / d)
        dxh = dy * wv
        dh_ref[...] = r * (dxh - xh * jnp.mean(dxh * xh, axis=-1, keepdims=True))

        @pl.when(pl.program_id(0) == 0)
        def _():
            dw_ref[...] = jnp.zeros_like(dw_ref)
            loss_ref[...] = jnp.zeros_like(loss_ref)

        dw_ref[...] += jnp.sum(dy * xh, axis=0, keepdims=True)
        part = 0.5 * jnp.sum(jnp.mean(err * err, axis=-1, keepdims=True), axis=0, keepdims=True)
        loss_ref[...] += part + jnp.zeros((1, HEAD), F32)

    return pl.pallas_call(
        body, name="final_loss", grid=(t // tm,),
        in_specs=[_row_spec(tm, d), _bcast_spec(1, d), _row_spec(tm, d)],
        out_specs=[_bcast_spec(1, HEAD), _row_spec(tm, d), _bcast_spec(1, d)],
        out_shape=[jax.ShapeDtypeStruct((1, HEAD), F32), jax.ShapeDtypeStruct((t, d), F32),
                   jax.ShapeDtypeStruct((1, d), F32)],
        compiler_params=_cp("arbitrary"),
    )(h, w, target)


def _ffn_in(hn, w_gate, w_up, tm=1024, tn=1408):
    t, d = hn.shape
    ff = w_gate.shape[1]

    def body(a_ref, wg_ref, wu_ref, g_ref, u_ref, act_ref):
        a = a_ref[...]
        g = _bdot(a, wg_ref[...])
        u = _bdot(a, wu_ref[...])
        g_ref[...] = g.astype(BF16)
        u_ref[...] = u.astype(BF16)
        gq = g.astype(BF16).astype(F32)
        act_ref[...] = (gq * _sigmoid(gq) * u.astype(BF16).astype(F32)).astype(BF16)

    tile = pl.BlockSpec((tm, tn), lambda i, j: (i, j))
    wspec = pl.BlockSpec((d, tn), lambda i, j: (0, j))
    return pl.pallas_call(
        body, name="ffn_in", grid=(t // tm, ff // tn),
        in_specs=[pl.BlockSpec((tm, d), lambda i, j: (i, 0)), wspec, wspec], out_specs=[tile] * 3,
        out_shape=[jax.ShapeDtypeStruct((t, ff), BF16)] * 3, compiler_params=_cp("parallel", "parallel"),
    )(hn, w_gate, w_up)


def _ffn_act_bwd(dh, w_down, g, u, tm=1024, tn=1408):
    t, d = dh.shape
    ff = w_down.shape[0]

    def body(a_ref, w_ref, g_ref, u_ref, dg_ref, du_ref):
        dv = _bdot(a_ref[...], w_ref[...], NT).astype(BF16).astype(F32)
        gv = g_ref[...].astype(F32)
        sg = _sigmoid(gv)
        dg_ref[...] = (dv * u_ref[...].astype(F32) * (sg + gv * sg * (1.0 - sg))).astype(BF16)
        du_ref[...] = (dv * gv * sg).astype(BF16)

    tile = pl.BlockSpec((tm, tn), lambda i, j: (i, j))
    return pl.pallas_call(
        body, name="ffn_act_bwd", grid=(t // tm, ff // tn),
        in_specs=[pl.BlockSpec((tm, d), lambda i, j: (i, 0)), pl.BlockSpec((tn, d), lambda i, j: (j, 0)),
                  tile, tile],
        out_specs=[tile] * 2, out_shape=[jax.ShapeDtypeStruct((t, ff), BF16)] * 2,
        compiler_params=_cp("parallel", "parallel"),
    )(dh, w_down, g, u)


def _gate_merge_fwd(pb, ya, yb, tm=512):
    t, d = ya.shape
    cb = PB_GATE // d

    def body(ga_ref, gb_ref, ya_ref, yb_ref, o_ref):
        o_ref[...] = (_sigmoid(ga_ref[...]) * ya_ref[...] + _sigmoid(gb_ref[...]) * yb_ref[...]).astype(BF16)

    return pl.pallas_call(
        body, name="gate_merge_fwd", grid=(t // tm,),
        in_specs=[_row_spec(tm, d, cb), _row_spec(tm, d, cb + 1), _row_spec(tm, d), _row_spec(tm, d)],
        out_specs=_row_spec(tm, d),
        out_shape=jax.ShapeDtypeStruct((t, d), BF16), compiler_params=_cp("parallel"),
    )(pb, pb, ya, yb)


def _gate_merge_bwd(pb, ya, yb, dm, tm=512):
    t, d = ya.shape
    cb = PB_GATE // d

    def body(ga_ref, gb_ref, ya_ref, yb_ref, dm_ref, dya_ref, dyb_ref, dga_ref, dgb_ref):
        dmv = dm_ref[...].astype(F32)
        sa = _sigmoid(ga_ref[...])
        sb = _sigmoid(gb_ref[...])
        dya_ref[...] = (dmv * sa).astype(BF16)
        dyb_ref[...] = (dmv * sb).astype(BF16)
        dga_ref[...] = (dmv * ya_ref[...] * sa * (1.0 - sa)).astype(BF16)
        dgb_ref[...] = (dmv * yb_ref[...] * sb * (1.0 - sb)).astype(BF16)

    return pl.pallas_call(
        body, name="gate_merge_bwd", grid=(t // tm,),
        in_specs=[_row_spec(tm, d, cb), _row_spec(tm, d, cb + 1), _row_spec(tm, d), _row_spec(tm, d),
                  _row_spec(tm, d)],
        out_specs=[_row_spec(tm, d)] * 4,
        out_shape=[jax.ShapeDtypeStruct((t, d), BF16)] * 4, compiler_params=_cp("parallel"),
    )(pb, pb, ya, yb, dm)


def _head_norm_fwd(o, pb, wn, tm=512):
    t, d = o.shape
    nh = d // HEAD

    def body(o_ref, z_ref, w_ref, out_ref):
        wv = w_ref[...]
        for h in range(nh):
            sl = slice(h * HEAD, (h + 1) * HEAD)
            ov = o_ref[:, sl]
            zv = z_ref[:, sl]
            r = lax.rsqrt(jnp.mean(ov * ov, axis=-1, keepdims=True) + EPS)
            out_ref[:, sl] = (ov * r * wv * (zv * _sigmoid(zv))).astype(BF16)

    return pl.pallas_call(
        body, name="head_norm_fwd", grid=(t // tm,),
        in_specs=[_row_spec(tm, d), _row_spec(tm, d, PB_Z // d), _bcast_spec(1, HEAD)],
        out_specs=_row_spec(tm, d),
        out_shape=jax.ShapeDtypeStruct((t, d), BF16), compiler_params=_cp("parallel"),
    )(o, pb, wn)


def _head_norm_bwd(o, pb, wn, dout, tm=512):
    t, d = o.shape
    nh = d // HEAD

    def body(o_ref, z_ref, w_ref, d_ref, do_ref, dz_ref, dw_ref):
        wv = w_ref[...]
        dw_acc = jnp.zeros((1, HEAD), F32)
        for h in range(nh):
            sl = slice(h * HEAD, (h + 1) * HEAD)
            ov = o_ref[:, sl]
            zv = z_ref[:, sl]
            dv = d_ref[:, sl].astype(F32)
            r = lax.rsqrt(jnp.mean(ov * ov, axis=-1, keepdims=True) + EPS)
            xh = ov * r
            sz = _sigmoid(zv)
            dn = dv * (zv * sz)
            dz_ref[:, sl] = (dv * xh * wv * (sz + zv * sz * (1.0 - sz))).astype(BF16)
            dxh = dn * wv
            do_ref[:, sl] = r * (dxh - xh * jnp.mean(dxh * xh, axis=-1, keepdims=True))
            dw_acc = dw_acc + jnp.sum(dn * xh, axis=0, keepdims=True)

        @pl.when(pl.program_id(0) == 0)
        def _():
            dw_ref[...] = jnp.zeros_like(dw_ref)

        dw_ref[...] += dw_acc

    return pl.pallas_call(
        body, name="head_norm_bwd", grid=(t // tm,),
        in_specs=[_row_spec(tm, d), _row_spec(tm, d, PB_Z // d), _bcast_spec(1, HEAD), _row_spec(tm, d)],
        out_specs=[_row_spec(tm, d), _row_spec(tm, d), _bcast_spec(1, HEAD)],
        out_shape=[jax.ShapeDtypeStruct((t, d), F32), jax.ShapeDtypeStruct((t, d), BF16),
                   jax.ShapeDtypeStruct((1, HEAD), F32)],
        compiler_params=_cp("arbitrary"),
    )(o, pb, wn, dout)


def _attn_bias(gi, hh, dil):
    i = lax.broadcasted_iota(jnp.int32, (BLOCK_A, BLOCK_A), 0)
    j = lax.broadcasted_iota(jnp.int32, (BLOCK_A, BLOCK_A), 1)
    hf = (gi * HEADS_PER_GROUP + hh + 1).astype(F32)
    slope = jnp.exp(jnp.full((1, BLOCK_A), -8.0 * math.log(2.0) / N_HEADS_A, F32) * hf) * float(dil)
    d_prev = (BLOCK_A + i - j).astype(F32)
    d_cur = (i - j).astype(F32)
    return -slope * d_prev, -slope * d_cur, j >= i, j <= i


ATTN_TOKENS = 2048


def _sub_rows(a, r, dil):
    start = a * BLOCK_A * dil + r
    return pl.ds(start, BLOCK_A) if dil == 1 else pl.ds(start, BLOCK_A, stride=dil)


def _attn_fwd_group(pa, gi, dil, tb=ATTN_TOKENS):
    t = pa.shape[0]
    tb = min(tb, t)
    hb = BLOCK_A * dil
    nb = tb // hb
    scale = HEAD ** -0.5

    def body(q_ref, k_ref, v_ref, kp_ref, vp_ref, o_ref, l_ref):
        hh = pl.program_id(0)
        step = pl.program_id(1)
        b_prev, b_cur, m_prev, m_cur = _attn_bias(gi, hh, dil)
        m_first = jnp.logical_and(m_prev, step > 0)
        for r in range(dil):
            kp, vp = kp_ref[_sub_rows(0, r, dil), :], vp_ref[_sub_rows(0, r, dil), :]
            for a in range(nb):
                rows = _sub_rows(a, r, dil)
                q, kc, vc = q_ref[rows, :], k_ref[rows, :], v_ref[rows, :]
                s_p = jnp.where(m_first if a == 0 else m_prev, _bdot(q, kp, NT) * scale + b_prev, -1e30)
                s_c = jnp.where(m_cur, _bdot(q, kc, NT) * scale + b_cur, -1e30)
                m = jnp.maximum(jnp.max(s_p, axis=-1, keepdims=True), jnp.max(s_c, axis=-1, keepdims=True))
                p_p = jnp.exp(s_p - m)
                p_c = jnp.exp(s_c - m)
                den = _rowsum(p_p) + _rowsum(p_c)
                o_ref[rows, :] = (_bdot(p_p, vp) + _bdot(p_c, vc)) / den
                l_ref[rows, :] = (m + jnp.log(den)) + jnp.zeros((BLOCK_A, HEAD), F32)
                kp, vp = kc, vc

    def col(base):
        return lambda hh, s: (s, base + hh)

    def col_prev(base):
        return lambda hh, s: (jnp.maximum(s * nb - 1, 0), base + hh)

    qb, kb, vb = gi * HEADS_PER_GROUP, N_HEADS_A + gi * HEADS_PER_GROUP, 2 * N_HEADS_A + gi * HEADS_PER_GROUP
    ospec = pl.BlockSpec((tb, HEAD), lambda hh, s: (s, hh))
    return pl.pallas_call(
        body, name=f"attn_fwd_g{gi}", grid=(HEADS_PER_GROUP, t // tb),
        in_specs=[pl.BlockSpec((tb, HEAD), col(qb)), pl.BlockSpec((tb, HEAD), col(kb)),
                  pl.BlockSpec((tb, HEAD), col(vb)),
                  pl.BlockSpec((hb, HEAD), col_prev(kb)), pl.BlockSpec((hb, HEAD), col_prev(vb))],
        out_specs=[ospec, ospec],
        out_shape=[jax.ShapeDtypeStruct((t, D_ATTN_OUT), F32)] * 2,
        compiler_params=_cp("parallel", "parallel"),
    )(pa, pa, pa, pa, pa)


def _attn_merge(os, ls, tm=512):
    t, d = os[0].shape

    def body(o0, o1, o2, l0, l1, l2, y_ref, lse_ref):
        a0, a1, a2 = l0[...], l1[...], l2[...]
        m = jnp.maximum(jnp.maximum(a0, a1), a2)
        e0, e1, e2 = jnp.exp(a0 - m), jnp.exp(a1 - m), jnp.exp(a2 - m)
        den = e0 + e1 + e2
        y_ref[...] = (e0 * o0[...] + e1 * o1[...] + e2 * o2[...]) / den
        lse_ref[...] = m + jnp.log(den)

    return pl.pallas_call(
        body, name="attn_merge", grid=(t // tm,),
        in_specs=[_row_spec(tm, d)] * 6, out_specs=[_row_spec(tm, d)] * 2,
        out_shape=[jax.ShapeDtypeStruct((t, d), F32)] * 2,
        compiler_params=_cp("parallel"),
    )(*os, *ls)


def _attn_bwd_group(pa, dy, y, lse, gi, dil, tb=ATTN_TOKENS):
    t = pa.shape[0]
    tb = min(tb, t)
    hb = BLOCK_A * dil
    nb = tb // hb
    nsteps = t // tb
    scale = HEAD ** -0.5

    def body(q_ref, k_ref, v_ref, dy_ref, y_ref, l_ref, kp_ref, vp_ref, d_ref,
             dq_s, dk_s, dv_s, carry_k, carry_v):
        hh = pl.program_id(0)
        step = pl.program_id(1)

        @pl.when(step == 0)
        def _():
            carry_k[...] = jnp.zeros_like(carry_k)
            carry_v[...] = jnp.zeros_like(carry_v)

        b_prev, b_cur, m_prev, m_cur = _attn_bias(gi, hh, dil)
        m_first = jnp.logical_and(m_prev, step < nsteps - 1)
        for r in range(dil):
            halo = _sub_rows(0, r, dil)
            dk_in, dv_in = carry_k[halo, :], carry_v[halo, :]
            kp, vp = kp_ref[halo, :], vp_ref[halo, :]
            prev_rows = None
            dk_pend = dv_pend = None
            for a in range(nb):
                rows = _sub_rows(a, r, dil)
                q, kc, vc = q_ref[rows, :], k_ref[rows, :], v_ref[rows, :]
                dyb, lb = dy_ref[rows, :], l_ref[rows, :]
                delta = _rowsum(dyb * y_ref[rows, :])
                mp = m_first if a == 0 else m_prev
                s = _bdot(q, kp, NT) * scale + b_prev
                p = jnp.where(mp, jnp.exp(jnp.where(mp, s - lb, 0.0)), 0.0)
                ds = p * (_bdot(dyb, vp, NT) - delta)
                dq = _bdot(ds, kp)
                dk_prev, dv_prev = _bdot(ds, q, TN), _bdot(p, dyb, TN)
                if a == 0:
                    carry_k[halo, :] = dk_prev
                    carry_v[halo, :] = dv_prev
                else:
                    dk_s[prev_rows, :] = dk_pend + dk_prev
                    dv_s[prev_rows, :] = dv_pend + dv_prev
                s = _bdot(q, kc, NT) * scale + b_cur
                p = jnp.where(m_cur, jnp.exp(jnp.where(m_cur, s - lb, 0.0)), 0.0)
                ds = p * (_bdot(dyb, vc, NT) - delta)
                dq_s[rows, :] = dq + _bdot(ds, kc)
                dk_pend, dv_pend = _bdot(ds, q, TN), _bdot(p, dyb, TN)
                prev_rows, kp, vp = rows, kc, vc
            dk_s[prev_rows, :] = dk_pend + dk_in
            dv_s[prev_rows, :] = dv_pend + dv_in
        d_ref[:, :HEAD] = (dq_s[...] * scale).astype(BF16)
        d_ref[:, HEAD:2 * HEAD] = (dk_s[...] * scale).astype(BF16)
        d_ref[:, 2 * HEAD:] = dv_s[...].astype(BF16)

    def col(base):
        return lambda hh, s: (nsteps - 1 - s, base + hh)

    def col_prev(base):
        return lambda hh, s: (jnp.maximum((nsteps - 1 - s) * nb - 1, 0), base + hh)

    qb, kb, vb = gi * HEADS_PER_GROUP, N_HEADS_A + gi * HEADS_PER_GROUP, 2 * N_HEADS_A + gi * HEADS_PER_GROUP
    big, small = (tb, HEAD), (hb, HEAD)
    return pl.pallas_call(
        body, name=f"attn_bwd_g{gi}", grid=(HEADS_PER_GROUP, nsteps),
        in_specs=[pl.BlockSpec(big, col(qb)), pl.BlockSpec(big, col(kb)), pl.BlockSpec(big, col(vb)),
                  pl.BlockSpec(big, col(0)), pl.BlockSpec(big, col(0)), pl.BlockSpec(big, col(0)),
                  pl.BlockSpec(small, col_prev(kb)), pl.BlockSpec(small, col_prev(vb))],
        out_specs=pl.BlockSpec((tb, 3 * HEAD), col(0)),
        out_shape=jax.ShapeDtypeStruct((t, 3 * D_ATTN_OUT), BF16),
        scratch_shapes=[pltpu.VMEM(big, F32)] * 3 + [pltpu.VMEM(small, F32)] * 2,
        compiler_params=_cp("parallel", "arbitrary"),
    )(pa, pa, pa, dy, y, lse, pa, pa)


def _shift_down(cur, prev8, s):
    if s == 0:
        return cur
    rolled = pltpu.roll(cur, s, 0)
    prolled = pltpu.roll(prev8, s, 0)
    rid = lax.broadcasted_iota(jnp.int32, prev8.shape, 0)
    top = jnp.where(rid < s, prolled, rolled[:8])
    return jnp.concatenate([top, rolled[8:]], axis=0)


def _shift_up(cur, next8, s):
    if s == 0:
        return cur
    n = cur.shape[0]
    rolled = pltpu.roll(cur, n - s, 0)
    nrolled = pltpu.roll(next8, 8 - s, 0)
    rid = lax.broadcasted_iota(jnp.int32, next8.shape, 0)
    bottom = jnp.where(rid >= 8 - s, nrolled, rolled[n - 8:])
    return jnp.concatenate([rolled[:n - 8], bottom], axis=0)


def _conv(xv, prev8, wv):
    c = jnp.zeros_like(xv)
    shifted = []
    for s in range(CONV_WIDTH):
        xs = _shift_down(xv, prev8, s)
        shifted.append(xs)
        c = c + wv[CONV_WIDTH - 1 - s:CONV_WIDTH - s, :] * xs
    return c, shifted


def _head_expand(psv, first):
    tm = psv.shape[0]
    return jnp.concatenate([jnp.broadcast_to(psv[:, first + h:first + h + 1], (tm, HEAD))
                            for h in range(N_HEADS_B)], axis=1)


def _head_collect(x, first):
    lane = lax.broadcasted_iota(jnp.int32, (x.shape[0], HEAD), 1)
    out = jnp.zeros((x.shape[0], HEAD), F32)
    for h in range(N_HEADS_B):
        out = jnp.where(lane == first + h, x[:, h * HEAD:(h + 1) * HEAD], out)
    return out


def _dn_prep_fwd(pb, ps, conv_w, a_log_bc, dt_bias_bc, tm=256, exch=None):
    t = pb.shape[0]
    c3 = 3 * D_B
    r8 = tm // 8

    def body(x_ref, xp_ref, ps_ref, w_ref, al_ref, dt_ref, q_ref, k_ref, v_ref, g_ref, beta_ref):
        first = pl.program_id(0) > 0

        def silu_conv(cols):
            c, _ = _conv(x_ref[:, cols], jnp.where(first, xp_ref[:, cols], 0.0), w_ref[:, cols])
            return c * _sigmoid(c)

        for h in range(N_HEADS_B):
            sl = slice(h * HEAD, (h + 1) * HEAD)
            sq = silu_conv(sl)
            q_ref[:, sl] = sq * lax.rsqrt(_rowsum(sq * sq) + EPS) * (HEAD ** -0.5)
            sk = silu_conv(slice(D_B + h * HEAD, D_B + (h + 1) * HEAD))
            k_ref[:, sl] = sk * lax.rsqrt(_rowsum(sk * sk) + EPS)
            v_ref[:, sl] = silu_conv(slice(2 * D_B + h * HEAD, 2 * D_B + (h + 1) * HEAD))
        psv = ps_ref[...]
        beta_ref[...] = _sigmoid(_head_expand(psv, 0))
        g_ref[...] = -jnp.exp(al_ref[...]) * _softplus(_head_expand(psv, N_HEADS_B) + dt_ref[...])

    return _hosted_call(
        body, name="dn_prep_fwd", grid=(t // tm,),
        in_specs=[_row_spec(tm, c3, 0),
                  pl.BlockSpec((8, c3), lambda i: (jnp.maximum(i * r8 - 1, 0), 0)),
                  _row_spec(tm, HEAD),
                  _bcast_spec(CONV_WIDTH, c3), _bcast_spec(1, D_B), _bcast_spec(1, D_B)],
        out_specs=[_row_spec(tm, D_B)] * 5,
        out_shape=[jax.ShapeDtypeStruct((t, D_B), F32)] * 5, scratch_shapes=[],
        args=(pb, pb, ps, conv_w, a_log_bc, dt_bias_bc), dims=("parallel",), exch=exch)


def _dn_prep_bwd(pb, ps, conv_w, a_log_bc, dt_bias_bc, g, dq, dk, dv, dg, dbeta, tm=128, exch=None):
    t = pb.shape[0]
    c3 = 3 * D_B
    r8 = tm // 8

    def body(x_ref, xp_ref, ps_ref, w_ref, al_ref, dt_ref, g_ref, dq_ref, dk_ref, dv_ref, dg_ref, db_ref,
             dc_ref, dps_ref, dw_ref, dal_ref, ddt_ref):
        first = pl.program_id(0) > 0

        @pl.when(pl.program_id(0) == 0)
        def _():
            dw_ref[...] = jnp.zeros_like(dw_ref)
            dal_ref[...] = jnp.zeros_like(dal_ref)
            ddt_ref[...] = jnp.zeros_like(ddt_ref)

        def column_block(cols, d_ref, sl, mult, normed):
            c, shifted = _conv(x_ref[:, cols], jnp.where(first, xp_ref[:, cols], 0.0), w_ref[:, cols])
            sg = _sigmoid(c)
            dsilu = sg + c * sg * (1.0 - sg)
            dyv = d_ref[:, sl]
            if normed:
                sv = c * sg
                r = lax.rsqrt(_rowsum(sv * sv) + EPS)
                yh = sv * r
                dyv = dyv * mult
                dyv = r * (dyv - yh * _rowsum(dyv * yh))
            dcv = dyv * dsilu
            dc_ref[:, cols] = dcv
            for sft in range(CONV_WIDTH):
                j = CONV_WIDTH - 1 - sft
                dw_ref[j:j + 1, cols] += jnp.sum(dcv * shifted[sft], axis=0, keepdims=True)

        for h in range(N_HEADS_B):
            sl = slice(h * HEAD, (h + 1) * HEAD)
            column_block(sl, dq_ref, sl, HEAD ** -0.5, True)
            column_block(slice(D_B + h * HEAD, D_B + (h + 1) * HEAD), dk_ref, sl, 1.0, True)
            column_block(slice(2 * D_B + h * HEAD, 2 * D_B + (h + 1) * HEAD), dv_ref, sl, 1.0, False)
        psv = ps_ref[...]
        beta = _sigmoid(_head_expand(psv, 0))
        dgv = dg_ref[...]
        da = dgv * (-jnp.exp(al_ref[...])) * _sigmoid(_head_expand(psv, N_HEADS_B) + dt_ref[...])
        dps_ref[...] = _head_collect(db_ref[...] * beta * (1.0 - beta), 0) + _head_collect(da, N_HEADS_B)
        dal_ref[...] += jnp.sum(dgv * g_ref[...], axis=0, keepdims=True)
        ddt_ref[...] += jnp.sum(da, axis=0, keepdims=True)

    row = _row_spec(tm, D_B)
    return _hosted_call(
        body, name="dn_prep_bwd", grid=(t // tm,),
        in_specs=[_row_spec(tm, c3, 0),
                  pl.BlockSpec((8, c3), lambda i: (jnp.maximum(i * r8 - 1, 0), 0)),
                  _row_spec(tm, HEAD),
                  _bcast_spec(CONV_WIDTH, c3), _bcast_spec(1, D_B), _bcast_spec(1, D_B),
                  row, row, row, row, row, row],
        out_specs=[_row_spec(tm, c3), _row_spec(tm, HEAD), _bcast_spec(CONV_WIDTH, c3), _bcast_spec(1, D_B),
                   _bcast_spec(1, D_B)],
        out_shape=[jax.ShapeDtypeStruct((t, c3), F32), jax.ShapeDtypeStruct((t, HEAD), F32),
                   jax.ShapeDtypeStruct((CONV_WIDTH, c3), F32),
                   jax.ShapeDtypeStruct((1, D_B), F32), jax.ShapeDtypeStruct((1, D_B), F32)],
        scratch_shapes=[], args=(pb, pb, ps, conv_w, a_log_bc, dt_bias_bc, g, dq, dk, dv, dg, dbeta),
        dims=("arbitrary",), exch=exch)


def _conv_bwd_input(dc, conv_w, tm=256):
    t, c3 = dc.shape
    r8 = tm // 8
    nlast = t // 8 - 1
    nsteps = t // tm

    def body(d_ref, dn_ref, w_ref, o_ref):
        not_last = pl.program_id(0) < nsteps - 1
        for cb in range(c3 // HEAD):
            cols = slice(cb * HEAD, (cb + 1) * HEAD)
            next8 = jnp.where(not_last, dn_ref[:, cols], 0.0)
            dv = d_ref[:, cols]
            wv = w_ref[:, cols]
            acc = jnp.zeros_like(dv)
            for s in range(CONV_WIDTH):
                acc = acc + wv[CONV_WIDTH - 1 - s:CONV_WIDTH - s, :] * _shift_up(dv, next8, s)
            o_ref[:, cols] = acc.astype(BF16)

    return pl.pallas_call(
        body, name="conv_bwd_input", grid=(nsteps,),
        in_specs=[_row_spec(tm, c3), pl.BlockSpec((8, c3), lambda i: (jnp.minimum((i + 1) * r8, nlast), 0)),
                  _bcast_spec(CONV_WIDTH, c3)],
        out_specs=_row_spec(tm, c3),
        out_shape=jax.ShapeDtypeStruct((t, c3), BF16), compiler_params=_cp("parallel"),
    )(dc, dc, conv_w)


def _lanes(x):
    return x[:, :CH]


def _tri_inv(a_list, r, c):
    eye = (r == c).astype(F32)
    b16 = (r >> 4) == (c >> 4)
    b32 = (r >> 5) == (c >> 5)
    ns = [jnp.where(b16, -a, 0.0) for a in a_list]
    xs = [eye + n for n in ns]
    ps = [_bdot(n, n) for n in ns]
    for last in (False, False, True):
        xs = [x + _bdot(x, p) for x, p in zip(xs, ps)]
        if not last:
            ps = [_bdot(p, p) for p in ps]
    for mask in (jnp.logical_and(b32, jnp.logical_not(b16)), jnp.logical_not(b32)):
        ts = [_bdot(x, jnp.where(mask, a, 0.0)) for x, a in zip(xs, a_list)]
        xs = [x - _bdot(t, x) for x, t in zip(xs, ts)]
    return xs


def _chunk_local(qs, ks, vs, gs, betas, solved=None):
    r = lax.broadcasted_iota(jnp.int32, (CH, CH), 0)
    c = lax.broadcasted_iota(jnp.int32, (CH, CH), 1)
    incl, strict = r >= c, r > c
    lm = incl.astype(F32)
    cums = [_hdot(lm, jnp.concatenate([g, jnp.where(strict, _lanes(g), 0.0)], axis=1)) for g in gs]
    gcbs = [cm[:, :HEAD] for cm in cums]
    decays = [jnp.where(incl, jnp.exp(jnp.where(incl, cm[:, HEAD:], 0.0)), 0.0) for cm in cums]
    bcols = [_lanes(b) for b in betas]
    kks = [_bdot(k, k, NT) for k in ks]
    qkraws = [_bdot(q, k, NT) for q, k in zip(qs, ks)]
    egs = [jnp.exp(gcb) for gcb in gcbs]
    if solved is None:
        tms = _tri_inv([jnp.where(strict, bc * kk * dc, 0.0) for bc, kk, dc in zip(bcols, kks, decays)], r, c)
        sols = [_hdot(tm, jnp.concatenate([b * v, b * eg * k], axis=1))
                for tm, b, v, eg, k in zip(tms, betas, vs, egs, ks)]
        ubars, ws = [sol[:, :HEAD] for sol in sols], [sol[:, HEAD:] for sol in sols]
    else:
        tms, ubars, ws = solved
    gls = [gcb[CH - 1:CH, :] for gcb in gcbs]
    eks = [jnp.exp(gl - gcb) for gl, gcb in zip(gls, gcbs)]
    return [dict(incl=incl, strict=strict, r=r, c=c, decay=dc, bcol=bc, kk=kk, tm=tm, eg=eg,
                 u_bar=ub, w=w, qkraw=qkraw, gl=gl, ek=ek)
            for dc, bc, kk, tm, eg, ub, w, qkraw, gl, ek
            in zip(decays, bcols, kks, tms, egs, ubars, ws, qkraws, gls, eks)]


def _dn_chunk_fwd(q, k, v, g, beta, cps=CHUNK_GROUP):
    t = q.shape[0]
    tm = cps * CH

    def body(q_ref, k_ref, v_ref, g_ref, b_ref, ub_ref, w_ref, qd_ref, kd_ref, qk_ref, ti_ref, gl_ref):
        for base in range(0, cps, CHUNK_GROUP):
            sls = [slice((base + j) * CH, (base + j + 1) * CH) for j in range(CHUNK_GROUP)]
            qs, ks = [q_ref[sl, :] for sl in sls], [k_ref[sl, :] for sl in sls]
            locs = _chunk_local(qs, ks, [v_ref[sl, :] for sl in sls], [g_ref[sl, :] for sl in sls],
                                [b_ref[sl, :] for sl in sls])
            for j, (sl, qv, kv, loc) in enumerate(zip(sls, qs, ks, locs)):
                ub_ref[sl, :] = loc["u_bar"]
                w_ref[sl, :] = loc["w"]
                qd_ref[sl, :] = qv * loc["eg"]
                kd_ref[sl, :] = kv * loc["ek"]
                qk_ref[sl, :] = loc["qkraw"] * loc["decay"]
                ti_ref[sl, :] = loc["tm"]
                gl_ref[base + j:base + j + 1, :] = jnp.exp(loc["gl"])

    hspec = pl.BlockSpec((tm, HEAD), lambda h, i: (i, h))
    sq_spec = pl.BlockSpec((None, tm, CH), lambda h, i: (h, i, 0))
    sq_shape = jax.ShapeDtypeStruct((N_HEADS_B, t, CH), F32)
    return pl.pallas_call(
        body, name="dn_chunk_fwd", grid=(N_HEADS_B, t // tm),
        in_specs=[hspec] * 5,
        out_specs=[hspec] * 4 + [sq_spec, sq_spec, pl.BlockSpec((cps, HEAD), lambda h, i: (i, h))],
        out_shape=[jax.ShapeDtypeStruct((t, D_B), F32)] * 4
        + [sq_shape, sq_shape, jax.ShapeDtypeStruct((t // CH, D_B), F32)],
        compiler_params=_cp("parallel", "parallel"),
    )(q, k, v, g, beta)


def _dn_scan_fwd(ub, w, qd, kd, qk, gl, cps=8):
    t = ub.shape[0]
    tm = cps * CH

    hg = SCAN_HEADS
    hs = list(range(hg))

    def body(ub_ref, w_ref, qd_ref, kd_ref, qk_ref, gl_ref, o_ref, st_ref, s_acc):
        @pl.when(pl.program_id(1) == 0)
        def _():
            s_acc[...] = jnp.zeros_like(s_acc)

        for ci in range(cps):
            sl = slice(ci * CH, (ci + 1) * CH)
            cols = [slice(h * HEAD, (h + 1) * HEAD) for h in hs]
            svs = [s_acc[h] for h in hs]
            for h in hs:
                st_ref[h, ci * HEAD:(ci + 1) * HEAD, :] = svs[h]
            us = [ub_ref[sl, cols[h]] - _bdot(w_ref[sl, cols[h]], svs[h]) for h in hs]
            for h in hs:
                s_acc[h] = gl_ref[ci:ci + 1, cols[h]] * svs[h] + _bdot(kd_ref[sl, cols[h]], us[h], TN)
            for h in hs:
                o_ref[sl, cols[h]] = _bdot(qd_ref[sl, cols[h]], svs[h]) + _bdot(qk_ref[h, sl, :], us[h])

    hspec = pl.BlockSpec((tm, hg * HEAD), lambda h, i: (i, h))
    return pl.pallas_call(
        body, name="dn_scan_fwd", grid=(N_HEADS_B // hg, t // tm),
        in_specs=[hspec] * 4 + [pl.BlockSpec((hg, tm, CH), lambda h, i: (h, i, 0)),
                                pl.BlockSpec((cps, hg * HEAD), lambda h, i: (i, h))],
        out_specs=[hspec, pl.BlockSpec((hg, cps * HEAD, HEAD), lambda h, i: (h, i, 0))],
        out_shape=[jax.ShapeDtypeStruct((t, D_B), F32),
                   jax.ShapeDtypeStruct((N_HEADS_B, (t // CH) * HEAD, HEAD), F32)],
        scratch_shapes=[pltpu.VMEM((hg, HEAD, HEAD), F32)],
        compiler_params=_cp("parallel", "arbitrary"),
    )(ub, w, qd, kd, qk, gl)


def _dn_scan_bwd(ub, w, qd, kd, qk, gl, st, do, cps=8):
    t = ub.shape[0]
    tm = cps * CH
    ns = t // tm

    hg = SCAN_HEADS
    hs = list(range(hg))

    def body(ub_ref, w_ref, qd_ref, kd_ref, qk_ref, gl_ref, st_ref, do_ref,
             dub_ref, dw_ref, dqd_ref, dkd_ref, dqk_ref, dgl_ref, ds_acc):
        @pl.when(pl.program_id(1) == 0)
        def _():
            ds_acc[...] = jnp.zeros_like(ds_acc)

        for ci in reversed(range(cps)):
            sl = slice(ci * CH, (ci + 1) * CH)
            cols = [slice(h * HEAD, (h + 1) * HEAD) for h in hs]
            svs = [st_ref[h, ci * HEAD:(ci + 1) * HEAD, :] for h in hs]
            wvs = [w_ref[sl, cols[h]] for h in hs]
            dovs = [do_ref[sl, cols[h]] for h in hs]
            dsvs = [ds_acc[h] for h in hs]
            us = [ub_ref[sl, cols[h]] - _bdot(wvs[h], svs[h]) for h in hs]
            dus = [_bdot(kd_ref[sl, cols[h]], dsvs[h]) + _bdot(qk_ref[h, sl, :], dovs[h], TN) for h in hs]
            for h in hs:
                ds_acc[h] = (gl_ref[ci:ci + 1, cols[h]] * dsvs[h] + _bdot(qd_ref[sl, cols[h]], dovs[h], TN)
                             - _bdot(wvs[h], dus[h], TN))
            for h in hs:
                dgl_ref[ci:ci + 1, cols[h]] = (jnp.sum(_rowsum(dsvs[h] * svs[h]), axis=0, keepdims=True)
                                              + jnp.zeros((1, HEAD), F32))
                dkd_ref[sl, cols[h]] = _bdot(us[h], dsvs[h], NT)
                dqd_ref[sl, cols[h]] = _bdot(dovs[h], svs[h], NT)
                dqk_ref[h, sl, :] = _bdot(dovs[h], us[h], NT)
                dub_ref[sl, cols[h]] = dus[h]
                dw_ref[sl, cols[h]] = -_bdot(dus[h], svs[h], NT)

    hspec = pl.BlockSpec((tm, hg * HEAD), lambda h, i: (ns - 1 - i, h))
    qkspec = pl.BlockSpec((hg, tm, CH), lambda h, i: (h, ns - 1 - i, 0))
    glspec = pl.BlockSpec((cps, hg * HEAD), lambda h, i: (ns - 1 - i, h))
    return pl.pallas_call(
        body, name="dn_scan_bwd", grid=(N_HEADS_B // hg, ns),
        in_specs=[hspec] * 4 + [qkspec, glspec,
                                pl.BlockSpec((hg, cps * HEAD, HEAD), lambda h, i: (h, ns - 1 - i, 0)), hspec],
        out_specs=[hspec] * 4 + [qkspec, glspec],
        out_shape=[jax.ShapeDtypeStruct((t, D_B), F32)] * 4
        + [jax.ShapeDtypeStruct((N_HEADS_B, t, CH), F32), jax.ShapeDtypeStruct((t // CH, D_B), F32)],
        scratch_shapes=[pltpu.VMEM((hg, HEAD, HEAD), F32)],
        compiler_params=_cp("parallel", "arbitrary"),
    )(ub, w, qd, kd, qk, gl, st, do)


def _dn_chunk_bwd(q, k, v, g, beta, ub, w, tinv, dub, dw, dqd, dkd, dqk, dgl, cps=CHUNK_GROUP):
    t = q.shape[0]
    tm = cps * CH

    def body(q_ref, k_ref, v_ref, g_ref, b_ref, ub_ref, w_ref, dub_ref, dw_ref, dqd_ref, dkd_ref, ti_ref, dqk_ref,
             dgl_ref, dq_ref, dk_ref, dv_ref, dg_ref, db_ref):
        ones = jnp.ones((CH, HEAD), F32)
        rid = lax.broadcasted_iota(jnp.int32, (CH, HEAD), 0)

        def rest(ci, sl, qv, kv, vv, beta_v, loc, dr, da):
            incl = loc["incl"]
            eg, ek, decay, bcol, kk = loc["eg"], loc["ek"], loc["decay"], loc["bcol"], loc["kk"]
            drv, drk = dr[:, :HEAD], dr[:, HEAD:]
            dv_ref[sl, :] = beta_v * drv
            beg = beta_v * eg
            t1 = drk * kv
            dbeta = _rowsum(drv * vv + t1 * eg) + _rowsum(da * kk * decay)
            dkk = da * bcol * decay
            dqk_m = jnp.where(incl, dqk_ref[sl, :], 0.0)
            ddecay = da * bcol * kk + dqk_m * loc["qkraw"]
            dqkraw = dqk_m * decay
            dqdv, dkdv = dqd_ref[sl, :], dkd_ref[sl, :]
            dq_ref[sl, :] = _bdot(dqkraw, kv) + dqdv * eg
            dk_ref[sl, :] = (beg * drk + _bdot(dqkraw, qv, TN) + _bdot(dkk, kv) + _bdot(dkk, kv, TN)
                             + dkdv * ek)
            e = ddecay * decay
            skd = _rowsum(dkdv * kv * ek)
            dgc = _rowsum(beg * t1) + _rowsum(e) + _rowsum(dqdv * qv * eg) - skd
            colsum = _hdot(e, ones, TN)
            last = jnp.sum(skd, axis=0, keepdims=True) + dgl_ref[ci:ci + 1, :] * jnp.exp(loc["gl"])
            db_ref[sl, :] = dbeta + jnp.zeros((CH, HEAD), F32)
            return (dgc - colsum) + jnp.where(rid == CH - 1, last, 0.0)

        for base in range(0, cps, CHUNK_GROUP):
            cis = list(range(base, base + CHUNK_GROUP))
            sls = [slice(ci * CH, (ci + 1) * CH) for ci in cis]
            qs, ks, vs = [q_ref[sl, :] for sl in sls], [k_ref[sl, :] for sl in sls], [v_ref[sl, :] for sl in sls]
            betas = [b_ref[sl, :] for sl in sls]
            locs = _chunk_local(qs, ks, vs, [g_ref[sl, :] for sl in sls], betas,
                                solved=([ti_ref[sl, :] for sl in sls], [ub_ref[sl, :] for sl in sls],
                                        [w_ref[sl, :] for sl in sls]))
            drs = [_hdot(loc["tm"], jnp.concatenate([dub_ref[sl, :], dw_ref[sl, :]], axis=1), TN)
                   for loc, sl in zip(locs, sls)]
            das = [jnp.where(loc["strict"],
                             -_hdot(dr, jnp.concatenate([loc["u_bar"], loc["w"]], axis=1), NT), 0.0)
                   for loc, dr in zip(locs, drs)]
            dgcs = [rest(*args) for args in zip(cis, sls, qs, ks, vs, betas, locs, drs, das)]
            um = (locs[0]["r"] <= locs[0]["c"]).astype(F32)
            for sl, dgc_bc in zip(sls, dgcs):
                dg_ref[sl, :] = _hdot(um, dgc_bc)

    hspec = pl.BlockSpec((tm, HEAD), lambda h, i: (i, h))
    sq_spec = pl.BlockSpec((None, tm, CH), lambda h, i: (h, i, 0))
    return pl.pallas_call(
        body, name="dn_chunk_bwd", grid=(N_HEADS_B, t // tm),
        in_specs=[hspec] * 11 + [sq_spec, sq_spec, pl.BlockSpec((cps, HEAD), lambda h, i: (i, h))],
        out_specs=[hspec] * 5,
        out_shape=[jax.ShapeDtypeStruct((t, D_B), F32)] * 5,
        compiler_params=_cp("parallel", "parallel"),
    )(q, k, v, g, beta, ub, w, dub, dw, dqd, dkd, tinv, dqk, dgl)


FLIPS = [(fx, fy, fc) for fx in (0, 1) for fy in (0, 1) for fc in (0, 1)][1:]


def _mesh_pos():
    return lax.axis_index("x"), lax.axis_index("y"), lax.axis_index("c")


def _peer(pos, flip):
    return tuple((1 - p) if f else p for p, f in zip(pos, flip))


def _dev_index(pos):
    return 4 * pos[0] + 2 * pos[1] + pos[2]


class _Exchange:
    def __init__(self, tensors, scatter):
        self.tensors, self.scatter, self.nt = list(tensors), list(scatter), len(tensors)
        hbm = pl.BlockSpec(memory_space=pltpu.HBM)
        self.in_specs = [hbm] * self.nt
        self.out_specs = [hbm] * self.nt
        self.out_shape = [jax.ShapeDtypeStruct(x.shape if sc else (N_DEV,) + x.shape, x.dtype)
                          for x, sc in zip(tensors, scatter)]
        self.scratch_shapes = [pltpu.SemaphoreType.DMA((self.nt * 7,)), pltpu.SemaphoreType.DMA((self.nt * 7,)),
                               pltpu.SemaphoreType.DMA((self.nt,))]

    def _copies(self, ins, outs, sems):
        send_sems, recv_sems, local_sems = sems
        pos = _mesh_pos()
        me = _dev_index(pos)

        def remote(ti, fi, landing):
            peer = _peer(pos, FLIPS[fi])
            src = ins[ti].at[_dev_index(peer)] if self.scatter[ti] else ins[ti]
            return pltpu.make_async_remote_copy(
                src_ref=src, dst_ref=outs[ti].at[landing(peer)],
                send_sem=send_sems.at[ti * 7 + fi], recv_sem=recv_sems.at[ti * 7 + fi],
                device_id=peer, device_id_type=pl.DeviceIdType.MESH)

        pairs = [(ti, fi) for ti in range(self.nt) for fi in range(7)]
        local = [pltpu.make_async_copy(ins[ti].at[me] if self.scatter[ti] else ins[ti], outs[ti].at[me],
                                       local_sems.at[ti]) for ti in range(self.nt)]
        sends = [remote(ti, fi, lambda peer: me) for ti, fi in pairs]
        recvs = [remote(ti, fi, _dev_index) for ti, fi in pairs]
        return local, sends, recvs

    def start(self, ins, outs, sems):
        local, sends, _ = self._copies(ins, outs, sems)
        for cp in local + sends:
            cp.start()

    def wait(self, ins, outs, sems):
        local, sends, recvs = self._copies(ins, outs, sems)
        for cp in recvs:
            cp.wait_recv()
        for cp in sends:
            cp.wait_send()
        for cp in local:
            cp.wait()


def _exchange(tensors, scatter, name):
    ex = _Exchange(tensors, scatter)

    def body(*refs):
        ins, outs, sems = refs[:ex.nt], refs[ex.nt:2 * ex.nt], refs[2 * ex.nt:]
        ex.start(ins, outs, sems)
        ex.wait(ins, outs, sems)

    return pl.pallas_call(
        body, name=name, in_specs=ex.in_specs, out_specs=ex.out_specs, out_shape=ex.out_shape,
        scratch_shapes=ex.scratch_shapes, compiler_params=pltpu.CompilerParams(has_side_effects=True),
    )(*tensors)


def _gather_two_level(tensors, name):
    nt = len(tensors)
    hbm = pl.BlockSpec(memory_space=pltpu.HBM)

    def body(*refs):
        ins, outs = refs[:nt], refs[nt:2 * nt]
        send_sems, recv_sems, local_sems = refs[2 * nt:]
        x, y, c = _mesh_pos()
        sibling = (x, y, 1 - c)
        chips = [(1 - x, y), (x, 1 - y), (1 - x, 1 - y)]

        def copy(ti, k, block, to, own=False):
            slot = outs[ti].at[_dev_index(block)]
            return pltpu.make_async_remote_copy(
                src_ref=ins[ti] if own else slot, dst_ref=slot,
                send_sem=send_sems.at[ti * 7 + k], recv_sem=recv_sems.at[ti * 7 + k],
                device_id=to, device_id_type=pl.DeviceIdType.MESH)

        me = (x, y, c)
        mine = [pltpu.make_async_copy(ins[ti], outs[ti].at[_dev_index(me)], local_sems.at[ti]) for ti in range(nt)]
        first = [copy(ti, 0, me, sibling, own=True) for ti in range(nt)]
        first += [copy(ti, 1 + j, me, (*chip, c), own=True) for ti in range(nt) for j, chip in enumerate(chips)]
        for cp in mine + first:
            cp.start()
        passed = []
        for j, chip in enumerate(chips):
            for ti in range(nt):
                copy(ti, 1 + j, (*chip, c), me).wait_recv()
                cp = copy(ti, 4 + j, (*chip, c), sibling)
                cp.start()
                passed.append(cp)
        for ti in range(nt):
            copy(ti, 0, sibling, me).wait_recv()
            for j, chip in enumerate(chips):
                copy(ti, 4 + j, (*chip, 1 - c), me).wait_recv()
        for cp in first + passed:
            cp.wait_send()
        for cp in mine:
            cp.wait()

    return pl.pallas_call(
        body, name=name, in_specs=[hbm] * nt, out_specs=[hbm] * nt,
        out_shape=[jax.ShapeDtypeStruct((N_DEV,) + x.shape, x.dtype) for x in tensors],
        scratch_shapes=[pltpu.SemaphoreType.DMA((nt * 7,)), pltpu.SemaphoreType.DMA((nt * 7,)),
                        pltpu.SemaphoreType.DMA((nt,))],
        compiler_params=pltpu.CompilerParams(has_side_effects=True),
    )(*tensors)


def _hosted_call(body, *, name, grid, in_specs, out_specs, out_shape, scratch_shapes, args, dims, exch=None):
    if exch is None:
        return pl.pallas_call(body, name=name, grid=grid, in_specs=in_specs, out_specs=out_specs,
                              out_shape=out_shape, scratch_shapes=scratch_shapes,
                              compiler_params=_cp(*dims))(*args)
    n_in, n_out, n_sc, ne = len(in_specs), len(out_specs), len(scratch_shapes), exch.nt
    nsteps = math.prod(grid)

    def wrapped(*refs):
        ins, ex_in = refs[:n_in], refs[n_in:n_in + ne]
        outs = refs[n_in + ne:n_in + ne + n_out]
        ex_out = refs[n_in + ne + n_out:n_in + 2 * ne + n_out]
        rest = refs[n_in + 2 * ne + n_out:]
        scratch, sems = rest[:n_sc], rest[n_sc:]
        step = pl.program_id(0)
        for ax in range(1, len(grid)):
            step = step * grid[ax] + pl.program_id(ax)

        @pl.when(step == 0)
        def _():
            exch.start(ex_in, ex_out, sems)

        body(*ins, *outs, *scratch)

        @pl.when(step == nsteps - 1)
        def _():
            exch.wait(ex_in, ex_out, sems)

    return pl.pallas_call(
        wrapped, name=name, grid=grid, in_specs=list(in_specs) + exch.in_specs,
        out_specs=list(out_specs) + exch.out_specs, out_shape=list(out_shape) + exch.out_shape,
        scratch_shapes=list(scratch_shapes) + exch.scratch_shapes,
        compiler_params=pltpu.CompilerParams(dimension_semantics=("arbitrary",) * len(grid),
                                             vmem_limit_bytes=VMEM_LIMIT, has_side_effects=True),
    )(*args, *exch.tensors)


def _adamw(land, w, m, v, name, tm=256):
    n, r, c = land.shape
    tm = r if r <= tm else max(s for s in range(8, tm + 1, 8) if r % s == 0)
    bc1 = 1.0 / (1.0 - ADAM_B1 ** ADAM_STEP)
    bc2 = 1.0 / (1.0 - ADAM_B2 ** ADAM_STEP)

    def body(l_ref, w_ref, m_ref, v_ref, g_ref, d_ref, nm_ref, nv_ref):
        g = l_ref[0].astype(F32)
        for i in range(1, n):
            g = g + l_ref[i].astype(F32)
        nm = ADAM_B1 * m_ref[...] + (1.0 - ADAM_B1) * g
        nv = ADAM_B2 * v_ref[...] + (1.0 - ADAM_B2) * (g * g)
        g_ref[...] = g
        nm_ref[...] = nm
        nv_ref[...] = nv
        d_ref[...] = -ADAM_LR * ((nm * bc1) / (jnp.sqrt(nv * bc2) + ADAM_EPS) + ADAM_WD * w_ref[...])

    spec = pl.BlockSpec((tm, c), lambda i: (i, 0))
    return pl.pallas_call(
        body, name=name, grid=(r // tm,),
        in_specs=[pl.BlockSpec((n, tm, c), lambda i: (0, i, 0)), spec, spec, spec],
        out_specs=[spec] * 4, out_shape=[jax.ShapeDtypeStruct((r, c), F32)] * 4,
        compiler_params=_cp("parallel"),
    )(land, w, m, v)


PACK_W = 2048


def _pack_rows(parts):
    flat = jnp.concatenate([p.reshape(-1).astype(F32) for p in parts])
    pad = (-flat.shape[0]) % (8 * PACK_W)
    return jnp.pad(flat, (0, pad)).reshape(-1, PACK_W)


def _unpack_rows(packed, shapes):
    flat = packed.reshape(-1)
    out, off = [], 0
    for s in shapes:
        n = math.prod(s)
        out.append(flat[off:off + n].reshape(s))
        off += n
    return out


def _col_slabs(gfull, width):
    r = gfull.shape[0]
    return jnp.transpose(gfull.reshape(r, N_DEV, width), (1, 0, 2)).astype(BF16)


def _row_slabs(gfull):
    return gfull.reshape(N_DEV, gfull.shape[0] // N_DEV, gfull.shape[1]).astype(BF16)


def _from_col_slabs(gathered):
    n, r, width = gathered.shape
    return jnp.transpose(gathered, (1, 0, 2)).reshape(r, n * width)


def _local_step(xs, target, norm_mix, wf_in, cw, a_log, dt_bias, dn_norm, rest, norm_ffn, norm_final,
                distributed=True):
    d = D_MODEL
    n_main = D_PA + 4 * D_B
    w_pa_cols = wf_in[:, :D_PA]
    w_pb_cols = jnp.concatenate([wf_in[:, D_PA:n_main], wf_in[:, n_main + 2 * N_HEADS_B:]], axis=1)
    w_small = jnp.pad(wf_in[:, n_main:n_main + 2 * N_HEADS_B], ((0, 0), (0, HEAD - 2 * N_HEADS_B)))
    a_log_bc = jnp.repeat(a_log, HEAD, axis=1)
    dt_bias_bc = jnp.repeat(dt_bias, HEAD, axis=1)

    u = _rms_fwd(xs, norm_mix)
    pa = _matmul(u, w_pa_cols, "nn", F32, 1024, 1536, d, name="proj_a")
    pb = _matmul(u, w_pb_cols, "nn", F32, 1024, 1024, d, name="proj_b")
    ps = _matmul(u, w_small, "nn", F32, 2048, HEAD, d, name="proj_small")
    os_, ls_ = [], []
    for gi, dil in enumerate(DILATIONS):
        o_g, l_g = _attn_fwd_group(pa, gi, dil)
        os_.append(o_g)
        ls_.append(l_g)
    y_att, lse = _attn_merge(os_, ls_)
    prep = _dn_prep_fwd(pb, ps, cw, a_log_bc, dt_bias_bc,
                        exch=_Exchange(rest, [False] * 6) if distributed else None)
    qn, kn, vn, gdec, beta = prep[:5]
    if distributed:
        g_pa, g_pd, g_out, g_gate, g_up, g_down = prep[5:]
        wf_pa, wf_pd, wf_out = _from_col_slabs(g_pa), g_pd.reshape(D_B, d), g_out.reshape(d, d)
        wf_gate, wf_up, wf_down = _from_col_slabs(g_gate), _from_col_slabs(g_up), g_down.reshape(D_FF, d)
    else:
        wf_pa, wf_pd, wf_out, wf_gate, wf_up, wf_down = rest
    wf_gu = jnp.concatenate([wf_gate, wf_up], axis=1)
    ub, ww, qd, kd, qk, tinv, gl = _dn_chunk_fwd(qn, kn, vn, gdec, beta)
    o_dn, states = _dn_scan_fwd(ub, ww, qd, kd, qk, gl)
    o_gated = _head_norm_fwd(o_dn, pb, dn_norm)
    y_a = _matmul(y_att, wf_pa, "nn", F32, 1024, d, D_ATTN_OUT, name="proj_attn")
    y_b = _matmul(o_gated, wf_pd, "nn", F32, 1024, d, d, name="proj_delta")
    merged = _gate_merge_fwd(pb, y_a, y_b)
    h1 = _matmul(merged, wf_out, "nn", F32, 1024, d, d, add=xs, name="out_proj")
    hn = _rms_fwd(h1, norm_ffn)
    gate, up, act = _ffn_in(hn, wf_gate, wf_up)
    h2 = _matmul(act, wf_down, "nn", F32, 512, d, D_FF, add=h1, name="ffn_out")
    loss_part, dh2, d_norm_final = _final_loss(h2, norm_final.reshape(1, d), target)

    dgate, dup = _ffn_act_bwd(dh2, wf_down, gate, up)
    gw_down = _matmul(act, dh2, "tn", F32, 1408, d, 512, name="gw_down")
    dhn = _matmul_nt_segments([dgate, dup], wf_gu, BF16, 1024, 1408, "d_hn")[0]
    gw_gate = _matmul(hn, dgate, "tn", F32, d, 1408, 512, name="gw_gate")
    gw_up = _matmul(hn, dup, "tn", F32, d, 1408, 512, name="gw_up")
    dh1, d_norm_ffn = _rms_bwd(h1, norm_ffn, dhn, dh2)
    dmerged = _matmul(dh1, wf_out, "nt", BF16, 1024, d, d, name="d_merged")
    gw_out = _matmul(merged, dh1, "tn", F32, d, d, 512, name="gw_out")
    dya, dyb, dga, dgb = _gate_merge_bwd(pb, y_a, y_b, dmerged)
    dy_att = _matmul(dya, wf_pa, "nt", F32, 1024, D_ATTN_OUT, d, name="d_y_att")
    gw_pa = _matmul(y_att, dya, "tn", F32, D_ATTN_OUT, d, 512, name="gw_pa")
    do_gated = _matmul(dyb, wf_pd, "nt", BF16, 1024, d, d, name="d_o_gated")
    gw_pd = _matmul(o_gated, dyb, "tn", F32, d, d, 512, name="gw_pd")
    do_dn, dz, d_dn_norm = _head_norm_bwd(o_dn, pb, dn_norm, do_gated)
    dub, dww, dqd, dkd, dqk, dgl = _dn_scan_bwd(ub, ww, qd, kd, qk, gl, states, do_dn)
    dqn, dkn, dvn, dgdec, dbeta = _dn_chunk_bwd(qn, kn, vn, gdec, beta, ub, ww, tinv, dub, dww, dqd, dkd, dqk, dgl)
    slabs = [_col_slabs(gw_pa, d // N_DEV), _row_slabs(gw_pd), _row_slabs(gw_out),
             _col_slabs(gw_gate, D_FF // N_DEV), _col_slabs(gw_up, D_FF // N_DEV),
             _row_slabs(gw_down)] if distributed else None
    prep = _dn_prep_bwd(pb, ps, cw, a_log_bc, dt_bias_bc, gdec, dqn, dkn, dvn, dgdec, dbeta,
                        exch=_Exchange(slabs, [True] * 6) if distributed else None)
    dc, dps, d_conv_full, d_alog_bc, d_dt_bc = prep[:5]
    dqkv_pre = _conv_bwd_input(dc, cw)
    segs = [_attn_bwd_group(pa, dy_att, y_att, lse, gi, dil) for gi, dil in enumerate(DILATIONS)]
    segs += [dqkv_pre, dz, dga, dgb]
    gws = [_matmul(u, s, "tn", F32, d, 1536, 512, name=f"gw_in_{i}") for i, s in enumerate(segs)]
    gw_small = _matmul(u, dps, "tn", F32, d, HEAD, 512, name="gw_in_small")
    g_att = jnp.concatenate(gws[:3], axis=1).reshape(d, N_HEADS_A, 3, HEAD)
    gw_in = jnp.concatenate(
        [g_att[:, :, i, :].reshape(d, D_ATTN) for i in range(3)]
        + [gws[3], gws[4], gw_small[:, :2 * N_HEADS_B], gws[5], gws[6]], axis=1)
    w_att = jnp.stack([w_pa_cols[:, i * D_ATTN:(i + 1) * D_ATTN].reshape(d, N_HEADS_A, HEAD) for i in range(3)],
                      axis=2).reshape(d, D_PA)
    du_first = _matmul_nt_segments(
        segs[:4], jnp.concatenate([w_att, w_pb_cols[:, :3 * D_B]], axis=1), F32, 1024, 1536, "d_u_0",
        exch=_Exchange([_col_slabs(gw_in, SHARD_IN)], [True]) if distributed else None)
    du_small = _matmul(dps, w_small, "nt", F32, 1024, d, HEAD, add=du_first[0], name="d_u_small")
    du = _matmul_nt_segments(segs[4:], w_pb_cols[:, 3 * D_B:], BF16, 1024, 1024, "d_u_1", add=du_small)[0]
    dx, d_norm_mix = _rms_bwd(xs, norm_mix, du, dh1)
    d_a_log = d_alog_bc.reshape(1, N_HEADS_B, HEAD)[:, :, 0]
    d_dt_bias = d_dt_bc.reshape(1, N_HEADS_B, HEAD)[:, :, 0]
    small = (d_conv_full, d_norm_mix, d_norm_ffn, d_norm_final, d_dn_norm, d_a_log, d_dt_bias)
    if distributed:
        return (loss_part, dx, [du_first[1]] + list(prep[5:])) + small
    return (loss_part, dx, gw_in, gw_pa, gw_pd, gw_out, jnp.concatenate([gw_gate, gw_up], axis=1), gw_down) + small


def kernel(x, norm_mix, w_in, conv_w, a_log, dt_bias, dn_norm, w_proj_attn, w_proj_delta, w_out, norm_ffn, w_gate, w_up, w_down, norm_final, loss_target, m_norm_mix, m_w_in, m_conv_w, m_a_log, m_dt_bias, m_dn_norm, m_w_proj_attn, m_w_proj_delta, m_w_out, m_norm_ffn, m_w_gate, m_w_up, m_w_down, m_norm_final, v_norm_mix, v_w_in, v_conv_w, v_a_log, v_dt_bias, v_dn_norm, v_w_proj_attn, v_w_proj_delta, v_w_out, v_norm_ffn, v_w_gate, v_w_up, v_w_down, v_norm_final):
    d = D_MODEL
    xs = x[0]
    target = loss_target[0]
    me = _dev_index(_mesh_pos())

    g_in, g_conv = _gather_two_level([w_in[0].astype(BF16), conv_w[0]], "gather_w_in")
    rest = [w[0].astype(BF16) for w in (w_proj_attn, w_proj_delta, w_out, w_gate, w_up, w_down)]
    (loss_part, dx, landed, d_conv_full, d_norm_mix, d_norm_ffn, d_norm_final, d_dn_norm, d_a_log,
     d_dt_bias) = _local_step(xs, target, norm_mix, _from_col_slabs(g_in), _from_col_slabs(g_conv), a_log, dt_bias,
                              dn_norm, rest, norm_ffn, norm_final)

    small_shapes = [(1, d), (1, d), (d,), (1, HEAD), (1, N_HEADS_B), (1, N_HEADS_B), (1, 1), (CONV_WIDTH, 3 * D_B)]
    packed = _pack_rows([d_norm_mix, d_norm_ffn, d_norm_final, d_dn_norm, d_a_log, d_dt_bias,
                         loss_part[:, :1], d_conv_full])
    landed = list(landed) + list(_exchange([packed], [False], "gather_small_grads"))
    zero1 = jnp.zeros((1, 1), F32)
    zconv = jnp.zeros((CONV_WIDTH, 3 * D_B), F32)
    small_w = _pack_rows([norm_mix, norm_ffn, norm_final, dn_norm, a_log, dt_bias, zero1, zconv])
    small_m = _pack_rows([m_norm_mix, m_norm_ffn, m_norm_final, m_dn_norm, m_a_log, m_dt_bias, zero1, zconv])
    small_v = _pack_rows([v_norm_mix, v_norm_ffn, v_norm_final, v_dn_norm, v_a_log, v_dt_bias, zero1, zconv])
    small = [_unpack_rows(z, small_shapes) for z in _adamw(landed[7], small_w, small_m, small_v, "adamw_small")]
    loss = small[0][6].reshape(())
    conv_shard = 3 * D_B // N_DEV
    g_conv_own = lax.dynamic_slice_in_dim(small[0][7], me * conv_shard, conv_shard, axis=1)
    r_conv = _adamw(g_conv_own[None], conv_w[0], m_conv_w[0], v_conv_w[0], "adamw_conv")
    big = [_adamw(landed[i], w[0], m[0], v[0], f"adamw_{i}") for i, (w, m, v) in enumerate([
        (w_in, m_w_in, v_w_in), (w_proj_attn, m_w_proj_attn, v_w_proj_attn),
        (w_proj_delta, m_w_proj_delta, v_w_proj_delta), (w_out, m_w_out, v_w_out),
        (w_gate, m_w_gate, v_w_gate), (w_up, m_w_up, v_w_up), (w_down, m_w_down, v_w_down)])]

    def leaves(k):
        sm = small[k]
        return [sm[0], big[0][k][None], r_conv[k][None], sm[4], sm[5], sm[3], big[1][k][None], big[2][k][None],
                big[3][k][None], sm[1], big[4][k][None], big[5][k][None], big[6][k][None], sm[2]]

    return (loss, dx[None], *leaves(0), *leaves(1), *leaves(2), *leaves(3))
```

```python
import math

import jax
import jax.numpy as jnp
from jax import lax
from jax.experimental import pallas as pl
from jax.experimental.pallas import tpu as pltpu

F32 = jnp.float32
BF16 = jnp.bfloat16
HI = lax.Precision.HIGH

D_MODEL = 1024
N_DEV = 8
HEAD = 128
N_HEADS_A = 12
HEADS_PER_GROUP = 4
DILATIONS = (1, 4, 16)
BLOCK_A = 128
D_ATTN = N_HEADS_A * HEAD
D_ATTN_OUT = HEADS_PER_GROUP * HEAD
N_HEADS_B = 8
D_B = N_HEADS_B * HEAD
CONV_WIDTH = 4
CH = 64
CHUNK_GROUP = 16
SCAN_HEADS = 4
TOKEN_TK = 1024
D_FF = 2816
EPS = 1e-6
D_IN = 3 * D_ATTN + 4 * D_B + 2 * N_HEADS_B + 2 * D_MODEL
SHARD_IN = D_IN // N_DEV
PB_Z, PB_GATE = 3072, 4096
D_PA = 3 * D_ATTN
ADAM_LR, ADAM_B1, ADAM_B2, ADAM_EPS, ADAM_WD, ADAM_STEP = 0.001, 0.9, 0.999, 1e-08, 0.01, 10
VMEM_LIMIT = 56 * 1024 * 1024

NN = ((1,), (0,))
NT = ((1,), (1,))
TN = ((0,), (0,))


def _dot(a, b, dims=NN, prec=None):
    return lax.dot_general(a, b, (dims, ((), ())), precision=prec, preferred_element_type=F32)


def _bdot(a, b, dims=NN):
    return _dot(a.astype(BF16), b.astype(BF16), dims)


def _hdot(a, b, dims=NN):
    return _dot(a.astype(F32), b.astype(F32), dims, HI)


def _cp(*sem):
    return pltpu.CompilerParams(dimension_semantics=sem, vmem_limit_bytes=VMEM_LIMIT)


def _sigmoid(x):
    return 0.5 * jnp.tanh(0.5 * x) + 0.5


def _softplus(x):
    return jnp.maximum(x, 0.0) + jnp.log(1.0 + jnp.exp(-jnp.abs(x)))


def _rowsum(x):
    return jnp.sum(x, axis=-1, keepdims=True)


def _matmul(a, b, mode, out_dtype, tm, tn, tk, add=None, name="mm"):
    if mode == "nn":
        (m, k), (k2, n) = a.shape, b.shape
    elif mode == "nt":
        (m, k), (n, k2) = a.shape, b.shape
    else:
        (k, m), (k2, n) = a.shape, b.shape
    assert k == k2, (a.shape, b.shape, mode)
    tm, tn, tk = min(tm, m), min(tn, n), min(tk, k)
    assert m % tm == 0 and n % tn == 0 and k % tk == 0, (a.shape, b.shape, tm, tn, tk)
    nk = k // tk
    dims = {"nn": NN, "nt": NT, "tn": TN}[mode]

    def body(*refs):
        if add is None:
            a_ref, b_ref, o_ref, acc = refs
            add_ref = None
        else:
            a_ref, b_ref, add_ref, o_ref, acc = refs
        kk = pl.program_id(2)

        @pl.when(kk == 0)
        def _():
            acc[...] = jnp.zeros_like(acc)

        acc[...] += _bdot(a_ref[...], b_ref[...], dims)

        @pl.when(kk == nk - 1)
        def _():
            r = acc[...]
            if add_ref is not None:
                r = r + add_ref[...].astype(F32)
            o_ref[...] = r.astype(out_dtype)

    a_spec = (pl.BlockSpec((tk, tm), lambda i, j, kk: (kk, i)) if mode == "tn"
              else pl.BlockSpec((tm, tk), lambda i, j, kk: (i, kk)))
    b_spec = (pl.BlockSpec((tn, tk), lambda i, j, kk: (j, kk)) if mode == "nt"
              else pl.BlockSpec((tk, tn), lambda i, j, kk: (kk, j)))
    in_specs = [a_spec, b_spec]
    args = [a, b]
    if add is not None:
        in_specs.append(pl.BlockSpec((tm, tn), lambda i, j, kk: (i, j)))
        args.append(add)
    return pl.pallas_call(
        body, name=name, grid=(m // tm, n // tn, nk),
        in_specs=in_specs, out_specs=pl.BlockSpec((tm, tn), lambda i, j, kk: (i, j)),
        out_shape=jax.ShapeDtypeStruct((m, n), out_dtype),
        scratch_shapes=[pltpu.VMEM((tm, tn), F32)],
        compiler_params=_cp("parallel", "parallel", "arbitrary"),
    )(*args)


def _matmul_nt_segments(segs, b, out_dtype, tm, tk, name, add=None, exch=None):
    m = segs[0].shape[0]
    n, ktot = b.shape
    ns = len(segs)
    nks = [s.shape[1] // tk for s in segs]
    assert all(s.shape[1] % tk == 0 for s in segs) and sum(s.shape[1] for s in segs) == ktot and m % tm == 0
    starts = [sum(nks[:i]) for i in range(ns)]
    nk = sum(nks)

    def body(*refs):
        seg_refs, b_ref = refs[:ns], refs[ns]
        add_ref = refs[ns + 1] if add is not None else None
        o_ref, acc = refs[-2], refs[-1]
        kk = pl.program_id(1)

        @pl.when(kk == 0)
        def _():
            acc[...] = jnp.zeros_like(acc)

        for a_ref, k0, nk_s in zip(seg_refs, starts, nks):
            @pl.when(jnp.logical_and(kk >= k0, kk < k0 + nk_s))
            def _(a_ref=a_ref):
                acc[...] += _bdot(a_ref[...], b_ref[...], NT)

        @pl.when(kk == nk - 1)
        def _():
            r = acc[...]
            if add_ref is not None:
                r = r + add_ref[...].astype(F32)
            o_ref[...] = r.astype(out_dtype)

    def seg_spec(k0, nk_s):
        return pl.BlockSpec((tm, tk), lambda i, kk: (i, jnp.clip(kk - k0, 0, nk_s - 1)))

    row = pl.BlockSpec((tm, n), lambda i, kk: (i, 0))
    in_specs = [seg_spec(k0, nk_s) for k0, nk_s in zip(starts, nks)] + [pl.BlockSpec((n, tk), lambda i, kk: (0, kk))]
    args = list(segs) + [b]
    if add is not None:
        in_specs.append(row)
        args.append(add)
    return _hosted_call(body, name=name, grid=(m // tm, nk), in_specs=in_specs, out_specs=[row],
                        out_shape=[jax.ShapeDtypeStruct((m, n), out_dtype)],
                        scratch_shapes=[pltpu.VMEM((tm, n), F32)], args=args,
                        dims=("parallel", "arbitrary"), exch=exch)


def _row_spec(tm, cols, cb=0):
    return pl.BlockSpec((tm, cols), lambda i, cb=cb: (i, cb))


def _bcast_spec(rows, cols):
    return pl.BlockSpec((rows, cols), lambda i: (0, 0))


def _rms_fwd(x, w, tm=512):
    t, d = x.shape

    def body(x_ref, w_ref, o_ref):
        xv = x_ref[...]
        r = lax.rsqrt(jnp.mean(xv * xv, axis=-1, keepdims=True) + EPS)
        o_ref[...] = (xv * r * w_ref[...]).astype(BF16)

    return pl.pallas_call(
        body, name="rms_fwd", grid=(t // tm,),
        in_specs=[_row_spec(tm, d), _bcast_spec(1, d)], out_specs=_row_spec(tm, d),
        out_shape=jax.ShapeDtypeStruct((t, d), BF16), compiler_params=_cp("parallel"),
    )(x, w)


def _rms_bwd(x, w, dy, resid, tm=512):
    t, d = x.shape

    def body(x_ref, w_ref, dy_ref, res_ref, dx_ref, dw_ref):
        xv = x_ref[...]
        r = lax.rsqrt(jnp.mean(xv * xv, axis=-1, keepdims=True) + EPS)
        xh = xv * r
        dyv = dy_ref[...].astype(F32)
        dxh = dyv * w_ref[...]
        dx_ref[...] = res_ref[...] + r * (dxh - xh * jnp.mean(dxh * xh, axis=-1, keepdims=True))

        @pl.when(pl.program_id(0) == 0)
        def _():
            dw_ref[...] = jnp.zeros_like(dw_ref)

        dw_ref[...] += jnp.sum(dyv * xh, axis=0, keepdims=True)

    return pl.pallas_call(
        body, name="rms_bwd", grid=(t // tm,),
        in_specs=[_row_spec(tm, d), _bcast_spec(1, d), _row_spec(tm, d), _row_spec(tm, d)],
        out_specs=[_row_spec(tm, d), _bcast_spec(1, d)],
        out_shape=[jax.ShapeDtypeStruct((t, d), F32), jax.ShapeDtypeStruct((1, d), F32)],
        compiler_params=_cp("arbitrary"),
    )(x, w, dy, resid)


def _final_loss(h, w, target, tm=512):
    t, d = h.shape

    def body(h_ref, w_ref, t_ref, loss_ref, dh_ref, dw_ref):
        hv = h_ref[...]
        r = lax.rsqrt(jnp.mean(hv * hv, axis=-1, keepdims=True) + EPS)
        xh = hv * r
        wv = w_ref[...]
        err = xh * wv - t_ref[...]
        dy = err * (1.0 / d)
        dxh = dy * wv
        dh_ref[...] = r * (dxh - xh * jnp.mean(dxh * xh, axis=-1, keepdims=True))

        @pl.when(pl.program_id(0) == 0)
        def _():
            dw_ref[...] = jnp.zeros_like(dw_ref)
            loss_ref[...] = jnp.zeros_like(loss_ref)

        dw_ref[...] += jnp.sum(dy * xh, axis=0, keepdims=True)
        part = 0.5 * jnp.sum(jnp.mean(err * err, axis=-1, keepdims=True), axis=0, keepdims=True)
        loss_ref[...] += part + jnp.zeros((1, HEAD), F32)

    return pl.pallas_call(
        body, name="final_loss", grid=(t // tm,),
        in_specs=[_row_spec(tm, d), _bcast_spec(1, d), _row_spec(tm, d)],
        out_specs=[_bcast_spec(1, HEAD), _row_spec(tm, d), _bcast_spec(1, d)],
        out_shape=[jax.ShapeDtypeStruct((1, HEAD), F32), jax.ShapeDtypeStruct((t, d), F32),
                   jax.ShapeDtypeStruct((1, d), F32)],
        compiler_params=_cp("arbitrary"),
    )(h, w, target)


def _ffn_in(hn, w_gate, w_up, tm=1024, tn=1408):
    t, d = hn.shape
    ff = w_gate.shape[1]

    def body(a_ref, wg_ref, wu_ref, g_ref, u_ref, act_ref):
        a = a_ref[...]
        g = _bdot(a, wg_ref[...])
        u = _bdot(a, wu_ref[...])
        g_ref[...] = g.astype(BF16)
        u_ref[...] = u.astype(BF16)
        gq = g.astype(BF16).astype(F32)
        act_ref[...] = (gq * _sigmoid(gq) * u.astype(BF16).astype(F32)).astype(BF16)

    tile = pl.BlockSpec((tm, tn), lambda i, j: (i, j))
    wspec = pl.BlockSpec((d, tn), lambda i, j: (0, j))
    return pl.pallas_call(
        body, name="ffn_in", grid=(t // tm, ff // tn),
        in_specs=[pl.BlockSpec((tm, d), lambda i, j: (i, 0)), wspec, wspec], out_specs=[tile] * 3,
        out_shape=[jax.ShapeDtypeStruct((t, ff), BF16)] * 3, compiler_params=_cp("parallel", "parallel"),
    )(hn, w_gate, w_up)


def _ffn_act_bwd(dh, w_down, g, u, tm=1024, tn=1408):
    t, d = dh.shape
    ff = w_down.shape[0]

    def body(a_ref, w_ref, g_ref, u_ref, dg_ref, du_ref):
        dv = _bdot(a_ref[...], w_ref[...], NT).astype(BF16).astype(F32)
        gv = g_ref[...].astype(F32)
        sg = _sigmoid(gv)
        dg_ref[...] = (dv * u_ref[...].astype(F32) * (sg + gv * sg * (1.0 - sg))).astype(BF16)
        du_ref[...] = (dv * gv * sg).astype(BF16)

    tile = pl.BlockSpec((tm, tn), lambda i, j: (i, j))
    return pl.pallas_call(
        body, name="ffn_act_bwd", grid=(t // tm, ff // tn),
        in_specs=[pl.BlockSpec((tm, d), lambda i, j: (i, 0)), pl.BlockSpec((tn, d), lambda i, j: (j, 0)),
                  tile, tile],
        out_specs=[tile] * 2, out_shape=[jax.ShapeDtypeStruct((t, ff), BF16)] * 2,
        compiler_params=_cp("parallel", "parallel"),
    )(dh, w_down, g, u)


def _gate_merge_fwd(pb, ya, yb, tm=512):
    t, d = ya.shape
    cb = PB_GATE // d

    def body(ga_ref, gb_ref, ya_ref, yb_ref, o_ref):
        o_ref[...] = (_sigmoid(ga_ref[...]) * ya_ref[...] + _sigmoid(gb_ref[...]) * yb_ref[...]).astype(BF16)

    return pl.pallas_call(
        body, name="gate_merge_fwd", grid=(t // tm,),
        in_specs=[_row_spec(tm, d, cb), _row_spec(tm, d, cb + 1), _row_spec(tm, d), _row_spec(tm, d)],
        out_specs=_row_spec(tm, d),
        out_shape=jax.ShapeDtypeStruct((t, d), BF16), compiler_params=_cp("parallel"),
    )(pb, pb, ya, yb)


def _gate_merge_bwd(pb, ya, yb, dm, tm=512):
    t, d = ya.shape
    cb = PB_GATE // d

    def body(ga_ref, gb_ref, ya_ref, yb_ref, dm_ref, dya_ref, dyb_ref, dga_ref, dgb_ref):
        dmv = dm_ref[...].astype(F32)
        sa = _sigmoid(ga_ref[...])
        sb = _sigmoid(gb_ref[...])
        dya_ref[...] = (dmv * sa).astype(BF16)
        dyb_ref[...] = (dmv * sb).astype(BF16)
        dga_ref[...] = (dmv * ya_ref[...] * sa * (1.0 - sa)).astype(BF16)
        dgb_ref[...] = (dmv * yb_ref[...] * sb * (1.0 - sb)).astype(BF16)

    return pl.pallas_call(
        body, name="gate_merge_bwd", grid=(t // tm,),
        in_specs=[_row_spec(tm, d, cb), _row_spec(tm, d, cb + 1), _row_spec(tm, d), _row_spec(tm, d),
                  _row_spec(tm, d)],
        out_specs=[_row_spec(tm, d)] * 4,
        out_shape=[jax.ShapeDtypeStruct((t, d), BF16)] * 4, compiler_params=_cp("parallel"),
    )(pb, pb, ya, yb, dm)


def _head_norm_fwd(o, pb, wn, tm=512):
    t, d = o.shape
    nh = d // HEAD

    def body(o_ref, z_ref, w_ref, out_ref):
        wv = w_ref[...]
        for h in range(nh):
            sl = slice(h * HEAD, (h + 1) * HEAD)
            ov = o_ref[:, sl]
            zv = z_ref[:, sl]
            r = lax.rsqrt(jnp.mean(ov * ov, axis=-1, keepdims=True) + EPS)
            out_ref[:, sl] = (ov * r * wv * (zv * _sigmoid(zv))).astype(BF16)

    return pl.pallas_call(
        body, name="head_norm_fwd", grid=(t // tm,),
        in_specs=[_row_spec(tm, d), _row_spec(tm, d, PB_Z // d), _bcast_spec(1, HEAD)],
        out_specs=_row_spec(tm, d),
        out_shape=jax.ShapeDtypeStruct((t, d), BF16), compiler_params=_cp("parallel"),
    )(o, pb, wn)


def _head_norm_bwd(o, pb, wn, dout, tm=512):
    t, d = o.shape
    nh = d // HEAD

    def body(o_ref, z_ref, w_ref, d_ref, do_ref, dz_ref, dw_ref):
        wv = w_ref[...]
        dw_acc = jnp.zeros((1, HEAD), F32)
        for h in range(nh):
            sl = slice(h * HEAD, (h + 1) * HEAD)
            ov = o_ref[:, sl]
            zv = z_ref[:, sl]
            dv = d_ref[:, sl].astype(F32)
            r = lax.rsqrt(jnp.mean(ov * ov, axis=-1, keepdims=True) + EPS)
            xh = ov * r
            sz = _sigmoid(zv)
            dn = dv * (zv * sz)
            dz_ref[:, sl] = (dv * xh * wv * (sz + zv * sz * (1.0 - sz))).astype(BF16)
            dxh = dn * wv
            do_ref[:, sl] = r * (dxh - xh * jnp.mean(dxh * xh, axis=-1, keepdims=True))
            dw_acc = dw_acc + jnp.sum(dn * xh, axis=0, keepdims=True)

        @pl.when(pl.program_id(0) == 0)
        def _():
            dw_ref[...] = jnp.zeros_like(dw_ref)

        dw_ref[...] += dw_acc

    return pl.pallas_call(
        body, name="head_norm_bwd", grid=(t // tm,),
        in_specs=[_row_spec(tm, d), _row_spec(tm, d, PB_Z // d), _bcast_spec(1, HEAD), _row_spec(tm, d)],
        out_specs=[_row_spec(tm, d), _row_spec(tm, d), _bcast_spec(1, HEAD)],
        out_shape=[jax.ShapeDtypeStruct((t, d), F32), jax.ShapeDtypeStruct((t, d), BF16),
                   jax.ShapeDtypeStruct((1, HEAD), F32)],
        compiler_params=_cp("arbitrary"),
    )(o, pb, wn, dout)


def _attn_bias(gi, hh, dil):
    i = lax.broadcasted_iota(jnp.int32, (BLOCK_A, BLOCK_A), 0)
    j = lax.broadcasted_iota(jnp.int32, (BLOCK_A, BLOCK_A), 1)
    hf = (gi * HEADS_PER_GROUP + hh + 1).astype(F32)
    slope = jnp.exp(jnp.full((1, BLOCK_A), -8.0 * math.log(2.0) / N_HEADS_A, F32) * hf) * float(dil)
    d_prev = (BLOCK_A + i - j).astype(F32)
    d_cur = (i - j).astype(F32)
    return -slope * d_prev, -slope * d_cur, j >= i, j <= i


ATTN_TOKENS = 2048


def _sub_rows(a, r, dil):
    start = a * BLOCK_A * dil + r
    return pl.ds(start, BLOCK_A) if dil == 1 else pl.ds(start, BLOCK_A, stride=dil)


def _attn_fwd_group(pa, gi, dil, tb=ATTN_TOKENS):
    t = pa.shape[0]
    tb = min(tb, t)
    hb = BLOCK_A * dil
    nb = tb // hb
    scale = HEAD ** -0.5

    def body(q_ref, k_ref, v_ref, kp_ref, vp_ref, o_ref, l_ref):
        hh = pl.program_id(0)
        step = pl.program_id(1)
        b_prev, b_cur, m_prev, m_cur = _attn_bias(gi, hh, dil)
        m_first = jnp.logical_and(m_prev, step > 0)
        for r in range(dil):
            kp, vp = kp_ref[_sub_rows(0, r, dil), :], vp_ref[_sub_rows(0, r, dil), :]
            for a in range(nb):
                rows = _sub_rows(a, r, dil)
                q, kc, vc = q_ref[rows, :], k_ref[rows, :], v_ref[rows, :]
                s_p = jnp.where(m_first if a == 0 else m_prev, _bdot(q, kp, NT) * scale + b_prev, -1e30)
                s_c = jnp.where(m_cur, _bdot(q, kc, NT) * scale + b_cur, -1e30)
                m = jnp.maximum(jnp.max(s_p, axis=-1, keepdims=True), jnp.max(s_c, axis=-1, keepdims=True))
                p_p = jnp.exp(s_p - m)
                p_c = jnp.exp(s_c - m)
                den = _rowsum(p_p) + _rowsum(p_c)
                o_ref[rows, :] = (_bdot(p_p, vp) + _bdot(p_c, vc)) / den
                l_ref[rows, :] = (m + jnp.log(den)) + jnp.zeros((BLOCK_A, HEAD), F32)
                kp, vp = kc, vc

    def col(base):
        return lambda hh, s: (s, base + hh)

    def col_prev(base):
        return lambda hh, s: (jnp.maximum(s * nb - 1, 0), base + hh)

    qb, kb, vb = gi * HEADS_PER_GROUP, N_HEADS_A + gi * HEADS_PER_GROUP, 2 * N_HEADS_A + gi * HEADS_PER_GROUP
    ospec = pl.BlockSpec((tb, HEAD), lambda hh, s: (s, hh))
    return pl.pallas_call(
        body, name=f"attn_fwd_g{gi}", grid=(HEADS_PER_GROUP, t // tb),
        in_specs=[pl.BlockSpec((tb, HEAD), col(qb)), pl.BlockSpec((tb, HEAD), col(kb)),
                  pl.BlockSpec((tb, HEAD), col(vb)),
                  pl.BlockSpec((hb, HEAD), col_prev(kb)), pl.BlockSpec((hb, HEAD), col_prev(vb))],
        out_specs=[ospec, ospec],
        out_shape=[jax.ShapeDtypeStruct((t, D_ATTN_OUT), F32)] * 2,
        compiler_params=_cp("parallel", "parallel"),
    )(pa, pa, pa, pa, pa)


def _attn_merge(os, ls, tm=512):
    t, d = os[0].shape

    def body(o0, o1, o2, l0, l1, l2, y_ref, lse_ref):
        a0, a1, a2 = l0[...], l1[...], l2[...]
        m = jnp.maximum(jnp.maximum(a0, a1), a2)
        e0, e1, e2 = jnp.exp(a0 - m), jnp.exp(a1 - m), jnp.exp(a2 - m)
        den = e0 + e1 + e2
        y_ref[...] = (e0 * o0[...] + e1 * o1[...] + e2 * o2[...]) / den
        lse_ref[...] = m + jnp.log(den)

    return pl.pallas_call(
        body, name="attn_merge", grid=(t // tm,),
        in_specs=[_row_spec(tm, d)] * 6, out_specs=[_row_spec(tm, d)] * 2,
        out_shape=[jax.ShapeDtypeStruct((t, d), F32)] * 2,
        compiler_params=_cp("parallel"),
    )(*os, *ls)


def _attn_bwd_group(pa, dy, y, lse, gi, dil, tb=ATTN_TOKENS):
    t = pa.shape[0]
    tb = min(tb, t)
    hb = BLOCK_A * dil
    nb = tb // hb
    nsteps = t // tb
    scale = HEAD ** -0.5

    def body(q_ref, k_ref, v_ref, dy_ref, y_ref, l_ref, kp_ref, vp_ref, d_ref,
             dq_s, dk_s, dv_s, carry_k, carry_v):
        hh = pl.program_id(0)
        step = pl.program_id(1)

        @pl.when(step == 0)
        def _():
            carry_k[...] = jnp.zeros_like(carry_k)
            carry_v[...] = jnp.zeros_like(carry_v)

        b_prev, b_cur, m_prev, m_cur = _attn_bias(gi, hh, dil)
        m_first = jnp.logical_and(m_prev, step < nsteps - 1)
        for r in range(dil):
            halo = _sub_rows(0, r, dil)
            dk_in, dv_in = carry_k[halo, :], carry_v[halo, :]
            kp, vp = kp_ref[halo, :], vp_ref[halo, :]
            prev_rows = None
            dk_pend = dv_pend = None
            for a in range(nb):
                rows = _sub_rows(a, r, dil)
                q, kc, vc = q_ref[rows, :], k_ref[rows, :], v_ref[rows, :]
                dyb, lb = dy_ref[rows, :], l_ref[rows, :]
                delta = _rowsum(dyb * y_ref[rows, :])
                mp = m_first if a == 0 else m_prev
                s = _bdot(q, kp, NT) * scale + b_prev
                p = jnp.where(mp, jnp.exp(jnp.where(mp, s - lb, 0.0)), 0.0)
                ds = p * (_bdot(dyb, vp, NT) - delta)
                dq = _bdot(ds, kp)
                dk_prev, dv_prev = _bdot(ds, q, TN), _bdot(p, dyb, TN)
                if a == 0:
                    carry_k[halo, :] = dk_prev
                    carry_v[halo, :] = dv_prev
                else:
                    dk_s[prev_rows, :] = dk_pend + dk_prev
                    dv_s[prev_rows, :] = dv_pend + dv_prev
                s = _bdot(q, kc, NT) * scale + b_cur
                p = jnp.where(m_cur, jnp.exp(jnp.where(m_cur, s - lb, 0.0)), 0.0)
                ds = p * (_bdot(dyb, vc, NT) - delta)
                dq_s[rows, :] = dq + _bdot(ds, kc)
                dk_pend, dv_pend = _bdot(ds, q, TN), _bdot(p, dyb, TN)
                prev_rows, kp, vp = rows, kc, vc
            dk_s[prev_rows, :] = dk_pend + dk_in
            dv_s[prev_rows, :] = dv_pend + dv_in
        d_ref[:, :HEAD] = (dq_s[...] * scale).astype(BF16)
        d_ref[:, HEAD:2 * HEAD] = (dk_s[...] * scale).astype(BF16)
        d_ref[:, 2 * HEAD:] = dv_s[...].astype(BF16)

    def col(base):
        return lambda hh, s: (nsteps - 1 - s, base + hh)

    def col_prev(base):
        return lambda hh, s: (jnp.maximum((nsteps - 1 - s) * nb - 1, 0), base + hh)

    qb, kb, vb = gi * HEADS_PER_GROUP, N_HEADS_A + gi * HEADS_PER_GROUP, 2 * N_HEADS_A + gi * HEADS_PER_GROUP
    big, small = (tb, HEAD), (hb, HEAD)
    return pl.pallas_call(
        body, name=f"attn_bwd_g{gi}", grid=(HEADS_PER_GROUP, nsteps),
        in_specs=[pl.BlockSpec(big, col(qb)), pl.BlockSpec(big, col(kb)), pl.BlockSpec(big, col(vb)),
                  pl.BlockSpec(big, col(0)), pl.BlockSpec(big, col(0)), pl.BlockSpec(big, col(0)),
                  pl.BlockSpec(small, col_prev(kb)), pl.BlockSpec(small, col_prev(vb))],
        out_specs=pl.BlockSpec((tb, 3 * HEAD), col(0)),
        out_shape=jax.ShapeDtypeStruct((t, 3 * D_ATTN_OUT), BF16),
        scratch_shapes=[pltpu.VMEM(big, F32)] * 3 + [pltpu.VMEM(small, F32)] * 2,
        compiler_params=_cp("parallel", "arbitrary"),
    )(pa, pa, pa, dy, y, lse, pa, pa)


def _shift_down(cur, prev8, s):
    if s == 0:
        return cur
    rolled = pltpu.roll(cur, s, 0)
    prolled = pltpu.roll(prev8, s, 0)
    rid = lax.broadcasted_iota(jnp.int32, prev8.shape, 0)
    top = jnp.where(rid < s, prolled, rolled[:8])
    return jnp.concatenate([top, rolled[8:]], axis=0)


def _shift_up(cur, next8, s):
    if s == 0:
        return cur
    n = cur.shape[0]
    rolled = pltpu.roll(cur, n - s, 0)
    nrolled = pltpu.roll(next8, 8 - s, 0)
    rid = lax.broadcasted_iota(jnp.int32, next8.shape, 0)
    bottom = jnp.where(rid >= 8 - s, nrolled, rolled[n - 8:])
    return jnp.concatenate([rolled[:n - 8], bottom], axis=0)


def _conv(xv, prev8, wv):
    c = jnp.zeros_like(xv)
    shifted = []
    for s in range(CONV_WIDTH):
        xs = _shift_down(xv, prev8, s)
        shifted.append(xs)
        c = c + wv[CONV_WIDTH - 1 - s:CONV_WIDTH - s, :] * xs
    return c, shifted


def _head_expand(psv, first):
    tm = psv.shape[0]
    return jnp.concatenate([jnp.broadcast_to(psv[:, first + h:first + h + 1], (tm, HEAD))
                            for h in range(N_HEADS_B)], axis=1)


def _head_collect(x, first):
    lane = lax.broadcasted_iota(jnp.int32, (x.shape[0], HEAD), 1)
    out = jnp.zeros((x.shape[0], HEAD), F32)
    for h in range(N_HEADS_B):
        out = jnp.where(lane == first + h, x[:, h * HEAD:(h + 1) * HEAD], out)
    return out


def _dn_prep_fwd(pb, ps, conv_w, a_log_bc, dt_bias_bc, tm=256, exch=None):
    t = pb.shape[0]
    c3 = 3 * D_B
    r8 = tm // 8

    def body(x_ref, xp_ref, ps_ref, w_ref, al_ref, dt_ref, q_ref, k_ref, v_ref, g_ref, beta_ref):
        first = pl.program_id(0) > 0

        def silu_conv(cols):
            c, _ = _conv(x_ref[:, cols], jnp.where(first, xp_ref[:, cols], 0.0), w_ref[:, cols])
            return c * _sigmoid(c)

        for h in range(N_HEADS_B):
            sl = slice(h * HEAD, (h + 1) * HEAD)
            sq = silu_conv(sl)
            q_ref[:, sl] = sq * lax.rsqrt(_rowsum(sq * sq) + EPS) * (HEAD ** -0.5)
            sk = silu_conv(slice(D_B + h * HEAD, D_B + (h + 1) * HEAD))
            k_ref[:, sl] = sk * lax.rsqrt(_rowsum(sk * sk) + EPS)
            v_ref[:, sl] = silu_conv(slice(2 * D_B + h * HEAD, 2 * D_B + (h + 1) * HEAD))
        psv = ps_ref[...]
        beta_ref[...] = _sigmoid(_head_expand(psv, 0))
        g_ref[...] = -jnp.exp(al_ref[...]) * _softplus(_head_expand(psv, N_HEADS_B) + dt_ref[...])

    return _hosted_call(
        body, name="dn_prep_fwd", grid=(t // tm,),
        in_specs=[_row_spec(tm, c3, 0),
                  pl.BlockSpec((8, c3), lambda i: (jnp.maximum(i * r8 - 1, 0), 0)),
                  _row_spec(tm, HEAD),
                  _bcast_spec(CONV_WIDTH, c3), _bcast_spec(1, D_B), _bcast_spec(1, D_B)],
        out_specs=[_row_spec(tm, D_B)] * 5,
        out_shape=[jax.ShapeDtypeStruct((t, D_B), F32)] * 5, scratch_shapes=[],
        args=(pb, pb, ps, conv_w, a_log_bc, dt_bias_bc), dims=("parallel",), exch=exch)


def _dn_prep_bwd(pb, ps, conv_w, a_log_bc, dt_bias_bc, g, dq, dk, dv, dg, dbeta, tm=128, exch=None):
    t = pb.shape[0]
    c3 = 3 * D_B
    r8 = tm // 8

    def body(x_ref, xp_ref, ps_ref, w_ref, al_ref, dt_ref, g_ref, dq_ref, dk_ref, dv_ref, dg_ref, db_ref,
             dc_ref, dps_ref, dw_ref, dal_ref, ddt_ref):
        first = pl.program_id(0) > 0

        @pl.when(pl.program_id(0) == 0)
        def _():
            dw_ref[...] = jnp.zeros_like(dw_ref)
            dal_ref[...] = jnp.zeros_like(dal_ref)
            ddt_ref[...] = jnp.zeros_like(ddt_ref)

        def column_block(cols, d_ref, sl, mult, normed):
            c, shifted = _conv(x_ref[:, cols], jnp.where(first, xp_ref[:, cols], 0.0), w_ref[:, cols])
            sg = _sigmoid(c)
            dsilu = sg + c * sg * (1.0 - sg)
            dyv = d_ref[:, sl]
            if normed:
                sv = c * sg
                r = lax.rsqrt(_rowsum(sv * sv) + EPS)
                yh = sv * r
                dyv = dyv * mult
                dyv = r * (dyv - yh * _rowsum(dyv * yh))
            dcv = dyv * dsilu
            dc_ref[:, cols] = dcv
            for sft in range(CONV_WIDTH):
                j = CONV_WIDTH - 1 - sft
                dw_ref[j:j + 1, cols] += jnp.sum(dcv * shifted[sft], axis=0, keepdims=True)

        for h in range(N_HEADS_B):
            sl = slice(h * HEAD, (h + 1) * HEAD)
            column_block(sl, dq_ref, sl, HEAD ** -0.5, True)
            column_block(slice(D_B + h * HEAD, D_B + (h + 1) * HEAD), dk_ref, sl, 1.0, True)
            column_block(slice(2 * D_B + h * HEAD, 2 * D_B + (h + 1) * HEAD), dv_ref, sl, 1.0, False)
        psv = ps_ref[...]
        beta = _sigmoid(_head_expand(psv, 0))
        dgv = dg_ref[...]
        da = dgv * (-jnp.exp(al_ref[...])) * _sigmoid(_head_expand(psv, N_HEADS_B) + dt_ref[...])
        dps_ref[...] = _head_collect(db_ref[...] * beta * (1.0 - beta), 0) + _head_collect(da, N_HEADS_B)
        dal_ref[...] += jnp.sum(dgv * g_ref[...], axis=0, keepdims=True)
        ddt_ref[...] += jnp.sum(da, axis=0, keepdims=True)

    row = _row_spec(tm, D_B)
    return _hosted_call(
        body, name="dn_prep_bwd", grid=(t // tm,),
        in_specs=[_row_spec(tm, c3, 0),
                  pl.BlockSpec((8, c3), lambda i: (jnp.maximum(i * r8 - 1, 0), 0)),
                  _row_spec(tm, HEAD),
                  _bcast_spec(CONV_WIDTH, c3), _bcast_spec(1, D_B), _bcast_spec(1, D_B),
                  row, row, row, row, row, row],
        out_specs=[_row_spec(tm, c3), _row_spec(tm, HEAD), _bcast_spec(CONV_WIDTH, c3), _bcast_spec(1, D_B),
                   _bcast_spec(1, D_B)],
        out_shape=[jax.ShapeDtypeStruct((t, c3), F32), jax.ShapeDtypeStruct((t, HEAD), F32),
                   jax.ShapeDtypeStruct((CONV_WIDTH, c3), F32),
                   jax.ShapeDtypeStruct((1, D_B), F32), jax.ShapeDtypeStruct((1, D_B), F32)],
        scratch_shapes=[], args=(pb, pb, ps, conv_w, a_log_bc, dt_bias_bc, g, dq, dk, dv, dg, dbeta),
        dims=("arbitrary",), exch=exch)


def _conv_bwd_input(dc, conv_w, tm=256):
    t, c3 = dc.shape
    r8 = tm // 8
    nlast = t // 8 - 1
    nsteps = t // tm

    def body(d_ref, dn_ref, w_ref, o_ref):
        not_last = pl.program_id(0) < nsteps - 1
        for cb in range(c3 // HEAD):
            cols = slice(cb * HEAD, (cb + 1) * HEAD)
            next8 = jnp.where(not_last, dn_ref[:, cols], 0.0)
            dv = d_ref[:, cols]
            wv = w_ref[:, cols]
            acc = jnp.zeros_like(dv)
            for s in range(CONV_WIDTH):
                acc = acc + wv[CONV_WIDTH - 1 - s:CONV_WIDTH - s, :] * _shift_up(dv, next8, s)
            o_ref[:, cols] = acc.astype(BF16)

    return pl.pallas_call(
        body, name="conv_bwd_input", grid=(nsteps,),
        in_specs=[_row_spec(tm, c3), pl.BlockSpec((8, c3), lambda i: (jnp.minimum((i + 1) * r8, nlast), 0)),
                  _bcast_spec(CONV_WIDTH, c3)],
        out_specs=_row_spec(tm, c3),
        out_shape=jax.ShapeDtypeStruct((t, c3), BF16), compiler_params=_cp("parallel"),
    )(dc, dc, conv_w)


def _lanes(x):
    return x[:, :CH]


def _tri_inv(a_list, r, c):
    eye = (r == c).astype(F32)
    b16 = (r >> 4) == (c >> 4)
    b32 = (r >> 5) == (c >> 5)
    ns = [jnp.where(b16, -a, 0.0) for a in a_list]
    xs = [eye + n for n in ns]
    ps = [_bdot(n, n) for n in ns]
    for last in (False, False, True):
        xs = [x + _bdot(x, p) for x, p in zip(xs, ps)]
        if not last:
            ps = [_bdot(p, p) for p in ps]
    for mask in (jnp.logical_and(b32, jnp.logical_not(b16)), jnp.logical_not(b32)):
        ts = [_bdot(x, jnp.where(mask, a, 0.0)) for x, a in zip(xs, a_list)]
        xs = [x - _bdot(t, x) for x, t in zip(xs, ts)]
    return xs


def _chunk_local(qs, ks, vs, gs, betas, solved=None):
    r = lax.broadcasted_iota(jnp.int32, (CH, CH), 0)
    c = lax.broadcasted_iota(jnp.int32, (CH, CH), 1)
    incl, strict = r >= c, r > c
    lm = incl.astype(F32)
    cums = [_hdot(lm, jnp.concatenate([g, jnp.where(strict, _lanes(g), 0.0)], axis=1)) for g in gs]
    gcbs = [cm[:, :HEAD] for cm in cums]
    decays = [jnp.where(incl, jnp.exp(jnp.where(incl, cm[:, HEAD:], 0.0)), 0.0) for cm in cums]
    bcols = [_lanes(b) for b in betas]
    kks = [_bdot(k, k, NT) for k in ks]
    qkraws = [_bdot(q, k, NT) for q, k in zip(qs, ks)]
    egs = [jnp.exp(gcb) for gcb in gcbs]
    if solved is None:
        tms = _tri_inv([jnp.where(strict, bc * kk * dc, 0.0) for bc, kk, dc in zip(bcols, kks, decays)], r, c)
        sols = [_hdot(tm, jnp.concatenate([b * v, b * eg * k], axis=1))
                for tm, b, v, eg, k in zip(tms, betas, vs, egs, ks)]
        ubars, ws = [sol[:, :HEAD] for sol in sols], [sol[:, HEAD:] for sol in sols]
    else:
        tms, ubars, ws = solved
    gls = [gcb[CH - 1:CH, :] for gcb in gcbs]
    eks = [jnp.exp(gl - gcb) for gl, gcb in zip(gls, gcbs)]
    return [dict(incl=incl, strict=strict, r=r, c=c, decay=dc, bcol=bc, kk=kk, tm=tm, eg=eg,
                 u_bar=ub, w=w, qkraw=qkraw, gl=gl, ek=ek)
            for dc, bc, kk, tm, eg, ub, w, qkraw, gl, ek
            in zip(decays, bcols, kks, tms, egs, ubars, ws, qkraws, gls, eks)]


def _dn_chunk_fwd(q, k, v, g, beta, cps=CHUNK_GROUP):
    t = q.shape[0]
    tm = cps * CH

    def body(q_ref, k_ref, v_ref, g_ref, b_ref, ub_ref, w_ref, qd_ref, kd_ref, qk_ref, ti_ref, gl_ref):
        for base in range(0, cps, CHUNK_GROUP):
            sls = [slice((base + j) * CH, (base + j + 1) * CH) for j in range(CHUNK_GROUP)]
            qs, ks = [q_ref[sl, :] for sl in sls], [k_ref[sl, :] for sl in sls]
            locs = _chunk_local(qs, ks, [v_ref[sl, :] for sl in sls], [g_ref[sl, :] for sl in sls],
                                [b_ref[sl, :] for sl in sls])
            for j, (sl, qv, kv, loc) in enumerate(zip(sls, qs, ks, locs)):
                ub_ref[sl, :] = loc["u_bar"]
                w_ref[sl, :] = loc["w"]
                qd_ref[sl, :] = qv * loc["eg"]
                kd_ref[sl, :] = kv * loc["ek"]
                qk_ref[sl, :] = loc["qkraw"] * loc["decay"]
                ti_ref[sl, :] = loc["tm"]
                gl_ref[base + j:base + j + 1, :] = jnp.exp(loc["gl"])

    hspec = pl.BlockSpec((tm, HEAD), lambda h, i: (i, h))
    sq_spec = pl.BlockSpec((None, tm, CH), lambda h, i: (h, i, 0))
    sq_shape = jax.ShapeDtypeStruct((N_HEADS_B, t, CH), F32)
    return pl.pallas_call(
        body, name="dn_chunk_fwd", grid=(N_HEADS_B, t // tm),
        in_specs=[hspec] * 5,
        out_specs=[hspec] * 4 + [sq_spec, sq_spec, pl.BlockSpec((cps, HEAD), lambda h, i: (i, h))],
        out_shape=[jax.ShapeDtypeStruct((t, D_B), F32)] * 4
        + [sq_shape, sq_shape, jax.ShapeDtypeStruct((t // CH, D_B), F32)],
        compiler_params=_cp("parallel", "parallel"),
    )(q, k, v, g, beta)


def _dn_scan_fwd(ub, w, qd, kd, qk, gl, cps=8):
    t = ub.shape[0]
    tm = cps * CH

    hg = SCAN_HEADS
    hs = list(range(hg))

    def body(ub_ref, w_ref, qd_ref, kd_ref, qk_ref, gl_ref, o_ref, st_ref, s_acc):
        @pl.when(pl.program_id(1) == 0)
        def _():
            s_acc[...] = jnp.zeros_like(s_acc)

        for ci in range(cps):
            sl = slice(ci * CH, (ci + 1) * CH)
            cols = [slice(h * HEAD, (h + 1) * HEAD) for h in hs]
            svs = [s_acc[h] for h in hs]
            for h in hs:
                st_ref[h, ci * HEAD:(ci + 1) * HEAD, :] = svs[h]
            us = [ub_ref[sl, cols[h]] - _bdot(w_ref[sl, cols[h]], svs[h]) for h in hs]
            for h in hs:
                s_acc[h] = gl_ref[ci:ci + 1, cols[h]] * svs[h] + _bdot(kd_ref[sl, cols[h]], us[h], TN)
            for h in hs:
                o_ref[sl, cols[h]] = _bdot(qd_ref[sl, cols[h]], svs[h]) + _bdot(qk_ref[h, sl, :], us[h])

    hspec = pl.BlockSpec((tm, hg * HEAD), lambda h, i: (i, h))
    return pl.pallas_call(
        body, name="dn_scan_fwd", grid=(N_HEADS_B // hg, t // tm),
        in_specs=[hspec] * 4 + [pl.BlockSpec((hg, tm, CH), lambda h, i: (h, i, 0)),
                                pl.BlockSpec((cps, hg * HEAD), lambda h, i: (i, h))],
        out_specs=[hspec, pl.BlockSpec((hg, cps * HEAD, HEAD), lambda h, i: (h, i, 0))],
        out_shape=[jax.ShapeDtypeStruct((t, D_B), F32),
                   jax.ShapeDtypeStruct((N_HEADS_B, (t // CH) * HEAD, HEAD), F32)],
        scratch_shapes=[pltpu.VMEM((hg, HEAD, HEAD), F32)],
        compiler_params=_cp("parallel", "arbitrary"),
    )(ub, w, qd, kd, qk, gl)


def _dn_scan_bwd(ub, w, qd, kd, qk, gl, st, do, cps=8):
    t = ub.shape[0]
    tm = cps * CH
    ns = t // tm

    hg = SCAN_HEADS
    hs = list(range(hg))

    def body(ub_ref, w_ref, qd_ref, kd_ref, qk_ref, gl_ref, st_ref, do_ref,
             dub_ref, dw_ref, dqd_ref, dkd_ref, dqk_ref, dgl_ref, ds_acc):
        @pl.when(pl.program_id(1) == 0)
        def _():
            ds_acc[...] = jnp.zeros_like(ds_acc)

        for ci in reversed(range(cps)):
            sl = slice(ci * CH, (ci + 1) * CH)
            cols = [slice(h * HEAD, (h + 1) * HEAD) for h in hs]
            svs = [st_ref[h, ci * HEAD:(ci + 1) * HEAD, :] for h in hs]
            wvs = [w_ref[sl, cols[h]] for h in hs]
            dovs = [do_ref[sl, cols[h]] for h in hs]
            dsvs = [ds_acc[h] for h in hs]
            us = [ub_ref[sl, cols[h]] - _bdot(wvs[h], svs[h]) for h in hs]
            dus = [_bdot(kd_ref[sl, cols[h]], dsvs[h]) + _bdot(qk_ref[h, sl, :], dovs[h], TN) for h in hs]
            for h in hs:
                ds_acc[h] = (gl_ref[ci:ci + 1, cols[h]] * dsvs[h] + _bdot(qd_ref[sl, cols[h]], dovs[h], TN)
                             - _bdot(wvs[h], dus[h], TN))
            for h in hs:
                dgl_ref[ci:ci + 1, cols[h]] = (jnp.sum(_rowsum(dsvs[h] * svs[h]), axis=0, keepdims=True)
                                              + jnp.zeros((1, HEAD), F32))
                dkd_ref[sl, cols[h]] = _bdot(us[h], dsvs[h], NT)
                dqd_ref[sl, cols[h]] = _bdot(dovs[h], svs[h], NT)
                dqk_ref[h, sl, :] = _bdot(dovs[h], us[h], NT)
                dub_ref[sl, cols[h]] = dus[h]
                dw_ref[sl, cols[h]] = -_bdot(dus[h], svs[h], NT)

    hspec = pl.BlockSpec((tm, hg * HEAD), lambda h, i: (ns - 1 - i, h))
    qkspec = pl.BlockSpec((hg, tm, CH), lambda h, i: (h, ns - 1 - i, 0))
    glspec = pl.BlockSpec((cps, hg * HEAD), lambda h, i: (ns - 1 - i, h))
    return pl.pallas_call(
        body, name="dn_scan_bwd", grid=(N_HEADS_B // hg, ns),
        in_specs=[hspec] * 4 + [qkspec, glspec,
                                pl.BlockSpec((hg, cps * HEAD, HEAD), lambda h, i: (h, ns - 1 - i, 0)), hspec],
        out_specs=[hspec] * 4 + [qkspec, glspec],
        out_shape=[jax.ShapeDtypeStruct((t, D_B), F32)] * 4
        + [jax.ShapeDtypeStruct((N_HEADS_B, t, CH), F32), jax.ShapeDtypeStruct((t // CH, D_B), F32)],
        scratch_shapes=[pltpu.VMEM((hg, HEAD, HEAD), F32)],
        compiler_params=_cp("parallel", "arbitrary"),
    )(ub, w, qd, kd, qk, gl, st, do)


def _dn_chunk_bwd(q, k, v, g, beta, ub, w, tinv, dub, dw, dqd, dkd, dqk, dgl, cps=CHUNK_GROUP):
    t = q.shape[0]
    tm = cps * CH

    def body(q_ref, k_ref, v_ref, g_ref, b_ref, ub_ref, w_ref, dub_ref, dw_ref, dqd_ref, dkd_ref, ti_ref, dqk_ref,
             dgl_ref, dq_ref, dk_ref, dv_ref, dg_ref, db_ref):
        ones = jnp.ones((CH, HEAD), F32)
        rid = lax.broadcasted_iota(jnp.int32, (CH, HEAD), 0)

        def rest(ci, sl, qv, kv, vv, beta_v, loc, dr, da):
            incl = loc["incl"]
            eg, ek, decay, bcol, kk = loc["eg"], loc["ek"], loc["decay"], loc["bcol"], loc["kk"]
            drv, drk = dr[:, :HEAD], dr[:, HEAD:]
            dv_ref[sl, :] = beta_v * drv
            beg = beta_v * eg
            t1 = drk * kv
            dbeta = _rowsum(drv * vv + t1 * eg) + _rowsum(da * kk * decay)
            dkk = da * bcol * decay
            dqk_m = jnp.where(incl, dqk_ref[sl, :], 0.0)
            ddecay = da * bcol * kk + dqk_m * loc["qkraw"]
            dqkraw = dqk_m * decay
            dqdv, dkdv = dqd_ref[sl, :], dkd_ref[sl, :]
            dq_ref[sl, :] = _bdot(dqkraw, kv) + dqdv * eg
            dk_ref[sl, :] = (beg * drk + _bdot(dqkraw, qv, TN) + _bdot(dkk, kv) + _bdot(dkk, kv, TN)
                             + dkdv * ek)
            e = ddecay * decay
            skd = _rowsum(dkdv * kv * ek)
            dgc = _rowsum(beg * t1) + _rowsum(e) + _rowsum(dqdv * qv * eg) - skd
            colsum = _hdot(e, ones, TN)
            last = jnp.sum(skd, axis=0, keepdims=True) + dgl_ref[ci:ci + 1, :] * jnp.exp(loc["gl"])
            db_ref[sl, :] = dbeta + jnp.zeros((CH, HEAD), F32)
            return (dgc - colsum) + jnp.where(rid == CH - 1, last, 0.0)

        for base in range(0, cps, CHUNK_GROUP):
            cis = list(range(base, base + CHUNK_GROUP))
            sls = [slice(ci * CH, (ci + 1) * CH) for ci in cis]
            qs, ks, vs = [q_ref[sl, :] for sl in sls], [k_ref[sl, :] for sl in sls], [v_ref[sl, :] for sl in sls]
            betas = [b_ref[sl, :] for sl in sls]
            locs = _chunk_local(qs, ks, vs, [g_ref[sl, :] for sl in sls], betas,
                                solved=([ti_ref[sl, :] for sl in sls], [ub_ref[sl, :] for sl in sls],
                                        [w_ref[sl, :] for sl in sls]))
            drs = [_hdot(loc["tm"], jnp.concatenate([dub_ref[sl, :], dw_ref[sl, :]], axis=1), TN)
                   for loc, sl in zip(locs, sls)]
            das = [jnp.where(loc["strict"],
                             -_hdot(dr, jnp.concatenate([loc["u_bar"], loc["w"]], axis=1), NT), 0.0)
                   for loc, dr in zip(locs, drs)]
            dgcs = [rest(*args) for args in zip(cis, sls, qs, ks, vs, betas, locs, drs, das)]
            um = (locs[0]["r"] <= locs[0]["c"]).astype(F32)
            for sl, dgc_bc in zip(sls, dgcs):
                dg_ref[sl, :] = _hdot(um, dgc_bc)

    hspec = pl.BlockSpec((tm, HEAD), lambda h, i: (i, h))
    sq_spec = pl.BlockSpec((None, tm, CH), lambda h, i: (h, i, 0))
    return pl.pallas_call(
        body, name="dn_chunk_bwd", grid=(N_HEADS_B, t // tm),
        in_specs=[hspec] * 11 + [sq_spec, sq_spec, pl.BlockSpec((cps, HEAD), lambda h, i: (i, h))],
        out_specs=[hspec] * 5,
        out_shape=[jax.ShapeDtypeStruct((t, D_B), F32)] * 5,
        compiler_params=_cp("parallel", "parallel"),
    )(q, k, v, g, beta, ub, w, dub, dw, dqd, dkd, tinv, dqk, dgl)


FLIPS = [(fx, fy, fc) for fx in (0, 1) for fy in (0, 1) for fc in (0, 1)][1:]


def _mesh_pos():
    return lax.axis_index("x"), lax.axis_index("y"), lax.axis_index("c")


def _peer(pos, flip):
    return tuple((1 - p) if f else p for p, f in zip(pos, flip))


def _dev_index(pos):
    return 4 * pos[0] + 2 * pos[1] + pos[2]


class _Exchange:
    def __init__(self, tensors, scatter):
        self.tensors, self.scatter, self.nt = list(tensors), list(scatter), len(tensors)
        hbm = pl.BlockSpec(memory_space=pltpu.HBM)
        self.in_specs = [hbm] * self.nt
        self.out_specs = [hbm] * self.nt
        self.out_shape = [jax.ShapeDtypeStruct(x.shape if sc else (N_DEV,) + x.shape, x.dtype)
                          for x, sc in zip(tensors, scatter)]
        self.scratch_shapes = [pltpu.SemaphoreType.DMA((self.nt * 7,)), pltpu.SemaphoreType.DMA((self.nt * 7,)),
                               pltpu.SemaphoreType.DMA((self.nt,))]

    def _copies(self, ins, outs, sems):
        send_sems, recv_sems, local_sems = sems
        pos = _mesh_pos()
        me = _dev_index(pos)

        def remote(ti, fi, landing):
            peer = _peer(pos, FLIPS[fi])
            src = ins[ti].at[_dev_index(peer)] if self.scatter[ti] else ins[ti]
            return pltpu.make_async_remote_copy(
                src_ref=src, dst_ref=outs[ti].at[landing(peer)],
                send_sem=send_sems.at[ti * 7 + fi], recv_sem=recv_sems.at[ti * 7 + fi],
                device_id=peer, device_id_type=pl.DeviceIdType.MESH)

        pairs = [(ti, fi) for ti in range(self.nt) for fi in range(7)]
        local = [pltpu.make_async_copy(ins[ti].at[me] if self.scatter[ti] else ins[ti], outs[ti].at[me],
                                       local_sems.at[ti]) for ti in range(self.nt)]
        sends = [remote(ti, fi, lambda peer: me) for ti, fi in pairs]
        recvs = [remote(ti, fi, _dev_index) for ti, fi in pairs]
        return local, sends, recvs

    def start(self, ins, outs, sems):
        local, sends, _ = self._copies(ins, outs, sems)
        for cp in local + sends:
            cp.start()

    def wait(self, ins, outs, sems):
        local, sends, recvs = self._copies(ins, outs, sems)
        for cp in recvs:
            cp.wait_recv()
        for cp in sends:
            cp.wait_send()
        for cp in local:
            cp.wait()


def _exchange(tensors, scatter, name):
    ex = _Exchange(tensors, scatter)

    def body(*refs):
        ins, outs, sems = refs[:ex.nt], refs[ex.nt:2 * ex.nt], refs[2 * ex.nt:]
        ex.start(ins, outs, sems)
        ex.wait(ins, outs, sems)

    return pl.pallas_call(
        body, name=name, in_specs=ex.in_specs, out_specs=ex.out_specs, out_shape=ex.out_shape,
        scratch_shapes=ex.scratch_shapes, compiler_params=pltpu.CompilerParams(has_side_effects=True),
    )(*tensors)


def _gather_two_level(tensors, name):
    nt = len(tensors)
    hbm = pl.BlockSpec(memory_space=pltpu.HBM)

    def body(*refs):
        ins, outs = refs[:nt], refs[nt:2 * nt]
        send_sems, recv_sems, local_sems = refs[2 * nt:]
        x, y, c = _mesh_pos()
        sibling = (x, y, 1 - c)
        chips = [(1 - x, y), (x, 1 - y), (1 - x, 1 - y)]

        def copy(ti, k, block, to, own=False):
            slot = outs[ti].at[_dev_index(block)]
            return pltpu.make_async_remote_copy(
                src_ref=ins[ti] if own else slot, dst_ref=slot,
                send_sem=send_sems.at[ti * 7 + k], recv_sem=recv_sems.at[ti * 7 + k],
                device_id=to, device_id_type=pl.DeviceIdType.MESH)

        me = (x, y, c)
        mine = [pltpu.make_async_copy(ins[ti], outs[ti].at[_dev_index(me)], local_sems.at[ti]) for ti in range(nt)]
        first = [copy(ti, 0, me, sibling, own=True) for ti in range(nt)]
        first += [copy(ti, 1 + j, me, (*chip, c), own=True) for ti in range(nt) for j, chip in enumerate(chips)]
        for cp in mine + first:
            cp.start()
        passed = []
        for j, chip in enumerate(chips):
            for ti in range(nt):
                copy(ti, 1 + j, (*chip, c), me).wait_recv()
                cp = copy(ti, 4 + j, (*chip, c), sibling)
                cp.start()
                passed.append(cp)
        for ti in range(nt):
            copy(ti, 0, sibling, me).wait_recv()
            for j, chip in enumerate(chips):
                copy(ti, 4 + j, (*chip, 1 - c), me).wait_recv()
        for cp in first + passed:
            cp.wait_send()
        for cp in mine:
            cp.wait()

    return pl.pallas_call(
        body, name=name, in_specs=[hbm] * nt, out_specs=[hbm] * nt,
        out_shape=[jax.ShapeDtypeStruct((N_DEV,) + x.shape, x.dtype) for x in tensors],
        scratch_shapes=[pltpu.SemaphoreType.DMA((nt * 7,)), pltpu.SemaphoreType.DMA((nt * 7,)),
                        pltpu.SemaphoreType.DMA((nt,))],
        compiler_params=pltpu.CompilerParams(has_side_effects=True),
    )(*tensors)


def _hosted_call(body, *, name, grid, in_specs, out_specs, out_shape, scratch_shapes, args, dims, exch=None):
    if exch is None:
        return pl.pallas_call(body, name=name, grid=grid, in_specs=in_specs, out_specs=out_specs,
                              out_shape=out_shape, scratch_shapes=scratch_shapes,
                              compiler_params=_cp(*dims))(*args)
    n_in, n_out, n_sc, ne = len(in_specs), len(out_specs), len(scratch_shapes), exch.nt
    nsteps = math.prod(grid)

    def wrapped(*refs):
        ins, ex_in = refs[:n_in], refs[n_in:n_in + ne]
        outs = refs[n_in + ne:n_in + ne + n_out]
        ex_out = refs[n_in + ne + n_out:n_in + 2 * ne + n_out]
        rest = refs[n_in + 2 * ne + n_out:]
        scratch, sems = rest[:n_sc], rest[n_sc:]
        step = pl.program_id(0)
        for ax in range(1, len(grid)):
            step = step * grid[ax] + pl.program_id(ax)

        @pl.when(step == 0)
        def _():
            exch.start(ex_in, ex_out, sems)

        body(*ins, *outs, *scratch)

        @pl.when(step == nsteps - 1)
        def _():
            exch.wait(ex_in, ex_out, sems)

    return pl.pallas_call(
        wrapped, name=name, grid=grid, in_specs=list(in_specs) + exch.in_specs,
        out_specs=list(out_specs) + exch.out_specs, out_shape=list(out_shape) + exch.out_shape,
        scratch_shapes=list(scratch_shapes) + exch.scratch_shapes,
        compiler_params=pltpu.CompilerParams(dimension_semantics=("arbitrary",) * len(grid),
                                             vmem_limit_bytes=VMEM_LIMIT, has_side_effects=True),
    )(*args, *exch.tensors)


def _adamw(land, w, m, v, name, tm=256):
    n, r, c = land.shape
    tm = r if r <= tm else max(s for s in range(8, tm + 1, 8) if r % s == 0)
    bc1 = 1.0 / (1.0 - ADAM_B1 ** ADAM_STEP)
    bc2 = 1.0 / (1.0 - ADAM_B2 ** ADAM_STEP)

    def body(l_ref, w_ref, m_ref, v_ref, g_ref, d_ref, nm_ref, nv_ref):
        g = l_ref[0].astype(F32)
        for i in range(1, n):
            g = g + l_ref[i].astype(F32)
        nm = ADAM_B1 * m_ref[...] + (1.0 - ADAM_B1) * g
        nv = ADAM_B2 * v_ref[...] + (1.0 - ADAM_B2) * (g * g)
        g_ref[...] = g
        nm_ref[...] = nm
        nv_ref[...] = nv
        d_ref[...] = -ADAM_LR * ((nm * bc1) / (jnp.sqrt(nv * bc2) + ADAM_EPS) + ADAM_WD * w_ref[...])

    spec = pl.BlockSpec((tm, c), lambda i: (i, 0))
    return pl.pallas_call(
        body, name=name, grid=(r // tm,),
        in_specs=[pl.BlockSpec((n, tm, c), lambda i: (0, i, 0)), spec, spec, spec],
        out_specs=[spec] * 4, out_shape=[jax.ShapeDtypeStruct((r, c), F32)] * 4,
        compiler_params=_cp("parallel"),
    )(land, w, m, v)


PACK_W = 2048


def _pack_rows(parts):
    flat = jnp.concatenate([p.reshape(-1).astype(F32) for p in parts])
    pad = (-flat.shape[0]) % (8 * PACK_W)
    return jnp.pad(flat, (0, pad)).reshape(-1, PACK_W)


def _unpack_rows(packed, shapes):
    flat = packed.reshape(-1)
    out, off = [], 0
    for s in shapes:
        n = math.prod(s)
        out.append(flat[off:off + n].reshape(s))
        off += n
    return out


def _col_slabs(gfull, width):
    r = gfull.shape[0]
    return jnp.transpose(gfull.reshape(r, N_DEV, width), (1, 0, 2)).astype(BF16)


def _row_slabs(gfull):
    return gfull.reshape(N_DEV, gfull.shape[0] // N_DEV, gfull.shape[1]).astype(BF16)


def _from_col_slabs(gathered):
    n, r, width = gathered.shape
    return jnp.transpose(gathered, (1, 0, 2)).reshape(r, n * width)


def _local_step(xs, target, norm_mix, wf_in, cw, a_log, dt_bias, dn_norm, rest, norm_ffn, norm_final,
                distributed=True):
    d = D_MODEL
    n_main = D_PA + 4 * D_B
    w_pa_cols = wf_in[:, :D_PA]
    w_pb_cols = jnp.concatenate([wf_in[:, D_PA:n_main], wf_in[:, n_main + 2 * N_HEADS_B:]], axis=1)
    w_small = jnp.pad(wf_in[:, n_main:n_main + 2 * N_HEADS_B], ((0, 0), (0, HEAD - 2 * N_HEADS_B)))
    a_log_bc = jnp.repeat(a_log, HEAD, axis=1)
    dt_bias_bc = jnp.repeat(dt_bias, HEAD, axis=1)

    u = _rms_fwd(xs, norm_mix)
    pa = _matmul(u, w_pa_cols, "nn", F32, 1024, 1536, d, name="proj_a")
    pb = _matmul(u, w_pb_cols, "nn", F32, 1024, 1024, d, name="proj_b")
    ps = _matmul(u, w_small, "nn", F32, 2048, HEAD, d, name="proj_small")
    os_, ls_ = [], []
    for gi, dil in enumerate(DILATIONS):
        o_g, l_g = _attn_fwd_group(pa, gi, dil)
        os_.append(o_g)
        ls_.append(l_g)
    y_att, lse = _attn_merge(os_, ls_)
    prep = _dn_prep_fwd(pb, ps, cw, a_log_bc, dt_bias_bc,
                        exch=_Exchange(rest, [False] * 6) if distributed else None)
    qn, kn, vn, gdec, beta = prep[:5]
    if distributed:
        g_pa, g_pd, g_out, g_gate, g_up, g_down = prep[5:]
        wf_pa, wf_pd, wf_out = _from_col_slabs(g_pa), g_pd.reshape(D_B, d), g_out.reshape(d, d)
        wf_gate, wf_up, wf_down = _from_col_slabs(g_gate), _from_col_slabs(g_up), g_down.reshape(D_FF, d)
    else:
        wf_pa, wf_pd, wf_out, wf_gate, wf_up, wf_down = rest
    wf_gu = jnp.concatenate([wf_gate, wf_up], axis=1)
    ub, ww, qd, kd, qk, tinv, gl = _dn_chunk_fwd(qn, kn, vn, gdec, beta)
    o_dn, states = _dn_scan_fwd(ub, ww, qd, kd, qk, gl)
    o_gated = _head_norm_fwd(o_dn, pb, dn_norm)
    y_a = _matmul(y_att, wf_pa, "nn", F32, 1024, d, D_ATTN_OUT, name="proj_attn")
    y_b = _matmul(o_gated, wf_pd, "nn", F32, 1024, d, d, name="proj_delta")
    merged = _gate_merge_fwd(pb, y_a, y_b)
    h1 = _matmul(merged, wf_out, "nn", F32, 1024, d, d, add=xs, name="out_proj")
    hn = _rms_fwd(h1, norm_ffn)
    gate, up, act = _ffn_in(hn, wf_gate, wf_up)
    h2 = _matmul(act, wf_down, "nn", F32, 512, d, D_FF, add=h1, name="ffn_out")
    loss_part, dh2, d_norm_final = _final_loss(h2, norm_final.reshape(1, d), target)

    dgate, dup = _ffn_act_bwd(dh2, wf_down, gate, up)
    gw_down = _matmul(act, dh2, "tn", BF16, 1408, d, TOKEN_TK, name="gw_down")
    dhn = _matmul_nt_segments([dgate, dup], wf_gu, BF16, 1024, 1408, "d_hn")[0]
    gw_gate = _matmul(hn, dgate, "tn", BF16, d, 1408, TOKEN_TK, name="gw_gate")
    gw_up = _matmul(hn, dup, "tn", BF16, d, 1408, TOKEN_TK, name="gw_up")
    dh1, d_norm_ffn = _rms_bwd(h1, norm_ffn, dhn, dh2)
    dmerged = _matmul(dh1, wf_out, "nt", BF16, 1024, d, d, name="d_merged")
    gw_out = _matmul(merged, dh1, "tn", BF16, d, d, TOKEN_TK, name="gw_out")
    dya, dyb, dga, dgb = _gate_merge_bwd(pb, y_a, y_b, dmerged)
    dy_att = _matmul(dya, wf_pa, "nt", F32, 1024, D_ATTN_OUT, d, name="d_y_att")
    gw_pa = _matmul(y_att, dya, "tn", BF16, D_ATTN_OUT, d, TOKEN_TK, name="gw_pa")
    do_gated = _matmul(dyb, wf_pd, "nt", BF16, 1024, d, d, name="d_o_gated")
    gw_pd = _matmul(o_gated, dyb, "tn", BF16, d, d, TOKEN_TK, name="gw_pd")
    do_dn, dz, d_dn_norm = _head_norm_bwd(o_dn, pb, dn_norm, do_gated)
    dub, dww, dqd, dkd, dqk, dgl = _dn_scan_bwd(ub, ww, qd, kd, qk, gl, states, do_dn)
    dqn, dkn, dvn, dgdec, dbeta = _dn_chunk_bwd(qn, kn, vn, gdec, beta, ub, ww, tinv, dub, dww, dqd, dkd, dqk, dgl)
    slabs = [_col_slabs(gw_pa, d // N_DEV), _row_slabs(gw_pd), _row_slabs(gw_out),
             _col_slabs(gw_gate, D_FF // N_DEV), _col_slabs(gw_up, D_FF // N_DEV),
             _row_slabs(gw_down)] if distributed else None
    prep = _dn_prep_bwd(pb, ps, cw, a_log_bc, dt_bias_bc, gdec, dqn, dkn, dvn, dgdec, dbeta,
                        exch=_Exchange(slabs, [True] * 6) if distributed else None)
    dc, dps, d_conv_full, d_alog_bc, d_dt_bc = prep[:5]
    dqkv_pre = _conv_bwd_input(dc, cw)
    segs = [_attn_bwd_group(pa, dy_att, y_att, lse, gi, dil) for gi, dil in enumerate(DILATIONS)]
    segs += [dqkv_pre, dz, dga, dgb]
    gws = [_matmul(u, s, "tn", BF16, d, 1536, TOKEN_TK, name=f"gw_in_{i}") for i, s in enumerate(segs)]
    gw_small = _matmul(u, dps, "tn", BF16, d, HEAD, TOKEN_TK, name="gw_in_small")
    g_att = jnp.concatenate(gws[:3], axis=1).reshape(d, N_HEADS_A, 3, HEAD)
    gw_in = jnp.concatenate(
        [g_att[:, :, i, :].reshape(d, D_ATTN) for i in range(3)]
        + [gws[3], gws[4], gw_small[:, :2 * N_HEADS_B], gws[5], gws[6]], axis=1)
    w_att = jnp.stack([w_pa_cols[:, i * D_ATTN:(i + 1) * D_ATTN].reshape(d, N_HEADS_A, HEAD) for i in range(3)],
                      axis=2).reshape(d, D_PA)
    du_first = _matmul_nt_segments(
        segs[:4], jnp.concatenate([w_att, w_pb_cols[:, :3 * D_B]], axis=1), F32, 1024, 1536, "d_u_0",
        exch=_Exchange([_col_slabs(gw_in, SHARD_IN)], [True]) if distributed else None)
    du_small = _matmul(dps, w_small, "nt", F32, 1024, d, HEAD, add=du_first[0], name="d_u_small")
    du = _matmul_nt_segments(segs[4:], w_pb_cols[:, 3 * D_B:], BF16, 1024, 1024, "d_u_1", add=du_small)[0]
    dx, d_norm_mix = _rms_bwd(xs, norm_mix, du, dh1)
    d_a_log = d_alog_bc.reshape(1, N_HEADS_B, HEAD)[:, :, 0]
    d_dt_bias = d_dt_bc.reshape(1, N_HEADS_B, HEAD)[:, :, 0]
    small = (d_conv_full, d_norm_mix, d_norm_ffn, d_norm_final, d_dn_norm, d_a_log, d_dt_bias)
    if distributed:
        return (loss_part, dx, [du_first[1]] + list(prep[5:])) + small
    return (loss_part, dx, gw_in, gw_pa, gw_pd, gw_out, jnp.concatenate([gw_gate, gw_up], axis=1), gw_down) + small


def kernel(x, norm_mix, w_in, conv_w, a_log, dt_bias, dn_norm, w_proj_attn, w_proj_delta, w_out, norm_ffn, w_gate, w_up, w_down, norm_final, loss_target, m_norm_mix, m_w_in, m_conv_w, m_a_log, m_dt_bias, m_dn_norm, m_w_proj_attn, m_w_proj_delta, m_w_out, m_norm_ffn, m_w_gate, m_w_up, m_w_down, m_norm_final, v_norm_mix, v_w_in, v_conv_w, v_a_log, v_dt_bias, v_dn_norm, v_w_proj_attn, v_w_proj_delta, v_w_out, v_norm_ffn, v_w_gate, v_w_up, v_w_down, v_norm_final):
    d = D_MODEL
    xs = x[0]
    target = loss_target[0]
    me = _dev_index(_mesh_pos())

    g_in, g_conv = _gather_two_level([w_in[0].astype(BF16), conv_w[0]], "gather_w_in")
    rest = [w[0].astype(BF16) for w in (w_proj_attn, w_proj_delta, w_out, w_gate, w_up, w_down)]
    (loss_part, dx, landed, d_conv_full, d_norm_mix, d_norm_ffn, d_norm_final, d_dn_norm, d_a_log,
     d_dt_bias) = _local_step(xs, target, norm_mix, _from_col_slabs(g_in), _from_col_slabs(g_conv), a_log, dt_bias,
                              dn_norm, rest, norm_ffn, norm_final)

    small_shapes = [(1, d), (1, d), (d,), (1, HEAD), (1, N_HEADS_B), (1, N_HEADS_B), (1, 1), (CONV_WIDTH, 3 * D_B)]
    packed = _pack_rows([d_norm_mix, d_norm_ffn, d_norm_final, d_dn_norm, d_a_log, d_dt_bias,
                         loss_part[:, :1], d_conv_full])
    landed = list(landed) + list(_exchange([packed], [False], "gather_small_grads"))
    zero1 = jnp.zeros((1, 1), F32)
    zconv = jnp.zeros((CONV_WIDTH, 3 * D_B), F32)
    small_w = _pack_rows([norm_mix, norm_ffn, norm_final, dn_norm, a_log, dt_bias, zero1, zconv])
    small_m = _pack_rows([m_norm_mix, m_norm_ffn, m_norm_final, m_dn_norm, m_a_log, m_dt_bias, zero1, zconv])
    small_v = _pack_rows([v_norm_mix, v_norm_ffn, v_norm_final, v_dn_norm, v_a_log, v_dt_bias, zero1, zconv])
    small = [_unpack_rows(z, small_shapes) for z in _adamw(landed[7], small_w, small_m, small_v, "adamw_small")]
    loss = small[0][6].reshape(())
    conv_shard = 3 * D_B // N_DEV
    g_conv_own = lax.dynamic_slice_in_dim(small[0][7], me * conv_shard, conv_shard, axis=1)
    r_conv = _adamw(g_conv_own[None], conv_w[0], m_conv_w[0], v_conv_w[0], "adamw_conv")
    big = [_adamw(landed[i], w[0], m[0], v[0], f"adamw_{i}") for i, (w, m, v) in enumerate([
        (w_in, m_w_in, v_w_in), (w_proj_attn, m_w_proj_attn, v_w_proj_attn),
        (w_proj_delta, m_w_proj_delta, v_w_proj_delta), (w_out, m_w_out, v_w_out),
        (w_gate, m_w_gate, v_w_gate), (w_up, m_w_up, v_w_up), (w_down, m_w_down, v_w_down)])]

    def leaves(k):
        sm = small[k]
        return [sm[0], big[0][k][None], r_conv[k][None], sm[4], sm[5], sm[3], big[1][k][None], big[2][k][None],
                big[3][k][None], sm[1], big[4][k][None], big[5][k][None], big[6][k][None], sm[2]]

    return (loss, dx[None], *leaves(0), *leaves(1), *leaves(2), *leaves(3))
```

```python
import math

import jax
import jax.numpy as jnp
from jax import lax
from jax.experimental import pallas as pl
from jax.experimental.pallas import tpu as pltpu

F32 = jnp.float32
BF16 = jnp.bfloat16
HI = lax.Precision.HIGH

D_MODEL = 1024
N_DEV = 8
HEAD = 128
N_HEADS_A = 12
HEADS_PER_GROUP = 4
DILATIONS = (1, 4, 16)
BLOCK_A = 128
D_ATTN = N_HEADS_A * HEAD
D_ATTN_OUT = HEADS_PER_GROUP * HEAD
N_HEADS_B = 8
D_B = N_HEADS_B * HEAD
CONV_WIDTH = 4
CH = 64
CHUNK_GROUP = 32
SCAN_HEADS = 4
TOKEN_TK = 1024
D_FF = 2816
EPS = 1e-6
D_IN = 3 * D_ATTN + 4 * D_B + 2 * N_HEADS_B + 2 * D_MODEL
SHARD_IN = D_IN // N_DEV
PB_Z, PB_GATE = 3072, 4096
D_PA = 3 * D_ATTN
ADAM_LR, ADAM_B1, ADAM_B2, ADAM_EPS, ADAM_WD, ADAM_STEP = 0.001, 0.9, 0.999, 1e-08, 0.01, 10
VMEM_LIMIT = 56 * 1024 * 1024

NN = ((1,), (0,))
NT = ((1,), (1,))
TN = ((0,), (0,))


def _dot(a, b, dims=NN, prec=None):
    return lax.dot_general(a, b, (dims, ((), ())), precision=prec, preferred_element_type=F32)


def _bdot(a, b, dims=NN):
    return _dot(a.astype(BF16), b.astype(BF16), dims)


def _hdot(a, b, dims=NN):
    return _dot(a.astype(F32), b.astype(F32), dims, HI)


def _cp(*sem):
    return pltpu.CompilerParams(dimension_semantics=sem, vmem_limit_bytes=VMEM_LIMIT)


def _sigmoid(x):
    return 0.5 * jnp.tanh(0.5 * x) + 0.5


def _softplus(x):
    return jnp.maximum(x, 0.0) + jnp.log(1.0 + jnp.exp(-jnp.abs(x)))


def _rowsum(x):
    return jnp.sum(x, axis=-1, keepdims=True)


def _matmul(a, b, mode, out_dtype, tm, tn, tk, add=None, name="mm"):
    if mode == "nn":
        (m, k), (k2, n) = a.shape, b.shape
    elif mode == "nt":
        (m, k), (n, k2) = a.shape, b.shape
    else:
        (k, m), (k2, n) = a.shape, b.shape
    assert k == k2, (a.shape, b.shape, mode)
    tm, tn, tk = min(tm, m), min(tn, n), min(tk, k)
    assert m % tm == 0 and n % tn == 0 and k % tk == 0, (a.shape, b.shape, tm, tn, tk)
    nk = k // tk
    dims = {"nn": NN, "nt": NT, "tn": TN}[mode]

    def body(*refs):
        if add is None:
            a_ref, b_ref, o_ref, acc = refs
            add_ref = None
        else:
            a_ref, b_ref, add_ref, o_ref, acc = refs
        kk = pl.program_id(2)

        @pl.when(kk == 0)
        def _():
            acc[...] = jnp.zeros_like(acc)

        acc[...] += _bdot(a_ref[...], b_ref[...], dims)

        @pl.when(kk == nk - 1)
        def _():
            r = acc[...]
            if add_ref is not None:
                r = r + add_ref[...].astype(F32)
            o_ref[...] = r.astype(out_dtype)

    a_spec = (pl.BlockSpec((tk, tm), lambda i, j, kk: (kk, i)) if mode == "tn"
              else pl.BlockSpec((tm, tk), lambda i, j, kk: (i, kk)))
    b_spec = (pl.BlockSpec((tn, tk), lambda i, j, kk: (j, kk)) if mode == "nt"
              else pl.BlockSpec((tk, tn), lambda i, j, kk: (kk, j)))
    in_specs = [a_spec, b_spec]
    args = [a, b]
    if add is not None:
        in_specs.append(pl.BlockSpec((tm, tn), lambda i, j, kk: (i, j)))
        args.append(add)
    return pl.pallas_call(
        body, name=name, grid=(m // tm, n // tn, nk),
        in_specs=in_specs, out_specs=pl.BlockSpec((tm, tn), lambda i, j, kk: (i, j)),
        out_shape=jax.ShapeDtypeStruct((m, n), out_dtype),
        scratch_shapes=[pltpu.VMEM((tm, tn), F32)],
        compiler_params=_cp("parallel", "parallel", "arbitrary"),
    )(*args)


def _matmul_nt_segments(groups, out_dtype, tm, name, add=None, exch=None):
    m = groups[0][0][0].shape[0]
    n = groups[0][1].shape[0]
    assert m % tm == 0
    seg_list, b_list = [], []
    step = 0
    for gi, (segs, b, tk) in enumerate(groups):
        assert all(s.shape[1] % tk == 0 for s in segs) and sum(s.shape[1] for s in segs) == b.shape[1]
        g0 = step
        for s in segs:
            seg_list.append((s, tk, step, s.shape[1] // tk, gi))
            step += s.shape[1] // tk
        b_list.append((b, tk, g0, step - g0))
    nk = step
    ns, nb = len(seg_list), len(b_list)

    def body(*refs):
        seg_refs, b_refs = refs[:ns], refs[ns:ns + nb]
        add_ref = refs[ns + nb] if add is not None else None
        o_ref, acc = refs[-2], refs[-1]
        kk = pl.program_id(1)

        @pl.when(kk == 0)
        def _():
            acc[...] = jnp.zeros_like(acc)

        for a_ref, (_, _, k0, nk_s, gi) in zip(seg_refs, seg_list):
            @pl.when(jnp.logical_and(kk >= k0, kk < k0 + nk_s))
            def _(a_ref=a_ref, b_ref=b_refs[gi]):
                acc[...] += _bdot(a_ref[...], b_ref[...], NT)

        @pl.when(kk == nk - 1)
        def _():
            r = acc[...]
            if add_ref is not None:
                r = r + add_ref[...].astype(F32)
            o_ref[...] = r.astype(out_dtype)

    def walk(rows, tk, k0, nk_s, row_axis):
        if row_axis:
            return pl.BlockSpec((rows, tk), lambda i, kk: (i, jnp.clip(kk - k0, 0, nk_s - 1)))
        return pl.BlockSpec((rows, tk), lambda i, kk: (0, jnp.clip(kk - k0, 0, nk_s - 1)))

    row = pl.BlockSpec((tm, n), lambda i, kk: (i, 0))
    in_specs = [walk(tm, tk, k0, nk_s, True) for _, tk, k0, nk_s, _ in seg_list]
    in_specs += [walk(n, tk, k0, nk_s, False) for _, tk, k0, nk_s in b_list]
    args = [s[0] for s in seg_list] + [b[0] for b in b_list]
    if add is not None:
        in_specs.append(row)
        args.append(add)
    return _hosted_call(body, name=name, grid=(m // tm, nk), in_specs=in_specs, out_specs=[row],
                        out_shape=[jax.ShapeDtypeStruct((m, n), out_dtype)],
                        scratch_shapes=[pltpu.VMEM((tm, n), F32)], args=args,
                        dims=("parallel", "arbitrary"), exch=exch)


def _row_spec(tm, cols, cb=0):
    return pl.BlockSpec((tm, cols), lambda i, cb=cb: (i, cb))


def _bcast_spec(rows, cols):
    return pl.BlockSpec((rows, cols), lambda i: (0, 0))


def _rms_fwd(x, w, tm=512):
    t, d = x.shape

    def body(x_ref, w_ref, o_ref):
        xv = x_ref[...]
        r = lax.rsqrt(jnp.mean(xv * xv, axis=-1, keepdims=True) + EPS)
        o_ref[...] = (xv * r * w_ref[...]).astype(BF16)

    return pl.pallas_call(
        body, name="rms_fwd", grid=(t // tm,),
        in_specs=[_row_spec(tm, d), _bcast_spec(1, d)], out_specs=_row_spec(tm, d),
        out_shape=jax.ShapeDtypeStruct((t, d), BF16), compiler_params=_cp("parallel"),
    )(x, w)


def _rms_bwd(x, w, dy, resid, tm=512):
    t, d = x.shape

    def body(x_ref, w_ref, dy_ref, res_ref, dx_ref, dw_ref):
        xv = x_ref[...]
        r = lax.rsqrt(jnp.mean(xv * xv, axis=-1, keepdims=True) + EPS)
        xh = xv * r
        dyv = dy_ref[...].astype(F32)
        dxh = dyv * w_ref[...]
        dx_ref[...] = res_ref[...] + r * (dxh - xh * jnp.mean(dxh * xh, axis=-1, keepdims=True))

        @pl.when(pl.program_id(0) == 0)
        def _():
            dw_ref[...] = jnp.zeros_like(dw_ref)

        dw_ref[...] += jnp.sum(dyv * xh, axis=0, keepdims=True)

    return pl.pallas_call(
        body, name="rms_bwd", grid=(t // tm,),
        in_specs=[_row_spec(tm, d), _bcast_spec(1, d), _row_spec(tm, d), _row_spec(tm, d)],
        out_specs=[_row_spec(tm, d), _bcast_spec(1, d)],
        out_shape=[jax.ShapeDtypeStruct((t, d), F32), jax.ShapeDtypeStruct((1, d), F32)],
        compiler_params=_cp("arbitrary"),
    )(x, w, dy, resid)


def _final_loss(h, w, target, tm=512):
    t, d = h.shape

    def body(h_ref, w_ref, t_ref, loss_ref, dh_ref, dw_ref):
        hv = h_ref[...]
        r = lax.rsqrt(jnp.mean(hv * hv, axis=-1, keepdims=True) + EPS)
        xh = hv * r
        wv = w_ref[...]
        err = xh * wv - t_ref[...]
        dy = err * (1.0 / d)
        dxh = dy * wv
        dh_ref[...] = r * (dxh - xh * jnp.mean(dxh * xh, axis=-1, keepdims=True))

        @pl.when(pl.program_id(0) == 0)
        def _():
            dw_ref[...] = jnp.zeros_like(dw_ref)
            loss_ref[...] = jnp.zeros_like(loss_ref)

        dw_ref[...] += jnp.sum(dy * xh, axis=0, keepdims=True)
        part = 0.5 * jnp.sum(jnp.mean(err * err, axis=-1, keepdims=True), axis=0, keepdims=True)
        loss_ref[...] += part + jnp.zeros((1, HEAD), F32)

    return pl.pallas_call(
        body, name="final_loss", grid=(t // tm,),
        in_specs=[_row_spec(tm, d), _bcast_spec(1, d), _row_spec(tm, d)],
        out_specs=[_bcast_spec(1, HEAD), _row_spec(tm, d), _bcast_spec(1, d)],
        out_shape=[jax.ShapeDtypeStruct((1, HEAD), F32), jax.ShapeDtypeStruct((t, d), F32),
                   jax.ShapeDtypeStruct((1, d), F32)],
        compiler_params=_cp("arbitrary"),
    )(h, w, target)


def _ffn_in(hn, w_gate, w_up, tm=1024, tn=1408):
    t, d = hn.shape
    ff = w_gate.shape[1]

    def body(a_ref, wg_ref, wu_ref, g_ref, u_ref, act_ref):
        a = a_ref[...]
        g = _bdot(a, wg_ref[...])
        u = _bdot(a, wu_ref[...])
        g_ref[...] = g.astype(BF16)
        u_ref[...] = u.astype(BF16)
        gq = g.astype(BF16).astype(F32)
        act_ref[...] = (gq * _sigmoid(gq) * u.astype(BF16).astype(F32)).astype(BF16)

    tile = pl.BlockSpec((tm, tn), lambda i, j: (i, j))
    wspec = pl.BlockSpec((d, tn), lambda i, j: (0, j))
    return pl.pallas_call(
        body, name="ffn_in", grid=(t // tm, ff // tn),
        in_specs=[pl.BlockSpec((tm, d), lambda i, j: (i, 0)), wspec, wspec], out_specs=[tile] * 3,
        out_shape=[jax.ShapeDtypeStruct((t, ff), BF16)] * 3, compiler_params=_cp("parallel", "parallel"),
    )(hn, w_gate, w_up)


def _ffn_act_bwd(dh, w_down, g, u, tm=1024, tn=1408):
    t, d = dh.shape
    ff = w_down.shape[0]

    def body(a_ref, w_ref, g_ref, u_ref, dg_ref, du_ref):
        dv = _bdot(a_ref[...], w_ref[...], NT).astype(BF16).astype(F32)
        gv = g_ref[...].astype(F32)
        sg = _sigmoid(gv)
        dg_ref[...] = (dv * u_ref[...].astype(F32) * (sg + gv * sg * (1.0 - sg))).astype(BF16)
        du_ref[...] = (dv * gv * sg).astype(BF16)

    tile = pl.BlockSpec((tm, tn), lambda i, j: (i, j))
    return pl.pallas_call(
        body, name="ffn_act_bwd", grid=(t // tm, ff // tn),
        in_specs=[pl.BlockSpec((tm, d), lambda i, j: (i, 0)), pl.BlockSpec((tn, d), lambda i, j: (j, 0)),
                  tile, tile],
        out_specs=[tile] * 2, out_shape=[jax.ShapeDtypeStruct((t, ff), BF16)] * 2,
        compiler_params=_cp("parallel", "parallel"),
    )(dh, w_down, g, u)


def _gate_merge_fwd(pb, ya, yb, tm=512):
    t, d = ya.shape
    cb = PB_GATE // d

    def body(ga_ref, gb_ref, ya_ref, yb_ref, o_ref):
        o_ref[...] = (_sigmoid(ga_ref[...]) * ya_ref[...] + _sigmoid(gb_ref[...]) * yb_ref[...]).astype(BF16)

    return pl.pallas_call(
        body, name="gate_merge_fwd", grid=(t // tm,),
        in_specs=[_row_spec(tm, d, cb), _row_spec(tm, d, cb + 1), _row_spec(tm, d), _row_spec(tm, d)],
        out_specs=_row_spec(tm, d),
        out_shape=jax.ShapeDtypeStruct((t, d), BF16), compiler_params=_cp("parallel"),
    )(pb, pb, ya, yb)


def _gate_merge_bwd(pb, ya, yb, dm, tm=512):
    t, d = ya.shape
    cb = PB_GATE // d

    def body(ga_ref, gb_ref, ya_ref, yb_ref, dm_ref, dya_ref, dyb_ref, dga_ref, dgb_ref):
        dmv = dm_ref[...].astype(F32)
        sa = _sigmoid(ga_ref[...])
        sb = _sigmoid(gb_ref[...])
        dya_ref[...] = (dmv * sa).astype(BF16)
        dyb_ref[...] = (dmv * sb).astype(BF16)
        dga_ref[...] = (dmv * ya_ref[...] * sa * (1.0 - sa)).astype(BF16)
        dgb_ref[...] = (dmv * yb_ref[...] * sb * (1.0 - sb)).astype(BF16)

    return pl.pallas_call(
        body, name="gate_merge_bwd", grid=(t // tm,),
        in_specs=[_row_spec(tm, d, cb), _row_spec(tm, d, cb + 1), _row_spec(tm, d), _row_spec(tm, d),
                  _row_spec(tm, d)],
        out_specs=[_row_spec(tm, d)] * 4,
        out_shape=[jax.ShapeDtypeStruct((t, d), BF16)] * 4, compiler_params=_cp("parallel"),
    )(pb, pb, ya, yb, dm)


def _head_norm_fwd(o, pb, wn, tm=512):
    t, d = o.shape
    nh = d // HEAD

    def body(o_ref, z_ref, w_ref, out_ref):
        wv = w_ref[...]
        for h in range(nh):
            sl = slice(h * HEAD, (h + 1) * HEAD)
            ov = o_ref[:, sl]
            zv = z_ref[:, sl]
            r = lax.rsqrt(jnp.mean(ov * ov, axis=-1, keepdims=True) + EPS)
            out_ref[:, sl] = (ov * r * wv * (zv * _sigmoid(zv))).astype(BF16)

    return pl.pallas_call(
        body, name="head_norm_fwd", grid=(t // tm,),
        in_specs=[_row_spec(tm, d), _row_spec(tm, d, PB_Z // d), _bcast_spec(1, HEAD)],
        out_specs=_row_spec(tm, d),
        out_shape=jax.ShapeDtypeStruct((t, d), BF16), compiler_params=_cp("parallel"),
    )(o, pb, wn)


def _head_norm_bwd(o, pb, wn, dout, tm=512):
    t, d = o.shape
    nh = d // HEAD

    def body(o_ref, z_ref, w_ref, d_ref, do_ref, dz_ref, dw_ref):
        wv = w_ref[...]
        dw_acc = jnp.zeros((1, HEAD), F32)
        for h in range(nh):
            sl = slice(h * HEAD, (h + 1) * HEAD)
            ov = o_ref[:, sl]
            zv = z_ref[:, sl]
            dv = d_ref[:, sl].astype(F32)
            r = lax.rsqrt(jnp.mean(ov * ov, axis=-1, keepdims=True) + EPS)
            xh = ov * r
            sz = _sigmoid(zv)
            dn = dv * (zv * sz)
            dz_ref[:, sl] = (dv * xh * wv * (sz + zv * sz * (1.0 - sz))).astype(BF16)
            dxh = dn * wv
            do_ref[:, sl] = r * (dxh - xh * jnp.mean(dxh * xh, axis=-1, keepdims=True))
            dw_acc = dw_acc + jnp.sum(dn * xh, axis=0, keepdims=True)

        @pl.when(pl.program_id(0) == 0)
        def _():
            dw_ref[...] = jnp.zeros_like(dw_ref)

        dw_ref[...] += dw_acc

    return pl.pallas_call(
        body, name="head_norm_bwd", grid=(t // tm,),
        in_specs=[_row_spec(tm, d), _row_spec(tm, d, PB_Z // d), _bcast_spec(1, HEAD), _row_spec(tm, d)],
        out_specs=[_row_spec(tm, d), _row_spec(tm, d), _bcast_spec(1, HEAD)],
        out_shape=[jax.ShapeDtypeStruct((t, d), F32), jax.ShapeDtypeStruct((t, d), BF16),
                   jax.ShapeDtypeStruct((1, HEAD), F32)],
        compiler_params=_cp("arbitrary"),
    )(o, pb, wn, dout)


def _attn_bias(gi, hh, dil):
    i = lax.broadcasted_iota(jnp.int32, (BLOCK_A, BLOCK_A), 0)
    j = lax.broadcasted_iota(jnp.int32, (BLOCK_A, BLOCK_A), 1)
    hf = (gi * HEADS_PER_GROUP + hh + 1).astype(F32)
    slope = jnp.exp(jnp.full((1, BLOCK_A), -8.0 * math.log(2.0) / N_HEADS_A, F32) * hf) * float(dil)
    d_prev = (BLOCK_A + i - j).astype(F32)
    d_cur = (i - j).astype(F32)
    return -slope * d_prev, -slope * d_cur, j >= i, j <= i


ATTN_TOKENS = 2048


def _sub_rows(a, r, dil):
    start = a * BLOCK_A * dil + r
    return pl.ds(start, BLOCK_A) if dil == 1 else pl.ds(start, BLOCK_A, stride=dil)


def _attn_fwd_group(pa, gi, dil, tb=ATTN_TOKENS):
    t = pa.shape[0]
    tb = min(tb, t)
    hb = BLOCK_A * dil
    nb = tb // hb
    scale = HEAD ** -0.5

    def body(q_ref, k_ref, v_ref, kp_ref, vp_ref, o_ref, l_ref):
        hh = pl.program_id(0)
        step = pl.program_id(1)
        b_prev, b_cur, m_prev, m_cur = _attn_bias(gi, hh, dil)
        m_first = jnp.logical_and(m_prev, step > 0)
        for r in range(dil):
            kp, vp = kp_ref[_sub_rows(0, r, dil), :], vp_ref[_sub_rows(0, r, dil), :]
            for a in range(nb):
                rows = _sub_rows(a, r, dil)
                q, kc, vc = q_ref[rows, :], k_ref[rows, :], v_ref[rows, :]
                s_p = jnp.where(m_first if a == 0 else m_prev, _bdot(q, kp, NT) * scale + b_prev, -1e30)
                s_c = jnp.where(m_cur, _bdot(q, kc, NT) * scale + b_cur, -1e30)
                m = jnp.maximum(jnp.max(s_p, axis=-1, keepdims=True), jnp.max(s_c, axis=-1, keepdims=True))
                p_p = jnp.exp(s_p - m)
                p_c = jnp.exp(s_c - m)
                den = _rowsum(p_p) + _rowsum(p_c)
                o_ref[rows, :] = (_bdot(p_p, vp) + _bdot(p_c, vc)) / den
                l_ref[rows, :] = (m + jnp.log(den)) + jnp.zeros((BLOCK_A, HEAD), F32)
                kp, vp = kc, vc

    def col(base):
        return lambda hh, s: (s, base + hh)

    def col_prev(base):
        return lambda hh, s: (jnp.maximum(s * nb - 1, 0), base + hh)

    qb, kb, vb = gi * HEADS_PER_GROUP, N_HEADS_A + gi * HEADS_PER_GROUP, 2 * N_HEADS_A + gi * HEADS_PER_GROUP
    ospec = pl.BlockSpec((tb, HEAD), lambda hh, s: (s, hh))
    return pl.pallas_call(
        body, name=f"attn_fwd_g{gi}", grid=(HEADS_PER_GROUP, t // tb),
        in_specs=[pl.BlockSpec((tb, HEAD), col(qb)), pl.BlockSpec((tb, HEAD), col(kb)),
                  pl.BlockSpec((tb, HEAD), col(vb)),
                  pl.BlockSpec((hb, HEAD), col_prev(kb)), pl.BlockSpec((hb, HEAD), col_prev(vb))],
        out_specs=[ospec, ospec],
        out_shape=[jax.ShapeDtypeStruct((t, D_ATTN_OUT), F32)] * 2,
        compiler_params=_cp("parallel", "parallel"),
    )(pa, pa, pa, pa, pa)


def _attn_merge(os, ls, tm=512):
    t, d = os[0].shape

    def body(o0, o1, o2, l0, l1, l2, y_ref, lse_ref):
        a0, a1, a2 = l0[...], l1[...], l2[...]
        m = jnp.maximum(jnp.maximum(a0, a1), a2)
        e0, e1, e2 = jnp.exp(a0 - m), jnp.exp(a1 - m), jnp.exp(a2 - m)
        den = e0 + e1 + e2
        y_ref[...] = (e0 * o0[...] + e1 * o1[...] + e2 * o2[...]) / den
        lse_ref[...] = m + jnp.log(den)

    return pl.pallas_call(
        body, name="attn_merge", grid=(t // tm,),
        in_specs=[_row_spec(tm, d)] * 6, out_specs=[_row_spec(tm, d)] * 2,
        out_shape=[jax.ShapeDtypeStruct((t, d), F32)] * 2,
        compiler_params=_cp("parallel"),
    )(*os, *ls)


def _attn_bwd_group(pa, dy, y, lse, gi, dil, tb=ATTN_TOKENS):
    t = pa.shape[0]
    tb = min(tb, t)
    hb = BLOCK_A * dil
    nb = tb // hb
    nsteps = t // tb
    scale = HEAD ** -0.5

    def body(q_ref, k_ref, v_ref, dy_ref, y_ref, l_ref, kp_ref, vp_ref, d_ref,
             dq_s, dk_s, dv_s, carry_k, carry_v):
        hh = pl.program_id(0)
        step = pl.program_id(1)

        @pl.when(step == 0)
        def _():
            carry_k[...] = jnp.zeros_like(carry_k)
            carry_v[...] = jnp.zeros_like(carry_v)

        b_prev, b_cur, m_prev, m_cur = _attn_bias(gi, hh, dil)
        m_first = jnp.logical_and(m_prev, step < nsteps - 1)
        for r in range(dil):
            halo = _sub_rows(0, r, dil)
            dk_in, dv_in = carry_k[halo, :], carry_v[halo, :]
            kp, vp = kp_ref[halo, :], vp_ref[halo, :]
            prev_rows = None
            dk_pend = dv_pend = None
            for a in range(nb):
                rows = _sub_rows(a, r, dil)
                q, kc, vc = q_ref[rows, :], k_ref[rows, :], v_ref[rows, :]
                dyb, lb = dy_ref[rows, :], l_ref[rows, :]
                delta = _rowsum(dyb * y_ref[rows, :])
                mp = m_first if a == 0 else m_prev
                s = _bdot(q, kp, NT) * scale + b_prev
                p = jnp.where(mp, jnp.exp(jnp.where(mp, s - lb, 0.0)), 0.0)
                ds = p * (_bdot(dyb, vp, NT) - delta)
                dq = _bdot(ds, kp)
                dk_prev, dv_prev = _bdot(ds, q, TN), _bdot(p, dyb, TN)
                if a == 0:
                    carry_k[halo, :] = dk_prev
                    carry_v[halo, :] = dv_prev
                else:
                    dk_s[prev_rows, :] = dk_pend + dk_prev
                    dv_s[prev_rows, :] = dv_pend + dv_prev
                s = _bdot(q, kc, NT) * scale + b_cur
                p = jnp.where(m_cur, jnp.exp(jnp.where(m_cur, s - lb, 0.0)), 0.0)
                ds = p * (_bdot(dyb, vc, NT) - delta)
                dq_s[rows, :] = dq + _bdot(ds, kc)
                dk_pend, dv_pend = _bdot(ds, q, TN), _bdot(p, dyb, TN)
                prev_rows, kp, vp = rows, kc, vc
            dk_s[prev_rows, :] = dk_pend + dk_in
            dv_s[prev_rows, :] = dv_pend + dv_in
        d_ref[:, :HEAD] = (dq_s[...] * scale).astype(BF16)
        d_ref[:, HEAD:2 * HEAD] = (dk_s[...] * scale).astype(BF16)
        d_ref[:, 2 * HEAD:] = dv_s[...].astype(BF16)

    def col(base):
        return lambda hh, s: (nsteps - 1 - s, base + hh)

    def col_prev(base):
        return lambda hh, s: (jnp.maximum((nsteps - 1 - s) * nb - 1, 0), base + hh)

    qb, kb, vb = gi * HEADS_PER_GROUP, N_HEADS_A + gi * HEADS_PER_GROUP, 2 * N_HEADS_A + gi * HEADS_PER_GROUP
    big, small = (tb, HEAD), (hb, HEAD)
    return pl.pallas_call(
        body, name=f"attn_bwd_g{gi}", grid=(HEADS_PER_GROUP, nsteps),
        in_specs=[pl.BlockSpec(big, col(qb)), pl.BlockSpec(big, col(kb)), pl.BlockSpec(big, col(vb)),
                  pl.BlockSpec(big, col(0)), pl.BlockSpec(big, col(0)), pl.BlockSpec(big, col(0)),
                  pl.BlockSpec(small, col_prev(kb)), pl.BlockSpec(small, col_prev(vb))],
        out_specs=pl.BlockSpec((tb, 3 * HEAD), col(0)),
        out_shape=jax.ShapeDtypeStruct((t, 3 * D_ATTN_OUT), BF16),
        scratch_shapes=[pltpu.VMEM(big, F32)] * 3 + [pltpu.VMEM(small, F32)] * 2,
        compiler_params=_cp("parallel", "arbitrary"),
    )(pa, pa, pa, dy, y, lse, pa, pa)


def _shift_down(cur, prev8, s):
    if s == 0:
        return cur
    rolled = pltpu.roll(cur, s, 0)
    prolled = pltpu.roll(prev8, s, 0)
    rid = lax.broadcasted_iota(jnp.int32, prev8.shape, 0)
    top = jnp.where(rid < s, prolled, rolled[:8])
    return jnp.concatenate([top, rolled[8:]], axis=0)


def _shift_up(cur, next8, s):
    if s == 0:
        return cur
    n = cur.shape[0]
    rolled = pltpu.roll(cur, n - s, 0)
    nrolled = pltpu.roll(next8, 8 - s, 0)
    rid = lax.broadcasted_iota(jnp.int32, next8.shape, 0)
    bottom = jnp.where(rid >= 8 - s, nrolled, rolled[n - 8:])
    return jnp.concatenate([rolled[:n - 8], bottom], axis=0)


def _conv(xv, prev8, wv):
    c = jnp.zeros_like(xv)
    shifted = []
    for s in range(CONV_WIDTH):
        xs = _shift_down(xv, prev8, s)
        shifted.append(xs)
        c = c + wv[CONV_WIDTH - 1 - s:CONV_WIDTH - s, :] * xs
    return c, shifted


def _head_expand(psv, first):
    tm = psv.shape[0]
    return jnp.concatenate([jnp.broadcast_to(psv[:, first + h:first + h + 1], (tm, HEAD))
                            for h in range(N_HEADS_B)], axis=1)


def _head_collect(x, first):
    lane = lax.broadcasted_iota(jnp.int32, (x.shape[0], HEAD), 1)
    out = jnp.zeros((x.shape[0], HEAD), F32)
    for h in range(N_HEADS_B):
        out = jnp.where(lane == first + h, x[:, h * HEAD:(h + 1) * HEAD], out)
    return out


def _dn_prep_fwd(pb, ps, conv_w, a_log_bc, dt_bias_bc, tm=256, exch=None):
    t = pb.shape[0]
    c3 = 3 * D_B
    r8 = tm // 8

    def body(x_ref, xp_ref, ps_ref, w_ref, al_ref, dt_ref, q_ref, k_ref, v_ref, g_ref, beta_ref):
        first = pl.program_id(0) > 0

        def silu_conv(cols):
            c, _ = _conv(x_ref[:, cols], jnp.where(first, xp_ref[:, cols], 0.0), w_ref[:, cols])
            return c * _sigmoid(c)

        for h in range(N_HEADS_B):
            sl = slice(h * HEAD, (h + 1) * HEAD)
            sq = silu_conv(sl)
            q_ref[:, sl] = sq * lax.rsqrt(_rowsum(sq * sq) + EPS) * (HEAD ** -0.5)
            sk = silu_conv(slice(D_B + h * HEAD, D_B + (h + 1) * HEAD))
            k_ref[:, sl] = sk * lax.rsqrt(_rowsum(sk * sk) + EPS)
            v_ref[:, sl] = silu_conv(slice(2 * D_B + h * HEAD, 2 * D_B + (h + 1) * HEAD))
        psv = ps_ref[...]
        beta_ref[...] = _sigmoid(_head_expand(psv, 0))
        g_ref[...] = -jnp.exp(al_ref[...]) * _softplus(_head_expand(psv, N_HEADS_B) + dt_ref[...])

    return _hosted_call(
        body, name="dn_prep_fwd", grid=(t // tm,),
        in_specs=[_row_spec(tm, c3, 0),
                  pl.BlockSpec((8, c3), lambda i: (jnp.maximum(i * r8 - 1, 0), 0)),
                  _row_spec(tm, HEAD),
                  _bcast_spec(CONV_WIDTH, c3), _bcast_spec(1, D_B), _bcast_spec(1, D_B)],
        out_specs=[_row_spec(tm, D_B)] * 5,
        out_shape=[jax.ShapeDtypeStruct((t, D_B), F32)] * 5, scratch_shapes=[],
        args=(pb, pb, ps, conv_w, a_log_bc, dt_bias_bc), dims=("parallel",), exch=exch)


def _dn_prep_bwd(pb, ps, conv_w, a_log_bc, dt_bias_bc, g, dq, dk, dv, dg, dbeta, tm=128, exch=None):
    t = pb.shape[0]
    c3 = 3 * D_B
    r8 = tm // 8

    def body(x_ref, xp_ref, ps_ref, w_ref, al_ref, dt_ref, g_ref, dq_ref, dk_ref, dv_ref, dg_ref, db_ref,
             dc_ref, dps_ref, dw_ref, dal_ref, ddt_ref):
        first = pl.program_id(0) > 0

        @pl.when(pl.program_id(0) == 0)
        def _():
            dw_ref[...] = jnp.zeros_like(dw_ref)
            dal_ref[...] = jnp.zeros_like(dal_ref)
            ddt_ref[...] = jnp.zeros_like(ddt_ref)

        def column_block(cols, d_ref, sl, mult, normed):
            c, shifted = _conv(x_ref[:, cols], jnp.where(first, xp_ref[:, cols], 0.0), w_ref[:, cols])
            sg = _sigmoid(c)
            dsilu = sg + c * sg * (1.0 - sg)
            dyv = d_ref[:, sl]
            if normed:
                sv = c * sg
                r = lax.rsqrt(_rowsum(sv * sv) + EPS)
                yh = sv * r
                dyv = dyv * mult
                dyv = r * (dyv - yh * _rowsum(dyv * yh))
            dcv = dyv * dsilu
            dc_ref[:, cols] = dcv
            for sft in range(CONV_WIDTH):
                j = CONV_WIDTH - 1 - sft
                dw_ref[j:j + 1, cols] += jnp.sum(dcv * shifted[sft], axis=0, keepdims=True)

        for h in range(N_HEADS_B):
            sl = slice(h * HEAD, (h + 1) * HEAD)
            column_block(sl, dq_ref, sl, HEAD ** -0.5, True)
            column_block(slice(D_B + h * HEAD, D_B + (h + 1) * HEAD), dk_ref, sl, 1.0, True)
            column_block(slice(2 * D_B + h * HEAD, 2 * D_B + (h + 1) * HEAD), dv_ref, sl, 1.0, False)
        psv = ps_ref[...]
        beta = _sigmoid(_head_expand(psv, 0))
        dgv = dg_ref[...]
        da = dgv * (-jnp.exp(al_ref[...])) * _sigmoid(_head_expand(psv, N_HEADS_B) + dt_ref[...])
        dps_ref[...] = _head_collect(db_ref[...] * beta * (1.0 - beta), 0) + _head_collect(da, N_HEADS_B)
        dal_ref[...] += jnp.sum(dgv * g_ref[...], axis=0, keepdims=True)
        ddt_ref[...] += jnp.sum(da, axis=0, keepdims=True)

    row = _row_spec(tm, D_B)
    return _hosted_call(
        body, name="dn_prep_bwd", grid=(t // tm,),
        in_specs=[_row_spec(tm, c3, 0),
                  pl.BlockSpec((8, c3), lambda i: (jnp.maximum(i * r8 - 1, 0), 0)),
                  _row_spec(tm, HEAD),
                  _bcast_spec(CONV_WIDTH, c3), _bcast_spec(1, D_B), _bcast_spec(1, D_B),
                  row, row, row, row, row, row],
        out_specs=[_row_spec(tm, c3), _row_spec(tm, HEAD), _bcast_spec(CONV_WIDTH, c3), _bcast_spec(1, D_B),
                   _bcast_spec(1, D_B)],
        out_shape=[jax.ShapeDtypeStruct((t, c3), F32), jax.ShapeDtypeStruct((t, HEAD), F32),
                   jax.ShapeDtypeStruct((CONV_WIDTH, c3), F32),
                   jax.ShapeDtypeStruct((1, D_B), F32), jax.ShapeDtypeStruct((1, D_B), F32)],
        scratch_shapes=[], args=(pb, pb, ps, conv_w, a_log_bc, dt_bias_bc, g, dq, dk, dv, dg, dbeta),
        dims=("arbitrary",), exch=exch)


def _conv_bwd_input(dc, conv_w, tm=256):
    t, c3 = dc.shape
    r8 = tm // 8
    nlast = t // 8 - 1
    nsteps = t // tm

    def body(d_ref, dn_ref, w_ref, o_ref):
        not_last = pl.program_id(0) < nsteps - 1
        for cb in range(c3 // HEAD):
            cols = slice(cb * HEAD, (cb + 1) * HEAD)
            next8 = jnp.where(not_last, dn_ref[:, cols], 0.0)
            dv = d_ref[:, cols]
            wv = w_ref[:, cols]
            acc = jnp.zeros_like(dv)
            for s in range(CONV_WIDTH):
                acc = acc + wv[CONV_WIDTH - 1 - s:CONV_WIDTH - s, :] * _shift_up(dv, next8, s)
            o_ref[:, cols] = acc.astype(BF16)

    return pl.pallas_call(
        body, name="conv_bwd_input", grid=(nsteps,),
        in_specs=[_row_spec(tm, c3), pl.BlockSpec((8, c3), lambda i: (jnp.minimum((i + 1) * r8, nlast), 0)),
                  _bcast_spec(CONV_WIDTH, c3)],
        out_specs=_row_spec(tm, c3),
        out_shape=jax.ShapeDtypeStruct((t, c3), BF16), compiler_params=_cp("parallel"),
    )(dc, dc, conv_w)


def _lanes(x):
    return x[:, :CH]


def _tri_inv(a_list, r, c):
    eye = (r == c).astype(F32)
    b16 = (r >> 4) == (c >> 4)
    b32 = (r >> 5) == (c >> 5)
    ns = [jnp.where(b16, -a, 0.0) for a in a_list]
    xs = [eye + n for n in ns]
    ps = [_bdot(n, n) for n in ns]
    for last in (False, False, True):
        xs = [x + _bdot(x, p) for x, p in zip(xs, ps)]
        if not last:
            ps = [_bdot(p, p) for p in ps]
    for mask in (jnp.logical_and(b32, jnp.logical_not(b16)), jnp.logical_not(b32)):
        ts = [_bdot(x, jnp.where(mask, a, 0.0)) for x, a in zip(xs, a_list)]
        xs = [x - _bdot(t, x) for x, t in zip(xs, ts)]
    return xs


def _chunk_local(qs, ks, vs, gs, betas, solved=None):
    r = lax.broadcasted_iota(jnp.int32, (CH, CH), 0)
    c = lax.broadcasted_iota(jnp.int32, (CH, CH), 1)
    incl, strict = r >= c, r > c
    lm = incl.astype(F32)
    cums = [_hdot(lm, jnp.concatenate([g, jnp.where(strict, _lanes(g), 0.0)], axis=1)) for g in gs]
    gcbs = [cm[:, :HEAD] for cm in cums]
    decays = [jnp.where(incl, jnp.exp(jnp.where(incl, cm[:, HEAD:], 0.0)), 0.0) for cm in cums]
    bcols = [_lanes(b) for b in betas]
    kks = [_bdot(k, k, NT) for k in ks]
    qkraws = [_bdot(q, k, NT) for q, k in zip(qs, ks)]
    egs = [jnp.exp(gcb) for gcb in gcbs]
    if solved is None:
        tms = _tri_inv([jnp.where(strict, bc * kk * dc, 0.0) for bc, kk, dc in zip(bcols, kks, decays)], r, c)
        sols = [_hdot(tm, jnp.concatenate([b * v, b * eg * k], axis=1))
                for tm, b, v, eg, k in zip(tms, betas, vs, egs, ks)]
        ubars, ws = [sol[:, :HEAD] for sol in sols], [sol[:, HEAD:] for sol in sols]
    else:
        tms, ubars, ws = solved
    gls = [gcb[CH - 1:CH, :] for gcb in gcbs]
    eks = [jnp.exp(gl - gcb) for gl, gcb in zip(gls, gcbs)]
    return [dict(incl=incl, strict=strict, r=r, c=c, decay=dc, bcol=bc, kk=kk, tm=tm, eg=eg,
                 u_bar=ub, w=w, qkraw=qkraw, gl=gl, ek=ek)
            for dc, bc, kk, tm, eg, ub, w, qkraw, gl, ek
            in zip(decays, bcols, kks, tms, egs, ubars, ws, qkraws, gls, eks)]


def _dn_chunk_fwd(q, k, v, g, beta, cps=CHUNK_GROUP):
    t = q.shape[0]
    tm = cps * CH

    def body(q_ref, k_ref, v_ref, g_ref, b_ref, ub_ref, w_ref, qd_ref, kd_ref, qk_ref, ti_ref, gl_ref):
        for base in range(0, cps, CHUNK_GROUP):
            sls = [slice((base + j) * CH, (base + j + 1) * CH) for j in range(CHUNK_GROUP)]
            qs, ks = [q_ref[sl, :] for sl in sls], [k_ref[sl, :] for sl in sls]
            locs = _chunk_local(qs, ks, [v_ref[sl, :] for sl in sls], [g_ref[sl, :] for sl in sls],
                                [b_ref[sl, :] for sl in sls])
            for j, (sl, qv, kv, loc) in enumerate(zip(sls, qs, ks, locs)):
                ub_ref[sl, :] = loc["u_bar"]
                w_ref[sl, :] = loc["w"]
                qd_ref[sl, :] = qv * loc["eg"]
                kd_ref[sl, :] = kv * loc["ek"]
                qk_ref[sl, :] = loc["qkraw"] * loc["decay"]
                ti_ref[sl, :] = loc["tm"]
                gl_ref[base + j:base + j + 1, :] = jnp.exp(loc["gl"])

    hspec = pl.BlockSpec((tm, HEAD), lambda h, i: (i, h))
    sq_spec = pl.BlockSpec((None, tm, CH), lambda h, i: (h, i, 0))
    sq_shape = jax.ShapeDtypeStruct((N_HEADS_B, t, CH), F32)
    return pl.pallas_call(
        body, name="dn_chunk_fwd", grid=(N_HEADS_B, t // tm),
        in_specs=[hspec] * 5,
        out_specs=[hspec] * 4 + [sq_spec, sq_spec, pl.BlockSpec((cps, HEAD), lambda h, i: (i, h))],
        out_shape=[jax.ShapeDtypeStruct((t, D_B), F32)] * 4
        + [sq_shape, sq_shape, jax.ShapeDtypeStruct((t // CH, D_B), F32)],
        compiler_params=_cp("parallel", "parallel"),
    )(q, k, v, g, beta)


def _dn_scan_fwd(ub, w, qd, kd, qk, gl, cps=8, hg=2 * SCAN_HEADS):
    t = ub.shape[0]
    tm = cps * CH
    hs = list(range(hg))

    def body(ub_ref, w_ref, qd_ref, kd_ref, qk_ref, gl_ref, o_ref, st_ref, s_acc):
        @pl.when(pl.program_id(1) == 0)
        def _():
            s_acc[...] = jnp.zeros_like(s_acc)

        for ci in range(cps):
            sl = slice(ci * CH, (ci + 1) * CH)
            cols = [slice(h * HEAD, (h + 1) * HEAD) for h in hs]
            svs = [s_acc[h] for h in hs]
            for h in hs:
                st_ref[h, ci * HEAD:(ci + 1) * HEAD, :] = svs[h]
            us = [ub_ref[sl, cols[h]] - _bdot(w_ref[sl, cols[h]], svs[h]) for h in hs]
            for h in hs:
                s_acc[h] = gl_ref[ci:ci + 1, cols[h]] * svs[h] + _bdot(kd_ref[sl, cols[h]], us[h], TN)
            for h in hs:
                o_ref[sl, cols[h]] = _bdot(qd_ref[sl, cols[h]], svs[h]) + _bdot(qk_ref[h, sl, :], us[h])

    hspec = pl.BlockSpec((tm, hg * HEAD), lambda h, i: (i, h))
    return pl.pallas_call(
        body, name="dn_scan_fwd", grid=(N_HEADS_B // hg, t // tm),
        in_specs=[hspec] * 4 + [pl.BlockSpec((hg, tm, CH), lambda h, i: (h, i, 0)),
                                pl.BlockSpec((cps, hg * HEAD), lambda h, i: (i, h))],
        out_specs=[hspec, pl.BlockSpec((hg, cps * HEAD, HEAD), lambda h, i: (h, i, 0))],
        out_shape=[jax.ShapeDtypeStruct((t, D_B), F32),
                   jax.ShapeDtypeStruct((N_HEADS_B, (t // CH) * HEAD, HEAD), F32)],
        scratch_shapes=[pltpu.VMEM((hg, HEAD, HEAD), F32)],
        compiler_params=_cp("parallel", "arbitrary"),
    )(ub, w, qd, kd, qk, gl)


def _dn_scan_bwd(ub, w, qd, kd, qk, gl, st, do, cps=8):
    t = ub.shape[0]
    tm = cps * CH
    ns = t // tm

    hg = SCAN_HEADS
    hs = list(range(hg))

    def body(ub_ref, w_ref, qd_ref, kd_ref, qk_ref, gl_ref, st_ref, do_ref,
             dub_ref, dw_ref, dqd_ref, dkd_ref, dqk_ref, dgl_ref, ds_acc):
        @pl.when(pl.program_id(1) == 0)
        def _():
            ds_acc[...] = jnp.zeros_like(ds_acc)

        for ci in reversed(range(cps)):
            sl = slice(ci * CH, (ci + 1) * CH)
            cols = [slice(h * HEAD, (h + 1) * HEAD) for h in hs]
            svs = [st_ref[h, ci * HEAD:(ci + 1) * HEAD, :] for h in hs]
            wvs = [w_ref[sl, cols[h]] for h in hs]
            dovs = [do_ref[sl, cols[h]] for h in hs]
            dsvs = [ds_acc[h] for h in hs]
            us = [ub_ref[sl, cols[h]] - _bdot(wvs[h], svs[h]) for h in hs]
            dus = [_bdot(kd_ref[sl, cols[h]], dsvs[h]) + _bdot(qk_ref[h, sl, :], dovs[h], TN) for h in hs]
            for h in hs:
                ds_acc[h] = (gl_ref[ci:ci + 1, cols[h]] * dsvs[h] + _bdot(qd_ref[sl, cols[h]], dovs[h], TN)
                             - _bdot(wvs[h], dus[h], TN))
            for h in hs:
                dgl_ref[ci:ci + 1, cols[h]] = (jnp.sum(_rowsum(dsvs[h] * svs[h]), axis=0, keepdims=True)
                                              + jnp.zeros((1, HEAD), F32))
                dkd_ref[sl, cols[h]] = _bdot(us[h], dsvs[h], NT)
                dqd_ref[sl, cols[h]] = _bdot(dovs[h], svs[h], NT)
                dqk_ref[h, sl, :] = _bdot(dovs[h], us[h], NT)
                dub_ref[sl, cols[h]] = dus[h]
                dw_ref[sl, cols[h]] = -_bdot(dus[h], svs[h], NT)

    hspec = pl.BlockSpec((tm, hg * HEAD), lambda h, i: (ns - 1 - i, h))
    qkspec = pl.BlockSpec((hg, tm, CH), lambda h, i: (h, ns - 1 - i, 0))
    glspec = pl.BlockSpec((cps, hg * HEAD), lambda h, i: (ns - 1 - i, h))
    return pl.pallas_call(
        body, name="dn_scan_bwd", grid=(N_HEADS_B // hg, ns),
        in_specs=[hspec] * 4 + [qkspec, glspec,
                                pl.BlockSpec((hg, cps * HEAD, HEAD), lambda h, i: (h, ns - 1 - i, 0)), hspec],
        out_specs=[hspec] * 4 + [qkspec, glspec],
        out_shape=[jax.ShapeDtypeStruct((t, D_B), F32)] * 4
        + [jax.ShapeDtypeStruct((N_HEADS_B, t, CH), F32), jax.ShapeDtypeStruct((t // CH, D_B), F32)],
        scratch_shapes=[pltpu.VMEM((hg, HEAD, HEAD), F32)],
        compiler_params=_cp("parallel", "arbitrary"),
    )(ub, w, qd, kd, qk, gl, st, do)


def _dn_chunk_bwd(q, k, v, g, beta, ub, w, tinv, dub, dw, dqd, dkd, dqk, dgl, cps=CHUNK_GROUP):
    t = q.shape[0]
    tm = cps * CH

    def body(q_ref, k_ref, v_ref, g_ref, b_ref, ub_ref, w_ref, dub_ref, dw_ref, dqd_ref, dkd_ref, ti_ref, dqk_ref,
             dgl_ref, dq_ref, dk_ref, dv_ref, dg_ref, db_ref):
        ones = jnp.ones((CH, HEAD), F32)
        rid = lax.broadcasted_iota(jnp.int32, (CH, HEAD), 0)

        def rest(ci, sl, qv, kv, vv, beta_v, loc, dr, da):
            incl = loc["incl"]
            eg, ek, decay, bcol, kk = loc["eg"], loc["ek"], loc["decay"], loc["bcol"], loc["kk"]
            drv, drk = dr[:, :HEAD], dr[:, HEAD:]
            dv_ref[sl, :] = beta_v * drv
            beg = beta_v * eg
            t1 = drk * kv
            dbeta = _rowsum(drv * vv + t1 * eg) + _rowsum(da * kk * decay)
            dkk = da * bcol * decay
            dqk_m = jnp.where(incl, dqk_ref[sl, :], 0.0)
            ddecay = da * bcol * kk + dqk_m * loc["qkraw"]
            dqkraw = dqk_m * decay
            dqdv, dkdv = dqd_ref[sl, :], dkd_ref[sl, :]
            dq_ref[sl, :] = _bdot(dqkraw, kv) + dqdv * eg
            dk_ref[sl, :] = (beg * drk + _bdot(dqkraw, qv, TN) + _bdot(dkk, kv) + _bdot(dkk, kv, TN)
                             + dkdv * ek)
            e = ddecay * decay
            skd = _rowsum(dkdv * kv * ek)
            dgc = _rowsum(beg * t1) + _rowsum(e) + _rowsum(dqdv * qv * eg) - skd
            colsum = _hdot(e, ones, TN)
            last = jnp.sum(skd, axis=0, keepdims=True) + dgl_ref[ci:ci + 1, :] * jnp.exp(loc["gl"])
            db_ref[sl, :] = dbeta + jnp.zeros((CH, HEAD), F32)
            return (dgc - colsum) + jnp.where(rid == CH - 1, last, 0.0)

        for base in range(0, cps, CHUNK_GROUP):
            cis = list(range(base, base + CHUNK_GROUP))
            sls = [slice(ci * CH, (ci + 1) * CH) for ci in cis]
            qs, ks, vs = [q_ref[sl, :] for sl in sls], [k_ref[sl, :] for sl in sls], [v_ref[sl, :] for sl in sls]
            betas = [b_ref[sl, :] for sl in sls]
            locs = _chunk_local(qs, ks, vs, [g_ref[sl, :] for sl in sls], betas,
                                solved=([ti_ref[sl, :] for sl in sls], [ub_ref[sl, :] for sl in sls],
                                        [w_ref[sl, :] for sl in sls]))
            drs = [_hdot(loc["tm"], jnp.concatenate([dub_ref[sl, :], dw_ref[sl, :]], axis=1), TN)
                   for loc, sl in zip(locs, sls)]
            das = [jnp.where(loc["strict"],
                             -_hdot(dr, jnp.concatenate([loc["u_bar"], loc["w"]], axis=1), NT), 0.0)
                   for loc, dr in zip(locs, drs)]
            dgcs = [rest(*args) for args in zip(cis, sls, qs, ks, vs, betas, locs, drs, das)]
            um = (locs[0]["r"] <= locs[0]["c"]).astype(F32)
            for sl, dgc_bc in zip(sls, dgcs):
                dg_ref[sl, :] = _hdot(um, dgc_bc)

    hspec = pl.BlockSpec((tm, HEAD), lambda h, i: (i, h))
    sq_spec = pl.BlockSpec((None, tm, CH), lambda h, i: (h, i, 0))
    return pl.pallas_call(
        body, name="dn_chunk_bwd", grid=(N_HEADS_B, t // tm),
        in_specs=[hspec] * 11 + [sq_spec, sq_spec, pl.BlockSpec((cps, HEAD), lambda h, i: (i, h))],
        out_specs=[hspec] * 5,
        out_shape=[jax.ShapeDtypeStruct((t, D_B), F32)] * 5,
        compiler_params=_cp("parallel", "parallel"),
    )(q, k, v, g, beta, ub, w, dub, dw, dqd, dkd, tinv, dqk, dgl)


FLIPS = [(fx, fy, fc) for fx in (0, 1) for fy in (0, 1) for fc in (0, 1)][1:]


def _mesh_pos():
    return lax.axis_index("x"), lax.axis_index("y"), lax.axis_index("c")


def _peer(pos, flip):
    return tuple((1 - p) if f else p for p, f in zip(pos, flip))


def _dev_index(pos):
    return 4 * pos[0] + 2 * pos[1] + pos[2]


class _Exchange:
    def __init__(self, tensors, scatter):
        self.tensors, self.scatter, self.nt = list(tensors), list(scatter), len(tensors)
        hbm = pl.BlockSpec(memory_space=pltpu.HBM)
        self.in_specs = [hbm] * self.nt
        self.out_specs = [hbm] * self.nt
        self.out_shape = [jax.ShapeDtypeStruct(x.shape if sc else (N_DEV,) + x.shape, x.dtype)
                          for x, sc in zip(tensors, scatter)]
        self.scratch_shapes = [pltpu.SemaphoreType.DMA((self.nt * 7,)), pltpu.SemaphoreType.DMA((self.nt * 7,)),
                               pltpu.SemaphoreType.DMA((self.nt,))]

    def _copies(self, ins, outs, sems):
        send_sems, recv_sems, local_sems = sems
        pos = _mesh_pos()
        me = _dev_index(pos)

        def remote(ti, fi, landing):
            peer = _peer(pos, FLIPS[fi])
            src = ins[ti].at[_dev_index(peer)] if self.scatter[ti] else ins[ti]
            return pltpu.make_async_remote_copy(
                src_ref=src, dst_ref=outs[ti].at[landing(peer)],
                send_sem=send_sems.at[ti * 7 + fi], recv_sem=recv_sems.at[ti * 7 + fi],
                device_id=peer, device_id_type=pl.DeviceIdType.MESH)

        pairs = [(ti, fi) for ti in range(self.nt) for fi in range(7)]
        local = [pltpu.make_async_copy(ins[ti].at[me] if self.scatter[ti] else ins[ti], outs[ti].at[me],
                                       local_sems.at[ti]) for ti in range(self.nt)]
        sends = [remote(ti, fi, lambda peer: me) for ti, fi in pairs]
        recvs = [remote(ti, fi, _dev_index) for ti, fi in pairs]
        return local, sends, recvs

    def start(self, ins, outs, sems):
        local, sends, _ = self._copies(ins, outs, sems)
        for cp in local + sends:
            cp.start()

    def wait(self, ins, outs, sems):
        local, sends, recvs = self._copies(ins, outs, sems)
        for cp in recvs:
            cp.wait_recv()
        for cp in sends:
            cp.wait_send()
        for cp in local:
            cp.wait()


def _exchange(tensors, scatter, name):
    ex = _Exchange(tensors, scatter)

    def body(*refs):
        ins, outs, sems = refs[:ex.nt], refs[ex.nt:2 * ex.nt], refs[2 * ex.nt:]
        ex.start(ins, outs, sems)
        ex.wait(ins, outs, sems)

    return pl.pallas_call(
        body, name=name, in_specs=ex.in_specs, out_specs=ex.out_specs, out_shape=ex.out_shape,
        scratch_shapes=ex.scratch_shapes, compiler_params=pltpu.CompilerParams(has_side_effects=True),
    )(*tensors)


def _gather_two_level(tensors, name):
    nt = len(tensors)
    hbm = pl.BlockSpec(memory_space=pltpu.HBM)

    def body(*refs):
        ins, outs = refs[:nt], refs[nt:2 * nt]
        send_sems, recv_sems, local_sems = refs[2 * nt:]
        x, y, c = _mesh_pos()
        sibling = (x, y, 1 - c)
        chips = [(1 - x, y), (x, 1 - y), (1 - x, 1 - y)]

        def copy(ti, k, block, to, own=False):
            slot = outs[ti].at[_dev_index(block)]
            return pltpu.make_async_remote_copy(
                src_ref=ins[ti] if own else slot, dst_ref=slot,
                send_sem=send_sems.at[ti * 7 + k], recv_sem=recv_sems.at[ti * 7 + k],
                device_id=to, device_id_type=pl.DeviceIdType.MESH)

        me = (x, y, c)
        mine = [pltpu.make_async_copy(ins[ti], outs[ti].at[_dev_index(me)], local_sems.at[ti]) for ti in range(nt)]
        first = [copy(ti, 0, me, sibling, own=True) for ti in range(nt)]
        first += [copy(ti, 1 + j, me, (*chip, c), own=True) for ti in range(nt) for j, chip in enumerate(chips)]
        for cp in mine + first:
            cp.start()
        passed = []
        for j, chip in enumerate(chips):
            for ti in range(nt):
                copy(ti, 1 + j, (*chip, c), me).wait_recv()
                cp = copy(ti, 4 + j, (*chip, c), sibling)
                cp.start()
                passed.append(cp)
        for ti in range(nt):
            copy(ti, 0, sibling, me).wait_recv()
            for j, chip in enumerate(chips):
                copy(ti, 4 + j, (*chip, 1 - c), me).wait_recv()
        for cp in first + passed:
            cp.wait_send()
        for cp in mine:
            cp.wait()

    return pl.pallas_call(
        body, name=name, in_specs=[hbm] * nt, out_specs=[hbm] * nt,
        out_shape=[jax.ShapeDtypeStruct((N_DEV,) + x.shape, x.dtype) for x in tensors],
        scratch_shapes=[pltpu.SemaphoreType.DMA((nt * 7,)), pltpu.SemaphoreType.DMA((nt * 7,)),
                        pltpu.SemaphoreType.DMA((nt,))],
        compiler_params=pltpu.CompilerParams(has_side_effects=True),
    )(*tensors)


def _hosted_call(body, *, name, grid, in_specs, out_specs, out_shape, scratch_shapes, args, dims, exch=None):
    if exch is None:
        return pl.pallas_call(body, name=name, grid=grid, in_specs=in_specs, out_specs=out_specs,
                              out_shape=out_shape, scratch_shapes=scratch_shapes,
                              compiler_params=_cp(*dims))(*args)
    n_in, n_out, n_sc, ne = len(in_specs), len(out_specs), len(scratch_shapes), exch.nt
    nsteps = math.prod(grid)

    def wrapped(*refs):
        ins, ex_in = refs[:n_in], refs[n_in:n_in + ne]
        outs = refs[n_in + ne:n_in + ne + n_out]
        ex_out = refs[n_in + ne + n_out:n_in + 2 * ne + n_out]
        rest = refs[n_in + 2 * ne + n_out:]
        scratch, sems = rest[:n_sc], rest[n_sc:]
        step = pl.program_id(0)
        for ax in range(1, len(grid)):
            step = step * grid[ax] + pl.program_id(ax)

        @pl.when(step == 0)
        def _():
            exch.start(ex_in, ex_out, sems)

        body(*ins, *outs, *scratch)

        @pl.when(step == nsteps - 1)
        def _():
            exch.wait(ex_in, ex_out, sems)

    return pl.pallas_call(
        wrapped, name=name, grid=grid, in_specs=list(in_specs) + exch.in_specs,
        out_specs=list(out_specs) + exch.out_specs, out_shape=list(out_shape) + exch.out_shape,
        scratch_shapes=list(scratch_shapes) + exch.scratch_shapes,
        compiler_params=pltpu.CompilerParams(dimension_semantics=("arbitrary",) * len(grid),
                                             vmem_limit_bytes=VMEM_LIMIT, has_side_effects=True),
    )(*args, *exch.tensors)


def _adamw(land, w, m, v, name, tm=256):
    n, r, c = land.shape
    tm = r if r <= tm else max(s for s in range(8, tm + 1, 8) if r % s == 0)
    bc1 = 1.0 / (1.0 - ADAM_B1 ** ADAM_STEP)
    bc2 = 1.0 / (1.0 - ADAM_B2 ** ADAM_STEP)

    def body(l_ref, w_ref, m_ref, v_ref, g_ref, d_ref, nm_ref, nv_ref):
        g = l_ref[0].astype(F32)
        for i in range(1, n):
            g = g + l_ref[i].astype(F32)
        nm = ADAM_B1 * m_ref[...] + (1.0 - ADAM_B1) * g
        nv = ADAM_B2 * v_ref[...] + (1.0 - ADAM_B2) * (g * g)
        g_ref[...] = g
        nm_ref[...] = nm
        nv_ref[...] = nv
        d_ref[...] = -ADAM_LR * ((nm * bc1) / (jnp.sqrt(nv * bc2) + ADAM_EPS) + ADAM_WD * w_ref[...])

    spec = pl.BlockSpec((tm, c), lambda i: (i, 0))
    return pl.pallas_call(
        body, name=name, grid=(r // tm,),
        in_specs=[pl.BlockSpec((n, tm, c), lambda i: (0, i, 0)), spec, spec, spec],
        out_specs=[spec] * 4, out_shape=[jax.ShapeDtypeStruct((r, c), F32)] * 4,
        compiler_params=_cp("parallel"),
    )(land, w, m, v)


PACK_W = 2048


def _pack_rows(parts):
    flat = jnp.concatenate([p.reshape(-1).astype(F32) for p in parts])
    pad = (-flat.shape[0]) % (8 * PACK_W)
    return jnp.pad(flat, (0, pad)).reshape(-1, PACK_W)


def _unpack_rows(packed, shapes):
    flat = packed.reshape(-1)
    out, off = [], 0
    for s in shapes:
        n = math.prod(s)
        out.append(flat[off:off + n].reshape(s))
        off += n
    return out


def _col_slabs(gfull, width):
    r = gfull.shape[0]
    return jnp.transpose(gfull.reshape(r, N_DEV, width), (1, 0, 2)).astype(BF16)


def _row_slabs(gfull):
    return gfull.reshape(N_DEV, gfull.shape[0] // N_DEV, gfull.shape[1]).astype(BF16)


def _from_col_slabs(gathered):
    n, r, width = gathered.shape
    return jnp.transpose(gathered, (1, 0, 2)).reshape(r, n * width)


def _local_step(xs, target, norm_mix, wf_in, cw, a_log, dt_bias, dn_norm, rest, norm_ffn, norm_final,
                distributed=True):
    d = D_MODEL
    n_main = D_PA + 4 * D_B
    w_pa_cols = wf_in[:, :D_PA]
    w_pb_cols = jnp.concatenate([wf_in[:, D_PA:n_main], wf_in[:, n_main + 2 * N_HEADS_B:]], axis=1)
    w_small = jnp.pad(wf_in[:, n_main:n_main + 2 * N_HEADS_B], ((0, 0), (0, HEAD - 2 * N_HEADS_B)))
    a_log_bc = jnp.repeat(a_log, HEAD, axis=1)
    dt_bias_bc = jnp.repeat(dt_bias, HEAD, axis=1)

    u = _rms_fwd(xs, norm_mix)
    pa = _matmul(u, w_pa_cols, "nn", F32, 1024, 1536, d, name="proj_a")
    pb = _matmul(u, w_pb_cols, "nn", F32, 1024, 1024, d, name="proj_b")
    ps = _matmul(u, w_small, "nn", F32, 2048, HEAD, d, name="proj_small")
    os_, ls_ = [], []
    for gi, dil in enumerate(DILATIONS):
        o_g, l_g = _attn_fwd_group(pa, gi, dil)
        os_.append(o_g)
        ls_.append(l_g)
    y_att, lse = _attn_merge(os_, ls_)
    prep = _dn_prep_fwd(pb, ps, cw, a_log_bc, dt_bias_bc,
                        exch=_Exchange(rest, [False] * 6) if distributed else None)
    qn, kn, vn, gdec, beta = prep[:5]
    if distributed:
        g_pa, g_pd, g_out, g_gate, g_up, g_down = prep[5:]
        wf_pa, wf_pd, wf_out = _from_col_slabs(g_pa), g_pd.reshape(D_B, d), g_out.reshape(d, d)
        wf_gate, wf_up, wf_down = _from_col_slabs(g_gate), _from_col_slabs(g_up), g_down.reshape(D_FF, d)
    else:
        wf_pa, wf_pd, wf_out, wf_gate, wf_up, wf_down = rest
    wf_gu = jnp.concatenate([wf_gate, wf_up], axis=1)
    ub, ww, qd, kd, qk, tinv, gl = _dn_chunk_fwd(qn, kn, vn, gdec, beta)
    o_dn, states = _dn_scan_fwd(ub, ww, qd, kd, qk, gl)
    o_gated = _head_norm_fwd(o_dn, pb, dn_norm)
    y_a = _matmul(y_att, wf_pa, "nn", F32, 1024, d, D_ATTN_OUT, name="proj_attn")
    y_b = _matmul(o_gated, wf_pd, "nn", F32, 1024, d, d, name="proj_delta")
    merged = _gate_merge_fwd(pb, y_a, y_b)
    h1 = _matmul(merged, wf_out, "nn", F32, 1024, d, d, add=xs, name="out_proj")
    hn = _rms_fwd(h1, norm_ffn)
    gate, up, act = _ffn_in(hn, wf_gate, wf_up)
    h2 = _matmul(act, wf_down, "nn", F32, 512, d, D_FF, add=h1, name="ffn_out")
    loss_part, dh2, d_norm_final = _final_loss(h2, norm_final.reshape(1, d), target)

    dgate, dup = _ffn_act_bwd(dh2, wf_down, gate, up)
    gw_down = _matmul(act, dh2, "tn", BF16, 1408, d, TOKEN_TK, name="gw_down")
    dhn = _matmul_nt_segments([([dgate, dup], wf_gu, 1408)], BF16, 1024, "d_hn")[0]
    gw_gate = _matmul(hn, dgate, "tn", BF16, d, 1408, TOKEN_TK, name="gw_gate")
    gw_up = _matmul(hn, dup, "tn", BF16, d, 1408, TOKEN_TK, name="gw_up")
    dh1, d_norm_ffn = _rms_bwd(h1, norm_ffn, dhn, dh2)
    dmerged = _matmul(dh1, wf_out, "nt", BF16, 1024, d, d, name="d_merged")
    gw_out = _matmul(merged, dh1, "tn", BF16, d, d, TOKEN_TK, name="gw_out")
    dya, dyb, dga, dgb = _gate_merge_bwd(pb, y_a, y_b, dmerged)
    dy_att = _matmul(dya, wf_pa, "nt", F32, 1024, D_ATTN_OUT, d, name="d_y_att")
    gw_pa = _matmul(y_att, dya, "tn", BF16, D_ATTN_OUT, d, TOKEN_TK, name="gw_pa")
    do_gated = _matmul(dyb, wf_pd, "nt", BF16, 1024, d, d, name="d_o_gated")
    gw_pd = _matmul(o_gated, dyb, "tn", BF16, d, d, TOKEN_TK, name="gw_pd")
    do_dn, dz, d_dn_norm = _head_norm_bwd(o_dn, pb, dn_norm, do_gated)
    dub, dww, dqd, dkd, dqk, dgl = _dn_scan_bwd(ub, ww, qd, kd, qk, gl, states, do_dn)
    dqn, dkn, dvn, dgdec, dbeta = _dn_chunk_bwd(qn, kn, vn, gdec, beta, ub, ww, tinv, dub, dww, dqd, dkd, dqk, dgl)
    slabs = [_col_slabs(gw_pa, d // N_DEV), _row_slabs(gw_pd), _row_slabs(gw_out),
             _col_slabs(gw_gate, D_FF // N_DEV), _col_slabs(gw_up, D_FF // N_DEV),
             _row_slabs(gw_down)] if distributed else None
    prep = _dn_prep_bwd(pb, ps, cw, a_log_bc, dt_bias_bc, gdec, dqn, dkn, dvn, dgdec, dbeta,
                        exch=_Exchange(slabs, [True] * 6) if distributed else None)
    dc, dps, d_conv_full, d_alog_bc, d_dt_bc = prep[:5]
    dqkv_pre = _conv_bwd_input(dc, cw)
    segs = [_attn_bwd_group(pa, dy_att, y_att, lse, gi, dil) for gi, dil in enumerate(DILATIONS)]
    segs += [dqkv_pre, dz, dga, dgb]
    gws = [_matmul(u, s, "tn", BF16, d, 1536, TOKEN_TK, name=f"gw_in_{i}") for i, s in enumerate(segs)]
    gw_small = _matmul(u, dps, "tn", BF16, d, HEAD, TOKEN_TK, name="gw_in_small")
    g_att = jnp.concatenate(gws[:3], axis=1).reshape(d, N_HEADS_A, 3, HEAD)
    gw_in = jnp.concatenate(
        [g_att[:, :, i, :].reshape(d, D_ATTN) for i in range(3)]
        + [gws[3], gws[4], gw_small[:, :2 * N_HEADS_B], gws[5], gws[6]], axis=1)
    w_att = jnp.stack([w_pa_cols[:, i * D_ATTN:(i + 1) * D_ATTN].reshape(d, N_HEADS_A, HEAD) for i in range(3)],
                      axis=2).reshape(d, D_PA)
    du_small = _matmul(dps, w_small, "nt", F32, 1024, d, HEAD, name="d_u_small")
    du_call = _matmul_nt_segments(
        [(segs[:4], jnp.concatenate([w_att, w_pb_cols[:, :3 * D_B]], axis=1), 768),
         (segs[4:], w_pb_cols[:, 3 * D_B:], 1024)], BF16, 1024, "d_u", add=du_small,
        exch=_Exchange([_col_slabs(gw_in, SHARD_IN)], [True]) if distributed else None)
    dx, d_norm_mix = _rms_bwd(xs, norm_mix, du_call[0], dh1)
    d_a_log = d_alog_bc.reshape(1, N_HEADS_B, HEAD)[:, :, 0]
    d_dt_bias = d_dt_bc.reshape(1, N_HEADS_B, HEAD)[:, :, 0]
    small = (d_conv_full, d_norm_mix, d_norm_ffn, d_norm_final, d_dn_norm, d_a_log, d_dt_bias)
    if distributed:
        return (loss_part, dx, [du_call[1]] + list(prep[5:])) + small
    return (loss_part, dx, gw_in, gw_pa, gw_pd, gw_out, jnp.concatenate([gw_gate, gw_up], axis=1), gw_down) + small


def kernel(x, norm_mix, w_in, conv_w, a_log, dt_bias, dn_norm, w_proj_attn, w_proj_delta, w_out, norm_ffn, w_gate, w_up, w_down, norm_final, loss_target, m_norm_mix, m_w_in, m_conv_w, m_a_log, m_dt_bias, m_dn_norm, m_w_proj_attn, m_w_proj_delta, m_w_out, m_norm_ffn, m_w_gate, m_w_up, m_w_down, m_norm_final, v_norm_mix, v_w_in, v_conv_w, v_a_log, v_dt_bias, v_dn_norm, v_w_proj_attn, v_w_proj_delta, v_w_out, v_norm_ffn, v_w_gate, v_w_up, v_w_down, v_norm_final):
    d = D_MODEL
    xs = x[0]
    target = loss_target[0]
    me = _dev_index(_mesh_pos())

    g_in, g_conv = _gather_two_level([w_in[0].astype(BF16), conv_w[0]], "gather_w_in")
    rest = [w[0].astype(BF16) for w in (w_proj_attn, w_proj_delta, w_out, w_gate, w_up, w_down)]
    (loss_part, dx, landed, d_conv_full, d_norm_mix, d_norm_ffn, d_norm_final, d_dn_norm, d_a_log,
     d_dt_bias) = _local_step(xs, target, norm_mix, _from_col_slabs(g_in), _from_col_slabs(g_conv), a_log, dt_bias,
                              dn_norm, rest, norm_ffn, norm_final)

    small_shapes = [(1, d), (1, d), (d,), (1, HEAD), (1, N_HEADS_B), (1, N_HEADS_B), (1, 1), (CONV_WIDTH, 3 * D_B)]
    packed = _pack_rows([d_norm_mix, d_norm_ffn, d_norm_final, d_dn_norm, d_a_log, d_dt_bias,
                         loss_part[:, :1], d_conv_full])
    landed = list(landed) + list(_exchange([packed], [False], "gather_small_grads"))
    zero1 = jnp.zeros((1, 1), F32)
    zconv = jnp.zeros((CONV_WIDTH, 3 * D_B), F32)
    small_w = _pack_rows([norm_mix, norm_ffn, norm_final, dn_norm, a_log, dt_bias, zero1, zconv])
    small_m = _pack_rows([m_norm_mix, m_norm_ffn, m_norm_final, m_dn_norm, m_a_log, m_dt_bias, zero1, zconv])
    small_v = _pack_rows([v_norm_mix, v_norm_ffn, v_norm_final, v_dn_norm, v_a_log, v_dt_bias, zero1, zconv])
    small = [_unpack_rows(z, small_shapes) for z in _adamw(landed[7], small_w, small_m, small_v, "adamw_small")]
    loss = small[0][6].reshape(())
    conv_shard = 3 * D_B // N_DEV
    g_conv_own = lax.dynamic_slice_in_dim(small[0][7], me * conv_shard, conv_shard, axis=1)
    r_conv = _adamw(g_conv_own[None], conv_w[0], m_conv_w[0], v_conv_w[0], "adamw_conv")
    big = [_adamw(landed[i], w[0], m[0], v[0], f"adamw_{i}") for i, (w, m, v) in enumerate([
        (w_in, m_w_in, v_w_in), (w_proj_attn, m_w_proj_attn, v_w_proj_attn),
        (w_proj_delta, m_w_proj_delta, v_w_proj_delta), (w_out, m_w_out, v_w_out),
        (w_gate, m_w_gate, v_w_gate), (w_up, m_w_up, v_w_up), (w_down, m_w_down, v_w_down)])]

    def leaves(k):
        sm = small[k]
        return [sm[0], big[0][k][None], r_conv[k][None], sm[4], sm[5], sm[3], big[1][k][None], big[2][k][None],
                big[3][k][None], sm[1], big[4][k][None], big[5][k][None], big[6][k][None], sm[2]]

    return (loss, dx[None], *leaves(0), *leaves(1), *leaves(2), *leaves(3))
```

```python
import math

import jax
import jax.numpy as jnp
from jax import lax
from jax.experimental import pallas as pl
from jax.experimental.pallas import tpu as pltpu

F32 = jnp.float32
BF16 = jnp.bfloat16
HI = lax.Precision.HIGH

D_MODEL = 1024
N_DEV = 8
HEAD = 128
N_HEADS_A = 12
HEADS_PER_GROUP = 4
DILATIONS = (1, 4, 16)
BLOCK_A = 128
D_ATTN = N_HEADS_A * HEAD
D_ATTN_OUT = HEADS_PER_GROUP * HEAD
N_HEADS_B = 8
D_B = N_HEADS_B * HEAD
CONV_WIDTH = 4
CH = 64
CHUNK_GROUP = 32
SCAN_HEADS = 4
TOKEN_TK = 1024
D_FF = 2816
EPS = 1e-6
D_IN = 3 * D_ATTN + 4 * D_B + 2 * N_HEADS_B + 2 * D_MODEL
SHARD_IN = D_IN // N_DEV
PB_Z, PB_GATE = 3072, 4096
D_PA = 3 * D_ATTN
ADAM_LR, ADAM_B1, ADAM_B2, ADAM_EPS, ADAM_WD, ADAM_STEP = 0.001, 0.9, 0.999, 1e-08, 0.01, 10
VMEM_LIMIT = 60 * 1024 * 1024

NN = ((1,), (0,))
NT = ((1,), (1,))
TN = ((0,), (0,))


def _dot(a, b, dims=NN, prec=None):
    return lax.dot_general(a, b, (dims, ((), ())), precision=prec, preferred_element_type=F32)


def _bdot(a, b, dims=NN):
    return _dot(a.astype(BF16), b.astype(BF16), dims)


def _hdot(a, b, dims=NN):
    return _dot(a.astype(F32), b.astype(F32), dims, HI)


def _cp(*sem):
    return pltpu.CompilerParams(dimension_semantics=sem, vmem_limit_bytes=VMEM_LIMIT)


def _sigmoid(x):
    return 0.5 * jnp.tanh(0.5 * x) + 0.5


def _softplus(x):
    return jnp.maximum(x, 0.0) + jnp.log(1.0 + jnp.exp(-jnp.abs(x)))


def _rowsum(x):
    return jnp.sum(x, axis=-1, keepdims=True)


def _matmul(a, b, mode, out_dtype, tm, tn, tk, add=None, name="mm"):
    if mode == "nn":
        (m, k), (k2, n) = a.shape, b.shape
    elif mode == "nt":
        (m, k), (n, k2) = a.shape, b.shape
    else:
        (k, m), (k2, n) = a.shape, b.shape
    assert k == k2, (a.shape, b.shape, mode)
    tm, tn, tk = min(tm, m), min(tn, n), min(tk, k)
    assert m % tm == 0 and n % tn == 0 and k % tk == 0, (a.shape, b.shape, tm, tn, tk)
    nk = k // tk
    dims = {"nn": NN, "nt": NT, "tn": TN}[mode]

    def body(*refs):
        if add is None:
            a_ref, b_ref, o_ref, acc = refs
            add_ref = None
        else:
            a_ref, b_ref, add_ref, o_ref, acc = refs
        kk = pl.program_id(2)

        @pl.when(kk == 0)
        def _():
            acc[...] = jnp.zeros_like(acc)

        acc[...] += _bdot(a_ref[...], b_ref[...], dims)

        @pl.when(kk == nk - 1)
        def _():
            r = acc[...]
            if add_ref is not None:
                r = r + add_ref[...].astype(F32)
            o_ref[...] = r.astype(out_dtype)

    a_spec = (pl.BlockSpec((tk, tm), lambda i, j, kk: (kk, i)) if mode == "tn"
              else pl.BlockSpec((tm, tk), lambda i, j, kk: (i, kk)))
    b_spec = (pl.BlockSpec((tn, tk), lambda i, j, kk: (j, kk)) if mode == "nt"
              else pl.BlockSpec((tk, tn), lambda i, j, kk: (kk, j)))
    in_specs = [a_spec, b_spec]
    args = [a, b]
    if add is not None:
        in_specs.append(pl.BlockSpec((tm, tn), lambda i, j, kk: (i, j)))
        args.append(add)
    return pl.pallas_call(
        body, name=name, grid=(m // tm, n // tn, nk),
        in_specs=in_specs, out_specs=pl.BlockSpec((tm, tn), lambda i, j, kk: (i, j)),
        out_shape=jax.ShapeDtypeStruct((m, n), out_dtype),
        scratch_shapes=[pltpu.VMEM((tm, tn), F32)],
        compiler_params=_cp("parallel", "parallel", "arbitrary"),
    )(*args)


def _matmul_nt_segments(groups, out_dtype, tm, name, add=None, exch=None):
    m = groups[0][0][0].shape[0]
    n = groups[0][1].shape[0]
    assert m % tm == 0
    seg_list, b_list = [], []
    step = 0
    for gi, (segs, b, tk) in enumerate(groups):
        assert all(s.shape[1] % tk == 0 for s in segs) and sum(s.shape[1] for s in segs) == b.shape[1]
        g0 = step
        for s in segs:
            seg_list.append((s, tk, step, s.shape[1] // tk, gi))
            step += s.shape[1] // tk
        b_list.append((b, tk, g0, step - g0))
    nk = step
    ns, nb = len(seg_list), len(b_list)

    def body(*refs):
        seg_refs, b_refs = refs[:ns], refs[ns:ns + nb]
        add_ref = refs[ns + nb] if add is not None else None
        o_ref, acc = refs[-2], refs[-1]
        kk = pl.program_id(1)

        @pl.when(kk == 0)
        def _():
            acc[...] = jnp.zeros_like(acc)

        for a_ref, (_, _, k0, nk_s, gi) in zip(seg_refs, seg_list):
            @pl.when(jnp.logical_and(kk >= k0, kk < k0 + nk_s))
            def _(a_ref=a_ref, b_ref=b_refs[gi]):
                acc[...] += _bdot(a_ref[...], b_ref[...], NT)

        @pl.when(kk == nk - 1)
        def _():
            r = acc[...]
            if add_ref is not None:
                r = r + add_ref[...].astype(F32)
            o_ref[...] = r.astype(out_dtype)

    def walk(rows, tk, k0, nk_s, row_axis):
        if row_axis:
            return pl.BlockSpec((rows, tk), lambda i, kk: (i, jnp.clip(kk - k0, 0, nk_s - 1)))
        return pl.BlockSpec((rows, tk), lambda i, kk: (0, jnp.clip(kk - k0, 0, nk_s - 1)))

    row = pl.BlockSpec((tm, n), lambda i, kk: (i, 0))
    in_specs = [walk(tm, tk, k0, nk_s, True) for _, tk, k0, nk_s, _ in seg_list]
    in_specs += [walk(n, tk, k0, nk_s, False) for _, tk, k0, nk_s in b_list]
    args = [s[0] for s in seg_list] + [b[0] for b in b_list]
    if add is not None:
        in_specs.append(row)
        args.append(add)
    return _hosted_call(body, name=name, grid=(m // tm, nk), in_specs=in_specs, out_specs=[row],
                        out_shape=[jax.ShapeDtypeStruct((m, n), out_dtype)],
                        scratch_shapes=[pltpu.VMEM((tm, n), F32)], args=args,
                        dims=("parallel", "arbitrary"), exch=exch)


def _row_spec(tm, cols, cb=0):
    return pl.BlockSpec((tm, cols), lambda i, cb=cb: (i, cb))


def _bcast_spec(rows, cols):
    return pl.BlockSpec((rows, cols), lambda i: (0, 0))


def _rms_fwd(x, w, tm=512):
    t, d = x.shape

    def body(x_ref, w_ref, o_ref):
        xv = x_ref[...]
        r = lax.rsqrt(jnp.mean(xv * xv, axis=-1, keepdims=True) + EPS)
        o_ref[...] = (xv * r * w_ref[...]).astype(BF16)

    return pl.pallas_call(
        body, name="rms_fwd", grid=(t // tm,),
        in_specs=[_row_spec(tm, d), _bcast_spec(1, d)], out_specs=_row_spec(tm, d),
        out_shape=jax.ShapeDtypeStruct((t, d), BF16), compiler_params=_cp("parallel"),
    )(x, w)


def _rms_bwd(x, w, dy, resid, tm=512):
    t, d = x.shape

    def body(x_ref, w_ref, dy_ref, res_ref, dx_ref, dw_ref):
        xv = x_ref[...]
        r = lax.rsqrt(jnp.mean(xv * xv, axis=-1, keepdims=True) + EPS)
        xh = xv * r
        dyv = dy_ref[...].astype(F32)
        dxh = dyv * w_ref[...]
        dx_ref[...] = res_ref[...] + r * (dxh - xh * jnp.mean(dxh * xh, axis=-1, keepdims=True))

        @pl.when(pl.program_id(0) == 0)
        def _():
            dw_ref[...] = jnp.zeros_like(dw_ref)

        dw_ref[...] += jnp.sum(dyv * xh, axis=0, keepdims=True)

    return pl.pallas_call(
        body, name="rms_bwd", grid=(t // tm,),
        in_specs=[_row_spec(tm, d), _bcast_spec(1, d), _row_spec(tm, d), _row_spec(tm, d)],
        out_specs=[_row_spec(tm, d), _bcast_spec(1, d)],
        out_shape=[jax.ShapeDtypeStruct((t, d), F32), jax.ShapeDtypeStruct((1, d), F32)],
        compiler_params=_cp("arbitrary"),
    )(x, w, dy, resid)


def _final_loss(h, w, target, tm=512):
    t, d = h.shape

    def body(h_ref, w_ref, t_ref, loss_ref, dh_ref, dw_ref):
        hv = h_ref[...]
        r = lax.rsqrt(jnp.mean(hv * hv, axis=-1, keepdims=True) + EPS)
        xh = hv * r
        wv = w_ref[...]
        err = xh * wv - t_ref[...]
        dy = err * (1.0 / d)
        dxh = dy * wv
        dh_ref[...] = r * (dxh - xh * jnp.mean(dxh * xh, axis=-1, keepdims=True))

        @pl.when(pl.program_id(0) == 0)
        def _():
            dw_ref[...] = jnp.zeros_like(dw_ref)
            loss_ref[...] = jnp.zeros_like(loss_ref)

        dw_ref[...] += jnp.sum(dy * xh, axis=0, keepdims=True)
        part = 0.5 * jnp.sum(jnp.mean(err * err, axis=-1, keepdims=True), axis=0, keepdims=True)
        loss_ref[...] += part + jnp.zeros((1, HEAD), F32)

    return pl.pallas_call(
        body, name="final_loss", grid=(t // tm,),
        in_specs=[_row_spec(tm, d), _bcast_spec(1, d), _row_spec(tm, d)],
        out_specs=[_bcast_spec(1, HEAD), _row_spec(tm, d), _bcast_spec(1, d)],
        out_shape=[jax.ShapeDtypeStruct((1, HEAD), F32), jax.ShapeDtypeStruct((t, d), F32),
                   jax.ShapeDtypeStruct((1, d), F32)],
        compiler_params=_cp("arbitrary"),
    )(h, w, target)


def _ffn_in(hn, w_gate, w_up, tm=1024, tn=1408):
    t, d = hn.shape
    ff = w_gate.shape[1]

    def body(a_ref, wg_ref, wu_ref, g_ref, u_ref, act_ref):
        a = a_ref[...]
        g = _bdot(a, wg_ref[...])
        u = _bdot(a, wu_ref[...])
        g_ref[...] = g.astype(BF16)
        u_ref[...] = u.astype(BF16)
        gq = g.astype(BF16).astype(F32)
        act_ref[...] = (gq * _sigmoid(gq) * u.astype(BF16).astype(F32)).astype(BF16)

    tile = pl.BlockSpec((tm, tn), lambda i, j: (i, j))
    wspec = pl.BlockSpec((d, tn), lambda i, j: (0, j))
    return pl.pallas_call(
        body, name="ffn_in", grid=(t // tm, ff // tn),
        in_specs=[pl.BlockSpec((tm, d), lambda i, j: (i, 0)), wspec, wspec], out_specs=[tile] * 3,
        out_shape=[jax.ShapeDtypeStruct((t, ff), BF16)] * 3, compiler_params=_cp("parallel", "parallel"),
    )(hn, w_gate, w_up)


def _ffn_act_bwd(dh, w_down, g, u, tm=1024, tn=1408):
    t, d = dh.shape
    ff = w_down.shape[0]

    def body(a_ref, w_ref, g_ref, u_ref, dg_ref, du_ref):
        dv = _bdot(a_ref[...], w_ref[...], NT).astype(BF16).astype(F32)
        gv = g_ref[...].astype(F32)
        sg = _sigmoid(gv)
        dg_ref[...] = (dv * u_ref[...].astype(F32) * (sg + gv * sg * (1.0 - sg))).astype(BF16)
        du_ref[...] = (dv * gv * sg).astype(BF16)

    tile = pl.BlockSpec((tm, tn), lambda i, j: (i, j))
    return pl.pallas_call(
        body, name="ffn_act_bwd", grid=(t // tm, ff // tn),
        in_specs=[pl.BlockSpec((tm, d), lambda i, j: (i, 0)), pl.BlockSpec((tn, d), lambda i, j: (j, 0)),
                  tile, tile],
        out_specs=[tile] * 2, out_shape=[jax.ShapeDtypeStruct((t, ff), BF16)] * 2,
        compiler_params=_cp("parallel", "parallel"),
    )(dh, w_down, g, u)


def _gate_merge_fwd(pb, ya, yb, tm=512):
    t, d = ya.shape
    cb = PB_GATE // d

    def body(ga_ref, gb_ref, ya_ref, yb_ref, o_ref):
        o_ref[...] = (_sigmoid(ga_ref[...]) * ya_ref[...] + _sigmoid(gb_ref[...]) * yb_ref[...]).astype(BF16)

    return pl.pallas_call(
        body, name="gate_merge_fwd", grid=(t // tm,),
        in_specs=[_row_spec(tm, d, cb), _row_spec(tm, d, cb + 1), _row_spec(tm, d), _row_spec(tm, d)],
        out_specs=_row_spec(tm, d),
        out_shape=jax.ShapeDtypeStruct((t, d), BF16), compiler_params=_cp("parallel"),
    )(pb, pb, ya, yb)


def _gate_merge_bwd(pb, ya, yb, dm, tm=512):
    t, d = ya.shape
    cb = PB_GATE // d

    def body(ga_ref, gb_ref, ya_ref, yb_ref, dm_ref, dya_ref, dyb_ref, dga_ref, dgb_ref):
        dmv = dm_ref[...].astype(F32)
        sa = _sigmoid(ga_ref[...])
        sb = _sigmoid(gb_ref[...])
        dya_ref[...] = (dmv * sa).astype(BF16)
        dyb_ref[...] = (dmv * sb).astype(BF16)
        dga_ref[...] = (dmv * ya_ref[...] * sa * (1.0 - sa)).astype(BF16)
        dgb_ref[...] = (dmv * yb_ref[...] * sb * (1.0 - sb)).astype(BF16)

    return pl.pallas_call(
        body, name="gate_merge_bwd", grid=(t // tm,),
        in_specs=[_row_spec(tm, d, cb), _row_spec(tm, d, cb + 1), _row_spec(tm, d), _row_spec(tm, d),
                  _row_spec(tm, d)],
        out_specs=[_row_spec(tm, d)] * 4,
        out_shape=[jax.ShapeDtypeStruct((t, d), BF16)] * 4, compiler_params=_cp("parallel"),
    )(pb, pb, ya, yb, dm)


def _head_norm_fwd(o, pb, wn, tm=512):
    t, d = o.shape
    nh = d // HEAD

    def body(o_ref, z_ref, w_ref, out_ref):
        wv = w_ref[...]
        for h in range(nh):
            sl = slice(h * HEAD, (h + 1) * HEAD)
            ov = o_ref[:, sl]
            zv = z_ref[:, sl]
            r = lax.rsqrt(jnp.mean(ov * ov, axis=-1, keepdims=True) + EPS)
            out_ref[:, sl] = (ov * r * wv * (zv * _sigmoid(zv))).astype(BF16)

    return pl.pallas_call(
        body, name="head_norm_fwd", grid=(t // tm,),
        in_specs=[_row_spec(tm, d), _row_spec(tm, d, PB_Z // d), _bcast_spec(1, HEAD)],
        out_specs=_row_spec(tm, d),
        out_shape=jax.ShapeDtypeStruct((t, d), BF16), compiler_params=_cp("parallel"),
    )(o, pb, wn)


def _head_norm_bwd(o, pb, wn, dout, tm=512):
    t, d = o.shape
    nh = d // HEAD

    def body(o_ref, z_ref, w_ref, d_ref, do_ref, dz_ref, dw_ref):
        wv = w_ref[...]
        dw_acc = jnp.zeros((1, HEAD), F32)
        for h in range(nh):
            sl = slice(h * HEAD, (h + 1) * HEAD)
            ov = o_ref[:, sl]
            zv = z_ref[:, sl]
            dv = d_ref[:, sl].astype(F32)
            r = lax.rsqrt(jnp.mean(ov * ov, axis=-1, keepdims=True) + EPS)
            xh = ov * r
            sz = _sigmoid(zv)
            dn = dv * (zv * sz)
            dz_ref[:, sl] = (dv * xh * wv * (sz + zv * sz * (1.0 - sz))).astype(BF16)
            dxh = dn * wv
            do_ref[:, sl] = r * (dxh - xh * jnp.mean(dxh * xh, axis=-1, keepdims=True))
            dw_acc = dw_acc + jnp.sum(dn * xh, axis=0, keepdims=True)

        @pl.when(pl.program_id(0) == 0)
        def _():
            dw_ref[...] = jnp.zeros_like(dw_ref)

        dw_ref[...] += dw_acc

    return pl.pallas_call(
        body, name="head_norm_bwd", grid=(t // tm,),
        in_specs=[_row_spec(tm, d), _row_spec(tm, d, PB_Z // d), _bcast_spec(1, HEAD), _row_spec(tm, d)],
        out_specs=[_row_spec(tm, d), _row_spec(tm, d), _bcast_spec(1, HEAD)],
        out_shape=[jax.ShapeDtypeStruct((t, d), F32), jax.ShapeDtypeStruct((t, d), BF16),
                   jax.ShapeDtypeStruct((1, HEAD), F32)],
        compiler_params=_cp("arbitrary"),
    )(o, pb, wn, dout)


def _attn_bias(gi, hh, dil):
    i = lax.broadcasted_iota(jnp.int32, (BLOCK_A, BLOCK_A), 0)
    j = lax.broadcasted_iota(jnp.int32, (BLOCK_A, BLOCK_A), 1)
    hf = (gi * HEADS_PER_GROUP + hh + 1).astype(F32)
    slope = jnp.exp(jnp.full((1, BLOCK_A), -8.0 * math.log(2.0) / N_HEADS_A, F32) * hf) * float(dil)
    d_prev = (BLOCK_A + i - j).astype(F32)
    d_cur = (i - j).astype(F32)
    return -slope * d_prev, -slope * d_cur, j >= i, j <= i


ATTN_TOKENS = 2048


def _sub_rows(a, r, dil):
    start = a * BLOCK_A * dil + r
    return pl.ds(start, BLOCK_A) if dil == 1 else pl.ds(start, BLOCK_A, stride=dil)


def _attn_fwd_group(pa, gi, dil, tb=ATTN_TOKENS):
    t = pa.shape[0]
    tb = min(tb, t)
    hb = BLOCK_A * dil
    nb = tb // hb
    scale = HEAD ** -0.5

    def body(q_ref, k_ref, v_ref, kp_ref, vp_ref, o_ref, l_ref):
        hh = pl.program_id(0)
        step = pl.program_id(1)
        b_prev, b_cur, m_prev, m_cur = _attn_bias(gi, hh, dil)
        m_first = jnp.logical_and(m_prev, step > 0)
        for r in range(dil):
            kp, vp = kp_ref[_sub_rows(0, r, dil), :], vp_ref[_sub_rows(0, r, dil), :]
            for a in range(nb):
                rows = _sub_rows(a, r, dil)
                q, kc, vc = q_ref[rows, :], k_ref[rows, :], v_ref[rows, :]
                s_p = jnp.where(m_first if a == 0 else m_prev, _bdot(q, kp, NT) * scale + b_prev, -1e30)
                s_c = jnp.where(m_cur, _bdot(q, kc, NT) * scale + b_cur, -1e30)
                m = jnp.maximum(jnp.max(s_p, axis=-1, keepdims=True), jnp.max(s_c, axis=-1, keepdims=True))
                p_p = jnp.exp(s_p - m)
                p_c = jnp.exp(s_c - m)
                den = _rowsum(p_p) + _rowsum(p_c)
                o_ref[rows, :] = (_bdot(p_p, vp) + _bdot(p_c, vc)) / den
                l_ref[rows, :] = (m + jnp.log(den)) + jnp.zeros((BLOCK_A, HEAD), F32)
                kp, vp = kc, vc

    def col(base):
        return lambda hh, s: (s, base + hh)

    def col_prev(base):
        return lambda hh, s: (jnp.maximum(s * nb - 1, 0), base + hh)

    qb, kb, vb = gi * HEADS_PER_GROUP, N_HEADS_A + gi * HEADS_PER_GROUP, 2 * N_HEADS_A + gi * HEADS_PER_GROUP
    ospec = pl.BlockSpec((tb, HEAD), lambda hh, s: (s, hh))
    return pl.pallas_call(
        body, name=f"attn_fwd_g{gi}", grid=(HEADS_PER_GROUP, t // tb),
        in_specs=[pl.BlockSpec((tb, HEAD), col(qb)), pl.BlockSpec((tb, HEAD), col(kb)),
                  pl.BlockSpec((tb, HEAD), col(vb)),
                  pl.BlockSpec((hb, HEAD), col_prev(kb)), pl.BlockSpec((hb, HEAD), col_prev(vb))],
        out_specs=[ospec, ospec],
        out_shape=[jax.ShapeDtypeStruct((t, D_ATTN_OUT), F32)] * 2,
        compiler_params=_cp("parallel", "parallel"),
    )(pa, pa, pa, pa, pa)


def _attn_merge(os, ls, tm=512):
    t, d = os[0].shape

    def body(o0, o1, o2, l0, l1, l2, y_ref, lse_ref):
        a0, a1, a2 = l0[...], l1[...], l2[...]
        m = jnp.maximum(jnp.maximum(a0, a1), a2)
        e0, e1, e2 = jnp.exp(a0 - m), jnp.exp(a1 - m), jnp.exp(a2 - m)
        den = e0 + e1 + e2
        y_ref[...] = (e0 * o0[...] + e1 * o1[...] + e2 * o2[...]) / den
        lse_ref[...] = m + jnp.log(den)

    return pl.pallas_call(
        body, name="attn_merge", grid=(t // tm,),
        in_specs=[_row_spec(tm, d)] * 6, out_specs=[_row_spec(tm, d)] * 2,
        out_shape=[jax.ShapeDtypeStruct((t, d), F32)] * 2,
        compiler_params=_cp("parallel"),
    )(*os, *ls)


def _attn_bwd_group(pa, dy, y, lse, gi, dil, tb=ATTN_TOKENS):
    t = pa.shape[0]
    tb = min(tb, t)
    hb = BLOCK_A * dil
    nb = tb // hb
    nsteps = t // tb
    scale = HEAD ** -0.5

    def body(q_ref, k_ref, v_ref, dy_ref, y_ref, l_ref, kp_ref, vp_ref, d_ref,
             dq_s, dk_s, dv_s, carry_k, carry_v):
        hh = pl.program_id(0)
        step = pl.program_id(1)

        @pl.when(step == 0)
        def _():
            carry_k[...] = jnp.zeros_like(carry_k)
            carry_v[...] = jnp.zeros_like(carry_v)

        b_prev, b_cur, m_prev, m_cur = _attn_bias(gi, hh, dil)
        m_first = jnp.logical_and(m_prev, step < nsteps - 1)
        for r in range(dil):
            halo = _sub_rows(0, r, dil)
            dk_in, dv_in = carry_k[halo, :], carry_v[halo, :]
            kp, vp = kp_ref[halo, :], vp_ref[halo, :]
            prev_rows = None
            dk_pend = dv_pend = None
            for a in range(nb):
                rows = _sub_rows(a, r, dil)
                q, kc, vc = q_ref[rows, :], k_ref[rows, :], v_ref[rows, :]
                dyb, lb = dy_ref[rows, :], l_ref[rows, :]
                delta = _rowsum(dyb * y_ref[rows, :])
                mp = m_first if a == 0 else m_prev
                s = _bdot(q, kp, NT) * scale + b_prev
                p = jnp.where(mp, jnp.exp(jnp.where(mp, s - lb, 0.0)), 0.0)
                ds = p * (_bdot(dyb, vp, NT) - delta)
                dq = _bdot(ds, kp)
                dk_prev, dv_prev = _bdot(ds, q, TN), _bdot(p, dyb, TN)
                if a == 0:
                    carry_k[halo, :] = dk_prev
                    carry_v[halo, :] = dv_prev
                else:
                    dk_s[prev_rows, :] = dk_pend + dk_prev
                    dv_s[prev_rows, :] = dv_pend + dv_prev
                s = _bdot(q, kc, NT) * scale + b_cur
                p = jnp.where(m_cur, jnp.exp(jnp.where(m_cur, s - lb, 0.0)), 0.0)
                ds = p * (_bdot(dyb, vc, NT) - delta)
                dq_s[rows, :] = dq + _bdot(ds, kc)
                dk_pend, dv_pend = _bdot(ds, q, TN), _bdot(p, dyb, TN)
                prev_rows, kp, vp = rows, kc, vc
            dk_s[prev_rows, :] = dk_pend + dk_in
            dv_s[prev_rows, :] = dv_pend + dv_in
        d_ref[:, :HEAD] = (dq_s[...] * scale).astype(BF16)
        d_ref[:, HEAD:2 * HEAD] = (dk_s[...] * scale).astype(BF16)
        d_ref[:, 2 * HEAD:] = dv_s[...].astype(BF16)

    def col(base):
        return lambda hh, s: (nsteps - 1 - s, base + hh)

    def col_prev(base):
        return lambda hh, s: (jnp.maximum((nsteps - 1 - s) * nb - 1, 0), base + hh)

    qb, kb, vb = gi * HEADS_PER_GROUP, N_HEADS_A + gi * HEADS_PER_GROUP, 2 * N_HEADS_A + gi * HEADS_PER_GROUP
    big, small = (tb, HEAD), (hb, HEAD)
    return pl.pallas_call(
        body, name=f"attn_bwd_g{gi}", grid=(HEADS_PER_GROUP, nsteps),
        in_specs=[pl.BlockSpec(big, col(qb)), pl.BlockSpec(big, col(kb)), pl.BlockSpec(big, col(vb)),
                  pl.BlockSpec(big, col(0)), pl.BlockSpec(big, col(0)), pl.BlockSpec(big, col(0)),
                  pl.BlockSpec(small, col_prev(kb)), pl.BlockSpec(small, col_prev(vb))],
        out_specs=pl.BlockSpec((tb, 3 * HEAD), col(0)),
        out_shape=jax.ShapeDtypeStruct((t, 3 * D_ATTN_OUT), BF16),
        scratch_shapes=[pltpu.VMEM(big, F32)] * 3 + [pltpu.VMEM(small, F32)] * 2,
        compiler_params=_cp("parallel", "arbitrary"),
    )(pa, pa, pa, dy, y, lse, pa, pa)


def _shift_down(cur, prev8, s):
    if s == 0:
        return cur
    rolled = pltpu.roll(cur, s, 0)
    prolled = pltpu.roll(prev8, s, 0)
    rid = lax.broadcasted_iota(jnp.int32, prev8.shape, 0)
    top = jnp.where(rid < s, prolled, rolled[:8])
    return jnp.concatenate([top, rolled[8:]], axis=0)


def _shift_up(cur, next8, s):
    if s == 0:
        return cur
    n = cur.shape[0]
    rolled = pltpu.roll(cur, n - s, 0)
    nrolled = pltpu.roll(next8, 8 - s, 0)
    rid = lax.broadcasted_iota(jnp.int32, next8.shape, 0)
    bottom = jnp.where(rid >= 8 - s, nrolled, rolled[n - 8:])
    return jnp.concatenate([rolled[:n - 8], bottom], axis=0)


def _conv(xv, prev8, wv):
    c = jnp.zeros_like(xv)
    shifted = []
    for s in range(CONV_WIDTH):
        xs = _shift_down(xv, prev8, s)
        shifted.append(xs)
        c = c + wv[CONV_WIDTH - 1 - s:CONV_WIDTH - s, :] * xs
    return c, shifted


def _head_expand(psv, first):
    tm = psv.shape[0]
    return jnp.concatenate([jnp.broadcast_to(psv[:, first + h:first + h + 1], (tm, HEAD))
                            for h in range(N_HEADS_B)], axis=1)


def _head_collect(x, first):
    lane = lax.broadcasted_iota(jnp.int32, (x.shape[0], HEAD), 1)
    out = jnp.zeros((x.shape[0], HEAD), F32)
    for h in range(N_HEADS_B):
        out = jnp.where(lane == first + h, x[:, h * HEAD:(h + 1) * HEAD], out)
    return out


def _dn_prep_fwd(pb, ps, conv_w, a_log_bc, dt_bias_bc, tm=256, exch=None):
    t = pb.shape[0]
    c3 = 3 * D_B
    r8 = tm // 8

    def body(x_ref, xp_ref, ps_ref, w_ref, al_ref, dt_ref, q_ref, k_ref, v_ref, g_ref, beta_ref):
        first = pl.program_id(0) > 0

        def silu_conv(cols):
            c, _ = _conv(x_ref[:, cols], jnp.where(first, xp_ref[:, cols], 0.0), w_ref[:, cols])
            return c * _sigmoid(c)

        for h in range(N_HEADS_B):
            sl = slice(h * HEAD, (h + 1) * HEAD)
            sq = silu_conv(sl)
            q_ref[:, sl] = sq * lax.rsqrt(_rowsum(sq * sq) + EPS) * (HEAD ** -0.5)
            sk = silu_conv(slice(D_B + h * HEAD, D_B + (h + 1) * HEAD))
            k_ref[:, sl] = sk * lax.rsqrt(_rowsum(sk * sk) + EPS)
            v_ref[:, sl] = silu_conv(slice(2 * D_B + h * HEAD, 2 * D_B + (h + 1) * HEAD))
        psv = ps_ref[...]
        beta_ref[...] = _sigmoid(_head_expand(psv, 0))
        g_ref[...] = -jnp.exp(al_ref[...]) * _softplus(_head_expand(psv, N_HEADS_B) + dt_ref[...])

    return _hosted_call(
        body, name="dn_prep_fwd", grid=(t // tm,),
        in_specs=[_row_spec(tm, c3, 0),
                  pl.BlockSpec((8, c3), lambda i: (jnp.maximum(i * r8 - 1, 0), 0)),
                  _row_spec(tm, HEAD),
                  _bcast_spec(CONV_WIDTH, c3), _bcast_spec(1, D_B), _bcast_spec(1, D_B)],
        out_specs=[_row_spec(tm, D_B)] * 5,
        out_shape=[jax.ShapeDtypeStruct((t, D_B), F32)] * 5, scratch_shapes=[],
        args=(pb, pb, ps, conv_w, a_log_bc, dt_bias_bc), dims=("parallel",), exch=exch)


def _dn_prep_bwd(pb, ps, conv_w, a_log_bc, dt_bias_bc, g, dq, dk, dv, dg, dbeta, tm=128, exch=None):
    t = pb.shape[0]
    c3 = 3 * D_B
    r8 = tm // 8

    def body(x_ref, xp_ref, ps_ref, w_ref, al_ref, dt_ref, g_ref, dq_ref, dk_ref, dv_ref, dg_ref, db_ref,
             dc_ref, dps_ref, dw_ref, dal_ref, ddt_ref):
        first = pl.program_id(0) > 0

        @pl.when(pl.program_id(0) == 0)
        def _():
            dw_ref[...] = jnp.zeros_like(dw_ref)
            dal_ref[...] = jnp.zeros_like(dal_ref)
            ddt_ref[...] = jnp.zeros_like(ddt_ref)

        def column_block(cols, d_ref, sl, mult, normed):
            c, shifted = _conv(x_ref[:, cols], jnp.where(first, xp_ref[:, cols], 0.0), w_ref[:, cols])
            sg = _sigmoid(c)
            dsilu = sg + c * sg * (1.0 - sg)
            dyv = d_ref[:, sl]
            if normed:
                sv = c * sg
                r = lax.rsqrt(_rowsum(sv * sv) + EPS)
                yh = sv * r
                dyv = dyv * mult
                dyv = r * (dyv - yh * _rowsum(dyv * yh))
            dcv = dyv * dsilu
            dc_ref[:, cols] = dcv
            for sft in range(CONV_WIDTH):
                j = CONV_WIDTH - 1 - sft
                dw_ref[j:j + 1, cols] += jnp.sum(dcv * shifted[sft], axis=0, keepdims=True)

        for h in range(N_HEADS_B):
            sl = slice(h * HEAD, (h + 1) * HEAD)
            column_block(sl, dq_ref, sl, HEAD ** -0.5, True)
            column_block(slice(D_B + h * HEAD, D_B + (h + 1) * HEAD), dk_ref, sl, 1.0, True)
            column_block(slice(2 * D_B + h * HEAD, 2 * D_B + (h + 1) * HEAD), dv_ref, sl, 1.0, False)
        psv = ps_ref[...]
        beta = _sigmoid(_head_expand(psv, 0))
        dgv = dg_ref[...]
        da = dgv * (-jnp.exp(al_ref[...])) * _sigmoid(_head_expand(psv, N_HEADS_B) + dt_ref[...])
        dps_ref[...] = _head_collect(db_ref[...] * beta * (1.0 - beta), 0) + _head_collect(da, N_HEADS_B)
        dal_ref[...] += jnp.sum(dgv * g_ref[...], axis=0, keepdims=True)
        ddt_ref[...] += jnp.sum(da, axis=0, keepdims=True)

    row = _row_spec(tm, D_B)
    return _hosted_call(
        body, name="dn_prep_bwd", grid=(t // tm,),
        in_specs=[_row_spec(tm, c3, 0),
                  pl.BlockSpec((8, c3), lambda i: (jnp.maximum(i * r8 - 1, 0), 0)),
                  _row_spec(tm, HEAD),
                  _bcast_spec(CONV_WIDTH, c3), _bcast_spec(1, D_B), _bcast_spec(1, D_B),
                  row, row, row, row, row, row],
        out_specs=[_row_spec(tm, c3), _row_spec(tm, HEAD), _bcast_spec(CONV_WIDTH, c3), _bcast_spec(1, D_B),
                   _bcast_spec(1, D_B)],
        out_shape=[jax.ShapeDtypeStruct((t, c3), F32), jax.ShapeDtypeStruct((t, HEAD), F32),
                   jax.ShapeDtypeStruct((CONV_WIDTH, c3), F32),
                   jax.ShapeDtypeStruct((1, D_B), F32), jax.ShapeDtypeStruct((1, D_B), F32)],
        scratch_shapes=[], args=(pb, pb, ps, conv_w, a_log_bc, dt_bias_bc, g, dq, dk, dv, dg, dbeta),
        dims=("arbitrary",), exch=exch)


def _conv_bwd_input(dc, conv_w, tm=256):
    t, c3 = dc.shape
    r8 = tm // 8
    nlast = t // 8 - 1
    nsteps = t // tm

    def body(d_ref, dn_ref, w_ref, o_ref):
        not_last = pl.program_id(0) < nsteps - 1
        for cb in range(c3 // HEAD):
            cols = slice(cb * HEAD, (cb + 1) * HEAD)
            next8 = jnp.where(not_last, dn_ref[:, cols], 0.0)
            dv = d_ref[:, cols]
            wv = w_ref[:, cols]
            acc = jnp.zeros_like(dv)
            for s in range(CONV_WIDTH):
                acc = acc + wv[CONV_WIDTH - 1 - s:CONV_WIDTH - s, :] * _shift_up(dv, next8, s)
            o_ref[:, cols] = acc.astype(BF16)

    return pl.pallas_call(
        body, name="conv_bwd_input", grid=(nsteps,),
        in_specs=[_row_spec(tm, c3), pl.BlockSpec((8, c3), lambda i: (jnp.minimum((i + 1) * r8, nlast), 0)),
                  _bcast_spec(CONV_WIDTH, c3)],
        out_specs=_row_spec(tm, c3),
        out_shape=jax.ShapeDtypeStruct((t, c3), BF16), compiler_params=_cp("parallel"),
    )(dc, dc, conv_w)


def _lanes(x):
    return x[:, :CH]


def _tri_inv(a_list, r, c):
    eye = (r == c).astype(F32)
    b16 = (r >> 4) == (c >> 4)
    b32 = (r >> 5) == (c >> 5)
    ns = [jnp.where(b16, -a, 0.0) for a in a_list]
    xs = [eye + n for n in ns]
    ps = [_bdot(n, n) for n in ns]
    for last in (False, False, True):
        xs = [x + _bdot(x, p) for x, p in zip(xs, ps)]
        if not last:
            ps = [_bdot(p, p) for p in ps]
    for mask in (jnp.logical_and(b32, jnp.logical_not(b16)), jnp.logical_not(b32)):
        ts = [_bdot(x, jnp.where(mask, a, 0.0)) for x, a in zip(xs, a_list)]
        xs = [x - _bdot(t, x) for x, t in zip(xs, ts)]
    return xs


def _chunk_local(qs, ks, vs, gs, betas, solved=None):
    r = lax.broadcasted_iota(jnp.int32, (CH, CH), 0)
    c = lax.broadcasted_iota(jnp.int32, (CH, CH), 1)
    incl, strict = r >= c, r > c
    lm = incl.astype(F32)
    cums = [_hdot(lm, jnp.concatenate([g, jnp.where(strict, _lanes(g), 0.0)], axis=1)) for g in gs]
    gcbs = [cm[:, :HEAD] for cm in cums]
    decays = [jnp.where(incl, jnp.exp(jnp.where(incl, cm[:, HEAD:], 0.0)), 0.0) for cm in cums]
    bcols = [_lanes(b) for b in betas]
    kks = [_bdot(k, k, NT) for k in ks]
    qkraws = [_bdot(q, k, NT) for q, k in zip(qs, ks)]
    egs = [jnp.exp(gcb) for gcb in gcbs]
    if solved is None:
        tms = _tri_inv([jnp.where(strict, bc * kk * dc, 0.0) for bc, kk, dc in zip(bcols, kks, decays)], r, c)
        sols = [_hdot(tm, jnp.concatenate([b * v, b * eg * k], axis=1))
                for tm, b, v, eg, k in zip(tms, betas, vs, egs, ks)]
        ubars, ws = [sol[:, :HEAD] for sol in sols], [sol[:, HEAD:] for sol in sols]
    else:
        tms, ubars, ws = solved
    gls = [gcb[CH - 1:CH, :] for gcb in gcbs]
    eks = [jnp.exp(gl - gcb) for gl, gcb in zip(gls, gcbs)]
    return [dict(incl=incl, strict=strict, r=r, c=c, decay=dc, bcol=bc, kk=kk, tm=tm, eg=eg,
                 u_bar=ub, w=w, qkraw=qkraw, gl=gl, ek=ek)
            for dc, bc, kk, tm, eg, ub, w, qkraw, gl, ek
            in zip(decays, bcols, kks, tms, egs, ubars, ws, qkraws, gls, eks)]


def _dn_chunk_fwd(q, k, v, g, beta, cps=CHUNK_GROUP):
    t = q.shape[0]
    tm = cps * CH

    def body(q_ref, k_ref, v_ref, g_ref, b_ref, ub_ref, w_ref, qd_ref, kd_ref, qk_ref, ti_ref, gl_ref):
        for base in range(0, cps, CHUNK_GROUP):
            sls = [slice((base + j) * CH, (base + j + 1) * CH) for j in range(CHUNK_GROUP)]
            qs, ks = [q_ref[sl, :] for sl in sls], [k_ref[sl, :] for sl in sls]
            locs = _chunk_local(qs, ks, [v_ref[sl, :] for sl in sls], [g_ref[sl, :] for sl in sls],
                                [b_ref[sl, :] for sl in sls])
            for j, (sl, qv, kv, loc) in enumerate(zip(sls, qs, ks, locs)):
                ub_ref[sl, :] = loc["u_bar"]
                w_ref[sl, :] = loc["w"]
                qd_ref[sl, :] = qv * loc["eg"]
                kd_ref[sl, :] = kv * loc["ek"]
                qk_ref[sl, :] = loc["qkraw"] * loc["decay"]
                ti_ref[sl, :] = loc["tm"]
                gl_ref[base + j:base + j + 1, :] = jnp.exp(loc["gl"])

    hspec = pl.BlockSpec((tm, HEAD), lambda h, i: (i, h))
    sq_spec = pl.BlockSpec((None, tm, CH), lambda h, i: (h, i, 0))
    sq_shape = jax.ShapeDtypeStruct((N_HEADS_B, t, CH), F32)
    return pl.pallas_call(
        body, name="dn_chunk_fwd", grid=(N_HEADS_B, t // tm),
        in_specs=[hspec] * 5,
        out_specs=[hspec] * 4 + [sq_spec, sq_spec, pl.BlockSpec((cps, HEAD), lambda h, i: (i, h))],
        out_shape=[jax.ShapeDtypeStruct((t, D_B), F32)] * 4
        + [sq_shape, sq_shape, jax.ShapeDtypeStruct((t // CH, D_B), F32)],
        compiler_params=_cp("parallel", "parallel"),
    )(q, k, v, g, beta)


def _dn_scan_fwd(ub, w, qd, kd, qk, gl, cps=8, hg=2 * SCAN_HEADS):
    t = ub.shape[0]
    tm = cps * CH
    hs = list(range(hg))

    def body(ub_ref, w_ref, qd_ref, kd_ref, qk_ref, gl_ref, o_ref, st_ref, s_acc):
        @pl.when(pl.program_id(1) == 0)
        def _():
            s_acc[...] = jnp.zeros_like(s_acc)

        for ci in range(cps):
            sl = slice(ci * CH, (ci + 1) * CH)
            cols = [slice(h * HEAD, (h + 1) * HEAD) for h in hs]
            svs = [s_acc[h] for h in hs]
            for h in hs:
                st_ref[h, ci * HEAD:(ci + 1) * HEAD, :] = svs[h]
            us = [ub_ref[sl, cols[h]] - _bdot(w_ref[sl, cols[h]], svs[h]) for h in hs]
            for h in hs:
                s_acc[h] = gl_ref[ci:ci + 1, cols[h]] * svs[h] + _bdot(kd_ref[sl, cols[h]], us[h], TN)
            for h in hs:
                o_ref[sl, cols[h]] = _bdot(qd_ref[sl, cols[h]], svs[h]) + _bdot(qk_ref[h, sl, :], us[h])

    hspec = pl.BlockSpec((tm, hg * HEAD), lambda h, i: (i, h))
    return pl.pallas_call(
        body, name="dn_scan_fwd", grid=(N_HEADS_B // hg, t // tm),
        in_specs=[hspec] * 4 + [pl.BlockSpec((hg, tm, CH), lambda h, i: (h, i, 0)),
                                pl.BlockSpec((cps, hg * HEAD), lambda h, i: (i, h))],
        out_specs=[hspec, pl.BlockSpec((hg, cps * HEAD, HEAD), lambda h, i: (h, i, 0))],
        out_shape=[jax.ShapeDtypeStruct((t, D_B), F32),
                   jax.ShapeDtypeStruct((N_HEADS_B, (t // CH) * HEAD, HEAD), F32)],
        scratch_shapes=[pltpu.VMEM((hg, HEAD, HEAD), F32)],
        compiler_params=_cp("parallel", "arbitrary"),
    )(ub, w, qd, kd, qk, gl)


def _dn_scan_bwd(ub, w, qd, kd, qk, gl, st, do, cps=8):
    t = ub.shape[0]
    tm = cps * CH
    ns = t // tm
    hg = 2 * SCAN_HEADS
    hs = list(range(hg))

    def body(ub_ref, w_ref, qd_ref, kd_ref, qk_ref, gl_ref, st_ref, do_ref,
             dub_ref, dw_ref, dqd_ref, dkd_ref, dqk_ref, dgl_ref, ds_acc):
        @pl.when(pl.program_id(1) == 0)
        def _():
            ds_acc[...] = jnp.zeros_like(ds_acc)

        for ci in reversed(range(cps)):
            sl = slice(ci * CH, (ci + 1) * CH)
            cols = [slice(h * HEAD, (h + 1) * HEAD) for h in hs]
            svs = [st_ref[h, ci * HEAD:(ci + 1) * HEAD, :] for h in hs]
            wvs = [w_ref[sl, cols[h]] for h in hs]
            dovs = [do_ref[sl, cols[h]] for h in hs]
            dsvs = [ds_acc[h] for h in hs]
            us = [ub_ref[sl, cols[h]] - _bdot(wvs[h], svs[h]) for h in hs]
            dus = [_bdot(kd_ref[sl, cols[h]], dsvs[h]) + _bdot(qk_ref[h, sl, :], dovs[h], TN) for h in hs]
            for h in hs:
                ds_acc[h] = (gl_ref[ci:ci + 1, cols[h]] * dsvs[h] + _bdot(qd_ref[sl, cols[h]], dovs[h], TN)
                             - _bdot(wvs[h], dus[h], TN))
            for h in hs:
                dgl_ref[ci:ci + 1, cols[h]] = (jnp.sum(_rowsum(dsvs[h] * svs[h]), axis=0, keepdims=True)
                                              + jnp.zeros((1, HEAD), F32))
                dkd_ref[sl, cols[h]] = _bdot(us[h], dsvs[h], NT)
                dqd_ref[sl, cols[h]] = _bdot(dovs[h], svs[h], NT)
                dqk_ref[h, sl, :] = _bdot(dovs[h], us[h], NT)
                dub_ref[sl, cols[h]] = dus[h]
                dw_ref[sl, cols[h]] = -_bdot(dus[h], svs[h], NT)

    hspec = pl.BlockSpec((tm, hg * HEAD), lambda h, i: (ns - 1 - i, h))
    qkspec = pl.BlockSpec((hg, tm, CH), lambda h, i: (h, ns - 1 - i, 0))
    glspec = pl.BlockSpec((cps, hg * HEAD), lambda h, i: (ns - 1 - i, h))
    return pl.pallas_call(
        body, name="dn_scan_bwd", grid=(N_HEADS_B // hg, ns),
        in_specs=[hspec] * 4 + [qkspec, glspec,
                                pl.BlockSpec((hg, cps * HEAD, HEAD), lambda h, i: (h, ns - 1 - i, 0)), hspec],
        out_specs=[hspec] * 4 + [qkspec, glspec],
        out_shape=[jax.ShapeDtypeStruct((t, D_B), F32)] * 4
        + [jax.ShapeDtypeStruct((N_HEADS_B, t, CH), F32), jax.ShapeDtypeStruct((t // CH, D_B), F32)],
        scratch_shapes=[pltpu.VMEM((hg, HEAD, HEAD), F32)],
        compiler_params=_cp("parallel", "arbitrary"),
    )(ub, w, qd, kd, qk, gl, st, do)


def _dn_chunk_bwd(q, k, v, g, beta, ub, w, tinv, dub, dw, dqd, dkd, dqk, dgl, cps=CHUNK_GROUP):
    t = q.shape[0]
    tm = cps * CH

    def body(q_ref, k_ref, v_ref, g_ref, b_ref, ub_ref, w_ref, dub_ref, dw_ref, dqd_ref, dkd_ref, ti_ref, dqk_ref,
             dgl_ref, dq_ref, dk_ref, dv_ref, dg_ref, db_ref):
        ones = jnp.ones((CH, HEAD), F32)
        rid = lax.broadcasted_iota(jnp.int32, (CH, HEAD), 0)

        def rest(ci, sl, qv, kv, vv, beta_v, loc, dr, da):
            incl = loc["incl"]
            eg, ek, decay, bcol, kk = loc["eg"], loc["ek"], loc["decay"], loc["bcol"], loc["kk"]
            drv, drk = dr[:, :HEAD], dr[:, HEAD:]
            dv_ref[sl, :] = beta_v * drv
            beg = beta_v * eg
            t1 = drk * kv
            dbeta = _rowsum(drv * vv + t1 * eg) + _rowsum(da * kk * decay)
            dkk = da * bcol * decay
            dqk_m = jnp.where(incl, dqk_ref[sl, :], 0.0)
            ddecay = da * bcol * kk + dqk_m * loc["qkraw"]
            dqkraw = dqk_m * decay
            dqdv, dkdv = dqd_ref[sl, :], dkd_ref[sl, :]
            dq_ref[sl, :] = _bdot(dqkraw, kv) + dqdv * eg
            dk_ref[sl, :] = (beg * drk + _bdot(dqkraw, qv, TN) + _bdot(dkk, kv) + _bdot(dkk, kv, TN)
                             + dkdv * ek)
            e = ddecay * decay
            skd = _rowsum(dkdv * kv * ek)
            dgc = _rowsum(beg * t1) + _rowsum(e) + _rowsum(dqdv * qv * eg) - skd
            colsum = _hdot(e, ones, TN)
            last = jnp.sum(skd, axis=0, keepdims=True) + dgl_ref[ci:ci + 1, :] * jnp.exp(loc["gl"])
            db_ref[sl, :] = dbeta + jnp.zeros((CH, HEAD), F32)
            return (dgc - colsum) + jnp.where(rid == CH - 1, last, 0.0)

        for base in range(0, cps, CHUNK_GROUP):
            cis = list(range(base, base + CHUNK_GROUP))
            sls = [slice(ci * CH, (ci + 1) * CH) for ci in cis]
            qs, ks, vs = [q_ref[sl, :] for sl in sls], [k_ref[sl, :] for sl in sls], [v_ref[sl, :] for sl in sls]
            betas = [b_ref[sl, :] for sl in sls]
            locs = _chunk_local(qs, ks, vs, [g_ref[sl, :] for sl in sls], betas,
                                solved=([ti_ref[sl, :] for sl in sls], [ub_ref[sl, :] for sl in sls],
                                        [w_ref[sl, :] for sl in sls]))
            drs = [_hdot(loc["tm"], jnp.concatenate([dub_ref[sl, :], dw_ref[sl, :]], axis=1), TN)
                   for loc, sl in zip(locs, sls)]
            das = [jnp.where(loc["strict"],
                             -_hdot(dr, jnp.concatenate([loc["u_bar"], loc["w"]], axis=1), NT), 0.0)
                   for loc, dr in zip(locs, drs)]
            dgcs = [rest(*args) for args in zip(cis, sls, qs, ks, vs, betas, locs, drs, das)]
            um = (locs[0]["r"] <= locs[0]["c"]).astype(F32)
            for sl, dgc_bc in zip(sls, dgcs):
                dg_ref[sl, :] = _hdot(um, dgc_bc)

    hspec = pl.BlockSpec((tm, HEAD), lambda h, i: (i, h))
    sq_spec = pl.BlockSpec((None, tm, CH), lambda h, i: (h, i, 0))
    return pl.pallas_call(
        body, name="dn_chunk_bwd", grid=(N_HEADS_B, t // tm),
        in_specs=[hspec] * 11 + [sq_spec, sq_spec, pl.BlockSpec((cps, HEAD), lambda h, i: (i, h))],
        out_specs=[hspec] * 5,
        out_shape=[jax.ShapeDtypeStruct((t, D_B), F32)] * 5,
        compiler_params=_cp("parallel", "parallel"),
    )(q, k, v, g, beta, ub, w, dub, dw, dqd, dkd, tinv, dqk, dgl)


FLIPS = [(fx, fy, fc) for fx in (0, 1) for fy in (0, 1) for fc in (0, 1)][1:]


def _mesh_pos():
    return lax.axis_index("x"), lax.axis_index("y"), lax.axis_index("c")


def _peer(pos, flip):
    return tuple((1 - p) if f else p for p, f in zip(pos, flip))


def _dev_index(pos):
    return 4 * pos[0] + 2 * pos[1] + pos[2]


class _Exchange:
    def __init__(self, tensors, scatter):
        self.tensors, self.scatter, self.nt = list(tensors), list(scatter), len(tensors)
        hbm = pl.BlockSpec(memory_space=pltpu.HBM)
        self.in_specs = [hbm] * self.nt
        self.out_specs = [hbm] * self.nt
        self.out_shape = [jax.ShapeDtypeStruct(x.shape if sc else (N_DEV,) + x.shape, x.dtype)
                          for x, sc in zip(tensors, scatter)]
        self.scratch_shapes = [pltpu.SemaphoreType.DMA((self.nt * 7,)), pltpu.SemaphoreType.DMA((self.nt * 7,)),
                               pltpu.SemaphoreType.DMA((self.nt,))]

    def _copies(self, ins, outs, sems):
        send_sems, recv_sems, local_sems = sems
        pos = _mesh_pos()
        me = _dev_index(pos)

        def remote(ti, fi, landing):
            peer = _peer(pos, FLIPS[fi])
            src = ins[ti].at[_dev_index(peer)] if self.scatter[ti] else ins[ti]
            return pltpu.make_async_remote_copy(
                src_ref=src, dst_ref=outs[ti].at[landing(peer)],
                send_sem=send_sems.at[ti * 7 + fi], recv_sem=recv_sems.at[ti * 7 + fi],
                device_id=peer, device_id_type=pl.DeviceIdType.MESH)

        pairs = [(ti, fi) for ti in range(self.nt) for fi in range(7)]
        local = [pltpu.make_async_copy(ins[ti].at[me] if self.scatter[ti] else ins[ti], outs[ti].at[me],
                                       local_sems.at[ti]) for ti in range(self.nt)]
        sends = [remote(ti, fi, lambda peer: me) for ti, fi in pairs]
        recvs = [remote(ti, fi, _dev_index) for ti, fi in pairs]
        return local, sends, recvs

    def start(self, ins, outs, sems):
        local, sends, _ = self._copies(ins, outs, sems)
        for cp in local + sends:
            cp.start()

    def wait(self, ins, outs, sems):
        local, sends, recvs = self._copies(ins, outs, sems)
        for cp in recvs:
            cp.wait_recv()
        for cp in sends:
            cp.wait_send()
        for cp in local:
            cp.wait()


def _exchange(tensors, scatter, name):
    ex = _Exchange(tensors, scatter)

    def body(*refs):
        ins, outs, sems = refs[:ex.nt], refs[ex.nt:2 * ex.nt], refs[2 * ex.nt:]
        ex.start(ins, outs, sems)
        ex.wait(ins, outs, sems)

    return pl.pallas_call(
        body, name=name, in_specs=ex.in_specs, out_specs=ex.out_specs, out_shape=ex.out_shape,
        scratch_shapes=ex.scratch_shapes, compiler_params=pltpu.CompilerParams(has_side_effects=True),
    )(*tensors)


def _gather_two_level(tensors, name):
    nt = len(tensors)
    hbm = pl.BlockSpec(memory_space=pltpu.HBM)

    def body(*refs):
        ins, outs = refs[:nt], refs[nt:2 * nt]
        send_sems, recv_sems, local_sems = refs[2 * nt:]
        x, y, c = _mesh_pos()
        sibling = (x, y, 1 - c)
        chips = [(1 - x, y), (x, 1 - y), (1 - x, 1 - y)]

        def copy(ti, k, block, to, own=False):
            slot = outs[ti].at[_dev_index(block)]
            return pltpu.make_async_remote_copy(
                src_ref=ins[ti] if own else slot, dst_ref=slot,
                send_sem=send_sems.at[ti * 7 + k], recv_sem=recv_sems.at[ti * 7 + k],
                device_id=to, device_id_type=pl.DeviceIdType.MESH)

        me = (x, y, c)
        mine = [pltpu.make_async_copy(ins[ti], outs[ti].at[_dev_index(me)], local_sems.at[ti]) for ti in range(nt)]
        first = [copy(ti, 0, me, sibling, own=True) for ti in range(nt)]
        first += [copy(ti, 1 + j, me, (*chip, c), own=True) for ti in range(nt) for j, chip in enumerate(chips)]
        for cp in mine + first:
            cp.start()
        passed = []
        for j, chip in enumerate(chips):
            for ti in range(nt):
                copy(ti, 1 + j, (*chip, c), me).wait_recv()
                cp = copy(ti, 4 + j, (*chip, c), sibling)
                cp.start()
                passed.append(cp)
        for ti in range(nt):
            copy(ti, 0, sibling, me).wait_recv()
            for j, chip in enumerate(chips):
                copy(ti, 4 + j, (*chip, 1 - c), me).wait_recv()
        for cp in first + passed:
            cp.wait_send()
        for cp in mine:
            cp.wait()

    return pl.pallas_call(
        body, name=name, in_specs=[hbm] * nt, out_specs=[hbm] * nt,
        out_shape=[jax.ShapeDtypeStruct((N_DEV,) + x.shape, x.dtype) for x in tensors],
        scratch_shapes=[pltpu.SemaphoreType.DMA((nt * 7,)), pltpu.SemaphoreType.DMA((nt * 7,)),
                        pltpu.SemaphoreType.DMA((nt,))],
        compiler_params=pltpu.CompilerParams(has_side_effects=True),
    )(*tensors)


def _hosted_call(body, *, name, grid, in_specs, out_specs, out_shape, scratch_shapes, args, dims, exch=None):
    if exch is None:
        return pl.pallas_call(body, name=name, grid=grid, in_specs=in_specs, out_specs=out_specs,
                              out_shape=out_shape, scratch_shapes=scratch_shapes,
                              compiler_params=_cp(*dims))(*args)
    n_in, n_out, n_sc, ne = len(in_specs), len(out_specs), len(scratch_shapes), exch.nt
    nsteps = math.prod(grid)

    def wrapped(*refs):
        ins, ex_in = refs[:n_in], refs[n_in:n_in + ne]
        outs = refs[n_in + ne:n_in + ne + n_out]
        ex_out = refs[n_in + ne + n_out:n_in + 2 * ne + n_out]
        rest = refs[n_in + 2 * ne + n_out:]
        scratch, sems = rest[:n_sc], rest[n_sc:]
        step = pl.program_id(0)
        for ax in range(1, len(grid)):
            step = step * grid[ax] + pl.program_id(ax)

        @pl.when(step == 0)
        def _():
            exch.start(ex_in, ex_out, sems)

        body(*ins, *outs, *scratch)

        @pl.when(step == nsteps - 1)
        def _():
            exch.wait(ex_in, ex_out, sems)

    return pl.pallas_call(
        wrapped, name=name, grid=grid, in_specs=list(in_specs) + exch.in_specs,
        out_specs=list(out_specs) + exch.out_specs, out_shape=list(out_shape) + exch.out_shape,
        scratch_shapes=list(scratch_shapes) + exch.scratch_shapes,
        compiler_params=pltpu.CompilerParams(dimension_semantics=("arbitrary",) * len(grid),
                                             vmem_limit_bytes=VMEM_LIMIT, has_side_effects=True),
    )(*args, *exch.tensors)


def _adamw(land, w, m, v, name, tm=256):
    n, r, c = land.shape
    tm = r if r <= tm else max(s for s in range(8, tm + 1, 8) if r % s == 0)
    bc1 = 1.0 / (1.0 - ADAM_B1 ** ADAM_STEP)
    bc2 = 1.0 / (1.0 - ADAM_B2 ** ADAM_STEP)

    def body(l_ref, w_ref, m_ref, v_ref, g_ref, d_ref, nm_ref, nv_ref):
        g = l_ref[0].astype(F32)
        for i in range(1, n):
            g = g + l_ref[i].astype(F32)
        nm = ADAM_B1 * m_ref[...] + (1.0 - ADAM_B1) * g
        nv = ADAM_B2 * v_ref[...] + (1.0 - ADAM_B2) * (g * g)
        g_ref[...] = g
        nm_ref[...] = nm
        nv_ref[...] = nv
        d_ref[...] = -ADAM_LR * ((nm * bc1) / (jnp.sqrt(nv * bc2) + ADAM_EPS) + ADAM_WD * w_ref[...])

    spec = pl.BlockSpec((tm, c), lambda i: (i, 0))
    return pl.pallas_call(
        body, name=name, grid=(r // tm,),
        in_specs=[pl.BlockSpec((n, tm, c), lambda i: (0, i, 0)), spec, spec, spec],
        out_specs=[spec] * 4, out_shape=[jax.ShapeDtypeStruct((r, c), F32)] * 4,
        compiler_params=_cp("parallel"),
    )(land, w, m, v)


PACK_W = 2048


def _pack_rows(parts):
    flat = jnp.concatenate([p.reshape(-1).astype(F32) for p in parts])
    pad = (-flat.shape[0]) % (8 * PACK_W)
    return jnp.pad(flat, (0, pad)).reshape(-1, PACK_W)


def _unpack_rows(packed, shapes):
    flat = packed.reshape(-1)
    out, off = [], 0
    for s in shapes:
        n = math.prod(s)
        out.append(flat[off:off + n].reshape(s))
        off += n
    return out


def _col_slabs(gfull, width):
    r = gfull.shape[0]
    return jnp.transpose(gfull.reshape(r, N_DEV, width), (1, 0, 2)).astype(BF16)


def _row_slabs(gfull):
    return gfull.reshape(N_DEV, gfull.shape[0] // N_DEV, gfull.shape[1]).astype(BF16)


def _from_col_slabs(gathered):
    n, r, width = gathered.shape
    return jnp.transpose(gathered, (1, 0, 2)).reshape(r, n * width)


def _local_step(xs, target, norm_mix, wf_in, cw, a_log, dt_bias, dn_norm, rest, norm_ffn, norm_final,
                distributed=True):
    d = D_MODEL
    n_main = D_PA + 4 * D_B
    w_pa_cols = wf_in[:, :D_PA]
    w_pb_cols = jnp.concatenate([wf_in[:, D_PA:n_main], wf_in[:, n_main + 2 * N_HEADS_B:]], axis=1)
    w_small = jnp.pad(wf_in[:, n_main:n_main + 2 * N_HEADS_B], ((0, 0), (0, HEAD - 2 * N_HEADS_B)))
    a_log_bc = jnp.repeat(a_log, HEAD, axis=1)
    dt_bias_bc = jnp.repeat(dt_bias, HEAD, axis=1)

    u = _rms_fwd(xs, norm_mix)
    pa = _matmul(u, w_pa_cols, "nn", F32, 1024, 1536, d, name="proj_a")
    pb = _matmul(u, w_pb_cols, "nn", F32, 1024, 1024, d, name="proj_b")
    ps = _matmul(u, w_small, "nn", F32, 2048, HEAD, d, name="proj_small")
    os_, ls_ = [], []
    for gi, dil in enumerate(DILATIONS):
        o_g, l_g = _attn_fwd_group(pa, gi, dil)
        os_.append(o_g)
        ls_.append(l_g)
    y_att, lse = _attn_merge(os_, ls_)
    prep = _dn_prep_fwd(pb, ps, cw, a_log_bc, dt_bias_bc,
                        exch=_Exchange(rest, [False] * 6) if distributed else None)
    qn, kn, vn, gdec, beta = prep[:5]
    if distributed:
        g_pa, g_pd, g_out, g_gate, g_up, g_down = prep[5:]
        wf_pa, wf_pd, wf_out = _from_col_slabs(g_pa), g_pd.reshape(D_B, d), g_out.reshape(d, d)
        wf_gate, wf_up, wf_down = _from_col_slabs(g_gate), _from_col_slabs(g_up), g_down.reshape(D_FF, d)
    else:
        wf_pa, wf_pd, wf_out, wf_gate, wf_up, wf_down = rest
    wf_gu = jnp.concatenate([wf_gate, wf_up], axis=1)
    ub, ww, qd, kd, qk, tinv, gl = _dn_chunk_fwd(qn, kn, vn, gdec, beta)
    o_dn, states = _dn_scan_fwd(ub, ww, qd, kd, qk, gl)
    o_gated = _head_norm_fwd(o_dn, pb, dn_norm)
    y_a = _matmul(y_att, wf_pa, "nn", F32, 1024, d, D_ATTN_OUT, name="proj_attn")
    y_b = _matmul(o_gated, wf_pd, "nn", F32, 1024, d, d, name="proj_delta")
    merged = _gate_merge_fwd(pb, y_a, y_b)
    h1 = _matmul(merged, wf_out, "nn", F32, 1024, d, d, add=xs, name="out_proj")
    hn = _rms_fwd(h1, norm_ffn)
    gate, up, act = _ffn_in(hn, wf_gate, wf_up)
    h2 = _matmul(act, wf_down, "nn", F32, 512, d, D_FF, add=h1, name="ffn_out")
    loss_part, dh2, d_norm_final = _final_loss(h2, norm_final.reshape(1, d), target)

    dgate, dup = _ffn_act_bwd(dh2, wf_down, gate, up)
    gw_down = _matmul(act, dh2, "tn", BF16, 1408, d, TOKEN_TK, name="gw_down")
    dhn = _matmul_nt_segments([([dgate, dup], wf_gu, 1408)], BF16, 1024, "d_hn")[0]
    gw_gate = _matmul(hn, dgate, "tn", BF16, d, 1408, TOKEN_TK, name="gw_gate")
    gw_up = _matmul(hn, dup, "tn", BF16, d, 1408, TOKEN_TK, name="gw_up")
    dh1, d_norm_ffn = _rms_bwd(h1, norm_ffn, dhn, dh2)
    dmerged = _matmul(dh1, wf_out, "nt", BF16, 1024, d, d, name="d_merged")
    gw_out = _matmul(merged, dh1, "tn", BF16, d, d, TOKEN_TK, name="gw_out")
    dya, dyb, dga, dgb = _gate_merge_bwd(pb, y_a, y_b, dmerged)
    dy_att = _matmul(dya, wf_pa, "nt", F32, 1024, D_ATTN_OUT, d, name="d_y_att")
    gw_pa = _matmul(y_att, dya, "tn", BF16, D_ATTN_OUT, d, TOKEN_TK, name="gw_pa")
    do_gated = _matmul(dyb, wf_pd, "nt", BF16, 1024, d, d, name="d_o_gated")
    gw_pd = _matmul(o_gated, dyb, "tn", BF16, d, d, TOKEN_TK, name="gw_pd")
    do_dn, dz, d_dn_norm = _head_norm_bwd(o_dn, pb, dn_norm, do_gated)
    dub, dww, dqd, dkd, dqk, dgl = _dn_scan_bwd(ub, ww, qd, kd, qk, gl, states, do_dn)
    dqn, dkn, dvn, dgdec, dbeta = _dn_chunk_bwd(qn, kn, vn, gdec, beta, ub, ww, tinv, dub, dww, dqd, dkd, dqk, dgl)
    slabs = [_col_slabs(gw_pa, d // N_DEV), _row_slabs(gw_pd), _row_slabs(gw_out),
             _col_slabs(gw_gate, D_FF // N_DEV), _col_slabs(gw_up, D_FF // N_DEV),
             _row_slabs(gw_down)] if distributed else None
    prep = _dn_prep_bwd(pb, ps, cw, a_log_bc, dt_bias_bc, gdec, dqn, dkn, dvn, dgdec, dbeta,
                        exch=_Exchange(slabs, [True] * 6) if distributed else None)
    dc, dps, d_conv_full, d_alog_bc, d_dt_bc = prep[:5]
    dqkv_pre = _conv_bwd_input(dc, cw)
    segs = [_attn_bwd_group(pa, dy_att, y_att, lse, gi, dil) for gi, dil in enumerate(DILATIONS)]
    segs += [dqkv_pre, dz, dga, dgb]
    gws = [_matmul(u, s, "tn", BF16, d, 1536, TOKEN_TK, name=f"gw_in_{i}") for i, s in enumerate(segs)]
    gw_small = _matmul(u, dps, "tn", BF16, d, HEAD, TOKEN_TK, name="gw_in_small")
    g_att = jnp.concatenate(gws[:3], axis=1).reshape(d, N_HEADS_A, 3, HEAD)
    gw_in = jnp.concatenate(
        [g_att[:, :, i, :].reshape(d, D_ATTN) for i in range(3)]
        + [gws[3], gws[4], gw_small[:, :2 * N_HEADS_B], gws[5], gws[6]], axis=1)
    w_att = jnp.stack([w_pa_cols[:, i * D_ATTN:(i + 1) * D_ATTN].reshape(d, N_HEADS_A, HEAD) for i in range(3)],
                      axis=2).reshape(d, D_PA)
    du_small = _matmul(dps, w_small, "nt", F32, 1024, d, HEAD, name="d_u_small")
    du_call = _matmul_nt_segments(
        [(segs[:4], jnp.concatenate([w_att, w_pb_cols[:, :3 * D_B]], axis=1), 768),
         (segs[4:], w_pb_cols[:, 3 * D_B:], 1024)], BF16, 1024, "d_u", add=du_small,
        exch=_Exchange([_col_slabs(gw_in, SHARD_IN)], [True]) if distributed else None)
    dx, d_norm_mix = _rms_bwd(xs, norm_mix, du_call[0], dh1)
    d_a_log = d_alog_bc.reshape(1, N_HEADS_B, HEAD)[:, :, 0]
    d_dt_bias = d_dt_bc.reshape(1, N_HEADS_B, HEAD)[:, :, 0]
    small = (d_conv_full, d_norm_mix, d_norm_ffn, d_norm_final, d_dn_norm, d_a_log, d_dt_bias)
    if distributed:
        return (loss_part, dx, [du_call[1]] + list(prep[5:])) + small
    return (loss_part, dx, gw_in, gw_pa, gw_pd, gw_out, jnp.concatenate([gw_gate, gw_up], axis=1), gw_down) + small


def kernel(x, norm_mix, w_in, conv_w, a_log, dt_bias, dn_norm, w_proj_attn, w_proj_delta, w_out, norm_ffn, w_gate, w_up, w_down, norm_final, loss_target, m_norm_mix, m_w_in, m_conv_w, m_a_log, m_dt_bias, m_dn_norm, m_w_proj_attn, m_w_proj_delta, m_w_out, m_norm_ffn, m_w_gate, m_w_up, m_w_down, m_norm_final, v_norm_mix, v_w_in, v_conv_w, v_a_log, v_dt_bias, v_dn_norm, v_w_proj_attn, v_w_proj_delta, v_w_out, v_norm_ffn, v_w_gate, v_w_up, v_w_down, v_norm_final):
    d = D_MODEL
    xs = x[0]
    target = loss_target[0]
    me = _dev_index(_mesh_pos())

    g_in, g_conv = _gather_two_level([w_in[0].astype(BF16), conv_w[0]], "gather_w_in")
    rest = [w[0].astype(BF16) for w in (w_proj_attn, w_proj_delta, w_out, w_gate, w_up, w_down)]
    (loss_part, dx, landed, d_conv_full, d_norm_mix, d_norm_ffn, d_norm_final, d_dn_norm, d_a_log,
     d_dt_bias) = _local_step(xs, target, norm_mix, _from_col_slabs(g_in), _from_col_slabs(g_conv), a_log, dt_bias,
                              dn_norm, rest, norm_ffn, norm_final)

    small_shapes = [(1, d), (1, d), (d,), (1, HEAD), (1, N_HEADS_B), (1, N_HEADS_B), (1, 1), (CONV_WIDTH, 3 * D_B)]
    packed = _pack_rows([d_norm_mix, d_norm_ffn, d_norm_final, d_dn_norm, d_a_log, d_dt_bias,
                         loss_part[:, :1], d_conv_full])
    landed = list(landed) + list(_exchange([packed], [False], "gather_small_grads"))
    zero1 = jnp.zeros((1, 1), F32)
    zconv = jnp.zeros((CONV_WIDTH, 3 * D_B), F32)
    small_w = _pack_rows([norm_mix, norm_ffn, norm_final, dn_norm, a_log, dt_bias, zero1, zconv])
    small_m = _pack_rows([m_norm_mix, m_norm_ffn, m_norm_final, m_dn_norm, m_a_log, m_dt_bias, zero1, zconv])
    small_v = _pack_rows([v_norm_mix, v_norm_ffn, v_norm_final, v_dn_norm, v_a_log, v_dt_bias, zero1, zconv])
    small = [_unpack_rows(z, small_shapes) for z in _adamw(landed[7], small_w, small_m, small_v, "adamw_small")]
    loss = small[0][6].reshape(())
    conv_shard = 3 * D_B // N_DEV
    g_conv_own = lax.dynamic_slice_in_dim(small[0][7], me * conv_shard, conv_shard, axis=1)
    r_conv = _adamw(g_conv_own[None], conv_w[0], m_conv_w[0], v_conv_w[0], "adamw_conv")
    big = [_adamw(landed[i], w[0], m[0], v[0], f"adamw_{i}") for i, (w, m, v) in enumerate([
        (w_in, m_w_in, v_w_in), (w_proj_attn, m_w_proj_attn, v_w_proj_attn),
        (w_proj_delta, m_w_proj_delta, v_w_proj_delta), (w_out, m_w_out, v_w_out),
        (w_gate, m_w_gate, v_w_gate), (w_up, m_w_up, v_w_up), (w_down, m_w_down, v_w_down)])]

    def leaves(k):
        sm = small[k]
        return [sm[0], big[0][k][None], r_conv[k][None], sm[4], sm[5], sm[3], big[1][k][None], big[2][k][None],
                big[3][k][None], sm[1], big[4][k][None], big[5][k][None], big[6][k][None], sm[2]]

    return (loss, dx[None], *leaves(0), *leaves(1), *leaves(2), *leaves(3))
```

```python
import math

import jax
import jax.numpy as jnp
from jax import lax
from jax.experimental import pallas as pl
from jax.experimental.pallas import tpu as pltpu

F32 = jnp.float32
BF16 = jnp.bfloat16
HI = lax.Precision.HIGH

D_MODEL = 1024
N_DEV = 8
HEAD = 128
N_HEADS_A = 12
HEADS_PER_GROUP = 4
DILATIONS = (1, 4, 16)
BLOCK_A = 128
D_ATTN = N_HEADS_A * HEAD
D_ATTN_OUT = HEADS_PER_GROUP * HEAD
N_HEADS_B = 8
D_B = N_HEADS_B * HEAD
CONV_WIDTH = 4
CH = 64
CHUNK_GROUP = 32
SCAN_HEADS = 4
TOKEN_TK = 1024
D_FF = 2816
EPS = 1e-6
D_IN = 3 * D_ATTN + 4 * D_B + 2 * N_HEADS_B + 2 * D_MODEL
SHARD_IN = D_IN // N_DEV
PB_Z, PB_GATE = 3072, 4096
D_PA = 3 * D_ATTN
ADAM_LR, ADAM_B1, ADAM_B2, ADAM_EPS, ADAM_WD, ADAM_STEP = 0.001, 0.9, 0.999, 1e-08, 0.01, 10
VMEM_LIMIT = 56 * 1024 * 1024

NN = ((1,), (0,))
NT = ((1,), (1,))
TN = ((0,), (0,))


def _dot(a, b, dims=NN, prec=None):
    return lax.dot_general(a, b, (dims, ((), ())), precision=prec, preferred_element_type=F32)


def _bdot(a, b, dims=NN):
    return _dot(a.astype(BF16), b.astype(BF16), dims)


def _hdot(a, b, dims=NN):
    return _dot(a.astype(F32), b.astype(F32), dims, HI)


def _cp(*sem):
    return pltpu.CompilerParams(dimension_semantics=sem, vmem_limit_bytes=VMEM_LIMIT)


def _sigmoid(x):
    return 0.5 * jnp.tanh(0.5 * x) + 0.5


def _softplus(x):
    return jnp.maximum(x, 0.0) + jnp.log(1.0 + jnp.exp(-jnp.abs(x)))


def _rowsum(x):
    return jnp.sum(x, axis=-1, keepdims=True)


def _matmul(a, b, mode, out_dtype, tm, tn, tk, add=None, name="mm"):
    if mode == "nn":
        (m, k), (k2, n) = a.shape, b.shape
    elif mode == "nt":
        (m, k), (n, k2) = a.shape, b.shape
    else:
        (k, m), (k2, n) = a.shape, b.shape
    assert k == k2, (a.shape, b.shape, mode)
    tm, tn, tk = min(tm, m), min(tn, n), min(tk, k)
    assert m % tm == 0 and n % tn == 0 and k % tk == 0, (a.shape, b.shape, tm, tn, tk)
    nk = k // tk
    dims = {"nn": NN, "nt": NT, "tn": TN}[mode]

    def body(*refs):
        if add is None:
            a_ref, b_ref, o_ref, acc = refs
            add_ref = None
        else:
            a_ref, b_ref, add_ref, o_ref, acc = refs
        kk = pl.program_id(2)

        @pl.when(kk == 0)
        def _():
            acc[...] = jnp.zeros_like(acc)

        acc[...] += _bdot(a_ref[...], b_ref[...], dims)

        @pl.when(kk == nk - 1)
        def _():
            r = acc[...]
            if add_ref is not None:
                r = r + add_ref[...].astype(F32)
            o_ref[...] = r.astype(out_dtype)

    a_spec = (pl.BlockSpec((tk, tm), lambda i, j, kk: (kk, i)) if mode == "tn"
              else pl.BlockSpec((tm, tk), lambda i, j, kk: (i, kk)))
    b_spec = (pl.BlockSpec((tn, tk), lambda i, j, kk: (j, kk)) if mode == "nt"
              else pl.BlockSpec((tk, tn), lambda i, j, kk: (kk, j)))
    in_specs = [a_spec, b_spec]
    args = [a, b]
    if add is not None:
        in_specs.append(pl.BlockSpec((tm, tn), lambda i, j, kk: (i, j)))
        args.append(add)
    return pl.pallas_call(
        body, name=name, grid=(m // tm, n // tn, nk),
        in_specs=in_specs, out_specs=pl.BlockSpec((tm, tn), lambda i, j, kk: (i, j)),
        out_shape=jax.ShapeDtypeStruct((m, n), out_dtype),
        scratch_shapes=[pltpu.VMEM((tm, tn), F32)],
        compiler_params=_cp("parallel", "parallel", "arbitrary"),
    )(*args)


def _matmul_nt_segments(groups, out_dtype, tm, name, add=None, exch=None):
    m = groups[0][0][0].shape[0]
    n = groups[0][1].shape[0]
    assert m % tm == 0
    seg_list, b_list = [], []
    step = 0
    for gi, (segs, b, tk) in enumerate(groups):
        assert all(s.shape[1] % tk == 0 for s in segs) and sum(s.shape[1] for s in segs) == b.shape[1]
        g0 = step
        for s in segs:
            seg_list.append((s, tk, step, s.shape[1] // tk, gi))
            step += s.shape[1] // tk
        b_list.append((b, tk, g0, step - g0))
    nk = step
    ns, nb = len(seg_list), len(b_list)

    def body(*refs):
        seg_refs, b_refs = refs[:ns], refs[ns:ns + nb]
        add_ref = refs[ns + nb] if add is not None else None
        o_ref, acc = refs[-2], refs[-1]
        kk = pl.program_id(1)

        @pl.when(kk == 0)
        def _():
            acc[...] = jnp.zeros_like(acc)

        for a_ref, (_, _, k0, nk_s, gi) in zip(seg_refs, seg_list):
            @pl.when(jnp.logical_and(kk >= k0, kk < k0 + nk_s))
            def _(a_ref=a_ref, b_ref=b_refs[gi]):
                acc[...] += _bdot(a_ref[...], b_ref[...], NT)

        @pl.when(kk == nk - 1)
        def _():
            r = acc[...]
            if add_ref is not None:
                r = r + add_ref[...].astype(F32)
            o_ref[...] = r.astype(out_dtype)

    def walk(rows, tk, k0, nk_s, row_axis):
        if row_axis:
            return pl.BlockSpec((rows, tk), lambda i, kk: (i, jnp.clip(kk - k0, 0, nk_s - 1)))
        return pl.BlockSpec((rows, tk), lambda i, kk: (0, jnp.clip(kk - k0, 0, nk_s - 1)))

    row = pl.BlockSpec((tm, n), lambda i, kk: (i, 0))
    in_specs = [walk(tm, tk, k0, nk_s, True) for _, tk, k0, nk_s, _ in seg_list]
    in_specs += [walk(n, tk, k0, nk_s, False) for _, tk, k0, nk_s in b_list]
    args = [s[0] for s in seg_list] + [b[0] for b in b_list]
    if add is not None:
        in_specs.append(row)
        args.append(add)
    return _hosted_call(body, name=name, grid=(m // tm, nk), in_specs=in_specs, out_specs=[row],
                        out_shape=[jax.ShapeDtypeStruct((m, n), out_dtype)],
                        scratch_shapes=[pltpu.VMEM((tm, n), F32)], args=args,
                        dims=("parallel", "arbitrary"), exch=exch)


def _row_spec(tm, cols, cb=0):
    return pl.BlockSpec((tm, cols), lambda i, cb=cb: (i, cb))


def _bcast_spec(rows, cols):
    return pl.BlockSpec((rows, cols), lambda i: (0, 0))


def _rms_fwd(x, w, tm=512):
    t, d = x.shape

    def body(x_ref, w_ref, o_ref):
        xv = x_ref[...]
        r = lax.rsqrt(jnp.mean(xv * xv, axis=-1, keepdims=True) + EPS)
        o_ref[...] = (xv * r * w_ref[...]).astype(BF16)

    return pl.pallas_call(
        body, name="rms_fwd", grid=(t // tm,),
        in_specs=[_row_spec(tm, d), _bcast_spec(1, d)], out_specs=_row_spec(tm, d),
        out_shape=jax.ShapeDtypeStruct((t, d), BF16), compiler_params=_cp("parallel"),
    )(x, w)


def _rms_bwd(x, w, dy, resid, tm=512):
    t, d = x.shape

    def body(x_ref, w_ref, dy_ref, res_ref, dx_ref, dw_ref):
        xv = x_ref[...]
        r = lax.rsqrt(jnp.mean(xv * xv, axis=-1, keepdims=True) + EPS)
        xh = xv * r
        dyv = dy_ref[...].astype(F32)
        dxh = dyv * w_ref[...]
        dx_ref[...] = res_ref[...] + r * (dxh - xh * jnp.mean(dxh * xh, axis=-1, keepdims=True))

        @pl.when(pl.program_id(0) == 0)
        def _():
            dw_ref[...] = jnp.zeros_like(dw_ref)

        dw_ref[...] += jnp.sum(dyv * xh, axis=0, keepdims=True)

    return pl.pallas_call(
        body, name="rms_bwd", grid=(t // tm,),
        in_specs=[_row_spec(tm, d), _bcast_spec(1, d), _row_spec(tm, d), _row_spec(tm, d)],
        out_specs=[_row_spec(tm, d), _bcast_spec(1, d)],
        out_shape=[jax.ShapeDtypeStruct((t, d), F32), jax.ShapeDtypeStruct((1, d), F32)],
        compiler_params=_cp("arbitrary"),
    )(x, w, dy, resid)


def _final_loss(h, w, target, tm=512):
    t, d = h.shape

    def body(h_ref, w_ref, t_ref, loss_ref, dh_ref, dw_ref):
        hv = h_ref[...]
        r = lax.rsqrt(jnp.mean(hv * hv, axis=-1, keepdims=True) + EPS)
        xh = hv * r
        wv = w_ref[...]
        err = xh * wv - t_ref[...]
        dy = err * (1.0 / d)
        dxh = dy * wv
        dh_ref[...] = r * (dxh - xh * jnp.mean(dxh * xh, axis=-1, keepdims=True))

        @pl.when(pl.program_id(0) == 0)
        def _():
            dw_ref[...] = jnp.zeros_like(dw_ref)
            loss_ref[...] = jnp.zeros_like(loss_ref)

        dw_ref[...] += jnp.sum(dy * xh, axis=0, keepdims=True)
        part = 0.5 * jnp.sum(jnp.mean(err * err, axis=-1, keepdims=True), axis=0, keepdims=True)
        loss_ref[...] += part + jnp.zeros((1, HEAD), F32)

    return pl.pallas_call(
        body, name="final_loss", grid=(t // tm,),
        in_specs=[_row_spec(tm, d), _bcast_spec(1, d), _row_spec(tm, d)],
        out_specs=[_bcast_spec(1, HEAD), _row_spec(tm, d), _bcast_spec(1, d)],
        out_shape=[jax.ShapeDtypeStruct((1, HEAD), F32), jax.ShapeDtypeStruct((t, d), F32),
                   jax.ShapeDtypeStruct((1, d), F32)],
        compiler_params=_cp("arbitrary"),
    )(h, w, target)


def _ffn_in(hn, w_gate, w_up, tm=1024, tn=1408):
    t, d = hn.shape
    ff = w_gate.shape[1]

    def body(a_ref, wg_ref, wu_ref, g_ref, u_ref, act_ref):
        a = a_ref[...]
        g = _bdot(a, wg_ref[...])
        u = _bdot(a, wu_ref[...])
        g_ref[...] = g.astype(BF16)
        u_ref[...] = u.astype(BF16)
        gq = g.astype(BF16).astype(F32)
        act_ref[...] = (gq * _sigmoid(gq) * u.astype(BF16).astype(F32)).astype(BF16)

    tile = pl.BlockSpec((tm, tn), lambda i, j: (i, j))
    wspec = pl.BlockSpec((d, tn), lambda i, j: (0, j))
    return pl.pallas_call(
        body, name="ffn_in", grid=(t // tm, ff // tn),
        in_specs=[pl.BlockSpec((tm, d), lambda i, j: (i, 0)), wspec, wspec], out_specs=[tile] * 3,
        out_shape=[jax.ShapeDtypeStruct((t, ff), BF16)] * 3, compiler_params=_cp("parallel", "parallel"),
    )(hn, w_gate, w_up)


def _ffn_act_bwd(dh, w_down, g, u, tm=1024, tn=1408):
    t, d = dh.shape
    ff = w_down.shape[0]

    def body(a_ref, w_ref, g_ref, u_ref, dg_ref, du_ref):
        dv = _bdot(a_ref[...], w_ref[...], NT).astype(BF16).astype(F32)
        gv = g_ref[...].astype(F32)
        sg = _sigmoid(gv)
        dg_ref[...] = (dv * u_ref[...].astype(F32) * (sg + gv * sg * (1.0 - sg))).astype(BF16)
        du_ref[...] = (dv * gv * sg).astype(BF16)

    tile = pl.BlockSpec((tm, tn), lambda i, j: (i, j))
    return pl.pallas_call(
        body, name="ffn_act_bwd", grid=(t // tm, ff // tn),
        in_specs=[pl.BlockSpec((tm, d), lambda i, j: (i, 0)), pl.BlockSpec((tn, d), lambda i, j: (j, 0)),
                  tile, tile],
        out_specs=[tile] * 2, out_shape=[jax.ShapeDtypeStruct((t, ff), BF16)] * 2,
        compiler_params=_cp("parallel", "parallel"),
    )(dh, w_down, g, u)


def _gate_merge_fwd(pb, ya, yb, tm=512):
    t, d = ya.shape
    cb = PB_GATE // d

    def body(ga_ref, gb_ref, ya_ref, yb_ref, o_ref):
        ga, gb = ga_ref[...].astype(F32), gb_ref[...].astype(F32)
        o_ref[...] = (_sigmoid(ga) * ya_ref[...].astype(F32) + _sigmoid(gb) * yb_ref[...].astype(F32)).astype(BF16)

    return pl.pallas_call(
        body, name="gate_merge_fwd", grid=(t // tm,),
        in_specs=[_row_spec(tm, d, cb), _row_spec(tm, d, cb + 1), _row_spec(tm, d), _row_spec(tm, d)],
        out_specs=_row_spec(tm, d),
        out_shape=jax.ShapeDtypeStruct((t, d), BF16), compiler_params=_cp("parallel"),
    )(pb, pb, ya, yb)


def _gate_merge_bwd(pb, ya, yb, dm, tm=512):
    t, d = ya.shape
    cb = PB_GATE // d

    def body(ga_ref, gb_ref, ya_ref, yb_ref, dm_ref, dya_ref, dyb_ref, dga_ref, dgb_ref):
        dmv = dm_ref[...].astype(F32)
        sa = _sigmoid(ga_ref[...].astype(F32))
        sb = _sigmoid(gb_ref[...].astype(F32))
        dya_ref[...] = (dmv * sa).astype(BF16)
        dyb_ref[...] = (dmv * sb).astype(BF16)
        dga_ref[...] = (dmv * ya_ref[...].astype(F32) * sa * (1.0 - sa)).astype(BF16)
        dgb_ref[...] = (dmv * yb_ref[...].astype(F32) * sb * (1.0 - sb)).astype(BF16)

    return pl.pallas_call(
        body, name="gate_merge_bwd", grid=(t // tm,),
        in_specs=[_row_spec(tm, d, cb), _row_spec(tm, d, cb + 1), _row_spec(tm, d), _row_spec(tm, d),
                  _row_spec(tm, d)],
        out_specs=[_row_spec(tm, d)] * 4,
        out_shape=[jax.ShapeDtypeStruct((t, d), BF16)] * 4, compiler_params=_cp("parallel"),
    )(pb, pb, ya, yb, dm)


def _head_norm_fwd(o, pb, wn, tm=512):
    t, d = o.shape
    nh = d // HEAD

    def body(o_ref, z_ref, w_ref, out_ref):
        wv = w_ref[...]
        for h in range(nh):
            sl = slice(h * HEAD, (h + 1) * HEAD)
            ov = o_ref[:, sl]
            zv = z_ref[:, sl].astype(F32)
            r = lax.rsqrt(jnp.mean(ov * ov, axis=-1, keepdims=True) + EPS)
            out_ref[:, sl] = (ov * r * wv * (zv * _sigmoid(zv))).astype(BF16)

    return pl.pallas_call(
        body, name="head_norm_fwd", grid=(t // tm,),
        in_specs=[_row_spec(tm, d), _row_spec(tm, d, PB_Z // d), _bcast_spec(1, HEAD)],
        out_specs=_row_spec(tm, d),
        out_shape=jax.ShapeDtypeStruct((t, d), BF16), compiler_params=_cp("parallel"),
    )(o, pb, wn)


def _head_norm_bwd(o, pb, wn, dout, tm=512):
    t, d = o.shape
    nh = d // HEAD

    def body(o_ref, z_ref, w_ref, d_ref, do_ref, dz_ref, dw_ref):
        wv = w_ref[...]
        dw_acc = jnp.zeros((1, HEAD), F32)
        for h in range(nh):
            sl = slice(h * HEAD, (h + 1) * HEAD)
            ov = o_ref[:, sl]
            zv = z_ref[:, sl].astype(F32)
            dv = d_ref[:, sl].astype(F32)
            r = lax.rsqrt(jnp.mean(ov * ov, axis=-1, keepdims=True) + EPS)
            xh = ov * r
            sz = _sigmoid(zv)
            dn = dv * (zv * sz)
            dz_ref[:, sl] = (dv * xh * wv * (sz + zv * sz * (1.0 - sz))).astype(BF16)
            dxh = dn * wv
            do_ref[:, sl] = r * (dxh - xh * jnp.mean(dxh * xh, axis=-1, keepdims=True))
            dw_acc = dw_acc + jnp.sum(dn * xh, axis=0, keepdims=True)

        @pl.when(pl.program_id(0) == 0)
        def _():
            dw_ref[...] = jnp.zeros_like(dw_ref)

        dw_ref[...] += dw_acc

    return pl.pallas_call(
        body, name="head_norm_bwd", grid=(t // tm,),
        in_specs=[_row_spec(tm, d), _row_spec(tm, d, PB_Z // d), _bcast_spec(1, HEAD), _row_spec(tm, d)],
        out_specs=[_row_spec(tm, d), _row_spec(tm, d), _bcast_spec(1, HEAD)],
        out_shape=[jax.ShapeDtypeStruct((t, d), F32), jax.ShapeDtypeStruct((t, d), BF16),
                   jax.ShapeDtypeStruct((1, HEAD), F32)],
        compiler_params=_cp("arbitrary"),
    )(o, pb, wn, dout)


def _attn_bias(gi, hh, dil):
    i = lax.broadcasted_iota(jnp.int32, (BLOCK_A, BLOCK_A), 0)
    j = lax.broadcasted_iota(jnp.int32, (BLOCK_A, BLOCK_A), 1)
    hf = (gi * HEADS_PER_GROUP + hh + 1).astype(F32)
    slope = jnp.exp(jnp.full((1, BLOCK_A), -8.0 * math.log(2.0) / N_HEADS_A, F32) * hf) * float(dil)
    d_prev = (BLOCK_A + i - j).astype(F32)
    d_cur = (i - j).astype(F32)
    return -slope * d_prev, -slope * d_cur, j >= i, j <= i


ATTN_TOKENS = 2048


def _sub_rows(a, r, dil):
    start = a * BLOCK_A * dil + r
    return pl.ds(start, BLOCK_A) if dil == 1 else pl.ds(start, BLOCK_A, stride=dil)


def _attn_fwd_group(pa, gi, dil, tb=ATTN_TOKENS):
    t = pa.shape[0]
    tb = min(tb, t)
    hb = BLOCK_A * dil
    nb = tb // hb
    scale = HEAD ** -0.5

    def body(q_ref, k_ref, v_ref, kp_ref, vp_ref, o_ref, l_ref):
        hh = pl.program_id(0)
        step = pl.program_id(1)
        b_prev, b_cur, m_prev, m_cur = _attn_bias(gi, hh, dil)
        m_first = jnp.logical_and(m_prev, step > 0)
        for r in range(dil):
            kp, vp = kp_ref[_sub_rows(0, r, dil), :], vp_ref[_sub_rows(0, r, dil), :]
            for a in range(nb):
                rows = _sub_rows(a, r, dil)
                q, kc, vc = q_ref[rows, :], k_ref[rows, :], v_ref[rows, :]
                s_p = jnp.where(m_first if a == 0 else m_prev, _bdot(q, kp, NT) * scale + b_prev, -1e30)
                s_c = jnp.where(m_cur, _bdot(q, kc, NT) * scale + b_cur, -1e30)
                m = jnp.maximum(jnp.max(s_p, axis=-1, keepdims=True), jnp.max(s_c, axis=-1, keepdims=True))
                p_p = jnp.exp(s_p - m)
                p_c = jnp.exp(s_c - m)
                den = _rowsum(p_p) + _rowsum(p_c)
                o_ref[rows, :] = (_bdot(p_p, vp) + _bdot(p_c, vc)) / den
                l_ref[rows, :] = (m + jnp.log(den)) + jnp.zeros((BLOCK_A, HEAD), F32)
                kp, vp = kc, vc

    def col(base):
        return lambda hh, s: (s, base + hh)

    def col_prev(base):
        return lambda hh, s: (jnp.maximum(s * nb - 1, 0), base + hh)

    qb, kb, vb = gi * HEADS_PER_GROUP, N_HEADS_A + gi * HEADS_PER_GROUP, 2 * N_HEADS_A + gi * HEADS_PER_GROUP
    ospec = pl.BlockSpec((tb, HEAD), lambda hh, s: (s, hh))
    return pl.pallas_call(
        body, name=f"attn_fwd_g{gi}", grid=(HEADS_PER_GROUP, t // tb),
        in_specs=[pl.BlockSpec((tb, HEAD), col(qb)), pl.BlockSpec((tb, HEAD), col(kb)),
                  pl.BlockSpec((tb, HEAD), col(vb)),
                  pl.BlockSpec((hb, HEAD), col_prev(kb)), pl.BlockSpec((hb, HEAD), col_prev(vb))],
        out_specs=[ospec, ospec],
        out_shape=[jax.ShapeDtypeStruct((t, D_ATTN_OUT), F32)] * 2,
        compiler_params=_cp("parallel", "parallel"),
    )(pa, pa, pa, pa, pa)


def _attn_merge(os, ls, tm=512):
    t, d = os[0].shape

    def body(o0, o1, o2, l0, l1, l2, y_ref, lse_ref):
        a0, a1, a2 = l0[...], l1[...], l2[...]
        m = jnp.maximum(jnp.maximum(a0, a1), a2)
        e0, e1, e2 = jnp.exp(a0 - m), jnp.exp(a1 - m), jnp.exp(a2 - m)
        den = e0 + e1 + e2
        y_ref[...] = (e0 * o0[...] + e1 * o1[...] + e2 * o2[...]) / den
        lse_ref[...] = m + jnp.log(den)

    return pl.pallas_call(
        body, name="attn_merge", grid=(t // tm,),
        in_specs=[_row_spec(tm, d)] * 6, out_specs=[_row_spec(tm, d)] * 2,
        out_shape=[jax.ShapeDtypeStruct((t, d), F32)] * 2,
        compiler_params=_cp("parallel"),
    )(*os, *ls)


def _attn_bwd_group(pa, dy, y, lse, gi, dil, tb=ATTN_TOKENS):
    t = pa.shape[0]
    tb = min(tb, t)
    hb = BLOCK_A * dil
    nb = tb // hb
    nsteps = t // tb
    scale = HEAD ** -0.5

    def body(q_ref, k_ref, v_ref, dy_ref, y_ref, l_ref, kp_ref, vp_ref, d_ref,
             dq_s, dk_s, dv_s, carry_k, carry_v):
        hh = pl.program_id(0)
        step = pl.program_id(1)

        @pl.when(step == 0)
        def _():
            carry_k[...] = jnp.zeros_like(carry_k)
            carry_v[...] = jnp.zeros_like(carry_v)

        b_prev, b_cur, m_prev, m_cur = _attn_bias(gi, hh, dil)
        m_first = jnp.logical_and(m_prev, step < nsteps - 1)
        for r in range(dil):
            halo = _sub_rows(0, r, dil)
            dk_in, dv_in = carry_k[halo, :], carry_v[halo, :]
            kp, vp = kp_ref[halo, :], vp_ref[halo, :]
            prev_rows = None
            dk_pend = dv_pend = None
            for a in range(nb):
                rows = _sub_rows(a, r, dil)
                q, kc, vc = q_ref[rows, :], k_ref[rows, :], v_ref[rows, :]
                dyb, lb = dy_ref[rows, :], l_ref[rows, :]
                delta = _rowsum(dyb * y_ref[rows, :])
                mp = m_first if a == 0 else m_prev
                s = _bdot(q, kp, NT) * scale + b_prev
                p = jnp.where(mp, jnp.exp(jnp.where(mp, s - lb, 0.0)), 0.0)
                ds = p * (_bdot(dyb, vp, NT) - delta)
                dq = _bdot(ds, kp)
                dk_prev, dv_prev = _bdot(ds, q, TN), _bdot(p, dyb, TN)
                if a == 0:
                    carry_k[halo, :] = dk_prev
                    carry_v[halo, :] = dv_prev
                else:
                    dk_s[prev_rows, :] = dk_pend + dk_prev
                    dv_s[prev_rows, :] = dv_pend + dv_prev
                s = _bdot(q, kc, NT) * scale + b_cur
                p = jnp.where(m_cur, jnp.exp(jnp.where(m_cur, s - lb, 0.0)), 0.0)
                ds = p * (_bdot(dyb, vc, NT) - delta)
                dq_s[rows, :] = dq + _bdot(ds, kc)
                dk_pend, dv_pend = _bdot(ds, q, TN), _bdot(p, dyb, TN)
                prev_rows, kp, vp = rows, kc, vc
            dk_s[prev_rows, :] = dk_pend + dk_in
            dv_s[prev_rows, :] = dv_pend + dv_in
        d_ref[:, :HEAD] = (dq_s[...] * scale).astype(BF16)
        d_ref[:, HEAD:2 * HEAD] = (dk_s[...] * scale).astype(BF16)
        d_ref[:, 2 * HEAD:] = dv_s[...].astype(BF16)

    def col(base):
        return lambda hh, s: (nsteps - 1 - s, base + hh)

    def col_prev(base):
        return lambda hh, s: (jnp.maximum((nsteps - 1 - s) * nb - 1, 0), base + hh)

    qb, kb, vb = gi * HEADS_PER_GROUP, N_HEADS_A + gi * HEADS_PER_GROUP, 2 * N_HEADS_A + gi * HEADS_PER_GROUP
    big, small = (tb, HEAD), (hb, HEAD)
    return pl.pallas_call(
        body, name=f"attn_bwd_g{gi}", grid=(HEADS_PER_GROUP, nsteps),
        in_specs=[pl.BlockSpec(big, col(qb)), pl.BlockSpec(big, col(kb)), pl.BlockSpec(big, col(vb)),
                  pl.BlockSpec(big, col(0)), pl.BlockSpec(big, col(0)), pl.BlockSpec(big, col(0)),
                  pl.BlockSpec(small, col_prev(kb)), pl.BlockSpec(small, col_prev(vb))],
        out_specs=pl.BlockSpec((tb, 3 * HEAD), col(0)),
        out_shape=jax.ShapeDtypeStruct((t, 3 * D_ATTN_OUT), BF16),
        scratch_shapes=[pltpu.VMEM(big, F32)] * 3 + [pltpu.VMEM(small, F32)] * 2,
        compiler_params=_cp("parallel", "arbitrary"),
    )(pa, pa, pa, dy, y, lse, pa, pa)


def _shift_down(cur, prev8, s):
    if s == 0:
        return cur
    rolled = pltpu.roll(cur, s, 0)
    prolled = pltpu.roll(prev8, s, 0)
    rid = lax.broadcasted_iota(jnp.int32, prev8.shape, 0)
    top = jnp.where(rid < s, prolled, rolled[:8])
    return jnp.concatenate([top, rolled[8:]], axis=0)


def _shift_up(cur, next8, s):
    if s == 0:
        return cur
    n = cur.shape[0]
    rolled = pltpu.roll(cur, n - s, 0)
    nrolled = pltpu.roll(next8, 8 - s, 0)
    rid = lax.broadcasted_iota(jnp.int32, next8.shape, 0)
    bottom = jnp.where(rid >= 8 - s, nrolled, rolled[n - 8:])
    return jnp.concatenate([rolled[:n - 8], bottom], axis=0)


def _conv(xv, prev8, wv):
    c = jnp.zeros_like(xv)
    shifted = []
    for s in range(CONV_WIDTH):
        xs = _shift_down(xv, prev8, s)
        shifted.append(xs)
        c = c + wv[CONV_WIDTH - 1 - s:CONV_WIDTH - s, :] * xs
    return c, shifted


def _head_expand(psv, first):
    tm = psv.shape[0]
    return jnp.concatenate([jnp.broadcast_to(psv[:, first + h:first + h + 1], (tm, HEAD))
                            for h in range(N_HEADS_B)], axis=1)


def _head_collect(x, first):
    lane = lax.broadcasted_iota(jnp.int32, (x.shape[0], HEAD), 1)
    out = jnp.zeros((x.shape[0], HEAD), F32)
    for h in range(N_HEADS_B):
        out = jnp.where(lane == first + h, x[:, h * HEAD:(h + 1) * HEAD], out)
    return out


def _dn_prep_fwd(pb, ps, conv_w, a_log_bc, dt_bias_bc, tm=256, exch=None):
    t = pb.shape[0]
    c3 = 3 * D_B
    r16 = tm // 16

    def body(x_ref, xp_ref, ps_ref, w_ref, al_ref, dt_ref, q_ref, k_ref, v_ref, g_ref, beta_ref):
        first = pl.program_id(0) > 0

        def silu_conv(cols):
            c, _ = _conv(x_ref[:, cols].astype(F32), jnp.where(first, xp_ref[8:, cols].astype(F32), 0.0),
                         w_ref[:, cols])
            return c * _sigmoid(c)

        for h in range(N_HEADS_B):
            sl = slice(h * HEAD, (h + 1) * HEAD)
            sq = silu_conv(sl)
            q_ref[:, sl] = sq * lax.rsqrt(_rowsum(sq * sq) + EPS) * (HEAD ** -0.5)
            sk = silu_conv(slice(D_B + h * HEAD, D_B + (h + 1) * HEAD))
            k_ref[:, sl] = sk * lax.rsqrt(_rowsum(sk * sk) + EPS)
            v_ref[:, sl] = silu_conv(slice(2 * D_B + h * HEAD, 2 * D_B + (h + 1) * HEAD))
        psv = ps_ref[...]
        beta_ref[...] = _sigmoid(_head_expand(psv, 0))
        g_ref[...] = -jnp.exp(al_ref[...]) * _softplus(_head_expand(psv, N_HEADS_B) + dt_ref[...])

    return _hosted_call(
        body, name="dn_prep_fwd", grid=(t // tm,),
        in_specs=[_row_spec(tm, c3, 0),
                  pl.BlockSpec((16, c3), lambda i: (jnp.maximum(i * r16 - 1, 0), 0)),
                  _row_spec(tm, HEAD),
                  _bcast_spec(CONV_WIDTH, c3), _bcast_spec(1, D_B), _bcast_spec(1, D_B)],
        out_specs=[_row_spec(tm, D_B)] * 5,
        out_shape=[jax.ShapeDtypeStruct((t, D_B), F32)] * 5, scratch_shapes=[],
        args=(pb, pb, ps, conv_w, a_log_bc, dt_bias_bc), dims=("parallel",), exch=exch)


def _dn_prep_bwd(pb, ps, conv_w, a_log_bc, dt_bias_bc, g, dq, dk, dv, dg, dbeta, tm=128, exch=None):
    t = pb.shape[0]
    c3 = 3 * D_B
    r16 = tm // 16

    def body(x_ref, xp_ref, ps_ref, w_ref, al_ref, dt_ref, g_ref, dq_ref, dk_ref, dv_ref, dg_ref, db_ref,
             dc_ref, dps_ref, dw_ref, dal_ref, ddt_ref):
        first = pl.program_id(0) > 0

        @pl.when(pl.program_id(0) == 0)
        def _():
            dw_ref[...] = jnp.zeros_like(dw_ref)
            dal_ref[...] = jnp.zeros_like(dal_ref)
            ddt_ref[...] = jnp.zeros_like(ddt_ref)

        def column_block(cols, d_ref, sl, mult, normed):
            c, shifted = _conv(x_ref[:, cols].astype(F32), jnp.where(first, xp_ref[8:, cols].astype(F32), 0.0),
                               w_ref[:, cols])
            sg = _sigmoid(c)
            dsilu = sg + c * sg * (1.0 - sg)
            dyv = d_ref[:, sl]
            if normed:
                sv = c * sg
                r = lax.rsqrt(_rowsum(sv * sv) + EPS)
                yh = sv * r
                dyv = dyv * mult
                dyv = r * (dyv - yh * _rowsum(dyv * yh))
            dcv = dyv * dsilu
            dc_ref[:, cols] = dcv
            for sft in range(CONV_WIDTH):
                j = CONV_WIDTH - 1 - sft
                dw_ref[j:j + 1, cols] += jnp.sum(dcv * shifted[sft], axis=0, keepdims=True)

        for h in range(N_HEADS_B):
            sl = slice(h * HEAD, (h + 1) * HEAD)
            column_block(sl, dq_ref, sl, HEAD ** -0.5, True)
            column_block(slice(D_B + h * HEAD, D_B + (h + 1) * HEAD), dk_ref, sl, 1.0, True)
            column_block(slice(2 * D_B + h * HEAD, 2 * D_B + (h + 1) * HEAD), dv_ref, sl, 1.0, False)
        psv = ps_ref[...]
        beta = _sigmoid(_head_expand(psv, 0))
        dgv = dg_ref[...]
        da = dgv * (-jnp.exp(al_ref[...])) * _sigmoid(_head_expand(psv, N_HEADS_B) + dt_ref[...])
        dps_ref[...] = _head_collect(db_ref[...] * beta * (1.0 - beta), 0) + _head_collect(da, N_HEADS_B)
        dal_ref[...] += jnp.sum(dgv * g_ref[...], axis=0, keepdims=True)
        ddt_ref[...] += jnp.sum(da, axis=0, keepdims=True)

    row = _row_spec(tm, D_B)
    return _hosted_call(
        body, name="dn_prep_bwd", grid=(t // tm,),
        in_specs=[_row_spec(tm, c3, 0),
                  pl.BlockSpec((16, c3), lambda i: (jnp.maximum(i * r16 - 1, 0), 0)),
                  _row_spec(tm, HEAD),
                  _bcast_spec(CONV_WIDTH, c3), _bcast_spec(1, D_B), _bcast_spec(1, D_B),
                  row, row, row, row, row, row],
        out_specs=[_row_spec(tm, c3), _row_spec(tm, HEAD), _bcast_spec(CONV_WIDTH, c3), _bcast_spec(1, D_B),
                   _bcast_spec(1, D_B)],
        out_shape=[jax.ShapeDtypeStruct((t, c3), F32), jax.ShapeDtypeStruct((t, HEAD), F32),
                   jax.ShapeDtypeStruct((CONV_WIDTH, c3), F32),
                   jax.ShapeDtypeStruct((1, D_B), F32), jax.ShapeDtypeStruct((1, D_B), F32)],
        scratch_shapes=[], args=(pb, pb, ps, conv_w, a_log_bc, dt_bias_bc, g, dq, dk, dv, dg, dbeta),
        dims=("arbitrary",), exch=exch)


def _conv_bwd_input(dc, conv_w, tm=256):
    t, c3 = dc.shape
    r8 = tm // 8
    nlast = t // 8 - 1
    nsteps = t // tm

    def body(d_ref, dn_ref, w_ref, o_ref):
        not_last = pl.program_id(0) < nsteps - 1
        for cb in range(c3 // HEAD):
            cols = slice(cb * HEAD, (cb + 1) * HEAD)
            next8 = jnp.where(not_last, dn_ref[:, cols], 0.0)
            dv = d_ref[:, cols]
            wv = w_ref[:, cols]
            acc = jnp.zeros_like(dv)
            for s in range(CONV_WIDTH):
                acc = acc + wv[CONV_WIDTH - 1 - s:CONV_WIDTH - s, :] * _shift_up(dv, next8, s)
            o_ref[:, cols] = acc.astype(BF16)

    return pl.pallas_call(
        body, name="conv_bwd_input", grid=(nsteps,),
        in_specs=[_row_spec(tm, c3), pl.BlockSpec((8, c3), lambda i: (jnp.minimum((i + 1) * r8, nlast), 0)),
                  _bcast_spec(CONV_WIDTH, c3)],
        out_specs=_row_spec(tm, c3),
        out_shape=jax.ShapeDtypeStruct((t, c3), BF16), compiler_params=_cp("parallel"),
    )(dc, dc, conv_w)


def _lanes(x):
    return x[:, :CH]


def _tri_inv(a_list, r, c):
    eye = (r == c).astype(F32)
    b16 = (r >> 4) == (c >> 4)
    b32 = (r >> 5) == (c >> 5)
    ns = [jnp.where(b16, -a, 0.0) for a in a_list]
    xs = [eye + n for n in ns]
    ps = [_bdot(n, n) for n in ns]
    for last in (False, False, True):
        xs = [x + _bdot(x, p) for x, p in zip(xs, ps)]
        if not last:
            ps = [_bdot(p, p) for p in ps]
    for mask in (jnp.logical_and(b32, jnp.logical_not(b16)), jnp.logical_not(b32)):
        ts = [_bdot(x, jnp.where(mask, a, 0.0)) for x, a in zip(xs, a_list)]
        xs = [x - _bdot(t, x) for x, t in zip(xs, ts)]
    return xs


def _chunk_local(qs, ks, vs, gs, betas, solved=None):
    r = lax.broadcasted_iota(jnp.int32, (CH, CH), 0)
    c = lax.broadcasted_iota(jnp.int32, (CH, CH), 1)
    incl, strict = r >= c, r > c
    lm = incl.astype(F32)
    cums = [_hdot(lm, jnp.concatenate([g, jnp.where(strict, _lanes(g), 0.0)], axis=1)) for g in gs]
    gcbs = [cm[:, :HEAD] for cm in cums]
    decays = [jnp.where(incl, jnp.exp(jnp.where(incl, cm[:, HEAD:], 0.0)), 0.0) for cm in cums]
    bcols = [_lanes(b) for b in betas]
    kks = [_bdot(k, k, NT) for k in ks]
    qkraws = [_bdot(q, k, NT) for q, k in zip(qs, ks)]
    egs = [jnp.exp(gcb) for gcb in gcbs]
    if solved is None:
        tms = _tri_inv([jnp.where(strict, bc * kk * dc, 0.0) for bc, kk, dc in zip(bcols, kks, decays)], r, c)
        sols = [_hdot(tm, jnp.concatenate([b * v, b * eg * k], axis=1))
                for tm, b, v, eg, k in zip(tms, betas, vs, egs, ks)]
        ubars, ws = [sol[:, :HEAD] for sol in sols], [sol[:, HEAD:] for sol in sols]
    else:
        tms, ubars, ws = solved
    gls = [gcb[CH - 1:CH, :] for gcb in gcbs]
    eks = [jnp.exp(gl - gcb) for gl, gcb in zip(gls, gcbs)]
    return [dict(incl=incl, strict=strict, r=r, c=c, decay=dc, bcol=bc, kk=kk, tm=tm, eg=eg,
                 u_bar=ub, w=w, qkraw=qkraw, gl=gl, ek=ek)
            for dc, bc, kk, tm, eg, ub, w, qkraw, gl, ek
            in zip(decays, bcols, kks, tms, egs, ubars, ws, qkraws, gls, eks)]


def _dn_chunk_fwd(q, k, v, g, beta, cps=CHUNK_GROUP):
    t = q.shape[0]
    tm = cps * CH

    def body(q_ref, k_ref, v_ref, g_ref, b_ref, ub_ref, w_ref, qd_ref, kd_ref, qk_ref, ti_ref, gl_ref):
        for base in range(0, cps, CHUNK_GROUP):
            sls = [slice((base + j) * CH, (base + j + 1) * CH) for j in range(CHUNK_GROUP)]
            qs, ks = [q_ref[sl, :] for sl in sls], [k_ref[sl, :] for sl in sls]
            locs = _chunk_local(qs, ks, [v_ref[sl, :] for sl in sls], [g_ref[sl, :] for sl in sls],
                                [b_ref[sl, :] for sl in sls])
            for j, (sl, qv, kv, loc) in enumerate(zip(sls, qs, ks, locs)):
                ub_ref[sl, :] = loc["u_bar"]
                w_ref[sl, :] = loc["w"]
                qd_ref[sl, :] = qv * loc["eg"]
                kd_ref[sl, :] = kv * loc["ek"]
                qk_ref[sl, :] = loc["qkraw"] * loc["decay"]
                ti_ref[sl, :] = loc["tm"]
                gl_ref[base + j:base + j + 1, :] = jnp.exp(loc["gl"])

    hspec = pl.BlockSpec((tm, HEAD), lambda h, i: (i, h))
    sq_spec = pl.BlockSpec((None, tm, CH), lambda h, i: (h, i, 0))
    sq_shape = jax.ShapeDtypeStruct((N_HEADS_B, t, CH), F32)
    return pl.pallas_call(
        body, name="dn_chunk_fwd", grid=(N_HEADS_B, t // tm),
        in_specs=[hspec] * 5,
        out_specs=[hspec] * 4 + [sq_spec, sq_spec, pl.BlockSpec((cps, HEAD), lambda h, i: (i, h))],
        out_shape=[jax.ShapeDtypeStruct((t, D_B), F32)] * 4
        + [sq_shape, sq_shape, jax.ShapeDtypeStruct((t // CH, D_B), F32)],
        compiler_params=_cp("parallel", "parallel"),
    )(q, k, v, g, beta)


def _dn_scan_fwd(ub, w, qd, kd, qk, gl, cps=8, hg=2 * SCAN_HEADS):
    t = ub.shape[0]
    tm = cps * CH
    hs = list(range(hg))

    def body(ub_ref, w_ref, qd_ref, kd_ref, qk_ref, gl_ref, o_ref, st_ref, s_acc):
        @pl.when(pl.program_id(1) == 0)
        def _():
            s_acc[...] = jnp.zeros_like(s_acc)

        for ci in range(cps):
            sl = slice(ci * CH, (ci + 1) * CH)
            cols = [slice(h * HEAD, (h + 1) * HEAD) for h in hs]
            svs = [s_acc[h] for h in hs]
            for h in hs:
                st_ref[h, ci * HEAD:(ci + 1) * HEAD, :] = svs[h]
            us = [ub_ref[sl, cols[h]] - _bdot(w_ref[sl, cols[h]], svs[h]) for h in hs]
            for h in hs:
                s_acc[h] = gl_ref[ci:ci + 1, cols[h]] * svs[h] + _bdot(kd_ref[sl, cols[h]], us[h], TN)
            for h in hs:
                o_ref[sl, cols[h]] = _bdot(qd_ref[sl, cols[h]], svs[h]) + _bdot(qk_ref[h, sl, :], us[h])

    hspec = pl.BlockSpec((tm, hg * HEAD), lambda h, i: (i, h))
    return pl.pallas_call(
        body, name="dn_scan_fwd", grid=(N_HEADS_B // hg, t // tm),
        in_specs=[hspec] * 4 + [pl.BlockSpec((hg, tm, CH), lambda h, i: (h, i, 0)),
                                pl.BlockSpec((cps, hg * HEAD), lambda h, i: (i, h))],
        out_specs=[hspec, pl.BlockSpec((hg, cps * HEAD, HEAD), lambda h, i: (h, i, 0))],
        out_shape=[jax.ShapeDtypeStruct((t, D_B), F32),
                   jax.ShapeDtypeStruct((N_HEADS_B, (t // CH) * HEAD, HEAD), F32)],
        scratch_shapes=[pltpu.VMEM((hg, HEAD, HEAD), F32)],
        compiler_params=_cp("parallel", "arbitrary"),
    )(ub, w, qd, kd, qk, gl)


def _dn_scan_bwd(ub, w, qd, kd, qk, gl, st, do, cps=8):
    t = ub.shape[0]
    tm = cps * CH
    ns = t // tm

    hg = SCAN_HEADS
    hs = list(range(hg))

    def body(ub_ref, w_ref, qd_ref, kd_ref, qk_ref, gl_ref, st_ref, do_ref,
             dub_ref, dw_ref, dqd_ref, dkd_ref, dqk_ref, dgl_ref, ds_acc):
        @pl.when(pl.program_id(1) == 0)
        def _():
            ds_acc[...] = jnp.zeros_like(ds_acc)

        for ci in reversed(range(cps)):
            sl = slice(ci * CH, (ci + 1) * CH)
            cols = [slice(h * HEAD, (h + 1) * HEAD) for h in hs]
            svs = [st_ref[h, ci * HEAD:(ci + 1) * HEAD, :] for h in hs]
            wvs = [w_ref[sl, cols[h]] for h in hs]
            dovs = [do_ref[sl, cols[h]] for h in hs]
            dsvs = [ds_acc[h] for h in hs]
            us = [ub_ref[sl, cols[h]] - _bdot(wvs[h], svs[h]) for h in hs]
            dus = [_bdot(kd_ref[sl, cols[h]], dsvs[h]) + _bdot(qk_ref[h, sl, :], dovs[h], TN) for h in hs]
            for h in hs:
                ds_acc[h] = (gl_ref[ci:ci + 1, cols[h]] * dsvs[h] + _bdot(qd_ref[sl, cols[h]], dovs[h], TN)
                             - _bdot(wvs[h], dus[h], TN))
            for h in hs:
                dgl_ref[ci:ci + 1, cols[h]] = (jnp.sum(_rowsum(dsvs[h] * svs[h]), axis=0, keepdims=True)
                                              + jnp.zeros((1, HEAD), F32))
                dkd_ref[sl, cols[h]] = _bdot(us[h], dsvs[h], NT)
                dqd_ref[sl, cols[h]] = _bdot(dovs[h], svs[h], NT)
                dqk_ref[h, sl, :] = _bdot(dovs[h], us[h], NT)
                dub_ref[sl, cols[h]] = dus[h]
                dw_ref[sl, cols[h]] = -_bdot(dus[h], svs[h], NT)

    hspec = pl.BlockSpec((tm, hg * HEAD), lambda h, i: (ns - 1 - i, h))
    qkspec = pl.BlockSpec((hg, tm, CH), lambda h, i: (h, ns - 1 - i, 0))
    glspec = pl.BlockSpec((cps, hg * HEAD), lambda h, i: (ns - 1 - i, h))
    return pl.pallas_call(
        body, name="dn_scan_bwd", grid=(N_HEADS_B // hg, ns),
        in_specs=[hspec] * 4 + [qkspec, glspec,
                                pl.BlockSpec((hg, cps * HEAD, HEAD), lambda h, i: (h, ns - 1 - i, 0)), hspec],
        out_specs=[hspec] * 4 + [qkspec, glspec],
        out_shape=[jax.ShapeDtypeStruct((t, D_B), F32)] * 4
        + [jax.ShapeDtypeStruct((N_HEADS_B, t, CH), F32), jax.ShapeDtypeStruct((t // CH, D_B), F32)],
        scratch_shapes=[pltpu.VMEM((hg, HEAD, HEAD), F32)],
        compiler_params=_cp("parallel", "arbitrary"),
    )(ub, w, qd, kd, qk, gl, st, do)


def _dn_chunk_bwd(q, k, v, g, beta, ub, w, tinv, dub, dw, dqd, dkd, dqk, dgl, cps=CHUNK_GROUP):
    t = q.shape[0]
    tm = cps * CH

    def body(q_ref, k_ref, v_ref, g_ref, b_ref, ub_ref, w_ref, dub_ref, dw_ref, dqd_ref, dkd_ref, ti_ref, dqk_ref,
             dgl_ref, dq_ref, dk_ref, dv_ref, dg_ref, db_ref):
        ones = jnp.ones((CH, HEAD), F32)
        rid = lax.broadcasted_iota(jnp.int32, (CH, HEAD), 0)

        def rest(ci, sl, qv, kv, vv, beta_v, loc, dr, da):
            incl = loc["incl"]
            eg, ek, decay, bcol, kk = loc["eg"], loc["ek"], loc["decay"], loc["bcol"], loc["kk"]
            drv, drk = dr[:, :HEAD], dr[:, HEAD:]
            dv_ref[sl, :] = beta_v * drv
            beg = beta_v * eg
            t1 = drk * kv
            dbeta = _rowsum(drv * vv + t1 * eg) + _rowsum(da * kk * decay)
            dkk = da * bcol * decay
            dqk_m = jnp.where(incl, dqk_ref[sl, :], 0.0)
            ddecay = da * bcol * kk + dqk_m * loc["qkraw"]
            dqkraw = dqk_m * decay
            dqdv, dkdv = dqd_ref[sl, :], dkd_ref[sl, :]
            dq_ref[sl, :] = _bdot(dqkraw, kv) + dqdv * eg
            dk_ref[sl, :] = (beg * drk + _bdot(dqkraw, qv, TN) + _bdot(dkk, kv) + _bdot(dkk, kv, TN)
                             + dkdv * ek)
            e = ddecay * decay
            skd = _rowsum(dkdv * kv * ek)
            dgc = _rowsum(beg * t1) + _rowsum(e) + _rowsum(dqdv * qv * eg) - skd
            colsum = _hdot(e, ones, TN)
            last = jnp.sum(skd, axis=0, keepdims=True) + dgl_ref[ci:ci + 1, :] * jnp.exp(loc["gl"])
            db_ref[sl, :] = dbeta + jnp.zeros((CH, HEAD), F32)
            return (dgc - colsum) + jnp.where(rid == CH - 1, last, 0.0)

        for base in range(0, cps, CHUNK_GROUP):
            cis = list(range(base, base + CHUNK_GROUP))
            sls = [slice(ci * CH, (ci + 1) * CH) for ci in cis]
            qs, ks, vs = [q_ref[sl, :] for sl in sls], [k_ref[sl, :] for sl in sls], [v_ref[sl, :] for sl in sls]
            betas = [b_ref[sl, :] for sl in sls]
            locs = _chunk_local(qs, ks, vs, [g_ref[sl, :] for sl in sls], betas,
                                solved=([ti_ref[sl, :] for sl in sls], [ub_ref[sl, :] for sl in sls],
                                        [w_ref[sl, :] for sl in sls]))
            drs = [_hdot(loc["tm"], jnp.concatenate([dub_ref[sl, :], dw_ref[sl, :]], axis=1), TN)
                   for loc, sl in zip(locs, sls)]
            das = [jnp.where(loc["strict"],
                             -_hdot(dr, jnp.concatenate([loc["u_bar"], loc["w"]], axis=1), NT), 0.0)
                   for loc, dr in zip(locs, drs)]
            dgcs = [rest(*args) for args in zip(cis, sls, qs, ks, vs, betas, locs, drs, das)]
            um = (locs[0]["r"] <= locs[0]["c"]).astype(F32)
            for sl, dgc_bc in zip(sls, dgcs):
                dg_ref[sl, :] = _hdot(um, dgc_bc)

    hspec = pl.BlockSpec((tm, HEAD), lambda h, i: (i, h))
    sq_spec = pl.BlockSpec((None, tm, CH), lambda h, i: (h, i, 0))
    return pl.pallas_call(
        body, name="dn_chunk_bwd", grid=(N_HEADS_B, t // tm),
        in_specs=[hspec] * 11 + [sq_spec, sq_spec, pl.BlockSpec((cps, HEAD), lambda h, i: (i, h))],
        out_specs=[hspec] * 5,
        out_shape=[jax.ShapeDtypeStruct((t, D_B), F32)] * 5,
        compiler_params=_cp("parallel", "parallel"),
    )(q, k, v, g, beta, ub, w, dub, dw, dqd, dkd, tinv, dqk, dgl)


FLIPS = [(fx, fy, fc) for fx in (0, 1) for fy in (0, 1) for fc in (0, 1)][1:]


def _mesh_pos():
    return lax.axis_index("x"), lax.axis_index("y"), lax.axis_index("c")


def _peer(pos, flip):
    return tuple((1 - p) if f else p for p, f in zip(pos, flip))


def _dev_index(pos):
    return 4 * pos[0] + 2 * pos[1] + pos[2]


class _Exchange:
    def __init__(self, tensors, scatter):
        self.tensors, self.scatter, self.nt = list(tensors), list(scatter), len(tensors)
        hbm = pl.BlockSpec(memory_space=pltpu.HBM)
        self.in_specs = [hbm] * self.nt
        self.out_specs = [hbm] * self.nt
        self.out_shape = [jax.ShapeDtypeStruct(x.shape if sc else (N_DEV,) + x.shape, x.dtype)
                          for x, sc in zip(tensors, scatter)]
        self.scratch_shapes = [pltpu.SemaphoreType.DMA((self.nt * 7,)), pltpu.SemaphoreType.DMA((self.nt * 7,)),
                               pltpu.SemaphoreType.DMA((self.nt,))]

    def _copies(self, ins, outs, sems):
        send_sems, recv_sems, local_sems = sems
        pos = _mesh_pos()
        me = _dev_index(pos)

        def remote(ti, fi, landing):
            peer = _peer(pos, FLIPS[fi])
            src = ins[ti].at[_dev_index(peer)] if self.scatter[ti] else ins[ti]
            return pltpu.make_async_remote_copy(
                src_ref=src, dst_ref=outs[ti].at[landing(peer)],
                send_sem=send_sems.at[ti * 7 + fi], recv_sem=recv_sems.at[ti * 7 + fi],
                device_id=peer, device_id_type=pl.DeviceIdType.MESH)

        pairs = [(ti, fi) for ti in range(self.nt) for fi in range(7)]
        local = [pltpu.make_async_copy(ins[ti].at[me] if self.scatter[ti] else ins[ti], outs[ti].at[me],
                                       local_sems.at[ti]) for ti in range(self.nt)]
        sends = [remote(ti, fi, lambda peer: me) for ti, fi in pairs]
        recvs = [remote(ti, fi, _dev_index) for ti, fi in pairs]
        return local, sends, recvs

    def start(self, ins, outs, sems):
        local, sends, _ = self._copies(ins, outs, sems)
        for cp in local + sends:
            cp.start()

    def wait(self, ins, outs, sems):
        local, sends, recvs = self._copies(ins, outs, sems)
        for cp in recvs:
            cp.wait_recv()
        for cp in sends:
            cp.wait_send()
        for cp in local:
            cp.wait()


def _exchange(tensors, scatter, name):
    ex = _Exchange(tensors, scatter)

    def body(*refs):
        ins, outs, sems = refs[:ex.nt], refs[ex.nt:2 * ex.nt], refs[2 * ex.nt:]
        ex.start(ins, outs, sems)
        ex.wait(ins, outs, sems)

    return pl.pallas_call(
        body, name=name, in_specs=ex.in_specs, out_specs=ex.out_specs, out_shape=ex.out_shape,
        scratch_shapes=ex.scratch_shapes, compiler_params=pltpu.CompilerParams(has_side_effects=True),
    )(*tensors)


def _gather_two_level(tensors, name):
    nt = len(tensors)
    hbm = pl.BlockSpec(memory_space=pltpu.HBM)

    def body(*refs):
        ins, outs = refs[:nt], refs[nt:2 * nt]
        send_sems, recv_sems, local_sems = refs[2 * nt:]
        x, y, c = _mesh_pos()
        sibling = (x, y, 1 - c)
        chips = [(1 - x, y), (x, 1 - y), (1 - x, 1 - y)]

        def copy(ti, k, block, to, own=False):
            slot = outs[ti].at[_dev_index(block)]
            return pltpu.make_async_remote_copy(
                src_ref=ins[ti] if own else slot, dst_ref=slot,
                send_sem=send_sems.at[ti * 7 + k], recv_sem=recv_sems.at[ti * 7 + k],
                device_id=to, device_id_type=pl.DeviceIdType.MESH)

        me = (x, y, c)
        mine = [pltpu.make_async_copy(ins[ti], outs[ti].at[_dev_index(me)], local_sems.at[ti]) for ti in range(nt)]
        first = [copy(ti, 0, me, sibling, own=True) for ti in range(nt)]
        first += [copy(ti, 1 + j, me, (*chip, c), own=True) for ti in range(nt) for j, chip in enumerate(chips)]
        for cp in mine + first:
            cp.start()
        passed = []
        for j, chip in enumerate(chips):
            for ti in range(nt):
                copy(ti, 1 + j, (*chip, c), me).wait_recv()
                cp = copy(ti, 4 + j, (*chip, c), sibling)
                cp.start()
                passed.append(cp)
        for ti in range(nt):
            copy(ti, 0, sibling, me).wait_recv()
            for j, chip in enumerate(chips):
                copy(ti, 4 + j, (*chip, 1 - c), me).wait_recv()
        for cp in first + passed:
            cp.wait_send()
        for cp in mine:
            cp.wait()

    return pl.pallas_call(
        body, name=name, in_specs=[hbm] * nt, out_specs=[hbm] * nt,
        out_shape=[jax.ShapeDtypeStruct((N_DEV,) + x.shape, x.dtype) for x in tensors],
        scratch_shapes=[pltpu.SemaphoreType.DMA((nt * 7,)), pltpu.SemaphoreType.DMA((nt * 7,)),
                        pltpu.SemaphoreType.DMA((nt,))],
        compiler_params=pltpu.CompilerParams(has_side_effects=True),
    )(*tensors)


def _hosted_call(body, *, name, grid, in_specs, out_specs, out_shape, scratch_shapes, args, dims, exch=None):
    if exch is None:
        return pl.pallas_call(body, name=name, grid=grid, in_specs=in_specs, out_specs=out_specs,
                              out_shape=out_shape, scratch_shapes=scratch_shapes,
                              compiler_params=_cp(*dims))(*args)
    n_in, n_out, n_sc, ne = len(in_specs), len(out_specs), len(scratch_shapes), exch.nt
    nsteps = math.prod(grid)

    def wrapped(*refs):
        ins, ex_in = refs[:n_in], refs[n_in:n_in + ne]
        outs = refs[n_in + ne:n_in + ne + n_out]
        ex_out = refs[n_in + ne + n_out:n_in + 2 * ne + n_out]
        rest = refs[n_in + 2 * ne + n_out:]
        scratch, sems = rest[:n_sc], rest[n_sc:]
        step = pl.program_id(0)
        for ax in range(1, len(grid)):
            step = step * grid[ax] + pl.program_id(ax)

        @pl.when(step == 0)
        def _():
            exch.start(ex_in, ex_out, sems)

        body(*ins, *outs, *scratch)

        @pl.when(step == nsteps - 1)
        def _():
            exch.wait(ex_in, ex_out, sems)

    return pl.pallas_call(
        wrapped, name=name, grid=grid, in_specs=list(in_specs) + exch.in_specs,
        out_specs=list(out_specs) + exch.out_specs, out_shape=list(out_shape) + exch.out_shape,
        scratch_shapes=list(scratch_shapes) + exch.scratch_shapes,
        compiler_params=pltpu.CompilerParams(dimension_semantics=("arbitrary",) * len(grid),
                                             vmem_limit_bytes=VMEM_LIMIT, has_side_effects=True),
    )(*args, *exch.tensors)


def _adamw(land, w, m, v, name, tm=256):
    n, r, c = land.shape
    tm = r if r <= tm else max(s for s in range(8, tm + 1, 8) if r % s == 0)
    bc1 = 1.0 / (1.0 - ADAM_B1 ** ADAM_STEP)
    bc2 = 1.0 / (1.0 - ADAM_B2 ** ADAM_STEP)

    def body(l_ref, w_ref, m_ref, v_ref, g_ref, d_ref, nm_ref, nv_ref):
        g = l_ref[0].astype(F32)
        for i in range(1, n):
            g = g + l_ref[i].astype(F32)
        nm = ADAM_B1 * m_ref[...] + (1.0 - ADAM_B1) * g
        nv = ADAM_B2 * v_ref[...] + (1.0 - ADAM_B2) * (g * g)
        g_ref[...] = g
        nm_ref[...] = nm
        nv_ref[...] = nv
        d_ref[...] = -ADAM_LR * ((nm * bc1) / (jnp.sqrt(nv * bc2) + ADAM_EPS) + ADAM_WD * w_ref[...])

    spec = pl.BlockSpec((tm, c), lambda i: (i, 0))
    return pl.pallas_call(
        body, name=name, grid=(r // tm,),
        in_specs=[pl.BlockSpec((n, tm, c), lambda i: (0, i, 0)), spec, spec, spec],
        out_specs=[spec] * 4, out_shape=[jax.ShapeDtypeStruct((r, c), F32)] * 4,
        compiler_params=_cp("parallel"),
    )(land, w, m, v)


PACK_W = 2048


def _pack_rows(parts):
    flat = jnp.concatenate([p.reshape(-1).astype(F32) for p in parts])
    pad = (-flat.shape[0]) % (8 * PACK_W)
    return jnp.pad(flat, (0, pad)).reshape(-1, PACK_W)


def _unpack_rows(packed, shapes):
    flat = packed.reshape(-1)
    out, off = [], 0
    for s in shapes:
        n = math.prod(s)
        out.append(flat[off:off + n].reshape(s))
        off += n
    return out


def _col_slabs(gfull, width):
    r = gfull.shape[0]
    return jnp.transpose(gfull.reshape(r, N_DEV, width), (1, 0, 2)).astype(BF16)


def _row_slabs(gfull):
    return gfull.reshape(N_DEV, gfull.shape[0] // N_DEV, gfull.shape[1]).astype(BF16)


def _from_col_slabs(gathered):
    n, r, width = gathered.shape
    return jnp.transpose(gathered, (1, 0, 2)).reshape(r, n * width)


def _local_step(xs, target, norm_mix, wf_in, cw, a_log, dt_bias, dn_norm, rest, norm_ffn, norm_final,
                distributed=True):
    d = D_MODEL
    n_main = D_PA + 4 * D_B
    w_pa_cols = wf_in[:, :D_PA]
    w_pb_cols = jnp.concatenate([wf_in[:, D_PA:n_main], wf_in[:, n_main + 2 * N_HEADS_B:]], axis=1)
    w_small = jnp.pad(wf_in[:, n_main:n_main + 2 * N_HEADS_B], ((0, 0), (0, HEAD - 2 * N_HEADS_B)))
    a_log_bc = jnp.repeat(a_log, HEAD, axis=1)
    dt_bias_bc = jnp.repeat(dt_bias, HEAD, axis=1)

    u = _rms_fwd(xs, norm_mix)
    pa = _matmul(u, w_pa_cols, "nn", F32, 1024, 1536, d, name="proj_a")
    pb = _matmul(u, w_pb_cols, "nn", BF16, 1024, 1024, d, name="proj_b")
    ps = _matmul(u, w_small, "nn", F32, 2048, HEAD, d, name="proj_small")
    os_, ls_ = [], []
    for gi, dil in enumerate(DILATIONS):
        o_g, l_g = _attn_fwd_group(pa, gi, dil)
        os_.append(o_g)
        ls_.append(l_g)
    y_att, lse = _attn_merge(os_, ls_)
    prep = _dn_prep_fwd(pb, ps, cw, a_log_bc, dt_bias_bc,
                        exch=_Exchange(rest, [False] * 6) if distributed else None)
    qn, kn, vn, gdec, beta = prep[:5]
    if distributed:
        g_pa, g_pd, g_out, g_gate, g_up, g_down = prep[5:]
        wf_pa, wf_pd, wf_out = _from_col_slabs(g_pa), g_pd.reshape(D_B, d), g_out.reshape(d, d)
        wf_gate, wf_up, wf_down = _from_col_slabs(g_gate), _from_col_slabs(g_up), g_down.reshape(D_FF, d)
    else:
        wf_pa, wf_pd, wf_out, wf_gate, wf_up, wf_down = rest
    wf_gu = jnp.concatenate([wf_gate, wf_up], axis=1)
    ub, ww, qd, kd, qk, tinv, gl = _dn_chunk_fwd(qn, kn, vn, gdec, beta)
    o_dn, states = _dn_scan_fwd(ub, ww, qd, kd, qk, gl)
    o_gated = _head_norm_fwd(o_dn, pb, dn_norm)
    y_a = _matmul(y_att, wf_pa, "nn", BF16, 1024, d, D_ATTN_OUT, name="proj_attn")
    y_b = _matmul(o_gated, wf_pd, "nn", BF16, 1024, d, d, name="proj_delta")
    merged = _gate_merge_fwd(pb, y_a, y_b)
    h1 = _matmul(merged, wf_out, "nn", F32, 1024, d, d, add=xs, name="out_proj")
    hn = _rms_fwd(h1, norm_ffn)
    gate, up, act = _ffn_in(hn, wf_gate, wf_up)
    h2 = _matmul(act, wf_down, "nn", F32, 512, d, D_FF, add=h1, name="ffn_out")
    loss_part, dh2, d_norm_final = _final_loss(h2, norm_final.reshape(1, d), target)

    dgate, dup = _ffn_act_bwd(dh2, wf_down, gate, up)
    gw_down = _matmul(act, dh2, "tn", BF16, 1408, d, TOKEN_TK, name="gw_down")
    dhn = _matmul_nt_segments([([dgate, dup], wf_gu, 1408)], BF16, 1024, "d_hn")[0]
    gw_gate = _matmul(hn, dgate, "tn", BF16, d, 1408, TOKEN_TK, name="gw_gate")
    gw_up = _matmul(hn, dup, "tn", BF16, d, 1408, TOKEN_TK, name="gw_up")
    dh1, d_norm_ffn = _rms_bwd(h1, norm_ffn, dhn, dh2)
    dmerged = _matmul(dh1, wf_out, "nt", BF16, 1024, d, d, name="d_merged")
    gw_out = _matmul(merged, dh1, "tn", BF16, d, d, TOKEN_TK, name="gw_out")
    dya, dyb, dga, dgb = _gate_merge_bwd(pb, y_a, y_b, dmerged)
    dy_att = _matmul(dya, wf_pa, "nt", F32, 1024, D_ATTN_OUT, d, name="d_y_att")
    gw_pa = _matmul(y_att, dya, "tn", BF16, D_ATTN_OUT, d, TOKEN_TK, name="gw_pa")
    do_gated = _matmul(dyb, wf_pd, "nt", BF16, 1024, d, d, name="d_o_gated")
    gw_pd = _matmul(o_gated, dyb, "tn", BF16, d, d, TOKEN_TK, name="gw_pd")
    do_dn, dz, d_dn_norm = _head_norm_bwd(o_dn, pb, dn_norm, do_gated)
    dub, dww, dqd, dkd, dqk, dgl = _dn_scan_bwd(ub, ww, qd, kd, qk, gl, states, do_dn)
    dqn, dkn, dvn, dgdec, dbeta = _dn_chunk_bwd(qn, kn, vn, gdec, beta, ub, ww, tinv, dub, dww, dqd, dkd, dqk, dgl)
    slabs = [_col_slabs(gw_pa, d // N_DEV), _row_slabs(gw_pd), _row_slabs(gw_out),
             _col_slabs(gw_gate, D_FF // N_DEV), _col_slabs(gw_up, D_FF // N_DEV),
             _row_slabs(gw_down)] if distributed else None
    prep = _dn_prep_bwd(pb, ps, cw, a_log_bc, dt_bias_bc, gdec, dqn, dkn, dvn, dgdec, dbeta,
                        exch=_Exchange(slabs, [True] * 6) if distributed else None)
    dc, dps, d_conv_full, d_alog_bc, d_dt_bc = prep[:5]
    dqkv_pre = _conv_bwd_input(dc, cw)
    segs = [_attn_bwd_group(pa, dy_att, y_att, lse, gi, dil) for gi, dil in enumerate(DILATIONS)]
    segs += [dqkv_pre, dz, dga, dgb]
    gws = [_matmul(u, s, "tn", BF16, d, 1536, TOKEN_TK, name=f"gw_in_{i}") for i, s in enumerate(segs)]
    gw_small = _matmul(u, dps, "tn", BF16, d, HEAD, TOKEN_TK, name="gw_in_small")
    g_att = jnp.concatenate(gws[:3], axis=1).reshape(d, N_HEADS_A, 3, HEAD)
    gw_in = jnp.concatenate(
        [g_att[:, :, i, :].reshape(d, D_ATTN) for i in range(3)]
        + [gws[3], gws[4], gw_small[:, :2 * N_HEADS_B], gws[5], gws[6]], axis=1)
    w_att = jnp.stack([w_pa_cols[:, i * D_ATTN:(i + 1) * D_ATTN].reshape(d, N_HEADS_A, HEAD) for i in range(3)],
                      axis=2).reshape(d, D_PA)
    du_small = _matmul(dps, w_small, "nt", F32, 1024, d, HEAD, name="d_u_small")
    du_call = _matmul_nt_segments(
        [(segs[:4], jnp.concatenate([w_att, w_pb_cols[:, :3 * D_B]], axis=1), 768),
         (segs[4:], w_pb_cols[:, 3 * D_B:], 1024)], BF16, 1024, "d_u", add=du_small,
        exch=_Exchange([_col_slabs(gw_in, SHARD_IN)], [True]) if distributed else None)
    dx, d_norm_mix = _rms_bwd(xs, norm_mix, du_call[0], dh1)
    d_a_log = d_alog_bc.reshape(1, N_HEADS_B, HEAD)[:, :, 0]
    d_dt_bias = d_dt_bc.reshape(1, N_HEADS_B, HEAD)[:, :, 0]
    small = (d_conv_full, d_norm_mix, d_norm_ffn, d_norm_final, d_dn_norm, d_a_log, d_dt_bias)
    if distributed:
        return (loss_part, dx, [du_call[1]] + list(prep[5:])) + small
    return (loss_part, dx, gw_in, gw_pa, gw_pd, gw_out, jnp.concatenate([gw_gate, gw_up], axis=1), gw_down) + small


def kernel(x, norm_mix, w_in, conv_w, a_log, dt_bias, dn_norm, w_proj_attn, w_proj_delta, w_out, norm_ffn, w_gate, w_up, w_down, norm_final, loss_target, m_norm_mix, m_w_in, m_conv_w, m_a_log, m_dt_bias, m_dn_norm, m_w_proj_attn, m_w_proj_delta, m_w_out, m_norm_ffn, m_w_gate, m_w_up, m_w_down, m_norm_final, v_norm_mix, v_w_in, v_conv_w, v_a_log, v_dt_bias, v_dn_norm, v_w_proj_attn, v_w_proj_delta, v_w_out, v_norm_ffn, v_w_gate, v_w_up, v_w_down, v_norm_final):
    d = D_MODEL
    xs = x[0]
    target = loss_target[0]
    me = _dev_index(_mesh_pos())

    g_in, g_conv = _gather_two_level([w_in[0].astype(BF16), conv_w[0]], "gather_w_in")
    rest = [w[0].astype(BF16) for w in (w_proj_attn, w_proj_delta, w_out, w_gate, w_up, w_down)]
    (loss_part, dx, landed, d_conv_full, d_norm_mix, d_norm_ffn, d_norm_final, d_dn_norm, d_a_log,
     d_dt_bias) = _local_step(xs, target, norm_mix, _from_col_slabs(g_in), _from_col_slabs(g_conv), a_log, dt_bias,
                              dn_norm, rest, norm_ffn, norm_final)

    small_shapes = [(1, d), (1, d), (d,), (1, HEAD), (1, N_HEADS_B), (1, N_HEADS_B), (1, 1), (CONV_WIDTH, 3 * D_B)]
    packed = _pack_rows([d_norm_mix, d_norm_ffn, d_norm_final, d_dn_norm, d_a_log, d_dt_bias,
                         loss_part[:, :1], d_conv_full])
    landed = list(landed) + list(_exchange([packed], [False], "gather_small_grads"))
    zero1 = jnp.zeros((1, 1), F32)
    zconv = jnp.zeros((CONV_WIDTH, 3 * D_B), F32)
    small_w = _pack_rows([norm_mix, norm_ffn, norm_final, dn_norm, a_log, dt_bias, zero1, zconv])
    small_m = _pack_rows([m_norm_mix, m_norm_ffn, m_norm_final, m_dn_norm, m_a_log, m_dt_bias, zero1, zconv])
    small_v = _pack_rows([v_norm_mix, v_norm_ffn, v_norm_final, v_dn_norm, v_a_log, v_dt_bias, zero1, zconv])
    small = [_unpack_rows(z, small_shapes) for z in _adamw(landed[7], small_w, small_m, small_v, "adamw_small")]
    loss = small[0][6].reshape(())
    conv_shard = 3 * D_B // N_DEV
    g_conv_own = lax.dynamic_slice_in_dim(small[0][7], me * conv_shard, conv_shard, axis=1)
    r_conv = _adamw(g_conv_own[None], conv_w[0], m_conv_w[0], v_conv_w[0], "adamw_conv")
    big = [_adamw(landed[i], w[0], m[0], v[0], f"adamw_{i}") for i, (w, m, v) in enumerate([
        (w_in, m_w_in, v_w_in), (w_proj_attn, m_w_proj_attn, v_w_proj_attn),
        (w_proj_delta, m_w_proj_delta, v_w_proj_delta), (w_out, m_w_out, v_w_out),
        (w_gate, m_w_gate, v_w_gate), (w_up, m_w_up, v_w_up), (w_down, m_w_down, v_w_down)])]

    def leaves(k):
        sm = small[k]
        return [sm[0], big[0][k][None], r_conv[k][None], sm[4], sm[5], sm[3], big[1][k][None], big[2][k][None],
                big[3][k][None], sm[1], big[4][k][None], big[5][k][None], big[6][k][None], sm[2]]

    return (loss, dx[None], *leaves(0), *leaves(1), *leaves(2), *leaves(3))
```

```python
import math

import jax
import jax.numpy as jnp
from jax import lax
from jax.experimental import pallas as pl
from jax.experimental.pallas import tpu as pltpu

F32 = jnp.float32
BF16 = jnp.bfloat16
HI = lax.Precision.HIGH

D_MODEL = 1024
N_DEV = 8
HEAD = 128
N_HEADS_A = 12
HEADS_PER_GROUP = 4
DILATIONS = (1, 4, 16)
BLOCK_A = 128
D_ATTN = N_HEADS_A * HEAD
D_ATTN_OUT = HEADS_PER_GROUP * HEAD
N_HEADS_B = 8
D_B = N_HEADS_B * HEAD
CONV_WIDTH = 4
CH = 64
CHUNK_GROUP = 32
SCAN_HEADS = 4
TOKEN_TK = 1024
D_FF = 2816
EPS = 1e-6
D_IN = 3 * D_ATTN + 4 * D_B + 2 * N_HEADS_B + 2 * D_MODEL
SHARD_IN = D_IN // N_DEV
PB_Z, PB_GATE = 3072, 4096
D_PA = 3 * D_ATTN
ADAM_LR, ADAM_B1, ADAM_B2, ADAM_EPS, ADAM_WD, ADAM_STEP = 0.001, 0.9, 0.999, 1e-08, 0.01, 10
VMEM_LIMIT = 56 * 1024 * 1024

NN = ((1,), (0,))
NT = ((1,), (1,))
TN = ((0,), (0,))


def _dot(a, b, dims=NN, prec=None):
    return lax.dot_general(a, b, (dims, ((), ())), precision=prec, preferred_element_type=F32)


def _bdot(a, b, dims=NN):
    return _dot(a.astype(BF16), b.astype(BF16), dims)


def _hdot(a, b, dims=NN):
    return _dot(a.astype(F32), b.astype(F32), dims, HI)


def _cp(*sem):
    return pltpu.CompilerParams(dimension_semantics=sem, vmem_limit_bytes=VMEM_LIMIT)


def _sigmoid(x):
    return 0.5 * jnp.tanh(0.5 * x) + 0.5


def _softplus(x):
    return jnp.maximum(x, 0.0) + jnp.log(1.0 + jnp.exp(-jnp.abs(x)))


def _rowsum(x):
    return jnp.sum(x, axis=-1, keepdims=True)


def _matmul(a, b, mode, out_dtype, tm, tn, tk, add=None, name="mm"):
    if mode == "nn":
        (m, k), (k2, n) = a.shape, b.shape
    elif mode == "nt":
        (m, k), (n, k2) = a.shape, b.shape
    else:
        (k, m), (k2, n) = a.shape, b.shape
    assert k == k2, (a.shape, b.shape, mode)
    tm, tn, tk = min(tm, m), min(tn, n), min(tk, k)
    assert m % tm == 0 and n % tn == 0 and k % tk == 0, (a.shape, b.shape, tm, tn, tk)
    nk = k // tk
    dims = {"nn": NN, "nt": NT, "tn": TN}[mode]

    def body(*refs):
        if add is None:
            a_ref, b_ref, o_ref, acc = refs
            add_ref = None
        else:
            a_ref, b_ref, add_ref, o_ref, acc = refs
        kk = pl.program_id(2)

        @pl.when(kk == 0)
        def _():
            acc[...] = jnp.zeros_like(acc)

        acc[...] += _bdot(a_ref[...], b_ref[...], dims)

        @pl.when(kk == nk - 1)
        def _():
            r = acc[...]
            if add_ref is not None:
                r = r + add_ref[...].astype(F32)
            o_ref[...] = r.astype(out_dtype)

    a_spec = (pl.BlockSpec((tk, tm), lambda i, j, kk: (kk, i)) if mode == "tn"
              else pl.BlockSpec((tm, tk), lambda i, j, kk: (i, kk)))
    b_spec = (pl.BlockSpec((tn, tk), lambda i, j, kk: (j, kk)) if mode == "nt"
              else pl.BlockSpec((tk, tn), lambda i, j, kk: (kk, j)))
    in_specs = [a_spec, b_spec]
    args = [a, b]
    if add is not None:
        in_specs.append(pl.BlockSpec((tm, tn), lambda i, j, kk: (i, j)))
        args.append(add)
    return pl.pallas_call(
        body, name=name, grid=(m // tm, n // tn, nk),
        in_specs=in_specs, out_specs=pl.BlockSpec((tm, tn), lambda i, j, kk: (i, j)),
        out_shape=jax.ShapeDtypeStruct((m, n), out_dtype),
        scratch_shapes=[pltpu.VMEM((tm, tn), F32)],
        compiler_params=_cp("parallel", "parallel", "arbitrary"),
    )(*args)


def _matmul_nt_segments(groups, out_dtype, tm, name, add=None, exch=None):
    m = groups[0][0][0].shape[0]
    n = groups[0][1].shape[0]
    assert m % tm == 0
    seg_list, b_list = [], []
    step = 0
    for gi, (segs, b, tk) in enumerate(groups):
        assert all(s.shape[1] % tk == 0 for s in segs) and sum(s.shape[1] for s in segs) == b.shape[1]
        g0 = step
        for s in segs:
            seg_list.append((s, tk, step, s.shape[1] // tk, gi))
            step += s.shape[1] // tk
        b_list.append((b, tk, g0, step - g0))
    nk = step
    ns, nb = len(seg_list), len(b_list)

    def body(*refs):
        seg_refs, b_refs = refs[:ns], refs[ns:ns + nb]
        add_ref = refs[ns + nb] if add is not None else None
        o_ref, acc = refs[-2], refs[-1]
        kk = pl.program_id(1)

        @pl.when(kk == 0)
        def _():
            acc[...] = jnp.zeros_like(acc)

        for a_ref, (_, _, k0, nk_s, gi) in zip(seg_refs, seg_list):
            @pl.when(jnp.logical_and(kk >= k0, kk < k0 + nk_s))
            def _(a_ref=a_ref, b_ref=b_refs[gi]):
                acc[...] += _bdot(a_ref[...], b_ref[...], NT)

        @pl.when(kk == nk - 1)
        def _():
            r = acc[...]
            if add_ref is not None:
                r = r + add_ref[...].astype(F32)
            o_ref[...] = r.astype(out_dtype)

    def walk(rows, tk, k0, nk_s, row_axis):
        if row_axis:
            return pl.BlockSpec((rows, tk), lambda i, kk: (i, jnp.clip(kk - k0, 0, nk_s - 1)))
        return pl.BlockSpec((rows, tk), lambda i, kk: (0, jnp.clip(kk - k0, 0, nk_s - 1)))

    row = pl.BlockSpec((tm, n), lambda i, kk: (i, 0))
    in_specs = [walk(tm, tk, k0, nk_s, True) for _, tk, k0, nk_s, _ in seg_list]
    in_specs += [walk(n, tk, k0, nk_s, False) for _, tk, k0, nk_s in b_list]
    args = [s[0] for s in seg_list] + [b[0] for b in b_list]
    if add is not None:
        in_specs.append(row)
        args.append(add)
    return _hosted_call(body, name=name, grid=(m // tm, nk), in_specs=in_specs, out_specs=[row],
                        out_shape=[jax.ShapeDtypeStruct((m, n), out_dtype)],
                        scratch_shapes=[pltpu.VMEM((tm, n), F32)], args=args,
                        dims=("parallel", "arbitrary"), exch=exch)


def _row_spec(tm, cols, cb=0):
    return pl.BlockSpec((tm, cols), lambda i, cb=cb: (i, cb))


def _bcast_spec(rows, cols):
    return pl.BlockSpec((rows, cols), lambda i: (0, 0))


def _rms_fwd(x, w, tm=1024):
    t, d = x.shape

    def body(x_ref, w_ref, o_ref):
        xv = x_ref[...]
        r = lax.rsqrt(jnp.mean(xv * xv, axis=-1, keepdims=True) + EPS)
        o_ref[...] = (xv * r * w_ref[...]).astype(BF16)

    return pl.pallas_call(
        body, name="rms_fwd", grid=(t // tm,),
        in_specs=[_row_spec(tm, d), _bcast_spec(1, d)], out_specs=_row_spec(tm, d),
        out_shape=jax.ShapeDtypeStruct((t, d), BF16), compiler_params=_cp("parallel"),
    )(x, w)


def _rms_bwd(x, w, dy, resid, tm=1024):
    t, d = x.shape

    def body(x_ref, w_ref, dy_ref, res_ref, dx_ref, dw_ref):
        xv = x_ref[...]
        r = lax.rsqrt(jnp.mean(xv * xv, axis=-1, keepdims=True) + EPS)
        xh = xv * r
        dyv = dy_ref[...].astype(F32)
        dxh = dyv * w_ref[...]
        dx_ref[...] = res_ref[...] + r * (dxh - xh * jnp.mean(dxh * xh, axis=-1, keepdims=True))

        @pl.when(pl.program_id(0) == 0)
        def _():
            dw_ref[...] = jnp.zeros_like(dw_ref)

        dw_ref[...] += jnp.sum(dyv * xh, axis=0, keepdims=True)

    return pl.pallas_call(
        body, name="rms_bwd", grid=(t // tm,),
        in_specs=[_row_spec(tm, d), _bcast_spec(1, d), _row_spec(tm, d), _row_spec(tm, d)],
        out_specs=[_row_spec(tm, d), _bcast_spec(1, d)],
        out_shape=[jax.ShapeDtypeStruct((t, d), F32), jax.ShapeDtypeStruct((1, d), F32)],
        compiler_params=_cp("arbitrary"),
    )(x, w, dy, resid)


def _final_loss(h, w, target, tm=1024):
    t, d = h.shape

    def body(h_ref, w_ref, t_ref, loss_ref, dh_ref, dw_ref):
        hv = h_ref[...]
        r = lax.rsqrt(jnp.mean(hv * hv, axis=-1, keepdims=True) + EPS)
        xh = hv * r
        wv = w_ref[...]
        err = xh * wv - t_ref[...]
        dy = err * (1.0 / d)
        dxh = dy * wv
        dh_ref[...] = r * (dxh - xh * jnp.mean(dxh * xh, axis=-1, keepdims=True))

        @pl.when(pl.program_id(0) == 0)
        def _():
            dw_ref[...] = jnp.zeros_like(dw_ref)
            loss_ref[...] = jnp.zeros_like(loss_ref)

        dw_ref[...] += jnp.sum(dy * xh, axis=0, keepdims=True)
        part = 0.5 * jnp.sum(jnp.mean(err * err, axis=-1, keepdims=True), axis=0, keepdims=True)
        loss_ref[...] += part + jnp.zeros((1, HEAD), F32)

    return pl.pallas_call(
        body, name="final_loss", grid=(t // tm,),
        in_specs=[_row_spec(tm, d), _bcast_spec(1, d), _row_spec(tm, d)],
        out_specs=[_bcast_spec(1, HEAD), _row_spec(tm, d), _bcast_spec(1, d)],
        out_shape=[jax.ShapeDtypeStruct((1, HEAD), F32), jax.ShapeDtypeStruct((t, d), F32),
                   jax.ShapeDtypeStruct((1, d), F32)],
        compiler_params=_cp("arbitrary"),
    )(h, w, target)


def _ffn_in(hn, w_gate, w_up, tm=1024, tn=1408):
    t, d = hn.shape
    ff = w_gate.shape[1]

    def body(a_ref, wg_ref, wu_ref, g_ref, u_ref, act_ref):
        a = a_ref[...]
        g = _bdot(a, wg_ref[...])
        u = _bdot(a, wu_ref[...])
        g_ref[...] = g.astype(BF16)
        u_ref[...] = u.astype(BF16)
        gq = g.astype(BF16).astype(F32)
        act_ref[...] = (gq * _sigmoid(gq) * u.astype(BF16).astype(F32)).astype(BF16)

    tile = pl.BlockSpec((tm, tn), lambda i, j: (i, j))
    wspec = pl.BlockSpec((d, tn), lambda i, j: (0, j))
    return pl.pallas_call(
        body, name="ffn_in", grid=(t // tm, ff // tn),
        in_specs=[pl.BlockSpec((tm, d), lambda i, j: (i, 0)), wspec, wspec], out_specs=[tile] * 3,
        out_shape=[jax.ShapeDtypeStruct((t, ff), BF16)] * 3, compiler_params=_cp("parallel", "parallel"),
    )(hn, w_gate, w_up)


def _ffn_act_bwd(dh, w_down, g, u, tm=1024, tn=1408):
    t, d = dh.shape
    ff = w_down.shape[0]

    def body(a_ref, w_ref, g_ref, u_ref, dg_ref, du_ref):
        dv = _bdot(a_ref[...], w_ref[...], NT).astype(BF16).astype(F32)
        gv = g_ref[...].astype(F32)
        sg = _sigmoid(gv)
        dg_ref[...] = (dv * u_ref[...].astype(F32) * (sg + gv * sg * (1.0 - sg))).astype(BF16)
        du_ref[...] = (dv * gv * sg).astype(BF16)

    tile = pl.BlockSpec((tm, tn), lambda i, j: (i, j))
    return pl.pallas_call(
        body, name="ffn_act_bwd", grid=(t // tm, ff // tn),
        in_specs=[pl.BlockSpec((tm, d), lambda i, j: (i, 0)), pl.BlockSpec((tn, d), lambda i, j: (j, 0)),
                  tile, tile],
        out_specs=[tile] * 2, out_shape=[jax.ShapeDtypeStruct((t, ff), BF16)] * 2,
        compiler_params=_cp("parallel", "parallel"),
    )(dh, w_down, g, u)


def _gate_merge_fwd(pb, ya, yb, tm=1024):
    t, d = ya.shape
    cb = PB_GATE // d

    def body(ga_ref, gb_ref, ya_ref, yb_ref, o_ref):
        ga, gb = ga_ref[...].astype(F32), gb_ref[...].astype(F32)
        o_ref[...] = (_sigmoid(ga) * ya_ref[...].astype(F32) + _sigmoid(gb) * yb_ref[...].astype(F32)).astype(BF16)

    return pl.pallas_call(
        body, name="gate_merge_fwd", grid=(t // tm,),
        in_specs=[_row_spec(tm, d, cb), _row_spec(tm, d, cb + 1), _row_spec(tm, d), _row_spec(tm, d)],
        out_specs=_row_spec(tm, d),
        out_shape=jax.ShapeDtypeStruct((t, d), BF16), compiler_params=_cp("parallel"),
    )(pb, pb, ya, yb)


def _gate_merge_bwd(pb, ya, yb, dm, tm=1024):
    t, d = ya.shape
    cb = PB_GATE // d

    def body(ga_ref, gb_ref, ya_ref, yb_ref, dm_ref, dya_ref, dyb_ref, dga_ref, dgb_ref):
        dmv = dm_ref[...].astype(F32)
        sa = _sigmoid(ga_ref[...].astype(F32))
        sb = _sigmoid(gb_ref[...].astype(F32))
        dya_ref[...] = (dmv * sa).astype(BF16)
        dyb_ref[...] = (dmv * sb).astype(BF16)
        dga_ref[...] = (dmv * ya_ref[...].astype(F32) * sa * (1.0 - sa)).astype(BF16)
        dgb_ref[...] = (dmv * yb_ref[...].astype(F32) * sb * (1.0 - sb)).astype(BF16)

    return pl.pallas_call(
        body, name="gate_merge_bwd", grid=(t // tm,),
        in_specs=[_row_spec(tm, d, cb), _row_spec(tm, d, cb + 1), _row_spec(tm, d), _row_spec(tm, d),
                  _row_spec(tm, d)],
        out_specs=[_row_spec(tm, d)] * 4,
        out_shape=[jax.ShapeDtypeStruct((t, d), BF16)] * 4, compiler_params=_cp("parallel"),
    )(pb, pb, ya, yb, dm)


def _head_norm_fwd(o, pb, wn, tm=1024):
    t, d = o.shape
    nh = d // HEAD

    def body(o_ref, z_ref, w_ref, out_ref):
        wv = w_ref[...]
        for h in range(nh):
            sl = slice(h * HEAD, (h + 1) * HEAD)
            ov = o_ref[:, sl]
            zv = z_ref[:, sl].astype(F32)
            r = lax.rsqrt(jnp.mean(ov * ov, axis=-1, keepdims=True) + EPS)
            out_ref[:, sl] = (ov * r * wv * (zv * _sigmoid(zv))).astype(BF16)

    return pl.pallas_call(
        body, name="head_norm_fwd", grid=(t // tm,),
        in_specs=[_row_spec(tm, d), _row_spec(tm, d, PB_Z // d), _bcast_spec(1, HEAD)],
        out_specs=_row_spec(tm, d),
        out_shape=jax.ShapeDtypeStruct((t, d), BF16), compiler_params=_cp("parallel"),
    )(o, pb, wn)


def _head_norm_bwd(o, pb, wn, dout, tm=1024):
    t, d = o.shape
    nh = d // HEAD

    def body(o_ref, z_ref, w_ref, d_ref, do_ref, dz_ref, dw_ref):
        wv = w_ref[...]
        dw_acc = jnp.zeros((1, HEAD), F32)
        for h in range(nh):
            sl = slice(h * HEAD, (h + 1) * HEAD)
            ov = o_ref[:, sl]
            zv = z_ref[:, sl].astype(F32)
            dv = d_ref[:, sl].astype(F32)
            r = lax.rsqrt(jnp.mean(ov * ov, axis=-1, keepdims=True) + EPS)
            xh = ov * r
            sz = _sigmoid(zv)
            dn = dv * (zv * sz)
            dz_ref[:, sl] = (dv * xh * wv * (sz + zv * sz * (1.0 - sz))).astype(BF16)
            dxh = dn * wv
            do_ref[:, sl] = r * (dxh - xh * jnp.mean(dxh * xh, axis=-1, keepdims=True))
            dw_acc = dw_acc + jnp.sum(dn * xh, axis=0, keepdims=True)

        @pl.when(pl.program_id(0) == 0)
        def _():
            dw_ref[...] = jnp.zeros_like(dw_ref)

        dw_ref[...] += dw_acc

    return pl.pallas_call(
        body, name="head_norm_bwd", grid=(t // tm,),
        in_specs=[_row_spec(tm, d), _row_spec(tm, d, PB_Z // d), _bcast_spec(1, HEAD), _row_spec(tm, d)],
        out_specs=[_row_spec(tm, d), _row_spec(tm, d), _bcast_spec(1, HEAD)],
        out_shape=[jax.ShapeDtypeStruct((t, d), F32), jax.ShapeDtypeStruct((t, d), BF16),
                   jax.ShapeDtypeStruct((1, HEAD), F32)],
        compiler_params=_cp("arbitrary"),
    )(o, pb, wn, dout)


def _attn_bias(gi, hh, dil):
    i = lax.broadcasted_iota(jnp.int32, (BLOCK_A, BLOCK_A), 0)
    j = lax.broadcasted_iota(jnp.int32, (BLOCK_A, BLOCK_A), 1)
    hf = (gi * HEADS_PER_GROUP + hh + 1).astype(F32)
    slope = jnp.exp(jnp.full((1, BLOCK_A), -8.0 * math.log(2.0) / N_HEADS_A, F32) * hf) * float(dil)
    d_prev = (BLOCK_A + i - j).astype(F32)
    d_cur = (i - j).astype(F32)
    return -slope * d_prev, -slope * d_cur, j >= i, j <= i


ATTN_TOKENS = 2048


def _sub_rows(a, r, dil):
    start = a * BLOCK_A * dil + r
    return pl.ds(start, BLOCK_A) if dil == 1 else pl.ds(start, BLOCK_A, stride=dil)


def _attn_fwd_group(pa, gi, dil, tb=ATTN_TOKENS):
    t = pa.shape[0]
    tb = min(tb, t)
    hb = BLOCK_A * dil
    nb = tb // hb
    scale = HEAD ** -0.5

    def body(q_ref, k_ref, v_ref, kp_ref, vp_ref, o_ref, l_ref):
        hh = pl.program_id(0)
        step = pl.program_id(1)
        b_prev, b_cur, m_prev, m_cur = _attn_bias(gi, hh, dil)
        m_first = jnp.logical_and(m_prev, step > 0)
        for r in range(dil):
            kp, vp = kp_ref[_sub_rows(0, r, dil), :], vp_ref[_sub_rows(0, r, dil), :]
            for a in range(nb):
                rows = _sub_rows(a, r, dil)
                q, kc, vc = q_ref[rows, :], k_ref[rows, :], v_ref[rows, :]
                s_p = jnp.where(m_first if a == 0 else m_prev, _bdot(q, kp, NT) * scale + b_prev, -1e30)
                s_c = jnp.where(m_cur, _bdot(q, kc, NT) * scale + b_cur, -1e30)
                m = jnp.maximum(jnp.max(s_p, axis=-1, keepdims=True), jnp.max(s_c, axis=-1, keepdims=True))
                p_p = jnp.exp(s_p - m)
                p_c = jnp.exp(s_c - m)
                den = _rowsum(p_p) + _rowsum(p_c)
                o_ref[rows, :] = (_bdot(p_p, vp) + _bdot(p_c, vc)) / den
                l_ref[rows, :] = (m + jnp.log(den)) + jnp.zeros((BLOCK_A, HEAD), F32)
                kp, vp = kc, vc

    def col(base):
        return lambda hh, s: (s, base + hh)

    def col_prev(base):
        return lambda hh, s: (jnp.maximum(s * nb - 1, 0), base + hh)

    qb, kb, vb = gi * HEADS_PER_GROUP, N_HEADS_A + gi * HEADS_PER_GROUP, 2 * N_HEADS_A + gi * HEADS_PER_GROUP
    ospec = pl.BlockSpec((tb, HEAD), lambda hh, s: (s, hh))
    return pl.pallas_call(
        body, name=f"attn_fwd_g{gi}", grid=(HEADS_PER_GROUP, t // tb),
        in_specs=[pl.BlockSpec((tb, HEAD), col(qb)), pl.BlockSpec((tb, HEAD), col(kb)),
                  pl.BlockSpec((tb, HEAD), col(vb)),
                  pl.BlockSpec((hb, HEAD), col_prev(kb)), pl.BlockSpec((hb, HEAD), col_prev(vb))],
        out_specs=[ospec, ospec],
        out_shape=[jax.ShapeDtypeStruct((t, D_ATTN_OUT), F32)] * 2,
        compiler_params=_cp("parallel", "parallel"),
    )(pa, pa, pa, pa, pa)


def _attn_merge(os, ls, tm=1024):
    t, d = os[0].shape

    def body(o0, o1, o2, l0, l1, l2, y_ref, lse_ref):
        a0, a1, a2 = l0[...], l1[...], l2[...]
        m = jnp.maximum(jnp.maximum(a0, a1), a2)
        e0, e1, e2 = jnp.exp(a0 - m), jnp.exp(a1 - m), jnp.exp(a2 - m)
        den = e0 + e1 + e2
        y_ref[...] = (e0 * o0[...] + e1 * o1[...] + e2 * o2[...]) / den
        lse_ref[...] = m + jnp.log(den)

    return pl.pallas_call(
        body, name="attn_merge", grid=(t // tm,),
        in_specs=[_row_spec(tm, d)] * 6, out_specs=[_row_spec(tm, d)] * 2,
        out_shape=[jax.ShapeDtypeStruct((t, d), F32)] * 2,
        compiler_params=_cp("parallel"),
    )(*os, *ls)


def _attn_bwd_group(pa, dy, y, lse, gi, dil, tb=ATTN_TOKENS):
    t = pa.shape[0]
    tb = min(tb, t)
    hb = BLOCK_A * dil
    nb = tb // hb
    nsteps = t // tb
    scale = HEAD ** -0.5

    def body(q_ref, k_ref, v_ref, dy_ref, y_ref, l_ref, kp_ref, vp_ref, d_ref,
             dq_s, dk_s, dv_s, carry_k, carry_v):
        hh = pl.program_id(0)
        step = pl.program_id(1)

        @pl.when(step == 0)
        def _():
            carry_k[...] = jnp.zeros_like(carry_k)
            carry_v[...] = jnp.zeros_like(carry_v)

        b_prev, b_cur, m_prev, m_cur = _attn_bias(gi, hh, dil)
        m_first = jnp.logical_and(m_prev, step < nsteps - 1)
        for r in range(dil):
            halo = _sub_rows(0, r, dil)
            dk_in, dv_in = carry_k[halo, :], carry_v[halo, :]
            kp, vp = kp_ref[halo, :], vp_ref[halo, :]
            prev_rows = None
            dk_pend = dv_pend = None
            for a in range(nb):
                rows = _sub_rows(a, r, dil)
                q, kc, vc = q_ref[rows, :], k_ref[rows, :], v_ref[rows, :]
                dyb, lb = dy_ref[rows, :], l_ref[rows, :]
                delta = _rowsum(dyb * y_ref[rows, :])
                mp = m_first if a == 0 else m_prev
                s = _bdot(q, kp, NT) * scale + b_prev
                p = jnp.where(mp, jnp.exp(jnp.where(mp, s - lb, 0.0)), 0.0)
                ds = p * (_bdot(dyb, vp, NT) - delta)
                dq = _bdot(ds, kp)
                dk_prev, dv_prev = _bdot(ds, q, TN), _bdot(p, dyb, TN)
                if a == 0:
                    carry_k[halo, :] = dk_prev
                    carry_v[halo, :] = dv_prev
                else:
                    dk_s[prev_rows, :] = dk_pend + dk_prev
                    dv_s[prev_rows, :] = dv_pend + dv_prev
                s = _bdot(q, kc, NT) * scale + b_cur
                p = jnp.where(m_cur, jnp.exp(jnp.where(m_cur, s - lb, 0.0)), 0.0)
                ds = p * (_bdot(dyb, vc, NT) - delta)
                dq_s[rows, :] = dq + _bdot(ds, kc)
                dk_pend, dv_pend = _bdot(ds, q, TN), _bdot(p, dyb, TN)
                prev_rows, kp, vp = rows, kc, vc
            dk_s[prev_rows, :] = dk_pend + dk_in
            dv_s[prev_rows, :] = dv_pend + dv_in
        d_ref[:, :HEAD] = (dq_s[...] * scale).astype(BF16)
        d_ref[:, HEAD:2 * HEAD] = (dk_s[...] * scale).astype(BF16)
        d_ref[:, 2 * HEAD:] = dv_s[...].astype(BF16)

    def col(base):
        return lambda hh, s: (nsteps - 1 - s, base + hh)

    def col_prev(base):
        return lambda hh, s: (jnp.maximum((nsteps - 1 - s) * nb - 1, 0), base + hh)

    qb, kb, vb = gi * HEADS_PER_GROUP, N_HEADS_A + gi * HEADS_PER_GROUP, 2 * N_HEADS_A + gi * HEADS_PER_GROUP
    big, small = (tb, HEAD), (hb, HEAD)
    return pl.pallas_call(
        body, name=f"attn_bwd_g{gi}", grid=(HEADS_PER_GROUP, nsteps),
        in_specs=[pl.BlockSpec(big, col(qb)), pl.BlockSpec(big, col(kb)), pl.BlockSpec(big, col(vb)),
                  pl.BlockSpec(big, col(0)), pl.BlockSpec(big, col(0)), pl.BlockSpec(big, col(0)),
                  pl.BlockSpec(small, col_prev(kb)), pl.BlockSpec(small, col_prev(vb))],
        out_specs=pl.BlockSpec((tb, 3 * HEAD), col(0)),
        out_shape=jax.ShapeDtypeStruct((t, 3 * D_ATTN_OUT), BF16),
        scratch_shapes=[pltpu.VMEM(big, F32)] * 3 + [pltpu.VMEM(small, F32)] * 2,
        compiler_params=_cp("parallel", "arbitrary"),
    )(pa, pa, pa, dy, y, lse, pa, pa)


def _shift_down(cur, prev8, s):
    if s == 0:
        return cur
    rolled = pltpu.roll(cur, s, 0)
    prolled = pltpu.roll(prev8, s, 0)
    rid = lax.broadcasted_iota(jnp.int32, prev8.shape, 0)
    top = jnp.where(rid < s, prolled, rolled[:8])
    return jnp.concatenate([top, rolled[8:]], axis=0)


def _shift_up(cur, next8, s):
    if s == 0:
        return cur
    n = cur.shape[0]
    rolled = pltpu.roll(cur, n - s, 0)
    nrolled = pltpu.roll(next8, 8 - s, 0)
    rid = lax.broadcasted_iota(jnp.int32, next8.shape, 0)
    bottom = jnp.where(rid >= 8 - s, nrolled, rolled[n - 8:])
    return jnp.concatenate([rolled[:n - 8], bottom], axis=0)


def _conv(xv, prev8, wv):
    c = jnp.zeros_like(xv)
    shifted = []
    for s in range(CONV_WIDTH):
        xs = _shift_down(xv, prev8, s)
        shifted.append(xs)
        c = c + wv[CONV_WIDTH - 1 - s:CONV_WIDTH - s, :] * xs
    return c, shifted


def _head_expand(psv, first):
    tm = psv.shape[0]
    return jnp.concatenate([jnp.broadcast_to(psv[:, first + h:first + h + 1], (tm, HEAD))
                            for h in range(N_HEADS_B)], axis=1)


def _head_collect(x, first):
    lane = lax.broadcasted_iota(jnp.int32, (x.shape[0], HEAD), 1)
    out = jnp.zeros((x.shape[0], HEAD), F32)
    for h in range(N_HEADS_B):
        out = jnp.where(lane == first + h, x[:, h * HEAD:(h + 1) * HEAD], out)
    return out


def _dn_prep_fwd(pb, ps, conv_w, a_log_bc, dt_bias_bc, tm=256, exch=None):
    t = pb.shape[0]
    c3 = 3 * D_B
    r16 = tm // 16

    def body(x_ref, xp_ref, ps_ref, w_ref, al_ref, dt_ref, q_ref, k_ref, v_ref, g_ref, beta_ref):
        first = pl.program_id(0) > 0

        def silu_conv(cols):
            c, _ = _conv(x_ref[:, cols].astype(F32), jnp.where(first, xp_ref[8:, cols].astype(F32), 0.0),
                         w_ref[:, cols])
            return c * _sigmoid(c)

        for h in range(N_HEADS_B):
            sl = slice(h * HEAD, (h + 1) * HEAD)
            sq = silu_conv(sl)
            q_ref[:, sl] = sq * lax.rsqrt(_rowsum(sq * sq) + EPS) * (HEAD ** -0.5)
            sk = silu_conv(slice(D_B + h * HEAD, D_B + (h + 1) * HEAD))
            k_ref[:, sl] = sk * lax.rsqrt(_rowsum(sk * sk) + EPS)
            v_ref[:, sl] = silu_conv(slice(2 * D_B + h * HEAD, 2 * D_B + (h + 1) * HEAD))
        psv = ps_ref[...]
        beta_ref[...] = _sigmoid(_head_expand(psv, 0))
        g_ref[...] = -jnp.exp(al_ref[...]) * _softplus(_head_expand(psv, N_HEADS_B) + dt_ref[...])

    return _hosted_call(
        body, name="dn_prep_fwd", grid=(t // tm,),
        in_specs=[_row_spec(tm, c3, 0),
                  pl.BlockSpec((16, c3), lambda i: (jnp.maximum(i * r16 - 1, 0), 0)),
                  _row_spec(tm, HEAD),
                  _bcast_spec(CONV_WIDTH, c3), _bcast_spec(1, D_B), _bcast_spec(1, D_B)],
        out_specs=[_row_spec(tm, D_B)] * 5,
        out_shape=[jax.ShapeDtypeStruct((t, D_B), F32)] * 5, scratch_shapes=[],
        args=(pb, pb, ps, conv_w, a_log_bc, dt_bias_bc), dims=("parallel",), exch=exch)


def _dn_prep_bwd(pb, ps, conv_w, a_log_bc, dt_bias_bc, g, dq, dk, dv, dg, dbeta, tm=128, exch=None):
    t = pb.shape[0]
    c3 = 3 * D_B
    r16 = tm // 16

    def body(x_ref, xp_ref, ps_ref, w_ref, al_ref, dt_ref, g_ref, dq_ref, dk_ref, dv_ref, dg_ref, db_ref,
             dc_ref, dps_ref, dw_ref, dal_ref, ddt_ref):
        first = pl.program_id(0) > 0

        @pl.when(pl.program_id(0) == 0)
        def _():
            dw_ref[...] = jnp.zeros_like(dw_ref)
            dal_ref[...] = jnp.zeros_like(dal_ref)
            ddt_ref[...] = jnp.zeros_like(ddt_ref)

        def column_block(cols, d_ref, sl, mult, normed):
            c, shifted = _conv(x_ref[:, cols].astype(F32), jnp.where(first, xp_ref[8:, cols].astype(F32), 0.0),
                               w_ref[:, cols])
            sg = _sigmoid(c)
            dsilu = sg + c * sg * (1.0 - sg)
            dyv = d_ref[:, sl]
            if normed:
                sv = c * sg
                r = lax.rsqrt(_rowsum(sv * sv) + EPS)
                yh = sv * r
                dyv = dyv * mult
                dyv = r * (dyv - yh * _rowsum(dyv * yh))
            dcv = dyv * dsilu
            dc_ref[:, cols] = dcv
            for sft in range(CONV_WIDTH):
                j = CONV_WIDTH - 1 - sft
                dw_ref[j:j + 1, cols] += jnp.sum(dcv * shifted[sft], axis=0, keepdims=True)

        for h in range(N_HEADS_B):
            sl = slice(h * HEAD, (h + 1) * HEAD)
            column_block(sl, dq_ref, sl, HEAD ** -0.5, True)
            column_block(slice(D_B + h * HEAD, D_B + (h + 1) * HEAD), dk_ref, sl, 1.0, True)
            column_block(slice(2 * D_B + h * HEAD, 2 * D_B + (h + 1) * HEAD), dv_ref, sl, 1.0, False)
        psv = ps_ref[...]
        beta = _sigmoid(_head_expand(psv, 0))
        dgv = dg_ref[...]
        da = dgv * (-jnp.exp(al_ref[...])) * _sigmoid(_head_expand(psv, N_HEADS_B) + dt_ref[...])
        dps_ref[...] = _head_collect(db_ref[...] * beta * (1.0 - beta), 0) + _head_collect(da, N_HEADS_B)
        dal_ref[...] += jnp.sum(dgv * g_ref[...], axis=0, keepdims=True)
        ddt_ref[...] += jnp.sum(da, axis=0, keepdims=True)

    row = _row_spec(tm, D_B)
    return _hosted_call(
        body, name="dn_prep_bwd", grid=(t // tm,),
        in_specs=[_row_spec(tm, c3, 0),
                  pl.BlockSpec((16, c3), lambda i: (jnp.maximum(i * r16 - 1, 0), 0)),
                  _row_spec(tm, HEAD),
                  _bcast_spec(CONV_WIDTH, c3), _bcast_spec(1, D_B), _bcast_spec(1, D_B),
                  row, row, row, row, row, row],
        out_specs=[_row_spec(tm, c3), _row_spec(tm, HEAD), _bcast_spec(CONV_WIDTH, c3), _bcast_spec(1, D_B),
                   _bcast_spec(1, D_B)],
        out_shape=[jax.ShapeDtypeStruct((t, c3), F32), jax.ShapeDtypeStruct((t, HEAD), F32),
                   jax.ShapeDtypeStruct((CONV_WIDTH, c3), F32),
                   jax.ShapeDtypeStruct((1, D_B), F32), jax.ShapeDtypeStruct((1, D_B), F32)],
        scratch_shapes=[], args=(pb, pb, ps, conv_w, a_log_bc, dt_bias_bc, g, dq, dk, dv, dg, dbeta),
        dims=("arbitrary",), exch=exch)


def _conv_bwd_input(dc, conv_w, tm=256):
    t, c3 = dc.shape
    r8 = tm // 8
    nlast = t // 8 - 1
    nsteps = t // tm

    def body(d_ref, dn_ref, w_ref, o_ref):
        not_last = pl.program_id(0) < nsteps - 1
        for cb in range(c3 // HEAD):
            cols = slice(cb * HEAD, (cb + 1) * HEAD)
            next8 = jnp.where(not_last, dn_ref[:, cols], 0.0)
            dv = d_ref[:, cols]
            wv = w_ref[:, cols]
            acc = jnp.zeros_like(dv)
            for s in range(CONV_WIDTH):
                acc = acc + wv[CONV_WIDTH - 1 - s:CONV_WIDTH - s, :] * _shift_up(dv, next8, s)
            o_ref[:, cols] = acc.astype(BF16)

    return pl.pallas_call(
        body, name="conv_bwd_input", grid=(nsteps,),
        in_specs=[_row_spec(tm, c3), pl.BlockSpec((8, c3), lambda i: (jnp.minimum((i + 1) * r8, nlast), 0)),
                  _bcast_spec(CONV_WIDTH, c3)],
        out_specs=_row_spec(tm, c3),
        out_shape=jax.ShapeDtypeStruct((t, c3), BF16), compiler_params=_cp("parallel"),
    )(dc, dc, conv_w)


def _lanes(x):
    return x[:, :CH]


def _tri_inv(a_list, r, c):
    eye = (r == c).astype(F32)
    b16 = (r >> 4) == (c >> 4)
    b32 = (r >> 5) == (c >> 5)
    ns = [jnp.where(b16, -a, 0.0) for a in a_list]
    xs = [eye + n for n in ns]
    ps = [_bdot(n, n) for n in ns]
    for last in (False, False, True):
        xs = [x + _bdot(x, p) for x, p in zip(xs, ps)]
        if not last:
            ps = [_bdot(p, p) for p in ps]
    for mask in (jnp.logical_and(b32, jnp.logical_not(b16)), jnp.logical_not(b32)):
        ts = [_bdot(x, jnp.where(mask, a, 0.0)) for x, a in zip(xs, a_list)]
        xs = [x - _bdot(t, x) for x, t in zip(xs, ts)]
    return xs


def _chunk_local(qs, ks, vs, gs, betas, solved=None):
    r = lax.broadcasted_iota(jnp.int32, (CH, CH), 0)
    c = lax.broadcasted_iota(jnp.int32, (CH, CH), 1)
    incl, strict = r >= c, r > c
    lm = incl.astype(F32)
    cums = [_hdot(lm, jnp.concatenate([g, jnp.where(strict, _lanes(g), 0.0)], axis=1)) for g in gs]
    gcbs = [cm[:, :HEAD] for cm in cums]
    decays = [jnp.where(incl, jnp.exp(jnp.where(incl, cm[:, HEAD:], 0.0)), 0.0) for cm in cums]
    bcols = [_lanes(b) for b in betas]
    kks = [_bdot(k, k, NT) for k in ks]
    qkraws = [_bdot(q, k, NT) for q, k in zip(qs, ks)]
    egs = [jnp.exp(gcb) for gcb in gcbs]
    if solved is None:
        tms = _tri_inv([jnp.where(strict, bc * kk * dc, 0.0) for bc, kk, dc in zip(bcols, kks, decays)], r, c)
        sols = [_hdot(tm, jnp.concatenate([b * v, b * eg * k], axis=1))
                for tm, b, v, eg, k in zip(tms, betas, vs, egs, ks)]
        ubars, ws = [sol[:, :HEAD] for sol in sols], [sol[:, HEAD:] for sol in sols]
    else:
        tms, ubars, ws = solved
    gls = [gcb[CH - 1:CH, :] for gcb in gcbs]
    eks = [jnp.exp(gl - gcb) for gl, gcb in zip(gls, gcbs)]
    return [dict(incl=incl, strict=strict, r=r, c=c, decay=dc, bcol=bc, kk=kk, tm=tm, eg=eg,
                 u_bar=ub, w=w, qkraw=qkraw, gl=gl, ek=ek)
            for dc, bc, kk, tm, eg, ub, w, qkraw, gl, ek
            in zip(decays, bcols, kks, tms, egs, ubars, ws, qkraws, gls, eks)]


def _dn_chunk_fwd(q, k, v, g, beta, cps=CHUNK_GROUP):
    t = q.shape[0]
    tm = cps * CH

    def body(q_ref, k_ref, v_ref, g_ref, b_ref, ub_ref, w_ref, qd_ref, kd_ref, qk_ref, ti_ref, gl_ref):
        for base in range(0, cps, CHUNK_GROUP):
            sls = [slice((base + j) * CH, (base + j + 1) * CH) for j in range(CHUNK_GROUP)]
            qs, ks = [q_ref[sl, :] for sl in sls], [k_ref[sl, :] for sl in sls]
            locs = _chunk_local(qs, ks, [v_ref[sl, :] for sl in sls], [g_ref[sl, :] for sl in sls],
                                [b_ref[sl, :] for sl in sls])
            for j, (sl, qv, kv, loc) in enumerate(zip(sls, qs, ks, locs)):
                ub_ref[sl, :] = loc["u_bar"]
                w_ref[sl, :] = loc["w"]
                qd_ref[sl, :] = qv * loc["eg"]
                kd_ref[sl, :] = kv * loc["ek"]
                qk_ref[sl, :] = loc["qkraw"] * loc["decay"]
                ti_ref[sl, :] = loc["tm"]
                gl_ref[base + j:base + j + 1, :] = jnp.exp(loc["gl"])

    hspec = pl.BlockSpec((tm, HEAD), lambda h, i: (i, h))
    sq_spec = pl.BlockSpec((None, tm, CH), lambda h, i: (h, i, 0))
    sq_shape = jax.ShapeDtypeStruct((N_HEADS_B, t, CH), F32)
    return pl.pallas_call(
        body, name="dn_chunk_fwd", grid=(N_HEADS_B, t // tm),
        in_specs=[hspec] * 5,
        out_specs=[hspec] * 4 + [sq_spec, sq_spec, pl.BlockSpec((cps, HEAD), lambda h, i: (i, h))],
        out_shape=[jax.ShapeDtypeStruct((t, D_B), F32)] * 4
        + [sq_shape, sq_shape, jax.ShapeDtypeStruct((t // CH, D_B), F32)],
        compiler_params=_cp("parallel", "parallel"),
    )(q, k, v, g, beta)


def _dn_scan_fwd(ub, w, qd, kd, qk, gl, cps=8, hg=2 * SCAN_HEADS):
    t = ub.shape[0]
    tm = cps * CH
    hs = list(range(hg))

    def body(ub_ref, w_ref, qd_ref, kd_ref, qk_ref, gl_ref, o_ref, st_ref, s_acc):
        @pl.when(pl.program_id(1) == 0)
        def _():
            s_acc[...] = jnp.zeros_like(s_acc)

        for ci in range(cps):
            sl = slice(ci * CH, (ci + 1) * CH)
            cols = [slice(h * HEAD, (h + 1) * HEAD) for h in hs]
            svs = [s_acc[h] for h in hs]
            for h in hs:
                st_ref[h, ci * HEAD:(ci + 1) * HEAD, :] = svs[h]
            us = [ub_ref[sl, cols[h]] - _bdot(w_ref[sl, cols[h]], svs[h]) for h in hs]
            for h in hs:
                s_acc[h] = gl_ref[ci:ci + 1, cols[h]] * svs[h] + _bdot(kd_ref[sl, cols[h]], us[h], TN)
            for h in hs:
                o_ref[sl, cols[h]] = _bdot(qd_ref[sl, cols[h]], svs[h]) + _bdot(qk_ref[h, sl, :], us[h])

    hspec = pl.BlockSpec((tm, hg * HEAD), lambda h, i: (i, h))
    return pl.pallas_call(
        body, name="dn_scan_fwd", grid=(N_HEADS_B // hg, t // tm),
        in_specs=[hspec] * 4 + [pl.BlockSpec((hg, tm, CH), lambda h, i: (h, i, 0)),
                                pl.BlockSpec((cps, hg * HEAD), lambda h, i: (i, h))],
        out_specs=[hspec, pl.BlockSpec((hg, cps * HEAD, HEAD), lambda h, i: (h, i, 0))],
        out_shape=[jax.ShapeDtypeStruct((t, D_B), F32),
                   jax.ShapeDtypeStruct((N_HEADS_B, (t // CH) * HEAD, HEAD), F32)],
        scratch_shapes=[pltpu.VMEM((hg, HEAD, HEAD), F32)],
        compiler_params=_cp("parallel", "arbitrary"),
    )(ub, w, qd, kd, qk, gl)


def _dn_scan_bwd(ub, w, qd, kd, qk, gl, st, do, cps=8):
    t = ub.shape[0]
    tm = cps * CH
    ns = t // tm

    hg = SCAN_HEADS
    hs = list(range(hg))

    def body(ub_ref, w_ref, qd_ref, kd_ref, qk_ref, gl_ref, st_ref, do_ref,
             dub_ref, dw_ref, dqd_ref, dkd_ref, dqk_ref, dgl_ref, ds_acc):
        @pl.when(pl.program_id(1) == 0)
        def _():
            ds_acc[...] = jnp.zeros_like(ds_acc)

        for ci in reversed(range(cps)):
            sl = slice(ci * CH, (ci + 1) * CH)
            cols = [slice(h * HEAD, (h + 1) * HEAD) for h in hs]
            svs = [st_ref[h, ci * HEAD:(ci + 1) * HEAD, :] for h in hs]
            wvs = [w_ref[sl, cols[h]] for h in hs]
            dovs = [do_ref[sl, cols[h]] for h in hs]
            dsvs = [ds_acc[h] for h in hs]
            us = [ub_ref[sl, cols[h]] - _bdot(wvs[h], svs[h]) for h in hs]
            dus = [_bdot(kd_ref[sl, cols[h]], dsvs[h]) + _bdot(qk_ref[h, sl, :], dovs[h], TN) for h in hs]
            for h in hs:
                ds_acc[h] = (gl_ref[ci:ci + 1, cols[h]] * dsvs[h] + _bdot(qd_ref[sl, cols[h]], dovs[h], TN)
                             - _bdot(wvs[h], dus[h], TN))
            for h in hs:
                dgl_ref[ci:ci + 1, cols[h]] = (jnp.sum(_rowsum(dsvs[h] * svs[h]), axis=0, keepdims=True)
                                              + jnp.zeros((1, HEAD), F32))
                dkd_ref[sl, cols[h]] = _bdot(us[h], dsvs[h], NT)
                dqd_ref[sl, cols[h]] = _bdot(dovs[h], svs[h], NT)
                dqk_ref[h, sl, :] = _bdot(dovs[h], us[h], NT)
                dub_ref[sl, cols[h]] = dus[h]
                dw_ref[sl, cols[h]] = -_bdot(dus[h], svs[h], NT)

    hspec = pl.BlockSpec((tm, hg * HEAD), lambda h, i: (ns - 1 - i, h))
    qkspec = pl.BlockSpec((hg, tm, CH), lambda h, i: (h, ns - 1 - i, 0))
    glspec = pl.BlockSpec((cps, hg * HEAD), lambda h, i: (ns - 1 - i, h))
    return pl.pallas_call(
        body, name="dn_scan_bwd", grid=(N_HEADS_B // hg, ns),
        in_specs=[hspec] * 4 + [qkspec, glspec,
                                pl.BlockSpec((hg, cps * HEAD, HEAD), lambda h, i: (h, ns - 1 - i, 0)), hspec],
        out_specs=[hspec] * 4 + [qkspec, glspec],
        out_shape=[jax.ShapeDtypeStruct((t, D_B), F32)] * 4
        + [jax.ShapeDtypeStruct((N_HEADS_B, t, CH), F32), jax.ShapeDtypeStruct((t // CH, D_B), F32)],
        scratch_shapes=[pltpu.VMEM((hg, HEAD, HEAD), F32)],
        compiler_params=_cp("parallel", "arbitrary"),
    )(ub, w, qd, kd, qk, gl, st, do)


def _dn_chunk_bwd(q, k, v, g, beta, ub, w, tinv, dub, dw, dqd, dkd, dqk, dgl, cps=CHUNK_GROUP):
    t = q.shape[0]
    tm = cps * CH

    def body(q_ref, k_ref, v_ref, g_ref, b_ref, ub_ref, w_ref, dub_ref, dw_ref, dqd_ref, dkd_ref, ti_ref, dqk_ref,
             dgl_ref, dq_ref, dk_ref, dv_ref, dg_ref, db_ref):
        ones = jnp.ones((CH, HEAD), F32)
        rid = lax.broadcasted_iota(jnp.int32, (CH, HEAD), 0)

        def rest(ci, sl, qv, kv, vv, beta_v, loc, dr, da):
            incl = loc["incl"]
            eg, ek, decay, bcol, kk = loc["eg"], loc["ek"], loc["decay"], loc["bcol"], loc["kk"]
            drv, drk = dr[:, :HEAD], dr[:, HEAD:]
            dv_ref[sl, :] = beta_v * drv
            beg = beta_v * eg
            t1 = drk * kv
            dbeta = _rowsum(drv * vv + t1 * eg) + _rowsum(da * kk * decay)
            dkk = da * bcol * decay
            dqk_m = jnp.where(incl, dqk_ref[sl, :], 0.0)
            ddecay = da * bcol * kk + dqk_m * loc["qkraw"]
            dqkraw = dqk_m * decay
            dqdv, dkdv = dqd_ref[sl, :], dkd_ref[sl, :]
            dq_ref[sl, :] = _bdot(dqkraw, kv) + dqdv * eg
            dk_ref[sl, :] = (beg * drk + _bdot(dqkraw, qv, TN) + _bdot(dkk, kv) + _bdot(dkk, kv, TN)
                             + dkdv * ek)
            e = ddecay * decay
            skd = _rowsum(dkdv * kv * ek)
            dgc = _rowsum(beg * t1) + _rowsum(e) + _rowsum(dqdv * qv * eg) - skd
            colsum = _hdot(e, ones, TN)
            last = jnp.sum(skd, axis=0, keepdims=True) + dgl_ref[ci:ci + 1, :] * jnp.exp(loc["gl"])
            db_ref[sl, :] = dbeta + jnp.zeros((CH, HEAD), F32)
            return (dgc - colsum) + jnp.where(rid == CH - 1, last, 0.0)

        for base in range(0, cps, CHUNK_GROUP):
            cis = list(range(base, base + CHUNK_GROUP))
            sls = [slice(ci * CH, (ci + 1) * CH) for ci in cis]
            qs, ks, vs = [q_ref[sl, :] for sl in sls], [k_ref[sl, :] for sl in sls], [v_ref[sl, :] for sl in sls]
            betas = [b_ref[sl, :] for sl in sls]
            locs = _chunk_local(qs, ks, vs, [g_ref[sl, :] for sl in sls], betas,
                                solved=([ti_ref[sl, :] for sl in sls], [ub_ref[sl, :] for sl in sls],
                                        [w_ref[sl, :] for sl in sls]))
            drs = [_hdot(loc["tm"], jnp.concatenate([dub_ref[sl, :], dw_ref[sl, :]], axis=1), TN)
                   for loc, sl in zip(locs, sls)]
            das = [jnp.where(loc["strict"],
                             -_hdot(dr, jnp.concatenate([loc["u_bar"], loc["w"]], axis=1), NT), 0.0)
                   for loc, dr in zip(locs, drs)]
            dgcs = [rest(*args) for args in zip(cis, sls, qs, ks, vs, betas, locs, drs, das)]
            um = (locs[0]["r"] <= locs[0]["c"]).astype(F32)
            for sl, dgc_bc in zip(sls, dgcs):
                dg_ref[sl, :] = _hdot(um, dgc_bc)

    hspec = pl.BlockSpec((tm, HEAD), lambda h, i: (i, h))
    sq_spec = pl.BlockSpec((None, tm, CH), lambda h, i: (h, i, 0))
    return pl.pallas_call(
        body, name="dn_chunk_bwd", grid=(N_HEADS_B, t // tm),
        in_specs=[hspec] * 11 + [sq_spec, sq_spec, pl.BlockSpec((cps, HEAD), lambda h, i: (i, h))],
        out_specs=[hspec] * 5,
        out_shape=[jax.ShapeDtypeStruct((t, D_B), F32)] * 5,
        compiler_params=_cp("parallel", "parallel"),
    )(q, k, v, g, beta, ub, w, dub, dw, dqd, dkd, tinv, dqk, dgl)


FLIPS = [(fx, fy, fc) for fx in (0, 1) for fy in (0, 1) for fc in (0, 1)][1:]


def _mesh_pos():
    return lax.axis_index("x"), lax.axis_index("y"), lax.axis_index("c")


def _peer(pos, flip):
    return tuple((1 - p) if f else p for p, f in zip(pos, flip))


def _dev_index(pos):
    return 4 * pos[0] + 2 * pos[1] + pos[2]


class _Exchange:
    def __init__(self, tensors, scatter):
        self.tensors, self.scatter, self.nt = list(tensors), list(scatter), len(tensors)
        hbm = pl.BlockSpec(memory_space=pltpu.HBM)
        self.in_specs = [hbm] * self.nt
        self.out_specs = [hbm] * self.nt
        self.out_shape = [jax.ShapeDtypeStruct(x.shape if sc else (N_DEV,) + x.shape, x.dtype)
                          for x, sc in zip(tensors, scatter)]
        self.scratch_shapes = [pltpu.SemaphoreType.DMA((self.nt * 7,)), pltpu.SemaphoreType.DMA((self.nt * 7,)),
                               pltpu.SemaphoreType.DMA((self.nt,))]

    def _copies(self, ins, outs, sems):
        send_sems, recv_sems, local_sems = sems
        pos = _mesh_pos()
        me = _dev_index(pos)

        def remote(ti, fi, landing):
            peer = _peer(pos, FLIPS[fi])
            src = ins[ti].at[_dev_index(peer)] if self.scatter[ti] else ins[ti]
            return pltpu.make_async_remote_copy(
                src_ref=src, dst_ref=outs[ti].at[landing(peer)],
                send_sem=send_sems.at[ti * 7 + fi], recv_sem=recv_sems.at[ti * 7 + fi],
                device_id=peer, device_id_type=pl.DeviceIdType.MESH)

        pairs = [(ti, fi) for ti in range(self.nt) for fi in range(7)]
        local = [pltpu.make_async_copy(ins[ti].at[me] if self.scatter[ti] else ins[ti], outs[ti].at[me],
                                       local_sems.at[ti]) for ti in range(self.nt)]
        sends = [remote(ti, fi, lambda peer: me) for ti, fi in pairs]
        recvs = [remote(ti, fi, _dev_index) for ti, fi in pairs]
        return local, sends, recvs

    def start(self, ins, outs, sems):
        local, sends, _ = self._copies(ins, outs, sems)
        for cp in local + sends:
            cp.start()

    def wait(self, ins, outs, sems):
        local, sends, recvs = self._copies(ins, outs, sems)
        for cp in recvs:
            cp.wait_recv()
        for cp in sends:
            cp.wait_send()
        for cp in local:
            cp.wait()


def _exchange(tensors, scatter, name):
    ex = _Exchange(tensors, scatter)

    def body(*refs):
        ins, outs, sems = refs[:ex.nt], refs[ex.nt:2 * ex.nt], refs[2 * ex.nt:]
        ex.start(ins, outs, sems)
        ex.wait(ins, outs, sems)

    return pl.pallas_call(
        body, name=name, in_specs=ex.in_specs, out_specs=ex.out_specs, out_shape=ex.out_shape,
        scratch_shapes=ex.scratch_shapes, compiler_params=pltpu.CompilerParams(has_side_effects=True),
    )(*tensors)


def _gather_two_level(tensors, name):
    nt = len(tensors)
    hbm = pl.BlockSpec(memory_space=pltpu.HBM)

    def body(*refs):
        ins, outs = refs[:nt], refs[nt:2 * nt]
        send_sems, recv_sems, local_sems = refs[2 * nt:]
        x, y, c = _mesh_pos()
        sibling = (x, y, 1 - c)
        chips = [(1 - x, y), (x, 1 - y), (1 - x, 1 - y)]

        def copy(ti, k, block, to, own=False):
            slot = outs[ti].at[_dev_index(block)]
            return pltpu.make_async_remote_copy(
                src_ref=ins[ti] if own else slot, dst_ref=slot,
                send_sem=send_sems.at[ti * 7 + k], recv_sem=recv_sems.at[ti * 7 + k],
                device_id=to, device_id_type=pl.DeviceIdType.MESH)

        me = (x, y, c)
        mine = [pltpu.make_async_copy(ins[ti], outs[ti].at[_dev_index(me)], local_sems.at[ti]) for ti in range(nt)]
        first = [copy(ti, 0, me, sibling, own=True) for ti in range(nt)]
        first += [copy(ti, 1 + j, me, (*chip, c), own=True) for ti in range(nt) for j, chip in enumerate(chips)]
        for cp in mine + first:
            cp.start()
        passed = []
        for j, chip in enumerate(chips):
            for ti in range(nt):
                copy(ti, 1 + j, (*chip, c), me).wait_recv()
                cp = copy(ti, 4 + j, (*chip, c), sibling)
                cp.start()
                passed.append(cp)
        for ti in range(nt):
            copy(ti, 0, sibling, me).wait_recv()
            for j, chip in enumerate(chips):
                copy(ti, 4 + j, (*chip, 1 - c), me).wait_recv()
        for cp in first + passed:
            cp.wait_send()
        for cp in mine:
            cp.wait()

    return pl.pallas_call(
        body, name=name, in_specs=[hbm] * nt, out_specs=[hbm] * nt,
        out_shape=[jax.ShapeDtypeStruct((N_DEV,) + x.shape, x.dtype) for x in tensors],
        scratch_shapes=[pltpu.SemaphoreType.DMA((nt * 7,)), pltpu.SemaphoreType.DMA((nt * 7,)),
                        pltpu.SemaphoreType.DMA((nt,))],
        compiler_params=pltpu.CompilerParams(has_side_effects=True),
    )(*tensors)


def _hosted_call(body, *, name, grid, in_specs, out_specs, out_shape, scratch_shapes, args, dims, exch=None):
    if exch is None:
        return pl.pallas_call(body, name=name, grid=grid, in_specs=in_specs, out_specs=out_specs,
                              out_shape=out_shape, scratch_shapes=scratch_shapes,
                              compiler_params=_cp(*dims))(*args)
    n_in, n_out, n_sc, ne = len(in_specs), len(out_specs), len(scratch_shapes), exch.nt
    nsteps = math.prod(grid)

    def wrapped(*refs):
        ins, ex_in = refs[:n_in], refs[n_in:n_in + ne]
        outs = refs[n_in + ne:n_in + ne + n_out]
        ex_out = refs[n_in + ne + n_out:n_in + 2 * ne + n_out]
        rest = refs[n_in + 2 * ne + n_out:]
        scratch, sems = rest[:n_sc], rest[n_sc:]
        step = pl.program_id(0)
        for ax in range(1, len(grid)):
            step = step * grid[ax] + pl.program_id(ax)

        @pl.when(step == 0)
        def _():
            exch.start(ex_in, ex_out, sems)

        body(*ins, *outs, *scratch)

        @pl.when(step == nsteps - 1)
        def _():
            exch.wait(ex_in, ex_out, sems)

    return pl.pallas_call(
        wrapped, name=name, grid=grid, in_specs=list(in_specs) + exch.in_specs,
        out_specs=list(out_specs) + exch.out_specs, out_shape=list(out_shape) + exch.out_shape,
        scratch_shapes=list(scratch_shapes) + exch.scratch_shapes,
        compiler_params=pltpu.CompilerParams(dimension_semantics=("arbitrary",) * len(grid),
                                             vmem_limit_bytes=VMEM_LIMIT, has_side_effects=True),
    )(*args, *exch.tensors)


def _adamw(land, w, m, v, name, tm=256):
    n, r, c = land.shape
    tm = r if r <= tm else max(s for s in range(8, tm + 1, 8) if r % s == 0)
    bc1 = 1.0 / (1.0 - ADAM_B1 ** ADAM_STEP)
    bc2 = 1.0 / (1.0 - ADAM_B2 ** ADAM_STEP)

    def body(l_ref, w_ref, m_ref, v_ref, g_ref, d_ref, nm_ref, nv_ref):
        g = l_ref[0].astype(F32)
        for i in range(1, n):
            g = g + l_ref[i].astype(F32)
        nm = ADAM_B1 * m_ref[...] + (1.0 - ADAM_B1) * g
        nv = ADAM_B2 * v_ref[...] + (1.0 - ADAM_B2) * (g * g)
        g_ref[...] = g
        nm_ref[...] = nm
        nv_ref[...] = nv
        d_ref[...] = -ADAM_LR * ((nm * bc1) / (jnp.sqrt(nv * bc2) + ADAM_EPS) + ADAM_WD * w_ref[...])

    spec = pl.BlockSpec((tm, c), lambda i: (i, 0))
    return pl.pallas_call(
        body, name=name, grid=(r // tm,),
        in_specs=[pl.BlockSpec((n, tm, c), lambda i: (0, i, 0)), spec, spec, spec],
        out_specs=[spec] * 4, out_shape=[jax.ShapeDtypeStruct((r, c), F32)] * 4,
        compiler_params=_cp("parallel"),
    )(land, w, m, v)


PACK_W = 2048


def _pack_rows(parts):
    flat = jnp.concatenate([p.reshape(-1).astype(F32) for p in parts])
    pad = (-flat.shape[0]) % (8 * PACK_W)
    return jnp.pad(flat, (0, pad)).reshape(-1, PACK_W)


def _unpack_rows(packed, shapes):
    flat = packed.reshape(-1)
    out, off = [], 0
    for s in shapes:
        n = math.prod(s)
        out.append(flat[off:off + n].reshape(s))
        off += n
    return out


def _col_slabs(gfull, width):
    r = gfull.shape[0]
    return jnp.transpose(gfull.reshape(r, N_DEV, width), (1, 0, 2)).astype(BF16)


def _row_slabs(gfull):
    return gfull.reshape(N_DEV, gfull.shape[0] // N_DEV, gfull.shape[1]).astype(BF16)


def _from_col_slabs(gathered):
    n, r, width = gathered.shape
    return jnp.transpose(gathered, (1, 0, 2)).reshape(r, n * width)


def _local_step(xs, target, norm_mix, wf_in, cw, a_log, dt_bias, dn_norm, rest, norm_ffn, norm_final,
                distributed=True):
    d = D_MODEL
    n_main = D_PA + 4 * D_B
    w_pa_cols = wf_in[:, :D_PA]
    w_pb_cols = jnp.concatenate([wf_in[:, D_PA:n_main], wf_in[:, n_main + 2 * N_HEADS_B:]], axis=1)
    w_small = jnp.pad(wf_in[:, n_main:n_main + 2 * N_HEADS_B], ((0, 0), (0, HEAD - 2 * N_HEADS_B)))
    a_log_bc = jnp.repeat(a_log, HEAD, axis=1)
    dt_bias_bc = jnp.repeat(dt_bias, HEAD, axis=1)

    u = _rms_fwd(xs, norm_mix)
    pa = _matmul(u, w_pa_cols, "nn", F32, 1024, 1536, d, name="proj_a")
    pb = _matmul(u, w_pb_cols, "nn", BF16, 1024, 1024, d, name="proj_b")
    ps = _matmul(u, w_small, "nn", F32, 2048, HEAD, d, name="proj_small")
    os_, ls_ = [], []
    for gi, dil in enumerate(DILATIONS):
        o_g, l_g = _attn_fwd_group(pa, gi, dil)
        os_.append(o_g)
        ls_.append(l_g)
    y_att, lse = _attn_merge(os_, ls_)
    prep = _dn_prep_fwd(pb, ps, cw, a_log_bc, dt_bias_bc,
                        exch=_Exchange(rest, [False] * 6) if distributed else None)
    qn, kn, vn, gdec, beta = prep[:5]
    if distributed:
        g_pa, g_pd, g_out, g_gate, g_up, g_down = prep[5:]
        wf_pa, wf_pd, wf_out = _from_col_slabs(g_pa), g_pd.reshape(D_B, d), g_out.reshape(d, d)
        wf_gate, wf_up, wf_down = _from_col_slabs(g_gate), _from_col_slabs(g_up), g_down.reshape(D_FF, d)
    else:
        wf_pa, wf_pd, wf_out, wf_gate, wf_up, wf_down = rest
    wf_gu = jnp.concatenate([wf_gate, wf_up], axis=1)
    ub, ww, qd, kd, qk, tinv, gl = _dn_chunk_fwd(qn, kn, vn, gdec, beta)
    o_dn, states = _dn_scan_fwd(ub, ww, qd, kd, qk, gl)
    o_gated = _head_norm_fwd(o_dn, pb, dn_norm)
    y_a = _matmul(y_att, wf_pa, "nn", BF16, 1024, d, D_ATTN_OUT, name="proj_attn")
    y_b = _matmul(o_gated, wf_pd, "nn", BF16, 1024, d, d, name="proj_delta")
    merged = _gate_merge_fwd(pb, y_a, y_b)
    h1 = _matmul(merged, wf_out, "nn", F32, 1024, d, d, add=xs, name="out_proj")
    hn = _rms_fwd(h1, norm_ffn)
    gate, up, act = _ffn_in(hn, wf_gate, wf_up)
    h2 = _matmul(act, wf_down, "nn", F32, 512, d, D_FF, add=h1, name="ffn_out")
    loss_part, dh2, d_norm_final = _final_loss(h2, norm_final.reshape(1, d), target)

    dgate, dup = _ffn_act_bwd(dh2, wf_down, gate, up)
    gw_down = _matmul(act, dh2, "tn", BF16, 1408, d, TOKEN_TK, name="gw_down")
    dhn = _matmul_nt_segments([([dgate, dup], wf_gu, 1408)], BF16, 1024, "d_hn")[0]
    gw_gate = _matmul(hn, dgate, "tn", BF16, d, 1408, TOKEN_TK, name="gw_gate")
    gw_up = _matmul(hn, dup, "tn", BF16, d, 1408, TOKEN_TK, name="gw_up")
    dh1, d_norm_ffn = _rms_bwd(h1, norm_ffn, dhn, dh2)
    dmerged = _matmul(dh1, wf_out, "nt", BF16, 1024, d, d, name="d_merged")
    gw_out = _matmul(merged, dh1, "tn", BF16, d, d, TOKEN_TK, name="gw_out")
    dya, dyb, dga, dgb = _gate_merge_bwd(pb, y_a, y_b, dmerged)
    dy_att = _matmul(dya, wf_pa, "nt", F32, 1024, D_ATTN_OUT, d, name="d_y_att")
    gw_pa = _matmul(y_att, dya, "tn", BF16, D_ATTN_OUT, d, TOKEN_TK, name="gw_pa")
    do_gated = _matmul(dyb, wf_pd, "nt", BF16, 1024, d, d, name="d_o_gated")
    gw_pd = _matmul(o_gated, dyb, "tn", BF16, d, d, TOKEN_TK, name="gw_pd")
    do_dn, dz, d_dn_norm = _head_norm_bwd(o_dn, pb, dn_norm, do_gated)
    dub, dww, dqd, dkd, dqk, dgl = _dn_scan_bwd(ub, ww, qd, kd, qk, gl, states, do_dn)
    dqn, dkn, dvn, dgdec, dbeta = _dn_chunk_bwd(qn, kn, vn, gdec, beta, ub, ww, tinv, dub, dww, dqd, dkd, dqk, dgl)
    slabs = [_col_slabs(gw_pa, d // N_DEV), _row_slabs(gw_pd), _row_slabs(gw_out),
             _col_slabs(gw_gate, D_FF // N_DEV), _col_slabs(gw_up, D_FF // N_DEV),
             _row_slabs(gw_down)] if distributed else None
    prep = _dn_prep_bwd(pb, ps, cw, a_log_bc, dt_bias_bc, gdec, dqn, dkn, dvn, dgdec, dbeta,
                        exch=_Exchange(slabs, [True] * 6) if distributed else None)
    dc, dps, d_conv_full, d_alog_bc, d_dt_bc = prep[:5]
    dqkv_pre = _conv_bwd_input(dc, cw)
    segs = [_attn_bwd_group(pa, dy_att, y_att, lse, gi, dil) for gi, dil in enumerate(DILATIONS)]
    segs += [dqkv_pre, dz, dga, dgb]
    gws = [_matmul(u, s, "tn", BF16, d, 1536, TOKEN_TK, name=f"gw_in_{i}") for i, s in enumerate(segs)]
    gw_small = _matmul(u, dps, "tn", BF16, d, HEAD, TOKEN_TK, name="gw_in_small")
    g_att = jnp.concatenate(gws[:3], axis=1).reshape(d, N_HEADS_A, 3, HEAD)
    gw_in = jnp.concatenate(
        [g_att[:, :, i, :].reshape(d, D_ATTN) for i in range(3)]
        + [gws[3], gws[4], gw_small[:, :2 * N_HEADS_B], gws[5], gws[6]], axis=1)
    w_att = jnp.stack([w_pa_cols[:, i * D_ATTN:(i + 1) * D_ATTN].reshape(d, N_HEADS_A, HEAD) for i in range(3)],
                      axis=2).reshape(d, D_PA)
    du_small = _matmul(dps, w_small, "nt", F32, 1024, d, HEAD, name="d_u_small")
    du_call = _matmul_nt_segments(
        [(segs[:4], jnp.concatenate([w_att, w_pb_cols[:, :3 * D_B]], axis=1), 768),
         (segs[4:], w_pb_cols[:, 3 * D_B:], 1024)], BF16, 1024, "d_u", add=du_small,
        exch=_Exchange([_col_slabs(gw_in, SHARD_IN)], [True]) if distributed else None)
    dx, d_norm_mix = _rms_bwd(xs, norm_mix, du_call[0], dh1)
    d_a_log = d_alog_bc.reshape(1, N_HEADS_B, HEAD)[:, :, 0]
    d_dt_bias = d_dt_bc.reshape(1, N_HEADS_B, HEAD)[:, :, 0]
    small = (d_conv_full, d_norm_mix, d_norm_ffn, d_norm_final, d_dn_norm, d_a_log, d_dt_bias)
    if distributed:
        return (loss_part, dx, [du_call[1]] + list(prep[5:])) + small
    return (loss_part, dx, gw_in, gw_pa, gw_pd, gw_out, jnp.concatenate([gw_gate, gw_up], axis=1), gw_down) + small


def kernel(x, norm_mix, w_in, conv_w, a_log, dt_bias, dn_norm, w_proj_attn, w_proj_delta, w_out, norm_ffn, w_gate, w_up, w_down, norm_final, loss_target, m_norm_mix, m_w_in, m_conv_w, m_a_log, m_dt_bias, m_dn_norm, m_w_proj_attn, m_w_proj_delta, m_w_out, m_norm_ffn, m_w_gate, m_w_up, m_w_down, m_norm_final, v_norm_mix, v_w_in, v_conv_w, v_a_log, v_dt_bias, v_dn_norm, v_w_proj_attn, v_w_proj_delta, v_w_out, v_norm_ffn, v_w_gate, v_w_up, v_w_down, v_norm_final):
    d = D_MODEL
    xs = x[0]
    target = loss_target[0]
    me = _dev_index(_mesh_pos())

    g_in, g_conv = _gather_two_level([w_in[0].astype(BF16), conv_w[0]], "gather_w_in")
    rest = [w[0].astype(BF16) for w in (w_proj_attn, w_proj_delta, w_out, w_gate, w_up, w_down)]
    (loss_part, dx, landed, d_conv_full, d_norm_mix, d_norm_ffn, d_norm_final, d_dn_norm, d_a_log,
     d_dt_bias) = _local_step(xs, target, norm_mix, _from_col_slabs(g_in), _from_col_slabs(g_conv), a_log, dt_bias,
                              dn_norm, rest, norm_ffn, norm_final)

    small_shapes = [(1, d), (1, d), (d,), (1, HEAD), (1, N_HEADS_B), (1, N_HEADS_B), (1, 1), (CONV_WIDTH, 3 * D_B)]
    packed = _pack_rows([d_norm_mix, d_norm_ffn, d_norm_final, d_dn_norm, d_a_log, d_dt_bias,
                         loss_part[:, :1], d_conv_full])
    landed = list(landed) + list(_exchange([packed], [False], "gather_small_grads"))
    zero1 = jnp.zeros((1, 1), F32)
    zconv = jnp.zeros((CONV_WIDTH, 3 * D_B), F32)
    small_w = _pack_rows([norm_mix, norm_ffn, norm_final, dn_norm, a_log, dt_bias, zero1, zconv])
    small_m = _pack_rows([m_norm_mix, m_norm_ffn, m_norm_final, m_dn_norm, m_a_log, m_dt_bias, zero1, zconv])
    small_v = _pack_rows([v_norm_mix, v_norm_ffn, v_norm_final, v_dn_norm, v_a_log, v_dt_bias, zero1, zconv])
    small = [_unpack_rows(z, small_shapes) for z in _adamw(landed[7], small_w, small_m, small_v, "adamw_small")]
    loss = small[0][6].reshape(())
    conv_shard = 3 * D_B // N_DEV
    g_conv_own = lax.dynamic_slice_in_dim(small[0][7], me * conv_shard, conv_shard, axis=1)
    r_conv = _adamw(g_conv_own[None], conv_w[0], m_conv_w[0], v_conv_w[0], "adamw_conv")
    big = [_adamw(landed[i], w[0], m[0], v[0], f"adamw_{i}") for i, (w, m, v) in enumerate([
        (w_in, m_w_in, v_w_in), (w_proj_attn, m_w_proj_attn, v_w_proj_attn),
        (w_proj_delta, m_w_proj_delta, v_w_proj_delta), (w_out, m_w_out, v_w_out),
        (w_gate, m_w_gate, v_w_gate), (w_up, m_w_up, v_w_up), (w_down, m_w_down, v_w_down)])]

    def leaves(k):
        sm = small[k]
        return [sm[0], big[0][k][None], r_conv[k][None], sm[4], sm[5], sm[3], big[1][k][None], big[2][k][None],
                big[3][k][None], sm[1], big[4][k][None], big[5][k][None], big[6][k][None], sm[2]]

    return (loss, dx[None], *leaves(0), *leaves(1), *leaves(2), *leaves(3))
```

```python
import math

import jax
import jax.numpy as jnp
from jax import lax
from jax.experimental import pallas as pl
from jax.experimental.pallas import tpu as pltpu

F32 = jnp.float32
BF16 = jnp.bfloat16
HI = lax.Precision.HIGH

D_MODEL = 1024
N_DEV = 8
HEAD = 128
N_HEADS_A = 12
HEADS_PER_GROUP = 4
DILATIONS = (1, 4, 16)
BLOCK_A = 128
D_ATTN = N_HEADS_A * HEAD
D_ATTN_OUT = HEADS_PER_GROUP * HEAD
N_HEADS_B = 8
D_B = N_HEADS_B * HEAD
CONV_WIDTH = 4
CH = 64
CHUNK_GROUP = 32
SCAN_HEADS = 4
TOKEN_TK = 1024
D_FF = 2816
EPS = 1e-6
D_IN = 3 * D_ATTN + 4 * D_B + 2 * N_HEADS_B + 2 * D_MODEL
SHARD_IN = D_IN // N_DEV
PB_Z, PB_GATE = 3072, 4096
D_PA = 3 * D_ATTN
ADAM_LR, ADAM_B1, ADAM_B2, ADAM_EPS, ADAM_WD, ADAM_STEP = 0.001, 0.9, 0.999, 1e-08, 0.01, 10
VMEM_LIMIT = 56 * 1024 * 1024

NN = ((1,), (0,))
NT = ((1,), (1,))
TN = ((0,), (0,))


def _dot(a, b, dims=NN, prec=None):
    return lax.dot_general(a, b, (dims, ((), ())), precision=prec, preferred_element_type=F32)


def _bdot(a, b, dims=NN):
    return _dot(a.astype(BF16), b.astype(BF16), dims)


def _hdot(a, b, dims=NN):
    return _dot(a.astype(F32), b.astype(F32), dims, HI)


def _cp(*sem):
    return pltpu.CompilerParams(dimension_semantics=sem, vmem_limit_bytes=VMEM_LIMIT)


def _sigmoid(x):
    return 0.5 * jnp.tanh(0.5 * x) + 0.5


def _softplus(x):
    return jnp.maximum(x, 0.0) + jnp.log(1.0 + jnp.exp(-jnp.abs(x)))


def _rowsum(x):
    return jnp.sum(x, axis=-1, keepdims=True)


def _matmul(a, b, mode, out_dtype, tm, tn, tk, add=None, name="mm"):
    if mode == "nn":
        (m, k), (k2, n) = a.shape, b.shape
    elif mode == "nt":
        (m, k), (n, k2) = a.shape, b.shape
    else:
        (k, m), (k2, n) = a.shape, b.shape
    assert k == k2, (a.shape, b.shape, mode)
    tm, tn, tk = min(tm, m), min(tn, n), min(tk, k)
    assert m % tm == 0 and n % tn == 0 and k % tk == 0, (a.shape, b.shape, tm, tn, tk)
    nk = k // tk
    dims = {"nn": NN, "nt": NT, "tn": TN}[mode]

    def body(*refs):
        if add is None:
            a_ref, b_ref, o_ref, acc = refs
            add_ref = None
        else:
            a_ref, b_ref, add_ref, o_ref, acc = refs
        kk = pl.program_id(2)

        @pl.when(kk == 0)
        def _():
            acc[...] = jnp.zeros_like(acc)

        acc[...] += _bdot(a_ref[...], b_ref[...], dims)

        @pl.when(kk == nk - 1)
        def _():
            r = acc[...]
            if add_ref is not None:
                r = r + add_ref[...].astype(F32)
            o_ref[...] = r.astype(out_dtype)

    a_spec = (pl.BlockSpec((tk, tm), lambda i, j, kk: (kk, i)) if mode == "tn"
              else pl.BlockSpec((tm, tk), lambda i, j, kk: (i, kk)))
    b_spec = (pl.BlockSpec((tn, tk), lambda i, j, kk: (j, kk)) if mode == "nt"
              else pl.BlockSpec((tk, tn), lambda i, j, kk: (kk, j)))
    in_specs = [a_spec, b_spec]
    args = [a, b]
    if add is not None:
        in_specs.append(pl.BlockSpec((tm, tn), lambda i, j, kk: (i, j)))
        args.append(add)
    return pl.pallas_call(
        body, name=name, grid=(m // tm, n // tn, nk),
        in_specs=in_specs, out_specs=pl.BlockSpec((tm, tn), lambda i, j, kk: (i, j)),
        out_shape=jax.ShapeDtypeStruct((m, n), out_dtype),
        scratch_shapes=[pltpu.VMEM((tm, tn), F32)],
        compiler_params=_cp("parallel", "parallel", "arbitrary"),
    )(*args)


def _matmul_nt_segments(groups, out_dtype, tm, name, add=None, exch=None):
    m = groups[0][0][0].shape[0]
    n = groups[0][1].shape[0]
    assert m % tm == 0
    seg_list, b_list = [], []
    step = 0
    for gi, (segs, b, tk) in enumerate(groups):
        assert all(s.shape[1] % tk == 0 for s in segs) and sum(s.shape[1] for s in segs) == b.shape[1]
        g0 = step
        for s in segs:
            seg_list.append((s, tk, step, s.shape[1] // tk, gi))
            step += s.shape[1] // tk
        b_list.append((b, tk, g0, step - g0))
    nk = step
    ns, nb = len(seg_list), len(b_list)

    def body(*refs):
        seg_refs, b_refs = refs[:ns], refs[ns:ns + nb]
        add_ref = refs[ns + nb] if add is not None else None
        o_ref, acc = refs[-2], refs[-1]
        kk = pl.program_id(1)

        @pl.when(kk == 0)
        def _():
            acc[...] = jnp.zeros_like(acc)

        for a_ref, (_, _, k0, nk_s, gi) in zip(seg_refs, seg_list):
            @pl.when(jnp.logical_and(kk >= k0, kk < k0 + nk_s))
            def _(a_ref=a_ref, b_ref=b_refs[gi]):
                acc[...] += _bdot(a_ref[...], b_ref[...], NT)

        @pl.when(kk == nk - 1)
        def _():
            r = acc[...]
            if add_ref is not None:
                r = r + add_ref[...].astype(F32)
            o_ref[...] = r.astype(out_dtype)

    def walk(rows, tk, k0, nk_s, row_axis):
        if row_axis:
            return pl.BlockSpec((rows, tk), lambda i, kk: (i, jnp.clip(kk - k0, 0, nk_s - 1)))
        return pl.BlockSpec((rows, tk), lambda i, kk: (0, jnp.clip(kk - k0, 0, nk_s - 1)))

    row = pl.BlockSpec((tm, n), lambda i, kk: (i, 0))
    in_specs = [walk(tm, tk, k0, nk_s, True) for _, tk, k0, nk_s, _ in seg_list]
    in_specs += [walk(n, tk, k0, nk_s, False) for _, tk, k0, nk_s in b_list]
    args = [s[0] for s in seg_list] + [b[0] for b in b_list]
    if add is not None:
        in_specs.append(row)
        args.append(add)
    return _hosted_call(body, name=name, grid=(m // tm, nk), in_specs=in_specs, out_specs=[row],
                        out_shape=[jax.ShapeDtypeStruct((m, n), out_dtype)],
                        scratch_shapes=[pltpu.VMEM((tm, n), F32)], args=args,
                        dims=("parallel", "arbitrary"), exch=exch)


def _row_spec(tm, cols, cb=0):
    return pl.BlockSpec((tm, cols), lambda i, cb=cb: (i, cb))


def _bcast_spec(rows, cols):
    return pl.BlockSpec((rows, cols), lambda i: (0, 0))


def _rms_fwd(x, w, tm=1024):
    t, d = x.shape

    def body(x_ref, w_ref, o_ref):
        xv = x_ref[...]
        r = lax.rsqrt(jnp.mean(xv * xv, axis=-1, keepdims=True) + EPS)
        o_ref[...] = (xv * r * w_ref[...]).astype(BF16)

    return pl.pallas_call(
        body, name="rms_fwd", grid=(t // tm,),
        in_specs=[_row_spec(tm, d), _bcast_spec(1, d)], out_specs=_row_spec(tm, d),
        out_shape=jax.ShapeDtypeStruct((t, d), BF16), compiler_params=_cp("parallel"),
    )(x, w)


def _rms_bwd(x, w, dy, resid, tm=1024):
    t, d = x.shape

    def body(x_ref, w_ref, dy_ref, res_ref, dx_ref, dw_ref):
        xv = x_ref[...]
        r = lax.rsqrt(jnp.mean(xv * xv, axis=-1, keepdims=True) + EPS)
        xh = xv * r
        dyv = dy_ref[...].astype(F32)
        dxh = dyv * w_ref[...]
        dx_ref[...] = res_ref[...] + r * (dxh - xh * jnp.mean(dxh * xh, axis=-1, keepdims=True))

        @pl.when(pl.program_id(0) == 0)
        def _():
            dw_ref[...] = jnp.zeros_like(dw_ref)

        dw_ref[...] += jnp.sum(dyv * xh, axis=0, keepdims=True)

    return pl.pallas_call(
        body, name="rms_bwd", grid=(t // tm,),
        in_specs=[_row_spec(tm, d), _bcast_spec(1, d), _row_spec(tm, d), _row_spec(tm, d)],
        out_specs=[_row_spec(tm, d), _bcast_spec(1, d)],
        out_shape=[jax.ShapeDtypeStruct((t, d), F32), jax.ShapeDtypeStruct((1, d), F32)],
        compiler_params=_cp("arbitrary"),
    )(x, w, dy, resid)


def _final_loss(h, w, target, tm=1024):
    t, d = h.shape

    def body(h_ref, w_ref, t_ref, loss_ref, dh_ref, dw_ref):
        hv = h_ref[...]
        r = lax.rsqrt(jnp.mean(hv * hv, axis=-1, keepdims=True) + EPS)
        xh = hv * r
        wv = w_ref[...]
        err = xh * wv - t_ref[...]
        dy = err * (1.0 / d)
        dxh = dy * wv
        dh_ref[...] = r * (dxh - xh * jnp.mean(dxh * xh, axis=-1, keepdims=True))

        @pl.when(pl.program_id(0) == 0)
        def _():
            dw_ref[...] = jnp.zeros_like(dw_ref)
            loss_ref[...] = jnp.zeros_like(loss_ref)

        dw_ref[...] += jnp.sum(dy * xh, axis=0, keepdims=True)
        part = 0.5 * jnp.sum(jnp.mean(err * err, axis=-1, keepdims=True), axis=0, keepdims=True)
        loss_ref[...] += part + jnp.zeros((1, HEAD), F32)

    return pl.pallas_call(
        body, name="final_loss", grid=(t // tm,),
        in_specs=[_row_spec(tm, d), _bcast_spec(1, d), _row_spec(tm, d)],
        out_specs=[_bcast_spec(1, HEAD), _row_spec(tm, d), _bcast_spec(1, d)],
        out_shape=[jax.ShapeDtypeStruct((1, HEAD), F32), jax.ShapeDtypeStruct((t, d), F32),
                   jax.ShapeDtypeStruct((1, d), F32)],
        compiler_params=_cp("arbitrary"),
    )(h, w, target)


def _ffn_in(hn, w_gate, w_up, tm=1024, tn=1408):
    t, d = hn.shape
    ff = w_gate.shape[1]

    def body(a_ref, wg_ref, wu_ref, g_ref, u_ref, act_ref):
        a = a_ref[...]
        g = _bdot(a, wg_ref[...])
        u = _bdot(a, wu_ref[...])
        g_ref[...] = g.astype(BF16)
        u_ref[...] = u.astype(BF16)
        gq = g.astype(BF16).astype(F32)
        act_ref[...] = (gq * _sigmoid(gq) * u.astype(BF16).astype(F32)).astype(BF16)

    tile = pl.BlockSpec((tm, tn), lambda i, j: (i, j))
    wspec = pl.BlockSpec((d, tn), lambda i, j: (0, j))
    return pl.pallas_call(
        body, name="ffn_in", grid=(t // tm, ff // tn),
        in_specs=[pl.BlockSpec((tm, d), lambda i, j: (i, 0)), wspec, wspec], out_specs=[tile] * 3,
        out_shape=[jax.ShapeDtypeStruct((t, ff), BF16)] * 3, compiler_params=_cp("parallel", "parallel"),
    )(hn, w_gate, w_up)


def _ffn_act_bwd(dh, w_down, g, u, tm=1024, tn=1408):
    t, d = dh.shape
    ff = w_down.shape[0]

    def body(a_ref, w_ref, g_ref, u_ref, dg_ref, du_ref):
        dv = _bdot(a_ref[...], w_ref[...], NT).astype(BF16).astype(F32)
        gv = g_ref[...].astype(F32)
        sg = _sigmoid(gv)
        dg_ref[...] = (dv * u_ref[...].astype(F32) * (sg + gv * sg * (1.0 - sg))).astype(BF16)
        du_ref[...] = (dv * gv * sg).astype(BF16)

    tile = pl.BlockSpec((tm, tn), lambda i, j: (i, j))
    return pl.pallas_call(
        body, name="ffn_act_bwd", grid=(t // tm, ff // tn),
        in_specs=[pl.BlockSpec((tm, d), lambda i, j: (i, 0)), pl.BlockSpec((tn, d), lambda i, j: (j, 0)),
                  tile, tile],
        out_specs=[tile] * 2, out_shape=[jax.ShapeDtypeStruct((t, ff), BF16)] * 2,
        compiler_params=_cp("parallel", "parallel"),
    )(dh, w_down, g, u)


def _gate_merge_fwd(pb, ya, yb, tm=1024):
    t, d = ya.shape
    cb = PB_GATE // d

    def body(ga_ref, gb_ref, ya_ref, yb_ref, o_ref):
        ga, gb = ga_ref[...].astype(F32), gb_ref[...].astype(F32)
        o_ref[...] = (_sigmoid(ga) * ya_ref[...].astype(F32) + _sigmoid(gb) * yb_ref[...].astype(F32)).astype(BF16)

    return pl.pallas_call(
        body, name="gate_merge_fwd", grid=(t // tm,),
        in_specs=[_row_spec(tm, d, cb), _row_spec(tm, d, cb + 1), _row_spec(tm, d), _row_spec(tm, d)],
        out_specs=_row_spec(tm, d),
        out_shape=jax.ShapeDtypeStruct((t, d), BF16), compiler_params=_cp("parallel"),
    )(pb, pb, ya, yb)


def _gate_merge_bwd(pb, ya, yb, dm, tm=1024):
    t, d = ya.shape
    cb = PB_GATE // d

    def body(ga_ref, gb_ref, ya_ref, yb_ref, dm_ref, dya_ref, dyb_ref, dga_ref, dgb_ref):
        dmv = dm_ref[...].astype(F32)
        sa = _sigmoid(ga_ref[...].astype(F32))
        sb = _sigmoid(gb_ref[...].astype(F32))
        dya_ref[...] = (dmv * sa).astype(BF16)
        dyb_ref[...] = (dmv * sb).astype(BF16)
        dga_ref[...] = (dmv * ya_ref[...].astype(F32) * sa * (1.0 - sa)).astype(BF16)
        dgb_ref[...] = (dmv * yb_ref[...].astype(F32) * sb * (1.0 - sb)).astype(BF16)

    return pl.pallas_call(
        body, name="gate_merge_bwd", grid=(t // tm,),
        in_specs=[_row_spec(tm, d, cb), _row_spec(tm, d, cb + 1), _row_spec(tm, d), _row_spec(tm, d),
                  _row_spec(tm, d)],
        out_specs=[_row_spec(tm, d)] * 4,
        out_shape=[jax.ShapeDtypeStruct((t, d), BF16)] * 4, compiler_params=_cp("parallel"),
    )(pb, pb, ya, yb, dm)


def _head_norm_fwd(o, pb, wn, tm=1024):
    t, d = o.shape
    nh = d // HEAD

    def body(o_ref, z_ref, w_ref, out_ref):
        wv = w_ref[...]
        for h in range(nh):
            sl = slice(h * HEAD, (h + 1) * HEAD)
            ov = o_ref[:, sl]
            zv = z_ref[:, sl].astype(F32)
            r = lax.rsqrt(jnp.mean(ov * ov, axis=-1, keepdims=True) + EPS)
            out_ref[:, sl] = (ov * r * wv * (zv * _sigmoid(zv))).astype(BF16)

    return pl.pallas_call(
        body, name="head_norm_fwd", grid=(t // tm,),
        in_specs=[_row_spec(tm, d), _row_spec(tm, d, PB_Z // d), _bcast_spec(1, HEAD)],
        out_specs=_row_spec(tm, d),
        out_shape=jax.ShapeDtypeStruct((t, d), BF16), compiler_params=_cp("parallel"),
    )(o, pb, wn)


def _head_norm_bwd(o, pb, wn, dout, tm=1024):
    t, d = o.shape
    nh = d // HEAD

    def body(o_ref, z_ref, w_ref, d_ref, do_ref, dz_ref, dw_ref):
        wv = w_ref[...]
        dw_acc = jnp.zeros((1, HEAD), F32)
        for h in range(nh):
            sl = slice(h * HEAD, (h + 1) * HEAD)
            ov = o_ref[:, sl]
            zv = z_ref[:, sl].astype(F32)
            dv = d_ref[:, sl].astype(F32)
            r = lax.rsqrt(jnp.mean(ov * ov, axis=-1, keepdims=True) + EPS)
            xh = ov * r
            sz = _sigmoid(zv)
            dn = dv * (zv * sz)
            dz_ref[:, sl] = (dv * xh * wv * (sz + zv * sz * (1.0 - sz))).astype(BF16)
            dxh = dn * wv
            do_ref[:, sl] = r * (dxh - xh * jnp.mean(dxh * xh, axis=-1, keepdims=True))
            dw_acc = dw_acc + jnp.sum(dn * xh, axis=0, keepdims=True)

        @pl.when(pl.program_id(0) == 0)
        def _():
            dw_ref[...] = jnp.zeros_like(dw_ref)

        dw_ref[...] += dw_acc

    return pl.pallas_call(
        body, name="head_norm_bwd", grid=(t // tm,),
        in_specs=[_row_spec(tm, d), _row_spec(tm, d, PB_Z // d), _bcast_spec(1, HEAD), _row_spec(tm, d)],
        out_specs=[_row_spec(tm, d), _row_spec(tm, d), _bcast_spec(1, HEAD)],
        out_shape=[jax.ShapeDtypeStruct((t, d), F32), jax.ShapeDtypeStruct((t, d), BF16),
                   jax.ShapeDtypeStruct((1, HEAD), F32)],
        compiler_params=_cp("arbitrary"),
    )(o, pb, wn, dout)


def _attn_bias(gi, hh, dil):
    i = lax.broadcasted_iota(jnp.int32, (BLOCK_A, BLOCK_A), 0)
    j = lax.broadcasted_iota(jnp.int32, (BLOCK_A, BLOCK_A), 1)
    hf = (gi * HEADS_PER_GROUP + hh + 1).astype(F32)
    slope = jnp.exp(jnp.full((1, BLOCK_A), -8.0 * math.log(2.0) / N_HEADS_A, F32) * hf) * float(dil)
    d_prev = (BLOCK_A + i - j).astype(F32)
    d_cur = (i - j).astype(F32)
    return -slope * d_prev, -slope * d_cur, j >= i, j <= i


ATTN_TOKENS = 2048


def _sub_rows(a, r, dil):
    start = a * BLOCK_A * dil + r
    return pl.ds(start, BLOCK_A) if dil == 1 else pl.ds(start, BLOCK_A, stride=dil)


def _attn_fwd_group(pa, gi, dil, tb=ATTN_TOKENS):
    t = pa.shape[0]
    tb = min(tb, t)
    hb = BLOCK_A * dil
    nb = tb // hb
    scale = HEAD ** -0.5

    def body(q_ref, k_ref, v_ref, kp_ref, vp_ref, o_ref, l_ref):
        hh = pl.program_id(0)
        step = pl.program_id(1)
        b_prev, b_cur, m_prev, m_cur = _attn_bias(gi, hh, dil)
        m_first = jnp.logical_and(m_prev, step > 0)
        for r in range(dil):
            kp, vp = kp_ref[_sub_rows(0, r, dil), :], vp_ref[_sub_rows(0, r, dil), :]
            for a in range(nb):
                rows = _sub_rows(a, r, dil)
                q, kc, vc = q_ref[rows, :], k_ref[rows, :], v_ref[rows, :]
                s_p = jnp.where(m_first if a == 0 else m_prev, _bdot(q, kp, NT) * scale + b_prev, -1e30)
                s_c = jnp.where(m_cur, _bdot(q, kc, NT) * scale + b_cur, -1e30)
                m = jnp.maximum(jnp.max(s_p, axis=-1, keepdims=True), jnp.max(s_c, axis=-1, keepdims=True))
                p_p = jnp.exp(s_p - m)
                p_c = jnp.exp(s_c - m)
                den = _rowsum(p_p) + _rowsum(p_c)
                o_ref[rows, :] = (_bdot(p_p, vp) + _bdot(p_c, vc)) / den
                l_ref[rows, :] = (m + jnp.log(den)) + jnp.zeros((BLOCK_A, HEAD), F32)
                kp, vp = kc, vc

    def col(base):
        return lambda hh, s: (s, base + hh)

    def col_prev(base):
        return lambda hh, s: (jnp.maximum(s * nb - 1, 0), base + hh)

    qb, kb, vb = gi * HEADS_PER_GROUP, N_HEADS_A + gi * HEADS_PER_GROUP, 2 * N_HEADS_A + gi * HEADS_PER_GROUP
    ospec = pl.BlockSpec((tb, HEAD), lambda hh, s: (s, hh))
    return pl.pallas_call(
        body, name=f"attn_fwd_g{gi}", grid=(HEADS_PER_GROUP, t // tb),
        in_specs=[pl.BlockSpec((tb, HEAD), col(qb)), pl.BlockSpec((tb, HEAD), col(kb)),
                  pl.BlockSpec((tb, HEAD), col(vb)),
                  pl.BlockSpec((hb, HEAD), col_prev(kb)), pl.BlockSpec((hb, HEAD), col_prev(vb))],
        out_specs=[ospec, ospec],
        out_shape=[jax.ShapeDtypeStruct((t, D_ATTN_OUT), F32)] * 2,
        compiler_params=_cp("parallel", "parallel"),
    )(pa, pa, pa, pa, pa)


def _attn_merge(os, ls, tm=1024):
    t, d = os[0].shape

    def body(o0, o1, o2, l0, l1, l2, y_ref, lse_ref):
        a0, a1, a2 = l0[...], l1[...], l2[...]
        m = jnp.maximum(jnp.maximum(a0, a1), a2)
        e0, e1, e2 = jnp.exp(a0 - m), jnp.exp(a1 - m), jnp.exp(a2 - m)
        den = e0 + e1 + e2
        y_ref[...] = (e0 * o0[...] + e1 * o1[...] + e2 * o2[...]) / den
        lse_ref[...] = m + jnp.log(den)

    return pl.pallas_call(
        body, name="attn_merge", grid=(t // tm,),
        in_specs=[_row_spec(tm, d)] * 6, out_specs=[_row_spec(tm, d)] * 2,
        out_shape=[jax.ShapeDtypeStruct((t, d), F32)] * 2,
        compiler_params=_cp("parallel"),
    )(*os, *ls)


def _attn_bwd_group(pa, dy, y, lse, gi, dil, tb=ATTN_TOKENS):
    t = pa.shape[0]
    tb = min(tb, t)
    hb = BLOCK_A * dil
    nb = tb // hb
    nsteps = t // tb
    scale = HEAD ** -0.5

    def body(q_ref, k_ref, v_ref, dy_ref, y_ref, l_ref, kp_ref, vp_ref, d_ref,
             dq_s, dk_s, dv_s, carry_k, carry_v):
        hh = pl.program_id(0)
        step = pl.program_id(1)

        @pl.when(step == 0)
        def _():
            carry_k[...] = jnp.zeros_like(carry_k)
            carry_v[...] = jnp.zeros_like(carry_v)

        b_prev, b_cur, m_prev, m_cur = _attn_bias(gi, hh, dil)
        m_first = jnp.logical_and(m_prev, step < nsteps - 1)
        for r in range(dil):
            halo = _sub_rows(0, r, dil)
            dk_in, dv_in = carry_k[halo, :], carry_v[halo, :]
            kp, vp = kp_ref[halo, :], vp_ref[halo, :]
            prev_rows = None
            dk_pend = dv_pend = None
            for a in range(nb):
                rows = _sub_rows(a, r, dil)
                q, kc, vc = q_ref[rows, :], k_ref[rows, :], v_ref[rows, :]
                dyb, lb = dy_ref[rows, :], l_ref[rows, :]
                delta = _rowsum(dyb * y_ref[rows, :])
                mp = m_first if a == 0 else m_prev
                s = _bdot(q, kp, NT) * scale + b_prev
                p = jnp.where(mp, jnp.exp(jnp.where(mp, s - lb, 0.0)), 0.0)
                ds = p * (_bdot(dyb, vp, NT) - delta)
                dq = _bdot(ds, kp)
                dk_prev, dv_prev = _bdot(ds, q, TN), _bdot(p, dyb, TN)
                if a == 0:
                    carry_k[halo, :] = dk_prev
                    carry_v[halo, :] = dv_prev
                else:
                    dk_s[prev_rows, :] = dk_pend + dk_prev
                    dv_s[prev_rows, :] = dv_pend + dv_prev
                s = _bdot(q, kc, NT) * scale + b_cur
                p = jnp.where(m_cur, jnp.exp(jnp.where(m_cur, s - lb, 0.0)), 0.0)
                ds = p * (_bdot(dyb, vc, NT) - delta)
                dq_s[rows, :] = dq + _bdot(ds, kc)
                dk_pend, dv_pend = _bdot(ds, q, TN), _bdot(p, dyb, TN)
                prev_rows, kp, vp = rows, kc, vc
            dk_s[prev_rows, :] = dk_pend + dk_in
            dv_s[prev_rows, :] = dv_pend + dv_in
        d_ref[:, :HEAD] = (dq_s[...] * scale).astype(BF16)
        d_ref[:, HEAD:2 * HEAD] = (dk_s[...] * scale).astype(BF16)
        d_ref[:, 2 * HEAD:] = dv_s[...].astype(BF16)

    def col(base):
        return lambda hh, s: (nsteps - 1 - s, base + hh)

    def col_prev(base):
        return lambda hh, s: (jnp.maximum((nsteps - 1 - s) * nb - 1, 0), base + hh)

    qb, kb, vb = gi * HEADS_PER_GROUP, N_HEADS_A + gi * HEADS_PER_GROUP, 2 * N_HEADS_A + gi * HEADS_PER_GROUP
    big, small = (tb, HEAD), (hb, HEAD)
    return pl.pallas_call(
        body, name=f"attn_bwd_g{gi}", grid=(HEADS_PER_GROUP, nsteps),
        in_specs=[pl.BlockSpec(big, col(qb)), pl.BlockSpec(big, col(kb)), pl.BlockSpec(big, col(vb)),
                  pl.BlockSpec(big, col(0)), pl.BlockSpec(big, col(0)), pl.BlockSpec(big, col(0)),
                  pl.BlockSpec(small, col_prev(kb)), pl.BlockSpec(small, col_prev(vb))],
        out_specs=pl.BlockSpec((tb, 3 * HEAD), col(0)),
        out_shape=jax.ShapeDtypeStruct((t, 3 * D_ATTN_OUT), BF16),
        scratch_shapes=[pltpu.VMEM(big, F32)] * 3 + [pltpu.VMEM(small, F32)] * 2,
        compiler_params=_cp("parallel", "arbitrary"),
    )(pa, pa, pa, dy, y, lse, pa, pa)


def _shift_down(cur, prev8, s):
    if s == 0:
        return cur
    rolled = pltpu.roll(cur, s, 0)
    prolled = pltpu.roll(prev8, s, 0)
    rid = lax.broadcasted_iota(jnp.int32, prev8.shape, 0)
    top = jnp.where(rid < s, prolled, rolled[:8])
    return jnp.concatenate([top, rolled[8:]], axis=0)


def _shift_up(cur, next8, s):
    if s == 0:
        return cur
    n = cur.shape[0]
    rolled = pltpu.roll(cur, n - s, 0)
    nrolled = pltpu.roll(next8, 8 - s, 0)
    rid = lax.broadcasted_iota(jnp.int32, next8.shape, 0)
    bottom = jnp.where(rid >= 8 - s, nrolled, rolled[n - 8:])
    return jnp.concatenate([rolled[:n - 8], bottom], axis=0)


def _conv(xv, prev8, wv):
    c = jnp.zeros_like(xv)
    shifted = []
    for s in range(CONV_WIDTH):
        xs = _shift_down(xv, prev8, s)
        shifted.append(xs)
        c = c + wv[CONV_WIDTH - 1 - s:CONV_WIDTH - s, :] * xs
    return c, shifted


def _head_expand(psv, first):
    tm = psv.shape[0]
    return jnp.concatenate([jnp.broadcast_to(psv[:, first + h:first + h + 1], (tm, HEAD))
                            for h in range(N_HEADS_B)], axis=1)


def _head_collect(x, first):
    lane = lax.broadcasted_iota(jnp.int32, (x.shape[0], HEAD), 1)
    out = jnp.zeros((x.shape[0], HEAD), F32)
    for h in range(N_HEADS_B):
        out = jnp.where(lane == first + h, x[:, h * HEAD:(h + 1) * HEAD], out)
    return out


def _dn_prep_fwd(pb, ps, conv_w, a_log_bc, dt_bias_bc, tm=256, exch=None):
    t = pb.shape[0]
    c3 = 3 * D_B
    r16 = tm // 16

    def body(x_ref, xp_ref, ps_ref, w_ref, al_ref, dt_ref, q_ref, k_ref, v_ref, g_ref, beta_ref):
        first = pl.program_id(0) > 0

        def silu_conv(cols):
            c, _ = _conv(x_ref[:, cols].astype(F32), jnp.where(first, xp_ref[8:, cols].astype(F32), 0.0),
                         w_ref[:, cols])
            return c * _sigmoid(c)

        for h in range(N_HEADS_B):
            sl = slice(h * HEAD, (h + 1) * HEAD)
            sq = silu_conv(sl)
            q_ref[:, sl] = sq * lax.rsqrt(_rowsum(sq * sq) + EPS) * (HEAD ** -0.5)
            sk = silu_conv(slice(D_B + h * HEAD, D_B + (h + 1) * HEAD))
            k_ref[:, sl] = sk * lax.rsqrt(_rowsum(sk * sk) + EPS)
            v_ref[:, sl] = silu_conv(slice(2 * D_B + h * HEAD, 2 * D_B + (h + 1) * HEAD))
        psv = ps_ref[...]
        beta_ref[...] = _sigmoid(_head_expand(psv, 0))
        g_ref[...] = -jnp.exp(al_ref[...]) * _softplus(_head_expand(psv, N_HEADS_B) + dt_ref[...])

    return _hosted_call(
        body, name="dn_prep_fwd", grid=(t // tm,),
        in_specs=[_row_spec(tm, c3, 0),
                  pl.BlockSpec((16, c3), lambda i: (jnp.maximum(i * r16 - 1, 0), 0)),
                  _row_spec(tm, HEAD),
                  _bcast_spec(CONV_WIDTH, c3), _bcast_spec(1, D_B), _bcast_spec(1, D_B)],
        out_specs=[_row_spec(tm, D_B)] * 5,
        out_shape=[jax.ShapeDtypeStruct((t, D_B), F32)] * 5, scratch_shapes=[],
        args=(pb, pb, ps, conv_w, a_log_bc, dt_bias_bc), dims=("parallel",), exch=exch)


def _dn_prep_bwd(pb, ps, conv_w, a_log_bc, dt_bias_bc, g, dq, dk, dv, dg, dbeta, tm=128, exch=None):
    t = pb.shape[0]
    c3 = 3 * D_B
    r16 = tm // 16

    def body(x_ref, xp_ref, ps_ref, w_ref, al_ref, dt_ref, g_ref, dq_ref, dk_ref, dv_ref, dg_ref, db_ref,
             dc_ref, dps_ref, dw_ref, dal_ref, ddt_ref):
        first = pl.program_id(0) > 0

        @pl.when(pl.program_id(0) == 0)
        def _():
            dw_ref[...] = jnp.zeros_like(dw_ref)
            dal_ref[...] = jnp.zeros_like(dal_ref)
            ddt_ref[...] = jnp.zeros_like(ddt_ref)

        def column_block(cols, d_ref, sl, mult, normed):
            c, shifted = _conv(x_ref[:, cols].astype(F32), jnp.where(first, xp_ref[8:, cols].astype(F32), 0.0),
                               w_ref[:, cols])
            sg = _sigmoid(c)
            dsilu = sg + c * sg * (1.0 - sg)
            dyv = d_ref[:, sl]
            if normed:
                sv = c * sg
                r = lax.rsqrt(_rowsum(sv * sv) + EPS)
                yh = sv * r
                dyv = dyv * mult
                dyv = r * (dyv - yh * _rowsum(dyv * yh))
            dcv = dyv * dsilu
            dc_ref[:, cols] = dcv
            prods = jnp.concatenate([dcv * shifted[sft] for sft in range(CONV_WIDTH)], axis=1)
            sums = _hdot(jnp.ones((8, tm), F32), prods)
            for sft in range(CONV_WIDTH):
                j = CONV_WIDTH - 1 - sft
                dw_ref[j:j + 1, cols] += sums[0:1, sft * HEAD:(sft + 1) * HEAD]

        for h in range(N_HEADS_B):
            sl = slice(h * HEAD, (h + 1) * HEAD)
            column_block(sl, dq_ref, sl, HEAD ** -0.5, True)
            column_block(slice(D_B + h * HEAD, D_B + (h + 1) * HEAD), dk_ref, sl, 1.0, True)
            column_block(slice(2 * D_B + h * HEAD, 2 * D_B + (h + 1) * HEAD), dv_ref, sl, 1.0, False)
        psv = ps_ref[...]
        beta = _sigmoid(_head_expand(psv, 0))
        dgv = dg_ref[...]
        da = dgv * (-jnp.exp(al_ref[...])) * _sigmoid(_head_expand(psv, N_HEADS_B) + dt_ref[...])
        dps_ref[...] = _head_collect(db_ref[...] * beta * (1.0 - beta), 0) + _head_collect(da, N_HEADS_B)
        dal_ref[...] += jnp.sum(dgv * g_ref[...], axis=0, keepdims=True)
        ddt_ref[...] += jnp.sum(da, axis=0, keepdims=True)

    row = _row_spec(tm, D_B)
    return _hosted_call(
        body, name="dn_prep_bwd", grid=(t // tm,),
        in_specs=[_row_spec(tm, c3, 0),
                  pl.BlockSpec((16, c3), lambda i: (jnp.maximum(i * r16 - 1, 0), 0)),
                  _row_spec(tm, HEAD),
                  _bcast_spec(CONV_WIDTH, c3), _bcast_spec(1, D_B), _bcast_spec(1, D_B),
                  row, row, row, row, row, row],
        out_specs=[_row_spec(tm, c3), _row_spec(tm, HEAD), _bcast_spec(CONV_WIDTH, c3), _bcast_spec(1, D_B),
                   _bcast_spec(1, D_B)],
        out_shape=[jax.ShapeDtypeStruct((t, c3), F32), jax.ShapeDtypeStruct((t, HEAD), F32),
                   jax.ShapeDtypeStruct((CONV_WIDTH, c3), F32),
                   jax.ShapeDtypeStruct((1, D_B), F32), jax.ShapeDtypeStruct((1, D_B), F32)],
        scratch_shapes=[], args=(pb, pb, ps, conv_w, a_log_bc, dt_bias_bc, g, dq, dk, dv, dg, dbeta),
        dims=("arbitrary",), exch=exch)


def _conv_bwd_input(dc, conv_w, tm=256):
    t, c3 = dc.shape
    r8 = tm // 8
    nlast = t // 8 - 1
    nsteps = t // tm

    def body(d_ref, dn_ref, w_ref, o_ref):
        not_last = pl.program_id(0) < nsteps - 1
        for cb in range(c3 // HEAD):
            cols = slice(cb * HEAD, (cb + 1) * HEAD)
            next8 = jnp.where(not_last, dn_ref[:, cols], 0.0)
            dv = d_ref[:, cols]
            wv = w_ref[:, cols]
            acc = jnp.zeros_like(dv)
            for s in range(CONV_WIDTH):
                acc = acc + wv[CONV_WIDTH - 1 - s:CONV_WIDTH - s, :] * _shift_up(dv, next8, s)
            o_ref[:, cols] = acc.astype(BF16)

    return pl.pallas_call(
        body, name="conv_bwd_input", grid=(nsteps,),
        in_specs=[_row_spec(tm, c3), pl.BlockSpec((8, c3), lambda i: (jnp.minimum((i + 1) * r8, nlast), 0)),
                  _bcast_spec(CONV_WIDTH, c3)],
        out_specs=_row_spec(tm, c3),
        out_shape=jax.ShapeDtypeStruct((t, c3), BF16), compiler_params=_cp("parallel"),
    )(dc, dc, conv_w)


def _lanes(x):
    return x[:, :CH]


def _tri_inv(a_list, r, c):
    eye = (r == c).astype(F32)
    b16 = (r >> 4) == (c >> 4)
    b32 = (r >> 5) == (c >> 5)
    ns = [jnp.where(b16, -a, 0.0) for a in a_list]
    xs = [eye + n for n in ns]
    ps = [_bdot(n, n) for n in ns]
    for last in (False, False, True):
        xs = [x + _bdot(x, p) for x, p in zip(xs, ps)]
        if not last:
            ps = [_bdot(p, p) for p in ps]
    for mask in (jnp.logical_and(b32, jnp.logical_not(b16)), jnp.logical_not(b32)):
        ts = [_bdot(x, jnp.where(mask, a, 0.0)) for x, a in zip(xs, a_list)]
        xs = [x - _bdot(t, x) for x, t in zip(xs, ts)]
    return xs


def _chunk_local(qs, ks, vs, gs, betas, solved=None):
    r = lax.broadcasted_iota(jnp.int32, (CH, CH), 0)
    c = lax.broadcasted_iota(jnp.int32, (CH, CH), 1)
    incl, strict = r >= c, r > c
    lm = incl.astype(F32)
    cums = [_hdot(lm, jnp.concatenate([g, jnp.where(strict, _lanes(g), 0.0)], axis=1)) for g in gs]
    gcbs = [cm[:, :HEAD] for cm in cums]
    decays = [jnp.where(incl, jnp.exp(jnp.where(incl, cm[:, HEAD:], 0.0)), 0.0) for cm in cums]
    bcols = [_lanes(b) for b in betas]
    kks = [_bdot(k, k, NT) for k in ks]
    qkraws = [_bdot(q, k, NT) for q, k in zip(qs, ks)]
    egs = [jnp.exp(gcb) for gcb in gcbs]
    if solved is None:
        tms = _tri_inv([jnp.where(strict, bc * kk * dc, 0.0) for bc, kk, dc in zip(bcols, kks, decays)], r, c)
        sols = [_hdot(tm, jnp.concatenate([b * v, b * eg * k], axis=1))
                for tm, b, v, eg, k in zip(tms, betas, vs, egs, ks)]
        ubars, ws = [sol[:, :HEAD] for sol in sols], [sol[:, HEAD:] for sol in sols]
    else:
        tms, ubars, ws = solved
    gls = [gcb[CH - 1:CH, :] for gcb in gcbs]
    eks = [jnp.exp(gl - gcb) for gl, gcb in zip(gls, gcbs)]
    return [dict(incl=incl, strict=strict, r=r, c=c, decay=dc, bcol=bc, kk=kk, tm=tm, eg=eg,
                 u_bar=ub, w=w, qkraw=qkraw, gl=gl, ek=ek)
            for dc, bc, kk, tm, eg, ub, w, qkraw, gl, ek
            in zip(decays, bcols, kks, tms, egs, ubars, ws, qkraws, gls, eks)]


def _dn_chunk_fwd(q, k, v, g, beta, cps=CHUNK_GROUP):
    t = q.shape[0]
    tm = cps * CH

    def body(q_ref, k_ref, v_ref, g_ref, b_ref, ub_ref, w_ref, qd_ref, kd_ref, qk_ref, ti_ref, gl_ref):
        for base in range(0, cps, CHUNK_GROUP):
            sls = [slice((base + j) * CH, (base + j + 1) * CH) for j in range(CHUNK_GROUP)]
            qs, ks = [q_ref[sl, :] for sl in sls], [k_ref[sl, :] for sl in sls]
            locs = _chunk_local(qs, ks, [v_ref[sl, :] for sl in sls], [g_ref[sl, :] for sl in sls],
                                [b_ref[sl, :] for sl in sls])
            for j, (sl, qv, kv, loc) in enumerate(zip(sls, qs, ks, locs)):
                ub_ref[sl, :] = loc["u_bar"]
                w_ref[sl, :] = loc["w"]
                qd_ref[sl, :] = qv * loc["eg"]
                kd_ref[sl, :] = kv * loc["ek"]
                qk_ref[sl, :] = loc["qkraw"] * loc["decay"]
                ti_ref[sl, :] = loc["tm"]
                gl_ref[base + j:base + j + 1, :] = jnp.exp(loc["gl"])

    hspec = pl.BlockSpec((tm, HEAD), lambda h, i: (i, h))
    sq_spec = pl.BlockSpec((None, tm, CH), lambda h, i: (h, i, 0))
    sq_shape = jax.ShapeDtypeStruct((N_HEADS_B, t, CH), F32)
    return pl.pallas_call(
        body, name="dn_chunk_fwd", grid=(N_HEADS_B, t // tm),
        in_specs=[hspec] * 5,
        out_specs=[hspec] * 4 + [sq_spec, sq_spec, pl.BlockSpec((cps, HEAD), lambda h, i: (i, h))],
        out_shape=[jax.ShapeDtypeStruct((t, D_B), F32)] * 4
        + [sq_shape, sq_shape, jax.ShapeDtypeStruct((t // CH, D_B), F32)],
        compiler_params=_cp("parallel", "parallel"),
    )(q, k, v, g, beta)


def _dn_scan_fwd(ub, w, qd, kd, qk, gl, cps=8, hg=2 * SCAN_HEADS):
    t = ub.shape[0]
    tm = cps * CH
    hs = list(range(hg))

    def body(ub_ref, w_ref, qd_ref, kd_ref, qk_ref, gl_ref, o_ref, st_ref, s_acc):
        @pl.when(pl.program_id(1) == 0)
        def _():
            s_acc[...] = jnp.zeros_like(s_acc)

        for ci in range(cps):
            sl = slice(ci * CH, (ci + 1) * CH)
            cols = [slice(h * HEAD, (h + 1) * HEAD) for h in hs]
            svs = [s_acc[h] for h in hs]
            for h in hs:
                st_ref[h, ci * HEAD:(ci + 1) * HEAD, :] = svs[h]
            us = [ub_ref[sl, cols[h]] - _bdot(w_ref[sl, cols[h]], svs[h]) for h in hs]
            for h in hs:
                s_acc[h] = gl_ref[ci:ci + 1, cols[h]] * svs[h] + _bdot(kd_ref[sl, cols[h]], us[h], TN)
            for h in hs:
                o_ref[sl, cols[h]] = _bdot(qd_ref[sl, cols[h]], svs[h]) + _bdot(qk_ref[h, sl, :], us[h])

    hspec = pl.BlockSpec((tm, hg * HEAD), lambda h, i: (i, h))
    return pl.pallas_call(
        body, name="dn_scan_fwd", grid=(N_HEADS_B // hg, t // tm),
        in_specs=[hspec] * 4 + [pl.BlockSpec((hg, tm, CH), lambda h, i: (h, i, 0)),
                                pl.BlockSpec((cps, hg * HEAD), lambda h, i: (i, h))],
        out_specs=[hspec, pl.BlockSpec((hg, cps * HEAD, HEAD), lambda h, i: (h, i, 0))],
        out_shape=[jax.ShapeDtypeStruct((t, D_B), F32),
                   jax.ShapeDtypeStruct((N_HEADS_B, (t // CH) * HEAD, HEAD), F32)],
        scratch_shapes=[pltpu.VMEM((hg, HEAD, HEAD), F32)],
        compiler_params=_cp("parallel", "arbitrary"),
    )(ub, w, qd, kd, qk, gl)


def _dn_scan_bwd(ub, w, qd, kd, qk, gl, st, do, cps=8):
    t = ub.shape[0]
    tm = cps * CH
    ns = t // tm

    hg = SCAN_HEADS
    hs = list(range(hg))

    def body(ub_ref, w_ref, qd_ref, kd_ref, qk_ref, gl_ref, st_ref, do_ref,
             dub_ref, dw_ref, dqd_ref, dkd_ref, dqk_ref, dgl_ref, ds_acc):
        @pl.when(pl.program_id(1) == 0)
        def _():
            ds_acc[...] = jnp.zeros_like(ds_acc)

        for ci in reversed(range(cps)):
            sl = slice(ci * CH, (ci + 1) * CH)
            cols = [slice(h * HEAD, (h + 1) * HEAD) for h in hs]
            svs = [st_ref[h, ci * HEAD:(ci + 1) * HEAD, :] for h in hs]
            wvs = [w_ref[sl, cols[h]] for h in hs]
            dovs = [do_ref[sl, cols[h]] for h in hs]
            dsvs = [ds_acc[h] for h in hs]
            us = [ub_ref[sl, cols[h]] - _bdot(wvs[h], svs[h]) for h in hs]
            dus = [_bdot(kd_ref[sl, cols[h]], dsvs[h]) + _bdot(qk_ref[h, sl, :], dovs[h], TN) for h in hs]
            for h in hs:
                ds_acc[h] = (gl_ref[ci:ci + 1, cols[h]] * dsvs[h] + _bdot(qd_ref[sl, cols[h]], dovs[h], TN)
                             - _bdot(wvs[h], dus[h], TN))
            for h in hs:
                dgl_ref[ci:ci + 1, cols[h]] = (jnp.sum(_rowsum(dsvs[h] * svs[h]), axis=0, keepdims=True)
                                              + jnp.zeros((1, HEAD), F32))
                dkd_ref[sl, cols[h]] = _bdot(us[h], dsvs[h], NT)
                dqd_ref[sl, cols[h]] = _bdot(dovs[h], svs[h], NT)
                dqk_ref[h, sl, :] = _bdot(dovs[h], us[h], NT)
                dub_ref[sl, cols[h]] = dus[h]
                dw_ref[sl, cols[h]] = -_bdot(dus[h], svs[h], NT)

    hspec = pl.BlockSpec((tm, hg * HEAD), lambda h, i: (ns - 1 - i, h))
    qkspec = pl.BlockSpec((hg, tm, CH), lambda h, i: (h, ns - 1 - i, 0))
    glspec = pl.BlockSpec((cps, hg * HEAD), lambda h, i: (ns - 1 - i, h))
    return pl.pallas_call(
        body, name="dn_scan_bwd", grid=(N_HEADS_B // hg, ns),
        in_specs=[hspec] * 4 + [qkspec, glspec,
                                pl.BlockSpec((hg, cps * HEAD, HEAD), lambda h, i: (h, ns - 1 - i, 0)), hspec],
        out_specs=[hspec] * 4 + [qkspec, glspec],
        out_shape=[jax.ShapeDtypeStruct((t, D_B), F32)] * 4
        + [jax.ShapeDtypeStruct((N_HEADS_B, t, CH), F32), jax.ShapeDtypeStruct((t // CH, D_B), F32)],
        scratch_shapes=[pltpu.VMEM((hg, HEAD, HEAD), F32)],
        compiler_params=_cp("parallel", "arbitrary"),
    )(ub, w, qd, kd, qk, gl, st, do)


def _dn_chunk_bwd(q, k, v, g, beta, ub, w, tinv, dub, dw, dqd, dkd, dqk, dgl, cps=CHUNK_GROUP):
    t = q.shape[0]
    tm = cps * CH

    def body(q_ref, k_ref, v_ref, g_ref, b_ref, ub_ref, w_ref, dub_ref, dw_ref, dqd_ref, dkd_ref, ti_ref, dqk_ref,
             dgl_ref, dq_ref, dk_ref, dv_ref, dg_ref, db_ref):
        ones = jnp.ones((CH, HEAD), F32)
        rid = lax.broadcasted_iota(jnp.int32, (CH, HEAD), 0)

        def rest(ci, sl, qv, kv, vv, beta_v, loc, dr, da):
            incl = loc["incl"]
            eg, ek, decay, bcol, kk = loc["eg"], loc["ek"], loc["decay"], loc["bcol"], loc["kk"]
            drv, drk = dr[:, :HEAD], dr[:, HEAD:]
            dv_ref[sl, :] = beta_v * drv
            beg = beta_v * eg
            t1 = drk * kv
            dbeta = _rowsum(drv * vv + t1 * eg) + _rowsum(da * kk * decay)
            dkk = da * bcol * decay
            dqk_m = jnp.where(incl, dqk_ref[sl, :], 0.0)
            ddecay = da * bcol * kk + dqk_m * loc["qkraw"]
            dqkraw = dqk_m * decay
            dqdv, dkdv = dqd_ref[sl, :], dkd_ref[sl, :]
            dq_ref[sl, :] = _bdot(dqkraw, kv) + dqdv * eg
            dk_ref[sl, :] = (beg * drk + _bdot(dqkraw, qv, TN) + _bdot(dkk, kv) + _bdot(dkk, kv, TN)
                             + dkdv * ek)
            e = ddecay * decay
            skd = _rowsum(dkdv * kv * ek)
            dgc = _rowsum(beg * t1) + _rowsum(e) + _rowsum(dqdv * qv * eg) - skd
            colsum = _hdot(e, ones, TN)
            last = jnp.sum(skd, axis=0, keepdims=True) + dgl_ref[ci:ci + 1, :] * jnp.exp(loc["gl"])
            db_ref[sl, :] = dbeta + jnp.zeros((CH, HEAD), F32)
            return (dgc - colsum) + jnp.where(rid == CH - 1, last, 0.0)

        for base in range(0, cps, CHUNK_GROUP):
            cis = list(range(base, base + CHUNK_GROUP))
            sls = [slice(ci * CH, (ci + 1) * CH) for ci in cis]
            qs, ks, vs = [q_ref[sl, :] for sl in sls], [k_ref[sl, :] for sl in sls], [v_ref[sl, :] for sl in sls]
            betas = [b_ref[sl, :] for sl in sls]
            locs = _chunk_local(qs, ks, vs, [g_ref[sl, :] for sl in sls], betas,
                                solved=([ti_ref[sl, :] for sl in sls], [ub_ref[sl, :] for sl in sls],
                                        [w_ref[sl, :] for sl in sls]))
            drs = [_hdot(loc["tm"], jnp.concatenate([dub_ref[sl, :], dw_ref[sl, :]], axis=1), TN)
                   for loc, sl in zip(locs, sls)]
            das = [jnp.where(loc["strict"],
                             -_hdot(dr, jnp.concatenate([loc["u_bar"], loc["w"]], axis=1), NT), 0.0)
                   for loc, dr in zip(locs, drs)]
            dgcs = [rest(*args) for args in zip(cis, sls, qs, ks, vs, betas, locs, drs, das)]
            um = (locs[0]["r"] <= locs[0]["c"]).astype(F32)
            for sl, dgc_bc in zip(sls, dgcs):
                dg_ref[sl, :] = _hdot(um, dgc_bc)

    hspec = pl.BlockSpec((tm, HEAD), lambda h, i: (i, h))
    sq_spec = pl.BlockSpec((None, tm, CH), lambda h, i: (h, i, 0))
    return pl.pallas_call(
        body, name="dn_chunk_bwd", grid=(N_HEADS_B, t // tm),
        in_specs=[hspec] * 11 + [sq_spec, sq_spec, pl.BlockSpec((cps, HEAD), lambda h, i: (i, h))],
        out_specs=[hspec] * 5,
        out_shape=[jax.ShapeDtypeStruct((t, D_B), F32)] * 5,
        compiler_params=_cp("parallel", "parallel"),
    )(q, k, v, g, beta, ub, w, dub, dw, dqd, dkd, tinv, dqk, dgl)


FLIPS = [(fx, fy, fc) for fx in (0, 1) for fy in (0, 1) for fc in (0, 1)][1:]


def _mesh_pos():
    return lax.axis_index("x"), lax.axis_index("y"), lax.axis_index("c")


def _peer(pos, flip):
    return tuple((1 - p) if f else p for p, f in zip(pos, flip))


def _dev_index(pos):
    return 4 * pos[0] + 2 * pos[1] + pos[2]


class _Exchange:
    def __init__(self, tensors, scatter):
        self.tensors, self.scatter, self.nt = list(tensors), list(scatter), len(tensors)
        hbm = pl.BlockSpec(memory_space=pltpu.HBM)
        self.in_specs = [hbm] * self.nt
        self.out_specs = [hbm] * self.nt
        self.out_shape = [jax.ShapeDtypeStruct(x.shape if sc else (N_DEV,) + x.shape, x.dtype)
                          for x, sc in zip(tensors, scatter)]
        self.scratch_shapes = [pltpu.SemaphoreType.DMA((self.nt * 7,)), pltpu.SemaphoreType.DMA((self.nt * 7,)),
                               pltpu.SemaphoreType.DMA((self.nt,))]

    def _copies(self, ins, outs, sems):
        send_sems, recv_sems, local_sems = sems
        pos = _mesh_pos()
        me = _dev_index(pos)

        def remote(ti, fi, landing):
            peer = _peer(pos, FLIPS[fi])
            src = ins[ti].at[_dev_index(peer)] if self.scatter[ti] else ins[ti]
            return pltpu.make_async_remote_copy(
                src_ref=src, dst_ref=outs[ti].at[landing(peer)],
                send_sem=send_sems.at[ti * 7 + fi], recv_sem=recv_sems.at[ti * 7 + fi],
                device_id=peer, device_id_type=pl.DeviceIdType.MESH)

        pairs = [(ti, fi) for ti in range(self.nt) for fi in range(7)]
        local = [pltpu.make_async_copy(ins[ti].at[me] if self.scatter[ti] else ins[ti], outs[ti].at[me],
                                       local_sems.at[ti]) for ti in range(self.nt)]
        sends = [remote(ti, fi, lambda peer: me) for ti, fi in pairs]
        recvs = [remote(ti, fi, _dev_index) for ti, fi in pairs]
        return local, sends, recvs

    def start(self, ins, outs, sems):
        local, sends, _ = self._copies(ins, outs, sems)
        for cp in local + sends:
            cp.start()

    def wait(self, ins, outs, sems):
        local, sends, recvs = self._copies(ins, outs, sems)
        for cp in recvs:
            cp.wait_recv()
        for cp in sends:
            cp.wait_send()
        for cp in local:
            cp.wait()


def _exchange(tensors, scatter, name):
    ex = _Exchange(tensors, scatter)

    def body(*refs):
        ins, outs, sems = refs[:ex.nt], refs[ex.nt:2 * ex.nt], refs[2 * ex.nt:]
        ex.start(ins, outs, sems)
        ex.wait(ins, outs, sems)

    return pl.pallas_call(
        body, name=name, in_specs=ex.in_specs, out_specs=ex.out_specs, out_shape=ex.out_shape,
        scratch_shapes=ex.scratch_shapes, compiler_params=pltpu.CompilerParams(has_side_effects=True),
    )(*tensors)


def _gather_two_level(tensors, name):
    nt = len(tensors)
    hbm = pl.BlockSpec(memory_space=pltpu.HBM)

    def body(*refs):
        ins, outs = refs[:nt], refs[nt:2 * nt]
        send_sems, recv_sems, local_sems = refs[2 * nt:]
        x, y, c = _mesh_pos()
        sibling = (x, y, 1 - c)
        chips = [(1 - x, y), (x, 1 - y), (1 - x, 1 - y)]

        def copy(ti, k, block, to, own=False):
            slot = outs[ti].at[_dev_index(block)]
            return pltpu.make_async_remote_copy(
                src_ref=ins[ti] if own else slot, dst_ref=slot,
                send_sem=send_sems.at[ti * 7 + k], recv_sem=recv_sems.at[ti * 7 + k],
                device_id=to, device_id_type=pl.DeviceIdType.MESH)

        me = (x, y, c)
        mine = [pltpu.make_async_copy(ins[ti], outs[ti].at[_dev_index(me)], local_sems.at[ti]) for ti in range(nt)]
        first = [copy(ti, 0, me, sibling, own=True) for ti in range(nt)]
        first += [copy(ti, 1 + j, me, (*chip, c), own=True) for ti in range(nt) for j, chip in enumerate(chips)]
        for cp in mine + first:
            cp.start()
        passed = []
        for j, chip in enumerate(chips):
            for ti in range(nt):
                copy(ti, 1 + j, (*chip, c), me).wait_recv()
                cp = copy(ti, 4 + j, (*chip, c), sibling)
                cp.start()
                passed.append(cp)
        for ti in range(nt):
            copy(ti, 0, sibling, me).wait_recv()
            for j, chip in enumerate(chips):
                copy(ti, 4 + j, (*chip, 1 - c), me).wait_recv()
        for cp in first + passed:
            cp.wait_send()
        for cp in mine:
            cp.wait()

    return pl.pallas_call(
        body, name=name, in_specs=[hbm] * nt, out_specs=[hbm] * nt,
        out_shape=[jax.ShapeDtypeStruct((N_DEV,) + x.shape, x.dtype) for x in tensors],
        scratch_shapes=[pltpu.SemaphoreType.DMA((nt * 7,)), pltpu.SemaphoreType.DMA((nt * 7,)),
                        pltpu.SemaphoreType.DMA((nt,))],
        compiler_params=pltpu.CompilerParams(has_side_effects=True),
    )(*tensors)


def _hosted_call(body, *, name, grid, in_specs, out_specs, out_shape, scratch_shapes, args, dims, exch=None):
    if exch is None:
        return pl.pallas_call(body, name=name, grid=grid, in_specs=in_specs, out_specs=out_specs,
                              out_shape=out_shape, scratch_shapes=scratch_shapes,
                              compiler_params=_cp(*dims))(*args)
    n_in, n_out, n_sc, ne = len(in_specs), len(out_specs), len(scratch_shapes), exch.nt
    nsteps = math.prod(grid)

    def wrapped(*refs):
        ins, ex_in = refs[:n_in], refs[n_in:n_in + ne]
        outs = refs[n_in + ne:n_in + ne + n_out]
        ex_out = refs[n_in + ne + n_out:n_in + 2 * ne + n_out]
        rest = refs[n_in + 2 * ne + n_out:]
        scratch, sems = rest[:n_sc], rest[n_sc:]
        step = pl.program_id(0)
        for ax in range(1, len(grid)):
            step = step * grid[ax] + pl.program_id(ax)

        @pl.when(step == 0)
        def _():
            exch.start(ex_in, ex_out, sems)

        body(*ins, *outs, *scratch)

        @pl.when(step == nsteps - 1)
        def _():
            exch.wait(ex_in, ex_out, sems)

    return pl.pallas_call(
        wrapped, name=name, grid=grid, in_specs=list(in_specs) + exch.in_specs,
        out_specs=list(out_specs) + exch.out_specs, out_shape=list(out_shape) + exch.out_shape,
        scratch_shapes=list(scratch_shapes) + exch.scratch_shapes,
        compiler_params=pltpu.CompilerParams(dimension_semantics=("arbitrary",) * len(grid),
                                             vmem_limit_bytes=VMEM_LIMIT, has_side_effects=True),
    )(*args, *exch.tensors)


def _adamw(land, w, m, v, name, tm=256):
    n, r, c = land.shape
    tm = r if r <= tm else max(s for s in range(8, tm + 1, 8) if r % s == 0)
    bc1 = 1.0 / (1.0 - ADAM_B1 ** ADAM_STEP)
    bc2 = 1.0 / (1.0 - ADAM_B2 ** ADAM_STEP)

    def body(l_ref, w_ref, m_ref, v_ref, g_ref, d_ref, nm_ref, nv_ref):
        g = l_ref[0].astype(F32)
        for i in range(1, n):
            g = g + l_ref[i].astype(F32)
        nm = ADAM_B1 * m_ref[...] + (1.0 - ADAM_B1) * g
        nv = ADAM_B2 * v_ref[...] + (1.0 - ADAM_B2) * (g * g)
        g_ref[...] = g
        nm_ref[...] = nm
        nv_ref[...] = nv
        d_ref[...] = -ADAM_LR * ((nm * bc1) / (jnp.sqrt(nv * bc2) + ADAM_EPS) + ADAM_WD * w_ref[...])

    spec = pl.BlockSpec((tm, c), lambda i: (i, 0))
    return pl.pallas_call(
        body, name=name, grid=(r // tm,),
        in_specs=[pl.BlockSpec((n, tm, c), lambda i: (0, i, 0)), spec, spec, spec],
        out_specs=[spec] * 4, out_shape=[jax.ShapeDtypeStruct((r, c), F32)] * 4,
        compiler_params=_cp("parallel"),
    )(land, w, m, v)


PACK_W = 2048


def _pack_rows(parts):
    flat = jnp.concatenate([p.reshape(-1).astype(F32) for p in parts])
    pad = (-flat.shape[0]) % (8 * PACK_W)
    return jnp.pad(flat, (0, pad)).reshape(-1, PACK_W)


def _unpack_rows(packed, shapes):
    flat = packed.reshape(-1)
    out, off = [], 0
    for s in shapes:
        n = math.prod(s)
        out.append(flat[off:off + n].reshape(s))
        off += n
    return out


def _col_slabs(gfull, width):
    r = gfull.shape[0]
    return jnp.transpose(gfull.reshape(r, N_DEV, width), (1, 0, 2)).astype(BF16)


def _row_slabs(gfull):
    return gfull.reshape(N_DEV, gfull.shape[0] // N_DEV, gfull.shape[1]).astype(BF16)


def _from_col_slabs(gathered):
    n, r, width = gathered.shape
    return jnp.transpose(gathered, (1, 0, 2)).reshape(r, n * width)


def _local_step(xs, target, norm_mix, wf_in, cw, a_log, dt_bias, dn_norm, rest, norm_ffn, norm_final,
                distributed=True):
    d = D_MODEL
    n_main = D_PA + 4 * D_B
    w_pa_cols = wf_in[:, :D_PA]
    w_pb_cols = jnp.concatenate([wf_in[:, D_PA:n_main], wf_in[:, n_main + 2 * N_HEADS_B:]], axis=1)
    w_small = jnp.pad(wf_in[:, n_main:n_main + 2 * N_HEADS_B], ((0, 0), (0, HEAD - 2 * N_HEADS_B)))
    a_log_bc = jnp.repeat(a_log, HEAD, axis=1)
    dt_bias_bc = jnp.repeat(dt_bias, HEAD, axis=1)

    u = _rms_fwd(xs, norm_mix)
    pa = _matmul(u, w_pa_cols, "nn", F32, 1024, 1536, d, name="proj_a")
    pb = _matmul(u, w_pb_cols, "nn", BF16, 1024, 1024, d, name="proj_b")
    ps = _matmul(u, w_small, "nn", F32, 2048, HEAD, d, name="proj_small")
    os_, ls_ = [], []
    for gi, dil in enumerate(DILATIONS):
        o_g, l_g = _attn_fwd_group(pa, gi, dil)
        os_.append(o_g)
        ls_.append(l_g)
    y_att, lse = _attn_merge(os_, ls_)
    prep = _dn_prep_fwd(pb, ps, cw, a_log_bc, dt_bias_bc,
                        exch=_Exchange(rest, [False] * 6) if distributed else None)
    qn, kn, vn, gdec, beta = prep[:5]
    if distributed:
        g_pa, g_pd, g_out, g_gate, g_up, g_down = prep[5:]
        wf_pa, wf_pd, wf_out = _from_col_slabs(g_pa), g_pd.reshape(D_B, d), g_out.reshape(d, d)
        wf_gate, wf_up, wf_down = _from_col_slabs(g_gate), _from_col_slabs(g_up), g_down.reshape(D_FF, d)
    else:
        wf_pa, wf_pd, wf_out, wf_gate, wf_up, wf_down = rest
    wf_gu = jnp.concatenate([wf_gate, wf_up], axis=1)
    ub, ww, qd, kd, qk, tinv, gl = _dn_chunk_fwd(qn, kn, vn, gdec, beta)
    o_dn, states = _dn_scan_fwd(ub, ww, qd, kd, qk, gl)
    o_gated = _head_norm_fwd(o_dn, pb, dn_norm)
    y_a = _matmul(y_att, wf_pa, "nn", BF16, 1024, d, D_ATTN_OUT, name="proj_attn")
    y_b = _matmul(o_gated, wf_pd, "nn", BF16, 1024, d, d, name="proj_delta")
    merged = _gate_merge_fwd(pb, y_a, y_b)
    h1 = _matmul(merged, wf_out, "nn", F32, 1024, d, d, add=xs, name="out_proj")
    hn = _rms_fwd(h1, norm_ffn)
    gate, up, act = _ffn_in(hn, wf_gate, wf_up)
    h2 = _matmul(act, wf_down, "nn", F32, 512, d, D_FF, add=h1, name="ffn_out")
    loss_part, dh2, d_norm_final = _final_loss(h2, norm_final.reshape(1, d), target)

    dgate, dup = _ffn_act_bwd(dh2, wf_down, gate, up)
    gw_down = _matmul(act, dh2, "tn", BF16, 1408, d, TOKEN_TK, name="gw_down")
    dhn = _matmul_nt_segments([([dgate, dup], wf_gu, 1408)], BF16, 1024, "d_hn")[0]
    gw_gate = _matmul(hn, dgate, "tn", BF16, d, 1408, TOKEN_TK, name="gw_gate")
    gw_up = _matmul(hn, dup, "tn", BF16, d, 1408, TOKEN_TK, name="gw_up")
    dh1, d_norm_ffn = _rms_bwd(h1, norm_ffn, dhn, dh2)
    dmerged = _matmul(dh1, wf_out, "nt", BF16, 1024, d, d, name="d_merged")
    gw_out = _matmul(merged, dh1, "tn", BF16, d, d, TOKEN_TK, name="gw_out")
    dya, dyb, dga, dgb = _gate_merge_bwd(pb, y_a, y_b, dmerged)
    dy_att = _matmul(dya, wf_pa, "nt", F32, 1024, D_ATTN_OUT, d, name="d_y_att")
    gw_pa = _matmul(y_att, dya, "tn", BF16, D_ATTN_OUT, d, TOKEN_TK, name="gw_pa")
    do_gated = _matmul(dyb, wf_pd, "nt", BF16, 1024, d, d, name="d_o_gated")
    gw_pd = _matmul(o_gated, dyb, "tn", BF16, d, d, TOKEN_TK, name="gw_pd")
    do_dn, dz, d_dn_norm = _head_norm_bwd(o_dn, pb, dn_norm, do_gated)
    dub, dww, dqd, dkd, dqk, dgl = _dn_scan_bwd(ub, ww, qd, kd, qk, gl, states, do_dn)
    dqn, dkn, dvn, dgdec, dbeta = _dn_chunk_bwd(qn, kn, vn, gdec, beta, ub, ww, tinv, dub, dww, dqd, dkd, dqk, dgl)
    slabs = [_col_slabs(gw_pa, d // N_DEV), _row_slabs(gw_pd), _row_slabs(gw_out),
             _col_slabs(gw_gate, D_FF // N_DEV), _col_slabs(gw_up, D_FF // N_DEV),
             _row_slabs(gw_down)] if distributed else None
    prep = _dn_prep_bwd(pb, ps, cw, a_log_bc, dt_bias_bc, gdec, dqn, dkn, dvn, dgdec, dbeta,
                        exch=_Exchange(slabs, [True] * 6) if distributed else None)
    dc, dps, d_conv_full, d_alog_bc, d_dt_bc = prep[:5]
    dqkv_pre = _conv_bwd_input(dc, cw)
    segs = [_attn_bwd_group(pa, dy_att, y_att, lse, gi, dil) for gi, dil in enumerate(DILATIONS)]
    segs += [dqkv_pre, dz, dga, dgb]
    gws = [_matmul(u, s, "tn", BF16, d, 1536, TOKEN_TK, name=f"gw_in_{i}") for i, s in enumerate(segs)]
    gw_small = _matmul(u, dps, "tn", BF16, d, HEAD, TOKEN_TK, name="gw_in_small")
    g_att = jnp.concatenate(gws[:3], axis=1).reshape(d, N_HEADS_A, 3, HEAD)
    gw_in = jnp.concatenate(
        [g_att[:, :, i, :].reshape(d, D_ATTN) for i in range(3)]
        + [gws[3], gws[4], gw_small[:, :2 * N_HEADS_B], gws[5], gws[6]], axis=1)
    w_att = jnp.stack([w_pa_cols[:, i * D_ATTN:(i + 1) * D_ATTN].reshape(d, N_HEADS_A, HEAD) for i in range(3)],
                      axis=2).reshape(d, D_PA)
    du_small = _matmul(dps, w_small, "nt", F32, 1024, d, HEAD, name="d_u_small")
    du_call = _matmul_nt_segments(
        [(segs[:4], jnp.concatenate([w_att, w_pb_cols[:, :3 * D_B]], axis=1), 768),
         (segs[4:], w_pb_cols[:, 3 * D_B:], 1024)], BF16, 1024, "d_u", add=du_small,
        exch=_Exchange([_col_slabs(gw_in, SHARD_IN)], [True]) if distributed else None)
    dx, d_norm_mix = _rms_bwd(xs, norm_mix, du_call[0], dh1)
    d_a_log = d_alog_bc.reshape(1, N_HEADS_B, HEAD)[:, :, 0]
    d_dt_bias = d_dt_bc.reshape(1, N_HEADS_B, HEAD)[:, :, 0]
    small = (d_conv_full, d_norm_mix, d_norm_ffn, d_norm_final, d_dn_norm, d_a_log, d_dt_bias)
    if distributed:
        return (loss_part, dx, [du_call[1]] + list(prep[5:])) + small
    return (loss_part, dx, gw_in, gw_pa, gw_pd, gw_out, jnp.concatenate([gw_gate, gw_up], axis=1), gw_down) + small


def kernel(x, norm_mix, w_in, conv_w, a_log, dt_bias, dn_norm, w_proj_attn, w_proj_delta, w_out, norm_ffn, w_gate, w_up, w_down, norm_final, loss_target, m_norm_mix, m_w_in, m_conv_w, m_a_log, m_dt_bias, m_dn_norm, m_w_proj_attn, m_w_proj_delta, m_w_out, m_norm_ffn, m_w_gate, m_w_up, m_w_down, m_norm_final, v_norm_mix, v_w_in, v_conv_w, v_a_log, v_dt_bias, v_dn_norm, v_w_proj_attn, v_w_proj_delta, v_w_out, v_norm_ffn, v_w_gate, v_w_up, v_w_down, v_norm_final):
    d = D_MODEL
    xs = x[0]
    target = loss_target[0]
    me = _dev_index(_mesh_pos())

    g_in, g_conv = _gather_two_level([w_in[0].astype(BF16), conv_w[0]], "gather_w_in")
    rest = [w[0].astype(BF16) for w in (w_proj_attn, w_proj_delta, w_out, w_gate, w_up, w_down)]
    (loss_part, dx, landed, d_conv_full, d_norm_mix, d_norm_ffn, d_norm_final, d_dn_norm, d_a_log,
     d_dt_bias) = _local_step(xs, target, norm_mix, _from_col_slabs(g_in), _from_col_slabs(g_conv), a_log, dt_bias,
                              dn_norm, rest, norm_ffn, norm_final)

    small_shapes = [(1, d), (1, d), (d,), (1, HEAD), (1, N_HEADS_B), (1, N_HEADS_B), (1, 1), (CONV_WIDTH, 3 * D_B)]
    packed = _pack_rows([d_norm_mix, d_norm_ffn, d_norm_final, d_dn_norm, d_a_log, d_dt_bias,
                         loss_part[:, :1], d_conv_full])
    landed = list(landed) + list(_exchange([packed], [False], "gather_small_grads"))
    zero1 = jnp.zeros((1, 1), F32)
    zconv = jnp.zeros((CONV_WIDTH, 3 * D_B), F32)
    small_w = _pack_rows([norm_mix, norm_ffn, norm_final, dn_norm, a_log, dt_bias, zero1, zconv])
    small_m = _pack_rows([m_norm_mix, m_norm_ffn, m_norm_final, m_dn_norm, m_a_log, m_dt_bias, zero1, zconv])
    small_v = _pack_rows([v_norm_mix, v_norm_ffn, v_norm_final, v_dn_norm, v_a_log, v_dt_bias, zero1, zconv])
    small = [_unpack_rows(z, small_shapes) for z in _adamw(landed[7], small_w, small_m, small_v, "adamw_small")]
    loss = small[0][6].reshape(())
    conv_shard = 3 * D_B // N_DEV
    g_conv_own = lax.dynamic_slice_in_dim(small[0][7], me * conv_shard, conv_shard, axis=1)
    r_conv = _adamw(g_conv_own[None], conv_w[0], m_conv_w[0], v_conv_w[0], "adamw_conv")
    big = [_adamw(landed[i], w[0], m[0], v[0], f"adamw_{i}") for i, (w, m, v) in enumerate([
        (w_in, m_w_in, v_w_in), (w_proj_attn, m_w_proj_attn, v_w_proj_attn),
        (w_proj_delta, m_w_proj_delta, v_w_proj_delta), (w_out, m_w_out, v_w_out),
        (w_gate, m_w_gate, v_w_gate), (w_up, m_w_up, v_w_up), (w_down, m_w_down, v_w_down)])]

    def leaves(k):
        sm = small[k]
        return [sm[0], big[0][k][None], r_conv[k][None], sm[4], sm[5], sm[3], big[1][k][None], big[2][k][None],
                big[3][k][None], sm[1], big[4][k][None], big[5][k][None], big[6][k][None], sm[2]]

    return (loss, dx[None], *leaves(0), *leaves(1), *leaves(2), *leaves(3))
```
